```python
import math
import jax
import jax.numpy as jnp
from jax import lax
import numpy as np

D_MODEL = 2048
BATCH = 4
SEQ = 2048
DEPTH = 2
DEC_BATCH = 128
DEC_SEQ = 1
PAST_LEN = 16384
PAGE_SIZE = 128

RW_HD = 64
RW_W = D_MODEL // 4
RW_HEADS = RW_W // RW_HD
RW_LORA_W = 64
RW_LORA_A = 64
RW_LORA_V = 32
RW_LORA_G = 128
RW_P = 3 * RW_W + RW_LORA_W + RW_LORA_A + RW_LORA_G
RW_GN_EPS = 64e-5
RET_V = 128
RET_QK = 64
RET_W = 3 * D_MODEL // 8
RET_HEADS = RET_W // RET_V
RET_P = 2 * RET_HEADS * RET_QK + 2 * RET_W
ML_V = 128
ML_QK = 64
ML_W = D_MODEL - RW_W - RET_W
ML_HEADS = ML_W // ML_V
ML_P = 2 * ML_HEADS * ML_QK + 2 * ML_W + 2 * ML_HEADS
ML_GATE_CAP = 15.0
MIX_W = RW_W + RET_W + ML_W
P_TOTAL = RW_P + RET_P + ML_P
D_FF = ((8 * D_MODEL + 3 * 256 - 1) // (3 * 256)) * 256
CHUNK = 128
ROPE_BASE = 10000.0
LN_EPS = 1e-5
NORM_EPS = 1e-6
ALPHA = (2 * DEPTH) ** 0.25
BETA = (8 * DEPTH) ** -0.25

kernel_name = 'hymba_rwkv7_retnet_mlstm_step'


def _split(a, sizes):
    return jnp.split(a, [int(s) for s in np.cumsum(sizes)[:-1]], axis=-1)


def _heads(a, h):
    return a.reshape(a.shape[:-1] + (h, a.shape[-1] // h))


def _layernorm(x, g, b):
    xf = x.astype(jnp.float32)
    mu = jnp.mean(xf, -1, keepdims=True)
    var = jnp.mean(jnp.square(xf - mu), -1, keepdims=True)
    return ((xf - mu) * lax.rsqrt(var + LN_EPS) * g + b).astype(x.dtype)


def _rmsnorm_heads(y):
    return y * lax.rsqrt(jnp.mean(jnp.square(y), -1, keepdims=True) + NORM_EPS)


def _shift_prev(p, prev_row):
    return jnp.concatenate([prev_row[:, None, :].astype(p.dtype), p[:, :-1]], axis=1)


def _rotary(x, pos):
    half = x.shape[-1] // 2
    inv = ROPE_BASE ** (-jnp.arange(half, dtype=jnp.float32) / half)
    ang = pos.astype(jnp.float32)[:, None] * inv[None, :]
    cos = jnp.cos(ang)[None, :, None, :]
    sin = jnp.sin(ang)[None, :, None, :]
    x1, x2 = x[..., :half], x[..., half:]
    return jnp.concatenate([x1 * cos - x2 * sin, x1 * sin + x2 * cos], -1)


def _rwkv7_scan(r, lw, k, v, kk, a, s0):
    def step(s, inp):
        r_t, lw_t, k_t, v_t, kk_t, a_t = inp
        sa = jnp.einsum('bhk,bhkv->bhv', kk_t, s)
        s = (jnp.exp(lw_t)[..., None] * s
             - (kk_t * a_t)[..., None] * sa[..., None, :]
             + k_t[..., None] * v_t[..., None, :])
        return s, jnp.einsum('bhk,bhkv->bhv', r_t, s)
    xs = tuple(jnp.moveaxis(t, 1, 0) for t in (r, lw, k, v, kk, a))
    s_fin, ys = lax.scan(step, s0, xs)
    return jnp.moveaxis(ys, 0, 1), s_fin


def _retention_chunked(q, k, v, r0):
    bsz, t_len, h, dk = q.shape
    dv = v.shape[-1]
    L = math.gcd(t_len, CHUNK)
    nc = t_len // L
    lg = jnp.log1p(-jnp.exp2(-5.0 - jnp.arange(h, dtype=jnp.float32)))
    idx = jnp.arange(L, dtype=jnp.float32)
    rel = idx[:, None] - idx[None, :]
    dmask = jnp.where(rel[None] >= 0, jnp.exp(jnp.maximum(rel, 0.0)[None] * lg[:, None, None]), 0.0)
    qc = q.reshape(bsz, nc, L, h, dk)
    kc = k.reshape(bsz, nc, L, h, dk)
    vc = v.reshape(bsz, nc, L, h, dv)
    scores = jnp.einsum('bcihd,bcjhd->bchij', qc, kc) * dmask
    y = jnp.einsum('bchij,bcjhv->bcihv', scores, vc)
    k_end = kc * jnp.exp((L - 1.0 - idx)[:, None] * lg[None, :])[:, :, None]
    kv = jnp.einsum('bcjhd,bcjhv->bchdv', k_end, vc)
    chunk_decay = jnp.exp(L * lg)[None, :, None, None]

    def step(rs, kv_c):
        return chunk_decay * rs + kv_c, rs
    r_fin, r_before = lax.scan(step, r0, jnp.moveaxis(kv, 1, 0))
    r_before = jnp.moveaxis(r_before, 0, 1)
    q_dec = qc * jnp.exp((idx + 1.0)[:, None] * lg[None, :])[:, :, None]
    y = y + jnp.einsum('bcihd,bchdv->bcihv', q_dec, r_before)
    return y.reshape(bsz, t_len, h, dv), r_fin


def _mlstm_chunked(q, k, v, ig, lf, c0, n0, m0):
    bsz, t_len, h, dk = q.shape
    dv = v.shape[-1]
    L = math.gcd(t_len, CHUNK)
    nc = t_len // L
    qc = q.reshape(bsz, nc, L, h, dk)
    kc = k.reshape(bsz, nc, L, h, dk)
    vc = v.reshape(bsz, nc, L, h, dv)
    igc = ig.reshape(bsz, nc, L, h)
    b = jnp.cumsum(lf.reshape(bsz, nc, L, h), axis=2)
    causal = jnp.tril(jnp.ones((L, L), dtype=bool))
    dlog = b[:, :, :, None, :] - b[:, :, None, :, :] + igc[:, :, None, :, :]
    dlog = jnp.where(causal[None, None, :, :, None], dlog, -jnp.inf)
    a_end = b[:, :, -1:, :] - b + igc
    g_tot = b[:, :, -1, :]

    def step(carry, inp):
        c, n, m = carry
        a_c, g_c, k_c, v_c = inp
        m_new = jnp.maximum(g_c + m, jnp.max(a_c, axis=1))
        dec = jnp.exp(g_c + m - m_new)
        kw = k_c * jnp.exp(a_c - m_new[:, None, :])[..., None]
        c_new = dec[..., None, None] * c + jnp.einsum('blhd,blhv->bhdv', kw, v_c)
        n_new = dec[..., None] * n + jnp.sum(kw, axis=1)
        return (c_new, n_new, m_new), (c, n, m)
    to_t = lambda t: jnp.moveaxis(t, 1, 0)
    to_b = lambda t: jnp.moveaxis(t, 0, 1)
    (c_f, n_f, m_f), (c_b, n_b, m_b) = lax.scan(
        step, (c0, n0, m0), (to_t(a_end), to_t(g_tot), to_t(kc), to_t(vc)))
    c_b, n_b, m_b = to_b(c_b), to_b(n_b), to_b(m_b)
    inter = b + m_b[:, :, None, :]
    m_i = jnp.maximum(jnp.max(dlog, axis=3), inter)
    s = jnp.einsum('bcihd,bcjhd->bcijh', qc, kc) * jnp.exp(dlog - m_i[:, :, :, None, :])
    sc = jnp.exp(inter - m_i)
    num = (jnp.einsum('bcijh,bcjhv->bcihv', s, vc)
           + sc[..., None] * jnp.einsum('bcihd,bchdv->bcihv', qc, c_b))
    den = jnp.sum(s, axis=3) + sc * jnp.einsum('bcihd,bchd->bcih', qc, n_b)
    hid = num / jnp.maximum(jnp.abs(den), jnp.exp(-m_i))[..., None]
    return hid.reshape(bsz, t_len, h, dv), c_f, n_f, m_f


def _layer(x, pos, v_first, st, lp):
    f32 = jnp.float32
    bsz, t_len, _ = x.shape
    p = x @ lp['w_in']
    p_rw, p_ret, p_ml = _split(p, [RW_P, RET_P, ML_P])

    p_rw_prev = _shift_prev(p_rw, st['rw_shift'] @ lp['w_in'][:, :RW_P])
    mixed = (p_rw + (p_rw_prev - p_rw) * lp['rw_mu']).astype(f32)
    r, k, v, xw, xa, xg = _split(mixed, [RW_W, RW_W, RW_W, RW_LORA_W, RW_LORA_A, RW_LORA_G])
    w = -jax.nn.softplus(-(lp['rw_w0'] + jnp.tanh(xw) @ lp['rw_w2'])) - 0.5
    a = jax.nn.sigmoid(lp['rw_a0'] + xa @ lp['rw_a2'])
    g = jax.nn.sigmoid(xg) @ lp['rw_g2']
    if v_first is None:
        v_first = v
    else:
        pv = x @ lp['rw_v1']
        pv_prev = _shift_prev(pv, st['rw_shift'] @ lp['rw_v1'])
        xv = (pv + (pv_prev - pv) * lp['rw_vmu']).astype(f32)
        v = v + (v_first - v) * jax.nn.sigmoid(lp['rw_v0'] + xv @ lp['rw_v2'])
    kk = _heads(k * lp['rw_kk'], RW_HEADS)
    kk = kk * lax.rsqrt(jnp.maximum(jnp.sum(jnp.square(kk), -1, keepdims=True), 1e-24))
    k = k * (1.0 + (a - 1.0) * lp['rw_ka'])
    rh, kh, vh, ah = (_heads(t, RW_HEADS) for t in (r, k, v, a))
    y, s_new = _rwkv7_scan(rh, _heads(-jnp.exp(w), RW_HEADS), kh, vh, kk, ah,
                           st['rw_wkv'].astype(f32))
    y_mu = jnp.mean(y, -1, keepdims=True)
    y_var = jnp.mean(jnp.square(y - y_mu), -1, keepdims=True)
    y = ((y - y_mu) * lax.rsqrt(y_var + RW_GN_EPS)).reshape(bsz, t_len, RW_W)
    y = y * lp['rw_lnx_g'] + lp['rw_lnx_b']
    bonus = jnp.sum(rh * kh * _heads(lp['rw_rk'].astype(f32), RW_HEADS), -1, keepdims=True) * vh
    o_rw = (y + bonus.reshape(bsz, t_len, RW_W)) * g

    qr, kr, vr, gr = _split(p_ret.astype(f32), [RET_HEADS * RET_QK] * 2 + [RET_W] * 2)
    qh = _rotary(_heads(qr, RET_HEADS), pos)
    khr = _rotary(_heads(kr, RET_HEADS), pos) * (RET_QK ** -0.5)
    yr, r_new = _retention_chunked(qh, khr, _heads(vr, RET_HEADS), st['ret'].astype(f32))
    o_ret = jax.nn.silu(gr) * _rmsnorm_heads(yr).reshape(bsz, t_len, RET_W)

    qm, km, vm, om, im, fm = _split(p_ml.astype(f32), [ML_HEADS * ML_QK] * 2 + [ML_W] * 2 + [ML_HEADS] * 2)
    ig = ML_GATE_CAP * jnp.tanh((im + lp['ml_ib']) / ML_GATE_CAP)
    lf = jax.nn.log_sigmoid(ML_GATE_CAP * jnp.tanh((fm + lp['ml_fb']) / ML_GATE_CAP))
    hm, c_new, n_new, m_new = _mlstm_chunked(
        _heads(qm, ML_HEADS), _heads(km, ML_HEADS) * (ML_QK ** -0.5), _heads(vm, ML_HEADS), ig, lf,
        st['ml_c'].astype(f32), st['ml_n'].astype(f32), st['ml_m'].astype(f32))
    o_ml = jax.nn.sigmoid(om) * (_rmsnorm_heads(hm).reshape(bsz, t_len, ML_W) * lp['ml_norm'])

    mix = jnp.concatenate([o_rw, o_ret, o_ml], -1).astype(x.dtype) @ lp['w_out']
    x1 = _layernorm(ALPHA * x + mix, lp['ln1_g'], lp['ln1_b'])
    hdn = jax.nn.silu(x1 @ lp['w_gate']) * (x1 @ lp['w_up'])
    x2 = _layernorm(ALPHA * x1 + hdn @ lp['w_down'], lp['ln2_g'], lp['ln2_b'])
    new_st = {
        'rw_shift': x[:, -1].astype(st['rw_shift'].dtype),
        'rw_wkv': s_new.astype(st['rw_wkv'].dtype),
        'ret': r_new.astype(st['ret'].dtype),
        'ml_c': c_new.astype(st['ml_c'].dtype),
        'ml_n': n_new.astype(st['ml_n'].dtype),
        'ml_m': m_new.astype(st['ml_m'].dtype),
    }
    return x2, v_first, new_st


def _zero_state(b, dtype):
    return {
        'rw_shift': jnp.zeros((b, D_MODEL), dtype),
        'rw_wkv': jnp.zeros((b, RW_HEADS, RW_HD, RW_HD), dtype),
        'ret': jnp.zeros((b, RET_HEADS, RET_QK, RET_V), dtype),
        'ml_c': jnp.zeros((b, ML_HEADS, ML_QK, ML_V), dtype),
        'ml_n': jnp.zeros((b, ML_HEADS, ML_QK), dtype),
        'ml_m': jnp.zeros((b, ML_HEADS), dtype),
    }


def setup_inputs(seed: int = 0) -> dict:
    key = jax.random.key(seed)
    ks = iter(jax.random.split(key, 64))
    f32 = jnp.float32

    def nrm(shape, scale=1.0):
        return jax.random.normal(next(ks), shape, f32) * scale

    def uni(shape, lo, hi):
        return jax.random.uniform(next(ks), shape, f32, lo, hi)

    lv = DEPTH - 1
    col_scale = np.ones((P_TOTAL,), np.float32)
    col_scale[2 * RW_W:3 * RW_W] = BETA
    o = RW_P + 2 * RET_HEADS * RET_QK
    col_scale[o:o + RET_W] = BETA
    o = RW_P + RET_P + 2 * ML_HEADS * ML_QK
    col_scale[o:o + ML_W] = BETA
    lin = jnp.linspace(0.0, 1.0, RW_W, dtype=f32)
    return {
        'x_prompt': nrm((BATCH, SEQ, D_MODEL)),
        'x_sample': nrm((DEC_BATCH, DEC_SEQ, D_MODEL)),
        'state_rw_shift': nrm((DEPTH, DEC_BATCH, D_MODEL)),
        'state_rw_wkv': nrm((DEPTH, DEC_BATCH, RW_HEADS, RW_HD, RW_HD), 0.5),
        'state_ret': nrm((DEPTH, DEC_BATCH, RET_HEADS, RET_QK, RET_V)),
        'state_ml_c': nrm((DEPTH, DEC_BATCH, ML_HEADS, ML_QK, ML_V), 0.5),
        'state_ml_n': nrm((DEPTH, DEC_BATCH, ML_HEADS, ML_QK), 0.5),
        'state_ml_m': uni((DEPTH, DEC_BATCH, ML_HEADS), -2.0, 2.0),
        'ln0_g': 1.0 + nrm((D_MODEL,), 0.05),
        'ln0_b': nrm((D_MODEL,), 0.02),
        'w_in': nrm((DEPTH, D_MODEL, P_TOTAL), D_MODEL ** -0.5) * jnp.asarray(col_scale),
        'rw_mu': uni((DEPTH, RW_P), 0.1, 0.9),
        'rw_w0': (-6.5 + 5.0 * lin ** 0.9)[None, :] + nrm((DEPTH, RW_W), 0.1),
        'rw_w2': nrm((DEPTH, RW_LORA_W, RW_W), 0.1),
        'rw_a0': nrm((DEPTH, RW_W), 0.1),
        'rw_a2': nrm((DEPTH, RW_LORA_A, RW_W), 0.1),
        'rw_g2': nrm((DEPTH, RW_LORA_G, RW_W), RW_LORA_G ** -0.5),
        'rw_kk': 0.85 + nrm((DEPTH, RW_W), 0.02),
        'rw_ka': 1.0 + nrm((DEPTH, RW_W), 0.02),
        'rw_rk': -0.04 + nrm((DEPTH, RW_W), 0.02),
        'rw_lnx_g': 1.0 + nrm((DEPTH, RW_W), 0.05),
        'rw_lnx_b': nrm((DEPTH, RW_W), 0.02),
        'rw_v0': 1.0 + nrm((lv, RW_W), 0.1),
        'rw_v1': nrm((lv, D_MODEL, RW_LORA_V), D_MODEL ** -0.5),
        'rw_vmu': uni((lv, RW_LORA_V), 0.1, 0.9),
        'rw_v2': nrm((lv, RW_LORA_V, RW_W), 0.1),
        'ml_ib': -2.0 + nrm((DEPTH, ML_HEADS), 0.1),
        'ml_fb': jnp.linspace(3.0, 6.0, ML_HEADS, dtype=f32)[None, :] + nrm((DEPTH, ML_HEADS), 0.1),
        'ml_norm': 1.0 + nrm((DEPTH, ML_W), 0.05),
        'w_out': nrm((DEPTH, MIX_W, D_MODEL), BETA * MIX_W ** -0.5),
        'ln1_g': 1.0 + nrm((DEPTH, D_MODEL), 0.05),
        'ln1_b': nrm((DEPTH, D_MODEL), 0.02),
        'w_gate': nrm((DEPTH, D_MODEL, D_FF), D_MODEL ** -0.5),
        'w_up': nrm((DEPTH, D_MODEL, D_FF), BETA * D_MODEL ** -0.5),
        'w_down': nrm((DEPTH, D_FF, D_MODEL), BETA * D_FF ** -0.5),
        'ln2_g': 1.0 + nrm((DEPTH, D_MODEL), 0.05),
        'ln2_b': nrm((DEPTH, D_MODEL), 0.02),
    }


def reference(x_prompt, x_sample, state_rw_shift, state_rw_wkv, state_ret, state_ml_c, state_ml_n, state_ml_m,
              ln0_g, ln0_b, w_in, rw_mu, rw_w0, rw_w2, rw_a0, rw_a2, rw_g2, rw_kk, rw_ka, rw_rk,
              rw_lnx_g, rw_lnx_b, rw_v0, rw_v1, rw_vmu, rw_v2, ml_ib, ml_fb, ml_norm, w_out,
              ln1_g, ln1_b, w_gate, w_up, w_down, ln2_g, ln2_b):
    names = ('rw_shift', 'rw_wkv', 'ret', 'ml_c', 'ml_n', 'ml_m')
    layers = []
    for l in range(DEPTH):
        lp = {
            'w_in': w_in[l], 'rw_mu': rw_mu[l], 'rw_w0': rw_w0[l], 'rw_w2': rw_w2[l],
            'rw_a0': rw_a0[l], 'rw_a2': rw_a2[l], 'rw_g2': rw_g2[l], 'rw_kk': rw_kk[l],
            'rw_ka': rw_ka[l], 'rw_rk': rw_rk[l], 'rw_lnx_g': rw_lnx_g[l], 'rw_lnx_b': rw_lnx_b[l],
            'ml_ib': ml_ib[l], 'ml_fb': ml_fb[l], 'ml_norm': ml_norm[l], 'w_out': w_out[l],
            'ln1_g': ln1_g[l], 'ln1_b': ln1_b[l], 'w_gate': w_gate[l], 'w_up': w_up[l],
            'w_down': w_down[l], 'ln2_g': ln2_g[l], 'ln2_b': ln2_b[l],
        }
        if l > 0:
            lp.update(rw_v0=rw_v0[l - 1], rw_v1=rw_v1[l - 1], rw_vmu=rw_vmu[l - 1], rw_v2=rw_v2[l - 1])
        layers.append(lp)

    def run(x, pos, states):
        x = _layernorm(x, ln0_g, ln0_b)
        v_first = None
        new = []
        for l in range(DEPTH):
            x, v_first, st = _layer(x, pos, v_first, states[l], layers[l])
            new.append(st)
        return x, [jnp.stack([st[n] for st in new]) for n in names]

    st_in = (state_rw_shift, state_rw_wkv, state_ret, state_ml_c, state_ml_n, state_ml_m)
    sample_states = [{n: s[l] for n, s in zip(names, st_in)} for l in range(DEPTH)]
    prompt_states = [_zero_state(x_prompt.shape[0], x_prompt.dtype) for _ in range(DEPTH)]
    y_p, sp = run(x_prompt, jnp.arange(x_prompt.shape[1]), prompt_states)
    y_s, ss = run(x_sample, PAST_LEN + jnp.arange(x_sample.shape[1]), sample_states)
    return (y_p, y_s, sp[0], sp[1], sp[2], sp[3], sp[4], sp[5], ss[0], ss[1], ss[2], ss[3], ss[4], ss[5])
```

```python
import functools
import math

import numpy as np
import jax
import jax.numpy as jnp
from jax import lax
from jax.experimental import pallas as pl
from jax.experimental.pallas import tpu as pltpu

F32 = jnp.float32
BF16 = jnp.bfloat16

D_MODEL = 2048
DEPTH = 2
PAST_LEN = 16384
RW_HD = 64
RW_W = D_MODEL // 4
RW_HEADS = RW_W // RW_HD
RW_LORA_W = 64
RW_LORA_A = 64
RW_LORA_V = 32
RW_LORA_G = 128
RW_P = 3 * RW_W + RW_LORA_W + RW_LORA_A + RW_LORA_G
RW_GN_EPS = 64e-5
RET_V = 128
RET_QK = 64
RET_W = 3 * D_MODEL // 8
RET_HEADS = RET_W // RET_V
RET_P = 2 * RET_HEADS * RET_QK + 2 * RET_W
ML_V = 128
ML_QK = 64
ML_W = D_MODEL - RW_W - RET_W
ML_HEADS = ML_W // ML_V
ML_P = 2 * ML_HEADS * ML_QK + 2 * ML_W + 2 * ML_HEADS
ML_GATE_CAP = 15.0
P_TOTAL = RW_P + RET_P + ML_P
D_FF = ((8 * D_MODEL + 3 * 256 - 1) // (3 * 256)) * 256
CHUNK = 128
ROPE_BASE = 10000.0
LN_EPS = 1e-5
NORM_EPS = 1e-6
ALPHA = (2 * DEPTH) ** 0.25

LANES = 128
P_PAD = 6656
RW_CHUNK = 64
DEC_TB = 8
VMEM_LIMIT = 56 * 1024 * 1024

HIGHEST = lax.Precision.HIGHEST


def _cparams(sem):
    return pltpu.CompilerParams(dimension_semantics=sem, vmem_limit_bytes=VMEM_LIMIT)


def _dot(a, b, precision=None):
    return lax.dot_general(a, b, (((1,), (0,)), ((), ())), precision=precision, preferred_element_type=F32)


def _dot_nt(a, b, precision=None):
    return lax.dot_general(a, b, (((1,), (1,)), ((), ())), precision=precision, preferred_element_type=F32)


def _dot_tn(a, b, precision=None):
    return lax.dot_general(a, b, (((0,), (0,)), ((), ())), precision=precision, preferred_element_type=F32)


def _ln_rows(x, g, b):
    mu = jnp.mean(x, -1, keepdims=True)
    xc = x - mu
    var = jnp.mean(xc * xc, -1, keepdims=True)
    return xc * lax.rsqrt(var + LN_EPS) * g + b


def _ln_body(x_ref, g_ref, b_ref, of_ref, ob_ref):
    y = _ln_rows(x_ref[...], g_ref[...], b_ref[...])
    of_ref[...] = y
    ob_ref[...] = y.astype(BF16)


def _layernorm(x, g, b, tm):
    m, d = x.shape
    return pl.pallas_call(
        _ln_body,
        grid=(m // tm,),
        in_specs=[pl.BlockSpec((tm, d), lambda i: (i, 0)),
                  pl.BlockSpec((1, d), lambda i: (0, 0)),
                  pl.BlockSpec((1, d), lambda i: (0, 0))],
        out_specs=[pl.BlockSpec((tm, d), lambda i: (i, 0)),
                   pl.BlockSpec((tm, d), lambda i: (i, 0))],
        out_shape=[jax.ShapeDtypeStruct((m, d), F32), jax.ShapeDtypeStruct((m, d), BF16)],
        compiler_params=_cparams(("parallel",)),
        name="layernorm",
    )(x, g.reshape(1, d), b.reshape(1, d))


def _mm_body(x_ref, w_ref, o_ref):
    o_ref[...] = _dot(x_ref[...], w_ref[...]).astype(o_ref.dtype)


def _matmul(x, w, tm, tn, out_dtype=F32):
    m, k = x.shape
    n = w.shape[1]
    return pl.pallas_call(
        _mm_body,
        grid=(m // tm, n // tn),
        in_specs=[pl.BlockSpec((tm, k), lambda i, j: (i, 0)),
                  pl.BlockSpec((k, tn), lambda i, j: (0, j))],
        out_specs=pl.BlockSpec((tm, tn), lambda i, j: (i, j)),
        out_shape=jax.ShapeDtypeStruct((m, n), out_dtype),
        compiler_params=_cparams(("parallel", "parallel")),
        name="matmul",
    )(x, w)


def _swiglu_body(x_ref, wg_ref, wu_ref, o_ref):
    x = x_ref[...]
    g = _dot(x, wg_ref[...])
    u = _dot(x, wu_ref[...])
    o_ref[...] = (g * jax.nn.sigmoid(g) * u).astype(o_ref.dtype)


def _matmul_swiglu(x, wg, wu, tm, tn):
    m, k = x.shape
    n = wg.shape[1]
    return pl.pallas_call(
        _swiglu_body,
        grid=(m // tm, n // tn),
        in_specs=[pl.BlockSpec((tm, k), lambda i, j: (i, 0)),
                  pl.BlockSpec((k, tn), lambda i, j: (0, j)),
                  pl.BlockSpec((k, tn), lambda i, j: (0, j))],
        out_specs=pl.BlockSpec((tm, tn), lambda i, j: (i, j)),
        out_shape=jax.ShapeDtypeStruct((m, n), BF16),
        compiler_params=_cparams(("parallel", "parallel")),
        name="matmul_swiglu",
    )(x, wg, wu)


def _mm_res_ln_body(x_ref, w_ref, res_ref, g_ref, b_ref, of_ref, ob_ref, acc_ref, *, nk):
    kk = pl.program_id(1)

    @pl.when(kk == 0)
    def _():
        acc_ref[...] = jnp.zeros_like(acc_ref)

    acc_ref[...] += _dot(x_ref[...], w_ref[...])

    @pl.when(kk == nk - 1)
    def _():
        y = _ln_rows(ALPHA * res_ref[...] + acc_ref[...], g_ref[...], b_ref[...])
        of_ref[...] = y
        ob_ref[...] = y.astype(BF16)


def _matmul_res_ln(x, w, res, g, b, tm, tk):
    m, k = x.shape
    n = w.shape[1]
    nk = k // tk
    return pl.pallas_call(
        functools.partial(_mm_res_ln_body, nk=nk),
        grid=(m // tm, nk),
        in_specs=[pl.BlockSpec((tm, tk), lambda i, j: (i, j)),
                  pl.BlockSpec((tk, n), lambda i, j: (j, 0)),
                  pl.BlockSpec((tm, n), lambda i, j: (i, 0)),
                  pl.BlockSpec((1, n), lambda i, j: (0, 0)),
                  pl.BlockSpec((1, n), lambda i, j: (0, 0))],
        out_specs=[pl.BlockSpec((tm, n), lambda i, j: (i, 0)),
                   pl.BlockSpec((tm, n), lambda i, j: (i, 0))],
        out_shape=[jax.ShapeDtypeStruct((m, n), F32), jax.ShapeDtypeStruct((m, n), BF16)],
        scratch_shapes=[pltpu.VMEM((tm, n), F32)],
        compiler_params=_cparams(("parallel", "arbitrary")),
        name="matmul_res_ln",
    )(x, w, res, g.reshape(1, n), b.reshape(1, n))


def _rwkv_chunk_body(r_ref, lw_ref, k_ref, v_ref, kk_ref, a_ref, s0_ref, y_ref, sf_ref, s_scr, *, nc):
    L, N, H = RW_CHUNK, RW_HD, RW_HEADS
    c = pl.program_id(1)

    @pl.when(c == 0)
    def _():
        s_scr[...] = s0_ref[0]

    row = lax.broadcasted_iota(jnp.int32, (L, L), 0)
    col = lax.broadcasted_iota(jnp.int32, (L, L), 1)
    lower = row >= col
    strict = row > col
    tri = lower.astype(F32)
    eye_l = (row == col).astype(F32)
    rown = lax.broadcasted_iota(jnp.int32, (N, N), 0)
    coln = lax.broadcasted_iota(jnp.int32, (N, N), 1)
    eye_n = rown == coln
    dot = functools.partial(_dot, precision=HIGHEST)
    dot_nt = functools.partial(_dot_nt, precision=HIGHEST)
    dot_tn = functools.partial(_dot_tn, precision=HIGHEST)

    for h in range(H):
        sl = slice(h * N, (h + 1) * N)
        r = r_ref[0, :, sl]
        lw = lw_ref[0, :, sl]
        k = k_ref[0, :, sl]
        v = v_ref[0, :, sl]
        kk = kk_ref[0, :, sl]
        ap = kk * a_ref[0, :, sl]
        cum = dot(tri, lw)
        tot = cum[L - 1:L, :]
        e_neg = jnp.exp(-cum)
        kk_t = kk * jnp.exp(cum - lw)
        ap_h = ap * e_neg
        k_h = k * e_neg
        r_t = r * jnp.exp(cum)
        s0 = s_scr[h]
        n_mat = jnp.where(strict, dot_nt(kk_t, ap_h), 0.0)
        m_k = jnp.where(strict, dot_nt(kk_t, k_h), 0.0)
        m_r = jnp.where(lower, dot_nt(r_t, k_h), 0.0)
        m_a = jnp.where(lower, dot_nt(r_t, ap_h), 0.0)
        inv = eye_l - n_mat
        pw = n_mat
        for _ in range(int(math.log2(L)) - 1):
            pw = dot(pw, pw)
            inv = inv + dot(inv, pw)
        u = dot(inv, dot(kk_t, s0) + dot(m_k, v))
        y_ref[0, :, sl] = dot(r_t, s0) + dot(m_r, v) - dot(m_a, u)
        e_end = jnp.exp(tot - cum)
        decay = jnp.where(eye_n, jnp.exp(tot), 0.0)
        s_scr[h] = dot(decay, s0) + dot_tn(k * e_end, v) - dot_tn(ap * e_end, u)

    @pl.when(c == nc - 1)
    def _():
        sf_ref[0] = s_scr[...]


def _rwkv_chunked(r, lw, k, v, kk, a, s0):
    b, t, w = r.shape
    L = RW_CHUNK
    nc = t // L
    seq = pl.BlockSpec((1, L, w), lambda i, j: (i, j, 0))
    st = pl.BlockSpec((1, RW_HEADS, RW_HD, RW_HD), lambda i, j: (i, 0, 0, 0))
    return pl.pallas_call(
        functools.partial(_rwkv_chunk_body, nc=nc),
        grid=(b, nc),
        in_specs=[seq] * 6 + [st],
        out_specs=[seq, st],
        out_shape=[jax.ShapeDtypeStruct((b, t, w), F32),
                   jax.ShapeDtypeStruct((b, RW_HEADS, RW_HD, RW_HD), F32)],
        scratch_shapes=[pltpu.VMEM((RW_HEADS, RW_HD, RW_HD), F32)],
        compiler_params=_cparams(("parallel", "arbitrary")),
        name="rwkv_chunked",
    )(r, lw, k, v, kk, a, s0)


def _ret_log_gamma(h):
    return math.log1p(-(2.0 ** (-5.0 - h)))


def _ret_chunk_body(q_ref, k_ref, v_ref, s0_ref, y_ref, sf_ref, s_scr, *, nc):
    L, DK, DV, H = CHUNK, RET_QK, RET_V, RET_HEADS
    c = pl.program_id(1)

    @pl.when(c == 0)
    def _():
        s_scr[...] = s0_ref[0]

    row = lax.broadcasted_iota(jnp.int32, (L, L), 0)
    col = lax.broadcasted_iota(jnp.int32, (L, L), 1)
    rel = (row - col).astype(F32)
    idx = lax.broadcasted_iota(jnp.int32, (L, 1), 0).astype(F32)
    for h in range(H):
        lg = _ret_log_gamma(h)
        q = q_ref[0, :, h * DK:(h + 1) * DK]
        k = k_ref[0, :, h * DK:(h + 1) * DK]
        v = v_ref[0, :, h * DV:(h + 1) * DV].astype(BF16)
        dmask = jnp.where(rel >= 0, jnp.exp(jnp.maximum(rel, 0.0) * lg), 0.0)
        scores = _dot_nt(q.astype(BF16), k.astype(BF16)) * dmask
        s_prev = s_scr[h]
        q_dec = q * jnp.exp((idx + 1.0) * lg)
        y_ref[0, :, h * DV:(h + 1) * DV] = (_dot(scores.astype(BF16), v)
                                            + _dot(q_dec.astype(BF16), s_prev.astype(BF16)))
        k_end = k * jnp.exp((L - 1.0 - idx) * lg)
        s_scr[h] = math.exp(L * lg) * s_prev + _dot_tn(k_end.astype(BF16), v)

    @pl.when(c == nc - 1)
    def _():
        sf_ref[0] = s_scr[...]


def _ret_chunked(q, k, v, s0):
    b, t, _ = q.shape
    L = CHUNK
    nc = t // L
    qk = pl.BlockSpec((1, L, RET_HEADS * RET_QK), lambda i, j: (i, j, 0))
    vs = pl.BlockSpec((1, L, RET_W), lambda i, j: (i, j, 0))
    st = pl.BlockSpec((1, RET_HEADS, RET_QK, RET_V), lambda i, j: (i, 0, 0, 0))
    return pl.pallas_call(
        functools.partial(_ret_chunk_body, nc=nc),
        grid=(b, nc),
        in_specs=[qk, qk, vs, st],
        out_specs=[vs, st],
        out_shape=[jax.ShapeDtypeStruct((b, t, RET_W), F32),
                   jax.ShapeDtypeStruct((b, RET_HEADS, RET_QK, RET_V), F32)],
        scratch_shapes=[pltpu.VMEM((RET_HEADS, RET_QK, RET_V), F32)],
        compiler_params=_cparams(("parallel", "arbitrary")),
        name="ret_chunked",
    )(q, k, v, s0)


ML_HPAD = 8


def _ml_chunk_body(q_ref, k_ref, v_ref, g_ref, c0_ref, n0_ref, m0_ref,
                   h_ref, cf_ref, nf_ref, mf_ref, c_scr, n_scr, m_scr, *, nc):
    L, DK, DV, H = CHUNK, ML_QK, ML_V, ML_HEADS
    ci = pl.program_id(1)

    @pl.when(ci == 0)
    def _():
        c_scr[...] = c0_ref[0]
        n_scr[...] = n0_ref[0]
        m_scr[...] = m0_ref[0]

    row = lax.broadcasted_iota(jnp.int32, (L, L), 0)
    col = lax.broadcasted_iota(jnp.int32, (L, L), 1)
    causal = row >= col
    tri = causal.astype(F32)
    g = g_ref[0]
    cum = _dot(tri, g, precision=HIGHEST)
    g_t = g.T
    cum_t = cum.T
    for h in range(H):
        q = q_ref[0, :, h * DK:(h + 1) * DK]
        k = k_ref[0, :, h * DK:(h + 1) * DK]
        v = v_ref[0, :, h * DV:(h + 1) * DV].astype(BF16)
        ig_col = g[:, h:h + 1]
        ig_row = g_t[h:h + 1, :]
        b_col = cum[:, H + h:H + h + 1]
        b_row = cum_t[H + h:H + h + 1, :]
        b_tot = cum[L - 1:L, H + h:H + h + 1]
        m_prev = m_scr[h:h + 1, 0:1]
        c_prev = c_scr[h]
        n_prev = n_scr[h:h + 1, :]
        a_row = b_tot - b_row + ig_row
        a_col = b_tot - b_col + ig_col
        m_new = jnp.maximum(b_tot + m_prev, jnp.max(a_row, axis=1, keepdims=True))
        dec = jnp.exp(b_tot + m_prev - m_new)
        kw = k * jnp.exp(a_col - m_new)
        c_scr[h] = dec * c_prev + _dot_tn(kw.astype(BF16), v)
        n_scr[h:h + 1, :] = dec * n_prev + jnp.sum(kw, axis=0, keepdims=True)
        m_scr[h:h + 1, :] = jnp.broadcast_to(m_new, (1, LANES))
        dlog = jnp.where(causal, b_col - b_row + ig_row, -jnp.inf)
        inter = b_col + m_prev
        m_i = jnp.maximum(jnp.max(dlog, axis=1, keepdims=True), inter)
        s = _dot_nt(q.astype(BF16), k.astype(BF16)) * jnp.exp(dlog - m_i)
        sc = jnp.exp(inter - m_i)
        num = _dot(s.astype(BF16), v) + sc * _dot(q.astype(BF16), c_prev.astype(BF16))
        den = jnp.sum(s, axis=1, keepdims=True) + sc * jnp.sum(q * n_prev, axis=1, keepdims=True)
        h_ref[0, :, h * DV:(h + 1) * DV] = num / jnp.maximum(jnp.abs(den), jnp.exp(-m_i))

    @pl.when(ci == nc - 1)
    def _():
        cf_ref[0] = c_scr[...]
        nf_ref[0] = n_scr[...]
        mf_ref[0] = m_scr[...]


def _ml_chunked(q, k, v, g, c0, n0, m0):
    b, t, _ = q.shape
    L = CHUNK
    nc = t // L
    qk = pl.BlockSpec((1, L, ML_HEADS * ML_QK), lambda i, j: (i, j, 0))
    vs = pl.BlockSpec((1, L, ML_W), lambda i, j: (i, j, 0))
    gs = pl.BlockSpec((1, L, LANES), lambda i, j: (i, j, 0))
    cs = pl.BlockSpec((1, ML_HEADS, ML_QK, ML_V), lambda i, j: (i, 0, 0, 0))
    ns = pl.BlockSpec((1, ML_HPAD, ML_QK), lambda i, j: (i, 0, 0))
    ms = pl.BlockSpec((1, ML_HPAD, LANES), lambda i, j: (i, 0, 0))
    return pl.pallas_call(
        functools.partial(_ml_chunk_body, nc=nc),
        grid=(b, nc),
        in_specs=[qk, qk, vs, gs, cs, ns, ms],
        out_specs=[vs, cs, ns, ms],
        out_shape=[jax.ShapeDtypeStruct((b, t, ML_W), F32),
                   jax.ShapeDtypeStruct((b, ML_HEADS, ML_QK, ML_V), F32),
                   jax.ShapeDtypeStruct((b, ML_HPAD, ML_QK), F32),
                   jax.ShapeDtypeStruct((b, ML_HPAD, LANES), F32)],
        scratch_shapes=[pltpu.VMEM((ML_HEADS, ML_QK, ML_V), F32),
                        pltpu.VMEM((ML_HPAD, ML_QK), F32),
                        pltpu.VMEM((ML_HPAD, LANES), F32)],
        compiler_params=_cparams(("parallel", "arbitrary")),
        name="ml_chunked",
    )(q, k, v, g, c0, n0, m0)


def _to_cols(x):
    b, c = x.shape
    return x.reshape(b // DEC_TB, DEC_TB, c).transpose(0, 2, 1)


def _rwkv_step_body(w_ref, ap_ref, k_ref, kk_ref, r_ref, v_ref, s_ref, y_ref, so_ref):
    N = RW_HD
    for j in range(DEC_TB):
        for h in range(RW_HEADS):
            cs = slice(h * N, (h + 1) * N)
            col = lambda ref: ref[0, cs, j:j + 1]
            s = s_ref[j, h]
            v = v_ref[j:j + 1, cs]
            sa = jnp.sum(col(kk_ref) * s, axis=0, keepdims=True)
            s_new = col(w_ref) * s - col(ap_ref) * sa + col(k_ref) * v
            so_ref[j, h] = s_new
            y_ref[j:j + 1, cs] = jnp.sum(col(r_ref) * s_new, axis=0, keepdims=True)


def _rwkv_step(wdec, ap, k, kk, r, v, s0):
    b, w = v.shape
    cols = pl.BlockSpec((1, w, DEC_TB), lambda i: (i, 0, 0))
    rows = pl.BlockSpec((DEC_TB, w), lambda i: (i, 0))
    st = pl.BlockSpec((DEC_TB, RW_HEADS, RW_HD, RW_HD), lambda i: (i, 0, 0, 0))
    return pl.pallas_call(
        _rwkv_step_body,
        grid=(b // DEC_TB,),
        in_specs=[cols] * 5 + [rows, st],
        out_specs=[rows, st],
        out_shape=[jax.ShapeDtypeStruct((b, w), F32), jax.ShapeDtypeStruct(s0.shape, F32)],
        compiler_params=_cparams(("parallel",)),
        name="rwkv_step",
    )(_to_cols(wdec), _to_cols(ap), _to_cols(k), _to_cols(kk), _to_cols(r), v, s0)


def _ret_step_body(qc_ref, kc_ref, v_ref, s_ref, y_ref, so_ref):
    DK, DV = RET_QK, RET_V
    for j in range(DEC_TB):
        for h in range(RET_HEADS):
            gamma = math.exp(_ret_log_gamma(h))
            q = qc_ref[0, h * DK:(h + 1) * DK, j:j + 1]
            k = kc_ref[0, h * DK:(h + 1) * DK, j:j + 1]
            v = v_ref[j:j + 1, h * DV:(h + 1) * DV]
            s = s_ref[j, h]
            qk = jnp.sum(q * k, axis=0, keepdims=True)
            y_ref[j:j + 1, h * DV:(h + 1) * DV] = qk * v + gamma * jnp.sum(q * s, axis=0, keepdims=True)
            so_ref[j, h] = gamma * s + k * v


def _ret_step(q, k, v, s0):
    b = q.shape[0]
    cols = pl.BlockSpec((1, RET_HEADS * RET_QK, DEC_TB), lambda i: (i, 0, 0))
    rows = pl.BlockSpec((DEC_TB, RET_W), lambda i: (i, 0))
    st = pl.BlockSpec((DEC_TB, RET_HEADS, RET_QK, RET_V), lambda i: (i, 0, 0, 0))
    return pl.pallas_call(
        _ret_step_body,
        grid=(b // DEC_TB,),
        in_specs=[cols, cols, rows, st],
        out_specs=[rows, st],
        out_shape=[jax.ShapeDtypeStruct((b, RET_W), F32), jax.ShapeDtypeStruct(s0.shape, F32)],
        compiler_params=_cparams(("parallel",)),
        name="ret_step",
    )(_to_cols(q), _to_cols(k), v, s0)


def _ml_step_body(qc_ref, kc_ref, q_ref, k_ref, v_ref, ig_ref, lf_ref, c_ref, n_ref, m_ref,
                  h_ref, co_ref, no_ref, mo_ref):
    DK, DV = ML_QK, ML_V
    for j in range(DEC_TB):
        for h in range(ML_HEADS):
            ks = slice(h * DK, (h + 1) * DK)
            vs = slice(h * DV, (h + 1) * DV)
            q_col = qc_ref[0, ks, j:j + 1]
            k_col = kc_ref[0, ks, j:j + 1]
            q_row = q_ref[j:j + 1, ks]
            k_row = k_ref[j:j + 1, ks]
            v = v_ref[j:j + 1, vs]
            ig = ig_ref[j:j + 1, h:h + 1]
            lf = lf_ref[j:j + 1, h:h + 1]
            m_prev = m_ref[j:j + 1, h:h + 1]
            c_prev = c_ref[j, h]
            n_prev = n_ref[j, h:h + 1, :]
            m_new = jnp.maximum(lf + m_prev, ig)
            dec = jnp.exp(lf + m_prev - m_new)
            wgt = jnp.exp(ig - m_new)
            co_ref[j, h] = dec * c_prev + (k_col * wgt) * v
            no_ref[j, h:h + 1, :] = dec * n_prev + k_row * wgt
            mo_ref[j:j + 1, h:h + 1] = m_new
            s = jnp.sum(q_row * k_row, axis=1, keepdims=True) * wgt
            num = s * v + dec * jnp.sum(q_col * c_prev, axis=0, keepdims=True)
            den = s + dec * jnp.sum(q_row * n_prev, axis=1, keepdims=True)
            h_ref[j:j + 1, vs] = num / jnp.maximum(jnp.abs(den), jnp.exp(-m_new))


def _ml_step(q, k, v, ig, lf, c0, n0, m0):
    b = q.shape[0]
    cols = pl.BlockSpec((1, ML_HEADS * ML_QK, DEC_TB), lambda i: (i, 0, 0))
    qk_rows = pl.BlockSpec((DEC_TB, ML_HEADS * ML_QK), lambda i: (i, 0))
    rows = pl.BlockSpec((DEC_TB, ML_W), lambda i: (i, 0))
    sc = pl.BlockSpec((DEC_TB, ML_HEADS), lambda i: (i, 0))
    cs = pl.BlockSpec((DEC_TB, ML_HEADS, ML_QK, ML_V), lambda i: (i, 0, 0, 0))
    ns = pl.BlockSpec((DEC_TB, ML_HEADS, ML_QK), lambda i: (i, 0, 0))
    return pl.pallas_call(
        _ml_step_body,
        grid=(b // DEC_TB,),
        in_specs=[cols, cols, qk_rows, qk_rows, rows, sc, sc, cs, ns, sc],
        out_specs=[rows, cs, ns, sc],
        out_shape=[jax.ShapeDtypeStruct((b, ML_W), F32), jax.ShapeDtypeStruct(c0.shape, F32),
                   jax.ShapeDtypeStruct(n0.shape, F32), jax.ShapeDtypeStruct(m0.shape, F32)],
        compiler_params=_cparams(("parallel",)),
        name="ml_step",
    )(_to_cols(q), _to_cols(k), q, k, v, ig, lf, c0, n0, m0)


def _heads(a, h):
    return a.reshape(a.shape[:-1] + (h, a.shape[-1] // h))


def _shift_prev(p, prev_row):
    return jnp.concatenate([prev_row[:, None, :], p[:, :-1]], axis=1)


def _rotary(x, pos):
    half = x.shape[-1] // 2
    inv = ROPE_BASE ** (-jnp.arange(half, dtype=F32) / half)
    ang = pos.astype(F32)[:, None] * inv[None, :]
    cos = jnp.cos(ang)[None, :, None, :]
    sin = jnp.sin(ang)[None, :, None, :]
    x1, x2 = x[..., :half], x[..., half:]
    return jnp.concatenate([x1 * cos - x2 * sin, x1 * sin + x2 * cos], -1)


def _small_matmul(x, w):
    lead = x.shape[:-1]
    kdim, n = w.shape
    x2 = x.reshape(-1, kdim)
    m = x2.shape[0]
    kp = -(-kdim // LANES) * LANES
    npad = -(-n // LANES) * LANES
    x2 = jnp.pad(x2.astype(BF16), ((0, 0), (0, kp - kdim)))
    w2 = jnp.pad(w.astype(BF16), ((0, kp - kdim), (0, npad - n)))
    tm = 1024 if m % 1024 == 0 else m
    out = _matmul(x2, w2, tm, npad)
    return out[:, :n].reshape(lead + (n,))


def _mix_group(p, pv, x_shape, pos, v_first, st, lp, prev_rw, prev_pv, chunked):
    bsz, t_len = x_shape
    p_rw = p[..., :RW_P]
    p_ret = p[..., RW_P:RW_P + RET_P]
    p_ml = p[..., RW_P + RET_P:P_TOTAL]

    p_rw_prev = _shift_prev(p_rw, prev_rw)
    mixed = p_rw + (p_rw_prev - p_rw) * lp['rw_mu']
    sizes = np.cumsum([RW_W, RW_W, RW_W, RW_LORA_W, RW_LORA_A, RW_LORA_G])[:-1]
    r, k, v, xw, xa, xg = jnp.split(mixed, [int(s) for s in sizes], axis=-1)
    w = -jax.nn.softplus(-(lp['rw_w0'] + _small_matmul(jnp.tanh(xw), lp['rw_w2']))) - 0.5
    a = jax.nn.sigmoid(lp['rw_a0'] + _small_matmul(xa, lp['rw_a2']))
    g = _small_matmul(jax.nn.sigmoid(xg), lp['rw_g2'])
    if v_first is None:
        v_first = v
    else:
        pv_prev = _shift_prev(pv, prev_pv)
        xv = pv + (pv_prev - pv) * lp['rw_vmu']
        v = v + (v_first - v) * jax.nn.sigmoid(lp['rw_v0'] + _small_matmul(xv, lp['rw_v2']))
    kk = _heads(k * lp['rw_kk'], RW_HEADS)
    kk = kk * lax.rsqrt(jnp.maximum(jnp.sum(jnp.square(kk), -1, keepdims=True), 1e-24))
    kk = kk.reshape(bsz, t_len, RW_W)
    k = k * (1.0 + (a - 1.0) * lp['rw_ka'])
    lw = -jnp.exp(w)
    if chunked:
        y, s_new = _rwkv_chunked(r, lw, k, v, kk, a, st['rw_wkv'])
    else:
        y, s_new = _rwkv_step(jnp.exp(lw)[:, 0], (kk * a)[:, 0], k[:, 0], kk[:, 0], r[:, 0], v[:, 0], st['rw_wkv'])
        y = y[:, None, :]
    y = _heads(y, RW_HEADS)
    y_mu = jnp.mean(y, -1, keepdims=True)
    y_var = jnp.mean(jnp.square(y - y_mu), -1, keepdims=True)
    y = ((y - y_mu) * lax.rsqrt(y_var + RW_GN_EPS)).reshape(bsz, t_len, RW_W)
    y = y * lp['rw_lnx_g'] + lp['rw_lnx_b']
    rh, kh, vh = (_heads(u, RW_HEADS) for u in (r, k, v))
    bonus = jnp.sum(rh * kh * _heads(lp['rw_rk'], RW_HEADS), -1, keepdims=True) * vh
    o_rw = (y + bonus.reshape(bsz, t_len, RW_W)) * g

    nqk = RET_HEADS * RET_QK
    qr, kr, vr, gr = (p_ret[..., :nqk], p_ret[..., nqk:2 * nqk],
                      p_ret[..., 2 * nqk:2 * nqk + RET_W], p_ret[..., 2 * nqk + RET_W:])
    qh = _rotary(_heads(qr, RET_HEADS), pos).reshape(bsz, t_len, nqk)
    khr = (_rotary(_heads(kr, RET_HEADS), pos) * (RET_QK ** -0.5)).reshape(bsz, t_len, nqk)
    if chunked:
        yr, r_new = _ret_chunked(qh, khr, vr, st['ret'])
    else:
        yr, r_new = _ret_step(qh[:, 0], khr[:, 0], vr[:, 0], st['ret'])
        yr = yr[:, None, :]
    yr = _heads(yr, RET_HEADS)
    yr = yr * lax.rsqrt(jnp.mean(jnp.square(yr), -1, keepdims=True) + NORM_EPS)
    o_ret = jax.nn.silu(gr) * yr.reshape(bsz, t_len, RET_W)

    nqk = ML_HEADS * ML_QK
    qm, km, vm, om = (p_ml[..., :nqk], p_ml[..., nqk:2 * nqk],
                      p_ml[..., 2 * nqk:2 * nqk + ML_W], p_ml[..., 2 * nqk + ML_W:2 * nqk + 2 * ML_W])
    im = p_ml[..., 2 * nqk + 2 * ML_W:2 * nqk + 2 * ML_W + ML_HEADS]
    fm = p_ml[..., 2 * nqk + 2 * ML_W + ML_HEADS:]
    ig = ML_GATE_CAP * jnp.tanh((im + lp['ml_ib']) / ML_GATE_CAP)
    lf = jax.nn.log_sigmoid(ML_GATE_CAP * jnp.tanh((fm + lp['ml_fb']) / ML_GATE_CAP))
    km = km * (ML_QK ** -0.5)
    if chunked:
        gates = jnp.pad(jnp.concatenate([ig, lf], -1), ((0, 0), (0, 0), (0, LANES - 2 * ML_HEADS)))
        n0 = jnp.pad(st['ml_n'], ((0, 0), (0, ML_HPAD - ML_HEADS), (0, 0)))
        m0 = jnp.pad(st['ml_m'][:, :, None], ((0, 0), (0, ML_HPAD - ML_HEADS), (0, LANES - 1)))
        hm, c_new, n_new, m_new = _ml_chunked(qm, km, vm, gates, st['ml_c'], n0, m0)
        n_new = n_new[:, :ML_HEADS]
        m_new = m_new[:, :ML_HEADS, 0]
    else:
        hm, c_new, n_new, m_new = _ml_step(qm[:, 0], km[:, 0], vm[:, 0], ig[:, 0], lf[:, 0],
                                           st['ml_c'], st['ml_n'], st['ml_m'])
        hm = hm[:, None, :]
    hm = _heads(hm, ML_HEADS)
    hm = hm * lax.rsqrt(jnp.mean(jnp.square(hm), -1, keepdims=True) + NORM_EPS)
    o_ml = jax.nn.sigmoid(om) * (hm.reshape(bsz, t_len, ML_W) * lp['ml_norm'])

    o = jnp.concatenate([o_rw, o_ret, o_ml], -1).astype(BF16)
    new_st = (s_new, r_new, c_new, n_new, m_new)
    return o, v_first, new_st


def _row_tile(m, cap):
    best = 16
    for t in range(16, cap + 1, 16):
        if m % t == 0:
            best = t
    return best


def kernel(x_prompt, x_sample, state_rw_shift, state_rw_wkv, state_ret, state_ml_c, state_ml_n, state_ml_m,
           ln0_g, ln0_b, w_in, rw_mu, rw_w0, rw_w2, rw_a0, rw_a2, rw_g2, rw_kk, rw_ka, rw_rk,
           rw_lnx_g, rw_lnx_b, rw_v0, rw_v1, rw_vmu, rw_v2, ml_ib, ml_fb, ml_norm, w_out,
           ln1_g, ln1_b, w_gate, w_up, w_down, ln2_g, ln2_b):
    bp, tp, d = x_prompt.shape
    bs, ts, _ = x_sample.shape
    n_p = bp * tp
    n_s = bs * ts
    n_all = n_p + n_s
    tm_big = _row_tile(n_all, 1664)
    tm_mid = _row_tile(n_all, 832)
    tm_small = _row_tile(n_all, 416)

    x_all = jnp.concatenate([x_prompt.reshape(n_p, d), x_sample.reshape(n_s, d)], 0)
    x_f, x_b = _layernorm(x_all, ln0_g, ln0_b, tm_small)

    pos_p = jnp.arange(tp)
    pos_s = PAST_LEN + jnp.arange(ts)
    v_first_p = v_first_s = None
    outs_p, outs_s = [], []
    for l in range(DEPTH):
        lp = {
            'rw_mu': rw_mu[l], 'rw_w0': rw_w0[l], 'rw_w2': rw_w2[l], 'rw_a0': rw_a0[l], 'rw_a2': rw_a2[l],
            'rw_g2': rw_g2[l], 'rw_kk': rw_kk[l], 'rw_ka': rw_ka[l], 'rw_rk': rw_rk[l],
            'rw_lnx_g': rw_lnx_g[l], 'rw_lnx_b': rw_lnx_b[l], 'ml_ib': ml_ib[l], 'ml_fb': ml_fb[l],
            'ml_norm': ml_norm[l],
        }
        if l > 0:
            lp.update(rw_v0=rw_v0[l - 1], rw_vmu=rw_vmu[l - 1], rw_v2=rw_v2[l - 1])
        w_in_b = jnp.pad(w_in[l].astype(BF16), ((0, 0), (0, P_PAD - P_TOTAL)))
        p_all = _matmul(x_b, w_in_b, tm_big, 512)
        shift_b = state_rw_shift[l].astype(BF16)
        prev_rw_s = _matmul(shift_b, w_in_b[:, :RW_P], bs, RW_P // 2)
        if l > 0:
            v1_b = jnp.pad(rw_v1[l - 1].astype(BF16), ((0, 0), (0, LANES - RW_LORA_V)))
            pv_all = _matmul(x_b, v1_b, tm_big, LANES)[:, :RW_LORA_V]
            prev_pv_s = _matmul(shift_b, v1_b, bs, LANES)[:, :RW_LORA_V]
            pv_p = pv_all[:n_p].reshape(bp, tp, RW_LORA_V)
            pv_s = pv_all[n_p:].reshape(bs, ts, RW_LORA_V)
            prev_pv_p = jnp.zeros((bp, RW_LORA_V), F32)
        else:
            pv_p = pv_s = prev_pv_p = prev_pv_s = None

        st_p = {
            'rw_wkv': jnp.zeros((bp, RW_HEADS, RW_HD, RW_HD), F32),
            'ret': jnp.zeros((bp, RET_HEADS, RET_QK, RET_V), F32),
            'ml_c': jnp.zeros((bp, ML_HEADS, ML_QK, ML_V), F32),
            'ml_n': jnp.zeros((bp, ML_HEADS, ML_QK), F32),
            'ml_m': jnp.zeros((bp, ML_HEADS), F32),
        }
        st_s = {'rw_wkv': state_rw_wkv[l], 'ret': state_ret[l], 'ml_c': state_ml_c[l],
                'ml_n': state_ml_n[l], 'ml_m': state_ml_m[l]}
        o_p, v_first_p, new_p = _mix_group(
            p_all[:n_p].reshape(bp, tp, P_PAD), pv_p, (bp, tp), pos_p, v_first_p, st_p, lp,
            jnp.zeros((bp, RW_P), F32), prev_pv_p, True)
        o_s, v_first_s, new_s = _mix_group(
            p_all[n_p:].reshape(bs, ts, P_PAD), pv_s, (bs, ts), pos_s, v_first_s, st_s, lp,
            prev_rw_s, prev_pv_s, False)
        outs_p.append((x_f[:n_p].reshape(bp, tp, d)[:, -1],) + new_p)
        outs_s.append((x_f[n_p:].reshape(bs, ts, d)[:, -1],) + new_s)

        o_all = jnp.concatenate([o_p.reshape(n_p, d), o_s.reshape(n_s, d)], 0)
        x1_f, x1_b = _matmul_res_ln(o_all, w_out[l].astype(BF16), x_f, ln1_g[l], ln1_b[l], tm_small, d)
        hdn = _matmul_swiglu(x1_b, w_gate[l].astype(BF16), w_up[l].astype(BF16), tm_big, 512)
        x_f, x_b = _matmul_res_ln(hdn, w_down[l].astype(BF16), x1_f, ln2_g[l], ln2_b[l], tm_mid, 512)

    y_p = x_f[:n_p].reshape(bp, tp, d)
    y_s = x_f[n_p:].reshape(bs, ts, d)
    sp = [jnp.stack([o[i] for o in outs_p]) for i in range(6)]
    ss = [jnp.stack([o[i] for o in outs_s]) for i in range(6)]
    return (y_p, y_s, sp[0], sp[1], sp[2], sp[3], sp[4], sp[5], ss[0], ss[1], ss[2], ss[3], ss[4], ss[5])
```

```python
import functools
import math

import numpy as np
import jax
import jax.numpy as jnp
from jax import lax
from jax.experimental import pallas as pl
from jax.experimental.pallas import tpu as pltpu

F32 = jnp.float32
BF16 = jnp.bfloat16

D_MODEL = 2048
DEPTH = 2
PAST_LEN = 16384
RW_HD = 64
RW_W = D_MODEL // 4
RW_HEADS = RW_W // RW_HD
RW_LORA_W = 64
RW_LORA_A = 64
RW_LORA_V = 32
RW_LORA_G = 128
RW_P = 3 * RW_W + RW_LORA_W + RW_LORA_A + RW_LORA_G
RW_GN_EPS = 64e-5
RET_V = 128
RET_QK = 64
RET_W = 3 * D_MODEL // 8
RET_HEADS = RET_W // RET_V
RET_P = 2 * RET_HEADS * RET_QK + 2 * RET_W
ML_V = 128
ML_QK = 64
ML_W = D_MODEL - RW_W - RET_W
ML_HEADS = ML_W // ML_V
ML_P = 2 * ML_HEADS * ML_QK + 2 * ML_W + 2 * ML_HEADS
ML_GATE_CAP = 15.0
P_TOTAL = RW_P + RET_P + ML_P
D_FF = ((8 * D_MODEL + 3 * 256 - 1) // (3 * 256)) * 256
CHUNK = 128
ROPE_BASE = 10000.0
LN_EPS = 1e-5
NORM_EPS = 1e-6
ALPHA = (2 * DEPTH) ** 0.25

LANES = 128
P_PAD = 6656
P_V1 = 6528
RW_CHUNK = 64
DEC_TB = 8
VMEM_LIMIT = 56 * 1024 * 1024

HIGHEST = lax.Precision.HIGHEST


def _cparams(sem):
    return pltpu.CompilerParams(dimension_semantics=sem, vmem_limit_bytes=VMEM_LIMIT)


def _dot(a, b, precision=None):
    return lax.dot_general(a, b, (((1,), (0,)), ((), ())), precision=precision, preferred_element_type=F32)


def _dot_nt(a, b, precision=None):
    return lax.dot_general(a, b, (((1,), (1,)), ((), ())), precision=precision, preferred_element_type=F32)


def _dot_tn(a, b, precision=None):
    return lax.dot_general(a, b, (((0,), (0,)), ((), ())), precision=precision, preferred_element_type=F32)


def _ln_rows(x, g, b):
    mu = jnp.mean(x, -1, keepdims=True)
    xc = x - mu
    var = jnp.mean(xc * xc, -1, keepdims=True)
    return xc * lax.rsqrt(var + LN_EPS) * g + b


def _ln_body(x_ref, g_ref, b_ref, of_ref, ob_ref):
    y = _ln_rows(x_ref[...], g_ref[...], b_ref[...])
    of_ref[...] = y
    ob_ref[...] = y.astype(BF16)


def _layernorm(x, g, b, tm):
    m, d = x.shape
    return pl.pallas_call(
        _ln_body,
        grid=(m // tm,),
        in_specs=[pl.BlockSpec((tm, d), lambda i: (i, 0)),
                  pl.BlockSpec((1, d), lambda i: (0, 0)),
                  pl.BlockSpec((1, d), lambda i: (0, 0))],
        out_specs=[pl.BlockSpec((tm, d), lambda i: (i, 0)),
                   pl.BlockSpec((tm, d), lambda i: (i, 0))],
        out_shape=[jax.ShapeDtypeStruct((m, d), F32), jax.ShapeDtypeStruct((m, d), BF16)],
        compiler_params=_cparams(("parallel",)),
        name="layernorm",
    )(x, g.reshape(1, d), b.reshape(1, d))


def _mm_body(x_ref, w_ref, o_ref):
    o_ref[...] = _dot(x_ref[...], w_ref[...]).astype(o_ref.dtype)


def _matmul(x, w, tm, tn, out_dtype=F32):
    m, k = x.shape
    n = w.shape[1]
    return pl.pallas_call(
        _mm_body,
        grid=(m // tm, n // tn),
        in_specs=[pl.BlockSpec((tm, k), lambda i, j: (i, 0)),
                  pl.BlockSpec((k, tn), lambda i, j: (0, j))],
        out_specs=pl.BlockSpec((tm, tn), lambda i, j: (i, j)),
        out_shape=jax.ShapeDtypeStruct((m, n), out_dtype),
        compiler_params=_cparams(("parallel", "parallel")),
        name="matmul",
    )(x, w)


def _swiglu_body(x_ref, wg_ref, wu_ref, o_ref):
    x = x_ref[...]
    g = _dot(x, wg_ref[...])
    u = _dot(x, wu_ref[...])
    o_ref[...] = (g * jax.nn.sigmoid(g) * u).astype(o_ref.dtype)


def _matmul_swiglu(x, wg, wu, tm, tn):
    m, k = x.shape
    n = wg.shape[1]
    return pl.pallas_call(
        _swiglu_body,
        grid=(m // tm, n // tn),
        in_specs=[pl.BlockSpec((tm, k), lambda i, j: (i, 0)),
                  pl.BlockSpec((k, tn), lambda i, j: (0, j)),
                  pl.BlockSpec((k, tn), lambda i, j: (0, j))],
        out_specs=pl.BlockSpec((tm, tn), lambda i, j: (i, j)),
        out_shape=jax.ShapeDtypeStruct((m, n), BF16),
        compiler_params=_cparams(("parallel", "parallel")),
        name="matmul_swiglu",
    )(x, wg, wu)


def _mm_res_ln_body(x_ref, w_ref, res_ref, g_ref, b_ref, of_ref, ob_ref, acc_ref, *, nk):
    kk = pl.program_id(1)

    @pl.when(kk == 0)
    def _():
        acc_ref[...] = jnp.zeros_like(acc_ref)

    acc_ref[...] += _dot(x_ref[...], w_ref[...])

    @pl.when(kk == nk - 1)
    def _():
        y = _ln_rows(ALPHA * res_ref[...] + acc_ref[...], g_ref[...], b_ref[...])
        of_ref[...] = y
        ob_ref[...] = y.astype(BF16)


def _matmul_res_ln(x, w, res, g, b, tm, tk):
    m, k = x.shape
    n = w.shape[1]
    nk = k // tk
    return pl.pallas_call(
        functools.partial(_mm_res_ln_body, nk=nk),
        grid=(m // tm, nk),
        in_specs=[pl.BlockSpec((tm, tk), lambda i, j: (i, j)),
                  pl.BlockSpec((tk, n), lambda i, j: (j, 0)),
                  pl.BlockSpec((tm, n), lambda i, j: (i, 0)),
                  pl.BlockSpec((1, n), lambda i, j: (0, 0)),
                  pl.BlockSpec((1, n), lambda i, j: (0, 0))],
        out_specs=[pl.BlockSpec((tm, n), lambda i, j: (i, 0)),
                   pl.BlockSpec((tm, n), lambda i, j: (i, 0))],
        out_shape=[jax.ShapeDtypeStruct((m, n), F32), jax.ShapeDtypeStruct((m, n), BF16)],
        scratch_shapes=[pltpu.VMEM((tm, n), F32)],
        compiler_params=_cparams(("parallel", "arbitrary")),
        name="matmul_res_ln",
    )(x, w, res, g.reshape(1, n), b.reshape(1, n))


def _rwkv_chunk_body(r_ref, lw_ref, k_ref, v_ref, kk_ref, a_ref, s0_ref, y_ref, sf_ref, s_scr, *, nc):
    L, N, H = RW_CHUNK, RW_HD, RW_HEADS
    c = pl.program_id(1)

    @pl.when(c == 0)
    def _():
        s_scr[...] = s0_ref[0]

    row = lax.broadcasted_iota(jnp.int32, (L, L), 0)
    col = lax.broadcasted_iota(jnp.int32, (L, L), 1)
    lower = row >= col
    strict = row > col
    tri = lower.astype(F32)
    eye_l = (row == col).astype(F32)
    rown = lax.broadcasted_iota(jnp.int32, (N, N), 0)
    coln = lax.broadcasted_iota(jnp.int32, (N, N), 1)
    eye_n = rown == coln
    dot = functools.partial(_dot, precision=HIGHEST)
    dot_nt = functools.partial(_dot_nt, precision=HIGHEST)
    dot_tn = functools.partial(_dot_tn, precision=HIGHEST)

    for h in range(H):
        sl = slice(h * N, (h + 1) * N)
        r = r_ref[0, :, sl]
        lw = lw_ref[0, :, sl]
        k = k_ref[0, :, sl]
        v = v_ref[0, :, sl]
        kk = kk_ref[0, :, sl]
        ap = kk * a_ref[0, :, sl]
        cum = dot(tri, lw)
        tot = cum[L - 1:L, :]
        e_neg = jnp.exp(-cum)
        kk_t = kk * jnp.exp(cum - lw)
        ap_h = ap * e_neg
        k_h = k * e_neg
        r_t = r * jnp.exp(cum)
        s0 = s_scr[h]
        n_mat = jnp.where(strict, dot_nt(kk_t, ap_h), 0.0)
        m_k = jnp.where(strict, dot_nt(kk_t, k_h), 0.0)
        m_r = jnp.where(lower, dot_nt(r_t, k_h), 0.0)
        m_a = jnp.where(lower, dot_nt(r_t, ap_h), 0.0)
        inv = eye_l - n_mat
        pw = n_mat
        for _ in range(int(math.log2(L)) - 1):
            pw = dot(pw, pw)
            inv = inv + dot(inv, pw)
        u = dot(inv, dot(kk_t, s0) + dot(m_k, v))
        y_ref[0, :, sl] = dot(r_t, s0) + dot(m_r, v) - dot(m_a, u)
        e_end = jnp.exp(tot - cum)
        decay = jnp.where(eye_n, jnp.exp(tot), 0.0)
        s_scr[h] = dot(decay, s0) + dot_tn(k * e_end, v) - dot_tn(ap * e_end, u)

    @pl.when(c == nc - 1)
    def _():
        sf_ref[0] = s_scr[...]


def _rwkv_chunked(r, lw, k, v, kk, a, s0):
    b, t, w = r.shape
    L = RW_CHUNK
    nc = t // L
    seq = pl.BlockSpec((1, L, w), lambda i, j: (i, j, 0))
    st = pl.BlockSpec((1, RW_HEADS, RW_HD, RW_HD), lambda i, j: (i, 0, 0, 0))
    return pl.pallas_call(
        functools.partial(_rwkv_chunk_body, nc=nc),
        grid=(b, nc),
        in_specs=[seq] * 6 + [st],
        out_specs=[seq, st],
        out_shape=[jax.ShapeDtypeStruct((b, t, w), F32),
                   jax.ShapeDtypeStruct((b, RW_HEADS, RW_HD, RW_HD), F32)],
        scratch_shapes=[pltpu.VMEM((RW_HEADS, RW_HD, RW_HD), F32)],
        compiler_params=_cparams(("parallel", "arbitrary")),
        name="rwkv_chunked",
    )(r, lw, k, v, kk, a, s0)


RW_TB = 256
RW_GH = 4
RW_GW = RW_GH * RW_HD
RW_VEC_ROWS = 8


def _split3(x):
    hi = x.astype(BF16)
    r1 = x - hi.astype(F32)
    mid = r1.astype(BF16)
    lo = (r1 - mid.astype(F32)).astype(BF16)
    return hi, mid, lo


def _mm(a, b, dims, passes):
    dg = lambda x, y: lax.dot_general(x, y, (dims, ((), ())), preferred_element_type=F32)
    if passes == 6:
        return lax.dot_general(a, b, (dims, ((), ())), precision=HIGHEST, preferred_element_type=F32)
    ah = a.astype(BF16)
    bh = b.astype(BF16)
    if passes == 1:
        return dg(ah, bh)
    al = (a - ah.astype(F32)).astype(BF16)
    bl = (b - bh.astype(F32)).astype(BF16)
    return dg(ah, bh) + (dg(ah, bl) + dg(al, bh))


_NN = ((1,), (0,))
_NT = ((1,), (1,))
_TN = ((0,), (0,))


def _exact_lhs_dot(a_bf16, b):
    hi, mid, lo = _split3(b)
    dg = lambda y: lax.dot_general(a_bf16, y, (_NN, ((), ())), preferred_element_type=F32)
    return dg(hi) + (dg(mid) + dg(lo))


def _exact_rhs_dot(a, b_bf16):
    hi, mid, lo = _split3(a)
    dg = lambda x: lax.dot_general(x, b_bf16, (_NN, ((), ())), preferred_element_type=F32)
    return dg(hi) + (dg(mid) + dg(lo))


def _rw_scan_block(r, lw, k, v, kk, a, st, masks, passes):
    L = RW_CHUNK
    tri, strict, lower, eye, mask_bd = masks
    p_sc, p_inv, p_app, p_st = passes
    bd = lambda x: jnp.where(mask_bd, jnp.concatenate([x] * RW_GH, axis=0), 0.0)
    cum = _exact_lhs_dot(tri, lw)
    tot = cum[L - 1:L, :]
    e_neg = jnp.exp(-cum)
    ap = kk * a
    ap_h = ap * e_neg
    k_h = k * e_neg
    lhs = jnp.concatenate([kk * jnp.exp(cum - lw), r * jnp.exp(cum)], axis=0)
    sc_a = _mm(lhs, bd(ap_h), _NT, p_sc)
    sc_k = _mm(lhs, bd(k_h), _NT, p_sc)
    n_m = jnp.where(strict, sc_a[:L], 0.0)
    m_a = jnp.where(lower, sc_a[L:], 0.0)
    m_k = jnp.where(strict, sc_k[:L], 0.0)
    m_r = jnp.where(lower, sc_k[L:], 0.0)
    inv = eye - n_m
    pw = n_m
    for _ in range(int(math.log2(L)) - 1):
        pw = _mm(pw, bd(pw), _NN, p_inv)
        inv = inv + _mm(inv, bd(pw), _NN, p_inv)
    s_terms = _mm(lhs, st, _NT, p_app)
    mv = _mm(jnp.concatenate([m_k, m_r], axis=0), bd(v), _NN, p_app)
    u = _mm(inv, bd(s_terms[:L] + mv[:L]), _NN, p_app)
    y = s_terms[L:] + mv[L:] - _mm(m_a, bd(u), _NN, p_app)
    e_end = jnp.exp(tot - cum)
    upd = _mm(jnp.concatenate([v, -u], axis=0), jnp.concatenate([k * e_end, ap * e_end], axis=0), _TN, p_st)
    st_new = jnp.where(mask_bd, st * jnp.exp(tot) + upd, 0.0)
    return y, st_new


def _rw_masks():
    L, G = RW_CHUNK, RW_GW
    row = lax.broadcasted_iota(jnp.int32, (L, G), 0)
    col = lax.broadcasted_iota(jnp.int32, (L, G), 1) & (L - 1)
    rl = lax.broadcasted_iota(jnp.int32, (L, L), 0)
    cl = lax.broadcasted_iota(jnp.int32, (L, L), 1)
    rg = lax.broadcasted_iota(jnp.int32, (G, G), 0) // RW_HD
    cg = lax.broadcasted_iota(jnp.int32, (G, G), 1) // RW_HD
    tri = (rl >= cl).astype(BF16)
    return tri, row > col, row >= col, (row == col).astype(F32), rg == cg


def _softplus(z):
    return jnp.maximum(z, 0.0) + jnp.log(1.0 + jnp.exp(-jnp.abs(z)))


def _rwkv_fused_body(*refs, nc, has_vres, passes):
    if has_vres:
        (p_ref, pv_ref, vf_ref, mu_ref, vec_ref, wa_ref, g2_ref, seg_ref, vmu_ref, v2_ref,
         o_ref, sf_ref, st_scr, prev_scr, y_scr, prevv_scr) = refs
    else:
        (p_ref, mu_ref, vec_ref, wa_ref, g2_ref, seg_ref,
         o_ref, vfo_ref, sf_ref, st_scr, prev_scr, y_scr) = refs
    TB, W, L = RW_TB, RW_W, RW_CHUNK
    c = pl.program_id(1)

    @pl.when(c == 0)
    def _():
        st_scr[...] = jnp.zeros_like(st_scr)
        prev_scr[...] = jnp.zeros_like(prev_scr)
        if has_vres:
            prevv_scr[...] = jnp.zeros_like(prevv_scr)

    first_row = lax.broadcasted_iota(jnp.int32, (TB, 1), 0) == 0

    def shift_mix(x, carry_ref, mu):
        prev = jnp.where(first_row, carry_ref[...], pltpu.roll(x, 1, 0))
        carry_ref[...] = x[TB - 1:TB, :]
        return x + (prev - x) * mu

    mixed = shift_mix(p_ref[0], prev_scr, mu_ref[...])
    r = mixed[:, 0:W]
    k = mixed[:, W:2 * W]
    v = mixed[:, 2 * W:3 * W]
    xwa = mixed[:, 3 * W:3 * W + LANES]
    xg = mixed[:, 3 * W + LANES:3 * W + 2 * LANES]
    vec = vec_ref[...]
    w0, a0, kk_s, ka, rk, lnx_g, lnx_b, v0 = (vec[i:i + 1, :] for i in range(RW_VEC_ROWS))
    seg = seg_ref[...]
    wa = wa_ref[...]
    w_lora = _dot(jnp.tanh(xwa).astype(BF16), wa[:, 0:W])
    a_lora = _dot(xwa.astype(BF16), wa[:, W:2 * W])
    lw = -jnp.exp(-_softplus(-(w0 + w_lora)) - 0.5)
    a = jax.nn.sigmoid(a0 + a_lora)
    g = _dot(jax.nn.sigmoid(xg).astype(BF16), g2_ref[...])
    if has_vres:
        xv = shift_mix(pv_ref[0], prevv_scr, vmu_ref[...])
        v = v + (vf_ref[0] - v) * jax.nn.sigmoid(v0 + _dot(xv.astype(BF16), v2_ref[...]))
    else:
        vfo_ref[0] = v
    kk = k * kk_s
    kk = kk * lax.rsqrt(jnp.maximum(_exact_rhs_dot(kk * kk, seg), 1e-24))
    k = k * (1.0 + (a - 1.0) * ka)

    masks = _rw_masks()
    for s in range(TB // L):
        rs = slice(s * L, (s + 1) * L)
        for gi in range(RW_HEADS // RW_GH):
            cs = slice(gi * RW_GW, (gi + 1) * RW_GW)
            y, st_new = _rw_scan_block(r[rs, cs], lw[rs, cs], k[rs, cs], v[rs, cs], kk[rs, cs], a[rs, cs],
                                       st_scr[gi], masks, passes)
            y_scr[rs, cs] = y
            st_scr[gi] = st_new

    y = y_scr[...]
    inv_n = 1.0 / RW_HD
    y_mu = _exact_rhs_dot(y, seg) * inv_n
    yc = y - y_mu
    y_var = _exact_rhs_dot(yc * yc, seg) * inv_n
    y = yc * lax.rsqrt(y_var + RW_GN_EPS) * lnx_g + lnx_b
    bonus = _exact_rhs_dot(r * k * rk, seg) * v
    o_ref[0] = ((y + bonus) * g).astype(BF16)

    @pl.when(c == nc - 1)
    def _():
        sf_ref[0] = st_scr[...]


def _rwkv_prompt(p3, lp, v_first, passes=(3, 3, 3, 3)):
    b, t, _ = p3.shape
    nc = t // RW_TB
    has_vres = v_first is not None
    ng = RW_HEADS // RW_GH
    zpad = jnp.zeros((RW_LORA_W, RW_W), F32)
    wa = jnp.concatenate([jnp.concatenate([lp['rw_w2'], zpad], 0), jnp.concatenate([zpad, lp['rw_a2']], 0)], 1)
    vec = jnp.stack([lp['rw_w0'], lp['rw_a0'], lp['rw_kk'], lp['rw_ka'], lp['rw_rk'], lp['rw_lnx_g'], lp['rw_lnx_b'],
                     lp['rw_v0'] if has_vres else jnp.zeros((RW_W,), F32)])
    hid = jnp.arange(RW_W) // RW_HD
    seg = (hid[:, None] == hid[None, :]).astype(BF16)
    full = lambda shape: pl.BlockSpec(shape, lambda i, j: (0,) * len(shape))
    seq = lambda w, blk: pl.BlockSpec((1, RW_TB, w), lambda i, j: (i, j, blk))
    in_specs = [seq(RW_P, 0)]
    args = [p3]
    if has_vres:
        in_specs += [seq(LANES, P_V1 // LANES), seq(RW_W, 0)]
        args += [p3, v_first]
    in_specs += [full((1, RW_P)), full((RW_VEC_ROWS, RW_W)), full((LANES, 2 * RW_W)), full((RW_LORA_G, RW_W)),
                 full((RW_W, RW_W))]
    args += [lp['rw_mu'].reshape(1, RW_P), vec, wa.astype(BF16), lp['rw_g2'].astype(BF16), seg]
    if has_vres:
        in_specs += [full((1, LANES)), full((LANES, RW_W))]
        args += [jnp.pad(lp['rw_vmu'], (0, LANES - RW_LORA_V)).reshape(1, LANES),
                 jnp.pad(lp['rw_v2'], ((0, LANES - RW_LORA_V), (0, 0))).astype(BF16)]
    out_specs = [seq(RW_W, 0)]
    out_shape = [jax.ShapeDtypeStruct((b, t, RW_W), BF16)]
    if not has_vres:
        out_specs.append(seq(RW_W, 0))
        out_shape.append(jax.ShapeDtypeStruct((b, t, RW_W), F32))
    out_specs.append(pl.BlockSpec((1, ng, RW_GW, RW_GW), lambda i, j: (i, 0, 0, 0)))
    out_shape.append(jax.ShapeDtypeStruct((b, ng, RW_GW, RW_GW), F32))
    scratch = [pltpu.VMEM((ng, RW_GW, RW_GW), F32), pltpu.VMEM((1, RW_P), F32), pltpu.VMEM((RW_TB, RW_W), F32)]
    if has_vres:
        scratch.append(pltpu.VMEM((1, LANES), F32))
    outs = pl.pallas_call(
        functools.partial(_rwkv_fused_body, nc=nc, has_vres=has_vres, passes=passes),
        grid=(b, nc),
        in_specs=in_specs,
        out_specs=out_specs,
        out_shape=out_shape,
        scratch_shapes=scratch,
        compiler_params=_cparams(("parallel", "arbitrary")),
        name="rwkv_fused",
    )(*args)
    if has_vres:
        o, st_bd = outs
    else:
        o, v_first, st_bd = outs
    st5 = st_bd.reshape(b, ng, RW_GH, RW_HD, RW_GH, RW_HD)
    s_fin = jnp.stack([st5[:, :, h, :, h, :] for h in range(RW_GH)], axis=2)
    s_fin = s_fin.reshape(b, RW_HEADS, RW_HD, RW_HD).transpose(0, 1, 3, 2)
    return o, v_first, s_fin


def _ret_log_gamma(h):
    return math.log1p(-(2.0 ** (-5.0 - h)))


def _ret_chunk_body(q_ref, k_ref, v_ref, s0_ref, y_ref, sf_ref, s_scr, *, nc):
    L, DK, DV, H = CHUNK, RET_QK, RET_V, RET_HEADS
    c = pl.program_id(1)

    @pl.when(c == 0)
    def _():
        s_scr[...] = s0_ref[0]

    row = lax.broadcasted_iota(jnp.int32, (L, L), 0)
    col = lax.broadcasted_iota(jnp.int32, (L, L), 1)
    rel = (row - col).astype(F32)
    idx = lax.broadcasted_iota(jnp.int32, (L, 1), 0).astype(F32)
    for h in range(H):
        lg = _ret_log_gamma(h)
        q = q_ref[0, :, h * DK:(h + 1) * DK]
        k = k_ref[0, :, h * DK:(h + 1) * DK]
        v = v_ref[0, :, h * DV:(h + 1) * DV].astype(BF16)
        dmask = jnp.where(rel >= 0, jnp.exp(jnp.maximum(rel, 0.0) * lg), 0.0)
        scores = _dot_nt(q.astype(BF16), k.astype(BF16)) * dmask
        s_prev = s_scr[h]
        q_dec = q * jnp.exp((idx + 1.0) * lg)
        y_ref[0, :, h * DV:(h + 1) * DV] = (_dot(scores.astype(BF16), v)
                                            + _dot(q_dec.astype(BF16), s_prev.astype(BF16)))
        k_end = k * jnp.exp((L - 1.0 - idx) * lg)
        s_scr[h] = math.exp(L * lg) * s_prev + _dot_tn(k_end.astype(BF16), v)

    @pl.when(c == nc - 1)
    def _():
        sf_ref[0] = s_scr[...]


def _ret_chunked(q, k, v, s0):
    b, t, _ = q.shape
    L = CHUNK
    nc = t // L
    qk = pl.BlockSpec((1, L, RET_HEADS * RET_QK), lambda i, j: (i, j, 0))
    vs = pl.BlockSpec((1, L, RET_W), lambda i, j: (i, j, 0))
    st = pl.BlockSpec((1, RET_HEADS, RET_QK, RET_V), lambda i, j: (i, 0, 0, 0))
    return pl.pallas_call(
        functools.partial(_ret_chunk_body, nc=nc),
        grid=(b, nc),
        in_specs=[qk, qk, vs, st],
        out_specs=[vs, st],
        out_shape=[jax.ShapeDtypeStruct((b, t, RET_W), F32),
                   jax.ShapeDtypeStruct((b, RET_HEADS, RET_QK, RET_V), F32)],
        scratch_shapes=[pltpu.VMEM((RET_HEADS, RET_QK, RET_V), F32)],
        compiler_params=_cparams(("parallel", "arbitrary")),
        name="ret_chunked",
    )(q, k, v, s0)


ML_HPAD = 8


def _ml_chunk_body(q_ref, k_ref, v_ref, g_ref, c0_ref, n0_ref, m0_ref,
                   h_ref, cf_ref, nf_ref, mf_ref, c_scr, n_scr, m_scr, *, nc):
    L, DK, DV, H = CHUNK, ML_QK, ML_V, ML_HEADS
    ci = pl.program_id(1)

    @pl.when(ci == 0)
    def _():
        c_scr[...] = c0_ref[0]
        n_scr[...] = n0_ref[0]
        m_scr[...] = m0_ref[0]

    row = lax.broadcasted_iota(jnp.int32, (L, L), 0)
    col = lax.broadcasted_iota(jnp.int32, (L, L), 1)
    causal = row >= col
    tri = causal.astype(F32)
    g = g_ref[0]
    cum = _dot(tri, g, precision=HIGHEST)
    g_t = g.T
    cum_t = cum.T
    for h in range(H):
        q = q_ref[0, :, h * DK:(h + 1) * DK]
        k = k_ref[0, :, h * DK:(h + 1) * DK]
        v = v_ref[0, :, h * DV:(h + 1) * DV].astype(BF16)
        ig_col = g[:, h:h + 1]
        ig_row = g_t[h:h + 1, :]
        b_col = cum[:, H + h:H + h + 1]
        b_row = cum_t[H + h:H + h + 1, :]
        b_tot = cum[L - 1:L, H + h:H + h + 1]
        m_prev = m_scr[h:h + 1, 0:1]
        c_prev = c_scr[h]
        n_prev = n_scr[h:h + 1, :]
        a_row = b_tot - b_row + ig_row
        a_col = b_tot - b_col + ig_col
        m_new = jnp.maximum(b_tot + m_prev, jnp.max(a_row, axis=1, keepdims=True))
        dec = jnp.exp(b_tot + m_prev - m_new)
        kw = k * jnp.exp(a_col - m_new)
        c_scr[h] = dec * c_prev + _dot_tn(kw.astype(BF16), v)
        n_scr[h:h + 1, :] = dec * n_prev + jnp.sum(kw, axis=0, keepdims=True)
        m_scr[h:h + 1, :] = jnp.broadcast_to(m_new, (1, LANES))
        dlog = jnp.where(causal, b_col - b_row + ig_row, -jnp.inf)
        inter = b_col + m_prev
        m_i = jnp.maximum(jnp.max(dlog, axis=1, keepdims=True), inter)
        s = _dot_nt(q.astype(BF16), k.astype(BF16)) * jnp.exp(dlog - m_i)
        sc = jnp.exp(inter - m_i)
        num = _dot(s.astype(BF16), v) + sc * _dot(q.astype(BF16), c_prev.astype(BF16))
        den = jnp.sum(s, axis=1, keepdims=True) + sc * jnp.sum(q * n_prev, axis=1, keepdims=True)
        h_ref[0, :, h * DV:(h + 1) * DV] = num / jnp.maximum(jnp.abs(den), jnp.exp(-m_i))

    @pl.when(ci == nc - 1)
    def _():
        cf_ref[0] = c_scr[...]
        nf_ref[0] = n_scr[...]
        mf_ref[0] = m_scr[...]


def _ml_chunked(q, k, v, g, c0, n0, m0):
    b, t, _ = q.shape
    L = CHUNK
    nc = t // L
    qk = pl.BlockSpec((1, L, ML_HEADS * ML_QK), lambda i, j: (i, j, 0))
    vs = pl.BlockSpec((1, L, ML_W), lambda i, j: (i, j, 0))
    gs = pl.BlockSpec((1, L, LANES), lambda i, j: (i, j, 0))
    cs = pl.BlockSpec((1, ML_HEADS, ML_QK, ML_V), lambda i, j: (i, 0, 0, 0))
    ns = pl.BlockSpec((1, ML_HPAD, ML_QK), lambda i, j: (i, 0, 0))
    ms = pl.BlockSpec((1, ML_HPAD, LANES), lambda i, j: (i, 0, 0))
    return pl.pallas_call(
        functools.partial(_ml_chunk_body, nc=nc),
        grid=(b, nc),
        in_specs=[qk, qk, vs, gs, cs, ns, ms],
        out_specs=[vs, cs, ns, ms],
        out_shape=[jax.ShapeDtypeStruct((b, t, ML_W), F32),
                   jax.ShapeDtypeStruct((b, ML_HEADS, ML_QK, ML_V), F32),
                   jax.ShapeDtypeStruct((b, ML_HPAD, ML_QK), F32),
                   jax.ShapeDtypeStruct((b, ML_HPAD, LANES), F32)],
        scratch_shapes=[pltpu.VMEM((ML_HEADS, ML_QK, ML_V), F32),
                        pltpu.VMEM((ML_HPAD, ML_QK), F32),
                        pltpu.VMEM((ML_HPAD, LANES), F32)],
        compiler_params=_cparams(("parallel", "arbitrary")),
        name="ml_chunked",
    )(q, k, v, g, c0, n0, m0)


def _to_cols(x):
    b, c = x.shape
    return x.reshape(b // DEC_TB, DEC_TB, c).transpose(0, 2, 1)


def _rwkv_step_body(w_ref, ap_ref, k_ref, kk_ref, r_ref, v_ref, s_ref, y_ref, so_ref):
    N = RW_HD
    for j in range(DEC_TB):
        for h in range(RW_HEADS):
            cs = slice(h * N, (h + 1) * N)
            col = lambda ref: ref[0, cs, j:j + 1]
            s = s_ref[j, h]
            v = v_ref[j:j + 1, cs]
            sa = jnp.sum(col(kk_ref) * s, axis=0, keepdims=True)
            s_new = col(w_ref) * s - col(ap_ref) * sa + col(k_ref) * v
            so_ref[j, h] = s_new
            y_ref[j:j + 1, cs] = jnp.sum(col(r_ref) * s_new, axis=0, keepdims=True)


def _rwkv_step(wdec, ap, k, kk, r, v, s0):
    b, w = v.shape
    cols = pl.BlockSpec((1, w, DEC_TB), lambda i: (i, 0, 0))
    rows = pl.BlockSpec((DEC_TB, w), lambda i: (i, 0))
    st = pl.BlockSpec((DEC_TB, RW_HEADS, RW_HD, RW_HD), lambda i: (i, 0, 0, 0))
    return pl.pallas_call(
        _rwkv_step_body,
        grid=(b // DEC_TB,),
        in_specs=[cols] * 5 + [rows, st],
        out_specs=[rows, st],
        out_shape=[jax.ShapeDtypeStruct((b, w), F32), jax.ShapeDtypeStruct(s0.shape, F32)],
        compiler_params=_cparams(("parallel",)),
        name="rwkv_step",
    )(_to_cols(wdec), _to_cols(ap), _to_cols(k), _to_cols(kk), _to_cols(r), v, s0)


def _ret_step_body(qc_ref, kc_ref, v_ref, s_ref, y_ref, so_ref):
    DK, DV = RET_QK, RET_V
    for j in range(DEC_TB):
        for h in range(RET_HEADS):
            gamma = math.exp(_ret_log_gamma(h))
            q = qc_ref[0, h * DK:(h + 1) * DK, j:j + 1]
            k = kc_ref[0, h * DK:(h + 1) * DK, j:j + 1]
            v = v_ref[j:j + 1, h * DV:(h + 1) * DV]
            s = s_ref[j, h]
            qk = jnp.sum(q * k, axis=0, keepdims=True)
            y_ref[j:j + 1, h * DV:(h + 1) * DV] = qk * v + gamma * jnp.sum(q * s, axis=0, keepdims=True)
            so_ref[j, h] = gamma * s + k * v


def _ret_step(q, k, v, s0):
    b = q.shape[0]
    cols = pl.BlockSpec((1, RET_HEADS * RET_QK, DEC_TB), lambda i: (i, 0, 0))
    rows = pl.BlockSpec((DEC_TB, RET_W), lambda i: (i, 0))
    st = pl.BlockSpec((DEC_TB, RET_HEADS, RET_QK, RET_V), lambda i: (i, 0, 0, 0))
    return pl.pallas_call(
        _ret_step_body,
        grid=(b // DEC_TB,),
        in_specs=[cols, cols, rows, st],
        out_specs=[rows, st],
        out_shape=[jax.ShapeDtypeStruct((b, RET_W), F32), jax.ShapeDtypeStruct(s0.shape, F32)],
        compiler_params=_cparams(("parallel",)),
        name="ret_step",
    )(_to_cols(q), _to_cols(k), v, s0)


def _ml_step_body(qc_ref, kc_ref, q_ref, k_ref, v_ref, ig_ref, lf_ref, c_ref, n_ref, m_ref,
                  h_ref, co_ref, no_ref, mo_ref):
    DK, DV = ML_QK, ML_V
    for j in range(DEC_TB):
        for h in range(ML_HEADS):
            ks = slice(h * DK, (h + 1) * DK)
            vs = slice(h * DV, (h + 1) * DV)
            q_col = qc_ref[0, ks, j:j + 1]
            k_col = kc_ref[0, ks, j:j + 1]
            q_row = q_ref[j:j + 1, ks]
            k_row = k_ref[j:j + 1, ks]
            v = v_ref[j:j + 1, vs]
            ig = ig_ref[j:j + 1, h:h + 1]
            lf = lf_ref[j:j + 1, h:h + 1]
            m_prev = m_ref[j:j + 1, h:h + 1]
            c_prev = c_ref[j, h]
            n_prev = n_ref[j, h:h + 1, :]
            m_new = jnp.maximum(lf + m_prev, ig)
            dec = jnp.exp(lf + m_prev - m_new)
            wgt = jnp.exp(ig - m_new)
            co_ref[j, h] = dec * c_prev + (k_col * wgt) * v
            no_ref[j, h:h + 1, :] = dec * n_prev + k_row * wgt
            mo_ref[j:j + 1, h:h + 1] = m_new
            s = jnp.sum(q_row * k_row, axis=1, keepdims=True) * wgt
            num = s * v + dec * jnp.sum(q_col * c_prev, axis=0, keepdims=True)
            den = s + dec * jnp.sum(q_row * n_prev, axis=1, keepdims=True)
            h_ref[j:j + 1, vs] = num / jnp.maximum(jnp.abs(den), jnp.exp(-m_new))


def _ml_step(q, k, v, ig, lf, c0, n0, m0):
    b = q.shape[0]
    cols = pl.BlockSpec((1, ML_HEADS * ML_QK, DEC_TB), lambda i: (i, 0, 0))
    qk_rows = pl.BlockSpec((DEC_TB, ML_HEADS * ML_QK), lambda i: (i, 0))
    rows = pl.BlockSpec((DEC_TB, ML_W), lambda i: (i, 0))
    sc = pl.BlockSpec((DEC_TB, ML_HEADS), lambda i: (i, 0))
    cs = pl.BlockSpec((DEC_TB, ML_HEADS, ML_QK, ML_V), lambda i: (i, 0, 0, 0))
    ns = pl.BlockSpec((DEC_TB, ML_HEADS, ML_QK), lambda i: (i, 0, 0))
    return pl.pallas_call(
        _ml_step_body,
        grid=(b // DEC_TB,),
        in_specs=[cols, cols, qk_rows, qk_rows, rows, sc, sc, cs, ns, sc],
        out_specs=[rows, cs, ns, sc],
        out_shape=[jax.ShapeDtypeStruct((b, ML_W), F32), jax.ShapeDtypeStruct(c0.shape, F32),
                   jax.ShapeDtypeStruct(n0.shape, F32), jax.ShapeDtypeStruct(m0.shape, F32)],
        compiler_params=_cparams(("parallel",)),
        name="ml_step",
    )(_to_cols(q), _to_cols(k), q, k, v, ig, lf, c0, n0, m0)


def _heads(a, h):
    return a.reshape(a.shape[:-1] + (h, a.shape[-1] // h))


def _shift_prev(p, prev_row):
    return jnp.concatenate([prev_row[:, None, :], p[:, :-1]], axis=1)


def _rotary(x, pos):
    half = x.shape[-1] // 2
    inv = ROPE_BASE ** (-jnp.arange(half, dtype=F32) / half)
    ang = pos.astype(F32)[:, None] * inv[None, :]
    cos = jnp.cos(ang)[None, :, None, :]
    sin = jnp.sin(ang)[None, :, None, :]
    x1, x2 = x[..., :half], x[..., half:]
    return jnp.concatenate([x1 * cos - x2 * sin, x1 * sin + x2 * cos], -1)


def _small_matmul(x, w):
    lead = x.shape[:-1]
    kdim, n = w.shape
    x2 = x.reshape(-1, kdim)
    m = x2.shape[0]
    kp = -(-kdim // LANES) * LANES
    npad = -(-n // LANES) * LANES
    x2 = jnp.pad(x2.astype(BF16), ((0, 0), (0, kp - kdim)))
    w2 = jnp.pad(w.astype(BF16), ((0, kp - kdim), (0, npad - n)))
    tm = 1024 if m % 1024 == 0 else m
    out = _matmul(x2, w2, tm, npad)
    return out[:, :n].reshape(lead + (n,))


def _mix_group(p, pos, v_first, st, lp, prev_row, chunked):
    bsz, t_len, _ = p.shape
    p_ret = p[..., RW_P:RW_P + RET_P]
    p_ml = p[..., RW_P + RET_P:P_TOTAL]

    if chunked:
        o_rw, v_first, s_new = _rwkv_prompt(p, lp, v_first, (1, 1, 1, 1))
    else:
        p_rw = p[..., :RW_P]
        mixed = p_rw + (_shift_prev(p_rw, prev_row[:, :RW_P]) - p_rw) * lp['rw_mu']
        sizes = np.cumsum([RW_W, RW_W, RW_W, RW_LORA_W, RW_LORA_A, RW_LORA_G])[:-1]
        r, k, v, xw, xa, xg = jnp.split(mixed, [int(s) for s in sizes], axis=-1)
        w = -jax.nn.softplus(-(lp['rw_w0'] + _small_matmul(jnp.tanh(xw), lp['rw_w2']))) - 0.5
        a = jax.nn.sigmoid(lp['rw_a0'] + _small_matmul(xa, lp['rw_a2']))
        g = _small_matmul(jax.nn.sigmoid(xg), lp['rw_g2'])
        if v_first is None:
            v_first = v
        else:
            pv = p[..., P_V1:P_V1 + RW_LORA_V]
            xv = pv + (_shift_prev(pv, prev_row[:, P_V1:P_V1 + RW_LORA_V]) - pv) * lp['rw_vmu']
            v = v + (v_first - v) * jax.nn.sigmoid(lp['rw_v0'] + _small_matmul(xv, lp['rw_v2']))
        kk = _heads(k * lp['rw_kk'], RW_HEADS)
        kk = kk * lax.rsqrt(jnp.maximum(jnp.sum(jnp.square(kk), -1, keepdims=True), 1e-24))
        kk = kk.reshape(bsz, t_len, RW_W)
        k = k * (1.0 + (a - 1.0) * lp['rw_ka'])
        lw = -jnp.exp(w)
        y, s_new = _rwkv_step(jnp.exp(lw)[:, 0], (kk * a)[:, 0], k[:, 0], kk[:, 0], r[:, 0], v[:, 0], st['rw_wkv'])
        y = _heads(y[:, None, :], RW_HEADS)
        y_mu = jnp.mean(y, -1, keepdims=True)
        y_var = jnp.mean(jnp.square(y - y_mu), -1, keepdims=True)
        y = ((y - y_mu) * lax.rsqrt(y_var + RW_GN_EPS)).reshape(bsz, t_len, RW_W)
        y = y * lp['rw_lnx_g'] + lp['rw_lnx_b']
        rh, kh, vh = (_heads(u, RW_HEADS) for u in (r, k, v))
        bonus = jnp.sum(rh * kh * _heads(lp['rw_rk'], RW_HEADS), -1, keepdims=True) * vh
        o_rw = ((y + bonus.reshape(bsz, t_len, RW_W)) * g).astype(BF16)

    nqk = RET_HEADS * RET_QK
    qr, kr, vr, gr = (p_ret[..., :nqk], p_ret[..., nqk:2 * nqk],
                      p_ret[..., 2 * nqk:2 * nqk + RET_W], p_ret[..., 2 * nqk + RET_W:])
    qh = _rotary(_heads(qr, RET_HEADS), pos).reshape(bsz, t_len, nqk)
    khr = (_rotary(_heads(kr, RET_HEADS), pos) * (RET_QK ** -0.5)).reshape(bsz, t_len, nqk)
    if chunked:
        yr, r_new = _ret_chunked(qh, khr, vr, st['ret'])
    else:
        yr, r_new = _ret_step(qh[:, 0], khr[:, 0], vr[:, 0], st['ret'])
        yr = yr[:, None, :]
    yr = _heads(yr, RET_HEADS)
    yr = yr * lax.rsqrt(jnp.mean(jnp.square(yr), -1, keepdims=True) + NORM_EPS)
    o_ret = jax.nn.silu(gr) * yr.reshape(bsz, t_len, RET_W)

    nqk = ML_HEADS * ML_QK
    qm, km, vm, om = (p_ml[..., :nqk], p_ml[..., nqk:2 * nqk],
                      p_ml[..., 2 * nqk:2 * nqk + ML_W], p_ml[..., 2 * nqk + ML_W:2 * nqk + 2 * ML_W])
    im = p_ml[..., 2 * nqk + 2 * ML_W:2 * nqk + 2 * ML_W + ML_HEADS]
    fm = p_ml[..., 2 * nqk + 2 * ML_W + ML_HEADS:]
    ig = ML_GATE_CAP * jnp.tanh((im + lp['ml_ib']) / ML_GATE_CAP)
    lf = jax.nn.log_sigmoid(ML_GATE_CAP * jnp.tanh((fm + lp['ml_fb']) / ML_GATE_CAP))
    km = km * (ML_QK ** -0.5)
    if chunked:
        gates = jnp.pad(jnp.concatenate([ig, lf], -1), ((0, 0), (0, 0), (0, LANES - 2 * ML_HEADS)))
        n0 = jnp.pad(st['ml_n'], ((0, 0), (0, ML_HPAD - ML_HEADS), (0, 0)))
        m0 = jnp.pad(st['ml_m'][:, :, None], ((0, 0), (0, ML_HPAD - ML_HEADS), (0, LANES - 1)))
        hm, c_new, n_new, m_new = _ml_chunked(qm, km, vm, gates, st['ml_c'], n0, m0)
        n_new = n_new[:, :ML_HEADS]
        m_new = m_new[:, :ML_HEADS, 0]
    else:
        hm, c_new, n_new, m_new = _ml_step(qm[:, 0], km[:, 0], vm[:, 0], ig[:, 0], lf[:, 0],
                                           st['ml_c'], st['ml_n'], st['ml_m'])
        hm = hm[:, None, :]
    hm = _heads(hm, ML_HEADS)
    hm = hm * lax.rsqrt(jnp.mean(jnp.square(hm), -1, keepdims=True) + NORM_EPS)
    o_ml = jax.nn.sigmoid(om) * (hm.reshape(bsz, t_len, ML_W) * lp['ml_norm'])

    o = jnp.concatenate([o_rw, o_ret.astype(BF16), o_ml.astype(BF16)], -1)
    return o, v_first, (s_new, r_new, c_new, n_new, m_new)


def _token_tiles(m):
    if m % 2048 == 0:
        return 2048, 512, 512
    return m, m, m


def kernel(x_prompt, x_sample, state_rw_shift, state_rw_wkv, state_ret, state_ml_c, state_ml_n, state_ml_m,
           ln0_g, ln0_b, w_in, rw_mu, rw_w0, rw_w2, rw_a0, rw_a2, rw_g2, rw_kk, rw_ka, rw_rk,
           rw_lnx_g, rw_lnx_b, rw_v0, rw_v1, rw_vmu, rw_v2, ml_ib, ml_fb, ml_norm, w_out,
           ln1_g, ln1_b, w_gate, w_up, w_down, ln2_g, ln2_b):
    bp, tp, d = x_prompt.shape
    bs, ts, _ = x_sample.shape
    groups = {
        'p': dict(b=bp, t=tp, pos=jnp.arange(tp), chunked=True),
        's': dict(b=bs, t=ts, pos=PAST_LEN + jnp.arange(ts), chunked=False),
    }
    xs_f, xs_b, v_first, outs = {}, {}, {}, {}
    for name, x in (('p', x_prompt), ('s', x_sample)):
        m = x.shape[0] * x.shape[1]
        xs_f[name], xs_b[name] = _layernorm(x.reshape(m, d), ln0_g, ln0_b, _token_tiles(m)[1])
        v_first[name] = None
        outs[name] = []

    for l in range(DEPTH):
        lp = {
            'rw_mu': rw_mu[l], 'rw_w0': rw_w0[l], 'rw_w2': rw_w2[l], 'rw_a0': rw_a0[l], 'rw_a2': rw_a2[l],
            'rw_g2': rw_g2[l], 'rw_kk': rw_kk[l], 'rw_ka': rw_ka[l], 'rw_rk': rw_rk[l],
            'rw_lnx_g': rw_lnx_g[l], 'rw_lnx_b': rw_lnx_b[l], 'ml_ib': ml_ib[l], 'ml_fb': ml_fb[l],
            'ml_norm': ml_norm[l],
        }
        w_in_b = jnp.pad(w_in[l].astype(BF16), ((0, 0), (0, P_PAD - P_TOTAL)))
        if l > 0:
            lp.update(rw_v0=rw_v0[l - 1], rw_vmu=rw_vmu[l - 1], rw_v2=rw_v2[l - 1])
            w_in_b = lax.dynamic_update_slice(w_in_b, rw_v1[l - 1].astype(BF16), (0, P_V1))
        w_out_b = w_out[l].astype(BF16)
        w_gate_b = w_gate[l].astype(BF16)
        w_up_b = w_up[l].astype(BF16)
        w_down_b = w_down[l].astype(BF16)
        prev_s = _matmul(state_rw_shift[l].astype(BF16), w_in_b, bs, 512)
        for name, grp in groups.items():
            b, t = grp['b'], grp['t']
            m = b * t
            tm_big, tm_out, tm_down = _token_tiles(m)
            p = _matmul(xs_b[name], w_in_b, tm_big, 512).reshape(b, t, P_PAD)
            if grp['chunked']:
                st = {
                    'ret': jnp.zeros((b, RET_HEADS, RET_QK, RET_V), F32),
                    'ml_c': jnp.zeros((b, ML_HEADS, ML_QK, ML_V), F32),
                    'ml_n': jnp.zeros((b, ML_HEADS, ML_QK), F32),
                    'ml_m': jnp.zeros((b, ML_HEADS), F32),
                }
                prev_row = None
            else:
                st = {'rw_wkv': state_rw_wkv[l], 'ret': state_ret[l], 'ml_c': state_ml_c[l],
                      'ml_n': state_ml_n[l], 'ml_m': state_ml_m[l]}
                prev_row = prev_s
            o, v_first[name], new_st = _mix_group(p, grp['pos'], v_first[name], st, lp, prev_row, grp['chunked'])
            outs[name].append((xs_f[name].reshape(b, t, d)[:, -1],) + new_st)
            x1_f, x1_b = _matmul_res_ln(o.reshape(m, d), w_out_b, xs_f[name], ln1_g[l], ln1_b[l], tm_out, d)
            hdn = _matmul_swiglu(x1_b, w_gate_b, w_up_b, tm_big, 512)
            xs_f[name], xs_b[name] = _matmul_res_ln(hdn, w_down_b, x1_f, ln2_g[l], ln2_b[l], tm_down, 512)

    y_p = xs_f['p'].reshape(bp, tp, d)
    y_s = xs_f['s'].reshape(bs, ts, d)
    sp = [jnp.stack([o[i] for o in outs['p']]) for i in range(6)]
    ss = [jnp.stack([o[i] for o in outs['s']]) for i in range(6)]
    return (y_p, y_s, sp[0], sp[1], sp[2], sp[3], sp[4], sp[5], ss[0], ss[1], ss[2], ss[3], ss[4], ss[5])
```

```python
import functools
import math

import numpy as np
import jax
import jax.numpy as jnp
from jax import lax
from jax.experimental import pallas as pl
from jax.experimental.pallas import tpu as pltpu

F32 = jnp.float32
BF16 = jnp.bfloat16

D_MODEL = 2048
DEPTH = 2
PAST_LEN = 16384
RW_HD = 64
RW_W = D_MODEL // 4
RW_HEADS = RW_W // RW_HD
RW_LORA_W = 64
RW_LORA_A = 64
RW_LORA_V = 32
RW_LORA_G = 128
RW_P = 3 * RW_W + RW_LORA_W + RW_LORA_A + RW_LORA_G
RW_GN_EPS = 64e-5
RET_V = 128
RET_QK = 64
RET_W = 3 * D_MODEL // 8
RET_HEADS = RET_W // RET_V
RET_P = 2 * RET_HEADS * RET_QK + 2 * RET_W
ML_V = 128
ML_QK = 64
ML_W = D_MODEL - RW_W - RET_W
ML_HEADS = ML_W // ML_V
ML_P = 2 * ML_HEADS * ML_QK + 2 * ML_W + 2 * ML_HEADS
ML_GATE_CAP = 15.0
P_TOTAL = RW_P + RET_P + ML_P
D_FF = ((8 * D_MODEL + 3 * 256 - 1) // (3 * 256)) * 256
CHUNK = 128
ROPE_BASE = 10000.0
LN_EPS = 1e-5
NORM_EPS = 1e-6
ALPHA = (2 * DEPTH) ** 0.25

LANES = 128
P_V1 = RW_P
P_GATE = RW_P + LANES
P_RET = 2304
P_ML = 2 * P_RET
P_MAIN = 2304
P_PAD = 3 * P_RET
P_TN = 768
RW_CHUNK = 64
DEC_TB = 8
VMEM_LIMIT = 56 * 1024 * 1024

HIGHEST = lax.Precision.HIGHEST


def _cparams(sem):
    return pltpu.CompilerParams(dimension_semantics=sem, vmem_limit_bytes=VMEM_LIMIT)


def _dot(a, b, precision=None):
    return lax.dot_general(a, b, (((1,), (0,)), ((), ())), precision=precision, preferred_element_type=F32)


def _dot_nt(a, b, precision=None):
    return lax.dot_general(a, b, (((1,), (1,)), ((), ())), precision=precision, preferred_element_type=F32)


def _dot_tn(a, b, precision=None):
    return lax.dot_general(a, b, (((0,), (0,)), ((), ())), precision=precision, preferred_element_type=F32)


def _ln_rows(x, g, b):
    mu = jnp.mean(x, -1, keepdims=True)
    xc = x - mu
    var = jnp.mean(xc * xc, -1, keepdims=True)
    return xc * lax.rsqrt(var + LN_EPS) * g + b


def _ln_body(x_ref, g_ref, b_ref, of_ref, ob_ref):
    y = _ln_rows(x_ref[...], g_ref[...], b_ref[...])
    of_ref[...] = y
    ob_ref[...] = y.astype(BF16)


def _layernorm(x, g, b, tm):
    m, d = x.shape
    return pl.pallas_call(
        _ln_body,
        grid=(m // tm,),
        in_specs=[pl.BlockSpec((tm, d), lambda i: (i, 0)),
                  pl.BlockSpec((1, d), lambda i: (0, 0)),
                  pl.BlockSpec((1, d), lambda i: (0, 0))],
        out_specs=[pl.BlockSpec((tm, d), lambda i: (i, 0)),
                   pl.BlockSpec((tm, d), lambda i: (i, 0))],
        out_shape=[jax.ShapeDtypeStruct((m, d), F32), jax.ShapeDtypeStruct((m, d), BF16)],
        compiler_params=_cparams(("parallel",)),
        name="layernorm",
    )(x, g.reshape(1, d), b.reshape(1, d))


def _mm_body(x_ref, w_ref, o_ref):
    o_ref[...] = _dot(x_ref[...], w_ref[...]).astype(o_ref.dtype)


def _matmul(x, w, tm, tn, out_dtype=F32):
    m, k = x.shape
    n = w.shape[1]
    return pl.pallas_call(
        _mm_body,
        grid=(m // tm, n // tn),
        in_specs=[pl.BlockSpec((tm, k), lambda i, j: (i, 0)),
                  pl.BlockSpec((k, tn), lambda i, j: (0, j))],
        out_specs=pl.BlockSpec((tm, tn), lambda i, j: (i, j)),
        out_shape=jax.ShapeDtypeStruct((m, n), out_dtype),
        compiler_params=_cparams(("parallel", "parallel")),
        name="matmul",
    )(x, w)


def _swiglu_body(x_ref, wg_ref, wu_ref, o_ref):
    x = x_ref[...]
    g = _dot(x, wg_ref[...])
    u = _dot(x, wu_ref[...])
    o_ref[...] = (g * jax.nn.sigmoid(g) * u).astype(o_ref.dtype)


def _matmul_swiglu(x, wg, wu, tm, tn):
    m, k = x.shape
    n = wg.shape[1]
    return pl.pallas_call(
        _swiglu_body,
        grid=(m // tm, n // tn),
        in_specs=[pl.BlockSpec((tm, k), lambda i, j: (i, 0)),
                  pl.BlockSpec((k, tn), lambda i, j: (0, j)),
                  pl.BlockSpec((k, tn), lambda i, j: (0, j))],
        out_specs=pl.BlockSpec((tm, tn), lambda i, j: (i, j)),
        out_shape=jax.ShapeDtypeStruct((m, n), BF16),
        compiler_params=_cparams(("parallel", "parallel")),
        name="matmul_swiglu",
    )(x, wg, wu)


def _mm_res_ln_body(x_ref, w_ref, res_ref, g_ref, b_ref, of_ref, ob_ref, acc_ref, *, nk):
    kk = pl.program_id(1)

    @pl.when(kk == 0)
    def _():
        acc_ref[...] = jnp.zeros_like(acc_ref)

    acc_ref[...] += _dot(x_ref[...], w_ref[...])

    @pl.when(kk == nk - 1)
    def _():
        y = _ln_rows(ALPHA * res_ref[...] + acc_ref[...], g_ref[...], b_ref[...])
        of_ref[...] = y
        ob_ref[...] = y.astype(BF16)


def _matmul_res_ln(x, w, res, g, b, tm, tk):
    m, k = x.shape
    n = w.shape[1]
    nk = k // tk
    return pl.pallas_call(
        functools.partial(_mm_res_ln_body, nk=nk),
        grid=(m // tm, nk),
        in_specs=[pl.BlockSpec((tm, tk), lambda i, j: (i, j)),
                  pl.BlockSpec((tk, n), lambda i, j: (j, 0)),
                  pl.BlockSpec((tm, n), lambda i, j: (i, 0)),
                  pl.BlockSpec((1, n), lambda i, j: (0, 0)),
                  pl.BlockSpec((1, n), lambda i, j: (0, 0))],
        out_specs=[pl.BlockSpec((tm, n), lambda i, j: (i, 0)),
                   pl.BlockSpec((tm, n), lambda i, j: (i, 0))],
        out_shape=[jax.ShapeDtypeStruct((m, n), F32), jax.ShapeDtypeStruct((m, n), BF16)],
        scratch_shapes=[pltpu.VMEM((tm, n), F32)],
        compiler_params=_cparams(("parallel", "arbitrary")),
        name="matmul_res_ln",
    )(x, w, res, g.reshape(1, n), b.reshape(1, n))


def _out_proj_ln_body(o_rw_ref, o_ret_ref, o_ml_ref, w_ref, res_ref, g_ref, b_ref, of_ref, ob_ref):
    mix = (_dot(o_rw_ref[...], w_ref[0:RW_W, :])
           + _dot(o_ret_ref[...], w_ref[RW_W:RW_W + RET_W, :])
           + _dot(o_ml_ref[...], w_ref[RW_W + RET_W:, :]))
    y = _ln_rows(ALPHA * res_ref[...] + mix, g_ref[...], b_ref[...])
    of_ref[...] = y
    ob_ref[...] = y.astype(BF16)


def _out_proj_ln(o_rw, o_ret, o_ml, w, res, g, b, tm):
    m = o_rw.shape[0]
    n = w.shape[1]
    rows = lambda width: pl.BlockSpec((tm, width), lambda i: (i, 0))
    full = lambda r, c: pl.BlockSpec((r, c), lambda i: (0, 0))
    return pl.pallas_call(
        _out_proj_ln_body,
        grid=(m // tm,),
        in_specs=[rows(RW_W), rows(RET_W), rows(ML_W), full(D_MODEL, n), rows(n), full(1, n), full(1, n)],
        out_specs=[rows(n), rows(n)],
        out_shape=[jax.ShapeDtypeStruct((m, n), F32), jax.ShapeDtypeStruct((m, n), BF16)],
        compiler_params=_cparams(("parallel",)),
        name="out_proj_ln",
    )(o_rw, o_ret, o_ml, w, res, g.reshape(1, n), b.reshape(1, n))


RW_TB = 256
RW_GH = 4
RW_GW = RW_GH * RW_HD
RW_VEC_ROWS = 8


def _split3(x):
    hi = x.astype(BF16)
    r1 = x - hi.astype(F32)
    mid = r1.astype(BF16)
    lo = (r1 - mid.astype(F32)).astype(BF16)
    return hi, mid, lo


def _mm(a, b, dims, passes):
    dg = lambda x, y: lax.dot_general(x, y, (dims, ((), ())), preferred_element_type=F32)
    if passes == 6:
        return lax.dot_general(a, b, (dims, ((), ())), precision=HIGHEST, preferred_element_type=F32)
    ah = a.astype(BF16)
    bh = b.astype(BF16)
    if passes == 1:
        return dg(ah, bh)
    al = (a - ah.astype(F32)).astype(BF16)
    bl = (b - bh.astype(F32)).astype(BF16)
    return dg(ah, bh) + (dg(ah, bl) + dg(al, bh))


_NN = ((1,), (0,))
_NT = ((1,), (1,))
_TN = ((0,), (0,))


def _exact_lhs_dot(a_bf16, b):
    hi, mid, lo = _split3(b)
    dg = lambda y: lax.dot_general(a_bf16, y, (_NN, ((), ())), preferred_element_type=F32)
    return dg(hi) + (dg(mid) + dg(lo))


def _exact_rhs_dot(a, b_bf16):
    hi, mid, lo = _split3(a)
    dg = lambda x: lax.dot_general(x, b_bf16, (_NN, ((), ())), preferred_element_type=F32)
    return dg(hi) + (dg(mid) + dg(lo))


def _rw_scan_block(r, lw, k, v, kk, a, st, masks, passes):
    L = RW_CHUNK
    tri, strict, lower, eye, mask_bd = masks
    p_sc, p_inv, p_app, p_st = passes
    bd = lambda x: jnp.where(mask_bd, jnp.concatenate([x] * RW_GH, axis=0), 0.0)
    cum = _exact_lhs_dot(tri, lw)
    tot = cum[L - 1:L, :]
    e_neg = jnp.exp(-cum)
    ap = kk * a
    ap_h = ap * e_neg
    k_h = k * e_neg
    lhs = jnp.concatenate([kk * jnp.exp(cum - lw), r * jnp.exp(cum)], axis=0)
    sc_a = _mm(lhs, bd(ap_h), _NT, p_sc)
    sc_k = _mm(lhs, bd(k_h), _NT, p_sc)
    n_m = jnp.where(strict, sc_a[:L], 0.0)
    m_a = jnp.where(lower, sc_a[L:], 0.0)
    m_k = jnp.where(strict, sc_k[:L], 0.0)
    m_r = jnp.where(lower, sc_k[L:], 0.0)
    inv = eye - n_m
    pw = n_m
    for _ in range(int(math.log2(L)) - 1):
        pw = _mm(pw, bd(pw), _NN, p_inv)
        inv = inv + _mm(inv, bd(pw), _NN, p_inv)
    s_terms = _mm(lhs, st, _NT, p_app)
    mv = _mm(jnp.concatenate([m_k, m_r], axis=0), bd(v), _NN, p_app)
    u = _mm(inv, bd(s_terms[:L] + mv[:L]), _NN, p_app)
    y = s_terms[L:] + mv[L:] - _mm(m_a, bd(u), _NN, p_app)
    e_end = jnp.exp(tot - cum)
    upd = _mm(jnp.concatenate([v, -u], axis=0), jnp.concatenate([k * e_end, ap * e_end], axis=0), _TN, p_st)
    st_new = jnp.where(mask_bd, st * jnp.exp(tot) + upd, 0.0)
    return y, st_new


def _rw_masks():
    L, G = RW_CHUNK, RW_GW
    row = lax.broadcasted_iota(jnp.int32, (L, G), 0)
    col = lax.broadcasted_iota(jnp.int32, (L, G), 1) & (L - 1)
    rl = lax.broadcasted_iota(jnp.int32, (L, L), 0)
    cl = lax.broadcasted_iota(jnp.int32, (L, L), 1)
    rg = lax.broadcasted_iota(jnp.int32, (G, G), 0) // RW_HD
    cg = lax.broadcasted_iota(jnp.int32, (G, G), 1) // RW_HD
    tri = (rl >= cl).astype(BF16)
    return tri, row > col, row >= col, (row == col).astype(F32), rg == cg


def _softplus(z):
    return jnp.maximum(z, 0.0) + jnp.log(1.0 + jnp.exp(-jnp.abs(z)))


def _rwkv_fused_body(*refs, nc, has_vres, passes):
    if has_vres:
        (p_ref, pv_ref, vf_ref, mu_ref, vec_ref, wa_ref, g2_ref, seg_ref, vmu_ref, v2_ref,
         o_ref, sf_ref, st_scr, prev_scr, y_scr, prevv_scr) = refs
    else:
        (p_ref, mu_ref, vec_ref, wa_ref, g2_ref, seg_ref,
         o_ref, vfo_ref, sf_ref, st_scr, prev_scr, y_scr) = refs
    TB, W, L = RW_TB, RW_W, RW_CHUNK
    c = pl.program_id(1)

    @pl.when(c == 0)
    def _():
        st_scr[...] = jnp.zeros_like(st_scr)
        prev_scr[...] = jnp.zeros_like(prev_scr)
        if has_vres:
            prevv_scr[...] = jnp.zeros_like(prevv_scr)

    first_row = lax.broadcasted_iota(jnp.int32, (TB, 1), 0) == 0

    def shift_mix(x, carry_ref, mu):
        prev = jnp.where(first_row, carry_ref[...], pltpu.roll(x, 1, 0))
        carry_ref[...] = x[TB - 1:TB, :]
        return x + (prev - x) * mu

    mixed = shift_mix(p_ref[0], prev_scr, mu_ref[...])
    r = mixed[:, 0:W]
    k = mixed[:, W:2 * W]
    v = mixed[:, 2 * W:3 * W]
    xwa = mixed[:, 3 * W:3 * W + LANES]
    xg = mixed[:, 3 * W + LANES:3 * W + 2 * LANES]
    vec = vec_ref[...]
    w0, a0, kk_s, ka, rk, lnx_g, lnx_b, v0 = (vec[i:i + 1, :] for i in range(RW_VEC_ROWS))
    seg = seg_ref[...]
    wa = wa_ref[...]
    w_lora = _dot(jnp.tanh(xwa).astype(BF16), wa[:, 0:W])
    a_lora = _dot(xwa.astype(BF16), wa[:, W:2 * W])
    lw = -jnp.exp(-_softplus(-(w0 + w_lora)) - 0.5)
    a = jax.nn.sigmoid(a0 + a_lora)
    g = _dot(jax.nn.sigmoid(xg).astype(BF16), g2_ref[...])
    if has_vres:
        xv = shift_mix(pv_ref[0], prevv_scr, vmu_ref[...])
        v = v + (vf_ref[0] - v) * jax.nn.sigmoid(v0 + _dot(xv.astype(BF16), v2_ref[...]))
    else:
        vfo_ref[0] = v
    kk = k * kk_s
    kk = kk * lax.rsqrt(jnp.maximum(_exact_rhs_dot(kk * kk, seg), 1e-24))
    k = k * (1.0 + (a - 1.0) * ka)

    masks = _rw_masks()
    for s in range(TB // L):
        rs = slice(s * L, (s + 1) * L)
        for gi in range(RW_HEADS // RW_GH):
            cs = slice(gi * RW_GW, (gi + 1) * RW_GW)
            y, st_new = _rw_scan_block(r[rs, cs], lw[rs, cs], k[rs, cs], v[rs, cs], kk[rs, cs], a[rs, cs],
                                       st_scr[gi], masks, passes)
            y_scr[rs, cs] = y
            st_scr[gi] = st_new

    y = y_scr[...]
    inv_n = 1.0 / RW_HD
    y_mu = _exact_rhs_dot(y, seg) * inv_n
    yc = y - y_mu
    y_var = _exact_rhs_dot(yc * yc, seg) * inv_n
    y = yc * lax.rsqrt(y_var + RW_GN_EPS) * lnx_g + lnx_b
    bonus = _exact_rhs_dot(r * k * rk, seg) * v
    o_ref[0] = ((y + bonus) * g).astype(BF16)

    @pl.when(c == nc - 1)
    def _():
        sf_ref[0] = st_scr[...]


def _rwkv_prompt(p3, lp, v_first, passes=(3, 3, 3, 3)):
    b, t, _ = p3.shape
    nc = t // RW_TB
    has_vres = v_first is not None
    ng = RW_HEADS // RW_GH
    zpad = jnp.zeros((RW_LORA_W, RW_W), F32)
    wa = jnp.concatenate([jnp.concatenate([lp['rw_w2'], zpad], 0), jnp.concatenate([zpad, lp['rw_a2']], 0)], 1)
    vec = jnp.stack([lp['rw_w0'], lp['rw_a0'], lp['rw_kk'], lp['rw_ka'], lp['rw_rk'], lp['rw_lnx_g'], lp['rw_lnx_b'],
                     lp['rw_v0'] if has_vres else jnp.zeros((RW_W,), F32)])
    hid = jnp.arange(RW_W) // RW_HD
    seg = (hid[:, None] == hid[None, :]).astype(BF16)
    full = lambda shape: pl.BlockSpec(shape, lambda i, j: (0,) * len(shape))
    seq = lambda w, blk: pl.BlockSpec((1, RW_TB, w), lambda i, j: (i, j, blk))
    in_specs = [seq(RW_P, 0)]
    args = [p3]
    if has_vres:
        in_specs += [seq(LANES, P_V1 // LANES), seq(RW_W, 0)]
        args += [p3, v_first]
    in_specs += [full((1, RW_P)), full((RW_VEC_ROWS, RW_W)), full((LANES, 2 * RW_W)), full((RW_LORA_G, RW_W)),
                 full((RW_W, RW_W))]
    args += [lp['rw_mu'].reshape(1, RW_P), vec, wa.astype(BF16), lp['rw_g2'].astype(BF16), seg]
    if has_vres:
        in_specs += [full((1, LANES)), full((LANES, RW_W))]
        args += [jnp.pad(lp['rw_vmu'], (0, LANES - RW_LORA_V)).reshape(1, LANES),
                 jnp.pad(lp['rw_v2'], ((0, LANES - RW_LORA_V), (0, 0))).astype(BF16)]
    out_specs = [seq(RW_W, 0)]
    out_shape = [jax.ShapeDtypeStruct((b, t, RW_W), BF16)]
    if not has_vres:
        out_specs.append(seq(RW_W, 0))
        out_shape.append(jax.ShapeDtypeStruct((b, t, RW_W), F32))
    out_specs.append(pl.BlockSpec((1, ng, RW_GW, RW_GW), lambda i, j: (i, 0, 0, 0)))
    out_shape.append(jax.ShapeDtypeStruct((b, ng, RW_GW, RW_GW), F32))
    scratch = [pltpu.VMEM((ng, RW_GW, RW_GW), F32), pltpu.VMEM((1, RW_P), F32), pltpu.VMEM((RW_TB, RW_W), F32)]
    if has_vres:
        scratch.append(pltpu.VMEM((1, LANES), F32))
    outs = pl.pallas_call(
        functools.partial(_rwkv_fused_body, nc=nc, has_vres=has_vres, passes=passes),
        grid=(b, nc),
        in_specs=in_specs,
        out_specs=out_specs,
        out_shape=out_shape,
        scratch_shapes=scratch,
        compiler_params=_cparams(("parallel", "arbitrary")),
        name="rwkv_fused",
    )(*args)
    if has_vres:
        o, st_bd = outs
    else:
        o, v_first, st_bd = outs
    st5 = st_bd.reshape(b, ng, RW_GH, RW_HD, RW_GH, RW_HD)
    s_fin = jnp.stack([st5[:, :, h, :, h, :] for h in range(RW_GH)], axis=2)
    s_fin = s_fin.reshape(b, RW_HEADS, RW_HD, RW_HD).transpose(0, 1, 3, 2)
    return o, v_first, s_fin


def _ret_log_gamma(h):
    return math.log1p(-(2.0 ** (-5.0 - h)))


def _rotary_tables(pos, heads, dk):
    half = dk // 2
    inv = ROPE_BASE ** (-jnp.arange(half, dtype=F32) / half)
    ang = pos.astype(F32)[:, None] * inv[None, :]
    cos = jnp.tile(jnp.concatenate([jnp.cos(ang), jnp.cos(ang)], -1), (1, heads))
    sin = jnp.tile(jnp.concatenate([-jnp.sin(ang), jnp.sin(ang)], -1), (1, heads))
    lane = jnp.arange(heads * dk)
    perm = (lane[:, None] == (lane[None, :] ^ half)).astype(BF16)
    return cos, sin, perm


def _ret_fused_body(p_ref, cos_ref, sin_ref, perm_ref, o_ref, sf_ref, s_scr, *, nc):
    L, DK, DV, H = CHUNK, RET_QK, RET_V, RET_HEADS
    nq = H * DK
    c = pl.program_id(1)

    @pl.when(c == 0)
    def _():
        s_scr[...] = jnp.zeros_like(s_scr)

    cos = cos_ref[...]
    sin = sin_ref[...]
    perm = perm_ref[...]
    rot = lambda x: x * cos + _exact_rhs_dot(x, perm) * sin
    q_all = rot(p_ref[0, :, 0:nq])
    k_all = rot(p_ref[0, :, nq:2 * nq]) * (DK ** -0.5)
    row = lax.broadcasted_iota(jnp.int32, (L, L), 0)
    col = lax.broadcasted_iota(jnp.int32, (L, L), 1)
    rel = (row - col).astype(F32)
    idx = lax.broadcasted_iota(jnp.int32, (L, 1), 0).astype(F32)
    for h in range(H):
        lg = _ret_log_gamma(h)
        q = q_all[:, h * DK:(h + 1) * DK]
        k = k_all[:, h * DK:(h + 1) * DK]
        v = p_ref[0, :, 2 * nq + h * DV:2 * nq + (h + 1) * DV].astype(BF16)
        gate = p_ref[0, :, 2 * nq + RET_W + h * DV:2 * nq + RET_W + (h + 1) * DV]
        dmask = jnp.where(rel >= 0, jnp.exp(jnp.maximum(rel, 0.0) * lg), 0.0)
        scores = _dot_nt(q.astype(BF16), k.astype(BF16)) * dmask
        s_prev = s_scr[h]
        q_dec = q * jnp.exp((idx + 1.0) * lg)
        y = _dot(scores.astype(BF16), v) + _dot(q_dec.astype(BF16), s_prev.astype(BF16))
        k_end = k * jnp.exp((L - 1.0 - idx) * lg)
        s_scr[h] = math.exp(L * lg) * s_prev + _dot_tn(k_end.astype(BF16), v)
        y = y * lax.rsqrt(jnp.mean(y * y, -1, keepdims=True) + NORM_EPS)
        o_ref[0, :, h * DV:(h + 1) * DV] = (gate * jax.nn.sigmoid(gate) * y).astype(BF16)

    @pl.when(c == nc - 1)
    def _():
        sf_ref[0] = s_scr[...]


def _ret_prompt(p3, pos):
    b, t, _ = p3.shape
    L = CHUNK
    nc = t // L
    nq = RET_HEADS * RET_QK
    cos, sin, perm = _rotary_tables(pos, RET_HEADS, RET_QK)
    tab = pl.BlockSpec((L, nq), lambda i, j: (j, 0))
    st = pl.BlockSpec((1, RET_HEADS, RET_QK, RET_V), lambda i, j: (i, 0, 0, 0))
    return pl.pallas_call(
        functools.partial(_ret_fused_body, nc=nc),
        grid=(b, nc),
        in_specs=[pl.BlockSpec((1, L, P_MAIN), lambda i, j: (i, j, P_RET // P_MAIN)), tab, tab,
                  pl.BlockSpec((nq, nq), lambda i, j: (0, 0))],
        out_specs=[pl.BlockSpec((1, L, RET_W), lambda i, j: (i, j, 0)), st],
        out_shape=[jax.ShapeDtypeStruct((b, t, RET_W), BF16),
                   jax.ShapeDtypeStruct((b, RET_HEADS, RET_QK, RET_V), F32)],
        scratch_shapes=[pltpu.VMEM((RET_HEADS, RET_QK, RET_V), F32)],
        compiler_params=_cparams(("parallel", "arbitrary")),
        name="ret_fused",
    )(p3, cos, sin, perm)


ML_HPAD = 8


def _ml_fused_body(p_ref, gate_ref, bias_ref, norm_ref, o_ref, cf_ref, nf_ref, mf_ref, c_scr, n_scr, m_scr, *, nc):
    L, DK, DV, H = CHUNK, ML_QK, ML_V, ML_HEADS
    nq = H * DK
    ci = pl.program_id(1)

    @pl.when(ci == 0)
    def _():
        c_scr[...] = jnp.zeros_like(c_scr)
        n_scr[...] = jnp.zeros_like(n_scr)
        m_scr[...] = jnp.zeros_like(m_scr)

    row = lax.broadcasted_iota(jnp.int32, (L, L), 0)
    col = lax.broadcasted_iota(jnp.int32, (L, L), 1)
    causal = row >= col
    tri = causal.astype(BF16)
    capped = ML_GATE_CAP * jnp.tanh((gate_ref[0] + bias_ref[...]) * (1.0 / ML_GATE_CAP))
    lane = lax.broadcasted_iota(jnp.int32, (L, LANES), 1)
    g = jnp.where(lane < H, capped, jnp.where(lane < 2 * H, -_softplus(-capped), 0.0))
    cum = _exact_lhs_dot(tri, g)
    g_t = g.T
    cum_t = cum.T
    for h in range(H):
        q = p_ref[0, :, h * DK:(h + 1) * DK]
        k = p_ref[0, :, nq + h * DK:nq + (h + 1) * DK] * (DK ** -0.5)
        v = p_ref[0, :, 2 * nq + h * DV:2 * nq + (h + 1) * DV].astype(BF16)
        og = p_ref[0, :, 2 * nq + ML_W + h * DV:2 * nq + ML_W + (h + 1) * DV]
        ig_col = g[:, h:h + 1]
        ig_row = g_t[h:h + 1, :]
        b_col = cum[:, H + h:H + h + 1]
        b_row = cum_t[H + h:H + h + 1, :]
        b_tot = cum[L - 1:L, H + h:H + h + 1]
        m_prev = m_scr[h:h + 1, 0:1]
        c_prev = c_scr[h]
        n_prev = n_scr[h:h + 1, :]
        a_row = b_tot - b_row + ig_row
        a_col = b_tot - b_col + ig_col
        m_new = jnp.maximum(b_tot + m_prev, jnp.max(a_row, axis=1, keepdims=True))
        dec = jnp.exp(b_tot + m_prev - m_new)
        kw = k * jnp.exp(a_col - m_new)
        c_scr[h] = dec * c_prev + _dot_tn(kw.astype(BF16), v)
        n_scr[h:h + 1, :] = dec * n_prev + jnp.sum(kw, axis=0, keepdims=True)
        m_scr[h:h + 1, :] = jnp.broadcast_to(m_new, (1, LANES))
        dlog = jnp.where(causal, b_col - b_row + ig_row, -jnp.inf)
        inter = b_col + m_prev
        m_i = jnp.maximum(jnp.max(dlog, axis=1, keepdims=True), inter)
        s = _dot_nt(q.astype(BF16), k.astype(BF16)) * jnp.exp(dlog - m_i)
        sc = jnp.exp(inter - m_i)
        num = _dot(s.astype(BF16), v) + sc * _dot(q.astype(BF16), c_prev.astype(BF16))
        den = jnp.sum(s, axis=1, keepdims=True) + sc * jnp.sum(q * n_prev, axis=1, keepdims=True)
        hid = num / jnp.maximum(jnp.abs(den), jnp.exp(-m_i))
        hid = hid * lax.rsqrt(jnp.mean(hid * hid, -1, keepdims=True) + NORM_EPS)
        o_ref[0, :, h * DV:(h + 1) * DV] = (jax.nn.sigmoid(og) * (hid * norm_ref[:, h * DV:(h + 1) * DV])).astype(BF16)

    @pl.when(ci == nc - 1)
    def _():
        cf_ref[0] = c_scr[...]
        nf_ref[0] = n_scr[...]
        mf_ref[0] = m_scr[...]


def _ml_prompt(p3, lp):
    b, t, _ = p3.shape
    L = CHUNK
    nc = t // L
    bias = jnp.pad(jnp.concatenate([lp['ml_ib'], lp['ml_fb']]), (0, LANES - 2 * ML_HEADS)).reshape(1, LANES)
    vs = pl.BlockSpec((1, L, ML_W), lambda i, j: (i, j, 0))
    cs = pl.BlockSpec((1, ML_HEADS, ML_QK, ML_V), lambda i, j: (i, 0, 0, 0))
    ns = pl.BlockSpec((1, ML_HPAD, ML_QK), lambda i, j: (i, 0, 0))
    ms = pl.BlockSpec((1, ML_HPAD, LANES), lambda i, j: (i, 0, 0))
    o, c_f, n_f, m_f = pl.pallas_call(
        functools.partial(_ml_fused_body, nc=nc),
        grid=(b, nc),
        in_specs=[pl.BlockSpec((1, L, P_MAIN), lambda i, j: (i, j, P_ML // P_MAIN)),
                  pl.BlockSpec((1, L, LANES), lambda i, j: (i, j, P_GATE // LANES)),
                  pl.BlockSpec((1, LANES), lambda i, j: (0, 0)),
                  pl.BlockSpec((1, ML_W), lambda i, j: (0, 0))],
        out_specs=[vs, cs, ns, ms],
        out_shape=[jax.ShapeDtypeStruct((b, t, ML_W), BF16),
                   jax.ShapeDtypeStruct((b, ML_HEADS, ML_QK, ML_V), F32),
                   jax.ShapeDtypeStruct((b, ML_HPAD, ML_QK), F32),
                   jax.ShapeDtypeStruct((b, ML_HPAD, LANES), F32)],
        scratch_shapes=[pltpu.VMEM((ML_HEADS, ML_QK, ML_V), F32),
                        pltpu.VMEM((ML_HPAD, ML_QK), F32),
                        pltpu.VMEM((ML_HPAD, LANES), F32)],
        compiler_params=_cparams(("parallel", "arbitrary")),
        name="ml_fused",
    )(p3, p3, bias, lp['ml_norm'].reshape(1, ML_W))
    return o, c_f, n_f[:, :ML_HEADS], m_f[:, :ML_HEADS, 0]


def _to_cols(x):
    b, c = x.shape
    return x.reshape(b // DEC_TB, DEC_TB, c).transpose(0, 2, 1)


def _rwkv_step_body(w_ref, ap_ref, k_ref, kk_ref, r_ref, v_ref, s_ref, y_ref, so_ref):
    N = RW_HD
    for j in range(DEC_TB):
        for h in range(RW_HEADS):
            cs = slice(h * N, (h + 1) * N)
            col = lambda ref: ref[0, cs, j:j + 1]
            s = s_ref[j, h]
            v = v_ref[j:j + 1, cs]
            sa = jnp.sum(col(kk_ref) * s, axis=0, keepdims=True)
            s_new = col(w_ref) * s - col(ap_ref) * sa + col(k_ref) * v
            so_ref[j, h] = s_new
            y_ref[j:j + 1, cs] = jnp.sum(col(r_ref) * s_new, axis=0, keepdims=True)


def _rwkv_step(wdec, ap, k, kk, r, v, s0):
    b, w = v.shape
    cols = pl.BlockSpec((1, w, DEC_TB), lambda i: (i, 0, 0))
    rows = pl.BlockSpec((DEC_TB, w), lambda i: (i, 0))
    st = pl.BlockSpec((DEC_TB, RW_HEADS, RW_HD, RW_HD), lambda i: (i, 0, 0, 0))
    return pl.pallas_call(
        _rwkv_step_body,
        grid=(b // DEC_TB,),
        in_specs=[cols] * 5 + [rows, st],
        out_specs=[rows, st],
        out_shape=[jax.ShapeDtypeStruct((b, w), F32), jax.ShapeDtypeStruct(s0.shape, F32)],
        compiler_params=_cparams(("parallel",)),
        name="rwkv_step",
    )(_to_cols(wdec), _to_cols(ap), _to_cols(k), _to_cols(kk), _to_cols(r), v, s0)


def _ret_step_body(qc_ref, kc_ref, v_ref, s_ref, y_ref, so_ref):
    DK, DV = RET_QK, RET_V
    for j in range(DEC_TB):
        for h in range(RET_HEADS):
            gamma = math.exp(_ret_log_gamma(h))
            q = qc_ref[0, h * DK:(h + 1) * DK, j:j + 1]
            k = kc_ref[0, h * DK:(h + 1) * DK, j:j + 1]
            v = v_ref[j:j + 1, h * DV:(h + 1) * DV]
            s = s_ref[j, h]
            qk = jnp.sum(q * k, axis=0, keepdims=True)
            y_ref[j:j + 1, h * DV:(h + 1) * DV] = qk * v + gamma * jnp.sum(q * s, axis=0, keepdims=True)
            so_ref[j, h] = gamma * s + k * v


def _ret_step(q, k, v, s0):
    b = q.shape[0]
    cols = pl.BlockSpec((1, RET_HEADS * RET_QK, DEC_TB), lambda i: (i, 0, 0))
    rows = pl.BlockSpec((DEC_TB, RET_W), lambda i: (i, 0))
    st = pl.BlockSpec((DEC_TB, RET_HEADS, RET_QK, RET_V), lambda i: (i, 0, 0, 0))
    return pl.pallas_call(
        _ret_step_body,
        grid=(b // DEC_TB,),
        in_specs=[cols, cols, rows, st],
        out_specs=[rows, st],
        out_shape=[jax.ShapeDtypeStruct((b, RET_W), F32), jax.ShapeDtypeStruct(s0.shape, F32)],
        compiler_params=_cparams(("parallel",)),
        name="ret_step",
    )(_to_cols(q), _to_cols(k), v, s0)


def _ml_step_body(qc_ref, kc_ref, q_ref, k_ref, v_ref, ig_ref, lf_ref, c_ref, n_ref, m_ref,
                  h_ref, co_ref, no_ref, mo_ref):
    DK, DV = ML_QK, ML_V
    for j in range(DEC_TB):
        for h in range(ML_HEADS):
            ks = slice(h * DK, (h + 1) * DK)
            vs = slice(h * DV, (h + 1) * DV)
            q_col = qc_ref[0, ks, j:j + 1]
            k_col = kc_ref[0, ks, j:j + 1]
            q_row = q_ref[j:j + 1, ks]
            k_row = k_ref[j:j + 1, ks]
            v = v_ref[j:j + 1, vs]
            ig = ig_ref[j:j + 1, h:h + 1]
            lf = lf_ref[j:j + 1, h:h + 1]
            m_prev = m_ref[j:j + 1, h:h + 1]
            c_prev = c_ref[j, h]
            n_prev = n_ref[j, h:h + 1, :]
            m_new = jnp.maximum(lf + m_prev, ig)
            dec = jnp.exp(lf + m_prev - m_new)
            wgt = jnp.exp(ig - m_new)
            co_ref[j, h] = dec * c_prev + (k_col * wgt) * v
            no_ref[j, h:h + 1, :] = dec * n_prev + k_row * wgt
            mo_ref[j:j + 1, h:h + 1] = m_new
            s = jnp.sum(q_row * k_row, axis=1, keepdims=True) * wgt
            num = s * v + dec * jnp.sum(q_col * c_prev, axis=0, keepdims=True)
            den = s + dec * jnp.sum(q_row * n_prev, axis=1, keepdims=True)
            h_ref[j:j + 1, vs] = num / jnp.maximum(jnp.abs(den), jnp.exp(-m_new))


def _ml_step(q, k, v, ig, lf, c0, n0, m0):
    b = q.shape[0]
    cols = pl.BlockSpec((1, ML_HEADS * ML_QK, DEC_TB), lambda i: (i, 0, 0))
    qk_rows = pl.BlockSpec((DEC_TB, ML_HEADS * ML_QK), lambda i: (i, 0))
    rows = pl.BlockSpec((DEC_TB, ML_W), lambda i: (i, 0))
    sc = pl.BlockSpec((DEC_TB, ML_HEADS), lambda i: (i, 0))
    cs = pl.BlockSpec((DEC_TB, ML_HEADS, ML_QK, ML_V), lambda i: (i, 0, 0, 0))
    ns = pl.BlockSpec((DEC_TB, ML_HEADS, ML_QK), lambda i: (i, 0, 0))
    return pl.pallas_call(
        _ml_step_body,
        grid=(b // DEC_TB,),
        in_specs=[cols, cols, qk_rows, qk_rows, rows, sc, sc, cs, ns, sc],
        out_specs=[rows, cs, ns, sc],
        out_shape=[jax.ShapeDtypeStruct((b, ML_W), F32), jax.ShapeDtypeStruct(c0.shape, F32),
                   jax.ShapeDtypeStruct(n0.shape, F32), jax.ShapeDtypeStruct(m0.shape, F32)],
        compiler_params=_cparams(("parallel",)),
        name="ml_step",
    )(_to_cols(q), _to_cols(k), q, k, v, ig, lf, c0, n0, m0)


def _heads(a, h):
    return a.reshape(a.shape[:-1] + (h, a.shape[-1] // h))


def _shift_prev(p, prev_row):
    return jnp.concatenate([prev_row[:, None, :], p[:, :-1]], axis=1)


def _rotary(x, pos):
    half = x.shape[-1] // 2
    inv = ROPE_BASE ** (-jnp.arange(half, dtype=F32) / half)
    ang = pos.astype(F32)[:, None] * inv[None, :]
    cos = jnp.cos(ang)[None, :, None, :]
    sin = jnp.sin(ang)[None, :, None, :]
    x1, x2 = x[..., :half], x[..., half:]
    return jnp.concatenate([x1 * cos - x2 * sin, x1 * sin + x2 * cos], -1)


def _small_matmul(x, w):
    lead = x.shape[:-1]
    kdim, n = w.shape
    x2 = x.reshape(-1, kdim)
    m = x2.shape[0]
    kp = -(-kdim // LANES) * LANES
    npad = -(-n // LANES) * LANES
    x2 = jnp.pad(x2.astype(BF16), ((0, 0), (0, kp - kdim)))
    w2 = jnp.pad(w.astype(BF16), ((0, kp - kdim), (0, npad - n)))
    tm = 1024 if m % 1024 == 0 else m
    out = _matmul(x2, w2, tm, npad)
    return out[:, :n].reshape(lead + (n,))


def _mix_prompt(p, pos, v_first, lp):
    o_rw, v_first, s_new = _rwkv_prompt(p, lp, v_first, (1, 1, 1, 1))
    o_ret, r_new = _ret_prompt(p, pos)
    o_ml, c_new, n_new, m_new = _ml_prompt(p, lp)
    return (o_rw, o_ret, o_ml), v_first, (s_new, r_new, c_new, n_new, m_new)


def _mix_sample(p, pos, v_first, st, lp, prev_row):
    bsz, t_len, _ = p.shape

    p_rw = p[..., :RW_P]
    mixed = p_rw + (_shift_prev(p_rw, prev_row[:, :RW_P]) - p_rw) * lp['rw_mu']
    sizes = np.cumsum([RW_W, RW_W, RW_W, RW_LORA_W, RW_LORA_A, RW_LORA_G])[:-1]
    r, k, v, xw, xa, xg = jnp.split(mixed, [int(s) for s in sizes], axis=-1)
    w = -jax.nn.softplus(-(lp['rw_w0'] + _small_matmul(jnp.tanh(xw), lp['rw_w2']))) - 0.5
    a = jax.nn.sigmoid(lp['rw_a0'] + _small_matmul(xa, lp['rw_a2']))
    g = _small_matmul(jax.nn.sigmoid(xg), lp['rw_g2'])
    if v_first is None:
        v_first = v
    else:
        pv = p[..., P_V1:P_V1 + RW_LORA_V]
        xv = pv + (_shift_prev(pv, prev_row[:, P_V1:P_V1 + RW_LORA_V]) - pv) * lp['rw_vmu']
        v = v + (v_first - v) * jax.nn.sigmoid(lp['rw_v0'] + _small_matmul(xv, lp['rw_v2']))
    kk = _heads(k * lp['rw_kk'], RW_HEADS)
    kk = kk * lax.rsqrt(jnp.maximum(jnp.sum(jnp.square(kk), -1, keepdims=True), 1e-24))
    kk = kk.reshape(bsz, t_len, RW_W)
    k = k * (1.0 + (a - 1.0) * lp['rw_ka'])
    lw = -jnp.exp(w)
    y, s_new = _rwkv_step(jnp.exp(lw)[:, 0], (kk * a)[:, 0], k[:, 0], kk[:, 0], r[:, 0], v[:, 0], st['rw_wkv'])
    y = _heads(y[:, None, :], RW_HEADS)
    y_mu = jnp.mean(y, -1, keepdims=True)
    y_var = jnp.mean(jnp.square(y - y_mu), -1, keepdims=True)
    y = ((y - y_mu) * lax.rsqrt(y_var + RW_GN_EPS)).reshape(bsz, t_len, RW_W)
    y = y * lp['rw_lnx_g'] + lp['rw_lnx_b']
    rh, kh, vh = (_heads(u, RW_HEADS) for u in (r, k, v))
    bonus = jnp.sum(rh * kh * _heads(lp['rw_rk'], RW_HEADS), -1, keepdims=True) * vh
    o_rw = ((y + bonus.reshape(bsz, t_len, RW_W)) * g).astype(BF16)

    nqk = RET_HEADS * RET_QK
    p_ret = p[..., P_RET:P_RET + P_MAIN]
    qr, kr, vr, gr = (p_ret[..., :nqk], p_ret[..., nqk:2 * nqk],
                      p_ret[..., 2 * nqk:2 * nqk + RET_W], p_ret[..., 2 * nqk + RET_W:])
    qh = _rotary(_heads(qr, RET_HEADS), pos).reshape(bsz, t_len, nqk)
    khr = (_rotary(_heads(kr, RET_HEADS), pos) * (RET_QK ** -0.5)).reshape(bsz, t_len, nqk)
    yr, r_new = _ret_step(qh[:, 0], khr[:, 0], vr[:, 0], st['ret'])
    yr = _heads(yr[:, None, :], RET_HEADS)
    yr = yr * lax.rsqrt(jnp.mean(jnp.square(yr), -1, keepdims=True) + NORM_EPS)
    o_ret = (jax.nn.silu(gr) * yr.reshape(bsz, t_len, RET_W)).astype(BF16)

    nqk = ML_HEADS * ML_QK
    p_ml = p[..., P_ML:P_ML + P_MAIN]
    qm, km, vm, om = (p_ml[..., :nqk], p_ml[..., nqk:2 * nqk],
                      p_ml[..., 2 * nqk:2 * nqk + ML_W], p_ml[..., 2 * nqk + ML_W:])
    im = p[..., P_GATE:P_GATE + ML_HEADS]
    fm = p[..., P_GATE + ML_HEADS:P_GATE + 2 * ML_HEADS]
    ig = ML_GATE_CAP * jnp.tanh((im + lp['ml_ib']) / ML_GATE_CAP)
    lf = jax.nn.log_sigmoid(ML_GATE_CAP * jnp.tanh((fm + lp['ml_fb']) / ML_GATE_CAP))
    km = km * (ML_QK ** -0.5)
    hm, c_new, n_new, m_new = _ml_step(qm[:, 0], km[:, 0], vm[:, 0], ig[:, 0], lf[:, 0],
                                       st['ml_c'], st['ml_n'], st['ml_m'])
    hm = _heads(hm[:, None, :], ML_HEADS)
    hm = hm * lax.rsqrt(jnp.mean(jnp.square(hm), -1, keepdims=True) + NORM_EPS)
    o_ml = (jax.nn.sigmoid(om) * (hm.reshape(bsz, t_len, ML_W) * lp['ml_norm'])).astype(BF16)

    return (o_rw, o_ret, o_ml), v_first, (s_new, r_new, c_new, n_new, m_new)


def _pack_w_in(w_in_l, rw_v1_l):
    d = w_in_l.shape[0]
    z = lambda n: jnp.zeros((d, n), F32)
    nq = 2 * ML_HEADS * ML_QK + 2 * ML_W
    ml0 = RW_P + RET_P
    v1 = z(LANES) if rw_v1_l is None else jnp.pad(rw_v1_l, ((0, 0), (0, LANES - RW_LORA_V)))
    gates = jnp.pad(w_in_l[:, ml0 + nq:], ((0, 0), (0, LANES - 2 * ML_HEADS)))
    cols = [w_in_l[:, :RW_P], v1, gates, z(P_RET - P_GATE - LANES), w_in_l[:, RW_P:ml0], w_in_l[:, ml0:ml0 + nq]]
    return jnp.concatenate(cols, axis=1).astype(BF16)


def _token_tiles(m):
    if m % 2048 == 0:
        return 2048, 512, 512
    return m, m, m


def kernel(x_prompt, x_sample, state_rw_shift, state_rw_wkv, state_ret, state_ml_c, state_ml_n, state_ml_m,
           ln0_g, ln0_b, w_in, rw_mu, rw_w0, rw_w2, rw_a0, rw_a2, rw_g2, rw_kk, rw_ka, rw_rk,
           rw_lnx_g, rw_lnx_b, rw_v0, rw_v1, rw_vmu, rw_v2, ml_ib, ml_fb, ml_norm, w_out,
           ln1_g, ln1_b, w_gate, w_up, w_down, ln2_g, ln2_b):
    bp, tp, d = x_prompt.shape
    bs, ts, _ = x_sample.shape
    groups = {
        'p': dict(b=bp, t=tp, pos=jnp.arange(tp), chunked=True),
        's': dict(b=bs, t=ts, pos=PAST_LEN + jnp.arange(ts), chunked=False),
    }
    xs_f, xs_b, v_first, outs = {}, {}, {}, {}
    for name, x in (('p', x_prompt), ('s', x_sample)):
        m = x.shape[0] * x.shape[1]
        xs_f[name], xs_b[name] = _layernorm(x.reshape(m, d), ln0_g, ln0_b, _token_tiles(m)[1])
        v_first[name] = None
        outs[name] = []

    for l in range(DEPTH):
        lp = {
            'rw_mu': rw_mu[l], 'rw_w0': rw_w0[l], 'rw_w2': rw_w2[l], 'rw_a0': rw_a0[l], 'rw_a2': rw_a2[l],
            'rw_g2': rw_g2[l], 'rw_kk': rw_kk[l], 'rw_ka': rw_ka[l], 'rw_rk': rw_rk[l],
            'rw_lnx_g': rw_lnx_g[l], 'rw_lnx_b': rw_lnx_b[l], 'ml_ib': ml_ib[l], 'ml_fb': ml_fb[l],
            'ml_norm': ml_norm[l],
        }
        if l > 0:
            lp.update(rw_v0=rw_v0[l - 1], rw_vmu=rw_vmu[l - 1], rw_v2=rw_v2[l - 1])
        w_in_b = _pack_w_in(w_in[l], rw_v1[l - 1] if l > 0 else None)
        w_out_b = w_out[l].astype(BF16)
        w_gate_b = w_gate[l].astype(BF16)
        w_up_b = w_up[l].astype(BF16)
        w_down_b = w_down[l].astype(BF16)
        prev_s = _matmul(state_rw_shift[l].astype(BF16), w_in_b, bs, P_TN)
        for name, grp in groups.items():
            b, t = grp['b'], grp['t']
            m = b * t
            tm_big, tm_out, tm_down = _token_tiles(m)
            p = _matmul(xs_b[name], w_in_b, tm_big, P_TN).reshape(b, t, P_PAD)
            if grp['chunked']:
                o, v_first[name], new_st = _mix_prompt(p, grp['pos'], v_first[name], lp)
            else:
                st = {'rw_wkv': state_rw_wkv[l], 'ret': state_ret[l], 'ml_c': state_ml_c[l],
                      'ml_n': state_ml_n[l], 'ml_m': state_ml_m[l]}
                o, v_first[name], new_st = _mix_sample(p, grp['pos'], v_first[name], st, lp, prev_s)
            outs[name].append((xs_f[name].reshape(b, t, d)[:, -1],) + new_st)
            o_rw, o_ret, o_ml = (u.reshape(m, u.shape[-1]) for u in o)
            x1_f, x1_b = _out_proj_ln(o_rw, o_ret, o_ml, w_out_b, xs_f[name], ln1_g[l], ln1_b[l], tm_out)
            hdn = _matmul_swiglu(x1_b, w_gate_b, w_up_b, tm_big, 512)
            xs_f[name], xs_b[name] = _matmul_res_ln(hdn, w_down_b, x1_f, ln2_g[l], ln2_b[l], tm_down, 512)

    y_p = xs_f['p'].reshape(bp, tp, d)
    y_s = xs_f['s'].reshape(bs, ts, d)
    sp = [jnp.stack([o[i] for o in outs['p']]) for i in range(6)]
    ss = [jnp.stack([o[i] for o in outs['s']]) for i in range(6)]
    return (y_p, y_s, sp[0], sp[1], sp[2], sp[3], sp[4], sp[5], ss[0], ss[1], ss[2], ss[3], ss[4], ss[5])
```

```python
import functools
import math

import numpy as np
import jax
import jax.numpy as jnp
from jax import lax
from jax.experimental import pallas as pl
from jax.experimental.pallas import tpu as pltpu

F32 = jnp.float32
BF16 = jnp.bfloat16

D_MODEL = 2048
DEPTH = 2
PAST_LEN = 16384
RW_HD = 64
RW_W = D_MODEL // 4
RW_HEADS = RW_W // RW_HD
RW_LORA_W = 64
RW_LORA_A = 64
RW_LORA_V = 32
RW_LORA_G = 128
RW_P = 3 * RW_W + RW_LORA_W + RW_LORA_A + RW_LORA_G
RW_GN_EPS = 64e-5
RET_V = 128
RET_QK = 64
RET_W = 3 * D_MODEL // 8
RET_HEADS = RET_W // RET_V
RET_P = 2 * RET_HEADS * RET_QK + 2 * RET_W
ML_V = 128
ML_QK = 64
ML_W = D_MODEL - RW_W - RET_W
ML_HEADS = ML_W // ML_V
ML_P = 2 * ML_HEADS * ML_QK + 2 * ML_W + 2 * ML_HEADS
ML_GATE_CAP = 15.0
P_TOTAL = RW_P + RET_P + ML_P
D_FF = ((8 * D_MODEL + 3 * 256 - 1) // (3 * 256)) * 256
CHUNK = 128
ROPE_BASE = 10000.0
LN_EPS = 1e-5
NORM_EPS = 1e-6
ALPHA = (2 * DEPTH) ** 0.25

LANES = 128
P_V1 = RW_P
P_GATE = RW_P + LANES
P_RET = 2304
P_ML = 2 * P_RET
P_MAIN = 2304
P_PAD = 3 * P_RET
P_TN = 768
RW_CHUNK = 64
DEC_TB = 8
VMEM_LIMIT = 56 * 1024 * 1024

HIGHEST = lax.Precision.HIGHEST


def _cparams(sem):
    return pltpu.CompilerParams(dimension_semantics=sem, vmem_limit_bytes=VMEM_LIMIT)


def _dot(a, b, precision=None):
    return lax.dot_general(a, b, (((1,), (0,)), ((), ())), precision=precision, preferred_element_type=F32)


def _dot_nt(a, b, precision=None):
    return lax.dot_general(a, b, (((1,), (1,)), ((), ())), precision=precision, preferred_element_type=F32)


def _dot_tn(a, b, precision=None):
    return lax.dot_general(a, b, (((0,), (0,)), ((), ())), precision=precision, preferred_element_type=F32)


def _ln_rows(x, g, b):
    mu = jnp.mean(x, -1, keepdims=True)
    xc = x - mu
    var = jnp.mean(xc * xc, -1, keepdims=True)
    return xc * lax.rsqrt(var + LN_EPS) * g + b


def _ln_body(x_ref, g_ref, b_ref, of_ref, ob_ref):
    y = _ln_rows(x_ref[...], g_ref[...], b_ref[...])
    of_ref[...] = y
    ob_ref[...] = y.astype(BF16)


def _layernorm(x, g, b, tm):
    m, d = x.shape
    return pl.pallas_call(
        _ln_body,
        grid=(m // tm,),
        in_specs=[pl.BlockSpec((tm, d), lambda i: (i, 0)),
                  pl.BlockSpec((1, d), lambda i: (0, 0)),
                  pl.BlockSpec((1, d), lambda i: (0, 0))],
        out_specs=[pl.BlockSpec((tm, d), lambda i: (i, 0)),
                   pl.BlockSpec((tm, d), lambda i: (i, 0))],
        out_shape=[jax.ShapeDtypeStruct((m, d), F32), jax.ShapeDtypeStruct((m, d), BF16)],
        compiler_params=_cparams(("parallel",)),
        name="layernorm",
    )(x, g.reshape(1, d), b.reshape(1, d))


def _mm_body(x_ref, w_ref, o_ref):
    o_ref[...] = _dot(x_ref[...], w_ref[...]).astype(o_ref.dtype)


def _matmul(x, w, tm, tn, out_dtype=F32):
    m, k = x.shape
    n = w.shape[1]
    return pl.pallas_call(
        _mm_body,
        grid=(m // tm, n // tn),
        in_specs=[pl.BlockSpec((tm, k), lambda i, j: (i, 0)),
                  pl.BlockSpec((k, tn), lambda i, j: (0, j))],
        out_specs=pl.BlockSpec((tm, tn), lambda i, j: (i, j)),
        out_shape=jax.ShapeDtypeStruct((m, n), out_dtype),
        compiler_params=_cparams(("parallel", "parallel")),
        name="matmul",
    )(x, w)


def _swiglu_body(x_ref, wg_ref, wu_ref, o_ref):
    x = x_ref[...]
    g = _dot(x, wg_ref[...])
    u = _dot(x, wu_ref[...])
    o_ref[...] = (g * jax.nn.sigmoid(g) * u).astype(o_ref.dtype)


def _matmul_swiglu(x, wg, wu, tm, tn):
    m, k = x.shape
    n = wg.shape[1]
    return pl.pallas_call(
        _swiglu_body,
        grid=(m // tm, n // tn),
        in_specs=[pl.BlockSpec((tm, k), lambda i, j: (i, 0)),
                  pl.BlockSpec((k, tn), lambda i, j: (0, j)),
                  pl.BlockSpec((k, tn), lambda i, j: (0, j))],
        out_specs=pl.BlockSpec((tm, tn), lambda i, j: (i, j)),
        out_shape=jax.ShapeDtypeStruct((m, n), BF16),
        compiler_params=_cparams(("parallel", "parallel")),
        name="matmul_swiglu",
    )(x, wg, wu)


def _mm_res_ln_body(x_ref, w_ref, res_ref, g_ref, b_ref, of_ref, ob_ref, acc_ref, *, nk):
    kk = pl.program_id(1)

    @pl.when(kk == 0)
    def _():
        acc_ref[...] = jnp.zeros_like(acc_ref)

    acc_ref[...] += _dot(x_ref[...], w_ref[...])

    @pl.when(kk == nk - 1)
    def _():
        y = _ln_rows(ALPHA * res_ref[...] + acc_ref[...], g_ref[...], b_ref[...])
        of_ref[...] = y
        ob_ref[...] = y.astype(BF16)


def _matmul_res_ln(x, w, res, g, b, tm, tk):
    m, k = x.shape
    n = w.shape[1]
    nk = k // tk
    return pl.pallas_call(
        functools.partial(_mm_res_ln_body, nk=nk),
        grid=(m // tm, nk),
        in_specs=[pl.BlockSpec((tm, tk), lambda i, j: (i, j)),
                  pl.BlockSpec((tk, n), lambda i, j: (j, 0)),
                  pl.BlockSpec((tm, n), lambda i, j: (i, 0)),
                  pl.BlockSpec((1, n), lambda i, j: (0, 0)),
                  pl.BlockSpec((1, n), lambda i, j: (0, 0))],
        out_specs=[pl.BlockSpec((tm, n), lambda i, j: (i, 0)),
                   pl.BlockSpec((tm, n), lambda i, j: (i, 0))],
        out_shape=[jax.ShapeDtypeStruct((m, n), F32), jax.ShapeDtypeStruct((m, n), BF16)],
        scratch_shapes=[pltpu.VMEM((tm, n), F32)],
        compiler_params=_cparams(("parallel", "arbitrary")),
        name="matmul_res_ln",
    )(x, w, res, g.reshape(1, n), b.reshape(1, n))


def _out_proj_ln_body(o_rw_ref, o_ret_ref, o_ml_ref, w_ref, res_ref, g_ref, b_ref, of_ref, ob_ref):
    mix = (_dot(o_rw_ref[...], w_ref[0:RW_W, :])
           + _dot(o_ret_ref[...], w_ref[RW_W:RW_W + RET_W, :])
           + _dot(o_ml_ref[...], w_ref[RW_W + RET_W:, :]))
    y = _ln_rows(ALPHA * res_ref[...] + mix, g_ref[...], b_ref[...])
    of_ref[...] = y
    ob_ref[...] = y.astype(BF16)


def _out_proj_ln(o_rw, o_ret, o_ml, w, res, g, b, tm):
    m = o_rw.shape[0]
    n = w.shape[1]
    rows = lambda width: pl.BlockSpec((tm, width), lambda i: (i, 0))
    full = lambda r, c: pl.BlockSpec((r, c), lambda i: (0, 0))
    return pl.pallas_call(
        _out_proj_ln_body,
        grid=(m // tm,),
        in_specs=[rows(RW_W), rows(RET_W), rows(ML_W), full(D_MODEL, n), rows(n), full(1, n), full(1, n)],
        out_specs=[rows(n), rows(n)],
        out_shape=[jax.ShapeDtypeStruct((m, n), F32), jax.ShapeDtypeStruct((m, n), BF16)],
        compiler_params=_cparams(("parallel",)),
        name="out_proj_ln",
    )(o_rw, o_ret, o_ml, w, res, g.reshape(1, n), b.reshape(1, n))


RW_TB = 256
RW_GH = 4
RW_GW = RW_GH * RW_HD
RW_VEC_ROWS = 8


def _split3(x):
    hi = x.astype(BF16)
    r1 = x - hi.astype(F32)
    mid = r1.astype(BF16)
    lo = (r1 - mid.astype(F32)).astype(BF16)
    return hi, mid, lo


def _mm(a, b, dims, passes):
    dg = lambda x, y: lax.dot_general(x, y, (dims, ((), ())), preferred_element_type=F32)
    if passes == 6:
        return lax.dot_general(a, b, (dims, ((), ())), precision=HIGHEST, preferred_element_type=F32)
    ah = a.astype(BF16)
    bh = b.astype(BF16)
    if passes == 1:
        return dg(ah, bh)
    al = (a - ah.astype(F32)).astype(BF16)
    bl = (b - bh.astype(F32)).astype(BF16)
    return dg(ah, bh) + (dg(ah, bl) + dg(al, bh))


_NN = ((1,), (0,))
_NT = ((1,), (1,))
_TN = ((0,), (0,))


def _exact_lhs_dot(a_bf16, b):
    hi, mid, lo = _split3(b)
    dg = lambda y: lax.dot_general(a_bf16, y, (_NN, ((), ())), preferred_element_type=F32)
    return dg(hi) + (dg(mid) + dg(lo))


def _exact_rhs_dot(a, b_bf16):
    hi, mid, lo = _split3(a)
    dg = lambda x: lax.dot_general(x, b_bf16, (_NN, ((), ())), preferred_element_type=F32)
    return dg(hi) + (dg(mid) + dg(lo))


def _rw_scan_block(r, lw, k, v, kk, a, st, masks, passes):
    L = RW_CHUNK
    tri, strict, lower, eye, mask_bd = masks
    p_sc, p_inv, p_app, p_st = passes
    bd = lambda x: jnp.where(mask_bd, jnp.concatenate([x] * RW_GH, axis=0), 0.0)
    cum = _exact_lhs_dot(tri, lw)
    tot = cum[L - 1:L, :]
    e_neg = jnp.exp(-cum)
    ap = kk * a
    ap_h = ap * e_neg
    k_h = k * e_neg
    lhs = jnp.concatenate([kk * jnp.exp(cum - lw), r * jnp.exp(cum)], axis=0)
    sc_a = _mm(lhs, bd(ap_h), _NT, p_sc)
    sc_k = _mm(lhs, bd(k_h), _NT, p_sc)
    n_m = jnp.where(strict, sc_a[:L], 0.0)
    m_a = jnp.where(lower, sc_a[L:], 0.0)
    m_k = jnp.where(strict, sc_k[:L], 0.0)
    m_r = jnp.where(lower, sc_k[L:], 0.0)
    inv = eye - n_m
    pw = n_m
    for _ in range(int(math.log2(L)) - 1):
        pw = _mm(pw, bd(pw), _NN, p_inv)
        inv = inv + _mm(inv, bd(pw), _NN, p_inv)
    s_terms = _mm(lhs, st, _NT, p_app)
    mv = _mm(jnp.concatenate([m_k, m_r], axis=0), bd(v), _NN, p_app)
    u = _mm(inv, bd(s_terms[:L] + mv[:L]), _NN, p_app)
    y = s_terms[L:] + mv[L:] - _mm(m_a, bd(u), _NN, p_app)
    e_end = jnp.exp(tot - cum)
    upd = _mm(jnp.concatenate([v, -u], axis=0), jnp.concatenate([k * e_end, ap * e_end], axis=0), _TN, p_st)
    st_new = jnp.where(mask_bd, st * jnp.exp(tot) + upd, 0.0)
    return y, st_new


def _rw_masks():
    L, G = RW_CHUNK, RW_GW
    row = lax.broadcasted_iota(jnp.int32, (L, G), 0)
    col = lax.broadcasted_iota(jnp.int32, (L, G), 1) & (L - 1)
    rl = lax.broadcasted_iota(jnp.int32, (L, L), 0)
    cl = lax.broadcasted_iota(jnp.int32, (L, L), 1)
    rg = lax.broadcasted_iota(jnp.int32, (G, G), 0) // RW_HD
    cg = lax.broadcasted_iota(jnp.int32, (G, G), 1) // RW_HD
    tri = (rl >= cl).astype(BF16)
    return tri, row > col, row >= col, (row == col).astype(F32), rg == cg


def _softplus(z):
    return jnp.maximum(z, 0.0) + jnp.log(1.0 + jnp.exp(-jnp.abs(z)))


def _rwkv_fused_body(*refs, nc, has_vres, passes):
    if has_vres:
        (p_ref, pv_ref, vf_ref, mu_ref, vec_ref, wa_ref, g2_ref, seg_ref, vmu_ref, v2_ref,
         o_ref, sf_ref, st_scr, prev_scr, y_scr, prevv_scr) = refs
    else:
        (p_ref, mu_ref, vec_ref, wa_ref, g2_ref, seg_ref,
         o_ref, vfo_ref, sf_ref, st_scr, prev_scr, y_scr) = refs
    TB, W, L = RW_TB, RW_W, RW_CHUNK
    c = pl.program_id(1)

    @pl.when(c == 0)
    def _():
        st_scr[...] = jnp.zeros_like(st_scr)
        prev_scr[...] = jnp.zeros_like(prev_scr)
        if has_vres:
            prevv_scr[...] = jnp.zeros_like(prevv_scr)

    first_row = lax.broadcasted_iota(jnp.int32, (TB, 1), 0) == 0

    def shift_mix(x, carry_ref, mu):
        prev = jnp.where(first_row, carry_ref[...], pltpu.roll(x, 1, 0))
        carry_ref[...] = x[TB - 1:TB, :]
        return x + (prev - x) * mu

    mixed = shift_mix(p_ref[0], prev_scr, mu_ref[...])
    r = mixed[:, 0:W]
    k = mixed[:, W:2 * W]
    v = mixed[:, 2 * W:3 * W]
    xwa = mixed[:, 3 * W:3 * W + LANES]
    xg = mixed[:, 3 * W + LANES:3 * W + 2 * LANES]
    vec = vec_ref[...]
    w0, a0, kk_s, ka, rk, lnx_g, lnx_b, v0 = (vec[i:i + 1, :] for i in range(RW_VEC_ROWS))
    seg = seg_ref[...]
    wa = wa_ref[...]
    w_lora = _dot(jnp.tanh(xwa).astype(BF16), wa[:, 0:W])
    a_lora = _dot(xwa.astype(BF16), wa[:, W:2 * W])
    lw = -jnp.exp(-_softplus(-(w0 + w_lora)) - 0.5)
    a = jax.nn.sigmoid(a0 + a_lora)
    g = _dot(jax.nn.sigmoid(xg).astype(BF16), g2_ref[...])
    if has_vres:
        xv = shift_mix(pv_ref[0], prevv_scr, vmu_ref[...])
        v = v + (vf_ref[0] - v) * jax.nn.sigmoid(v0 + _dot(xv.astype(BF16), v2_ref[...]))
    else:
        vfo_ref[0] = v
    kk = k * kk_s
    kk = kk * lax.rsqrt(jnp.maximum(_exact_rhs_dot(kk * kk, seg), 1e-24))
    k = k * (1.0 + (a - 1.0) * ka)

    masks = _rw_masks()
    for s in range(TB // L):
        rs = slice(s * L, (s + 1) * L)
        for gi in range(RW_HEADS // RW_GH):
            cs = slice(gi * RW_GW, (gi + 1) * RW_GW)
            y, st_new = _rw_scan_block(r[rs, cs], lw[rs, cs], k[rs, cs], v[rs, cs], kk[rs, cs], a[rs, cs],
                                       st_scr[gi], masks, passes)
            y_scr[rs, cs] = y
            st_scr[gi] = st_new

    y = y_scr[...]
    inv_n = 1.0 / RW_HD
    y_mu = _exact_rhs_dot(y, seg) * inv_n
    yc = y - y_mu
    y_var = _exact_rhs_dot(yc * yc, seg) * inv_n
    y = yc * lax.rsqrt(y_var + RW_GN_EPS) * lnx_g + lnx_b
    bonus = _exact_rhs_dot(r * k * rk, seg) * v
    o_ref[0] = ((y + bonus) * g).astype(BF16)

    @pl.when(c == nc - 1)
    def _():
        sf_ref[0] = st_scr[...]


def _rwkv_prompt(p3, lp, v_first, passes=(3, 3, 3, 3)):
    b, t, _ = p3.shape
    nc = t // RW_TB
    has_vres = v_first is not None
    ng = RW_HEADS // RW_GH
    zpad = jnp.zeros((RW_LORA_W, RW_W), F32)
    wa = jnp.concatenate([jnp.concatenate([lp['rw_w2'], zpad], 0), jnp.concatenate([zpad, lp['rw_a2']], 0)], 1)
    vec = jnp.stack([lp['rw_w0'], lp['rw_a0'], lp['rw_kk'], lp['rw_ka'], lp['rw_rk'], lp['rw_lnx_g'], lp['rw_lnx_b'],
                     lp['rw_v0'] if has_vres else jnp.zeros((RW_W,), F32)])
    hid = jnp.arange(RW_W) // RW_HD
    seg = (hid[:, None] == hid[None, :]).astype(BF16)
    full = lambda shape: pl.BlockSpec(shape, lambda i, j: (0,) * len(shape))
    seq = lambda w, blk: pl.BlockSpec((1, RW_TB, w), lambda i, j: (i, j, blk))
    in_specs = [seq(RW_P, 0)]
    args = [p3]
    if has_vres:
        in_specs += [seq(LANES, P_V1 // LANES), seq(RW_W, 0)]
        args += [p3, v_first]
    in_specs += [full((1, RW_P)), full((RW_VEC_ROWS, RW_W)), full((LANES, 2 * RW_W)), full((RW_LORA_G, RW_W)),
                 full((RW_W, RW_W))]
    args += [lp['rw_mu'].reshape(1, RW_P), vec, wa.astype(BF16), lp['rw_g2'].astype(BF16), seg]
    if has_vres:
        in_specs += [full((1, LANES)), full((LANES, RW_W))]
        args += [jnp.pad(lp['rw_vmu'], (0, LANES - RW_LORA_V)).reshape(1, LANES),
                 jnp.pad(lp['rw_v2'], ((0, LANES - RW_LORA_V), (0, 0))).astype(BF16)]
    out_specs = [seq(RW_W, 0)]
    out_shape = [jax.ShapeDtypeStruct((b, t, RW_W), BF16)]
    if not has_vres:
        out_specs.append(seq(RW_W, 0))
        out_shape.append(jax.ShapeDtypeStruct((b, t, RW_W), F32))
    out_specs.append(pl.BlockSpec((1, ng, RW_GW, RW_GW), lambda i, j: (i, 0, 0, 0)))
    out_shape.append(jax.ShapeDtypeStruct((b, ng, RW_GW, RW_GW), F32))
    scratch = [pltpu.VMEM((ng, RW_GW, RW_GW), F32), pltpu.VMEM((1, RW_P), F32), pltpu.VMEM((RW_TB, RW_W), F32)]
    if has_vres:
        scratch.append(pltpu.VMEM((1, LANES), F32))
    outs = pl.pallas_call(
        functools.partial(_rwkv_fused_body, nc=nc, has_vres=has_vres, passes=passes),
        grid=(b, nc),
        in_specs=in_specs,
        out_specs=out_specs,
        out_shape=out_shape,
        scratch_shapes=scratch,
        compiler_params=_cparams(("parallel", "arbitrary")),
        name="rwkv_fused",
    )(*args)
    if has_vres:
        o, st_bd = outs
    else:
        o, v_first, st_bd = outs
    st5 = st_bd.reshape(b, ng, RW_GH, RW_HD, RW_GH, RW_HD)
    s_fin = jnp.stack([st5[:, :, h, :, h, :] for h in range(RW_GH)], axis=2)
    s_fin = s_fin.reshape(b, RW_HEADS, RW_HD, RW_HD).transpose(0, 1, 3, 2)
    return o, v_first, s_fin


def _ret_log_gamma(h):
    return math.log1p(-(2.0 ** (-5.0 - h)))


def _rotary_tables(pos, heads, dk):
    half = dk // 2
    inv = ROPE_BASE ** (-jnp.arange(half, dtype=F32) / half)
    ang = pos.astype(F32)[:, None] * inv[None, :]
    cos = jnp.tile(jnp.concatenate([jnp.cos(ang), jnp.cos(ang)], -1), (1, heads))
    sin = jnp.tile(jnp.concatenate([-jnp.sin(ang), jnp.sin(ang)], -1), (1, heads))
    lane = jnp.arange(heads * dk)
    perm = (lane[:, None] == (lane[None, :] ^ half)).astype(BF16)
    return cos, sin, perm


def _ret_fused_body(p_ref, cos_ref, sin_ref, perm_ref, o_ref, sf_ref, s_scr, *, nc):
    L, DK, DV, H = CHUNK, RET_QK, RET_V, RET_HEADS
    nq = H * DK
    c = pl.program_id(1)

    @pl.when(c == 0)
    def _():
        s_scr[...] = jnp.zeros_like(s_scr)

    cos = cos_ref[...]
    sin = sin_ref[...]
    perm = perm_ref[...]
    rot = lambda x: x * cos + _exact_rhs_dot(x, perm) * sin
    q_all = rot(p_ref[0, :, 0:nq])
    k_all = rot(p_ref[0, :, nq:2 * nq]) * (DK ** -0.5)
    row = lax.broadcasted_iota(jnp.int32, (L, L), 0)
    col = lax.broadcasted_iota(jnp.int32, (L, L), 1)
    rel = (row - col).astype(F32)
    idx = lax.broadcasted_iota(jnp.int32, (L, 1), 0).astype(F32)
    for h in range(H):
        lg = _ret_log_gamma(h)
        q = q_all[:, h * DK:(h + 1) * DK]
        k = k_all[:, h * DK:(h + 1) * DK]
        v = p_ref[0, :, 2 * nq + h * DV:2 * nq + (h + 1) * DV].astype(BF16)
        gate = p_ref[0, :, 2 * nq + RET_W + h * DV:2 * nq + RET_W + (h + 1) * DV]
        dmask = jnp.where(rel >= 0, jnp.exp(jnp.maximum(rel, 0.0) * lg), 0.0)
        scores = _dot_nt(q.astype(BF16), k.astype(BF16)) * dmask
        s_prev = s_scr[h]
        q_dec = q * jnp.exp((idx + 1.0) * lg)
        y = _dot(scores.astype(BF16), v) + _dot(q_dec.astype(BF16), s_prev.astype(BF16))
        k_end = k * jnp.exp((L - 1.0 - idx) * lg)
        s_scr[h] = math.exp(L * lg) * s_prev + _dot_tn(k_end.astype(BF16), v)
        y = y * lax.rsqrt(jnp.mean(y * y, -1, keepdims=True) + NORM_EPS)
        o_ref[0, :, h * DV:(h + 1) * DV] = (gate * jax.nn.sigmoid(gate) * y).astype(BF16)

    @pl.when(c == nc - 1)
    def _():
        sf_ref[0] = s_scr[...]


def _ret_prompt(p3, pos):
    b, t, _ = p3.shape
    L = CHUNK
    nc = t // L
    nq = RET_HEADS * RET_QK
    cos, sin, perm = _rotary_tables(pos, RET_HEADS, RET_QK)
    tab = pl.BlockSpec((L, nq), lambda i, j: (j, 0))
    st = pl.BlockSpec((1, RET_HEADS, RET_QK, RET_V), lambda i, j: (i, 0, 0, 0))
    return pl.pallas_call(
        functools.partial(_ret_fused_body, nc=nc),
        grid=(b, nc),
        in_specs=[pl.BlockSpec((1, L, P_MAIN), lambda i, j: (i, j, P_RET // P_MAIN)), tab, tab,
                  pl.BlockSpec((nq, nq), lambda i, j: (0, 0))],
        out_specs=[pl.BlockSpec((1, L, RET_W), lambda i, j: (i, j, 0)), st],
        out_shape=[jax.ShapeDtypeStruct((b, t, RET_W), BF16),
                   jax.ShapeDtypeStruct((b, RET_HEADS, RET_QK, RET_V), F32)],
        scratch_shapes=[pltpu.VMEM((RET_HEADS, RET_QK, RET_V), F32)],
        compiler_params=_cparams(("parallel", "arbitrary")),
        name="ret_fused",
    )(p3, cos, sin, perm)


ML_HPAD = 8


def _ml_fused_body(p_ref, gate_ref, bias_ref, norm_ref, o_ref, cf_ref, nf_ref, mf_ref, c_scr, n_scr, m_scr, *, nc):
    L, DK, DV, H = CHUNK, ML_QK, ML_V, ML_HEADS
    nq = H * DK
    ci = pl.program_id(1)

    @pl.when(ci == 0)
    def _():
        c_scr[...] = jnp.zeros_like(c_scr)
        n_scr[...] = jnp.zeros_like(n_scr)
        m_scr[...] = jnp.zeros_like(m_scr)

    row = lax.broadcasted_iota(jnp.int32, (L, L), 0)
    col = lax.broadcasted_iota(jnp.int32, (L, L), 1)
    causal = row >= col
    tri = causal.astype(BF16)
    capped = ML_GATE_CAP * jnp.tanh((gate_ref[0] + bias_ref[...]) * (1.0 / ML_GATE_CAP))
    lane = lax.broadcasted_iota(jnp.int32, (L, LANES), 1)
    g = jnp.where(lane < H, capped, jnp.where(lane < 2 * H, -_softplus(-capped), 0.0))
    cum = _exact_lhs_dot(tri, g)
    g_t = g.T
    cum_t = cum.T
    for h in range(H):
        q = p_ref[0, :, h * DK:(h + 1) * DK]
        k = p_ref[0, :, nq + h * DK:nq + (h + 1) * DK] * (DK ** -0.5)
        v = p_ref[0, :, 2 * nq + h * DV:2 * nq + (h + 1) * DV].astype(BF16)
        og = p_ref[0, :, 2 * nq + ML_W + h * DV:2 * nq + ML_W + (h + 1) * DV]
        ig_col = g[:, h:h + 1]
        ig_row = g_t[h:h + 1, :]
        b_col = cum[:, H + h:H + h + 1]
        b_row = cum_t[H + h:H + h + 1, :]
        b_tot = cum[L - 1:L, H + h:H + h + 1]
        m_prev = m_scr[h:h + 1, 0:1]
        c_prev = c_scr[h]
        n_prev = n_scr[h:h + 1, :]
        a_row = b_tot - b_row + ig_row
        a_col = b_tot - b_col + ig_col
        m_new = jnp.maximum(b_tot + m_prev, jnp.max(a_row, axis=1, keepdims=True))
        dec = jnp.exp(b_tot + m_prev - m_new)
        kw = k * jnp.exp(a_col - m_new)
        c_scr[h] = dec * c_prev + _dot_tn(kw.astype(BF16), v)
        n_scr[h:h + 1, :] = dec * n_prev + jnp.sum(kw, axis=0, keepdims=True)
        m_scr[h:h + 1, :] = jnp.broadcast_to(m_new, (1, LANES))
        dlog = jnp.where(causal, b_col - b_row + ig_row, -jnp.inf)
        inter = b_col + m_prev
        m_i = jnp.maximum(jnp.max(dlog, axis=1, keepdims=True), inter)
        s = _dot_nt(q.astype(BF16), k.astype(BF16)) * jnp.exp(dlog - m_i)
        sc = jnp.exp(inter - m_i)
        num = _dot(s.astype(BF16), v) + sc * _dot(q.astype(BF16), c_prev.astype(BF16))
        den = jnp.sum(s, axis=1, keepdims=True) + sc * jnp.sum(q * n_prev, axis=1, keepdims=True)
        hid = num / jnp.maximum(jnp.abs(den), jnp.exp(-m_i))
        hid = hid * lax.rsqrt(jnp.mean(hid * hid, -1, keepdims=True) + NORM_EPS)
        o_ref[0, :, h * DV:(h + 1) * DV] = (jax.nn.sigmoid(og) * (hid * norm_ref[:, h * DV:(h + 1) * DV])).astype(BF16)

    @pl.when(ci == nc - 1)
    def _():
        cf_ref[0] = c_scr[...]
        nf_ref[0] = n_scr[...]
        mf_ref[0] = m_scr[...]


def _ml_prompt(p3, lp):
    b, t, _ = p3.shape
    L = CHUNK
    nc = t // L
    bias = jnp.pad(jnp.concatenate([lp['ml_ib'], lp['ml_fb']]), (0, LANES - 2 * ML_HEADS)).reshape(1, LANES)
    vs = pl.BlockSpec((1, L, ML_W), lambda i, j: (i, j, 0))
    cs = pl.BlockSpec((1, ML_HEADS, ML_QK, ML_V), lambda i, j: (i, 0, 0, 0))
    ns = pl.BlockSpec((1, ML_HPAD, ML_QK), lambda i, j: (i, 0, 0))
    ms = pl.BlockSpec((1, ML_HPAD, LANES), lambda i, j: (i, 0, 0))
    o, c_f, n_f, m_f = pl.pallas_call(
        functools.partial(_ml_fused_body, nc=nc),
        grid=(b, nc),
        in_specs=[pl.BlockSpec((1, L, P_MAIN), lambda i, j: (i, j, P_ML // P_MAIN)),
                  pl.BlockSpec((1, L, LANES), lambda i, j: (i, j, P_GATE // LANES)),
                  pl.BlockSpec((1, LANES), lambda i, j: (0, 0)),
                  pl.BlockSpec((1, ML_W), lambda i, j: (0, 0))],
        out_specs=[vs, cs, ns, ms],
        out_shape=[jax.ShapeDtypeStruct((b, t, ML_W), BF16),
                   jax.ShapeDtypeStruct((b, ML_HEADS, ML_QK, ML_V), F32),
                   jax.ShapeDtypeStruct((b, ML_HPAD, ML_QK), F32),
                   jax.ShapeDtypeStruct((b, ML_HPAD, LANES), F32)],
        scratch_shapes=[pltpu.VMEM((ML_HEADS, ML_QK, ML_V), F32),
                        pltpu.VMEM((ML_HPAD, ML_QK), F32),
                        pltpu.VMEM((ML_HPAD, LANES), F32)],
        compiler_params=_cparams(("parallel", "arbitrary")),
        name="ml_fused",
    )(p3, p3, bias, lp['ml_norm'].reshape(1, ML_W))
    return o, c_f, n_f[:, :ML_HEADS], m_f[:, :ML_HEADS, 0]


def _to_cols(x):
    b, c = x.shape
    return x.reshape(b // DEC_TB, DEC_TB, c).transpose(0, 2, 1)


def _rwkv_step_body(w_ref, ap_ref, k_ref, kk_ref, r_ref, v_ref, s_ref, y_ref, so_ref):
    N = RW_HD
    for j in range(DEC_TB):
        for h in range(RW_HEADS):
            cs = slice(h * N, (h + 1) * N)
            col = lambda ref: ref[0, cs, j:j + 1]
            s = s_ref[j, h]
            v = v_ref[j:j + 1, cs]
            sa = jnp.sum(col(kk_ref) * s, axis=0, keepdims=True)
            s_new = col(w_ref) * s - col(ap_ref) * sa + col(k_ref) * v
            so_ref[j, h] = s_new
            y_ref[j:j + 1, cs] = jnp.sum(col(r_ref) * s_new, axis=0, keepdims=True)


def _rwkv_step(wdec, ap, k, kk, r, v, s0):
    b, w = v.shape
    cols = pl.BlockSpec((1, w, DEC_TB), lambda i: (i, 0, 0))
    rows = pl.BlockSpec((DEC_TB, w), lambda i: (i, 0))
    st = pl.BlockSpec((DEC_TB, RW_HEADS, RW_HD, RW_HD), lambda i: (i, 0, 0, 0))
    return pl.pallas_call(
        _rwkv_step_body,
        grid=(b // DEC_TB,),
        in_specs=[cols] * 5 + [rows, st],
        out_specs=[rows, st],
        out_shape=[jax.ShapeDtypeStruct((b, w), F32), jax.ShapeDtypeStruct(s0.shape, F32)],
        compiler_params=_cparams(("parallel",)),
        name="rwkv_step",
    )(_to_cols(wdec), _to_cols(ap), _to_cols(k), _to_cols(kk), _to_cols(r), v, s0)


def _ret_step_body(qc_ref, kc_ref, v_ref, s_ref, y_ref, so_ref):
    DK, DV = RET_QK, RET_V
    for j in range(DEC_TB):
        for h in range(RET_HEADS):
            gamma = math.exp(_ret_log_gamma(h))
            q = qc_ref[0, h * DK:(h + 1) * DK, j:j + 1]
            k = kc_ref[0, h * DK:(h + 1) * DK, j:j + 1]
            v = v_ref[j:j + 1, h * DV:(h + 1) * DV]
            s = s_ref[j, h]
            qk = jnp.sum(q * k, axis=0, keepdims=True)
            y_ref[j:j + 1, h * DV:(h + 1) * DV] = qk * v + gamma * jnp.sum(q * s, axis=0, keepdims=True)
            so_ref[j, h] = gamma * s + k * v


def _ret_step(q, k, v, s0):
    b = q.shape[0]
    cols = pl.BlockSpec((1, RET_HEADS * RET_QK, DEC_TB), lambda i: (i, 0, 0))
    rows = pl.BlockSpec((DEC_TB, RET_W), lambda i: (i, 0))
    st = pl.BlockSpec((DEC_TB, RET_HEADS, RET_QK, RET_V), lambda i: (i, 0, 0, 0))
    return pl.pallas_call(
        _ret_step_body,
        grid=(b // DEC_TB,),
        in_specs=[cols, cols, rows, st],
        out_specs=[rows, st],
        out_shape=[jax.ShapeDtypeStruct((b, RET_W), F32), jax.ShapeDtypeStruct(s0.shape, F32)],
        compiler_params=_cparams(("parallel",)),
        name="ret_step",
    )(_to_cols(q), _to_cols(k), v, s0)


def _ml_step_body(qc_ref, kc_ref, q_ref, k_ref, v_ref, ig_ref, lf_ref, c_ref, n_ref, m_ref,
                  h_ref, co_ref, no_ref, mo_ref):
    DK, DV = ML_QK, ML_V
    for j in range(DEC_TB):
        for h in range(ML_HEADS):
            ks = slice(h * DK, (h + 1) * DK)
            vs = slice(h * DV, (h + 1) * DV)
            q_col = qc_ref[0, ks, j:j + 1]
            k_col = kc_ref[0, ks, j:j + 1]
            q_row = q_ref[j:j + 1, ks]
            k_row = k_ref[j:j + 1, ks]
            v = v_ref[j:j + 1, vs]
            ig = ig_ref[j:j + 1, h:h + 1]
            lf = lf_ref[j:j + 1, h:h + 1]
            m_prev = m_ref[j:j + 1, h:h + 1]
            c_prev = c_ref[j, h]
            n_prev = n_ref[j, h:h + 1, :]
            m_new = jnp.maximum(lf + m_prev, ig)
            dec = jnp.exp(lf + m_prev - m_new)
            wgt = jnp.exp(ig - m_new)
            co_ref[j, h] = dec * c_prev + (k_col * wgt) * v
            no_ref[j, h:h + 1, :] = dec * n_prev + k_row * wgt
            mo_ref[j:j + 1, h:h + 1] = m_new
            s = jnp.sum(q_row * k_row, axis=1, keepdims=True) * wgt
            num = s * v + dec * jnp.sum(q_col * c_prev, axis=0, keepdims=True)
            den = s + dec * jnp.sum(q_row * n_prev, axis=1, keepdims=True)
            h_ref[j:j + 1, vs] = num / jnp.maximum(jnp.abs(den), jnp.exp(-m_new))


def _ml_step(q, k, v, ig, lf, c0, n0, m0):
    b = q.shape[0]
    cols = pl.BlockSpec((1, ML_HEADS * ML_QK, DEC_TB), lambda i: (i, 0, 0))
    qk_rows = pl.BlockSpec((DEC_TB, ML_HEADS * ML_QK), lambda i: (i, 0))
    rows = pl.BlockSpec((DEC_TB, ML_W), lambda i: (i, 0))
    sc = pl.BlockSpec((DEC_TB, ML_HEADS), lambda i: (i, 0))
    cs = pl.BlockSpec((DEC_TB, ML_HEADS, ML_QK, ML_V), lambda i: (i, 0, 0, 0))
    ns = pl.BlockSpec((DEC_TB, ML_HEADS, ML_QK), lambda i: (i, 0, 0))
    return pl.pallas_call(
        _ml_step_body,
        grid=(b // DEC_TB,),
        in_specs=[cols, cols, qk_rows, qk_rows, rows, sc, sc, cs, ns, sc],
        out_specs=[rows, cs, ns, sc],
        out_shape=[jax.ShapeDtypeStruct((b, ML_W), F32), jax.ShapeDtypeStruct(c0.shape, F32),
                   jax.ShapeDtypeStruct(n0.shape, F32), jax.ShapeDtypeStruct(m0.shape, F32)],
        compiler_params=_cparams(("parallel",)),
        name="ml_step",
    )(_to_cols(q), _to_cols(k), q, k, v, ig, lf, c0, n0, m0)


def _heads(a, h):
    return a.reshape(a.shape[:-1] + (h, a.shape[-1] // h))


def _shift_prev(p, prev_row):
    return jnp.concatenate([prev_row[:, None, :], p[:, :-1]], axis=1)


def _rotary(x, pos):
    half = x.shape[-1] // 2
    inv = ROPE_BASE ** (-jnp.arange(half, dtype=F32) / half)
    ang = pos.astype(F32)[:, None] * inv[None, :]
    cos = jnp.cos(ang)[None, :, None, :]
    sin = jnp.sin(ang)[None, :, None, :]
    x1, x2 = x[..., :half], x[..., half:]
    return jnp.concatenate([x1 * cos - x2 * sin, x1 * sin + x2 * cos], -1)


def _small_matmul(x, w):
    lead = x.shape[:-1]
    kdim, n = w.shape
    x2 = x.reshape(-1, kdim)
    m = x2.shape[0]
    kp = -(-kdim // LANES) * LANES
    npad = -(-n // LANES) * LANES
    x2 = jnp.pad(x2.astype(BF16), ((0, 0), (0, kp - kdim)))
    w2 = jnp.pad(w.astype(BF16), ((0, kp - kdim), (0, npad - n)))
    tm = 1024 if m % 1024 == 0 else m
    out = _matmul(x2, w2, tm, npad)
    return out[:, :n].reshape(lead + (n,))


def _mix_prompt(p, pos, v_first, lp):
    o_rw, v_first, s_new = _rwkv_prompt(p, lp, v_first, (1, 1, 1, 1))
    o_ret, r_new = _ret_prompt(p, pos)
    o_ml, c_new, n_new, m_new = _ml_prompt(p, lp)
    return (o_rw, o_ret, o_ml), v_first, (s_new, r_new, c_new, n_new, m_new)


def _mix_sample(p, pos, v_first, st, lp, prev_row):
    bsz, t_len, _ = p.shape

    p_rw = p[..., :RW_P]
    mixed = p_rw + (_shift_prev(p_rw, prev_row[:, :RW_P]) - p_rw) * lp['rw_mu']
    sizes = np.cumsum([RW_W, RW_W, RW_W, RW_LORA_W, RW_LORA_A, RW_LORA_G])[:-1]
    r, k, v, xw, xa, xg = jnp.split(mixed, [int(s) for s in sizes], axis=-1)
    w = -jax.nn.softplus(-(lp['rw_w0'] + _small_matmul(jnp.tanh(xw), lp['rw_w2']))) - 0.5
    a = jax.nn.sigmoid(lp['rw_a0'] + _small_matmul(xa, lp['rw_a2']))
    g = _small_matmul(jax.nn.sigmoid(xg), lp['rw_g2'])
    if v_first is None:
        v_first = v
    else:
        pv = p[..., P_V1:P_V1 + RW_LORA_V]
        xv = pv + (_shift_prev(pv, prev_row[:, P_V1:P_V1 + RW_LORA_V]) - pv) * lp['rw_vmu']
        v = v + (v_first - v) * jax.nn.sigmoid(lp['rw_v0'] + _small_matmul(xv, lp['rw_v2']))
    kk = _heads(k * lp['rw_kk'], RW_HEADS)
    kk = kk * lax.rsqrt(jnp.maximum(jnp.sum(jnp.square(kk), -1, keepdims=True), 1e-24))
    kk = kk.reshape(bsz, t_len, RW_W)
    k = k * (1.0 + (a - 1.0) * lp['rw_ka'])
    lw = -jnp.exp(w)
    y, s_new = _rwkv_step(jnp.exp(lw)[:, 0], (kk * a)[:, 0], k[:, 0], kk[:, 0], r[:, 0], v[:, 0], st['rw_wkv'])
    y = _heads(y[:, None, :], RW_HEADS)
    y_mu = jnp.mean(y, -1, keepdims=True)
    y_var = jnp.mean(jnp.square(y - y_mu), -1, keepdims=True)
    y = ((y - y_mu) * lax.rsqrt(y_var + RW_GN_EPS)).reshape(bsz, t_len, RW_W)
    y = y * lp['rw_lnx_g'] + lp['rw_lnx_b']
    rh, kh, vh = (_heads(u, RW_HEADS) for u in (r, k, v))
    bonus = jnp.sum(rh * kh * _heads(lp['rw_rk'], RW_HEADS), -1, keepdims=True) * vh
    o_rw = ((y + bonus.reshape(bsz, t_len, RW_W)) * g).astype(BF16)

    nqk = RET_HEADS * RET_QK
    p_ret = p[..., P_RET:P_RET + P_MAIN]
    qr, kr, vr, gr = (p_ret[..., :nqk], p_ret[..., nqk:2 * nqk],
                      p_ret[..., 2 * nqk:2 * nqk + RET_W], p_ret[..., 2 * nqk + RET_W:])
    qh = _rotary(_heads(qr, RET_HEADS), pos).reshape(bsz, t_len, nqk)
    khr = (_rotary(_heads(kr, RET_HEADS), pos) * (RET_QK ** -0.5)).reshape(bsz, t_len, nqk)
    yr, r_new = _ret_step(qh[:, 0], khr[:, 0], vr[:, 0], st['ret'])
    yr = _heads(yr[:, None, :], RET_HEADS)
    yr = yr * lax.rsqrt(jnp.mean(jnp.square(yr), -1, keepdims=True) + NORM_EPS)
    o_ret = (jax.nn.silu(gr) * yr.reshape(bsz, t_len, RET_W)).astype(BF16)

    nqk = ML_HEADS * ML_QK
    p_ml = p[..., P_ML:P_ML + P_MAIN]
    qm, km, vm, om = (p_ml[..., :nqk], p_ml[..., nqk:2 * nqk],
                      p_ml[..., 2 * nqk:2 * nqk + ML_W], p_ml[..., 2 * nqk + ML_W:])
    im = p[..., P_GATE:P_GATE + ML_HEADS]
    fm = p[..., P_GATE + ML_HEADS:P_GATE + 2 * ML_HEADS]
    ig = ML_GATE_CAP * jnp.tanh((im + lp['ml_ib']) / ML_GATE_CAP)
    lf = jax.nn.log_sigmoid(ML_GATE_CAP * jnp.tanh((fm + lp['ml_fb']) / ML_GATE_CAP))
    km = km * (ML_QK ** -0.5)
    hm, c_new, n_new, m_new = _ml_step(qm[:, 0], km[:, 0], vm[:, 0], ig[:, 0], lf[:, 0],
                                       st['ml_c'], st['ml_n'], st['ml_m'])
    hm = _heads(hm[:, None, :], ML_HEADS)
    hm = hm * lax.rsqrt(jnp.mean(jnp.square(hm), -1, keepdims=True) + NORM_EPS)
    o_ml = (jax.nn.sigmoid(om) * (hm.reshape(bsz, t_len, ML_W) * lp['ml_norm'])).astype(BF16)

    return (o_rw, o_ret, o_ml), v_first, (s_new, r_new, c_new, n_new, m_new)


_T_V1 = P_V1 // LANES
_T_GATE = P_GATE // LANES
_T_RET = P_RET // LANES
_T_SHIFT = (P_RET - RW_P) // LANES
_T_SRC_GATE = (RW_P + RET_P + 2 * ML_HEADS * ML_QK + 2 * ML_W) // LANES


def _pack_body(w_ref, v1_ref, o_ref):
    j = pl.program_id(0)
    w = w_ref[...]
    lane = lax.broadcasted_iota(jnp.int32, (1, LANES), 1)
    gates = jnp.where(lane < 2 * ML_HEADS, w, 0.0)
    spare = jnp.logical_and(j > _T_GATE, j < _T_RET)
    out = jnp.where(j == _T_V1, v1_ref[...], jnp.where(j == _T_GATE, gates, jnp.where(spare, 0.0, w)))
    o_ref[...] = out.astype(BF16)


def _pack_w_in(w_in, layer, rw_v1_l):
    d = w_in.shape[1]
    v1 = (jnp.zeros((d, LANES), F32) if rw_v1_l is None
          else jnp.pad(rw_v1_l, ((0, 0), (0, LANES - RW_LORA_V))))

    def src_tile(j):
        return jnp.where(j < _T_V1, j, jnp.where(j == _T_GATE, _T_SRC_GATE, j - _T_SHIFT))

    return pl.pallas_call(
        _pack_body,
        grid=(P_PAD // LANES,),
        in_specs=[pl.BlockSpec((None, d, LANES), lambda j: (layer, 0, src_tile(j))),
                  pl.BlockSpec((d, LANES), lambda j: (0, 0))],
        out_specs=pl.BlockSpec((d, LANES), lambda j: (0, j)),
        out_shape=jax.ShapeDtypeStruct((d, P_PAD), BF16),
        compiler_params=_cparams(("parallel",)),
        name="pack_w_in",
    )(w_in, v1)


def _token_tiles(m):
    if m % 2048 == 0:
        return 2048, 512, 512
    return m, m, m


def kernel(x_prompt, x_sample, state_rw_shift, state_rw_wkv, state_ret, state_ml_c, state_ml_n, state_ml_m,
           ln0_g, ln0_b, w_in, rw_mu, rw_w0, rw_w2, rw_a0, rw_a2, rw_g2, rw_kk, rw_ka, rw_rk,
           rw_lnx_g, rw_lnx_b, rw_v0, rw_v1, rw_vmu, rw_v2, ml_ib, ml_fb, ml_norm, w_out,
           ln1_g, ln1_b, w_gate, w_up, w_down, ln2_g, ln2_b):
    bp, tp, d = x_prompt.shape
    bs, ts, _ = x_sample.shape
    groups = {
        'p': dict(b=bp, t=tp, pos=jnp.arange(tp), chunked=True),
        's': dict(b=bs, t=ts, pos=PAST_LEN + jnp.arange(ts), chunked=False),
    }
    xs_f, xs_b, v_first, outs = {}, {}, {}, {}
    for name, x in (('p', x_prompt), ('s', x_sample)):
        m = x.shape[0] * x.shape[1]
        xs_f[name], xs_b[name] = _layernorm(x.reshape(m, d), ln0_g, ln0_b, _token_tiles(m)[1])
        v_first[name] = None
        outs[name] = []

    for l in range(DEPTH):
        lp = {
            'rw_mu': rw_mu[l], 'rw_w0': rw_w0[l], 'rw_w2': rw_w2[l], 'rw_a0': rw_a0[l], 'rw_a2': rw_a2[l],
            'rw_g2': rw_g2[l], 'rw_kk': rw_kk[l], 'rw_ka': rw_ka[l], 'rw_rk': rw_rk[l],
            'rw_lnx_g': rw_lnx_g[l], 'rw_lnx_b': rw_lnx_b[l], 'ml_ib': ml_ib[l], 'ml_fb': ml_fb[l],
            'ml_norm': ml_norm[l],
        }
        if l > 0:
            lp.update(rw_v0=rw_v0[l - 1], rw_vmu=rw_vmu[l - 1], rw_v2=rw_v2[l - 1])
        w_in_b = _pack_w_in(w_in, l, rw_v1[l - 1] if l > 0 else None)
        w_out_b = w_out[l].astype(BF16)
        w_gate_b = w_gate[l].astype(BF16)
        w_up_b = w_up[l].astype(BF16)
        w_down_b = w_down[l].astype(BF16)
        prev_s = _matmul(state_rw_shift[l].astype(BF16), w_in_b, bs, P_TN)
        for name, grp in groups.items():
            b, t = grp['b'], grp['t']
            m = b * t
            tm_big, tm_out, tm_down = _token_tiles(m)
            p = _matmul(xs_b[name], w_in_b, tm_big, P_TN).reshape(b, t, P_PAD)
            if grp['chunked']:
                o, v_first[name], new_st = _mix_prompt(p, grp['pos'], v_first[name], lp)
            else:
                st = {'rw_wkv': state_rw_wkv[l], 'ret': state_ret[l], 'ml_c': state_ml_c[l],
                      'ml_n': state_ml_n[l], 'ml_m': state_ml_m[l]}
                o, v_first[name], new_st = _mix_sample(p, grp['pos'], v_first[name], st, lp, prev_s)
            outs[name].append((xs_f[name].reshape(b, t, d)[:, -1],) + new_st)
            o_rw, o_ret, o_ml = (u.reshape(m, u.shape[-1]) for u in o)
            x1_f, x1_b = _out_proj_ln(o_rw, o_ret, o_ml, w_out_b, xs_f[name], ln1_g[l], ln1_b[l], tm_out)
            hdn = _matmul_swiglu(x1_b, w_gate_b, w_up_b, tm_big, 512)
            xs_f[name], xs_b[name] = _matmul_res_ln(hdn, w_down_b, x1_f, ln2_g[l], ln2_b[l], tm_down, 512)

    y_p = xs_f['p'].reshape(bp, tp, d)
    y_s = xs_f['s'].reshape(bs, ts, d)
    sp = [jnp.stack([o[i] for o in outs['p']]) for i in range(6)]
    ss = [jnp.stack([o[i] for o in outs['s']]) for i in range(6)]
    return (y_p, y_s, sp[0], sp[1], sp[2], sp[3], sp[4], sp[5], ss[0], ss[1], ss[2], ss[3], ss[4], ss[5])
```

```python
import functools
import math

import numpy as np
import jax
import jax.numpy as jnp
from jax import lax
from jax.experimental import pallas as pl
from jax.experimental.pallas import tpu as pltpu

F32 = jnp.float32
BF16 = jnp.bfloat16

D_MODEL = 2048
DEPTH = 2
PAST_LEN = 16384
RW_HD = 64
RW_W = D_MODEL // 4
RW_HEADS = RW_W // RW_HD
RW_LORA_W = 64
RW_LORA_A = 64
RW_LORA_V = 32
RW_LORA_G = 128
RW_P = 3 * RW_W + RW_LORA_W + RW_LORA_A + RW_LORA_G
RW_GN_EPS = 64e-5
RET_V = 128
RET_QK = 64
RET_W = 3 * D_MODEL // 8
RET_HEADS = RET_W // RET_V
RET_P = 2 * RET_HEADS * RET_QK + 2 * RET_W
ML_V = 128
ML_QK = 64
ML_W = D_MODEL - RW_W - RET_W
ML_HEADS = ML_W // ML_V
ML_P = 2 * ML_HEADS * ML_QK + 2 * ML_W + 2 * ML_HEADS
ML_GATE_CAP = 15.0
P_TOTAL = RW_P + RET_P + ML_P
D_FF = ((8 * D_MODEL + 3 * 256 - 1) // (3 * 256)) * 256
CHUNK = 128
ROPE_BASE = 10000.0
LN_EPS = 1e-5
NORM_EPS = 1e-6
ALPHA = (2 * DEPTH) ** 0.25

LANES = 128
P_V1 = RW_P
P_GATE = RW_P + LANES
P_RET = 2304
P_ML = 2 * P_RET
P_MAIN = 2304
P_PAD = 3 * P_RET
P_TN = 768
RW_CHUNK = 64
DEC_TB = 8
VMEM_LIMIT = 56 * 1024 * 1024

HIGHEST = lax.Precision.HIGHEST


def _cparams(sem):
    return pltpu.CompilerParams(dimension_semantics=sem, vmem_limit_bytes=VMEM_LIMIT)


def _dot(a, b, precision=None):
    return lax.dot_general(a, b, (((1,), (0,)), ((), ())), precision=precision, preferred_element_type=F32)


def _dot_nt(a, b, precision=None):
    return lax.dot_general(a, b, (((1,), (1,)), ((), ())), precision=precision, preferred_element_type=F32)


def _dot_tn(a, b, precision=None):
    return lax.dot_general(a, b, (((0,), (0,)), ((), ())), precision=precision, preferred_element_type=F32)


def _ln_rows(x, g, b):
    mu = jnp.mean(x, -1, keepdims=True)
    xc = x - mu
    var = jnp.mean(xc * xc, -1, keepdims=True)
    return xc * lax.rsqrt(var + LN_EPS) * g + b


def _ln_body(x_ref, g_ref, b_ref, of_ref, ob_ref):
    y = _ln_rows(x_ref[...], g_ref[...], b_ref[...])
    of_ref[...] = y
    ob_ref[...] = y.astype(BF16)


def _layernorm(x, g, b, tm):
    m, d = x.shape
    return pl.pallas_call(
        _ln_body,
        grid=(m // tm,),
        in_specs=[pl.BlockSpec((tm, d), lambda i: (i, 0)),
                  pl.BlockSpec((1, d), lambda i: (0, 0)),
                  pl.BlockSpec((1, d), lambda i: (0, 0))],
        out_specs=[pl.BlockSpec((tm, d), lambda i: (i, 0)),
                   pl.BlockSpec((tm, d), lambda i: (i, 0))],
        out_shape=[jax.ShapeDtypeStruct((m, d), F32), jax.ShapeDtypeStruct((m, d), BF16)],
        compiler_params=_cparams(("parallel",)),
        name="layernorm",
    )(x, g.reshape(1, d), b.reshape(1, d))


def _mm_body(x_ref, w_ref, o_ref, *, w_transposed):
    dot = _dot_nt if w_transposed else _dot
    o_ref[...] = dot(x_ref[...], w_ref[...]).astype(o_ref.dtype)


def _matmul(x, w, tm, tn, out_dtype=F32, w_transposed=False):
    m, k = x.shape
    n = w.shape[0] if w_transposed else w.shape[1]
    w_spec = (pl.BlockSpec((tn, k), lambda i, j: (j, 0)) if w_transposed
              else pl.BlockSpec((k, tn), lambda i, j: (0, j)))
    return pl.pallas_call(
        functools.partial(_mm_body, w_transposed=w_transposed),
        grid=(m // tm, n // tn),
        in_specs=[pl.BlockSpec((tm, k), lambda i, j: (i, 0)), w_spec],
        out_specs=pl.BlockSpec((tm, tn), lambda i, j: (i, j)),
        out_shape=jax.ShapeDtypeStruct((m, n), out_dtype),
        compiler_params=_cparams(("parallel", "parallel")),
        name="matmul",
    )(x, w)


def _swiglu_body(x_ref, wg_ref, wu_ref, o_ref):
    x = x_ref[...]
    g = _dot(x, wg_ref[...])
    u = _dot(x, wu_ref[...])
    o_ref[...] = (g * jax.nn.sigmoid(g) * u).astype(o_ref.dtype)


def _matmul_swiglu(x, wg, wu, tm, tn):
    m, k = x.shape
    n = wg.shape[1]
    return pl.pallas_call(
        _swiglu_body,
        grid=(m // tm, n // tn),
        in_specs=[pl.BlockSpec((tm, k), lambda i, j: (i, 0)),
                  pl.BlockSpec((k, tn), lambda i, j: (0, j)),
                  pl.BlockSpec((k, tn), lambda i, j: (0, j))],
        out_specs=pl.BlockSpec((tm, tn), lambda i, j: (i, j)),
        out_shape=jax.ShapeDtypeStruct((m, n), BF16),
        compiler_params=_cparams(("parallel", "parallel")),
        name="matmul_swiglu",
    )(x, wg, wu)


def _mm_res_ln_body(x_ref, w_ref, res_ref, g_ref, b_ref, of_ref, ob_ref, acc_ref, *, nk):
    kk = pl.program_id(1)

    @pl.when(kk == 0)
    def _():
        acc_ref[...] = jnp.zeros_like(acc_ref)

    acc_ref[...] += _dot(x_ref[...], w_ref[...])

    @pl.when(kk == nk - 1)
    def _():
        y = _ln_rows(ALPHA * res_ref[...] + acc_ref[...], g_ref[...], b_ref[...])
        of_ref[...] = y
        ob_ref[...] = y.astype(BF16)


def _matmul_res_ln(x, w, res, g, b, tm, tk):
    m, k = x.shape
    n = w.shape[1]
    nk = k // tk
    return pl.pallas_call(
        functools.partial(_mm_res_ln_body, nk=nk),
        grid=(m // tm, nk),
        in_specs=[pl.BlockSpec((tm, tk), lambda i, j: (i, j)),
                  pl.BlockSpec((tk, n), lambda i, j: (j, 0)),
                  pl.BlockSpec((tm, n), lambda i, j: (i, 0)),
                  pl.BlockSpec((1, n), lambda i, j: (0, 0)),
                  pl.BlockSpec((1, n), lambda i, j: (0, 0))],
        out_specs=[pl.BlockSpec((tm, n), lambda i, j: (i, 0)),
                   pl.BlockSpec((tm, n), lambda i, j: (i, 0))],
        out_shape=[jax.ShapeDtypeStruct((m, n), F32), jax.ShapeDtypeStruct((m, n), BF16)],
        scratch_shapes=[pltpu.VMEM((tm, n), F32)],
        compiler_params=_cparams(("parallel", "arbitrary")),
        name="matmul_res_ln",
    )(x, w, res, g.reshape(1, n), b.reshape(1, n))


def _out_proj_ln_body(o_rw_ref, o_ret_ref, o_ml_ref, w_ref, res_ref, g_ref, b_ref, of_ref, ob_ref):
    mix = (_dot(o_rw_ref[...], w_ref[0:RW_W, :])
           + _dot(o_ret_ref[...], w_ref[RW_W:RW_W + RET_W, :])
           + _dot(o_ml_ref[...], w_ref[RW_W + RET_W:, :]))
    y = _ln_rows(ALPHA * res_ref[...] + mix, g_ref[...], b_ref[...])
    of_ref[...] = y
    ob_ref[...] = y.astype(BF16)


def _out_proj_ln(o_rw, o_ret, o_ml, w, res, g, b, tm):
    m = o_rw.shape[0]
    n = w.shape[1]
    rows = lambda width: pl.BlockSpec((tm, width), lambda i: (i, 0))
    full = lambda r, c: pl.BlockSpec((r, c), lambda i: (0, 0))
    return pl.pallas_call(
        _out_proj_ln_body,
        grid=(m // tm,),
        in_specs=[rows(RW_W), rows(RET_W), rows(ML_W), full(D_MODEL, n), rows(n), full(1, n), full(1, n)],
        out_specs=[rows(n), rows(n)],
        out_shape=[jax.ShapeDtypeStruct((m, n), F32), jax.ShapeDtypeStruct((m, n), BF16)],
        compiler_params=_cparams(("parallel",)),
        name="out_proj_ln",
    )(o_rw, o_ret, o_ml, w, res, g.reshape(1, n), b.reshape(1, n))


RW_TB = 256
RW_GH = 4
RW_GW = RW_GH * RW_HD
RW_VEC_ROWS = 8


def _split3(x):
    hi = x.astype(BF16)
    r1 = x - hi.astype(F32)
    mid = r1.astype(BF16)
    lo = (r1 - mid.astype(F32)).astype(BF16)
    return hi, mid, lo


def _mm(a, b, dims, passes):
    dg = lambda x, y: lax.dot_general(x, y, (dims, ((), ())), preferred_element_type=F32)
    if passes == 6:
        return lax.dot_general(a, b, (dims, ((), ())), precision=HIGHEST, preferred_element_type=F32)
    ah = a.astype(BF16)
    bh = b.astype(BF16)
    if passes == 1:
        return dg(ah, bh)
    al = (a - ah.astype(F32)).astype(BF16)
    bl = (b - bh.astype(F32)).astype(BF16)
    return dg(ah, bh) + (dg(ah, bl) + dg(al, bh))


_NN = ((1,), (0,))
_NT = ((1,), (1,))
_TN = ((0,), (0,))


def _exact_lhs_dot(a_bf16, b):
    hi, mid, lo = _split3(b)
    dg = lambda y: lax.dot_general(a_bf16, y, (_NN, ((), ())), preferred_element_type=F32)
    return dg(hi) + (dg(mid) + dg(lo))


def _exact_rhs_dot(a, b_bf16):
    hi, mid, lo = _split3(a)
    dg = lambda x: lax.dot_general(x, b_bf16, (_NN, ((), ())), preferred_element_type=F32)
    return dg(hi) + (dg(mid) + dg(lo))


def _rw_scan_block(r, lw, k, v, kk, a, st, masks, passes):
    L = RW_CHUNK
    tri, strict, lower, eye, mask_bd = masks
    p_sc, p_inv, p_app, p_st = passes
    bd = lambda x: jnp.where(mask_bd, jnp.concatenate([x] * RW_GH, axis=0), 0.0)
    cum = _exact_lhs_dot(tri, lw)
    tot = cum[L - 1:L, :]
    e_neg = jnp.exp(-cum)
    ap = kk * a
    ap_h = ap * e_neg
    k_h = k * e_neg
    lhs = jnp.concatenate([kk * jnp.exp(cum - lw), r * jnp.exp(cum)], axis=0)
    sc_a = _mm(lhs, bd(ap_h), _NT, p_sc)
    sc_k = _mm(lhs, bd(k_h), _NT, p_sc)
    n_m = jnp.where(strict, sc_a[:L], 0.0)
    m_a = jnp.where(lower, sc_a[L:], 0.0)
    m_k = jnp.where(strict, sc_k[:L], 0.0)
    m_r = jnp.where(lower, sc_k[L:], 0.0)
    inv = eye - n_m
    pw = n_m
    for _ in range(int(math.log2(L)) - 1):
        pw = _mm(pw, bd(pw), _NN, p_inv)
        inv = inv + _mm(inv, bd(pw), _NN, p_inv)
    s_terms = _mm(lhs, st, _NT, p_app)
    mv = _mm(jnp.concatenate([m_k, m_r], axis=0), bd(v), _NN, p_app)
    u = _mm(inv, bd(s_terms[:L] + mv[:L]), _NN, p_app)
    y = s_terms[L:] + mv[L:] - _mm(m_a, bd(u), _NN, p_app)
    e_end = jnp.exp(tot - cum)
    upd = _mm(jnp.concatenate([v, -u], axis=0), jnp.concatenate([k * e_end, ap * e_end], axis=0), _TN, p_st)
    st_new = jnp.where(mask_bd, st * jnp.exp(tot) + upd, 0.0)
    return y, st_new


def _rw_masks():
    L, G = RW_CHUNK, RW_GW
    row = lax.broadcasted_iota(jnp.int32, (L, G), 0)
    col = lax.broadcasted_iota(jnp.int32, (L, G), 1) & (L - 1)
    rl = lax.broadcasted_iota(jnp.int32, (L, L), 0)
    cl = lax.broadcasted_iota(jnp.int32, (L, L), 1)
    rg = lax.broadcasted_iota(jnp.int32, (G, G), 0) // RW_HD
    cg = lax.broadcasted_iota(jnp.int32, (G, G), 1) // RW_HD
    tri = (rl >= cl).astype(BF16)
    return tri, row > col, row >= col, (row == col).astype(F32), rg == cg


def _softplus(z):
    return jnp.maximum(z, 0.0) + jnp.log(1.0 + jnp.exp(-jnp.abs(z)))


def _rwkv_fused_body(*refs, nc, has_vres, passes):
    if has_vres:
        (p_ref, pv_ref, vf_ref, mu_ref, vec_ref, wa_ref, g2_ref, seg_ref, vmu_ref, v2_ref,
         o_ref, sf_ref, st_scr, prev_scr, y_scr, prevv_scr) = refs
    else:
        (p_ref, mu_ref, vec_ref, wa_ref, g2_ref, seg_ref,
         o_ref, vfo_ref, sf_ref, st_scr, prev_scr, y_scr) = refs
    TB, W, L = RW_TB, RW_W, RW_CHUNK
    c = pl.program_id(1)

    @pl.when(c == 0)
    def _():
        st_scr[...] = jnp.zeros_like(st_scr)
        prev_scr[...] = jnp.zeros_like(prev_scr)
        if has_vres:
            prevv_scr[...] = jnp.zeros_like(prevv_scr)

    first_row = lax.broadcasted_iota(jnp.int32, (TB, 1), 0) == 0

    def shift_mix(x, carry_ref, mu):
        prev = jnp.where(first_row, carry_ref[...], pltpu.roll(x, 1, 0))
        carry_ref[...] = x[TB - 1:TB, :]
        return x + (prev - x) * mu

    mixed = shift_mix(p_ref[0], prev_scr, mu_ref[...])
    r = mixed[:, 0:W]
    k = mixed[:, W:2 * W]
    v = mixed[:, 2 * W:3 * W]
    xwa = mixed[:, 3 * W:3 * W + LANES]
    xg = mixed[:, 3 * W + LANES:3 * W + 2 * LANES]
    vec = vec_ref[...]
    w0, a0, kk_s, ka, rk, lnx_g, lnx_b, v0 = (vec[i:i + 1, :] for i in range(RW_VEC_ROWS))
    seg = seg_ref[...]
    wa = wa_ref[...]
    w_lora = _dot(jnp.tanh(xwa).astype(BF16), wa[:, 0:W])
    a_lora = _dot(xwa.astype(BF16), wa[:, W:2 * W])
    lw = -jnp.exp(-_softplus(-(w0 + w_lora)) - 0.5)
    a = jax.nn.sigmoid(a0 + a_lora)
    g = _dot(jax.nn.sigmoid(xg).astype(BF16), g2_ref[...])
    if has_vres:
        xv = shift_mix(pv_ref[0], prevv_scr, vmu_ref[...])
        v = v + (vf_ref[0] - v) * jax.nn.sigmoid(v0 + _dot(xv.astype(BF16), v2_ref[...]))
    else:
        vfo_ref[0] = v
    kk = k * kk_s
    kk = kk * lax.rsqrt(jnp.maximum(_exact_rhs_dot(kk * kk, seg), 1e-24))
    k = k * (1.0 + (a - 1.0) * ka)

    masks = _rw_masks()
    for s in range(TB // L):
        rs = slice(s * L, (s + 1) * L)
        for gi in range(RW_HEADS // RW_GH):
            cs = slice(gi * RW_GW, (gi + 1) * RW_GW)
            y, st_new = _rw_scan_block(r[rs, cs], lw[rs, cs], k[rs, cs], v[rs, cs], kk[rs, cs], a[rs, cs],
                                       st_scr[gi], masks, passes)
            y_scr[rs, cs] = y
            st_scr[gi] = st_new

    y = y_scr[...]
    inv_n = 1.0 / RW_HD
    y_mu = _exact_rhs_dot(y, seg) * inv_n
    yc = y - y_mu
    y_var = _exact_rhs_dot(yc * yc, seg) * inv_n
    y = yc * lax.rsqrt(y_var + RW_GN_EPS) * lnx_g + lnx_b
    bonus = _exact_rhs_dot(r * k * rk, seg) * v
    o_ref[0] = ((y + bonus) * g).astype(BF16)

    @pl.when(c == nc - 1)
    def _():
        sf_ref[0] = st_scr[...]


def _rwkv_prompt(p3, lp, v_first, passes=(3, 3, 3, 3)):
    b, t, _ = p3.shape
    nc = t // RW_TB
    has_vres = v_first is not None
    ng = RW_HEADS // RW_GH
    zpad = jnp.zeros((RW_LORA_W, RW_W), F32)
    wa = jnp.concatenate([jnp.concatenate([lp['rw_w2'], zpad], 0), jnp.concatenate([zpad, lp['rw_a2']], 0)], 1)
    vec = jnp.stack([lp['rw_w0'], lp['rw_a0'], lp['rw_kk'], lp['rw_ka'], lp['rw_rk'], lp['rw_lnx_g'], lp['rw_lnx_b'],
                     lp['rw_v0'] if has_vres else jnp.zeros((RW_W,), F32)])
    hid = jnp.arange(RW_W) // RW_HD
    seg = (hid[:, None] == hid[None, :]).astype(BF16)
    full = lambda shape: pl.BlockSpec(shape, lambda i, j: (0,) * len(shape))
    seq = lambda w, blk: pl.BlockSpec((1, RW_TB, w), lambda i, j: (i, j, blk))
    in_specs = [seq(RW_P, 0)]
    args = [p3]
    if has_vres:
        in_specs += [seq(LANES, P_V1 // LANES), seq(RW_W, 0)]
        args += [p3, v_first]
    in_specs += [full((1, RW_P)), full((RW_VEC_ROWS, RW_W)), full((LANES, 2 * RW_W)), full((RW_LORA_G, RW_W)),
                 full((RW_W, RW_W))]
    args += [lp['rw_mu'].reshape(1, RW_P), vec, wa.astype(BF16), lp['rw_g2'].astype(BF16), seg]
    if has_vres:
        in_specs += [full((1, LANES)), full((LANES, RW_W))]
        args += [jnp.pad(lp['rw_vmu'], (0, LANES - RW_LORA_V)).reshape(1, LANES),
                 jnp.pad(lp['rw_v2'], ((0, LANES - RW_LORA_V), (0, 0))).astype(BF16)]
    out_specs = [seq(RW_W, 0)]
    out_shape = [jax.ShapeDtypeStruct((b, t, RW_W), BF16)]
    if not has_vres:
        out_specs.append(seq(RW_W, 0))
        out_shape.append(jax.ShapeDtypeStruct((b, t, RW_W), F32))
    out_specs.append(pl.BlockSpec((1, ng, RW_GW, RW_GW), lambda i, j: (i, 0, 0, 0)))
    out_shape.append(jax.ShapeDtypeStruct((b, ng, RW_GW, RW_GW), F32))
    scratch = [pltpu.VMEM((ng, RW_GW, RW_GW), F32), pltpu.VMEM((1, RW_P), F32), pltpu.VMEM((RW_TB, RW_W), F32)]
    if has_vres:
        scratch.append(pltpu.VMEM((1, LANES), F32))
    outs = pl.pallas_call(
        functools.partial(_rwkv_fused_body, nc=nc, has_vres=has_vres, passes=passes),
        grid=(b, nc),
        in_specs=in_specs,
        out_specs=out_specs,
        out_shape=out_shape,
        scratch_shapes=scratch,
        compiler_params=_cparams(("parallel", "arbitrary")),
        name="rwkv_fused",
    )(*args)
    if has_vres:
        o, st_bd = outs
    else:
        o, v_first, st_bd = outs
    st5 = st_bd.reshape(b, ng, RW_GH, RW_HD, RW_GH, RW_HD)
    s_fin = jnp.stack([st5[:, :, h, :, h, :] for h in range(RW_GH)], axis=2)
    s_fin = s_fin.reshape(b, RW_HEADS, RW_HD, RW_HD).transpose(0, 1, 3, 2)
    return o, v_first, s_fin


def _ret_log_gamma(h):
    return math.log1p(-(2.0 ** (-5.0 - h)))


def _rotary_tables(pos, heads, dk):
    half = dk // 2
    inv = ROPE_BASE ** (-jnp.arange(half, dtype=F32) / half)
    ang = pos.astype(F32)[:, None] * inv[None, :]
    cos = jnp.tile(jnp.concatenate([jnp.cos(ang), jnp.cos(ang)], -1), (1, heads))
    sin = jnp.tile(jnp.concatenate([-jnp.sin(ang), jnp.sin(ang)], -1), (1, heads))
    lane = jnp.arange(heads * dk)
    perm = (lane[:, None] == (lane[None, :] ^ half)).astype(BF16)
    return cos, sin, perm


def _ret_fused_body(p_ref, cos_ref, sin_ref, perm_ref, o_ref, sf_ref, s_scr, *, nc):
    L, DK, DV, H = CHUNK, RET_QK, RET_V, RET_HEADS
    nq = H * DK
    c = pl.program_id(1)

    @pl.when(c == 0)
    def _():
        s_scr[...] = jnp.zeros_like(s_scr)

    cos = cos_ref[...]
    sin = sin_ref[...]
    perm = perm_ref[...]
    rot = lambda x: x * cos + _exact_rhs_dot(x, perm) * sin
    q_all = rot(p_ref[0, :, 0:nq])
    k_all = rot(p_ref[0, :, nq:2 * nq]) * (DK ** -0.5)
    row = lax.broadcasted_iota(jnp.int32, (L, L), 0)
    col = lax.broadcasted_iota(jnp.int32, (L, L), 1)
    rel = (row - col).astype(F32)
    idx = lax.broadcasted_iota(jnp.int32, (L, 1), 0).astype(F32)
    for h in range(H):
        lg = _ret_log_gamma(h)
        q = q_all[:, h * DK:(h + 1) * DK]
        k = k_all[:, h * DK:(h + 1) * DK]
        v = p_ref[0, :, 2 * nq + h * DV:2 * nq + (h + 1) * DV].astype(BF16)
        gate = p_ref[0, :, 2 * nq + RET_W + h * DV:2 * nq + RET_W + (h + 1) * DV]
        dmask = jnp.where(rel >= 0, jnp.exp(jnp.maximum(rel, 0.0) * lg), 0.0)
        scores = _dot_nt(q.astype(BF16), k.astype(BF16)) * dmask
        s_prev = s_scr[h]
        q_dec = q * jnp.exp((idx + 1.0) * lg)
        y = _dot(scores.astype(BF16), v) + _dot(q_dec.astype(BF16), s_prev.astype(BF16))
        k_end = k * jnp.exp((L - 1.0 - idx) * lg)
        s_scr[h] = math.exp(L * lg) * s_prev + _dot_tn(k_end.astype(BF16), v)
        y = y * lax.rsqrt(jnp.mean(y * y, -1, keepdims=True) + NORM_EPS)
        o_ref[0, :, h * DV:(h + 1) * DV] = (gate * jax.nn.sigmoid(gate) * y).astype(BF16)

    @pl.when(c == nc - 1)
    def _():
        sf_ref[0] = s_scr[...]


def _ret_prompt(p3, pos):
    b, t, _ = p3.shape
    L = CHUNK
    nc = t // L
    nq = RET_HEADS * RET_QK
    cos, sin, perm = _rotary_tables(pos, RET_HEADS, RET_QK)
    tab = pl.BlockSpec((L, nq), lambda i, j: (j, 0))
    st = pl.BlockSpec((1, RET_HEADS, RET_QK, RET_V), lambda i, j: (i, 0, 0, 0))
    return pl.pallas_call(
        functools.partial(_ret_fused_body, nc=nc),
        grid=(b, nc),
        in_specs=[pl.BlockSpec((1, L, P_MAIN), lambda i, j: (i, j, P_RET // P_MAIN)), tab, tab,
                  pl.BlockSpec((nq, nq), lambda i, j: (0, 0))],
        out_specs=[pl.BlockSpec((1, L, RET_W), lambda i, j: (i, j, 0)), st],
        out_shape=[jax.ShapeDtypeStruct((b, t, RET_W), BF16),
                   jax.ShapeDtypeStruct((b, RET_HEADS, RET_QK, RET_V), F32)],
        scratch_shapes=[pltpu.VMEM((RET_HEADS, RET_QK, RET_V), F32)],
        compiler_params=_cparams(("parallel", "arbitrary")),
        name="ret_fused",
    )(p3, cos, sin, perm)


ML_HPAD = 8


def _ml_fused_body(p_ref, gate_ref, bias_ref, norm_ref, o_ref, cf_ref, nf_ref, mf_ref, c_scr, n_scr, m_scr, *, nc):
    L, DK, DV, H = CHUNK, ML_QK, ML_V, ML_HEADS
    nq = H * DK
    ci = pl.program_id(1)

    @pl.when(ci == 0)
    def _():
        c_scr[...] = jnp.zeros_like(c_scr)
        n_scr[...] = jnp.zeros_like(n_scr)
        m_scr[...] = jnp.zeros_like(m_scr)

    row = lax.broadcasted_iota(jnp.int32, (L, L), 0)
    col = lax.broadcasted_iota(jnp.int32, (L, L), 1)
    causal = row >= col
    tri = causal.astype(BF16)
    capped = ML_GATE_CAP * jnp.tanh((gate_ref[0] + bias_ref[...]) * (1.0 / ML_GATE_CAP))
    lane = lax.broadcasted_iota(jnp.int32, (L, LANES), 1)
    g = jnp.where(lane < H, capped, jnp.where(lane < 2 * H, -_softplus(-capped), 0.0))
    cum = _exact_lhs_dot(tri, g)
    g_t = g.T
    cum_t = cum.T
    for h in range(H):
        q = p_ref[0, :, h * DK:(h + 1) * DK]
        k = p_ref[0, :, nq + h * DK:nq + (h + 1) * DK] * (DK ** -0.5)
        v = p_ref[0, :, 2 * nq + h * DV:2 * nq + (h + 1) * DV].astype(BF16)
        og = p_ref[0, :, 2 * nq + ML_W + h * DV:2 * nq + ML_W + (h + 1) * DV]
        ig_col = g[:, h:h + 1]
        ig_row = g_t[h:h + 1, :]
        b_col = cum[:, H + h:H + h + 1]
        b_row = cum_t[H + h:H + h + 1, :]
        b_tot = cum[L - 1:L, H + h:H + h + 1]
        m_prev = m_scr[h:h + 1, 0:1]
        c_prev = c_scr[h]
        n_prev = n_scr[h:h + 1, :]
        a_row = b_tot - b_row + ig_row
        a_col = b_tot - b_col + ig_col
        m_new = jnp.maximum(b_tot + m_prev, jnp.max(a_row, axis=1, keepdims=True))
        dec = jnp.exp(b_tot + m_prev - m_new)
        kw = k * jnp.exp(a_col - m_new)
        c_scr[h] = dec * c_prev + _dot_tn(kw.astype(BF16), v)
        n_scr[h:h + 1, :] = dec * n_prev + jnp.sum(kw, axis=0, keepdims=True)
        m_scr[h:h + 1, :] = jnp.broadcast_to(m_new, (1, LANES))
        dlog = jnp.where(causal, b_col - b_row + ig_row, -jnp.inf)
        inter = b_col + m_prev
        m_i = jnp.maximum(jnp.max(dlog, axis=1, keepdims=True), inter)
        s = _dot_nt(q.astype(BF16), k.astype(BF16)) * jnp.exp(dlog - m_i)
        sc = jnp.exp(inter - m_i)
        num = _dot(s.astype(BF16), v) + sc * _dot(q.astype(BF16), c_prev.astype(BF16))
        den = jnp.sum(s, axis=1, keepdims=True) + sc * jnp.sum(q * n_prev, axis=1, keepdims=True)
        hid = num / jnp.maximum(jnp.abs(den), jnp.exp(-m_i))
        hid = hid * lax.rsqrt(jnp.mean(hid * hid, -1, keepdims=True) + NORM_EPS)
        o_ref[0, :, h * DV:(h + 1) * DV] = (jax.nn.sigmoid(og) * (hid * norm_ref[:, h * DV:(h + 1) * DV])).astype(BF16)

    @pl.when(ci == nc - 1)
    def _():
        cf_ref[0] = c_scr[...]
        nf_ref[0] = n_scr[...]
        mf_ref[0] = m_scr[...]


def _ml_prompt(p3, lp):
    b, t, _ = p3.shape
    L = CHUNK
    nc = t // L
    bias = jnp.pad(jnp.concatenate([lp['ml_ib'], lp['ml_fb']]), (0, LANES - 2 * ML_HEADS)).reshape(1, LANES)
    vs = pl.BlockSpec((1, L, ML_W), lambda i, j: (i, j, 0))
    cs = pl.BlockSpec((1, ML_HEADS, ML_QK, ML_V), lambda i, j: (i, 0, 0, 0))
    ns = pl.BlockSpec((1, ML_HPAD, ML_QK), lambda i, j: (i, 0, 0))
    ms = pl.BlockSpec((1, ML_HPAD, LANES), lambda i, j: (i, 0, 0))
    o, c_f, n_f, m_f = pl.pallas_call(
        functools.partial(_ml_fused_body, nc=nc),
        grid=(b, nc),
        in_specs=[pl.BlockSpec((1, L, P_MAIN), lambda i, j: (i, j, P_ML // P_MAIN)),
                  pl.BlockSpec((1, L, LANES), lambda i, j: (i, j, P_GATE // LANES)),
                  pl.BlockSpec((1, LANES), lambda i, j: (0, 0)),
                  pl.BlockSpec((1, ML_W), lambda i, j: (0, 0))],
        out_specs=[vs, cs, ns, ms],
        out_shape=[jax.ShapeDtypeStruct((b, t, ML_W), BF16),
                   jax.ShapeDtypeStruct((b, ML_HEADS, ML_QK, ML_V), F32),
                   jax.ShapeDtypeStruct((b, ML_HPAD, ML_QK), F32),
                   jax.ShapeDtypeStruct((b, ML_HPAD, LANES), F32)],
        scratch_shapes=[pltpu.VMEM((ML_HEADS, ML_QK, ML_V), F32),
                        pltpu.VMEM((ML_HPAD, ML_QK), F32),
                        pltpu.VMEM((ML_HPAD, LANES), F32)],
        compiler_params=_cparams(("parallel", "arbitrary")),
        name="ml_fused",
    )(p3, p3, bias, lp['ml_norm'].reshape(1, ML_W))
    return o, c_f, n_f[:, :ML_HEADS], m_f[:, :ML_HEADS, 0]


def _to_cols(x):
    b, c = x.shape
    return x.reshape(b // DEC_TB, DEC_TB, c).transpose(0, 2, 1)


def _rwkv_step_body(w_ref, ap_ref, k_ref, kk_ref, r_ref, v_ref, s_ref, y_ref, so_ref):
    N = RW_HD
    v = v_ref[0]

    def sa_step(i, acc):
        return acc + kk_ref[0, pl.ds(i, 1), :] * s_ref[i]

    sa = lax.fori_loop(0, N, sa_step, jnp.zeros_like(v), unroll=8)

    def upd_step(i, y):
        row = lambda ref: ref[0, pl.ds(i, 1), :]
        s_new = row(w_ref) * s_ref[i] - row(ap_ref) * sa + row(k_ref) * v
        so_ref[i] = s_new
        return y + row(r_ref) * s_new

    y_ref[0] = lax.fori_loop(0, N, upd_step, jnp.zeros_like(v), unroll=8)


def _rwkv_step(wdec, ap, k, kk, r, v, s_all, layer):
    b, w = v.shape
    heads = lambda x: x.reshape(b, RW_HEADS, RW_HD).transpose(1, 2, 0)
    vec = pl.BlockSpec((1, RW_HD, b), lambda h: (h, 0, 0))
    y, s_new = pl.pallas_call(
        _rwkv_step_body,
        grid=(RW_HEADS,),
        in_specs=[vec] * 6 + [pl.BlockSpec((None, None, RW_HD, RW_HD, b), lambda h: (layer, h, 0, 0, 0))],
        out_specs=[vec, pl.BlockSpec((None, RW_HD, RW_HD, b), lambda h: (h, 0, 0, 0))],
        out_shape=[jax.ShapeDtypeStruct((RW_HEADS, RW_HD, b), F32),
                   jax.ShapeDtypeStruct((RW_HEADS, RW_HD, RW_HD, b), F32)],
        compiler_params=_cparams(("parallel",)),
        name="rwkv_step",
    )(heads(wdec), heads(ap), heads(k), heads(kk), heads(r), heads(v), s_all)
    return y.transpose(2, 0, 1).reshape(b, w), s_new


def _ret_step_body(qc_ref, kc_ref, v_ref, s_ref, y_ref, so_ref):
    DK, DV = RET_QK, RET_V
    for j in range(DEC_TB):
        for h in range(RET_HEADS):
            gamma = math.exp(_ret_log_gamma(h))
            q = qc_ref[0, h * DK:(h + 1) * DK, j:j + 1]
            k = kc_ref[0, h * DK:(h + 1) * DK, j:j + 1]
            v = v_ref[j:j + 1, h * DV:(h + 1) * DV]
            s = s_ref[j, h]
            qk = jnp.sum(q * k, axis=0, keepdims=True)
            y_ref[j:j + 1, h * DV:(h + 1) * DV] = qk * v + gamma * jnp.sum(q * s, axis=0, keepdims=True)
            so_ref[j, h] = gamma * s + k * v


def _ret_step(q, k, v, s0):
    b = q.shape[0]
    cols = pl.BlockSpec((1, RET_HEADS * RET_QK, DEC_TB), lambda i: (i, 0, 0))
    rows = pl.BlockSpec((DEC_TB, RET_W), lambda i: (i, 0))
    st = pl.BlockSpec((DEC_TB, RET_HEADS, RET_QK, RET_V), lambda i: (i, 0, 0, 0))
    return pl.pallas_call(
        _ret_step_body,
        grid=(b // DEC_TB,),
        in_specs=[cols, cols, rows, st],
        out_specs=[rows, st],
        out_shape=[jax.ShapeDtypeStruct((b, RET_W), F32), jax.ShapeDtypeStruct(s0.shape, F32)],
        compiler_params=_cparams(("parallel",)),
        name="ret_step",
    )(_to_cols(q), _to_cols(k), v, s0)


def _ml_step_body(qc_ref, kc_ref, q_ref, k_ref, v_ref, ig_ref, lf_ref, c_ref, n_ref, m_ref,
                  h_ref, co_ref, no_ref, mo_ref):
    DK, DV = ML_QK, ML_V
    for j in range(DEC_TB):
        for h in range(ML_HEADS):
            ks = slice(h * DK, (h + 1) * DK)
            vs = slice(h * DV, (h + 1) * DV)
            q_col = qc_ref[0, ks, j:j + 1]
            k_col = kc_ref[0, ks, j:j + 1]
            q_row = q_ref[j:j + 1, ks]
            k_row = k_ref[j:j + 1, ks]
            v = v_ref[j:j + 1, vs]
            ig = ig_ref[j:j + 1, h:h + 1]
            lf = lf_ref[j:j + 1, h:h + 1]
            m_prev = m_ref[j:j + 1, h:h + 1]
            c_prev = c_ref[j, h]
            n_prev = n_ref[j, h:h + 1, :]
            m_new = jnp.maximum(lf + m_prev, ig)
            dec = jnp.exp(lf + m_prev - m_new)
            wgt = jnp.exp(ig - m_new)
            co_ref[j, h] = dec * c_prev + (k_col * wgt) * v
            no_ref[j, h:h + 1, :] = dec * n_prev + k_row * wgt
            mo_ref[j:j + 1, h:h + 1] = m_new
            s = jnp.sum(q_row * k_row, axis=1, keepdims=True) * wgt
            num = s * v + dec * jnp.sum(q_col * c_prev, axis=0, keepdims=True)
            den = s + dec * jnp.sum(q_row * n_prev, axis=1, keepdims=True)
            h_ref[j:j + 1, vs] = num / jnp.maximum(jnp.abs(den), jnp.exp(-m_new))


def _ml_step(q, k, v, ig, lf, c0, n0, m0):
    b = q.shape[0]
    cols = pl.BlockSpec((1, ML_HEADS * ML_QK, DEC_TB), lambda i: (i, 0, 0))
    qk_rows = pl.BlockSpec((DEC_TB, ML_HEADS * ML_QK), lambda i: (i, 0))
    rows = pl.BlockSpec((DEC_TB, ML_W), lambda i: (i, 0))
    sc = pl.BlockSpec((DEC_TB, ML_HEADS), lambda i: (i, 0))
    cs = pl.BlockSpec((DEC_TB, ML_HEADS, ML_QK, ML_V), lambda i: (i, 0, 0, 0))
    ns = pl.BlockSpec((DEC_TB, ML_HEADS, ML_QK), lambda i: (i, 0, 0))
    return pl.pallas_call(
        _ml_step_body,
        grid=(b // DEC_TB,),
        in_specs=[cols, cols, qk_rows, qk_rows, rows, sc, sc, cs, ns, sc],
        out_specs=[rows, cs, ns, sc],
        out_shape=[jax.ShapeDtypeStruct((b, ML_W), F32), jax.ShapeDtypeStruct(c0.shape, F32),
                   jax.ShapeDtypeStruct(n0.shape, F32), jax.ShapeDtypeStruct(m0.shape, F32)],
        compiler_params=_cparams(("parallel",)),
        name="ml_step",
    )(_to_cols(q), _to_cols(k), q, k, v, ig, lf, c0, n0, m0)


def _heads(a, h):
    return a.reshape(a.shape[:-1] + (h, a.shape[-1] // h))


def _shift_prev(p, prev_row):
    return jnp.concatenate([prev_row[:, None, :], p[:, :-1]], axis=1)


def _rotary(x, pos):
    half = x.shape[-1] // 2
    inv = ROPE_BASE ** (-jnp.arange(half, dtype=F32) / half)
    ang = pos.astype(F32)[:, None] * inv[None, :]
    cos = jnp.cos(ang)[None, :, None, :]
    sin = jnp.sin(ang)[None, :, None, :]
    x1, x2 = x[..., :half], x[..., half:]
    return jnp.concatenate([x1 * cos - x2 * sin, x1 * sin + x2 * cos], -1)


def _small_matmul(x, w):
    lead = x.shape[:-1]
    kdim, n = w.shape
    x2 = x.reshape(-1, kdim)
    m = x2.shape[0]
    kp = -(-kdim // LANES) * LANES
    npad = -(-n // LANES) * LANES
    x2 = jnp.pad(x2.astype(BF16), ((0, 0), (0, kp - kdim)))
    w2 = jnp.pad(w.astype(BF16), ((0, kp - kdim), (0, npad - n)))
    tm = 1024 if m % 1024 == 0 else m
    out = _matmul(x2, w2, tm, npad)
    return out[:, :n].reshape(lead + (n,))


def _mix_prompt(p, pos, v_first, lp):
    o_rw, v_first, s_new = _rwkv_prompt(p, lp, v_first, (1, 1, 1, 1))
    o_ret, r_new = _ret_prompt(p, pos)
    o_ml, c_new, n_new, m_new = _ml_prompt(p, lp)
    return (o_rw, o_ret, o_ml), v_first, (s_new, r_new, c_new, n_new, m_new)


def _mix_sample(p, pos, v_first, st, lp, prev_row):
    bsz, t_len, _ = p.shape

    p_rw = p[..., :RW_P]
    mixed = p_rw + (_shift_prev(p_rw, prev_row[:, :RW_P]) - p_rw) * lp['rw_mu']
    sizes = np.cumsum([RW_W, RW_W, RW_W, RW_LORA_W, RW_LORA_A, RW_LORA_G])[:-1]
    r, k, v, xw, xa, xg = jnp.split(mixed, [int(s) for s in sizes], axis=-1)
    w = -jax.nn.softplus(-(lp['rw_w0'] + _small_matmul(jnp.tanh(xw), lp['rw_w2']))) - 0.5
    a = jax.nn.sigmoid(lp['rw_a0'] + _small_matmul(xa, lp['rw_a2']))
    g = _small_matmul(jax.nn.sigmoid(xg), lp['rw_g2'])
    if v_first is None:
        v_first = v
    else:
        pv = p[..., P_V1:P_V1 + RW_LORA_V]
        xv = pv + (_shift_prev(pv, prev_row[:, P_V1:P_V1 + RW_LORA_V]) - pv) * lp['rw_vmu']
        v = v + (v_first - v) * jax.nn.sigmoid(lp['rw_v0'] + _small_matmul(xv, lp['rw_v2']))
    kk = _heads(k * lp['rw_kk'], RW_HEADS)
    kk = kk * lax.rsqrt(jnp.maximum(jnp.sum(jnp.square(kk), -1, keepdims=True), 1e-24))
    kk = kk.reshape(bsz, t_len, RW_W)
    k = k * (1.0 + (a - 1.0) * lp['rw_ka'])
    lw = -jnp.exp(w)
    y, s_new = _rwkv_step(jnp.exp(lw)[:, 0], (kk * a)[:, 0], k[:, 0], kk[:, 0], r[:, 0], v[:, 0],
                          st['rw_wkv_t'], st['layer'])
    y = _heads(y[:, None, :], RW_HEADS)
    y_mu = jnp.mean(y, -1, keepdims=True)
    y_var = jnp.mean(jnp.square(y - y_mu), -1, keepdims=True)
    y = ((y - y_mu) * lax.rsqrt(y_var + RW_GN_EPS)).reshape(bsz, t_len, RW_W)
    y = y * lp['rw_lnx_g'] + lp['rw_lnx_b']
    rh, kh, vh = (_heads(u, RW_HEADS) for u in (r, k, v))
    bonus = jnp.sum(rh * kh * _heads(lp['rw_rk'], RW_HEADS), -1, keepdims=True) * vh
    o_rw = ((y + bonus.reshape(bsz, t_len, RW_W)) * g).astype(BF16)

    nqk = RET_HEADS * RET_QK
    p_ret = p[..., P_RET:P_RET + P_MAIN]
    qr, kr, vr, gr = (p_ret[..., :nqk], p_ret[..., nqk:2 * nqk],
                      p_ret[..., 2 * nqk:2 * nqk + RET_W], p_ret[..., 2 * nqk + RET_W:])
    qh = _rotary(_heads(qr, RET_HEADS), pos).reshape(bsz, t_len, nqk)
    khr = (_rotary(_heads(kr, RET_HEADS), pos) * (RET_QK ** -0.5)).reshape(bsz, t_len, nqk)
    yr, r_new = _ret_step(qh[:, 0], khr[:, 0], vr[:, 0], st['ret'])
    yr = _heads(yr[:, None, :], RET_HEADS)
    yr = yr * lax.rsqrt(jnp.mean(jnp.square(yr), -1, keepdims=True) + NORM_EPS)
    o_ret = (jax.nn.silu(gr) * yr.reshape(bsz, t_len, RET_W)).astype(BF16)

    nqk = ML_HEADS * ML_QK
    p_ml = p[..., P_ML:P_ML + P_MAIN]
    qm, km, vm, om = (p_ml[..., :nqk], p_ml[..., nqk:2 * nqk],
                      p_ml[..., 2 * nqk:2 * nqk + ML_W], p_ml[..., 2 * nqk + ML_W:])
    im = p[..., P_GATE:P_GATE + ML_HEADS]
    fm = p[..., P_GATE + ML_HEADS:P_GATE + 2 * ML_HEADS]
    ig = ML_GATE_CAP * jnp.tanh((im + lp['ml_ib']) / ML_GATE_CAP)
    lf = jax.nn.log_sigmoid(ML_GATE_CAP * jnp.tanh((fm + lp['ml_fb']) / ML_GATE_CAP))
    km = km * (ML_QK ** -0.5)
    hm, c_new, n_new, m_new = _ml_step(qm[:, 0], km[:, 0], vm[:, 0], ig[:, 0], lf[:, 0],
                                       st['ml_c'], st['ml_n'], st['ml_m'])
    hm = _heads(hm[:, None, :], ML_HEADS)
    hm = hm * lax.rsqrt(jnp.mean(jnp.square(hm), -1, keepdims=True) + NORM_EPS)
    o_ml = (jax.nn.sigmoid(om) * (hm.reshape(bsz, t_len, ML_W) * lp['ml_norm'])).astype(BF16)

    return (o_rw, o_ret, o_ml), v_first, (s_new, r_new, c_new, n_new, m_new)


_T_V1 = P_V1 // LANES
_T_GATE = P_GATE // LANES
_T_RET = P_RET // LANES
_T_SHIFT = (P_RET - RW_P) // LANES
_T_SRC_GATE = (RW_P + RET_P + 2 * ML_HEADS * ML_QK + 2 * ML_W) // LANES


def _pack_body(w_ref, v1_ref, *o_refs):
    j = pl.program_id(0)
    row = lax.broadcasted_iota(jnp.int32, (LANES, 1), 0)
    spare = jnp.logical_and(j > _T_GATE, j < _T_RET)
    for l, o_ref in enumerate(o_refs):
        w = w_ref[:, l, :]
        gates = jnp.where(row < 2 * ML_HEADS, w, 0.0)
        out = jnp.where(j == _T_V1, v1_ref[l], jnp.where(j == _T_GATE, gates, jnp.where(spare, 0.0, w)))
        o_ref[...] = out.astype(BF16)


def _pack_w_in(w_in, rw_v1):
    depth, d, _ = w_in.shape
    w_t = w_in.transpose(2, 0, 1)
    v1_t = jnp.pad(rw_v1.transpose(0, 2, 1), ((1, 0), (0, LANES - RW_LORA_V), (0, 0)))

    def src_tile(j):
        return jnp.where(j < _T_V1, j, jnp.where(j == _T_GATE, _T_SRC_GATE, j - _T_SHIFT))

    return pl.pallas_call(
        _pack_body,
        grid=(P_PAD // LANES,),
        in_specs=[pl.BlockSpec((LANES, depth, d), lambda j: (src_tile(j), 0, 0)),
                  pl.BlockSpec((depth, LANES, d), lambda j: (0, 0, 0))],
        out_specs=[pl.BlockSpec((LANES, d), lambda j: (j, 0))] * depth,
        out_shape=[jax.ShapeDtypeStruct((P_PAD, d), BF16)] * depth,
        compiler_params=_cparams(("parallel",)),
        name="pack_w_in",
    )(w_t, v1_t)


def _token_tiles(m):
    if m % 2048 == 0:
        return 2048, 512, 512
    return m, m, m


def kernel(x_prompt, x_sample, state_rw_shift, state_rw_wkv, state_ret, state_ml_c, state_ml_n, state_ml_m,
           ln0_g, ln0_b, w_in, rw_mu, rw_w0, rw_w2, rw_a0, rw_a2, rw_g2, rw_kk, rw_ka, rw_rk,
           rw_lnx_g, rw_lnx_b, rw_v0, rw_v1, rw_vmu, rw_v2, ml_ib, ml_fb, ml_norm, w_out,
           ln1_g, ln1_b, w_gate, w_up, w_down, ln2_g, ln2_b):
    bp, tp, d = x_prompt.shape
    bs, ts, _ = x_sample.shape
    groups = {
        'p': dict(b=bp, t=tp, pos=jnp.arange(tp), chunked=True),
        's': dict(b=bs, t=ts, pos=PAST_LEN + jnp.arange(ts), chunked=False),
    }
    rw_wkv_t = state_rw_wkv.transpose(0, 2, 3, 4, 1)
    w_in_packed = _pack_w_in(w_in, rw_v1)
    xs_f, xs_b, v_first, outs = {}, {}, {}, {}
    for name, x in (('p', x_prompt), ('s', x_sample)):
        m = x.shape[0] * x.shape[1]
        xs_f[name], xs_b[name] = _layernorm(x.reshape(m, d), ln0_g, ln0_b, _token_tiles(m)[1])
        v_first[name] = None
        outs[name] = []

    for l in range(DEPTH):
        lp = {
            'rw_mu': rw_mu[l], 'rw_w0': rw_w0[l], 'rw_w2': rw_w2[l], 'rw_a0': rw_a0[l], 'rw_a2': rw_a2[l],
            'rw_g2': rw_g2[l], 'rw_kk': rw_kk[l], 'rw_ka': rw_ka[l], 'rw_rk': rw_rk[l],
            'rw_lnx_g': rw_lnx_g[l], 'rw_lnx_b': rw_lnx_b[l], 'ml_ib': ml_ib[l], 'ml_fb': ml_fb[l],
            'ml_norm': ml_norm[l],
        }
        if l > 0:
            lp.update(rw_v0=rw_v0[l - 1], rw_vmu=rw_vmu[l - 1], rw_v2=rw_v2[l - 1])
        w_in_b = w_in_packed[l]
        w_out_b = w_out[l].astype(BF16)
        w_gate_b = w_gate[l].astype(BF16)
        w_up_b = w_up[l].astype(BF16)
        w_down_b = w_down[l].astype(BF16)
        prev_s = _matmul(state_rw_shift[l].astype(BF16), w_in_b, bs, P_TN, w_transposed=True)
        for name, grp in groups.items():
            b, t = grp['b'], grp['t']
            m = b * t
            tm_big, tm_out, tm_down = _token_tiles(m)
            p = _matmul(xs_b[name], w_in_b, tm_big, P_TN, w_transposed=True).reshape(b, t, P_PAD)
            if grp['chunked']:
                o, v_first[name], new_st = _mix_prompt(p, grp['pos'], v_first[name], lp)
            else:
                st = {'rw_wkv_t': rw_wkv_t, 'layer': l, 'ret': state_ret[l], 'ml_c': state_ml_c[l],
                      'ml_n': state_ml_n[l], 'ml_m': state_ml_m[l]}
                o, v_first[name], new_st = _mix_sample(p, grp['pos'], v_first[name], st, lp, prev_s)
            outs[name].append((xs_f[name].reshape(b, t, d)[:, -1],) + new_st)
            o_rw, o_ret, o_ml = (u.reshape(m, u.shape[-1]) for u in o)
            x1_f, x1_b = _out_proj_ln(o_rw, o_ret, o_ml, w_out_b, xs_f[name], ln1_g[l], ln1_b[l], tm_out)
            hdn = _matmul_swiglu(x1_b, w_gate_b, w_up_b, tm_big, 512)
            xs_f[name], xs_b[name] = _matmul_res_ln(hdn, w_down_b, x1_f, ln2_g[l], ln2_b[l], tm_down, 512)

    y_p = xs_f['p'].reshape(bp, tp, d)
    y_s = xs_f['s'].reshape(bs, ts, d)
    sp = [jnp.stack([o[i] for o in outs['p']]) for i in range(6)]
    ss = [jnp.stack([o[i] for o in outs['s']]) for i in range(6)]
    ss[1] = ss[1].transpose(0, 4, 1, 2, 3)
    return (y_p, y_s, sp[0], sp[1], sp[2], sp[3], sp[4], sp[5], ss[0], ss[1], ss[2], ss[3], ss[4], ss[5])
```

```python
import functools
import math

import numpy as np
import jax
import jax.numpy as jnp
from jax import lax
from jax.experimental import pallas as pl
from jax.experimental.pallas import tpu as pltpu

F32 = jnp.float32
BF16 = jnp.bfloat16

D_MODEL = 2048
DEPTH = 2
PAST_LEN = 16384
RW_HD = 64
RW_W = D_MODEL // 4
RW_HEADS = RW_W // RW_HD
RW_LORA_W = 64
RW_LORA_A = 64
RW_LORA_V = 32
RW_LORA_G = 128
RW_P = 3 * RW_W + RW_LORA_W + RW_LORA_A + RW_LORA_G
RW_GN_EPS = 64e-5
RET_V = 128
RET_QK = 64
RET_W = 3 * D_MODEL // 8
RET_HEADS = RET_W // RET_V
RET_P = 2 * RET_HEADS * RET_QK + 2 * RET_W
ML_V = 128
ML_QK = 64
ML_W = D_MODEL - RW_W - RET_W
ML_HEADS = ML_W // ML_V
ML_P = 2 * ML_HEADS * ML_QK + 2 * ML_W + 2 * ML_HEADS
ML_GATE_CAP = 15.0
P_TOTAL = RW_P + RET_P + ML_P
D_FF = ((8 * D_MODEL + 3 * 256 - 1) // (3 * 256)) * 256
CHUNK = 128
ROPE_BASE = 10000.0
LN_EPS = 1e-5
NORM_EPS = 1e-6
ALPHA = (2 * DEPTH) ** 0.25

LANES = 128
P_V1 = RW_P
P_GATE = RW_P + LANES
P_RET = 2304
P_ML = 2 * P_RET
P_MAIN = 2304
P_PAD = 3 * P_RET
P_TN = 768
RW_CHUNK = 64
DEC_TB = 8
VMEM_LIMIT = 56 * 1024 * 1024

HIGHEST = lax.Precision.HIGHEST


def _cparams(sem):
    return pltpu.CompilerParams(dimension_semantics=sem, vmem_limit_bytes=VMEM_LIMIT)


def _dot(a, b, precision=None):
    return lax.dot_general(a, b, (((1,), (0,)), ((), ())), precision=precision, preferred_element_type=F32)


def _dot_nt(a, b, precision=None):
    return lax.dot_general(a, b, (((1,), (1,)), ((), ())), precision=precision, preferred_element_type=F32)


def _dot_tn(a, b, precision=None):
    return lax.dot_general(a, b, (((0,), (0,)), ((), ())), precision=precision, preferred_element_type=F32)


def _ln_rows(x, g, b):
    mu = jnp.mean(x, -1, keepdims=True)
    xc = x - mu
    var = jnp.mean(xc * xc, -1, keepdims=True)
    return xc * lax.rsqrt(var + LN_EPS) * g + b


def _ln_body(x_ref, g_ref, b_ref, of_ref, ob_ref):
    y = _ln_rows(x_ref[...], g_ref[...], b_ref[...])
    of_ref[...] = y
    ob_ref[...] = y.astype(BF16)


def _layernorm(x, g, b, tm):
    m, d = x.shape
    return pl.pallas_call(
        _ln_body,
        grid=(m // tm,),
        in_specs=[pl.BlockSpec((tm, d), lambda i: (i, 0)),
                  pl.BlockSpec((1, d), lambda i: (0, 0)),
                  pl.BlockSpec((1, d), lambda i: (0, 0))],
        out_specs=[pl.BlockSpec((tm, d), lambda i: (i, 0)),
                   pl.BlockSpec((tm, d), lambda i: (i, 0))],
        out_shape=[jax.ShapeDtypeStruct((m, d), F32), jax.ShapeDtypeStruct((m, d), BF16)],
        compiler_params=_cparams(("parallel",)),
        name="layernorm",
    )(x, g.reshape(1, d), b.reshape(1, d))


def _mm_body(x_ref, w_ref, o_ref, *, w_transposed):
    dot = _dot_nt if w_transposed else _dot
    o_ref[...] = dot(x_ref[...], w_ref[...]).astype(o_ref.dtype)


def _matmul(x, w, tm, tn, out_dtype=F32, w_transposed=False):
    m, k = x.shape
    n = w.shape[0] if w_transposed else w.shape[1]
    w_spec = (pl.BlockSpec((tn, k), lambda i, j: (j, 0)) if w_transposed
              else pl.BlockSpec((k, tn), lambda i, j: (0, j)))
    return pl.pallas_call(
        functools.partial(_mm_body, w_transposed=w_transposed),
        grid=(m // tm, n // tn),
        in_specs=[pl.BlockSpec((tm, k), lambda i, j: (i, 0)), w_spec],
        out_specs=pl.BlockSpec((tm, tn), lambda i, j: (i, j)),
        out_shape=jax.ShapeDtypeStruct((m, n), out_dtype),
        compiler_params=_cparams(("parallel", "parallel")),
        name="matmul",
    )(x, w)


def _swiglu_body(x_ref, wg_ref, wu_ref, o_ref, wg_scr, wu_scr):
    @pl.when(pl.program_id(1) == 0)
    def _():
        wg_scr[...] = wg_ref[...].astype(BF16)
        wu_scr[...] = wu_ref[...].astype(BF16)

    x = x_ref[...]
    g = _dot(x, wg_scr[...])
    u = _dot(x, wu_scr[...])
    o_ref[...] = (g * jax.nn.sigmoid(g) * u).astype(o_ref.dtype)


def _matmul_swiglu(x, wg, wu, layer, tm, tn):
    m, k = x.shape
    n = wg.shape[2]
    w_spec = pl.BlockSpec((None, k, tn), lambda j, i: (layer, 0, j))
    return pl.pallas_call(
        _swiglu_body,
        grid=(n // tn, m // tm),
        in_specs=[pl.BlockSpec((tm, k), lambda j, i: (i, 0)), w_spec, w_spec],
        out_specs=pl.BlockSpec((tm, tn), lambda j, i: (i, j)),
        out_shape=jax.ShapeDtypeStruct((m, n), BF16),
        scratch_shapes=[pltpu.VMEM((k, tn), BF16), pltpu.VMEM((k, tn), BF16)],
        compiler_params=_cparams(("parallel", "arbitrary")),
        name="matmul_swiglu",
    )(x, wg, wu)


LN_ROWS = 256
OUT_ROWS = 512


def _mm_res_ln_body(x_ref, w_ref, res_ref, g_ref, b_ref, of_ref, ob_ref, *, nk, tm):
    kk = pl.program_id(1)
    part = _dot(x_ref[...], w_ref[...])

    @pl.when(kk == 0)
    def _():
        of_ref[...] = part

    @pl.when(kk > 0)
    def _():
        of_ref[...] += part

    @pl.when(kk == nk - 1)
    def _():
        for r in range(0, tm, min(LN_ROWS, tm)):
            rows = pl.ds(r, min(LN_ROWS, tm))
            y = _ln_rows(ALPHA * res_ref[rows, :] + of_ref[rows, :], g_ref[...], b_ref[...])
            of_ref[rows, :] = y
            ob_ref[rows, :] = y.astype(BF16)


def _matmul_res_ln(x, w, res, g, b, tm, tk):
    m, k = x.shape
    n = w.shape[1]
    nk = k // tk
    return pl.pallas_call(
        functools.partial(_mm_res_ln_body, nk=nk, tm=tm),
        grid=(m // tm, nk),
        in_specs=[pl.BlockSpec((tm, tk), lambda i, j: (i, j)),
                  pl.BlockSpec((tk, n), lambda i, j: (j, 0)),
                  pl.BlockSpec((tm, n), lambda i, j: (i, 0), pipeline_mode=pl.Buffered(1)),
                  pl.BlockSpec((1, n), lambda i, j: (0, 0)),
                  pl.BlockSpec((1, n), lambda i, j: (0, 0))],
        out_specs=[pl.BlockSpec((tm, n), lambda i, j: (i, 0)),
                   pl.BlockSpec((tm, n), lambda i, j: (i, 0))],
        out_shape=[jax.ShapeDtypeStruct((m, n), F32), jax.ShapeDtypeStruct((m, n), BF16)],
        compiler_params=_cparams(("parallel", "arbitrary")),
        name="matmul_res_ln",
    )(x, w, res, g.reshape(1, n), b.reshape(1, n))


def _out_proj_ln_body(o_rw_ref, o_ret_ref, o_ml_ref, w_ref, res_ref, g_ref, b_ref, of_ref, ob_ref, *, tm):
    for r in range(0, tm, min(OUT_ROWS, tm)):
        rows = pl.ds(r, min(OUT_ROWS, tm))
        mix = (_dot(o_rw_ref[rows, :], w_ref[0:RW_W, :])
               + _dot(o_ret_ref[rows, :], w_ref[RW_W:RW_W + RET_W, :])
               + _dot(o_ml_ref[rows, :], w_ref[RW_W + RET_W:, :]))
        y = _ln_rows(ALPHA * res_ref[rows, :] + mix, g_ref[...], b_ref[...])
        of_ref[rows, :] = y
        ob_ref[rows, :] = y.astype(BF16)


def _out_proj_ln(o_rw, o_ret, o_ml, w, res, g, b, tm):
    m = o_rw.shape[0]
    n = w.shape[1]
    rows = lambda width: pl.BlockSpec((tm, width), lambda i: (i, 0))
    full = lambda r, c: pl.BlockSpec((r, c), lambda i: (0, 0))
    single = pl.Buffered(1)
    return pl.pallas_call(
        functools.partial(_out_proj_ln_body, tm=tm),
        grid=(m // tm,),
        in_specs=[rows(RW_W), rows(RET_W), rows(ML_W),
                  pl.BlockSpec((D_MODEL, n), lambda i: (0, 0), pipeline_mode=single),
                  pl.BlockSpec((tm, n), lambda i: (i, 0), pipeline_mode=single), full(1, n), full(1, n)],
        out_specs=[rows(n), rows(n)],
        out_shape=[jax.ShapeDtypeStruct((m, n), F32), jax.ShapeDtypeStruct((m, n), BF16)],
        compiler_params=_cparams(("parallel",)),
        name="out_proj_ln",
    )(o_rw, o_ret, o_ml, w, res, g.reshape(1, n), b.reshape(1, n))


RW_TB = 256
RW_GH = 4
RW_GW = RW_GH * RW_HD
RW_VEC_ROWS = 8


def _split3(x):
    hi = x.astype(BF16)
    r1 = x - hi.astype(F32)
    mid = r1.astype(BF16)
    lo = (r1 - mid.astype(F32)).astype(BF16)
    return hi, mid, lo


def _mm(a, b, dims, passes):
    dg = lambda x, y: lax.dot_general(x, y, (dims, ((), ())), preferred_element_type=F32)
    if passes == 6:
        return lax.dot_general(a, b, (dims, ((), ())), precision=HIGHEST, preferred_element_type=F32)
    ah = a.astype(BF16)
    bh = b.astype(BF16)
    if passes == 1:
        return dg(ah, bh)
    al = (a - ah.astype(F32)).astype(BF16)
    bl = (b - bh.astype(F32)).astype(BF16)
    return dg(ah, bh) + (dg(ah, bl) + dg(al, bh))


_NN = ((1,), (0,))
_NT = ((1,), (1,))
_TN = ((0,), (0,))


def _exact_lhs_dot(a_bf16, b):
    hi, mid, lo = _split3(b)
    dg = lambda y: lax.dot_general(a_bf16, y, (_NN, ((), ())), preferred_element_type=F32)
    return dg(hi) + (dg(mid) + dg(lo))


def _exact_rhs_dot(a, b_bf16):
    hi, mid, lo = _split3(a)
    dg = lambda x: lax.dot_general(x, b_bf16, (_NN, ((), ())), preferred_element_type=F32)
    return dg(hi) + (dg(mid) + dg(lo))


def _rw_scan_block(r, lw, k, v, kk, a, st, masks, passes):
    L = RW_CHUNK
    tri, strict, lower, eye, mask_bd = masks
    p_sc, p_inv, p_app, p_st = passes
    bd = lambda x: jnp.where(mask_bd, jnp.concatenate([x] * RW_GH, axis=0), 0.0)
    cum = _exact_lhs_dot(tri, lw)
    tot = cum[L - 1:L, :]
    e_neg = jnp.exp(-cum)
    ap = kk * a
    ap_h = ap * e_neg
    k_h = k * e_neg
    lhs = jnp.concatenate([kk * jnp.exp(cum - lw), r * jnp.exp(cum)], axis=0)
    sc_a = _mm(lhs, bd(ap_h), _NT, p_sc)
    sc_k = _mm(lhs, bd(k_h), _NT, p_sc)
    n_m = jnp.where(strict, sc_a[:L], 0.0)
    m_a = jnp.where(lower, sc_a[L:], 0.0)
    m_k = jnp.where(strict, sc_k[:L], 0.0)
    m_r = jnp.where(lower, sc_k[L:], 0.0)
    inv = eye - n_m
    pw = n_m
    for _ in range(int(math.log2(L)) - 1):
        pw = _mm(pw, bd(pw), _NN, p_inv)
        inv = inv + _mm(inv, bd(pw), _NN, p_inv)
    s_terms = _mm(lhs, st, _NT, p_app)
    mv = _mm(jnp.concatenate([m_k, m_r], axis=0), bd(v), _NN, p_app)
    u = _mm(inv, bd(s_terms[:L] + mv[:L]), _NN, p_app)
    y = s_terms[L:] + mv[L:] - _mm(m_a, bd(u), _NN, p_app)
    e_end = jnp.exp(tot - cum)
    upd = _mm(jnp.concatenate([v, -u], axis=0), jnp.concatenate([k * e_end, ap * e_end], axis=0), _TN, p_st)
    st_new = jnp.where(mask_bd, st * jnp.exp(tot) + upd, 0.0)
    return y, st_new


def _rw_masks():
    L, G = RW_CHUNK, RW_GW
    row = lax.broadcasted_iota(jnp.int32, (L, G), 0)
    col = lax.broadcasted_iota(jnp.int32, (L, G), 1) & (L - 1)
    rl = lax.broadcasted_iota(jnp.int32, (L, L), 0)
    cl = lax.broadcasted_iota(jnp.int32, (L, L), 1)
    rg = lax.broadcasted_iota(jnp.int32, (G, G), 0) // RW_HD
    cg = lax.broadcasted_iota(jnp.int32, (G, G), 1) // RW_HD
    tri = (rl >= cl).astype(BF16)
    return tri, row > col, row >= col, (row == col).astype(F32), rg == cg


def _softplus(z):
    return jnp.maximum(z, 0.0) + jnp.log(1.0 + jnp.exp(-jnp.abs(z)))


def _rwkv_fused_body(*refs, nc, has_vres, passes):
    if has_vres:
        (p_ref, pv_ref, vf_ref, mu_ref, vec_ref, wa_ref, g2_ref, seg_ref, vmu_ref, v2_ref,
         o_ref, sf_ref, st_scr, prev_scr, y_scr, prevv_scr) = refs
    else:
        (p_ref, mu_ref, vec_ref, wa_ref, g2_ref, seg_ref,
         o_ref, vfo_ref, sf_ref, st_scr, prev_scr, y_scr) = refs
    TB, W, L = RW_TB, RW_W, RW_CHUNK
    c = pl.program_id(1)

    @pl.when(c == 0)
    def _():
        st_scr[...] = jnp.zeros_like(st_scr)
        prev_scr[...] = jnp.zeros_like(prev_scr)
        if has_vres:
            prevv_scr[...] = jnp.zeros_like(prevv_scr)

    first_row = lax.broadcasted_iota(jnp.int32, (TB, 1), 0) == 0

    def shift_mix(x, carry_ref, mu):
        prev = jnp.where(first_row, carry_ref[...], pltpu.roll(x, 1, 0))
        carry_ref[...] = x[TB - 1:TB, :]
        return x + (prev - x) * mu

    mixed = shift_mix(p_ref[0], prev_scr, mu_ref[...])
    r = mixed[:, 0:W]
    k = mixed[:, W:2 * W]
    v = mixed[:, 2 * W:3 * W]
    xwa = mixed[:, 3 * W:3 * W + LANES]
    xg = mixed[:, 3 * W + LANES:3 * W + 2 * LANES]
    vec = vec_ref[...]
    w0, a0, kk_s, ka, rk, lnx_g, lnx_b, v0 = (vec[i:i + 1, :] for i in range(RW_VEC_ROWS))
    seg = seg_ref[...]
    wa = wa_ref[...]
    w_lora = _dot(jnp.tanh(xwa).astype(BF16), wa[:, 0:W])
    a_lora = _dot(xwa.astype(BF16), wa[:, W:2 * W])
    lw = -jnp.exp(-_softplus(-(w0 + w_lora)) - 0.5)
    a = jax.nn.sigmoid(a0 + a_lora)
    g = _dot(jax.nn.sigmoid(xg).astype(BF16), g2_ref[...])
    if has_vres:
        xv = shift_mix(pv_ref[0], prevv_scr, vmu_ref[...])
        v = v + (vf_ref[0] - v) * jax.nn.sigmoid(v0 + _dot(xv.astype(BF16), v2_ref[...]))
    else:
        vfo_ref[0] = v
    kk = k * kk_s
    kk = kk * lax.rsqrt(jnp.maximum(_exact_rhs_dot(kk * kk, seg), 1e-24))
    k = k * (1.0 + (a - 1.0) * ka)

    masks = _rw_masks()
    for s in range(TB // L):
        rs = slice(s * L, (s + 1) * L)
        for gi in range(RW_HEADS // RW_GH):
            cs = slice(gi * RW_GW, (gi + 1) * RW_GW)
            y, st_new = _rw_scan_block(r[rs, cs], lw[rs, cs], k[rs, cs], v[rs, cs], kk[rs, cs], a[rs, cs],
                                       st_scr[gi], masks, passes)
            y_scr[rs, cs] = y
            st_scr[gi] = st_new

    y = y_scr[...]
    inv_n = 1.0 / RW_HD
    y_mu = _exact_rhs_dot(y, seg) * inv_n
    yc = y - y_mu
    y_var = _exact_rhs_dot(yc * yc, seg) * inv_n
    y = yc * lax.rsqrt(y_var + RW_GN_EPS) * lnx_g + lnx_b
    bonus = _exact_rhs_dot(r * k * rk, seg) * v
    o_ref[0] = ((y + bonus) * g).astype(BF16)

    @pl.when(c == nc - 1)
    def _():
        sf_ref[0] = st_scr[...]


def _rwkv_prompt(p3, lp, v_first, passes=(3, 3, 3, 3)):
    b, t, _ = p3.shape
    nc = t // RW_TB
    has_vres = v_first is not None
    ng = RW_HEADS // RW_GH
    zpad = jnp.zeros((RW_LORA_W, RW_W), F32)
    wa = jnp.concatenate([jnp.concatenate([lp['rw_w2'], zpad], 0), jnp.concatenate([zpad, lp['rw_a2']], 0)], 1)
    vec = jnp.stack([lp['rw_w0'], lp['rw_a0'], lp['rw_kk'], lp['rw_ka'], lp['rw_rk'], lp['rw_lnx_g'], lp['rw_lnx_b'],
                     lp['rw_v0'] if has_vres else jnp.zeros((RW_W,), F32)])
    hid = jnp.arange(RW_W) // RW_HD
    seg = (hid[:, None] == hid[None, :]).astype(BF16)
    full = lambda shape: pl.BlockSpec(shape, lambda i, j: (0,) * len(shape))
    seq = lambda w, blk: pl.BlockSpec((1, RW_TB, w), lambda i, j: (i, j, blk))
    in_specs = [seq(RW_P, 0)]
    args = [p3]
    if has_vres:
        in_specs += [seq(LANES, P_V1 // LANES), seq(RW_W, 0)]
        args += [p3, v_first]
    in_specs += [full((1, RW_P)), full((RW_VEC_ROWS, RW_W)), full((LANES, 2 * RW_W)), full((RW_LORA_G, RW_W)),
                 full((RW_W, RW_W))]
    args += [lp['rw_mu'].reshape(1, RW_P), vec, wa.astype(BF16), lp['rw_g2'].astype(BF16), seg]
    if has_vres:
        in_specs += [full((1, LANES)), full((LANES, RW_W))]
        args += [jnp.pad(lp['rw_vmu'], (0, LANES - RW_LORA_V)).reshape(1, LANES),
                 jnp.pad(lp['rw_v2'], ((0, LANES - RW_LORA_V), (0, 0))).astype(BF16)]
    out_specs = [seq(RW_W, 0)]
    out_shape = [jax.ShapeDtypeStruct((b, t, RW_W), BF16)]
    if not has_vres:
        out_specs.append(seq(RW_W, 0))
        out_shape.append(jax.ShapeDtypeStruct((b, t, RW_W), F32))
    out_specs.append(pl.BlockSpec((1, ng, RW_GW, RW_GW), lambda i, j: (i, 0, 0, 0)))
    out_shape.append(jax.ShapeDtypeStruct((b, ng, RW_GW, RW_GW), F32))
    scratch = [pltpu.VMEM((ng, RW_GW, RW_GW), F32), pltpu.VMEM((1, RW_P), F32), pltpu.VMEM((RW_TB, RW_W), F32)]
    if has_vres:
        scratch.append(pltpu.VMEM((1, LANES), F32))
    outs = pl.pallas_call(
        functools.partial(_rwkv_fused_body, nc=nc, has_vres=has_vres, passes=passes),
        grid=(b, nc),
        in_specs=in_specs,
        out_specs=out_specs,
        out_shape=out_shape,
        scratch_shapes=scratch,
        compiler_params=_cparams(("parallel", "arbitrary")),
        name="rwkv_fused",
    )(*args)
    if has_vres:
        o, st_bd = outs
    else:
        o, v_first, st_bd = outs
    st5 = st_bd.reshape(b, ng, RW_GH, RW_HD, RW_GH, RW_HD)
    s_fin = jnp.stack([st5[:, :, h, :, h, :] for h in range(RW_GH)], axis=2)
    s_fin = s_fin.reshape(b, RW_HEADS, RW_HD, RW_HD).transpose(0, 1, 3, 2)
    return o, v_first, s_fin


def _ret_log_gamma(h):
    return math.log1p(-(2.0 ** (-5.0 - h)))


def _rotary_tables(pos, heads, dk):
    half = dk // 2
    inv = ROPE_BASE ** (-jnp.arange(half, dtype=F32) / half)
    ang = pos.astype(F32)[:, None] * inv[None, :]
    cos = jnp.tile(jnp.concatenate([jnp.cos(ang), jnp.cos(ang)], -1), (1, heads))
    sin = jnp.tile(jnp.concatenate([-jnp.sin(ang), jnp.sin(ang)], -1), (1, heads))
    lane = jnp.arange(heads * dk)
    perm = (lane[:, None] == (lane[None, :] ^ half)).astype(BF16)
    return cos, sin, perm


def _ret_fused_body(p_ref, cos_ref, sin_ref, perm_ref, o_ref, sf_ref, s_scr, *, nc):
    L, DK, DV, H = CHUNK, RET_QK, RET_V, RET_HEADS
    nq = H * DK
    c = pl.program_id(1)

    @pl.when(c == 0)
    def _():
        s_scr[...] = jnp.zeros_like(s_scr)

    cos = cos_ref[...]
    sin = sin_ref[...]
    perm = perm_ref[...]
    rot = lambda x: x * cos + _exact_rhs_dot(x, perm) * sin
    q_all = rot(p_ref[0, :, 0:nq])
    k_all = rot(p_ref[0, :, nq:2 * nq]) * (DK ** -0.5)
    row = lax.broadcasted_iota(jnp.int32, (L, L), 0)
    col = lax.broadcasted_iota(jnp.int32, (L, L), 1)
    rel = (row - col).astype(F32)
    idx = lax.broadcasted_iota(jnp.int32, (L, 1), 0).astype(F32)
    for h in range(H):
        lg = _ret_log_gamma(h)
        q = q_all[:, h * DK:(h + 1) * DK]
        k = k_all[:, h * DK:(h + 1) * DK]
        v = p_ref[0, :, 2 * nq + h * DV:2 * nq + (h + 1) * DV].astype(BF16)
        gate = p_ref[0, :, 2 * nq + RET_W + h * DV:2 * nq + RET_W + (h + 1) * DV]
        dmask = jnp.where(rel >= 0, jnp.exp(jnp.maximum(rel, 0.0) * lg), 0.0)
        scores = _dot_nt(q.astype(BF16), k.astype(BF16)) * dmask
        s_prev = s_scr[h]
        q_dec = q * jnp.exp((idx + 1.0) * lg)
        y = _dot(scores.astype(BF16), v) + _dot(q_dec.astype(BF16), s_prev.astype(BF16))
        k_end = k * jnp.exp((L - 1.0 - idx) * lg)
        s_scr[h] = math.exp(L * lg) * s_prev + _dot_tn(k_end.astype(BF16), v)
        y = y * lax.rsqrt(jnp.mean(y * y, -1, keepdims=True) + NORM_EPS)
        o_ref[0, :, h * DV:(h + 1) * DV] = (gate * jax.nn.sigmoid(gate) * y).astype(BF16)

    @pl.when(c == nc - 1)
    def _():
        sf_ref[0] = s_scr[...]


def _ret_prompt(p3, pos):
    b, t, _ = p3.shape
    L = CHUNK
    nc = t // L
    nq = RET_HEADS * RET_QK
    cos, sin, perm = _rotary_tables(pos, RET_HEADS, RET_QK)
    tab = pl.BlockSpec((L, nq), lambda i, j: (j, 0))
    st = pl.BlockSpec((1, RET_HEADS, RET_QK, RET_V), lambda i, j: (i, 0, 0, 0))
    return pl.pallas_call(
        functools.partial(_ret_fused_body, nc=nc),
        grid=(b, nc),
        in_specs=[pl.BlockSpec((1, L, P_MAIN), lambda i, j: (i, j, P_RET // P_MAIN)), tab, tab,
                  pl.BlockSpec((nq, nq), lambda i, j: (0, 0))],
        out_specs=[pl.BlockSpec((1, L, RET_W), lambda i, j: (i, j, 0)), st],
        out_shape=[jax.ShapeDtypeStruct((b, t, RET_W), BF16),
                   jax.ShapeDtypeStruct((b, RET_HEADS, RET_QK, RET_V), F32)],
        scratch_shapes=[pltpu.VMEM((RET_HEADS, RET_QK, RET_V), F32)],
        compiler_params=_cparams(("parallel", "arbitrary")),
        name="ret_fused",
    )(p3, cos, sin, perm)


ML_HPAD = 8


def _ml_fused_body(p_ref, gate_ref, bias_ref, norm_ref, o_ref, cf_ref, nf_ref, mf_ref, c_scr, n_scr, m_scr, *, nc):
    L, DK, DV, H = CHUNK, ML_QK, ML_V, ML_HEADS
    nq = H * DK
    ci = pl.program_id(1)

    @pl.when(ci == 0)
    def _():
        c_scr[...] = jnp.zeros_like(c_scr)
        n_scr[...] = jnp.zeros_like(n_scr)
        m_scr[...] = jnp.zeros_like(m_scr)

    row = lax.broadcasted_iota(jnp.int32, (L, L), 0)
    col = lax.broadcasted_iota(jnp.int32, (L, L), 1)
    causal = row >= col
    tri = causal.astype(BF16)
    capped = ML_GATE_CAP * jnp.tanh((gate_ref[0] + bias_ref[...]) * (1.0 / ML_GATE_CAP))
    lane = lax.broadcasted_iota(jnp.int32, (L, LANES), 1)
    g = jnp.where(lane < H, capped, jnp.where(lane < 2 * H, -_softplus(-capped), 0.0))
    cum = _exact_lhs_dot(tri, g)
    g_t = g.T
    cum_t = cum.T
    for h in range(H):
        q = p_ref[0, :, h * DK:(h + 1) * DK]
        k = p_ref[0, :, nq + h * DK:nq + (h + 1) * DK] * (DK ** -0.5)
        v = p_ref[0, :, 2 * nq + h * DV:2 * nq + (h + 1) * DV].astype(BF16)
        og = p_ref[0, :, 2 * nq + ML_W + h * DV:2 * nq + ML_W + (h + 1) * DV]
        ig_col = g[:, h:h + 1]
        ig_row = g_t[h:h + 1, :]
        b_col = cum[:, H + h:H + h + 1]
        b_row = cum_t[H + h:H + h + 1, :]
        b_tot = cum[L - 1:L, H + h:H + h + 1]
        m_prev = m_scr[h:h + 1, 0:1]
        c_prev = c_scr[h]
        n_prev = n_scr[h:h + 1, :]
        a_row = b_tot - b_row + ig_row
        a_col = b_tot - b_col + ig_col
        m_new = jnp.maximum(b_tot + m_prev, jnp.max(a_row, axis=1, keepdims=True))
        dec = jnp.exp(b_tot + m_prev - m_new)
        kw = k * jnp.exp(a_col - m_new)
        c_scr[h] = dec * c_prev + _dot_tn(kw.astype(BF16), v)
        n_scr[h:h + 1, :] = dec * n_prev + jnp.sum(kw, axis=0, keepdims=True)
        m_scr[h:h + 1, :] = jnp.broadcast_to(m_new, (1, LANES))
        dlog = jnp.where(causal, b_col - b_row + ig_row, -jnp.inf)
        inter = b_col + m_prev
        m_i = jnp.maximum(jnp.max(dlog, axis=1, keepdims=True), inter)
        s = _dot_nt(q.astype(BF16), k.astype(BF16)) * jnp.exp(dlog - m_i)
        sc = jnp.exp(inter - m_i)
        num = _dot(s.astype(BF16), v) + sc * _dot(q.astype(BF16), c_prev.astype(BF16))
        den = jnp.sum(s, axis=1, keepdims=True) + sc * jnp.sum(q * n_prev, axis=1, keepdims=True)
        hid = num / jnp.maximum(jnp.abs(den), jnp.exp(-m_i))
        hid = hid * lax.rsqrt(jnp.mean(hid * hid, -1, keepdims=True) + NORM_EPS)
        o_ref[0, :, h * DV:(h + 1) * DV] = (jax.nn.sigmoid(og) * (hid * norm_ref[:, h * DV:(h + 1) * DV])).astype(BF16)

    @pl.when(ci == nc - 1)
    def _():
        cf_ref[0] = c_scr[...]
        nf_ref[0] = n_scr[...]
        mf_ref[0] = m_scr[...]


def _ml_prompt(p3, lp):
    b, t, _ = p3.shape
    L = CHUNK
    nc = t // L
    bias = jnp.pad(jnp.concatenate([lp['ml_ib'], lp['ml_fb']]), (0, LANES - 2 * ML_HEADS)).reshape(1, LANES)
    vs = pl.BlockSpec((1, L, ML_W), lambda i, j: (i, j, 0))
    cs = pl.BlockSpec((1, ML_HEADS, ML_QK, ML_V), lambda i, j: (i, 0, 0, 0))
    ns = pl.BlockSpec((1, ML_HPAD, ML_QK), lambda i, j: (i, 0, 0))
    ms = pl.BlockSpec((1, ML_HPAD, LANES), lambda i, j: (i, 0, 0))
    o, c_f, n_f, m_f = pl.pallas_call(
        functools.partial(_ml_fused_body, nc=nc),
        grid=(b, nc),
        in_specs=[pl.BlockSpec((1, L, P_MAIN), lambda i, j: (i, j, P_ML // P_MAIN)),
                  pl.BlockSpec((1, L, LANES), lambda i, j: (i, j, P_GATE // LANES)),
                  pl.BlockSpec((1, LANES), lambda i, j: (0, 0)),
                  pl.BlockSpec((1, ML_W), lambda i, j: (0, 0))],
        out_specs=[vs, cs, ns, ms],
        out_shape=[jax.ShapeDtypeStruct((b, t, ML_W), BF16),
                   jax.ShapeDtypeStruct((b, ML_HEADS, ML_QK, ML_V), F32),
                   jax.ShapeDtypeStruct((b, ML_HPAD, ML_QK), F32),
                   jax.ShapeDtypeStruct((b, ML_HPAD, LANES), F32)],
        scratch_shapes=[pltpu.VMEM((ML_HEADS, ML_QK, ML_V), F32),
                        pltpu.VMEM((ML_HPAD, ML_QK), F32),
                        pltpu.VMEM((ML_HPAD, LANES), F32)],
        compiler_params=_cparams(("parallel", "arbitrary")),
        name="ml_fused",
    )(p3, p3, bias, lp['ml_norm'].reshape(1, ML_W))
    return o, c_f, n_f[:, :ML_HEADS], m_f[:, :ML_HEADS, 0]


def _to_cols(x):
    b, c = x.shape
    return x.reshape(b // DEC_TB, DEC_TB, c).transpose(0, 2, 1)


def _rwkv_step_body(w_ref, ap_ref, k_ref, kk_ref, r_ref, v_ref, s_ref, y_ref, so_ref):
    N = RW_HD
    v = v_ref[0]

    def sa_step(i, acc):
        return acc + kk_ref[0, pl.ds(i, 1), :] * s_ref[i]

    sa = lax.fori_loop(0, N, sa_step, jnp.zeros_like(v), unroll=8)

    def upd_step(i, y):
        row = lambda ref: ref[0, pl.ds(i, 1), :]
        s_new = row(w_ref) * s_ref[i] - row(ap_ref) * sa + row(k_ref) * v
        so_ref[i] = s_new
        return y + row(r_ref) * s_new

    y_ref[0] = lax.fori_loop(0, N, upd_step, jnp.zeros_like(v), unroll=8)


def _rwkv_step(wdec, ap, k, kk, r, v, s_all, layer):
    b, w = v.shape
    heads = lambda x: x.reshape(b, RW_HEADS, RW_HD).transpose(1, 2, 0)
    vec = pl.BlockSpec((1, RW_HD, b), lambda h: (h, 0, 0))
    y, s_new = pl.pallas_call(
        _rwkv_step_body,
        grid=(RW_HEADS,),
        in_specs=[vec] * 6 + [pl.BlockSpec((None, None, RW_HD, RW_HD, b), lambda h: (layer, h, 0, 0, 0))],
        out_specs=[vec, pl.BlockSpec((None, RW_HD, RW_HD, b), lambda h: (h, 0, 0, 0))],
        out_shape=[jax.ShapeDtypeStruct((RW_HEADS, RW_HD, b), F32),
                   jax.ShapeDtypeStruct((RW_HEADS, RW_HD, RW_HD, b), F32)],
        compiler_params=_cparams(("parallel",)),
        name="rwkv_step",
    )(heads(wdec), heads(ap), heads(k), heads(kk), heads(r), heads(v), s_all)
    return y.transpose(2, 0, 1).reshape(b, w), s_new


def _ret_step_body(qc_ref, kc_ref, v_ref, s_ref, y_ref, so_ref):
    DK, DV = RET_QK, RET_V
    for j in range(DEC_TB):
        for h in range(RET_HEADS):
            gamma = math.exp(_ret_log_gamma(h))
            q = qc_ref[0, h * DK:(h + 1) * DK, j:j + 1]
            k = kc_ref[0, h * DK:(h + 1) * DK, j:j + 1]
            v = v_ref[j:j + 1, h * DV:(h + 1) * DV]
            s = s_ref[j, h]
            qk = jnp.sum(q * k, axis=0, keepdims=True)
            y_ref[j:j + 1, h * DV:(h + 1) * DV] = qk * v + gamma * jnp.sum(q * s, axis=0, keepdims=True)
            so_ref[j, h] = gamma * s + k * v


def _ret_step(q, k, v, s0):
    b = q.shape[0]
    cols = pl.BlockSpec((1, RET_HEADS * RET_QK, DEC_TB), lambda i: (i, 0, 0))
    rows = pl.BlockSpec((DEC_TB, RET_W), lambda i: (i, 0))
    st = pl.BlockSpec((DEC_TB, RET_HEADS, RET_QK, RET_V), lambda i: (i, 0, 0, 0))
    return pl.pallas_call(
        _ret_step_body,
        grid=(b // DEC_TB,),
        in_specs=[cols, cols, rows, st],
        out_specs=[rows, st],
        out_shape=[jax.ShapeDtypeStruct((b, RET_W), F32), jax.ShapeDtypeStruct(s0.shape, F32)],
        compiler_params=_cparams(("parallel",)),
        name="ret_step",
    )(_to_cols(q), _to_cols(k), v, s0)


def _ml_step_body(qc_ref, kc_ref, q_ref, k_ref, v_ref, ig_ref, lf_ref, c_ref, n_ref, m_ref,
                  h_ref, co_ref, no_ref, mo_ref):
    DK, DV = ML_QK, ML_V
    for j in range(DEC_TB):
        for h in range(ML_HEADS):
            ks = slice(h * DK, (h + 1) * DK)
            vs = slice(h * DV, (h + 1) * DV)
            q_col = qc_ref[0, ks, j:j + 1]
            k_col = kc_ref[0, ks, j:j + 1]
            q_row = q_ref[j:j + 1, ks]
            k_row = k_ref[j:j + 1, ks]
            v = v_ref[j:j + 1, vs]
            ig = ig_ref[j:j + 1, h:h + 1]
            lf = lf_ref[j:j + 1, h:h + 1]
            m_prev = m_ref[j:j + 1, h:h + 1]
            c_prev = c_ref[j, h]
            n_prev = n_ref[j, h:h + 1, :]
            m_new = jnp.maximum(lf + m_prev, ig)
            dec = jnp.exp(lf + m_prev - m_new)
            wgt = jnp.exp(ig - m_new)
            co_ref[j, h] = dec * c_prev + (k_col * wgt) * v
            no_ref[j, h:h + 1, :] = dec * n_prev + k_row * wgt
            mo_ref[j:j + 1, h:h + 1] = m_new
            s = jnp.sum(q_row * k_row, axis=1, keepdims=True) * wgt
            num = s * v + dec * jnp.sum(q_col * c_prev, axis=0, keepdims=True)
            den = s + dec * jnp.sum(q_row * n_prev, axis=1, keepdims=True)
            h_ref[j:j + 1, vs] = num / jnp.maximum(jnp.abs(den), jnp.exp(-m_new))


def _ml_step(q, k, v, ig, lf, c0, n0, m0):
    b = q.shape[0]
    cols = pl.BlockSpec((1, ML_HEADS * ML_QK, DEC_TB), lambda i: (i, 0, 0))
    qk_rows = pl.BlockSpec((DEC_TB, ML_HEADS * ML_QK), lambda i: (i, 0))
    rows = pl.BlockSpec((DEC_TB, ML_W), lambda i: (i, 0))
    sc = pl.BlockSpec((DEC_TB, ML_HEADS), lambda i: (i, 0))
    cs = pl.BlockSpec((DEC_TB, ML_HEADS, ML_QK, ML_V), lambda i: (i, 0, 0, 0))
    ns = pl.BlockSpec((DEC_TB, ML_HEADS, ML_QK), lambda i: (i, 0, 0))
    return pl.pallas_call(
        _ml_step_body,
        grid=(b // DEC_TB,),
        in_specs=[cols, cols, qk_rows, qk_rows, rows, sc, sc, cs, ns, sc],
        out_specs=[rows, cs, ns, sc],
        out_shape=[jax.ShapeDtypeStruct((b, ML_W), F32), jax.ShapeDtypeStruct(c0.shape, F32),
                   jax.ShapeDtypeStruct(n0.shape, F32), jax.ShapeDtypeStruct(m0.shape, F32)],
        compiler_params=_cparams(("parallel",)),
        name="ml_step",
    )(_to_cols(q), _to_cols(k), q, k, v, ig, lf, c0, n0, m0)


def _heads(a, h):
    return a.reshape(a.shape[:-1] + (h, a.shape[-1] // h))


def _shift_prev(p, prev_row):
    return jnp.concatenate([prev_row[:, None, :], p[:, :-1]], axis=1)


def _rotary(x, pos):
    half = x.shape[-1] // 2
    inv = ROPE_BASE ** (-jnp.arange(half, dtype=F32) / half)
    ang = pos.astype(F32)[:, None] * inv[None, :]
    cos = jnp.cos(ang)[None, :, None, :]
    sin = jnp.sin(ang)[None, :, None, :]
    x1, x2 = x[..., :half], x[..., half:]
    return jnp.concatenate([x1 * cos - x2 * sin, x1 * sin + x2 * cos], -1)


def _small_matmul(x, w):
    lead = x.shape[:-1]
    kdim, n = w.shape
    x2 = x.reshape(-1, kdim)
    m = x2.shape[0]
    kp = -(-kdim // LANES) * LANES
    npad = -(-n // LANES) * LANES
    x2 = jnp.pad(x2.astype(BF16), ((0, 0), (0, kp - kdim)))
    w2 = jnp.pad(w.astype(BF16), ((0, kp - kdim), (0, npad - n)))
    tm = 1024 if m % 1024 == 0 else m
    out = _matmul(x2, w2, tm, npad)
    return out[:, :n].reshape(lead + (n,))


def _mix_prompt(p, pos, v_first, lp):
    o_rw, v_first, s_new = _rwkv_prompt(p, lp, v_first, (1, 1, 1, 1))
    o_ret, r_new = _ret_prompt(p, pos)
    o_ml, c_new, n_new, m_new = _ml_prompt(p, lp)
    return (o_rw, o_ret, o_ml), v_first, (s_new, r_new, c_new, n_new, m_new)


def _mix_sample(p, pos, v_first, st, lp, prev_row):
    bsz, t_len, _ = p.shape

    p_rw = p[..., :RW_P]
    mixed = p_rw + (_shift_prev(p_rw, prev_row[:, :RW_P]) - p_rw) * lp['rw_mu']
    sizes = np.cumsum([RW_W, RW_W, RW_W, RW_LORA_W, RW_LORA_A, RW_LORA_G])[:-1]
    r, k, v, xw, xa, xg = jnp.split(mixed, [int(s) for s in sizes], axis=-1)
    w = -jax.nn.softplus(-(lp['rw_w0'] + _small_matmul(jnp.tanh(xw), lp['rw_w2']))) - 0.5
    a = jax.nn.sigmoid(lp['rw_a0'] + _small_matmul(xa, lp['rw_a2']))
    g = _small_matmul(jax.nn.sigmoid(xg), lp['rw_g2'])
    if v_first is None:
        v_first = v
    else:
        pv = p[..., P_V1:P_V1 + RW_LORA_V]
        xv = pv + (_shift_prev(pv, prev_row[:, P_V1:P_V1 + RW_LORA_V]) - pv) * lp['rw_vmu']
        v = v + (v_first - v) * jax.nn.sigmoid(lp['rw_v0'] + _small_matmul(xv, lp['rw_v2']))
    kk = _heads(k * lp['rw_kk'], RW_HEADS)
    kk = kk * lax.rsqrt(jnp.maximum(jnp.sum(jnp.square(kk), -1, keepdims=True), 1e-24))
    kk = kk.reshape(bsz, t_len, RW_W)
    k = k * (1.0 + (a - 1.0) * lp['rw_ka'])
    lw = -jnp.exp(w)
    y, s_new = _rwkv_step(jnp.exp(lw)[:, 0], (kk * a)[:, 0], k[:, 0], kk[:, 0], r[:, 0], v[:, 0],
                          st['rw_wkv_t'], st['layer'])
    y = _heads(y[:, None, :], RW_HEADS)
    y_mu = jnp.mean(y, -1, keepdims=True)
    y_var = jnp.mean(jnp.square(y - y_mu), -1, keepdims=True)
    y = ((y - y_mu) * lax.rsqrt(y_var + RW_GN_EPS)).reshape(bsz, t_len, RW_W)
    y = y * lp['rw_lnx_g'] + lp['rw_lnx_b']
    rh, kh, vh = (_heads(u, RW_HEADS) for u in (r, k, v))
    bonus = jnp.sum(rh * kh * _heads(lp['rw_rk'], RW_HEADS), -1, keepdims=True) * vh
    o_rw = ((y + bonus.reshape(bsz, t_len, RW_W)) * g).astype(BF16)

    nqk = RET_HEADS * RET_QK
    p_ret = p[..., P_RET:P_RET + P_MAIN]
    qr, kr, vr, gr = (p_ret[..., :nqk], p_ret[..., nqk:2 * nqk],
                      p_ret[..., 2 * nqk:2 * nqk + RET_W], p_ret[..., 2 * nqk + RET_W:])
    qh = _rotary(_heads(qr, RET_HEADS), pos).reshape(bsz, t_len, nqk)
    khr = (_rotary(_heads(kr, RET_HEADS), pos) * (RET_QK ** -0.5)).reshape(bsz, t_len, nqk)
    yr, r_new = _ret_step(qh[:, 0], khr[:, 0], vr[:, 0], st['ret'])
    yr = _heads(yr[:, None, :], RET_HEADS)
    yr = yr * lax.rsqrt(jnp.mean(jnp.square(yr), -1, keepdims=True) + NORM_EPS)
    o_ret = (jax.nn.silu(gr) * yr.reshape(bsz, t_len, RET_W)).astype(BF16)

    nqk = ML_HEADS * ML_QK
    p_ml = p[..., P_ML:P_ML + P_MAIN]
    qm, km, vm, om = (p_ml[..., :nqk], p_ml[..., nqk:2 * nqk],
                      p_ml[..., 2 * nqk:2 * nqk + ML_W], p_ml[..., 2 * nqk + ML_W:])
    im = p[..., P_GATE:P_GATE + ML_HEADS]
    fm = p[..., P_GATE + ML_HEADS:P_GATE + 2 * ML_HEADS]
    ig = ML_GATE_CAP * jnp.tanh((im + lp['ml_ib']) / ML_GATE_CAP)
    lf = jax.nn.log_sigmoid(ML_GATE_CAP * jnp.tanh((fm + lp['ml_fb']) / ML_GATE_CAP))
    km = km * (ML_QK ** -0.5)
    hm, c_new, n_new, m_new = _ml_step(qm[:, 0], km[:, 0], vm[:, 0], ig[:, 0], lf[:, 0],
                                       st['ml_c'], st['ml_n'], st['ml_m'])
    hm = _heads(hm[:, None, :], ML_HEADS)
    hm = hm * lax.rsqrt(jnp.mean(jnp.square(hm), -1, keepdims=True) + NORM_EPS)
    o_ml = (jax.nn.sigmoid(om) * (hm.reshape(bsz, t_len, ML_W) * lp['ml_norm'])).astype(BF16)

    return (o_rw, o_ret, o_ml), v_first, (s_new, r_new, c_new, n_new, m_new)


_T_V1 = P_V1 // LANES
_T_GATE = P_GATE // LANES
_T_RET = P_RET // LANES
_T_SHIFT = (P_RET - RW_P) // LANES
_T_SRC_GATE = (RW_P + RET_P + 2 * ML_HEADS * ML_QK + 2 * ML_W) // LANES


def _pack_body(w_ref, v1_ref, *o_refs):
    j = pl.program_id(0)
    row = lax.broadcasted_iota(jnp.int32, (LANES, 1), 0)
    spare = jnp.logical_and(j > _T_GATE, j < _T_RET)
    for l, o_ref in enumerate(o_refs):
        w = w_ref[:, l, :]
        gates = jnp.where(row < 2 * ML_HEADS, w, 0.0)
        out = jnp.where(j == _T_V1, v1_ref[l], jnp.where(j == _T_GATE, gates, jnp.where(spare, 0.0, w)))
        o_ref[...] = out.astype(BF16)


def _pack_w_in(w_in, rw_v1):
    depth, d, _ = w_in.shape
    w_t = w_in.transpose(2, 0, 1)
    v1_t = jnp.pad(rw_v1.transpose(0, 2, 1), ((1, 0), (0, LANES - RW_LORA_V), (0, 0)))

    def src_tile(j):
        return jnp.where(j < _T_V1, j, jnp.where(j == _T_GATE, _T_SRC_GATE, j - _T_SHIFT))

    return pl.pallas_call(
        _pack_body,
        grid=(P_PAD // LANES,),
        in_specs=[pl.BlockSpec((LANES, depth, d), lambda j: (src_tile(j), 0, 0)),
                  pl.BlockSpec((depth, LANES, d), lambda j: (0, 0, 0))],
        out_specs=[pl.BlockSpec((LANES, d), lambda j: (j, 0))] * depth,
        out_shape=[jax.ShapeDtypeStruct((P_PAD, d), BF16)] * depth,
        compiler_params=_cparams(("parallel",)),
        name="pack_w_in",
    )(w_t, v1_t)


def _token_tiles(m):
    if m % 2048 == 0:
        return 2048, 1024, 1024
    return m, m, m


def kernel(x_prompt, x_sample, state_rw_shift, state_rw_wkv, state_ret, state_ml_c, state_ml_n, state_ml_m,
           ln0_g, ln0_b, w_in, rw_mu, rw_w0, rw_w2, rw_a0, rw_a2, rw_g2, rw_kk, rw_ka, rw_rk,
           rw_lnx_g, rw_lnx_b, rw_v0, rw_v1, rw_vmu, rw_v2, ml_ib, ml_fb, ml_norm, w_out,
           ln1_g, ln1_b, w_gate, w_up, w_down, ln2_g, ln2_b):
    bp, tp, d = x_prompt.shape
    bs, ts, _ = x_sample.shape
    groups = {
        'p': dict(b=bp, t=tp, pos=jnp.arange(tp), chunked=True),
        's': dict(b=bs, t=ts, pos=PAST_LEN + jnp.arange(ts), chunked=False),
    }
    rw_wkv_t = state_rw_wkv.transpose(0, 2, 3, 4, 1)
    w_in_packed = _pack_w_in(w_in, rw_v1)
    xs_f, xs_b, v_first, outs = {}, {}, {}, {}
    for name, x in (('p', x_prompt), ('s', x_sample)):
        m = x.shape[0] * x.shape[1]
        xs_f[name], xs_b[name] = _layernorm(x.reshape(m, d), ln0_g, ln0_b, _token_tiles(m)[1])
        v_first[name] = None
        outs[name] = []

    for l in range(DEPTH):
        lp = {
            'rw_mu': rw_mu[l], 'rw_w0': rw_w0[l], 'rw_w2': rw_w2[l], 'rw_a0': rw_a0[l], 'rw_a2': rw_a2[l],
            'rw_g2': rw_g2[l], 'rw_kk': rw_kk[l], 'rw_ka': rw_ka[l], 'rw_rk': rw_rk[l],
            'rw_lnx_g': rw_lnx_g[l], 'rw_lnx_b': rw_lnx_b[l], 'ml_ib': ml_ib[l], 'ml_fb': ml_fb[l],
            'ml_norm': ml_norm[l],
        }
        if l > 0:
            lp.update(rw_v0=rw_v0[l - 1], rw_vmu=rw_vmu[l - 1], rw_v2=rw_v2[l - 1])
        w_in_b = w_in_packed[l]
        w_out_b = w_out[l].astype(BF16)
        w_down_b = w_down[l].astype(BF16)
        prev_s = _matmul(state_rw_shift[l].astype(BF16), w_in_b, bs, P_TN, w_transposed=True)
        for name, grp in groups.items():
            b, t = grp['b'], grp['t']
            m = b * t
            tm_big, tm_out, tm_down = _token_tiles(m)
            p = _matmul(xs_b[name], w_in_b, tm_big, P_TN, w_transposed=True).reshape(b, t, P_PAD)
            if grp['chunked']:
                o, v_first[name], new_st = _mix_prompt(p, grp['pos'], v_first[name], lp)
            else:
                st = {'rw_wkv_t': rw_wkv_t, 'layer': l, 'ret': state_ret[l], 'ml_c': state_ml_c[l],
                      'ml_n': state_ml_n[l], 'ml_m': state_ml_m[l]}
                o, v_first[name], new_st = _mix_sample(p, grp['pos'], v_first[name], st, lp, prev_s)
            outs[name].append((xs_f[name].reshape(b, t, d)[:, -1],) + new_st)
            o_rw, o_ret, o_ml = (u.reshape(m, u.shape[-1]) for u in o)
            x1_f, x1_b = _out_proj_ln(o_rw, o_ret, o_ml, w_out_b, xs_f[name], ln1_g[l], ln1_b[l], tm_out)
            hdn = _matmul_swiglu(x1_b, w_gate, w_up, l, tm_big, 512)
            xs_f[name], xs_b[name] = _matmul_res_ln(hdn, w_down_b, x1_f, ln2_g[l], ln2_b[l], tm_down, 512)

    y_p = xs_f['p'].reshape(bp, tp, d)
    y_s = xs_f['s'].reshape(bs, ts, d)
    sp = [jnp.stack([o[i] for o in outs['p']]) for i in range(6)]
    ss = [jnp.stack([o[i] for o in outs['s']]) for i in range(6)]
    ss[1] = ss[1].transpose(0, 4, 1, 2, 3)
    return (y_p, y_s, sp[0], sp[1], sp[2], sp[3], sp[4], sp[5], ss[0], ss[1], ss[2], ss[3], ss[4], ss[5])
```

```python
import functools
import math

import numpy as np
import jax
import jax.numpy as jnp
from jax import lax
from jax.experimental import pallas as pl
from jax.experimental.pallas import tpu as pltpu

F32 = jnp.float32
BF16 = jnp.bfloat16

D_MODEL = 2048
DEPTH = 2
PAST_LEN = 16384
RW_HD = 64
RW_W = D_MODEL // 4
RW_HEADS = RW_W // RW_HD
RW_LORA_W = 64
RW_LORA_A = 64
RW_LORA_V = 32
RW_LORA_G = 128
RW_P = 3 * RW_W + RW_LORA_W + RW_LORA_A + RW_LORA_G
RW_GN_EPS = 64e-5
RET_V = 128
RET_QK = 64
RET_W = 3 * D_MODEL // 8
RET_HEADS = RET_W // RET_V
RET_P = 2 * RET_HEADS * RET_QK + 2 * RET_W
ML_V = 128
ML_QK = 64
ML_W = D_MODEL - RW_W - RET_W
ML_HEADS = ML_W // ML_V
ML_P = 2 * ML_HEADS * ML_QK + 2 * ML_W + 2 * ML_HEADS
ML_GATE_CAP = 15.0
P_TOTAL = RW_P + RET_P + ML_P
D_FF = ((8 * D_MODEL + 3 * 256 - 1) // (3 * 256)) * 256
CHUNK = 128
ROPE_BASE = 10000.0
LN_EPS = 1e-5
NORM_EPS = 1e-6
ALPHA = (2 * DEPTH) ** 0.25

LANES = 128
P_V1 = RW_P
P_GATE = RW_P + LANES
P_RET = 2304
P_ML = 2 * P_RET
P_MAIN = 2304
P_PAD = 3 * P_RET
P_TN = 768
RW_CHUNK = 64
DEC_TB = 8
VMEM_LIMIT = 56 * 1024 * 1024

HIGHEST = lax.Precision.HIGHEST


def _cparams(sem):
    return pltpu.CompilerParams(dimension_semantics=sem, vmem_limit_bytes=VMEM_LIMIT)


def _dot(a, b, precision=None):
    return lax.dot_general(a, b, (((1,), (0,)), ((), ())), precision=precision, preferred_element_type=F32)


def _dot_nt(a, b, precision=None):
    return lax.dot_general(a, b, (((1,), (1,)), ((), ())), precision=precision, preferred_element_type=F32)


def _dot_tn(a, b, precision=None):
    return lax.dot_general(a, b, (((0,), (0,)), ((), ())), precision=precision, preferred_element_type=F32)


def _ln_rows(x, g, b):
    mu = jnp.mean(x, -1, keepdims=True)
    xc = x - mu
    var = jnp.mean(xc * xc, -1, keepdims=True)
    return xc * lax.rsqrt(var + LN_EPS) * g + b


def _ln_body(x_ref, g_ref, b_ref, of_ref, ob_ref):
    y = _ln_rows(x_ref[...], g_ref[...], b_ref[...])
    of_ref[...] = y
    ob_ref[...] = y.astype(BF16)


def _layernorm(x, g, b, tm):
    m, d = x.shape
    return pl.pallas_call(
        _ln_body,
        grid=(m // tm,),
        in_specs=[pl.BlockSpec((tm, d), lambda i: (i, 0)),
                  pl.BlockSpec((1, d), lambda i: (0, 0)),
                  pl.BlockSpec((1, d), lambda i: (0, 0))],
        out_specs=[pl.BlockSpec((tm, d), lambda i: (i, 0)),
                   pl.BlockSpec((tm, d), lambda i: (i, 0))],
        out_shape=[jax.ShapeDtypeStruct((m, d), F32), jax.ShapeDtypeStruct((m, d), BF16)],
        compiler_params=_cparams(("parallel",)),
        name="layernorm",
    )(x, g.reshape(1, d), b.reshape(1, d))


def _mm_body(x_ref, w_ref, o_ref, *, w_transposed):
    dot = _dot_nt if w_transposed else _dot
    o_ref[...] = dot(x_ref[...], w_ref[...]).astype(o_ref.dtype)


def _matmul(x, w, tm, tn, out_dtype=F32, w_transposed=False):
    m, k = x.shape
    n = w.shape[0] if w_transposed else w.shape[1]
    w_spec = (pl.BlockSpec((tn, k), lambda i, j: (j, 0)) if w_transposed
              else pl.BlockSpec((k, tn), lambda i, j: (0, j)))
    return pl.pallas_call(
        functools.partial(_mm_body, w_transposed=w_transposed),
        grid=(m // tm, n // tn),
        in_specs=[pl.BlockSpec((tm, k), lambda i, j: (i, 0)), w_spec],
        out_specs=pl.BlockSpec((tm, tn), lambda i, j: (i, j)),
        out_shape=jax.ShapeDtypeStruct((m, n), out_dtype),
        compiler_params=_cparams(("parallel", "parallel")),
        name="matmul",
    )(x, w)


def _swiglu_body(x_ref, wg_ref, wu_ref, o_ref, wg_scr, wu_scr):
    @pl.when(pl.program_id(1) == 0)
    def _():
        wg_scr[...] = wg_ref[...].astype(BF16)
        wu_scr[...] = wu_ref[...].astype(BF16)

    x = x_ref[...]
    g = _dot(x, wg_scr[...])
    u = _dot(x, wu_scr[...])
    o_ref[...] = (g * jax.nn.sigmoid(g) * u).astype(o_ref.dtype)


def _matmul_swiglu(x, wg, wu, layer, tm, tn):
    m, k = x.shape
    n = wg.shape[2]
    w_spec = pl.BlockSpec((None, k, tn), lambda j, i: (layer, 0, j))
    return pl.pallas_call(
        _swiglu_body,
        grid=(n // tn, m // tm),
        in_specs=[pl.BlockSpec((tm, k), lambda j, i: (i, 0)), w_spec, w_spec],
        out_specs=pl.BlockSpec((tm, tn), lambda j, i: (i, j)),
        out_shape=jax.ShapeDtypeStruct((m, n), BF16),
        scratch_shapes=[pltpu.VMEM((k, tn), BF16), pltpu.VMEM((k, tn), BF16)],
        compiler_params=_cparams(("parallel", "arbitrary")),
        name="matmul_swiglu",
    )(x, wg, wu)


DOWN_TK = D_FF // 4
LN_ROWS = 256
OUT_ROWS = 256


def _mm_res_ln_body(x_ref, w_ref, res_ref, g_ref, b_ref, of_ref, ob_ref, *, nk, tm):
    kk = pl.program_id(1)
    part = _dot(x_ref[...], w_ref[...])

    @pl.when(kk == 0)
    def _():
        of_ref[...] = part

    @pl.when(kk > 0)
    def _():
        of_ref[...] += part

    @pl.when(kk == nk - 1)
    def _():
        for r in range(0, tm, min(LN_ROWS, tm)):
            rows = pl.ds(r, min(LN_ROWS, tm))
            y = _ln_rows(ALPHA * res_ref[rows, :] + of_ref[rows, :], g_ref[...], b_ref[...])
            of_ref[rows, :] = y
            ob_ref[rows, :] = y.astype(BF16)


def _matmul_res_ln(x, w, res, g, b, tm, tk):
    m, k = x.shape
    n = w.shape[1]
    nk = k // tk
    return pl.pallas_call(
        functools.partial(_mm_res_ln_body, nk=nk, tm=tm),
        grid=(m // tm, nk),
        in_specs=[pl.BlockSpec((tm, tk), lambda i, j: (i, j)),
                  pl.BlockSpec((tk, n), lambda i, j: (j, 0)),
                  pl.BlockSpec((tm, n), lambda i, j: (i, 0)),
                  pl.BlockSpec((1, n), lambda i, j: (0, 0)),
                  pl.BlockSpec((1, n), lambda i, j: (0, 0))],
        out_specs=[pl.BlockSpec((tm, n), lambda i, j: (i, 0)),
                   pl.BlockSpec((tm, n), lambda i, j: (i, 0))],
        out_shape=[jax.ShapeDtypeStruct((m, n), F32), jax.ShapeDtypeStruct((m, n), BF16)],
        compiler_params=_cparams(("parallel", "arbitrary")),
        name="matmul_res_ln",
    )(x, w, res, g.reshape(1, n), b.reshape(1, n))


def _out_proj_ln_body(o_rw_ref, o_ret_ref, o_ml_ref, w_ref, res_ref, g_ref, b_ref, of_ref, ob_ref, *, tm):
    for r in range(0, tm, min(OUT_ROWS, tm)):
        rows = pl.ds(r, min(OUT_ROWS, tm))
        mix = (_dot(o_rw_ref[rows, :], w_ref[0:RW_W, :])
               + _dot(o_ret_ref[rows, :], w_ref[RW_W:RW_W + RET_W, :])
               + _dot(o_ml_ref[rows, :], w_ref[RW_W + RET_W:, :]))
        y = _ln_rows(ALPHA * res_ref[rows, :] + mix, g_ref[...], b_ref[...])
        of_ref[rows, :] = y
        ob_ref[rows, :] = y.astype(BF16)


def _out_proj_ln(o_rw, o_ret, o_ml, w, res, g, b, tm):
    m = o_rw.shape[0]
    n = w.shape[1]
    rows = lambda width: pl.BlockSpec((tm, width), lambda i: (i, 0))
    full = lambda r, c: pl.BlockSpec((r, c), lambda i: (0, 0))
    return pl.pallas_call(
        functools.partial(_out_proj_ln_body, tm=tm),
        grid=(m // tm,),
        in_specs=[rows(RW_W), rows(RET_W), rows(ML_W), full(D_MODEL, n), rows(n), full(1, n), full(1, n)],
        out_specs=[rows(n), rows(n)],
        out_shape=[jax.ShapeDtypeStruct((m, n), F32), jax.ShapeDtypeStruct((m, n), BF16)],
        compiler_params=_cparams(("parallel",)),
        name="out_proj_ln",
    )(o_rw, o_ret, o_ml, w, res, g.reshape(1, n), b.reshape(1, n))


RW_TB = 256
RW_GH = 4
RW_GW = RW_GH * RW_HD
RW_VEC_ROWS = 8


def _split3(x):
    hi = x.astype(BF16)
    r1 = x - hi.astype(F32)
    mid = r1.astype(BF16)
    lo = (r1 - mid.astype(F32)).astype(BF16)
    return hi, mid, lo


def _mm(a, b, dims, passes):
    dg = lambda x, y: lax.dot_general(x, y, (dims, ((), ())), preferred_element_type=F32)
    if passes == 6:
        return lax.dot_general(a, b, (dims, ((), ())), precision=HIGHEST, preferred_element_type=F32)
    ah = a.astype(BF16)
    bh = b.astype(BF16)
    if passes == 1:
        return dg(ah, bh)
    al = (a - ah.astype(F32)).astype(BF16)
    bl = (b - bh.astype(F32)).astype(BF16)
    return dg(ah, bh) + (dg(ah, bl) + dg(al, bh))


_NN = ((1,), (0,))
_NT = ((1,), (1,))
_TN = ((0,), (0,))


def _exact_lhs_dot(a_bf16, b):
    hi, mid, lo = _split3(b)
    dg = lambda y: lax.dot_general(a_bf16, y, (_NN, ((), ())), preferred_element_type=F32)
    return dg(hi) + (dg(mid) + dg(lo))


def _exact_rhs_dot(a, b_bf16):
    hi, mid, lo = _split3(a)
    dg = lambda x: lax.dot_general(x, b_bf16, (_NN, ((), ())), preferred_element_type=F32)
    return dg(hi) + (dg(mid) + dg(lo))


def _rw_scan_block(r, lw, k, v, kk, a, st, masks, passes):
    L = RW_CHUNK
    tri, strict, lower, eye, mask_bd = masks
    p_sc, p_inv, p_app, p_st = passes
    bd = lambda x: jnp.where(mask_bd, jnp.concatenate([x] * RW_GH, axis=0), 0.0)
    cum = _exact_lhs_dot(tri, lw)
    tot = cum[L - 1:L, :]
    e_neg = jnp.exp(-cum)
    ap = kk * a
    ap_h = ap * e_neg
    k_h = k * e_neg
    lhs = jnp.concatenate([kk * jnp.exp(cum - lw), r * jnp.exp(cum)], axis=0)
    sc_a = _mm(lhs, bd(ap_h), _NT, p_sc)
    sc_k = _mm(lhs, bd(k_h), _NT, p_sc)
    n_m = jnp.where(strict, sc_a[:L], 0.0)
    m_a = jnp.where(lower, sc_a[L:], 0.0)
    m_k = jnp.where(strict, sc_k[:L], 0.0)
    m_r = jnp.where(lower, sc_k[L:], 0.0)
    inv = eye - n_m
    pw = n_m
    for _ in range(int(math.log2(L)) - 1):
        pw = _mm(pw, bd(pw), _NN, p_inv)
        inv = inv + _mm(inv, bd(pw), _NN, p_inv)
    s_terms = _mm(lhs, st, _NT, p_app)
    mv = _mm(jnp.concatenate([m_k, m_r], axis=0), bd(v), _NN, p_app)
    u = _mm(inv, bd(s_terms[:L] + mv[:L]), _NN, p_app)
    y = s_terms[L:] + mv[L:] - _mm(m_a, bd(u), _NN, p_app)
    e_end = jnp.exp(tot - cum)
    upd = _mm(jnp.concatenate([v, -u], axis=0), jnp.concatenate([k * e_end, ap * e_end], axis=0), _TN, p_st)
    st_new = jnp.where(mask_bd, st * jnp.exp(tot) + upd, 0.0)
    return y, st_new


def _rw_masks():
    L, G = RW_CHUNK, RW_GW
    row = lax.broadcasted_iota(jnp.int32, (L, G), 0)
    col = lax.broadcasted_iota(jnp.int32, (L, G), 1) & (L - 1)
    rl = lax.broadcasted_iota(jnp.int32, (L, L), 0)
    cl = lax.broadcasted_iota(jnp.int32, (L, L), 1)
    rg = lax.broadcasted_iota(jnp.int32, (G, G), 0) // RW_HD
    cg = lax.broadcasted_iota(jnp.int32, (G, G), 1) // RW_HD
    tri = (rl >= cl).astype(BF16)
    return tri, row > col, row >= col, (row == col).astype(F32), rg == cg


def _softplus(z):
    return jnp.maximum(z, 0.0) + jnp.log(1.0 + jnp.exp(-jnp.abs(z)))


def _rwkv_fused_body(*refs, nc, has_vres, passes):
    if has_vres:
        (p_ref, pv_ref, vf_ref, mu_ref, vec_ref, wa_ref, g2_ref, seg_ref, vmu_ref, v2_ref,
         o_ref, sf_ref, st_scr, prev_scr, y_scr, prevv_scr) = refs
    else:
        (p_ref, mu_ref, vec_ref, wa_ref, g2_ref, seg_ref,
         o_ref, vfo_ref, sf_ref, st_scr, prev_scr, y_scr) = refs
    TB, W, L = RW_TB, RW_W, RW_CHUNK
    c = pl.program_id(1)

    @pl.when(c == 0)
    def _():
        st_scr[...] = jnp.zeros_like(st_scr)
        prev_scr[...] = jnp.zeros_like(prev_scr)
        if has_vres:
            prevv_scr[...] = jnp.zeros_like(prevv_scr)

    first_row = lax.broadcasted_iota(jnp.int32, (TB, 1), 0) == 0

    def shift_mix(x, carry_ref, mu):
        prev = jnp.where(first_row, carry_ref[...], pltpu.roll(x, 1, 0))
        carry_ref[...] = x[TB - 1:TB, :]
        return x + (prev - x) * mu

    mixed = shift_mix(p_ref[0], prev_scr, mu_ref[...])
    r = mixed[:, 0:W]
    k = mixed[:, W:2 * W]
    v = mixed[:, 2 * W:3 * W]
    xwa = mixed[:, 3 * W:3 * W + LANES]
    xg = mixed[:, 3 * W + LANES:3 * W + 2 * LANES]
    vec = vec_ref[...]
    w0, a0, kk_s, ka, rk, lnx_g, lnx_b, v0 = (vec[i:i + 1, :] for i in range(RW_VEC_ROWS))
    seg = seg_ref[...]
    wa = wa_ref[...]
    w_lora = _dot(jnp.tanh(xwa).astype(BF16), wa[:, 0:W])
    a_lora = _dot(xwa.astype(BF16), wa[:, W:2 * W])
    lw = -jnp.exp(-_softplus(-(w0 + w_lora)) - 0.5)
    a = jax.nn.sigmoid(a0 + a_lora)
    g = _dot(jax.nn.sigmoid(xg).astype(BF16), g2_ref[...])
    if has_vres:
        xv = shift_mix(pv_ref[0], prevv_scr, vmu_ref[...])
        v = v + (vf_ref[0] - v) * jax.nn.sigmoid(v0 + _dot(xv.astype(BF16), v2_ref[...]))
    else:
        vfo_ref[0] = v
    kk = k * kk_s
    kk = kk * lax.rsqrt(jnp.maximum(_exact_rhs_dot(kk * kk, seg), 1e-24))
    k = k * (1.0 + (a - 1.0) * ka)

    masks = _rw_masks()
    for s in range(TB // L):
        rs = slice(s * L, (s + 1) * L)
        for gi in range(RW_HEADS // RW_GH):
            cs = slice(gi * RW_GW, (gi + 1) * RW_GW)
            y, st_new = _rw_scan_block(r[rs, cs], lw[rs, cs], k[rs, cs], v[rs, cs], kk[rs, cs], a[rs, cs],
                                       st_scr[gi], masks, passes)
            y_scr[rs, cs] = y
            st_scr[gi] = st_new

    y = y_scr[...]
    inv_n = 1.0 / RW_HD
    y_mu = _exact_rhs_dot(y, seg) * inv_n
    yc = y - y_mu
    y_var = _exact_rhs_dot(yc * yc, seg) * inv_n
    y = yc * lax.rsqrt(y_var + RW_GN_EPS) * lnx_g + lnx_b
    bonus = _exact_rhs_dot(r * k * rk, seg) * v
    o_ref[0] = ((y + bonus) * g).astype(BF16)

    @pl.when(c == nc - 1)
    def _():
        sf_ref[0] = st_scr[...]


def _rwkv_prompt(p3, lp, v_first, passes=(3, 3, 3, 3)):
    b, t, _ = p3.shape
    nc = t // RW_TB
    has_vres = v_first is not None
    ng = RW_HEADS // RW_GH
    zpad = jnp.zeros((RW_LORA_W, RW_W), F32)
    wa = jnp.concatenate([jnp.concatenate([lp['rw_w2'], zpad], 0), jnp.concatenate([zpad, lp['rw_a2']], 0)], 1)
    vec = jnp.stack([lp['rw_w0'], lp['rw_a0'], lp['rw_kk'], lp['rw_ka'], lp['rw_rk'], lp['rw_lnx_g'], lp['rw_lnx_b'],
                     lp['rw_v0'] if has_vres else jnp.zeros((RW_W,), F32)])
    hid = jnp.arange(RW_W) // RW_HD
    seg = (hid[:, None] == hid[None, :]).astype(BF16)
    full = lambda shape: pl.BlockSpec(shape, lambda i, j: (0,) * len(shape))
    seq = lambda w, blk: pl.BlockSpec((1, RW_TB, w), lambda i, j: (i, j, blk))
    in_specs = [seq(RW_P, 0)]
    args = [p3]
    if has_vres:
        in_specs += [seq(LANES, P_V1 // LANES), seq(RW_W, 0)]
        args += [p3, v_first]
    in_specs += [full((1, RW_P)), full((RW_VEC_ROWS, RW_W)), full((LANES, 2 * RW_W)), full((RW_LORA_G, RW_W)),
                 full((RW_W, RW_W))]
    args += [lp['rw_mu'].reshape(1, RW_P), vec, wa.astype(BF16), lp['rw_g2'].astype(BF16), seg]
    if has_vres:
        in_specs += [full((1, LANES)), full((LANES, RW_W))]
        args += [jnp.pad(lp['rw_vmu'], (0, LANES - RW_LORA_V)).reshape(1, LANES),
                 jnp.pad(lp['rw_v2'], ((0, LANES - RW_LORA_V), (0, 0))).astype(BF16)]
    out_specs = [seq(RW_W, 0)]
    out_shape = [jax.ShapeDtypeStruct((b, t, RW_W), BF16)]
    if not has_vres:
        out_specs.append(seq(RW_W, 0))
        out_shape.append(jax.ShapeDtypeStruct((b, t, RW_W), F32))
    out_specs.append(pl.BlockSpec((1, ng, RW_GW, RW_GW), lambda i, j: (i, 0, 0, 0)))
    out_shape.append(jax.ShapeDtypeStruct((b, ng, RW_GW, RW_GW), F32))
    scratch = [pltpu.VMEM((ng, RW_GW, RW_GW), F32), pltpu.VMEM((1, RW_P), F32), pltpu.VMEM((RW_TB, RW_W), F32)]
    if has_vres:
        scratch.append(pltpu.VMEM((1, LANES), F32))
    outs = pl.pallas_call(
        functools.partial(_rwkv_fused_body, nc=nc, has_vres=has_vres, passes=passes),
        grid=(b, nc),
        in_specs=in_specs,
        out_specs=out_specs,
        out_shape=out_shape,
        scratch_shapes=scratch,
        compiler_params=_cparams(("parallel", "arbitrary")),
        name="rwkv_fused",
    )(*args)
    if has_vres:
        o, st_bd = outs
    else:
        o, v_first, st_bd = outs
    st5 = st_bd.reshape(b, ng, RW_GH, RW_HD, RW_GH, RW_HD)
    s_fin = jnp.stack([st5[:, :, h, :, h, :] for h in range(RW_GH)], axis=2)
    s_fin = s_fin.reshape(b, RW_HEADS, RW_HD, RW_HD).transpose(0, 1, 3, 2)
    return o, v_first, s_fin


def _ret_log_gamma(h):
    return math.log1p(-(2.0 ** (-5.0 - h)))


def _rotary_tables(pos, heads, dk):
    half = dk // 2
    inv = ROPE_BASE ** (-jnp.arange(half, dtype=F32) / half)
    ang = pos.astype(F32)[:, None] * inv[None, :]
    cos = jnp.tile(jnp.concatenate([jnp.cos(ang), jnp.cos(ang)], -1), (1, heads))
    sin = jnp.tile(jnp.concatenate([-jnp.sin(ang), jnp.sin(ang)], -1), (1, heads))
    lane = jnp.arange(heads * dk)
    perm = (lane[:, None] == (lane[None, :] ^ half)).astype(BF16)
    return cos, sin, perm


def _ret_fused_body(p_ref, cos_ref, sin_ref, perm_ref, o_ref, sf_ref, s_scr, *, nc):
    L, DK, DV, H = CHUNK, RET_QK, RET_V, RET_HEADS
    nq = H * DK
    c = pl.program_id(1)

    @pl.when(c == 0)
    def _():
        s_scr[...] = jnp.zeros_like(s_scr)

    cos = cos_ref[...]
    sin = sin_ref[...]
    perm = perm_ref[...]
    rot = lambda x: x * cos + _exact_rhs_dot(x, perm) * sin
    q_all = rot(p_ref[0, :, 0:nq])
    k_all = rot(p_ref[0, :, nq:2 * nq]) * (DK ** -0.5)
    row = lax.broadcasted_iota(jnp.int32, (L, L), 0)
    col = lax.broadcasted_iota(jnp.int32, (L, L), 1)
    rel = (row - col).astype(F32)
    idx = lax.broadcasted_iota(jnp.int32, (L, 1), 0).astype(F32)
    for h in range(H):
        lg = _ret_log_gamma(h)
        q = q_all[:, h * DK:(h + 1) * DK]
        k = k_all[:, h * DK:(h + 1) * DK]
        v = p_ref[0, :, 2 * nq + h * DV:2 * nq + (h + 1) * DV].astype(BF16)
        gate = p_ref[0, :, 2 * nq + RET_W + h * DV:2 * nq + RET_W + (h + 1) * DV]
        dmask = jnp.where(rel >= 0, jnp.exp(jnp.maximum(rel, 0.0) * lg), 0.0)
        scores = _dot_nt(q.astype(BF16), k.astype(BF16)) * dmask
        s_prev = s_scr[h]
        q_dec = q * jnp.exp((idx + 1.0) * lg)
        y = _dot(scores.astype(BF16), v) + _dot(q_dec.astype(BF16), s_prev.astype(BF16))
        k_end = k * jnp.exp((L - 1.0 - idx) * lg)
        s_scr[h] = math.exp(L * lg) * s_prev + _dot_tn(k_end.astype(BF16), v)
        y = y * lax.rsqrt(jnp.mean(y * y, -1, keepdims=True) + NORM_EPS)
        o_ref[0, :, h * DV:(h + 1) * DV] = (gate * jax.nn.sigmoid(gate) * y).astype(BF16)

    @pl.when(c == nc - 1)
    def _():
        sf_ref[0] = s_scr[...]


def _ret_prompt(p3, pos):
    b, t, _ = p3.shape
    L = CHUNK
    nc = t // L
    nq = RET_HEADS * RET_QK
    cos, sin, perm = _rotary_tables(pos, RET_HEADS, RET_QK)
    tab = pl.BlockSpec((L, nq), lambda i, j: (j, 0))
    st = pl.BlockSpec((1, RET_HEADS, RET_QK, RET_V), lambda i, j: (i, 0, 0, 0))
    return pl.pallas_call(
        functools.partial(_ret_fused_body, nc=nc),
        grid=(b, nc),
        in_specs=[pl.BlockSpec((1, L, P_MAIN), lambda i, j: (i, j, P_RET // P_MAIN)), tab, tab,
                  pl.BlockSpec((nq, nq), lambda i, j: (0, 0))],
        out_specs=[pl.BlockSpec((1, L, RET_W), lambda i, j: (i, j, 0)), st],
        out_shape=[jax.ShapeDtypeStruct((b, t, RET_W), BF16),
                   jax.ShapeDtypeStruct((b, RET_HEADS, RET_QK, RET_V), F32)],
        scratch_shapes=[pltpu.VMEM((RET_HEADS, RET_QK, RET_V), F32)],
        compiler_params=_cparams(("parallel", "arbitrary")),
        name="ret_fused",
    )(p3, cos, sin, perm)


ML_HPAD = 8


def _ml_fused_body(p_ref, gate_ref, bias_ref, norm_ref, o_ref, cf_ref, nf_ref, mf_ref, c_scr, n_scr, m_scr, *, nc):
    L, DK, DV, H = CHUNK, ML_QK, ML_V, ML_HEADS
    nq = H * DK
    ci = pl.program_id(1)

    @pl.when(ci == 0)
    def _():
        c_scr[...] = jnp.zeros_like(c_scr)
        n_scr[...] = jnp.zeros_like(n_scr)
        m_scr[...] = jnp.zeros_like(m_scr)

    row = lax.broadcasted_iota(jnp.int32, (L, L), 0)
    col = lax.broadcasted_iota(jnp.int32, (L, L), 1)
    causal = row >= col
    tri = causal.astype(BF16)
    capped = ML_GATE_CAP * jnp.tanh((gate_ref[0] + bias_ref[...]) * (1.0 / ML_GATE_CAP))
    lane = lax.broadcasted_iota(jnp.int32, (L, LANES), 1)
    g = jnp.where(lane < H, capped, jnp.where(lane < 2 * H, -_softplus(-capped), 0.0))
    cum = _exact_lhs_dot(tri, g)
    g_t = g.T
    cum_t = cum.T
    for h in range(H):
        q = p_ref[0, :, h * DK:(h + 1) * DK]
        k = p_ref[0, :, nq + h * DK:nq + (h + 1) * DK] * (DK ** -0.5)
        v = p_ref[0, :, 2 * nq + h * DV:2 * nq + (h + 1) * DV].astype(BF16)
        og = p_ref[0, :, 2 * nq + ML_W + h * DV:2 * nq + ML_W + (h + 1) * DV]
        ig_col = g[:, h:h + 1]
        ig_row = g_t[h:h + 1, :]
        b_col = cum[:, H + h:H + h + 1]
        b_row = cum_t[H + h:H + h + 1, :]
        b_tot = cum[L - 1:L, H + h:H + h + 1]
        m_prev = m_scr[h:h + 1, 0:1]
        c_prev = c_scr[h]
        n_prev = n_scr[h:h + 1, :]
        a_row = b_tot - b_row + ig_row
        a_col = b_tot - b_col + ig_col
        m_new = jnp.maximum(b_tot + m_prev, jnp.max(a_row, axis=1, keepdims=True))
        dec = jnp.exp(b_tot + m_prev - m_new)
        kw = k * jnp.exp(a_col - m_new)
        c_scr[h] = dec * c_prev + _dot_tn(kw.astype(BF16), v)
        n_scr[h:h + 1, :] = dec * n_prev + jnp.sum(kw, axis=0, keepdims=True)
        m_scr[h:h + 1, :] = jnp.broadcast_to(m_new, (1, LANES))
        dlog = jnp.where(causal, b_col - b_row + ig_row, -jnp.inf)
        inter = b_col + m_prev
        m_i = jnp.maximum(jnp.max(dlog, axis=1, keepdims=True), inter)
        s = _dot_nt(q.astype(BF16), k.astype(BF16)) * jnp.exp(dlog - m_i)
        sc = jnp.exp(inter - m_i)
        num = _dot(s.astype(BF16), v) + sc * _dot(q.astype(BF16), c_prev.astype(BF16))
        den = jnp.sum(s, axis=1, keepdims=True) + sc * jnp.sum(q * n_prev, axis=1, keepdims=True)
        hid = num / jnp.maximum(jnp.abs(den), jnp.exp(-m_i))
        hid = hid * lax.rsqrt(jnp.mean(hid * hid, -1, keepdims=True) + NORM_EPS)
        o_ref[0, :, h * DV:(h + 1) * DV] = (jax.nn.sigmoid(og) * (hid * norm_ref[:, h * DV:(h + 1) * DV])).astype(BF16)

    @pl.when(ci == nc - 1)
    def _():
        cf_ref[0] = c_scr[...]
        nf_ref[0] = n_scr[...]
        mf_ref[0] = m_scr[...]


def _ml_prompt(p3, lp):
    b, t, _ = p3.shape
    L = CHUNK
    nc = t // L
    bias = jnp.pad(jnp.concatenate([lp['ml_ib'], lp['ml_fb']]), (0, LANES - 2 * ML_HEADS)).reshape(1, LANES)
    vs = pl.BlockSpec((1, L, ML_W), lambda i, j: (i, j, 0))
    cs = pl.BlockSpec((1, ML_HEADS, ML_QK, ML_V), lambda i, j: (i, 0, 0, 0))
    ns = pl.BlockSpec((1, ML_HPAD, ML_QK), lambda i, j: (i, 0, 0))
    ms = pl.BlockSpec((1, ML_HPAD, LANES), lambda i, j: (i, 0, 0))
    o, c_f, n_f, m_f = pl.pallas_call(
        functools.partial(_ml_fused_body, nc=nc),
        grid=(b, nc),
        in_specs=[pl.BlockSpec((1, L, P_MAIN), lambda i, j: (i, j, P_ML // P_MAIN)),
                  pl.BlockSpec((1, L, LANES), lambda i, j: (i, j, P_GATE // LANES)),
                  pl.BlockSpec((1, LANES), lambda i, j: (0, 0)),
                  pl.BlockSpec((1, ML_W), lambda i, j: (0, 0))],
        out_specs=[vs, cs, ns, ms],
        out_shape=[jax.ShapeDtypeStruct((b, t, ML_W), BF16),
                   jax.ShapeDtypeStruct((b, ML_HEADS, ML_QK, ML_V), F32),
                   jax.ShapeDtypeStruct((b, ML_HPAD, ML_QK), F32),
                   jax.ShapeDtypeStruct((b, ML_HPAD, LANES), F32)],
        scratch_shapes=[pltpu.VMEM((ML_HEADS, ML_QK, ML_V), F32),
                        pltpu.VMEM((ML_HPAD, ML_QK), F32),
                        pltpu.VMEM((ML_HPAD, LANES), F32)],
        compiler_params=_cparams(("parallel", "arbitrary")),
        name="ml_fused",
    )(p3, p3, bias, lp['ml_norm'].reshape(1, ML_W))
    return o, c_f, n_f[:, :ML_HEADS], m_f[:, :ML_HEADS, 0]


def _to_cols(x):
    b, c = x.shape
    return x.reshape(b // DEC_TB, DEC_TB, c).transpose(0, 2, 1)


def _rwkv_step_body(w_ref, ap_ref, k_ref, kk_ref, r_ref, v_ref, s_ref, y_ref, so_ref):
    N = RW_HD
    v = v_ref[0]

    def sa_step(i, acc):
        return acc + kk_ref[0, pl.ds(i, 1), :] * s_ref[i]

    sa = lax.fori_loop(0, N, sa_step, jnp.zeros_like(v), unroll=8)

    def upd_step(i, y):
        row = lambda ref: ref[0, pl.ds(i, 1), :]
        s_new = row(w_ref) * s_ref[i] - row(ap_ref) * sa + row(k_ref) * v
        so_ref[i] = s_new
        return y + row(r_ref) * s_new

    y_ref[0] = lax.fori_loop(0, N, upd_step, jnp.zeros_like(v), unroll=8)


def _rwkv_step(wdec, ap, k, kk, r, v, s_all, layer):
    b, w = v.shape
    heads = lambda x: x.reshape(b, RW_HEADS, RW_HD).transpose(1, 2, 0)
    vec = pl.BlockSpec((1, RW_HD, b), lambda h: (h, 0, 0))
    y, s_new = pl.pallas_call(
        _rwkv_step_body,
        grid=(RW_HEADS,),
        in_specs=[vec] * 6 + [pl.BlockSpec((None, None, RW_HD, RW_HD, b), lambda h: (layer, h, 0, 0, 0))],
        out_specs=[vec, pl.BlockSpec((None, RW_HD, RW_HD, b), lambda h: (h, 0, 0, 0))],
        out_shape=[jax.ShapeDtypeStruct((RW_HEADS, RW_HD, b), F32),
                   jax.ShapeDtypeStruct((RW_HEADS, RW_HD, RW_HD, b), F32)],
        compiler_params=_cparams(("parallel",)),
        name="rwkv_step",
    )(heads(wdec), heads(ap), heads(k), heads(kk), heads(r), heads(v), s_all)
    return y.transpose(2, 0, 1).reshape(b, w), s_new


def _ret_step_body(qc_ref, kc_ref, v_ref, s_ref, y_ref, so_ref):
    DK, DV = RET_QK, RET_V
    for j in range(DEC_TB):
        for h in range(RET_HEADS):
            gamma = math.exp(_ret_log_gamma(h))
            q = qc_ref[0, h * DK:(h + 1) * DK, j:j + 1]
            k = kc_ref[0, h * DK:(h + 1) * DK, j:j + 1]
            v = v_ref[j:j + 1, h * DV:(h + 1) * DV]
            s = s_ref[j, h]
            qk = jnp.sum(q * k, axis=0, keepdims=True)
            y_ref[j:j + 1, h * DV:(h + 1) * DV] = qk * v + gamma * jnp.sum(q * s, axis=0, keepdims=True)
            so_ref[j, h] = gamma * s + k * v


def _ret_step(q, k, v, s_all, layer):
    b = q.shape[0]
    cols = pl.BlockSpec((1, RET_HEADS * RET_QK, DEC_TB), lambda i: (i, 0, 0))
    rows = pl.BlockSpec((DEC_TB, RET_W), lambda i: (i, 0))
    st = pl.BlockSpec((DEC_TB, RET_HEADS, RET_QK, RET_V), lambda i: (i, 0, 0, 0))
    st_in = pl.BlockSpec((None, DEC_TB, RET_HEADS, RET_QK, RET_V), lambda i: (layer, i, 0, 0, 0))
    return pl.pallas_call(
        _ret_step_body,
        grid=(b // DEC_TB,),
        in_specs=[cols, cols, rows, st_in],
        out_specs=[rows, st],
        out_shape=[jax.ShapeDtypeStruct((b, RET_W), F32), jax.ShapeDtypeStruct(s_all.shape[1:], F32)],
        compiler_params=_cparams(("parallel",)),
        name="ret_step",
    )(_to_cols(q), _to_cols(k), v, s_all)


def _ml_step_body(qc_ref, kc_ref, q_ref, k_ref, v_ref, ig_ref, lf_ref, c_ref, n_ref, m_ref,
                  h_ref, co_ref, no_ref, mo_ref):
    DK, DV = ML_QK, ML_V
    for j in range(DEC_TB):
        for h in range(ML_HEADS):
            ks = slice(h * DK, (h + 1) * DK)
            vs = slice(h * DV, (h + 1) * DV)
            q_col = qc_ref[0, ks, j:j + 1]
            k_col = kc_ref[0, ks, j:j + 1]
            q_row = q_ref[j:j + 1, ks]
            k_row = k_ref[j:j + 1, ks]
            v = v_ref[j:j + 1, vs]
            ig = ig_ref[j:j + 1, h:h + 1]
            lf = lf_ref[j:j + 1, h:h + 1]
            m_prev = m_ref[j:j + 1, h:h + 1]
            c_prev = c_ref[j, h]
            n_prev = n_ref[j, h:h + 1, :]
            m_new = jnp.maximum(lf + m_prev, ig)
            dec = jnp.exp(lf + m_prev - m_new)
            wgt = jnp.exp(ig - m_new)
            co_ref[j, h] = dec * c_prev + (k_col * wgt) * v
            no_ref[j, h:h + 1, :] = dec * n_prev + k_row * wgt
            mo_ref[j:j + 1, h:h + 1] = m_new
            s = jnp.sum(q_row * k_row, axis=1, keepdims=True) * wgt
            num = s * v + dec * jnp.sum(q_col * c_prev, axis=0, keepdims=True)
            den = s + dec * jnp.sum(q_row * n_prev, axis=1, keepdims=True)
            h_ref[j:j + 1, vs] = num / jnp.maximum(jnp.abs(den), jnp.exp(-m_new))


def _ml_step(q, k, v, ig, lf, c_all, layer, n0, m0):
    b = q.shape[0]
    cols = pl.BlockSpec((1, ML_HEADS * ML_QK, DEC_TB), lambda i: (i, 0, 0))
    qk_rows = pl.BlockSpec((DEC_TB, ML_HEADS * ML_QK), lambda i: (i, 0))
    rows = pl.BlockSpec((DEC_TB, ML_W), lambda i: (i, 0))
    sc = pl.BlockSpec((DEC_TB, ML_HEADS), lambda i: (i, 0))
    cs = pl.BlockSpec((DEC_TB, ML_HEADS, ML_QK, ML_V), lambda i: (i, 0, 0, 0))
    ns = pl.BlockSpec((DEC_TB, ML_HEADS, ML_QK), lambda i: (i, 0, 0))
    cs_in = pl.BlockSpec((None, DEC_TB, ML_HEADS, ML_QK, ML_V), lambda i: (layer, i, 0, 0, 0))
    return pl.pallas_call(
        _ml_step_body,
        grid=(b // DEC_TB,),
        in_specs=[cols, cols, qk_rows, qk_rows, rows, sc, sc, cs_in, ns, sc],
        out_specs=[rows, cs, ns, sc],
        out_shape=[jax.ShapeDtypeStruct((b, ML_W), F32), jax.ShapeDtypeStruct(c_all.shape[1:], F32),
                   jax.ShapeDtypeStruct(n0.shape, F32), jax.ShapeDtypeStruct(m0.shape, F32)],
        compiler_params=_cparams(("parallel",)),
        name="ml_step",
    )(_to_cols(q), _to_cols(k), q, k, v, ig, lf, c_all, n0, m0)


def _heads(a, h):
    return a.reshape(a.shape[:-1] + (h, a.shape[-1] // h))


def _shift_prev(p, prev_row):
    return jnp.concatenate([prev_row[:, None, :], p[:, :-1]], axis=1)


def _rotary(x, pos):
    half = x.shape[-1] // 2
    inv = ROPE_BASE ** (-jnp.arange(half, dtype=F32) / half)
    ang = pos.astype(F32)[:, None] * inv[None, :]
    cos = jnp.cos(ang)[None, :, None, :]
    sin = jnp.sin(ang)[None, :, None, :]
    x1, x2 = x[..., :half], x[..., half:]
    return jnp.concatenate([x1 * cos - x2 * sin, x1 * sin + x2 * cos], -1)


def _small_matmul(x, w):
    lead = x.shape[:-1]
    kdim, n = w.shape
    x2 = x.reshape(-1, kdim)
    m = x2.shape[0]
    kp = -(-kdim // LANES) * LANES
    npad = -(-n // LANES) * LANES
    x2 = jnp.pad(x2.astype(BF16), ((0, 0), (0, kp - kdim)))
    w2 = jnp.pad(w.astype(BF16), ((0, kp - kdim), (0, npad - n)))
    tm = 1024 if m % 1024 == 0 else m
    out = _matmul(x2, w2, tm, npad)
    return out[:, :n].reshape(lead + (n,))


def _mix_prompt(p, pos, v_first, lp):
    o_rw, v_first, s_new = _rwkv_prompt(p, lp, v_first, (1, 1, 1, 1))
    o_ret, r_new = _ret_prompt(p, pos)
    o_ml, c_new, n_new, m_new = _ml_prompt(p, lp)
    return (o_rw, o_ret, o_ml), v_first, (s_new, r_new, c_new, n_new, m_new)


def _mix_sample(p, pos, v_first, st, lp, prev_row):
    bsz, t_len, _ = p.shape

    p_rw = p[..., :RW_P]
    mixed = p_rw + (_shift_prev(p_rw, prev_row[:, :RW_P]) - p_rw) * lp['rw_mu']
    sizes = np.cumsum([RW_W, RW_W, RW_W, RW_LORA_W, RW_LORA_A, RW_LORA_G])[:-1]
    r, k, v, xw, xa, xg = jnp.split(mixed, [int(s) for s in sizes], axis=-1)
    w = -jax.nn.softplus(-(lp['rw_w0'] + _small_matmul(jnp.tanh(xw), lp['rw_w2']))) - 0.5
    a = jax.nn.sigmoid(lp['rw_a0'] + _small_matmul(xa, lp['rw_a2']))
    g = _small_matmul(jax.nn.sigmoid(xg), lp['rw_g2'])
    if v_first is None:
        v_first = v
    else:
        pv = p[..., P_V1:P_V1 + RW_LORA_V]
        xv = pv + (_shift_prev(pv, prev_row[:, P_V1:P_V1 + RW_LORA_V]) - pv) * lp['rw_vmu']
        v = v + (v_first - v) * jax.nn.sigmoid(lp['rw_v0'] + _small_matmul(xv, lp['rw_v2']))
    kk = _heads(k * lp['rw_kk'], RW_HEADS)
    kk = kk * lax.rsqrt(jnp.maximum(jnp.sum(jnp.square(kk), -1, keepdims=True), 1e-24))
    kk = kk.reshape(bsz, t_len, RW_W)
    k = k * (1.0 + (a - 1.0) * lp['rw_ka'])
    lw = -jnp.exp(w)
    y, s_new = _rwkv_step(jnp.exp(lw)[:, 0], (kk * a)[:, 0], k[:, 0], kk[:, 0], r[:, 0], v[:, 0],
                          st['rw_wkv_t'], st['layer'])
    y = _heads(y[:, None, :], RW_HEADS)
    y_mu = jnp.mean(y, -1, keepdims=True)
    y_var = jnp.mean(jnp.square(y - y_mu), -1, keepdims=True)
    y = ((y - y_mu) * lax.rsqrt(y_var + RW_GN_EPS)).reshape(bsz, t_len, RW_W)
    y = y * lp['rw_lnx_g'] + lp['rw_lnx_b']
    rh, kh, vh = (_heads(u, RW_HEADS) for u in (r, k, v))
    bonus = jnp.sum(rh * kh * _heads(lp['rw_rk'], RW_HEADS), -1, keepdims=True) * vh
    o_rw = ((y + bonus.reshape(bsz, t_len, RW_W)) * g).astype(BF16)

    nqk = RET_HEADS * RET_QK
    p_ret = p[..., P_RET:P_RET + P_MAIN]
    qr, kr, vr, gr = (p_ret[..., :nqk], p_ret[..., nqk:2 * nqk],
                      p_ret[..., 2 * nqk:2 * nqk + RET_W], p_ret[..., 2 * nqk + RET_W:])
    qh = _rotary(_heads(qr, RET_HEADS), pos).reshape(bsz, t_len, nqk)
    khr = (_rotary(_heads(kr, RET_HEADS), pos) * (RET_QK ** -0.5)).reshape(bsz, t_len, nqk)
    yr, r_new = _ret_step(qh[:, 0], khr[:, 0], vr[:, 0], st['ret_all'], st['layer'])
    yr = _heads(yr[:, None, :], RET_HEADS)
    yr = yr * lax.rsqrt(jnp.mean(jnp.square(yr), -1, keepdims=True) + NORM_EPS)
    o_ret = (jax.nn.silu(gr) * yr.reshape(bsz, t_len, RET_W)).astype(BF16)

    nqk = ML_HEADS * ML_QK
    p_ml = p[..., P_ML:P_ML + P_MAIN]
    qm, km, vm, om = (p_ml[..., :nqk], p_ml[..., nqk:2 * nqk],
                      p_ml[..., 2 * nqk:2 * nqk + ML_W], p_ml[..., 2 * nqk + ML_W:])
    im = p[..., P_GATE:P_GATE + ML_HEADS]
    fm = p[..., P_GATE + ML_HEADS:P_GATE + 2 * ML_HEADS]
    ig = ML_GATE_CAP * jnp.tanh((im + lp['ml_ib']) / ML_GATE_CAP)
    lf = jax.nn.log_sigmoid(ML_GATE_CAP * jnp.tanh((fm + lp['ml_fb']) / ML_GATE_CAP))
    km = km * (ML_QK ** -0.5)
    hm, c_new, n_new, m_new = _ml_step(qm[:, 0], km[:, 0], vm[:, 0], ig[:, 0], lf[:, 0],
                                       st['ml_c_all'], st['layer'], st['ml_n'], st['ml_m'])
    hm = _heads(hm[:, None, :], ML_HEADS)
    hm = hm * lax.rsqrt(jnp.mean(jnp.square(hm), -1, keepdims=True) + NORM_EPS)
    o_ml = (jax.nn.sigmoid(om) * (hm.reshape(bsz, t_len, ML_W) * lp['ml_norm'])).astype(BF16)

    return (o_rw, o_ret, o_ml), v_first, (s_new, r_new, c_new, n_new, m_new)


_T_V1 = P_V1 // LANES
_T_GATE = P_GATE // LANES
_T_RET = P_RET // LANES
_T_SHIFT = (P_RET - RW_P) // LANES
_T_SRC_GATE = (RW_P + RET_P + 2 * ML_HEADS * ML_QK + 2 * ML_W) // LANES


def _pack_body(w_ref, v1_ref, *o_refs):
    j = pl.program_id(0)
    row = lax.broadcasted_iota(jnp.int32, (LANES, 1), 0)
    spare = jnp.logical_and(j > _T_GATE, j < _T_RET)
    for l, o_ref in enumerate(o_refs):
        w = w_ref[:, l, :]
        gates = jnp.where(row < 2 * ML_HEADS, w, 0.0)
        out = jnp.where(j == _T_V1, v1_ref[l], jnp.where(j == _T_GATE, gates, jnp.where(spare, 0.0, w)))
        o_ref[...] = out.astype(BF16)


def _pack_w_in(w_in, rw_v1):
    depth, d, _ = w_in.shape
    w_t = w_in.transpose(2, 0, 1)
    v1_t = jnp.pad(rw_v1.transpose(0, 2, 1), ((1, 0), (0, LANES - RW_LORA_V), (0, 0)))

    def src_tile(j):
        return jnp.where(j < _T_V1, j, jnp.where(j == _T_GATE, _T_SRC_GATE, j - _T_SHIFT))

    return pl.pallas_call(
        _pack_body,
        grid=(P_PAD // LANES,),
        in_specs=[pl.BlockSpec((LANES, depth, d), lambda j: (src_tile(j), 0, 0)),
                  pl.BlockSpec((depth, LANES, d), lambda j: (0, 0, 0))],
        out_specs=[pl.BlockSpec((LANES, d), lambda j: (j, 0))] * depth,
        out_shape=[jax.ShapeDtypeStruct((P_PAD, d), BF16)] * depth,
        compiler_params=_cparams(("parallel",)),
        name="pack_w_in",
    )(w_t, v1_t)


def _token_tiles(m):
    if m % 2048 == 0:
        return 2048, 512, 512
    return m, m, m


def kernel(x_prompt, x_sample, state_rw_shift, state_rw_wkv, state_ret, state_ml_c, state_ml_n, state_ml_m,
           ln0_g, ln0_b, w_in, rw_mu, rw_w0, rw_w2, rw_a0, rw_a2, rw_g2, rw_kk, rw_ka, rw_rk,
           rw_lnx_g, rw_lnx_b, rw_v0, rw_v1, rw_vmu, rw_v2, ml_ib, ml_fb, ml_norm, w_out,
           ln1_g, ln1_b, w_gate, w_up, w_down, ln2_g, ln2_b):
    bp, tp, d = x_prompt.shape
    bs, ts, _ = x_sample.shape
    groups = {
        'p': dict(b=bp, t=tp, pos=jnp.arange(tp), chunked=True),
        's': dict(b=bs, t=ts, pos=PAST_LEN + jnp.arange(ts), chunked=False),
    }
    rw_wkv_t = state_rw_wkv.transpose(0, 2, 3, 4, 1)
    w_in_packed = _pack_w_in(w_in, rw_v1)
    xs_f, xs_b, v_first, outs = {}, {}, {}, {}
    for name, x in (('p', x_prompt), ('s', x_sample)):
        m = x.shape[0] * x.shape[1]
        xs_f[name], xs_b[name] = _layernorm(x.reshape(m, d), ln0_g, ln0_b, _token_tiles(m)[1])
        v_first[name] = None
        outs[name] = []

    for l in range(DEPTH):
        lp = {
            'rw_mu': rw_mu[l], 'rw_w0': rw_w0[l], 'rw_w2': rw_w2[l], 'rw_a0': rw_a0[l], 'rw_a2': rw_a2[l],
            'rw_g2': rw_g2[l], 'rw_kk': rw_kk[l], 'rw_ka': rw_ka[l], 'rw_rk': rw_rk[l],
            'rw_lnx_g': rw_lnx_g[l], 'rw_lnx_b': rw_lnx_b[l], 'ml_ib': ml_ib[l], 'ml_fb': ml_fb[l],
            'ml_norm': ml_norm[l],
        }
        if l > 0:
            lp.update(rw_v0=rw_v0[l - 1], rw_vmu=rw_vmu[l - 1], rw_v2=rw_v2[l - 1])
        w_in_b = w_in_packed[l]
        w_out_b = w_out[l].astype(BF16)
        w_down_b = w_down[l].astype(BF16)
        prev_s = _matmul(state_rw_shift[l].astype(BF16), w_in_b, bs, P_TN, w_transposed=True)
        for name, grp in groups.items():
            b, t = grp['b'], grp['t']
            m = b * t
            tm_big, tm_out, tm_down = _token_tiles(m)
            p = _matmul(xs_b[name], w_in_b, tm_big, P_TN, w_transposed=True).reshape(b, t, P_PAD)
            if grp['chunked']:
                o, v_first[name], new_st = _mix_prompt(p, grp['pos'], v_first[name], lp)
            else:
                st = {'rw_wkv_t': rw_wkv_t, 'layer': l, 'ret_all': state_ret, 'ml_c_all': state_ml_c,
                      'ml_n': state_ml_n[l], 'ml_m': state_ml_m[l]}
                o, v_first[name], new_st = _mix_sample(p, grp['pos'], v_first[name], st, lp, prev_s)
            outs[name].append((xs_f[name].reshape(b, t, d)[:, -1],) + new_st)
            o_rw, o_ret, o_ml = (u.reshape(m, u.shape[-1]) for u in o)
            x1_f, x1_b = _out_proj_ln(o_rw, o_ret, o_ml, w_out_b, xs_f[name], ln1_g[l], ln1_b[l], tm_out)
            hdn = _matmul_swiglu(x1_b, w_gate, w_up, l, tm_big, 512)
            xs_f[name], xs_b[name] = _matmul_res_ln(hdn, w_down_b, x1_f, ln2_g[l], ln2_b[l], tm_down, DOWN_TK)

    y_p = xs_f['p'].reshape(bp, tp, d)
    y_s = xs_f['s'].reshape(bs, ts, d)
    sp = [jnp.stack([o[i] for o in outs['p']]) for i in range(6)]
    ss = [jnp.stack([o[i] for o in outs['s']]) for i in range(6)]
    ss[1] = ss[1].transpose(0, 4, 1, 2, 3)
    return (y_p, y_s, sp[0], sp[1], sp[2], sp[3], sp[4], sp[5], ss[0], ss[1], ss[2], ss[3], ss[4], ss[5])
```

```python
import functools
import math

import numpy as np
import jax
import jax.numpy as jnp
from jax import lax
from jax.experimental import pallas as pl
from jax.experimental.pallas import tpu as pltpu

F32 = jnp.float32
BF16 = jnp.bfloat16

D_MODEL = 2048
DEPTH = 2
PAST_LEN = 16384
RW_HD = 64
RW_W = D_MODEL // 4
RW_HEADS = RW_W // RW_HD
RW_LORA_W = 64
RW_LORA_A = 64
RW_LORA_V = 32
RW_LORA_G = 128
RW_P = 3 * RW_W + RW_LORA_W + RW_LORA_A + RW_LORA_G
RW_GN_EPS = 64e-5
RET_V = 128
RET_QK = 64
RET_W = 3 * D_MODEL // 8
RET_HEADS = RET_W // RET_V
RET_P = 2 * RET_HEADS * RET_QK + 2 * RET_W
ML_V = 128
ML_QK = 64
ML_W = D_MODEL - RW_W - RET_W
ML_HEADS = ML_W // ML_V
ML_P = 2 * ML_HEADS * ML_QK + 2 * ML_W + 2 * ML_HEADS
ML_GATE_CAP = 15.0
P_TOTAL = RW_P + RET_P + ML_P
D_FF = ((8 * D_MODEL + 3 * 256 - 1) // (3 * 256)) * 256
CHUNK = 128
ROPE_BASE = 10000.0
LN_EPS = 1e-5
NORM_EPS = 1e-6
ALPHA = (2 * DEPTH) ** 0.25

LANES = 128
P_V1 = RW_P
P_GATE = RW_P + LANES
P_RET = 2304
P_ML = 2 * P_RET
P_MAIN = 2304
P_PAD = 3 * P_RET
P_TN = 768
RW_CHUNK = 64
DEC_TB = 8
VMEM_LIMIT = 56 * 1024 * 1024

HIGHEST = lax.Precision.HIGHEST


def _cparams(sem):
    return pltpu.CompilerParams(dimension_semantics=sem, vmem_limit_bytes=VMEM_LIMIT)


def _dot(a, b, precision=None):
    return lax.dot_general(a, b, (((1,), (0,)), ((), ())), precision=precision, preferred_element_type=F32)


def _dot_nt(a, b, precision=None):
    return lax.dot_general(a, b, (((1,), (1,)), ((), ())), precision=precision, preferred_element_type=F32)


def _dot_tn(a, b, precision=None):
    return lax.dot_general(a, b, (((0,), (0,)), ((), ())), precision=precision, preferred_element_type=F32)


def _ln_rows(x, g, b):
    mu = jnp.mean(x, -1, keepdims=True)
    xc = x - mu
    var = jnp.mean(xc * xc, -1, keepdims=True)
    return xc * lax.rsqrt(var + LN_EPS) * g + b


def _ln_body(x_ref, g_ref, b_ref, of_ref, ob_ref):
    y = _ln_rows(x_ref[...], g_ref[...], b_ref[...])
    of_ref[...] = y
    ob_ref[...] = y.astype(BF16)


def _layernorm(x, g, b, tm):
    m, d = x.shape
    return pl.pallas_call(
        _ln_body,
        grid=(m // tm,),
        in_specs=[pl.BlockSpec((tm, d), lambda i: (i, 0)),
                  pl.BlockSpec((1, d), lambda i: (0, 0)),
                  pl.BlockSpec((1, d), lambda i: (0, 0))],
        out_specs=[pl.BlockSpec((tm, d), lambda i: (i, 0)),
                   pl.BlockSpec((tm, d), lambda i: (i, 0))],
        out_shape=[jax.ShapeDtypeStruct((m, d), F32), jax.ShapeDtypeStruct((m, d), BF16)],
        compiler_params=_cparams(("parallel",)),
        name="layernorm",
    )(x, g.reshape(1, d), b.reshape(1, d))


def _mm_body(x_ref, w_ref, o_ref, *, w_transposed):
    dot = _dot_nt if w_transposed else _dot
    o_ref[...] = dot(x_ref[...], w_ref[...]).astype(o_ref.dtype)


def _matmul(x, w, tm, tn, out_dtype=F32, w_transposed=False):
    m, k = x.shape
    n = w.shape[0] if w_transposed else w.shape[1]
    w_spec = (pl.BlockSpec((tn, k), lambda i, j: (j, 0)) if w_transposed
              else pl.BlockSpec((k, tn), lambda i, j: (0, j)))
    return pl.pallas_call(
        functools.partial(_mm_body, w_transposed=w_transposed),
        grid=(m // tm, n // tn),
        in_specs=[pl.BlockSpec((tm, k), lambda i, j: (i, 0)), w_spec],
        out_specs=pl.BlockSpec((tm, tn), lambda i, j: (i, j)),
        out_shape=jax.ShapeDtypeStruct((m, n), out_dtype),
        compiler_params=_cparams(("parallel", "parallel")),
        name="matmul",
    )(x, w)


def _swiglu_body(x_ref, wg_ref, wu_ref, o_ref, wg_scr, wu_scr):
    @pl.when(pl.program_id(1) == 0)
    def _():
        wg_scr[...] = wg_ref[...].astype(BF16)
        wu_scr[...] = wu_ref[...].astype(BF16)

    x = x_ref[...]
    g = _dot(x, wg_scr[...])
    u = _dot(x, wu_scr[...])
    o_ref[...] = (g * jax.nn.sigmoid(g) * u).astype(o_ref.dtype)


def _matmul_swiglu(x, wg, wu, layer, tm, tn):
    m, k = x.shape
    n = wg.shape[2]
    w_spec = pl.BlockSpec((None, k, tn), lambda j, i: (layer, 0, j))
    return pl.pallas_call(
        _swiglu_body,
        grid=(n // tn, m // tm),
        in_specs=[pl.BlockSpec((tm, k), lambda j, i: (i, 0)), w_spec, w_spec],
        out_specs=pl.BlockSpec((tm, tn), lambda j, i: (i, j)),
        out_shape=jax.ShapeDtypeStruct((m, n), BF16),
        scratch_shapes=[pltpu.VMEM((k, tn), BF16), pltpu.VMEM((k, tn), BF16)],
        compiler_params=_cparams(("parallel", "arbitrary")),
        name="matmul_swiglu",
    )(x, wg, wu)


DOWN_TK = D_FF // 4
LN_ROWS = 256
OUT_ROWS = 256


def _mm_res_ln_body(x_ref, w_ref, res_ref, g_ref, b_ref, of_ref, ob_ref, *, nk, tm):
    kk = pl.program_id(1)
    part = _dot(x_ref[...], w_ref[...])

    @pl.when(kk == 0)
    def _():
        of_ref[...] = part

    @pl.when(kk > 0)
    def _():
        of_ref[...] += part

    @pl.when(kk == nk - 1)
    def _():
        for r in range(0, tm, min(LN_ROWS, tm)):
            rows = pl.ds(r, min(LN_ROWS, tm))
            y = _ln_rows(ALPHA * res_ref[rows, :] + of_ref[rows, :], g_ref[...], b_ref[...])
            of_ref[rows, :] = y
            ob_ref[rows, :] = y.astype(BF16)


def _matmul_res_ln(x, w, res, g, b, tm, tk):
    m, k = x.shape
    n = w.shape[1]
    nk = k // tk
    return pl.pallas_call(
        functools.partial(_mm_res_ln_body, nk=nk, tm=tm),
        grid=(m // tm, nk),
        in_specs=[pl.BlockSpec((tm, tk), lambda i, j: (i, j)),
                  pl.BlockSpec((tk, n), lambda i, j: (j, 0)),
                  pl.BlockSpec((tm, n), lambda i, j: (i, 0)),
                  pl.BlockSpec((1, n), lambda i, j: (0, 0)),
                  pl.BlockSpec((1, n), lambda i, j: (0, 0))],
        out_specs=[pl.BlockSpec((tm, n), lambda i, j: (i, 0)),
                   pl.BlockSpec((tm, n), lambda i, j: (i, 0))],
        out_shape=[jax.ShapeDtypeStruct((m, n), F32), jax.ShapeDtypeStruct((m, n), BF16)],
        compiler_params=_cparams(("parallel", "arbitrary")),
        name="matmul_res_ln",
    )(x, w, res, g.reshape(1, n), b.reshape(1, n))


def _out_proj_ln_body(o_rw_ref, o_ret_ref, o_ml_ref, w_ref, res_ref, g_ref, b_ref, of_ref, ob_ref, *, tm):
    for r in range(0, tm, min(OUT_ROWS, tm)):
        rows = pl.ds(r, min(OUT_ROWS, tm))
        mix = (_dot(o_rw_ref[rows, :], w_ref[0:RW_W, :])
               + _dot(o_ret_ref[rows, :], w_ref[RW_W:RW_W + RET_W, :])
               + _dot(o_ml_ref[rows, :], w_ref[RW_W + RET_W:, :]))
        y = _ln_rows(ALPHA * res_ref[rows, :] + mix, g_ref[...], b_ref[...])
        of_ref[rows, :] = y
        ob_ref[rows, :] = y.astype(BF16)


def _out_proj_ln(o_rw, o_ret, o_ml, w, res, g, b, tm):
    m = o_rw.shape[0]
    n = w.shape[1]
    rows = lambda width: pl.BlockSpec((tm, width), lambda i: (i, 0))
    full = lambda r, c: pl.BlockSpec((r, c), lambda i: (0, 0))
    return pl.pallas_call(
        functools.partial(_out_proj_ln_body, tm=tm),
        grid=(m // tm,),
        in_specs=[rows(RW_W), rows(RET_W), rows(ML_W), full(D_MODEL, n), rows(n), full(1, n), full(1, n)],
        out_specs=[rows(n), rows(n)],
        out_shape=[jax.ShapeDtypeStruct((m, n), F32), jax.ShapeDtypeStruct((m, n), BF16)],
        compiler_params=_cparams(("parallel",)),
        name="out_proj_ln",
    )(o_rw, o_ret, o_ml, w, res, g.reshape(1, n), b.reshape(1, n))


RW_TB = 256
RW_GH = 4
RW_GW = RW_GH * RW_HD
RW_VEC_ROWS = 8


def _split3(x):
    hi = x.astype(BF16)
    r1 = x - hi.astype(F32)
    mid = r1.astype(BF16)
    lo = (r1 - mid.astype(F32)).astype(BF16)
    return hi, mid, lo


def _mm(a, b, dims, passes):
    dg = lambda x, y: lax.dot_general(x, y, (dims, ((), ())), preferred_element_type=F32)
    if passes == 6:
        return lax.dot_general(a, b, (dims, ((), ())), precision=HIGHEST, preferred_element_type=F32)
    ah = a.astype(BF16)
    bh = b.astype(BF16)
    if passes == 1:
        return dg(ah, bh)
    al = (a - ah.astype(F32)).astype(BF16)
    bl = (b - bh.astype(F32)).astype(BF16)
    return dg(ah, bh) + (dg(ah, bl) + dg(al, bh))


_NN = ((1,), (0,))
_NT = ((1,), (1,))
_TN = ((0,), (0,))


def _exact_lhs_dot(a_bf16, b):
    hi, mid, lo = _split3(b)
    dg = lambda y: lax.dot_general(a_bf16, y, (_NN, ((), ())), preferred_element_type=F32)
    return dg(hi) + (dg(mid) + dg(lo))


def _exact_rhs_dot(a, b_bf16):
    hi, mid, lo = _split3(a)
    dg = lambda x: lax.dot_general(x, b_bf16, (_NN, ((), ())), preferred_element_type=F32)
    return dg(hi) + (dg(mid) + dg(lo))


def _rw_scan(r, lw, k, v, kk, a, st_scr, y_scr, passes):
    L, TB, G = RW_CHUNK, RW_TB, RW_GW
    p_sc, p_inv, p_app, p_st = passes
    row = lax.broadcasted_iota(jnp.int32, (L, G), 0)
    col = lax.broadcasted_iota(jnp.int32, (L, G), 1) & (L - 1)
    strict, lower, eye = row > col, row >= col, (row == col).astype(F32)
    rg = lax.broadcasted_iota(jnp.int32, (G, G), 0) // RW_HD
    cg = lax.broadcasted_iota(jnp.int32, (G, G), 1) // RW_HD
    mask_bd = rg == cg
    rt = lax.broadcasted_iota(jnp.int32, (TB, TB), 0)
    ct = lax.broadcasted_iota(jnp.int32, (TB, TB), 1)
    tri = jnp.logical_and(rt >= ct, rt // L == ct // L).astype(BF16)
    bd = lambda x: jnp.where(mask_bd, jnp.concatenate([x] * RW_GH, axis=0), 0.0)
    blocks = [(s, g) for s in range(TB // L) for g in range(RW_HEADS // RW_GH)]
    cut = lambda x, b: x[b[0] * L:(b[0] + 1) * L, b[1] * G:(b[1] + 1) * G]

    cum = _exact_lhs_dot(tri, lw)
    e_neg = jnp.exp(-cum)
    ap = kk * a
    ap_h = ap * e_neg
    k_h = k * e_neg
    kk_t = kk * jnp.exp(cum - lw)
    r_t = r * jnp.exp(cum)

    lhs, n_m, m_a, m_kr = {}, {}, {}, {}
    for b in blocks:
        lhs[b] = jnp.concatenate([cut(kk_t, b), cut(r_t, b)], axis=0)
        sc_a = _mm(lhs[b], bd(cut(ap_h, b)), _NT, p_sc)
        sc_k = _mm(lhs[b], bd(cut(k_h, b)), _NT, p_sc)
        n_m[b] = jnp.where(strict, sc_a[:L], 0.0)
        m_a[b] = jnp.where(lower, sc_a[L:], 0.0)
        m_kr[b] = jnp.concatenate([jnp.where(strict, sc_k[:L], 0.0), jnp.where(lower, sc_k[L:], 0.0)], axis=0)
    inv = {b: eye - n_m[b] for b in blocks}
    pw = dict(n_m)
    for _ in range(int(math.log2(L)) - 1):
        for b in blocks:
            pw[b] = _mm(pw[b], bd(pw[b]), _NN, p_inv)
        for b in blocks:
            inv[b] = inv[b] + _mm(inv[b], bd(pw[b]), _NN, p_inv)
    mv, vk, a_end, decay = {}, {}, {}, {}
    for b in blocks:
        s, g = b
        tot = cum[(s + 1) * L - 1:(s + 1) * L, g * G:(g + 1) * G]
        e_end = jnp.exp(tot - cut(cum, b))
        mv[b] = _mm(m_kr[b], bd(cut(v, b)), _NN, p_app)
        vk[b] = _mm(cut(v, b), cut(k, b) * e_end, _TN, p_st)
        a_end[b] = cut(ap, b) * e_end
        decay[b] = jnp.exp(tot)

    for b in blocks:
        s, g = b
        st = st_scr[g]
        s_terms = _mm(lhs[b], st, _NT, p_app)
        u = _mm(inv[b], bd(s_terms[:L] + mv[b][:L]), _NN, p_app)
        y_scr[s * L:(s + 1) * L, g * G:(g + 1) * G] = s_terms[L:] + mv[b][L:] - _mm(m_a[b], bd(u), _NN, p_app)
        st_scr[g] = jnp.where(mask_bd, st * decay[b] + vk[b] - _mm(u, a_end[b], _TN, p_st), 0.0)


def _softplus(z):
    return jnp.maximum(z, 0.0) + jnp.log(1.0 + jnp.exp(-jnp.abs(z)))


def _rwkv_fused_body(*refs, nc, has_vres, passes):
    if has_vres:
        (p_ref, pv_ref, vf_ref, mu_ref, vec_ref, wa_ref, g2_ref, seg_ref, vmu_ref, v2_ref,
         o_ref, sf_ref, st_scr, prev_scr, y_scr, prevv_scr) = refs
    else:
        (p_ref, mu_ref, vec_ref, wa_ref, g2_ref, seg_ref,
         o_ref, vfo_ref, sf_ref, st_scr, prev_scr, y_scr) = refs
    TB, W, L = RW_TB, RW_W, RW_CHUNK
    c = pl.program_id(1)

    @pl.when(c == 0)
    def _():
        st_scr[...] = jnp.zeros_like(st_scr)
        prev_scr[...] = jnp.zeros_like(prev_scr)
        if has_vres:
            prevv_scr[...] = jnp.zeros_like(prevv_scr)

    first_row = lax.broadcasted_iota(jnp.int32, (TB, 1), 0) == 0

    def shift_mix(x, carry_ref, mu):
        prev = jnp.where(first_row, carry_ref[...], pltpu.roll(x, 1, 0))
        carry_ref[...] = x[TB - 1:TB, :]
        return x + (prev - x) * mu

    mixed = shift_mix(p_ref[0], prev_scr, mu_ref[...])
    r = mixed[:, 0:W]
    k = mixed[:, W:2 * W]
    v = mixed[:, 2 * W:3 * W]
    xwa = mixed[:, 3 * W:3 * W + LANES]
    xg = mixed[:, 3 * W + LANES:3 * W + 2 * LANES]
    vec = vec_ref[...]
    w0, a0, kk_s, ka, rk, lnx_g, lnx_b, v0 = (vec[i:i + 1, :] for i in range(RW_VEC_ROWS))
    seg = seg_ref[...]
    wa = wa_ref[...]
    w_lora = _dot(jnp.tanh(xwa).astype(BF16), wa[:, 0:W])
    a_lora = _dot(xwa.astype(BF16), wa[:, W:2 * W])
    lw = -jnp.exp(-_softplus(-(w0 + w_lora)) - 0.5)
    a = jax.nn.sigmoid(a0 + a_lora)
    g = _dot(jax.nn.sigmoid(xg).astype(BF16), g2_ref[...])
    if has_vres:
        xv = shift_mix(pv_ref[0], prevv_scr, vmu_ref[...])
        v = v + (vf_ref[0] - v) * jax.nn.sigmoid(v0 + _dot(xv.astype(BF16), v2_ref[...]))
    else:
        vfo_ref[0] = v
    kk = k * kk_s
    kk = kk * lax.rsqrt(jnp.maximum(_exact_rhs_dot(kk * kk, seg), 1e-24))
    k = k * (1.0 + (a - 1.0) * ka)

    _rw_scan(r, lw, k, v, kk, a, st_scr, y_scr, passes)

    y = y_scr[...]
    inv_n = 1.0 / RW_HD
    y_mu = _exact_rhs_dot(y, seg) * inv_n
    yc = y - y_mu
    y_var = _exact_rhs_dot(yc * yc, seg) * inv_n
    y = yc * lax.rsqrt(y_var + RW_GN_EPS) * lnx_g + lnx_b
    bonus = _exact_rhs_dot(r * k * rk, seg) * v
    o_ref[0] = ((y + bonus) * g).astype(BF16)

    @pl.when(c == nc - 1)
    def _():
        sf_ref[0] = st_scr[...]


def _rwkv_prompt(p3, lp, v_first, passes=(3, 3, 3, 3)):
    b, t, _ = p3.shape
    nc = t // RW_TB
    has_vres = v_first is not None
    ng = RW_HEADS // RW_GH
    zpad = jnp.zeros((RW_LORA_W, RW_W), F32)
    wa = jnp.concatenate([jnp.concatenate([lp['rw_w2'], zpad], 0), jnp.concatenate([zpad, lp['rw_a2']], 0)], 1)
    vec = jnp.stack([lp['rw_w0'], lp['rw_a0'], lp['rw_kk'], lp['rw_ka'], lp['rw_rk'], lp['rw_lnx_g'], lp['rw_lnx_b'],
                     lp['rw_v0'] if has_vres else jnp.zeros((RW_W,), F32)])
    hid = jnp.arange(RW_W) // RW_HD
    seg = (hid[:, None] == hid[None, :]).astype(BF16)
    full = lambda shape: pl.BlockSpec(shape, lambda i, j: (0,) * len(shape))
    seq = lambda w, blk: pl.BlockSpec((1, RW_TB, w), lambda i, j: (i, j, blk))
    in_specs = [seq(RW_P, 0)]
    args = [p3]
    if has_vres:
        in_specs += [seq(LANES, P_V1 // LANES), seq(RW_W, 0)]
        args += [p3, v_first]
    in_specs += [full((1, RW_P)), full((RW_VEC_ROWS, RW_W)), full((LANES, 2 * RW_W)), full((RW_LORA_G, RW_W)),
                 full((RW_W, RW_W))]
    args += [lp['rw_mu'].reshape(1, RW_P), vec, wa.astype(BF16), lp['rw_g2'].astype(BF16), seg]
    if has_vres:
        in_specs += [full((1, LANES)), full((LANES, RW_W))]
        args += [jnp.pad(lp['rw_vmu'], (0, LANES - RW_LORA_V)).reshape(1, LANES),
                 jnp.pad(lp['rw_v2'], ((0, LANES - RW_LORA_V), (0, 0))).astype(BF16)]
    out_specs = [seq(RW_W, 0)]
    out_shape = [jax.ShapeDtypeStruct((b, t, RW_W), BF16)]
    if not has_vres:
        out_specs.append(seq(RW_W, 0))
        out_shape.append(jax.ShapeDtypeStruct((b, t, RW_W), F32))
    out_specs.append(pl.BlockSpec((1, ng, RW_GW, RW_GW), lambda i, j: (i, 0, 0, 0)))
    out_shape.append(jax.ShapeDtypeStruct((b, ng, RW_GW, RW_GW), F32))
    scratch = [pltpu.VMEM((ng, RW_GW, RW_GW), F32), pltpu.VMEM((1, RW_P), F32), pltpu.VMEM((RW_TB, RW_W), F32)]
    if has_vres:
        scratch.append(pltpu.VMEM((1, LANES), F32))
    outs = pl.pallas_call(
        functools.partial(_rwkv_fused_body, nc=nc, has_vres=has_vres, passes=passes),
        grid=(b, nc),
        in_specs=in_specs,
        out_specs=out_specs,
        out_shape=out_shape,
        scratch_shapes=scratch,
        compiler_params=_cparams(("parallel", "arbitrary")),
        name="rwkv_fused",
    )(*args)
    if has_vres:
        o, st_bd = outs
    else:
        o, v_first, st_bd = outs
    st5 = st_bd.reshape(b, ng, RW_GH, RW_HD, RW_GH, RW_HD)
    s_fin = jnp.stack([st5[:, :, h, :, h, :] for h in range(RW_GH)], axis=2)
    s_fin = s_fin.reshape(b, RW_HEADS, RW_HD, RW_HD).transpose(0, 1, 3, 2)
    return o, v_first, s_fin


def _ret_log_gamma(h):
    return math.log1p(-(2.0 ** (-5.0 - h)))


def _rotary_tables(pos, heads, dk):
    half = dk // 2
    inv = ROPE_BASE ** (-jnp.arange(half, dtype=F32) / half)
    ang = pos.astype(F32)[:, None] * inv[None, :]
    cos = jnp.tile(jnp.concatenate([jnp.cos(ang), jnp.cos(ang)], -1), (1, heads))
    sin = jnp.tile(jnp.concatenate([-jnp.sin(ang), jnp.sin(ang)], -1), (1, heads))
    lane = jnp.arange(heads * dk)
    perm = (lane[:, None] == (lane[None, :] ^ half)).astype(BF16)
    return cos, sin, perm


def _ret_fused_body(p_ref, cos_ref, sin_ref, perm_ref, o_ref, sf_ref, s_scr, *, nc):
    L, DK, DV, H = CHUNK, RET_QK, RET_V, RET_HEADS
    nq = H * DK
    c = pl.program_id(1)

    @pl.when(c == 0)
    def _():
        s_scr[...] = jnp.zeros_like(s_scr)

    cos = cos_ref[...]
    sin = sin_ref[...]
    perm = perm_ref[...]
    rot = lambda x: x * cos + _exact_rhs_dot(x, perm) * sin
    q_all = rot(p_ref[0, :, 0:nq])
    k_all = rot(p_ref[0, :, nq:2 * nq]) * (DK ** -0.5)
    row = lax.broadcasted_iota(jnp.int32, (L, L), 0)
    col = lax.broadcasted_iota(jnp.int32, (L, L), 1)
    rel = (row - col).astype(F32)
    idx = lax.broadcasted_iota(jnp.int32, (L, 1), 0).astype(F32)
    vs = lambda h: p_ref[0, :, 2 * nq + h * DV:2 * nq + (h + 1) * DV].astype(BF16)
    hd = [dict() for _ in range(H)]
    for h, t in enumerate(hd):
        lg = _ret_log_gamma(h)
        q = q_all[:, h * DK:(h + 1) * DK]
        k = k_all[:, h * DK:(h + 1) * DK]
        t['s_prev'] = s_scr[h]
        t['dmask'] = jnp.where(rel >= 0, jnp.exp(jnp.maximum(rel, 0.0) * lg), 0.0)
        t['qk'] = _dot_nt(q.astype(BF16), k.astype(BF16))
        q_dec = q * jnp.exp((idx + 1.0) * lg)
        t['qs'] = _dot(q_dec.astype(BF16), t['s_prev'].astype(BF16))
        k_end = k * jnp.exp((L - 1.0 - idx) * lg)
        t['kv'] = _dot_tn(k_end.astype(BF16), vs(h))
    for h, t in enumerate(hd):
        y = _dot((t['qk'] * t['dmask']).astype(BF16), vs(h)) + t['qs']
        y = y * lax.rsqrt(jnp.mean(y * y, -1, keepdims=True) + NORM_EPS)
        gate = p_ref[0, :, 2 * nq + RET_W + h * DV:2 * nq + RET_W + (h + 1) * DV]
        o_ref[0, :, h * DV:(h + 1) * DV] = (gate * jax.nn.sigmoid(gate) * y).astype(BF16)
    for h, t in enumerate(hd):
        s_scr[h] = math.exp(L * _ret_log_gamma(h)) * t['s_prev'] + t['kv']

    @pl.when(c == nc - 1)
    def _():
        sf_ref[0] = s_scr[...]


def _ret_prompt(p3, pos):
    b, t, _ = p3.shape
    L = CHUNK
    nc = t // L
    nq = RET_HEADS * RET_QK
    cos, sin, perm = _rotary_tables(pos, RET_HEADS, RET_QK)
    tab = pl.BlockSpec((L, nq), lambda i, j: (j, 0))
    st = pl.BlockSpec((1, RET_HEADS, RET_QK, RET_V), lambda i, j: (i, 0, 0, 0))
    return pl.pallas_call(
        functools.partial(_ret_fused_body, nc=nc),
        grid=(b, nc),
        in_specs=[pl.BlockSpec((1, L, P_MAIN), lambda i, j: (i, j, P_RET // P_MAIN)), tab, tab,
                  pl.BlockSpec((nq, nq), lambda i, j: (0, 0))],
        out_specs=[pl.BlockSpec((1, L, RET_W), lambda i, j: (i, j, 0)), st],
        out_shape=[jax.ShapeDtypeStruct((b, t, RET_W), BF16),
                   jax.ShapeDtypeStruct((b, RET_HEADS, RET_QK, RET_V), F32)],
        scratch_shapes=[pltpu.VMEM((RET_HEADS, RET_QK, RET_V), F32)],
        compiler_params=_cparams(("parallel", "arbitrary")),
        name="ret_fused",
    )(p3, cos, sin, perm)


ML_HPAD = 8


def _ml_fused_body(p_ref, gate_ref, bias_ref, norm_ref, o_ref, cf_ref, nf_ref, mf_ref, c_scr, n_scr, m_scr, *, nc):
    L, DK, DV, H = CHUNK, ML_QK, ML_V, ML_HEADS
    nq = H * DK
    ci = pl.program_id(1)

    @pl.when(ci == 0)
    def _():
        c_scr[...] = jnp.zeros_like(c_scr)
        n_scr[...] = jnp.zeros_like(n_scr)
        m_scr[...] = jnp.zeros_like(m_scr)

    row = lax.broadcasted_iota(jnp.int32, (L, L), 0)
    col = lax.broadcasted_iota(jnp.int32, (L, L), 1)
    causal = row >= col
    tri = causal.astype(BF16)
    capped = ML_GATE_CAP * jnp.tanh((gate_ref[0] + bias_ref[...]) * (1.0 / ML_GATE_CAP))
    lane = lax.broadcasted_iota(jnp.int32, (L, LANES), 1)
    g = jnp.where(lane < H, capped, jnp.where(lane < 2 * H, -_softplus(-capped), 0.0))
    cum = _exact_lhs_dot(tri, g)
    g_t = g.T
    cum_t = cum.T
    m_all = m_scr[...]
    n_all = n_scr[...]
    qs = lambda h: p_ref[0, :, h * DK:(h + 1) * DK]
    vs = lambda h: p_ref[0, :, 2 * nq + h * DV:2 * nq + (h + 1) * DV].astype(BF16)
    hd = [dict() for _ in range(H)]
    for h, t in enumerate(hd):
        q = qs(h)
        k = p_ref[0, :, nq + h * DK:nq + (h + 1) * DK] * (DK ** -0.5)
        ig_col = g[:, h:h + 1]
        ig_row = g_t[h:h + 1, :]
        b_col = cum[:, H + h:H + h + 1]
        b_row = cum_t[H + h:H + h + 1, :]
        b_tot = cum[L - 1:L, H + h:H + h + 1]
        m_prev = m_all[h:h + 1, 0:1]
        t['c_prev'] = c_scr[h]
        t['n_prev'] = n_all[h:h + 1, :]
        a_row = b_tot - b_row + ig_row
        a_col = b_tot - b_col + ig_col
        t['m_new'] = jnp.maximum(b_tot + m_prev, jnp.max(a_row, axis=1, keepdims=True))
        t['dec'] = jnp.exp(b_tot + m_prev - t['m_new'])
        kw = k * jnp.exp(a_col - t['m_new'])
        t['kv'] = _dot_tn(kw.astype(BF16), vs(h))
        t['kw_sum'] = jnp.sum(kw, axis=0, keepdims=True)
        dlog = jnp.where(causal, b_col - b_row + ig_row, -jnp.inf)
        inter = b_col + m_prev
        t['m_i'] = jnp.maximum(jnp.max(dlog, axis=1, keepdims=True), inter)
        t['e'] = jnp.exp(dlog - t['m_i'])
        t['sc'] = jnp.exp(inter - t['m_i'])
        t['qk'] = _dot_nt(q.astype(BF16), k.astype(BF16))
        t['qc'] = _dot(q.astype(BF16), t['c_prev'].astype(BF16))
        t['qn'] = jnp.sum(q * t['n_prev'], axis=1, keepdims=True)
    for h, t in enumerate(hd):
        s = t['qk'] * t['e']
        num = _dot(s.astype(BF16), vs(h)) + t['sc'] * t['qc']
        den = jnp.sum(s, axis=1, keepdims=True) + t['sc'] * t['qn']
        hid = num / jnp.maximum(jnp.abs(den), jnp.exp(-t['m_i']))
        hid = hid * lax.rsqrt(jnp.mean(hid * hid, -1, keepdims=True) + NORM_EPS)
        og = p_ref[0, :, 2 * nq + ML_W + h * DV:2 * nq + ML_W + (h + 1) * DV]
        o_ref[0, :, h * DV:(h + 1) * DV] = (jax.nn.sigmoid(og) * (hid * norm_ref[:, h * DV:(h + 1) * DV])).astype(BF16)
    for h, t in enumerate(hd):
        c_scr[h] = t['dec'] * t['c_prev'] + t['kv']
        n_scr[h:h + 1, :] = t['dec'] * t['n_prev'] + t['kw_sum']
        m_scr[h:h + 1, :] = jnp.broadcast_to(t['m_new'], (1, LANES))

    @pl.when(ci == nc - 1)
    def _():
        cf_ref[0] = c_scr[...]
        nf_ref[0] = n_scr[...]
        mf_ref[0] = m_scr[...]


def _ml_prompt(p3, lp):
    b, t, _ = p3.shape
    L = CHUNK
    nc = t // L
    bias = jnp.pad(jnp.concatenate([lp['ml_ib'], lp['ml_fb']]), (0, LANES - 2 * ML_HEADS)).reshape(1, LANES)
    vs = pl.BlockSpec((1, L, ML_W), lambda i, j: (i, j, 0))
    cs = pl.BlockSpec((1, ML_HEADS, ML_QK, ML_V), lambda i, j: (i, 0, 0, 0))
    ns = pl.BlockSpec((1, ML_HPAD, ML_QK), lambda i, j: (i, 0, 0))
    ms = pl.BlockSpec((1, ML_HPAD, LANES), lambda i, j: (i, 0, 0))
    o, c_f, n_f, m_f = pl.pallas_call(
        functools.partial(_ml_fused_body, nc=nc),
        grid=(b, nc),
        in_specs=[pl.BlockSpec((1, L, P_MAIN), lambda i, j: (i, j, P_ML // P_MAIN)),
                  pl.BlockSpec((1, L, LANES), lambda i, j: (i, j, P_GATE // LANES)),
                  pl.BlockSpec((1, LANES), lambda i, j: (0, 0)),
                  pl.BlockSpec((1, ML_W), lambda i, j: (0, 0))],
        out_specs=[vs, cs, ns, ms],
        out_shape=[jax.ShapeDtypeStruct((b, t, ML_W), BF16),
                   jax.ShapeDtypeStruct((b, ML_HEADS, ML_QK, ML_V), F32),
                   jax.ShapeDtypeStruct((b, ML_HPAD, ML_QK), F32),
                   jax.ShapeDtypeStruct((b, ML_HPAD, LANES), F32)],
        scratch_shapes=[pltpu.VMEM((ML_HEADS, ML_QK, ML_V), F32),
                        pltpu.VMEM((ML_HPAD, ML_QK), F32),
                        pltpu.VMEM((ML_HPAD, LANES), F32)],
        compiler_params=_cparams(("parallel", "arbitrary")),
        name="ml_fused",
    )(p3, p3, bias, lp['ml_norm'].reshape(1, ML_W))
    return o, c_f, n_f[:, :ML_HEADS], m_f[:, :ML_HEADS, 0]


def _to_cols(x):
    b, c = x.shape
    return x.reshape(b // DEC_TB, DEC_TB, c).transpose(0, 2, 1)


def _rwkv_step_body(w_ref, ap_ref, k_ref, kk_ref, r_ref, v_ref, s_ref, y_ref, so_ref):
    N = RW_HD
    v = v_ref[0]

    def sa_step(i, acc):
        return acc + kk_ref[0, pl.ds(i, 1), :] * s_ref[i]

    sa = lax.fori_loop(0, N, sa_step, jnp.zeros_like(v), unroll=8)

    def upd_step(i, y):
        row = lambda ref: ref[0, pl.ds(i, 1), :]
        s_new = row(w_ref) * s_ref[i] - row(ap_ref) * sa + row(k_ref) * v
        so_ref[i] = s_new
        return y + row(r_ref) * s_new

    y_ref[0] = lax.fori_loop(0, N, upd_step, jnp.zeros_like(v), unroll=8)


def _rwkv_step(wdec, ap, k, kk, r, v, s_all, layer):
    b, w = v.shape
    heads = lambda x: x.reshape(b, RW_HEADS, RW_HD).transpose(1, 2, 0)
    vec = pl.BlockSpec((1, RW_HD, b), lambda h: (h, 0, 0))
    y, s_new = pl.pallas_call(
        _rwkv_step_body,
        grid=(RW_HEADS,),
        in_specs=[vec] * 6 + [pl.BlockSpec((None, None, RW_HD, RW_HD, b), lambda h: (layer, h, 0, 0, 0))],
        out_specs=[vec, pl.BlockSpec((None, RW_HD, RW_HD, b), lambda h: (h, 0, 0, 0))],
        out_shape=[jax.ShapeDtypeStruct((RW_HEADS, RW_HD, b), F32),
                   jax.ShapeDtypeStruct((RW_HEADS, RW_HD, RW_HD, b), F32)],
        compiler_params=_cparams(("parallel",)),
        name="rwkv_step",
    )(heads(wdec), heads(ap), heads(k), heads(kk), heads(r), heads(v), s_all)
    return y.transpose(2, 0, 1).reshape(b, w), s_new


def _ret_step_body(qc_ref, kc_ref, v_ref, s_ref, y_ref, so_ref):
    DK, DV = RET_QK, RET_V
    for j in range(DEC_TB):
        for h in range(RET_HEADS):
            gamma = math.exp(_ret_log_gamma(h))
            q = qc_ref[0, h * DK:(h + 1) * DK, j:j + 1]
            k = kc_ref[0, h * DK:(h + 1) * DK, j:j + 1]
            v = v_ref[j:j + 1, h * DV:(h + 1) * DV]
            s = s_ref[j, h]
            qk = jnp.sum(q * k, axis=0, keepdims=True)
            y_ref[j:j + 1, h * DV:(h + 1) * DV] = qk * v + gamma * jnp.sum(q * s, axis=0, keepdims=True)
            so_ref[j, h] = gamma * s + k * v


def _ret_step(q, k, v, s_all, layer):
    b = q.shape[0]
    cols = pl.BlockSpec((1, RET_HEADS * RET_QK, DEC_TB), lambda i: (i, 0, 0))
    rows = pl.BlockSpec((DEC_TB, RET_W), lambda i: (i, 0))
    st = pl.BlockSpec((DEC_TB, RET_HEADS, RET_QK, RET_V), lambda i: (i, 0, 0, 0))
    st_in = pl.BlockSpec((None, DEC_TB, RET_HEADS, RET_QK, RET_V), lambda i: (layer, i, 0, 0, 0))
    return pl.pallas_call(
        _ret_step_body,
        grid=(b // DEC_TB,),
        in_specs=[cols, cols, rows, st_in],
        out_specs=[rows, st],
        out_shape=[jax.ShapeDtypeStruct((b, RET_W), F32), jax.ShapeDtypeStruct(s_all.shape[1:], F32)],
        compiler_params=_cparams(("parallel",)),
        name="ret_step",
    )(_to_cols(q), _to_cols(k), v, s_all)


def _ml_step_body(qc_ref, kc_ref, q_ref, k_ref, v_ref, ig_ref, lf_ref, c_ref, n_ref, m_ref,
                  h_ref, co_ref, no_ref, mo_ref):
    DK, DV = ML_QK, ML_V
    for j in range(DEC_TB):
        for h in range(ML_HEADS):
            ks = slice(h * DK, (h + 1) * DK)
            vs = slice(h * DV, (h + 1) * DV)
            q_col = qc_ref[0, ks, j:j + 1]
            k_col = kc_ref[0, ks, j:j + 1]
            q_row = q_ref[j:j + 1, ks]
            k_row = k_ref[j:j + 1, ks]
            v = v_ref[j:j + 1, vs]
            ig = ig_ref[j:j + 1, h:h + 1]
            lf = lf_ref[j:j + 1, h:h + 1]
            m_prev = m_ref[j:j + 1, h:h + 1]
            c_prev = c_ref[j, h]
            n_prev = n_ref[j, h:h + 1, :]
            m_new = jnp.maximum(lf + m_prev, ig)
            dec = jnp.exp(lf + m_prev - m_new)
            wgt = jnp.exp(ig - m_new)
            co_ref[j, h] = dec * c_prev + (k_col * wgt) * v
            no_ref[j, h:h + 1, :] = dec * n_prev + k_row * wgt
            mo_ref[j:j + 1, h:h + 1] = m_new
            s = jnp.sum(q_row * k_row, axis=1, keepdims=True) * wgt
            num = s * v + dec * jnp.sum(q_col * c_prev, axis=0, keepdims=True)
            den = s + dec * jnp.sum(q_row * n_prev, axis=1, keepdims=True)
            h_ref[j:j + 1, vs] = num / jnp.maximum(jnp.abs(den), jnp.exp(-m_new))


def _ml_step(q, k, v, ig, lf, c_all, layer, n0, m0):
    b = q.shape[0]
    cols = pl.BlockSpec((1, ML_HEADS * ML_QK, DEC_TB), lambda i: (i, 0, 0))
    qk_rows = pl.BlockSpec((DEC_TB, ML_HEADS * ML_QK), lambda i: (i, 0))
    rows = pl.BlockSpec((DEC_TB, ML_W), lambda i: (i, 0))
    sc = pl.BlockSpec((DEC_TB, ML_HEADS), lambda i: (i, 0))
    cs = pl.BlockSpec((DEC_TB, ML_HEADS, ML_QK, ML_V), lambda i: (i, 0, 0, 0))
    ns = pl.BlockSpec((DEC_TB, ML_HEADS, ML_QK), lambda i: (i, 0, 0))
    cs_in = pl.BlockSpec((None, DEC_TB, ML_HEADS, ML_QK, ML_V), lambda i: (layer, i, 0, 0, 0))
    return pl.pallas_call(
        _ml_step_body,
        grid=(b // DEC_TB,),
        in_specs=[cols, cols, qk_rows, qk_rows, rows, sc, sc, cs_in, ns, sc],
        out_specs=[rows, cs, ns, sc],
        out_shape=[jax.ShapeDtypeStruct((b, ML_W), F32), jax.ShapeDtypeStruct(c_all.shape[1:], F32),
                   jax.ShapeDtypeStruct(n0.shape, F32), jax.ShapeDtypeStruct(m0.shape, F32)],
        compiler_params=_cparams(("parallel",)),
        name="ml_step",
    )(_to_cols(q), _to_cols(k), q, k, v, ig, lf, c_all, n0, m0)


def _heads(a, h):
    return a.reshape(a.shape[:-1] + (h, a.shape[-1] // h))


def _shift_prev(p, prev_row):
    return jnp.concatenate([prev_row[:, None, :], p[:, :-1]], axis=1)


def _rotary(x, pos):
    half = x.shape[-1] // 2
    inv = ROPE_BASE ** (-jnp.arange(half, dtype=F32) / half)
    ang = pos.astype(F32)[:, None] * inv[None, :]
    cos = jnp.cos(ang)[None, :, None, :]
    sin = jnp.sin(ang)[None, :, None, :]
    x1, x2 = x[..., :half], x[..., half:]
    return jnp.concatenate([x1 * cos - x2 * sin, x1 * sin + x2 * cos], -1)


def _small_matmul(x, w):
    lead = x.shape[:-1]
    kdim, n = w.shape
    x2 = x.reshape(-1, kdim)
    m = x2.shape[0]
    kp = -(-kdim // LANES) * LANES
    npad = -(-n // LANES) * LANES
    x2 = jnp.pad(x2.astype(BF16), ((0, 0), (0, kp - kdim)))
    w2 = jnp.pad(w.astype(BF16), ((0, kp - kdim), (0, npad - n)))
    tm = 1024 if m % 1024 == 0 else m
    out = _matmul(x2, w2, tm, npad)
    return out[:, :n].reshape(lead + (n,))


def _mix_prompt(p, pos, v_first, lp):
    o_rw, v_first, s_new = _rwkv_prompt(p, lp, v_first, (1, 1, 1, 1))
    o_ret, r_new = _ret_prompt(p, pos)
    o_ml, c_new, n_new, m_new = _ml_prompt(p, lp)
    return (o_rw, o_ret, o_ml), v_first, (s_new, r_new, c_new, n_new, m_new)


def _mix_sample(p, pos, v_first, st, lp, prev_row):
    bsz, t_len, _ = p.shape

    p_rw = p[..., :RW_P]
    mixed = p_rw + (_shift_prev(p_rw, prev_row[:, :RW_P]) - p_rw) * lp['rw_mu']
    sizes = np.cumsum([RW_W, RW_W, RW_W, RW_LORA_W, RW_LORA_A, RW_LORA_G])[:-1]
    r, k, v, xw, xa, xg = jnp.split(mixed, [int(s) for s in sizes], axis=-1)
    w = -jax.nn.softplus(-(lp['rw_w0'] + _small_matmul(jnp.tanh(xw), lp['rw_w2']))) - 0.5
    a = jax.nn.sigmoid(lp['rw_a0'] + _small_matmul(xa, lp['rw_a2']))
    g = _small_matmul(jax.nn.sigmoid(xg), lp['rw_g2'])
    if v_first is None:
        v_first = v
    else:
        pv = p[..., P_V1:P_V1 + RW_LORA_V]
        xv = pv + (_shift_prev(pv, prev_row[:, P_V1:P_V1 + RW_LORA_V]) - pv) * lp['rw_vmu']
        v = v + (v_first - v) * jax.nn.sigmoid(lp['rw_v0'] + _small_matmul(xv, lp['rw_v2']))
    kk = _heads(k * lp['rw_kk'], RW_HEADS)
    kk = kk * lax.rsqrt(jnp.maximum(jnp.sum(jnp.square(kk), -1, keepdims=True), 1e-24))
    kk = kk.reshape(bsz, t_len, RW_W)
    k = k * (1.0 + (a - 1.0) * lp['rw_ka'])
    lw = -jnp.exp(w)
    y, s_new = _rwkv_step(jnp.exp(lw)[:, 0], (kk * a)[:, 0], k[:, 0], kk[:, 0], r[:, 0], v[:, 0],
                          st['rw_wkv_t'], st['layer'])
    y = _heads(y[:, None, :], RW_HEADS)
    y_mu = jnp.mean(y, -1, keepdims=True)
    y_var = jnp.mean(jnp.square(y - y_mu), -1, keepdims=True)
    y = ((y - y_mu) * lax.rsqrt(y_var + RW_GN_EPS)).reshape(bsz, t_len, RW_W)
    y = y * lp['rw_lnx_g'] + lp['rw_lnx_b']
    rh, kh, vh = (_heads(u, RW_HEADS) for u in (r, k, v))
    bonus = jnp.sum(rh * kh * _heads(lp['rw_rk'], RW_HEADS), -1, keepdims=True) * vh
    o_rw = ((y + bonus.reshape(bsz, t_len, RW_W)) * g).astype(BF16)

    nqk = RET_HEADS * RET_QK
    p_ret = p[..., P_RET:P_RET + P_MAIN]
    qr, kr, vr, gr = (p_ret[..., :nqk], p_ret[..., nqk:2 * nqk],
                      p_ret[..., 2 * nqk:2 * nqk + RET_W], p_ret[..., 2 * nqk + RET_W:])
    qh = _rotary(_heads(qr, RET_HEADS), pos).reshape(bsz, t_len, nqk)
    khr = (_rotary(_heads(kr, RET_HEADS), pos) * (RET_QK ** -0.5)).reshape(bsz, t_len, nqk)
    yr, r_new = _ret_step(qh[:, 0], khr[:, 0], vr[:, 0], st['ret_all'], st['layer'])
    yr = _heads(yr[:, None, :], RET_HEADS)
    yr = yr * lax.rsqrt(jnp.mean(jnp.square(yr), -1, keepdims=True) + NORM_EPS)
    o_ret = (jax.nn.silu(gr) * yr.reshape(bsz, t_len, RET_W)).astype(BF16)

    nqk = ML_HEADS * ML_QK
    p_ml = p[..., P_ML:P_ML + P_MAIN]
    qm, km, vm, om = (p_ml[..., :nqk], p_ml[..., nqk:2 * nqk],
                      p_ml[..., 2 * nqk:2 * nqk + ML_W], p_ml[..., 2 * nqk + ML_W:])
    im = p[..., P_GATE:P_GATE + ML_HEADS]
    fm = p[..., P_GATE + ML_HEADS:P_GATE + 2 * ML_HEADS]
    ig = ML_GATE_CAP * jnp.tanh((im + lp['ml_ib']) / ML_GATE_CAP)
    lf = jax.nn.log_sigmoid(ML_GATE_CAP * jnp.tanh((fm + lp['ml_fb']) / ML_GATE_CAP))
    km = km * (ML_QK ** -0.5)
    hm, c_new, n_new, m_new = _ml_step(qm[:, 0], km[:, 0], vm[:, 0], ig[:, 0], lf[:, 0],
                                       st['ml_c_all'], st['layer'], st['ml_n'], st['ml_m'])
    hm = _heads(hm[:, None, :], ML_HEADS)
    hm = hm * lax.rsqrt(jnp.mean(jnp.square(hm), -1, keepdims=True) + NORM_EPS)
    o_ml = (jax.nn.sigmoid(om) * (hm.reshape(bsz, t_len, ML_W) * lp['ml_norm'])).astype(BF16)

    return (o_rw, o_ret, o_ml), v_first, (s_new, r_new, c_new, n_new, m_new)


_T_V1 = P_V1 // LANES
_T_GATE = P_GATE // LANES
_T_RET = P_RET // LANES
_T_SHIFT = (P_RET - RW_P) // LANES
_T_SRC_GATE = (RW_P + RET_P + 2 * ML_HEADS * ML_QK + 2 * ML_W) // LANES


def _pack_body(w_ref, v1_ref, *o_refs):
    j = pl.program_id(0)
    row = lax.broadcasted_iota(jnp.int32, (LANES, 1), 0)
    spare = jnp.logical_and(j > _T_GATE, j < _T_RET)
    for l, o_ref in enumerate(o_refs):
        w = w_ref[:, l, :]
        gates = jnp.where(row < 2 * ML_HEADS, w, 0.0)
        out = jnp.where(j == _T_V1, v1_ref[l], jnp.where(j == _T_GATE, gates, jnp.where(spare, 0.0, w)))
        o_ref[...] = out.astype(BF16)


def _pack_w_in(w_in, rw_v1):
    depth, d, _ = w_in.shape
    w_t = w_in.transpose(2, 0, 1)
    v1_t = jnp.pad(rw_v1.transpose(0, 2, 1), ((1, 0), (0, LANES - RW_LORA_V), (0, 0)))

    def src_tile(j):
        return jnp.where(j < _T_V1, j, jnp.where(j == _T_GATE, _T_SRC_GATE, j - _T_SHIFT))

    return pl.pallas_call(
        _pack_body,
        grid=(P_PAD // LANES,),
        in_specs=[pl.BlockSpec((LANES, depth, d), lambda j: (src_tile(j), 0, 0)),
                  pl.BlockSpec((depth, LANES, d), lambda j: (0, 0, 0))],
        out_specs=[pl.BlockSpec((LANES, d), lambda j: (j, 0))] * depth,
        out_shape=[jax.ShapeDtypeStruct((P_PAD, d), BF16)] * depth,
        compiler_params=_cparams(("parallel",)),
        name="pack_w_in",
    )(w_t, v1_t)


def _token_tiles(m):
    if m % 2048 == 0:
        return 2048, 512, 512
    return m, m, m


def kernel(x_prompt, x_sample, state_rw_shift, state_rw_wkv, state_ret, state_ml_c, state_ml_n, state_ml_m,
           ln0_g, ln0_b, w_in, rw_mu, rw_w0, rw_w2, rw_a0, rw_a2, rw_g2, rw_kk, rw_ka, rw_rk,
           rw_lnx_g, rw_lnx_b, rw_v0, rw_v1, rw_vmu, rw_v2, ml_ib, ml_fb, ml_norm, w_out,
           ln1_g, ln1_b, w_gate, w_up, w_down, ln2_g, ln2_b):
    bp, tp, d = x_prompt.shape
    bs, ts, _ = x_sample.shape
    groups = {
        'p': dict(b=bp, t=tp, pos=jnp.arange(tp), chunked=True),
        's': dict(b=bs, t=ts, pos=PAST_LEN + jnp.arange(ts), chunked=False),
    }
    rw_wkv_t = state_rw_wkv.transpose(0, 2, 3, 4, 1)
    w_in_packed = _pack_w_in(w_in, rw_v1)
    xs_f, xs_b, v_first, outs = {}, {}, {}, {}
    for name, x in (('p', x_prompt), ('s', x_sample)):
        m = x.shape[0] * x.shape[1]
        xs_f[name], xs_b[name] = _layernorm(x.reshape(m, d), ln0_g, ln0_b, _token_tiles(m)[1])
        v_first[name] = None
        outs[name] = []

    for l in range(DEPTH):
        lp = {
            'rw_mu': rw_mu[l], 'rw_w0': rw_w0[l], 'rw_w2': rw_w2[l], 'rw_a0': rw_a0[l], 'rw_a2': rw_a2[l],
            'rw_g2': rw_g2[l], 'rw_kk': rw_kk[l], 'rw_ka': rw_ka[l], 'rw_rk': rw_rk[l],
            'rw_lnx_g': rw_lnx_g[l], 'rw_lnx_b': rw_lnx_b[l], 'ml_ib': ml_ib[l], 'ml_fb': ml_fb[l],
            'ml_norm': ml_norm[l],
        }
        if l > 0:
            lp.update(rw_v0=rw_v0[l - 1], rw_vmu=rw_vmu[l - 1], rw_v2=rw_v2[l - 1])
        w_in_b = w_in_packed[l]
        w_out_b = w_out[l].astype(BF16)
        w_down_b = w_down[l].astype(BF16)
        prev_s = _matmul(state_rw_shift[l].astype(BF16), w_in_b, bs, P_TN, w_transposed=True)
        for name, grp in groups.items():
            b, t = grp['b'], grp['t']
            m = b * t
            tm_big, tm_out, tm_down = _token_tiles(m)
            p = _matmul(xs_b[name], w_in_b, tm_big, P_TN, w_transposed=True).reshape(b, t, P_PAD)
            if grp['chunked']:
                o, v_first[name], new_st = _mix_prompt(p, grp['pos'], v_first[name], lp)
            else:
                st = {'rw_wkv_t': rw_wkv_t, 'layer': l, 'ret_all': state_ret, 'ml_c_all': state_ml_c,
                      'ml_n': state_ml_n[l], 'ml_m': state_ml_m[l]}
                o, v_first[name], new_st = _mix_sample(p, grp['pos'], v_first[name], st, lp, prev_s)
            outs[name].append((xs_f[name].reshape(b, t, d)[:, -1],) + new_st)
            o_rw, o_ret, o_ml = (u.reshape(m, u.shape[-1]) for u in o)
            x1_f, x1_b = _out_proj_ln(o_rw, o_ret, o_ml, w_out_b, xs_f[name], ln1_g[l], ln1_b[l], tm_out)
            hdn = _matmul_swiglu(x1_b, w_gate, w_up, l, tm_big, 512)
            xs_f[name], xs_b[name] = _matmul_res_ln(hdn, w_down_b, x1_f, ln2_g[l], ln2_b[l], tm_down, DOWN_TK)

    y_p = xs_f['p'].reshape(bp, tp, d)
    y_s = xs_f['s'].reshape(bs, ts, d)
    sp = [jnp.stack([o[i] for o in outs['p']]) for i in range(6)]
    ss = [jnp.stack([o[i] for o in outs['s']]) for i in range(6)]
    ss[1] = ss[1].transpose(0, 4, 1, 2, 3)
    return (y_p, y_s, sp[0], sp[1], sp[2], sp[3], sp[4], sp[5], ss[0], ss[1], ss[2], ss[3], ss[4], ss[5])
```

```python
import functools
import math

import numpy as np
import jax
import jax.numpy as jnp
from jax import lax
from jax.experimental import pallas as pl
from jax.experimental.pallas import tpu as pltpu

F32 = jnp.float32
BF16 = jnp.bfloat16

D_MODEL = 2048
DEPTH = 2
PAST_LEN = 16384
RW_HD = 64
RW_W = D_MODEL // 4
RW_HEADS = RW_W // RW_HD
RW_LORA_W = 64
RW_LORA_A = 64
RW_LORA_V = 32
RW_LORA_G = 128
RW_P = 3 * RW_W + RW_LORA_W + RW_LORA_A + RW_LORA_G
RW_GN_EPS = 64e-5
RET_V = 128
RET_QK = 64
RET_W = 3 * D_MODEL // 8
RET_HEADS = RET_W // RET_V
RET_P = 2 * RET_HEADS * RET_QK + 2 * RET_W
ML_V = 128
ML_QK = 64
ML_W = D_MODEL - RW_W - RET_W
ML_HEADS = ML_W // ML_V
ML_P = 2 * ML_HEADS * ML_QK + 2 * ML_W + 2 * ML_HEADS
ML_GATE_CAP = 15.0
P_TOTAL = RW_P + RET_P + ML_P
D_FF = ((8 * D_MODEL + 3 * 256 - 1) // (3 * 256)) * 256
CHUNK = 128
ROPE_BASE = 10000.0
LN_EPS = 1e-5
NORM_EPS = 1e-6
ALPHA = (2 * DEPTH) ** 0.25

LANES = 128
P_V1 = RW_P
P_GATE = RW_P + LANES
P_RET = 2304
P_ML = 2 * P_RET
P_MAIN = 2304
P_PAD = 3 * P_RET
P_TN = 768
RW_CHUNK = 64
DEC_TB = 8
VMEM_LIMIT = 56 * 1024 * 1024

HIGHEST = lax.Precision.HIGHEST


def _cparams(sem):
    return pltpu.CompilerParams(dimension_semantics=sem, vmem_limit_bytes=VMEM_LIMIT)


def _dot(a, b, precision=None):
    return lax.dot_general(a, b, (((1,), (0,)), ((), ())), precision=precision, preferred_element_type=F32)


def _dot_nt(a, b, precision=None):
    return lax.dot_general(a, b, (((1,), (1,)), ((), ())), precision=precision, preferred_element_type=F32)


def _dot_tn(a, b, precision=None):
    return lax.dot_general(a, b, (((0,), (0,)), ((), ())), precision=precision, preferred_element_type=F32)


def _ln_rows(x, g, b):
    mu = jnp.mean(x, -1, keepdims=True)
    xc = x - mu
    var = jnp.mean(xc * xc, -1, keepdims=True)
    return xc * lax.rsqrt(var + LN_EPS) * g + b


def _ln_body(x_ref, g_ref, b_ref, of_ref, ob_ref):
    y = _ln_rows(x_ref[...], g_ref[...], b_ref[...])
    of_ref[...] = y
    ob_ref[...] = y.astype(BF16)


def _layernorm(x, g, b, tm):
    m, d = x.shape
    return pl.pallas_call(
        _ln_body,
        grid=(m // tm,),
        in_specs=[pl.BlockSpec((tm, d), lambda i: (i, 0)),
                  pl.BlockSpec((1, d), lambda i: (0, 0)),
                  pl.BlockSpec((1, d), lambda i: (0, 0))],
        out_specs=[pl.BlockSpec((tm, d), lambda i: (i, 0)),
                   pl.BlockSpec((tm, d), lambda i: (i, 0))],
        out_shape=[jax.ShapeDtypeStruct((m, d), F32), jax.ShapeDtypeStruct((m, d), BF16)],
        compiler_params=_cparams(("parallel",)),
        name="layernorm",
    )(x, g.reshape(1, d), b.reshape(1, d))


def _mm_body(x_ref, w_ref, o_ref, *, w_transposed):
    dot = _dot_nt if w_transposed else _dot
    o_ref[...] = dot(x_ref[...], w_ref[...]).astype(o_ref.dtype)


def _matmul(x, w, tm, tn, out_dtype=F32, w_transposed=False):
    m, k = x.shape
    n = w.shape[0] if w_transposed else w.shape[1]
    w_spec = (pl.BlockSpec((tn, k), lambda i, j: (j, 0)) if w_transposed
              else pl.BlockSpec((k, tn), lambda i, j: (0, j)))
    return pl.pallas_call(
        functools.partial(_mm_body, w_transposed=w_transposed),
        grid=(m // tm, n // tn),
        in_specs=[pl.BlockSpec((tm, k), lambda i, j: (i, 0)), w_spec],
        out_specs=pl.BlockSpec((tm, tn), lambda i, j: (i, j)),
        out_shape=jax.ShapeDtypeStruct((m, n), out_dtype),
        compiler_params=_cparams(("parallel", "parallel")),
        name="matmul",
    )(x, w)


def _in_proj_body(x_ref, xs_ref, w_ref, o_ref, os_ref):
    @pl.when(pl.program_id(1) == 0)
    def _():
        os_ref[...] = _dot_nt(xs_ref[...], w_ref[...])

    o_ref[...] = _dot_nt(x_ref[...], w_ref[...])


def _in_proj(x, x_side, w_t, tm, tn):
    m, k = x.shape
    ms = x_side.shape[0]
    n = w_t.shape[0]
    return pl.pallas_call(
        _in_proj_body,
        grid=(n // tn, m // tm),
        in_specs=[pl.BlockSpec((tm, k), lambda j, i: (i, 0)), pl.BlockSpec((ms, k), lambda j, i: (0, 0)),
                  pl.BlockSpec((tn, k), lambda j, i: (j, 0))],
        out_specs=[pl.BlockSpec((tm, tn), lambda j, i: (i, j)), pl.BlockSpec((ms, tn), lambda j, i: (0, j))],
        out_shape=[jax.ShapeDtypeStruct((m, n), F32), jax.ShapeDtypeStruct((ms, n), F32)],
        compiler_params=_cparams(("parallel", "arbitrary")),
        name="in_proj",
    )(x, x_side, w_t)


def _swiglu_body(x_ref, xs_ref, wg_ref, wu_ref, o_ref, os_ref, wg_scr, wu_scr):
    def act(x):
        g = _dot(x, wg_scr[...])
        return (g * jax.nn.sigmoid(g) * _dot(x, wu_scr[...])).astype(BF16)

    @pl.when(pl.program_id(1) == 0)
    def _():
        wg_scr[...] = wg_ref[...].astype(BF16)
        wu_scr[...] = wu_ref[...].astype(BF16)
        os_ref[...] = act(xs_ref[...])

    o_ref[...] = act(x_ref[...])


def _matmul_swiglu(x, x_side, wg, wu, layer, tm, tn):
    m, k = x.shape
    ms = x_side.shape[0]
    n = wg.shape[2]
    w_spec = pl.BlockSpec((None, k, tn), lambda j, i: (layer, 0, j))
    return pl.pallas_call(
        _swiglu_body,
        grid=(n // tn, m // tm),
        in_specs=[pl.BlockSpec((tm, k), lambda j, i: (i, 0)), pl.BlockSpec((ms, k), lambda j, i: (0, 0)),
                  w_spec, w_spec],
        out_specs=[pl.BlockSpec((tm, tn), lambda j, i: (i, j)), pl.BlockSpec((ms, tn), lambda j, i: (0, j))],
        out_shape=[jax.ShapeDtypeStruct((m, n), BF16), jax.ShapeDtypeStruct((ms, n), BF16)],
        scratch_shapes=[pltpu.VMEM((k, tn), BF16), pltpu.VMEM((k, tn), BF16)],
        compiler_params=_cparams(("parallel", "arbitrary")),
        name="matmul_swiglu",
    )(x, x_side, wg, wu)


DOWN_TK = D_FF // 4
LN_ROWS = 256
OUT_ROWS = 256


def _res_ln_store(acc_ref, res_ref, g_ref, b_ref, of_ref, ob_ref, n_rows):
    step = min(LN_ROWS, n_rows)
    for r in range(0, n_rows, step):
        rows = pl.ds(r, step)
        y = _ln_rows(ALPHA * res_ref[rows, :] + acc_ref[rows, :], g_ref[...], b_ref[...])
        of_ref[rows, :] = y
        ob_ref[rows, :] = y.astype(BF16)


def _mm_res_ln_body(x_ref, xs_ref, w_ref, res_ref, ress_ref, g_ref, b_ref, of_ref, ob_ref, ofs_ref, obs_ref, *, nk, tm, ms):
    i = pl.program_id(0)
    kk = pl.program_id(1)

    @pl.when(kk == 0)
    def _():
        of_ref[...] = jnp.zeros_like(of_ref)

    of_ref[...] += _dot(x_ref[...], w_ref[...])

    @pl.when(kk == nk - 1)
    def _():
        _res_ln_store(of_ref, res_ref, g_ref, b_ref, of_ref, ob_ref, tm)

    @pl.when(i == 0)
    def _():
        @pl.when(kk == 0)
        def _():
            ofs_ref[...] = jnp.zeros_like(ofs_ref)

        ofs_ref[...] += _dot(xs_ref[...], w_ref[...])

        @pl.when(kk == nk - 1)
        def _():
            _res_ln_store(ofs_ref, ress_ref, g_ref, b_ref, ofs_ref, obs_ref, ms)


def _matmul_res_ln(x, x_side, w, res, res_side, g, b, tm, tk):
    m, k = x.shape
    ms = x_side.shape[0]
    n = w.shape[1]
    nk = k // tk
    const = lambda r: pl.BlockSpec((r, n), lambda i, j: (0, 0))
    main = pl.BlockSpec((tm, n), lambda i, j: (i, 0))
    return pl.pallas_call(
        functools.partial(_mm_res_ln_body, nk=nk, tm=tm, ms=ms),
        grid=(m // tm, nk),
        in_specs=[pl.BlockSpec((tm, tk), lambda i, j: (i, j)), pl.BlockSpec((ms, tk), lambda i, j: (0, j)),
                  pl.BlockSpec((tk, n), lambda i, j: (j, 0)), main, const(ms), const(1), const(1)],
        out_specs=[main, main, const(ms), const(ms)],
        out_shape=[jax.ShapeDtypeStruct((m, n), F32), jax.ShapeDtypeStruct((m, n), BF16),
                   jax.ShapeDtypeStruct((ms, n), F32), jax.ShapeDtypeStruct((ms, n), BF16)],
        compiler_params=_cparams(("arbitrary", "arbitrary")),
        name="matmul_res_ln",
    )(x, x_side, w, res, res_side, g.reshape(1, n), b.reshape(1, n))


def _out_proj_ln_body(o_rw_ref, o_ret_ref, o_ml_ref, res_ref, s_rw_ref, s_ret_ref, s_ml_ref, ress_ref,
                      w_ref, g_ref, b_ref, of_ref, ob_ref, ofs_ref, obs_ref, *, tm, ms):
    def project(rw_ref, ret_ref, ml_ref, r_ref, f_ref, h_ref, n_rows):
        step = min(OUT_ROWS, n_rows)
        for r in range(0, n_rows, step):
            rows = pl.ds(r, step)
            mix = (_dot(rw_ref[rows, :], w_ref[0:RW_W, :])
                   + _dot(ret_ref[rows, :], w_ref[RW_W:RW_W + RET_W, :])
                   + _dot(ml_ref[rows, :], w_ref[RW_W + RET_W:, :]))
            y = _ln_rows(ALPHA * r_ref[rows, :] + mix, g_ref[...], b_ref[...])
            f_ref[rows, :] = y
            h_ref[rows, :] = y.astype(BF16)

    @pl.when(pl.program_id(0) == 0)
    def _():
        project(s_rw_ref, s_ret_ref, s_ml_ref, ress_ref, ofs_ref, obs_ref, ms)

    project(o_rw_ref, o_ret_ref, o_ml_ref, res_ref, of_ref, ob_ref, tm)


def _out_proj_ln(o, res, o_side, res_side, w, g, b, tm):
    m = res.shape[0]
    ms = res_side.shape[0]
    n = w.shape[1]
    rows = lambda width: pl.BlockSpec((tm, width), lambda i: (i, 0))
    const = lambda r, c: pl.BlockSpec((r, c), lambda i: (0, 0))
    widths = (RW_W, RET_W, ML_W)
    return pl.pallas_call(
        functools.partial(_out_proj_ln_body, tm=tm, ms=ms),
        grid=(m // tm,),
        in_specs=([rows(c) for c in widths] + [rows(n)] + [const(ms, c) for c in widths] + [const(ms, n)]
                  + [const(D_MODEL, n), const(1, n), const(1, n)]),
        out_specs=[rows(n), rows(n), const(ms, n), const(ms, n)],
        out_shape=[jax.ShapeDtypeStruct((m, n), F32), jax.ShapeDtypeStruct((m, n), BF16),
                   jax.ShapeDtypeStruct((ms, n), F32), jax.ShapeDtypeStruct((ms, n), BF16)],
        compiler_params=_cparams(("arbitrary",)),
        name="out_proj_ln",
    )(*o, res, *o_side, res_side, w, g.reshape(1, n), b.reshape(1, n))


RW_TB = 256
RW_GH = 4
RW_GW = RW_GH * RW_HD
RW_VEC_ROWS = 8


def _split3(x):
    hi = x.astype(BF16)
    r1 = x - hi.astype(F32)
    mid = r1.astype(BF16)
    lo = (r1 - mid.astype(F32)).astype(BF16)
    return hi, mid, lo


def _mm(a, b, dims, passes):
    dg = lambda x, y: lax.dot_general(x, y, (dims, ((), ())), preferred_element_type=F32)
    if passes == 6:
        return lax.dot_general(a, b, (dims, ((), ())), precision=HIGHEST, preferred_element_type=F32)
    ah = a.astype(BF16)
    bh = b.astype(BF16)
    if passes == 1:
        return dg(ah, bh)
    al = (a - ah.astype(F32)).astype(BF16)
    bl = (b - bh.astype(F32)).astype(BF16)
    return dg(ah, bh) + (dg(ah, bl) + dg(al, bh))


_NN = ((1,), (0,))
_NT = ((1,), (1,))
_TN = ((0,), (0,))


def _exact_lhs_dot(a_bf16, b):
    hi, mid, lo = _split3(b)
    dg = lambda y: lax.dot_general(a_bf16, y, (_NN, ((), ())), preferred_element_type=F32)
    return dg(hi) + (dg(mid) + dg(lo))


def _exact_rhs_dot(a, b_bf16):
    hi, mid, lo = _split3(a)
    dg = lambda x: lax.dot_general(x, b_bf16, (_NN, ((), ())), preferred_element_type=F32)
    return dg(hi) + (dg(mid) + dg(lo))


def _seg_sum(a, seg_bf16):
    hi = a.astype(BF16)
    lo = (a - hi.astype(F32)).astype(BF16)
    dg = lambda x: lax.dot_general(x, seg_bf16, (_NN, ((), ())), preferred_element_type=F32)
    return dg(hi) + dg(lo)


def _rw_scan(r, lw, k, v, kk, a, st_scr, y_scr, passes):
    L, TB, G = RW_CHUNK, RW_TB, RW_GW
    p_sc, p_inv, p_app, p_st = passes
    row = lax.broadcasted_iota(jnp.int32, (L, G), 0)
    col = lax.broadcasted_iota(jnp.int32, (L, G), 1) & (L - 1)
    strict, lower, eye = row > col, row >= col, (row == col).astype(F32)
    rg = lax.broadcasted_iota(jnp.int32, (G, G), 0) // RW_HD
    cg = lax.broadcasted_iota(jnp.int32, (G, G), 1) // RW_HD
    mask_bd = rg == cg
    rt = lax.broadcasted_iota(jnp.int32, (TB, TB), 0)
    ct = lax.broadcasted_iota(jnp.int32, (TB, TB), 1)
    tri = jnp.logical_and(rt >= ct, rt // L == ct // L).astype(BF16)
    bd = lambda x: jnp.where(mask_bd, jnp.concatenate([x] * RW_GH, axis=0), 0.0)
    blocks = [(s, g) for s in range(TB // L) for g in range(RW_HEADS // RW_GH)]
    cut = lambda x, b: x[b[0] * L:(b[0] + 1) * L, b[1] * G:(b[1] + 1) * G]

    cum = _exact_lhs_dot(tri, lw)
    e_neg = jnp.exp(-cum)
    ap = kk * a
    ap_h = ap * e_neg
    k_h = k * e_neg
    kk_t = kk * jnp.exp(cum - lw)
    r_t = r * jnp.exp(cum)

    lhs, n_m, m_a, m_kr = {}, {}, {}, {}
    for b in blocks:
        lhs[b] = jnp.concatenate([cut(kk_t, b), cut(r_t, b)], axis=0)
        sc_a = _mm(lhs[b], bd(cut(ap_h, b)), _NT, p_sc)
        sc_k = _mm(lhs[b], bd(cut(k_h, b)), _NT, p_sc)
        n_m[b] = jnp.where(strict, sc_a[:L], 0.0)
        m_a[b] = jnp.where(lower, sc_a[L:], 0.0)
        m_kr[b] = jnp.concatenate([jnp.where(strict, sc_k[:L], 0.0), jnp.where(lower, sc_k[L:], 0.0)], axis=0)
    inv = {b: eye - n_m[b] for b in blocks}
    pw = {b: _mm(n_m[b], bd(n_m[b]), _NN, p_inv) for b in blocks}
    n_iter = int(math.log2(L)) - 1
    for j in range(n_iter):
        last = j == n_iter - 1
        for b in blocks:
            lhs_j = inv[b] if last else jnp.concatenate([inv[b], pw[b]], axis=0)
            prod = _mm(lhs_j, bd(pw[b]), _NN, p_inv)
            inv[b] = inv[b] + prod[:L]
            if not last:
                pw[b] = prod[L:]
    mv, vk, a_end, decay = {}, {}, {}, {}
    for b in blocks:
        s, g = b
        tot = cum[(s + 1) * L - 1:(s + 1) * L, g * G:(g + 1) * G]
        e_end = jnp.exp(tot - cut(cum, b))
        mv[b] = _mm(m_kr[b], bd(cut(v, b)), _NN, p_app)
        vk[b] = _mm(cut(v, b), cut(k, b) * e_end, _TN, p_st)
        a_end[b] = cut(ap, b) * e_end
        decay[b] = jnp.exp(tot)

    for b in blocks:
        s, g = b
        st = st_scr[g]
        s_terms = _mm(lhs[b], st, _NT, p_app)
        u = _mm(inv[b], bd(s_terms[:L] + mv[b][:L]), _NN, p_app)
        y_scr[s * L:(s + 1) * L, g * G:(g + 1) * G] = s_terms[L:] + mv[b][L:] - _mm(m_a[b], bd(u), _NN, p_app)
        st_scr[g] = jnp.where(mask_bd, st * decay[b] + vk[b] - _mm(u, a_end[b], _TN, p_st), 0.0)


def _softplus(z):
    return jnp.maximum(z, 0.0) + jnp.log(1.0 + jnp.exp(-jnp.abs(z)))


def _rwkv_fused_body(*refs, nc, has_vres, passes):
    if has_vres:
        (p_ref, pv_ref, vf_ref, mu_ref, vec_ref, wa_ref, g2_ref, seg_ref, vmu_ref, v2_ref,
         o_ref, sf_ref, st_scr, prev_scr, y_scr, prevv_scr) = refs
    else:
        (p_ref, mu_ref, vec_ref, wa_ref, g2_ref, seg_ref,
         o_ref, vfo_ref, sf_ref, st_scr, prev_scr, y_scr) = refs
    TB, W, L = RW_TB, RW_W, RW_CHUNK
    c = pl.program_id(1)

    @pl.when(c == 0)
    def _():
        st_scr[...] = jnp.zeros_like(st_scr)
        prev_scr[...] = jnp.zeros_like(prev_scr)
        if has_vres:
            prevv_scr[...] = jnp.zeros_like(prevv_scr)

    first_row = lax.broadcasted_iota(jnp.int32, (TB, 1), 0) == 0

    def shift_mix(x, carry_ref, mu):
        prev = jnp.where(first_row, carry_ref[...], pltpu.roll(x, 1, 0))
        carry_ref[...] = x[TB - 1:TB, :]
        return x + (prev - x) * mu

    mixed = shift_mix(p_ref[0], prev_scr, mu_ref[...])
    r = mixed[:, 0:W]
    k = mixed[:, W:2 * W]
    v = mixed[:, 2 * W:3 * W]
    xwa = mixed[:, 3 * W:3 * W + LANES]
    xg = mixed[:, 3 * W + LANES:3 * W + 2 * LANES]
    vec = vec_ref[...]
    w0, a0, kk_s, ka, rk, lnx_g, lnx_b, v0 = (vec[i:i + 1, :] for i in range(RW_VEC_ROWS))
    seg = seg_ref[...]
    wa = wa_ref[...]
    w_lora = _dot(jnp.tanh(xwa).astype(BF16), wa[:, 0:W])
    a_lora = _dot(xwa.astype(BF16), wa[:, W:2 * W])
    lw = -jnp.exp(-_softplus(-(w0 + w_lora)) - 0.5)
    a = jax.nn.sigmoid(a0 + a_lora)
    g = _dot(jax.nn.sigmoid(xg).astype(BF16), g2_ref[...])
    if has_vres:
        xv = shift_mix(pv_ref[0], prevv_scr, vmu_ref[...])
        v = v + (vf_ref[0] - v) * jax.nn.sigmoid(v0 + _dot(xv.astype(BF16), v2_ref[...]))
    else:
        vfo_ref[0] = v
    kk = k * kk_s
    kk = kk * lax.rsqrt(jnp.maximum(_seg_sum(kk * kk, seg), 1e-24))
    k = k * (1.0 + (a - 1.0) * ka)

    _rw_scan(r, lw, k, v, kk, a, st_scr, y_scr, passes)

    y = y_scr[...]
    inv_n = 1.0 / RW_HD
    y_mu = _seg_sum(y, seg) * inv_n
    yc = y - y_mu
    y_var = _seg_sum(yc * yc, seg) * inv_n
    y = yc * lax.rsqrt(y_var + RW_GN_EPS) * lnx_g + lnx_b
    bonus = _seg_sum(r * k * rk, seg) * v
    o_ref[0] = ((y + bonus) * g).astype(BF16)

    @pl.when(c == nc - 1)
    def _():
        sf_ref[0] = st_scr[...]


def _rwkv_prompt(p3, lp, v_first, passes=(3, 3, 3, 3)):
    b, t, _ = p3.shape
    nc = t // RW_TB
    has_vres = v_first is not None
    ng = RW_HEADS // RW_GH
    zpad = jnp.zeros((RW_LORA_W, RW_W), F32)
    wa = jnp.concatenate([jnp.concatenate([lp['rw_w2'], zpad], 0), jnp.concatenate([zpad, lp['rw_a2']], 0)], 1)
    vec = jnp.stack([lp['rw_w0'], lp['rw_a0'], lp['rw_kk'], lp['rw_ka'], lp['rw_rk'], lp['rw_lnx_g'], lp['rw_lnx_b'],
                     lp['rw_v0'] if has_vres else jnp.zeros((RW_W,), F32)])
    hid = jnp.arange(RW_W) // RW_HD
    seg = (hid[:, None] == hid[None, :]).astype(BF16)
    full = lambda shape: pl.BlockSpec(shape, lambda i, j: (0,) * len(shape))
    seq = lambda w, blk: pl.BlockSpec((1, RW_TB, w), lambda i, j: (i, j, blk))
    in_specs = [seq(RW_P, 0)]
    args = [p3]
    if has_vres:
        in_specs += [seq(LANES, P_V1 // LANES), seq(RW_W, 0)]
        args += [p3, v_first]
    in_specs += [full((1, RW_P)), full((RW_VEC_ROWS, RW_W)), full((LANES, 2 * RW_W)), full((RW_LORA_G, RW_W)),
                 full((RW_W, RW_W))]
    args += [lp['rw_mu'].reshape(1, RW_P), vec, wa.astype(BF16), lp['rw_g2'].astype(BF16), seg]
    if has_vres:
        in_specs += [full((1, LANES)), full((LANES, RW_W))]
        args += [jnp.pad(lp['rw_vmu'], (0, LANES - RW_LORA_V)).reshape(1, LANES),
                 jnp.pad(lp['rw_v2'], ((0, LANES - RW_LORA_V), (0, 0))).astype(BF16)]
    out_specs = [seq(RW_W, 0)]
    out_shape = [jax.ShapeDtypeStruct((b, t, RW_W), BF16)]
    if not has_vres:
        out_specs.append(seq(RW_W, 0))
        out_shape.append(jax.ShapeDtypeStruct((b, t, RW_W), F32))
    out_specs.append(pl.BlockSpec((1, ng, RW_GW, RW_GW), lambda i, j: (i, 0, 0, 0)))
    out_shape.append(jax.ShapeDtypeStruct((b, ng, RW_GW, RW_GW), F32))
    scratch = [pltpu.VMEM((ng, RW_GW, RW_GW), F32), pltpu.VMEM((1, RW_P), F32), pltpu.VMEM((RW_TB, RW_W), F32)]
    if has_vres:
        scratch.append(pltpu.VMEM((1, LANES), F32))
    outs = pl.pallas_call(
        functools.partial(_rwkv_fused_body, nc=nc, has_vres=has_vres, passes=passes),
        grid=(b, nc),
        in_specs=in_specs,
        out_specs=out_specs,
        out_shape=out_shape,
        scratch_shapes=scratch,
        compiler_params=_cparams(("parallel", "arbitrary")),
        name="rwkv_fused",
    )(*args)
    if has_vres:
        o, st_bd = outs
    else:
        o, v_first, st_bd = outs
    st5 = st_bd.reshape(b, ng, RW_GH, RW_HD, RW_GH, RW_HD)
    s_fin = jnp.stack([st5[:, :, h, :, h, :] for h in range(RW_GH)], axis=2)
    s_fin = s_fin.reshape(b, RW_HEADS, RW_HD, RW_HD).transpose(0, 1, 3, 2)
    return o, v_first, s_fin


def _ret_log_gamma(h):
    return math.log1p(-(2.0 ** (-5.0 - h)))


def _rotary_tables(pos, heads, dk):
    half = dk // 2
    inv = ROPE_BASE ** (-jnp.arange(half, dtype=F32) / half)
    ang = pos.astype(F32)[:, None] * inv[None, :]
    cos = jnp.tile(jnp.concatenate([jnp.cos(ang), jnp.cos(ang)], -1), (1, heads))
    sin = jnp.tile(jnp.concatenate([-jnp.sin(ang), jnp.sin(ang)], -1), (1, heads))
    lane = jnp.arange(heads * dk)
    perm = (lane[:, None] == (lane[None, :] ^ half)).astype(BF16)
    return cos, sin, perm


def _ret_fused_body(p_ref, cos_ref, sin_ref, perm_ref, o_ref, sf_ref, s_scr, *, nc):
    L, DK, DV, H = CHUNK, RET_QK, RET_V, RET_HEADS
    nq = H * DK
    c = pl.program_id(1)

    @pl.when(c == 0)
    def _():
        s_scr[...] = jnp.zeros_like(s_scr)

    cos = cos_ref[...]
    sin = sin_ref[...]
    perm = perm_ref[...]
    rot = lambda x: x * cos + _exact_rhs_dot(x, perm) * sin
    q_all = rot(p_ref[0, :, 0:nq])
    k_all = rot(p_ref[0, :, nq:2 * nq]) * (DK ** -0.5)
    row = lax.broadcasted_iota(jnp.int32, (L, L), 0)
    col = lax.broadcasted_iota(jnp.int32, (L, L), 1)
    rel = (row - col).astype(F32)
    idx = lax.broadcasted_iota(jnp.int32, (L, 1), 0).astype(F32)
    vs = lambda h: p_ref[0, :, 2 * nq + h * DV:2 * nq + (h + 1) * DV].astype(BF16)
    hd = [dict() for _ in range(H)]
    for h, t in enumerate(hd):
        lg = _ret_log_gamma(h)
        q = q_all[:, h * DK:(h + 1) * DK]
        k = k_all[:, h * DK:(h + 1) * DK]
        t['s_prev'] = s_scr[h]
        t['dmask'] = jnp.where(rel >= 0, jnp.exp(jnp.maximum(rel, 0.0) * lg), 0.0)
        t['qk'] = _dot_nt(q.astype(BF16), k.astype(BF16))
        q_dec = q * jnp.exp((idx + 1.0) * lg)
        t['qs'] = _dot(q_dec.astype(BF16), t['s_prev'].astype(BF16))
        k_end = k * jnp.exp((L - 1.0 - idx) * lg)
        t['kv'] = _dot_tn(k_end.astype(BF16), vs(h))
    for h, t in enumerate(hd):
        y = _dot((t['qk'] * t['dmask']).astype(BF16), vs(h)) + t['qs']
        y = y * lax.rsqrt(jnp.mean(y * y, -1, keepdims=True) + NORM_EPS)
        gate = p_ref[0, :, 2 * nq + RET_W + h * DV:2 * nq + RET_W + (h + 1) * DV]
        o_ref[0, :, h * DV:(h + 1) * DV] = (gate * jax.nn.sigmoid(gate) * y).astype(BF16)
    for h, t in enumerate(hd):
        s_scr[h] = math.exp(L * _ret_log_gamma(h)) * t['s_prev'] + t['kv']

    @pl.when(c == nc - 1)
    def _():
        sf_ref[0] = s_scr[...]


def _ret_prompt(p3, pos):
    b, t, _ = p3.shape
    L = CHUNK
    nc = t // L
    nq = RET_HEADS * RET_QK
    cos, sin, perm = _rotary_tables(pos, RET_HEADS, RET_QK)
    tab = pl.BlockSpec((L, nq), lambda i, j: (j, 0))
    st = pl.BlockSpec((1, RET_HEADS, RET_QK, RET_V), lambda i, j: (i, 0, 0, 0))
    return pl.pallas_call(
        functools.partial(_ret_fused_body, nc=nc),
        grid=(b, nc),
        in_specs=[pl.BlockSpec((1, L, P_MAIN), lambda i, j: (i, j, P_RET // P_MAIN)), tab, tab,
                  pl.BlockSpec((nq, nq), lambda i, j: (0, 0))],
        out_specs=[pl.BlockSpec((1, L, RET_W), lambda i, j: (i, j, 0)), st],
        out_shape=[jax.ShapeDtypeStruct((b, t, RET_W), BF16),
                   jax.ShapeDtypeStruct((b, RET_HEADS, RET_QK, RET_V), F32)],
        scratch_shapes=[pltpu.VMEM((RET_HEADS, RET_QK, RET_V), F32)],
        compiler_params=_cparams(("parallel", "arbitrary")),
        name="ret_fused",
    )(p3, cos, sin, perm)


ML_HPAD = 8


def _ml_fused_body(p_ref, gate_ref, bias_ref, norm_ref, o_ref, cf_ref, nf_ref, mf_ref, c_scr, n_scr, m_scr, *, nc):
    L, DK, DV, H = CHUNK, ML_QK, ML_V, ML_HEADS
    nq = H * DK
    ci = pl.program_id(1)

    @pl.when(ci == 0)
    def _():
        c_scr[...] = jnp.zeros_like(c_scr)
        n_scr[...] = jnp.zeros_like(n_scr)
        m_scr[...] = jnp.zeros_like(m_scr)

    row = lax.broadcasted_iota(jnp.int32, (L, L), 0)
    col = lax.broadcasted_iota(jnp.int32, (L, L), 1)
    causal = row >= col
    tri = causal.astype(BF16)
    capped = ML_GATE_CAP * jnp.tanh((gate_ref[0] + bias_ref[...]) * (1.0 / ML_GATE_CAP))
    lane = lax.broadcasted_iota(jnp.int32, (L, LANES), 1)
    g = jnp.where(lane < H, capped, jnp.where(lane < 2 * H, -_softplus(-capped), 0.0))
    cum = _exact_lhs_dot(tri, g)
    g_t = g.T
    cum_t = cum.T
    m_all = m_scr[...]
    n_all = n_scr[...]
    qs = lambda h: p_ref[0, :, h * DK:(h + 1) * DK]
    vs = lambda h: p_ref[0, :, 2 * nq + h * DV:2 * nq + (h + 1) * DV].astype(BF16)
    hd = [dict() for _ in range(H)]
    for h, t in enumerate(hd):
        q = qs(h)
        k = p_ref[0, :, nq + h * DK:nq + (h + 1) * DK] * (DK ** -0.5)
        ig_col = g[:, h:h + 1]
        ig_row = g_t[h:h + 1, :]
        b_col = cum[:, H + h:H + h + 1]
        b_row = cum_t[H + h:H + h + 1, :]
        b_tot = cum[L - 1:L, H + h:H + h + 1]
        m_prev = m_all[h:h + 1, 0:1]
        t['c_prev'] = c_scr[h]
        t['n_prev'] = n_all[h:h + 1, :]
        a_row = b_tot - b_row + ig_row
        a_col = b_tot - b_col + ig_col
        t['m_new'] = jnp.maximum(b_tot + m_prev, jnp.max(a_row, axis=1, keepdims=True))
        t['dec'] = jnp.exp(b_tot + m_prev - t['m_new'])
        kw = k * jnp.exp(a_col - t['m_new'])
        t['kv'] = _dot_tn(kw.astype(BF16), vs(h))
        t['kw_sum'] = jnp.sum(kw, axis=0, keepdims=True)
        dlog = jnp.where(causal, b_col - b_row + ig_row, -jnp.inf)
        inter = b_col + m_prev
        t['m_i'] = jnp.maximum(jnp.max(dlog, axis=1, keepdims=True), inter)
        t['e'] = jnp.exp(dlog - t['m_i'])
        t['sc'] = jnp.exp(inter - t['m_i'])
        t['qk'] = _dot_nt(q.astype(BF16), k.astype(BF16))
        t['qc'] = _dot(q.astype(BF16), t['c_prev'].astype(BF16))
        t['qn'] = jnp.sum(q * t['n_prev'], axis=1, keepdims=True)
    for h, t in enumerate(hd):
        s = t['qk'] * t['e']
        num = _dot(s.astype(BF16), vs(h)) + t['sc'] * t['qc']
        den = jnp.sum(s, axis=1, keepdims=True) + t['sc'] * t['qn']
        hid = num / jnp.maximum(jnp.abs(den), jnp.exp(-t['m_i']))
        hid = hid * lax.rsqrt(jnp.mean(hid * hid, -1, keepdims=True) + NORM_EPS)
        og = p_ref[0, :, 2 * nq + ML_W + h * DV:2 * nq + ML_W + (h + 1) * DV]
        o_ref[0, :, h * DV:(h + 1) * DV] = (jax.nn.sigmoid(og) * (hid * norm_ref[:, h * DV:(h + 1) * DV])).astype(BF16)
    for h, t in enumerate(hd):
        c_scr[h] = t['dec'] * t['c_prev'] + t['kv']
        n_scr[h:h + 1, :] = t['dec'] * t['n_prev'] + t['kw_sum']
        m_scr[h:h + 1, :] = jnp.broadcast_to(t['m_new'], (1, LANES))

    @pl.when(ci == nc - 1)
    def _():
        cf_ref[0] = c_scr[...]
        nf_ref[0] = n_scr[...]
        mf_ref[0] = m_scr[...]


def _ml_prompt(p3, lp):
    b, t, _ = p3.shape
    L = CHUNK
    nc = t // L
    bias = jnp.pad(jnp.concatenate([lp['ml_ib'], lp['ml_fb']]), (0, LANES - 2 * ML_HEADS)).reshape(1, LANES)
    vs = pl.BlockSpec((1, L, ML_W), lambda i, j: (i, j, 0))
    cs = pl.BlockSpec((1, ML_HEADS, ML_QK, ML_V), lambda i, j: (i, 0, 0, 0))
    ns = pl.BlockSpec((1, ML_HPAD, ML_QK), lambda i, j: (i, 0, 0))
    ms = pl.BlockSpec((1, ML_HPAD, LANES), lambda i, j: (i, 0, 0))
    o, c_f, n_f, m_f = pl.pallas_call(
        functools.partial(_ml_fused_body, nc=nc),
        grid=(b, nc),
        in_specs=[pl.BlockSpec((1, L, P_MAIN), lambda i, j: (i, j, P_ML // P_MAIN)),
                  pl.BlockSpec((1, L, LANES), lambda i, j: (i, j, P_GATE // LANES)),
                  pl.BlockSpec((1, LANES), lambda i, j: (0, 0)),
                  pl.BlockSpec((1, ML_W), lambda i, j: (0, 0))],
        out_specs=[vs, cs, ns, ms],
        out_shape=[jax.ShapeDtypeStruct((b, t, ML_W), BF16),
                   jax.ShapeDtypeStruct((b, ML_HEADS, ML_QK, ML_V), F32),
                   jax.ShapeDtypeStruct((b, ML_HPAD, ML_QK), F32),
                   jax.ShapeDtypeStruct((b, ML_HPAD, LANES), F32)],
        scratch_shapes=[pltpu.VMEM((ML_HEADS, ML_QK, ML_V), F32),
                        pltpu.VMEM((ML_HPAD, ML_QK), F32),
                        pltpu.VMEM((ML_HPAD, LANES), F32)],
        compiler_params=_cparams(("parallel", "arbitrary")),
        name="ml_fused",
    )(p3, p3, bias, lp['ml_norm'].reshape(1, ML_W))
    return o, c_f, n_f[:, :ML_HEADS], m_f[:, :ML_HEADS, 0]


def _to_cols(x):
    b, c = x.shape
    return x.reshape(b // DEC_TB, DEC_TB, c).transpose(0, 2, 1)


def _rwkv_step_body(w_ref, ap_ref, k_ref, kk_ref, r_ref, v_ref, s_ref, y_ref, so_ref):
    N = RW_HD
    v = v_ref[0]

    def sa_step(i, acc):
        return acc + kk_ref[0, pl.ds(i, 1), :] * s_ref[i]

    sa = lax.fori_loop(0, N, sa_step, jnp.zeros_like(v), unroll=8)

    def upd_step(i, y):
        row = lambda ref: ref[0, pl.ds(i, 1), :]
        s_new = row(w_ref) * s_ref[i] - row(ap_ref) * sa + row(k_ref) * v
        so_ref[i] = s_new
        return y + row(r_ref) * s_new

    y_ref[0] = lax.fori_loop(0, N, upd_step, jnp.zeros_like(v), unroll=8)


def _rwkv_step(wdec, ap, k, kk, r, v, s_all, layer):
    b, w = v.shape
    heads = lambda x: x.reshape(b, RW_HEADS, RW_HD).transpose(1, 2, 0)
    vec = pl.BlockSpec((1, RW_HD, b), lambda h: (h, 0, 0))
    y, s_new = pl.pallas_call(
        _rwkv_step_body,
        grid=(RW_HEADS,),
        in_specs=[vec] * 6 + [pl.BlockSpec((None, None, RW_HD, RW_HD, b), lambda h: (layer, h, 0, 0, 0))],
        out_specs=[vec, pl.BlockSpec((None, RW_HD, RW_HD, b), lambda h: (h, 0, 0, 0))],
        out_shape=[jax.ShapeDtypeStruct((RW_HEADS, RW_HD, b), F32),
                   jax.ShapeDtypeStruct((RW_HEADS, RW_HD, RW_HD, b), F32)],
        compiler_params=_cparams(("parallel",)),
        name="rwkv_step",
    )(heads(wdec), heads(ap), heads(k), heads(kk), heads(r), heads(v), s_all)
    return y.transpose(2, 0, 1).reshape(b, w), s_new


def _ret_step_body(qc_ref, kc_ref, v_ref, s_ref, y_ref, so_ref):
    DK, DV = RET_QK, RET_V
    for j in range(DEC_TB):
        for h in range(RET_HEADS):
            gamma = math.exp(_ret_log_gamma(h))
            q = qc_ref[0, h * DK:(h + 1) * DK, j:j + 1]
            k = kc_ref[0, h * DK:(h + 1) * DK, j:j + 1]
            v = v_ref[j:j + 1, h * DV:(h + 1) * DV]
            s = s_ref[j, h]
            qk = jnp.sum(q * k, axis=0, keepdims=True)
            y_ref[j:j + 1, h * DV:(h + 1) * DV] = qk * v + gamma * jnp.sum(q * s, axis=0, keepdims=True)
            so_ref[j, h] = gamma * s + k * v


def _ret_step(q, k, v, s_all, layer):
    b = q.shape[0]
    cols = pl.BlockSpec((1, RET_HEADS * RET_QK, DEC_TB), lambda i: (i, 0, 0))
    rows = pl.BlockSpec((DEC_TB, RET_W), lambda i: (i, 0))
    st = pl.BlockSpec((DEC_TB, RET_HEADS, RET_QK, RET_V), lambda i: (i, 0, 0, 0))
    st_in = pl.BlockSpec((None, DEC_TB, RET_HEADS, RET_QK, RET_V), lambda i: (layer, i, 0, 0, 0))
    return pl.pallas_call(
        _ret_step_body,
        grid=(b // DEC_TB,),
        in_specs=[cols, cols, rows, st_in],
        out_specs=[rows, st],
        out_shape=[jax.ShapeDtypeStruct((b, RET_W), F32), jax.ShapeDtypeStruct(s_all.shape[1:], F32)],
        compiler_params=_cparams(("parallel",)),
        name="ret_step",
    )(_to_cols(q), _to_cols(k), v, s_all)


def _ml_step_body(qc_ref, kc_ref, q_ref, k_ref, v_ref, ig_ref, lf_ref, c_ref, n_ref, m_ref,
                  h_ref, co_ref, no_ref, mo_ref):
    DK, DV = ML_QK, ML_V
    for j in range(DEC_TB):
        for h in range(ML_HEADS):
            ks = slice(h * DK, (h + 1) * DK)
            vs = slice(h * DV, (h + 1) * DV)
            q_col = qc_ref[0, ks, j:j + 1]
            k_col = kc_ref[0, ks, j:j + 1]
            q_row = q_ref[j:j + 1, ks]
            k_row = k_ref[j:j + 1, ks]
            v = v_ref[j:j + 1, vs]
            ig = ig_ref[j:j + 1, h:h + 1]
            lf = lf_ref[j:j + 1, h:h + 1]
            m_prev = m_ref[j:j + 1, h:h + 1]
            c_prev = c_ref[j, h]
            n_prev = n_ref[j, h:h + 1, :]
            m_new = jnp.maximum(lf + m_prev, ig)
            dec = jnp.exp(lf + m_prev - m_new)
            wgt = jnp.exp(ig - m_new)
            co_ref[j, h] = dec * c_prev + (k_col * wgt) * v
            no_ref[j, h:h + 1, :] = dec * n_prev + k_row * wgt
            mo_ref[j:j + 1, h:h + 1] = m_new
            s = jnp.sum(q_row * k_row, axis=1, keepdims=True) * wgt
            num = s * v + dec * jnp.sum(q_col * c_prev, axis=0, keepdims=True)
            den = s + dec * jnp.sum(q_row * n_prev, axis=1, keepdims=True)
            h_ref[j:j + 1, vs] = num / jnp.maximum(jnp.abs(den), jnp.exp(-m_new))


def _ml_step(q, k, v, ig, lf, c_all, layer, n0, m0):
    b = q.shape[0]
    cols = pl.BlockSpec((1, ML_HEADS * ML_QK, DEC_TB), lambda i: (i, 0, 0))
    qk_rows = pl.BlockSpec((DEC_TB, ML_HEADS * ML_QK), lambda i: (i, 0))
    rows = pl.BlockSpec((DEC_TB, ML_W), lambda i: (i, 0))
    sc = pl.BlockSpec((DEC_TB, ML_HEADS), lambda i: (i, 0))
    cs = pl.BlockSpec((DEC_TB, ML_HEADS, ML_QK, ML_V), lambda i: (i, 0, 0, 0))
    ns = pl.BlockSpec((DEC_TB, ML_HEADS, ML_QK), lambda i: (i, 0, 0))
    cs_in = pl.BlockSpec((None, DEC_TB, ML_HEADS, ML_QK, ML_V), lambda i: (layer, i, 0, 0, 0))
    return pl.pallas_call(
        _ml_step_body,
        grid=(b // DEC_TB,),
        in_specs=[cols, cols, qk_rows, qk_rows, rows, sc, sc, cs_in, ns, sc],
        out_specs=[rows, cs, ns, sc],
        out_shape=[jax.ShapeDtypeStruct((b, ML_W), F32), jax.ShapeDtypeStruct(c_all.shape[1:], F32),
                   jax.ShapeDtypeStruct(n0.shape, F32), jax.ShapeDtypeStruct(m0.shape, F32)],
        compiler_params=_cparams(("parallel",)),
        name="ml_step",
    )(_to_cols(q), _to_cols(k), q, k, v, ig, lf, c_all, n0, m0)


def _heads(a, h):
    return a.reshape(a.shape[:-1] + (h, a.shape[-1] // h))


def _shift_prev(p, prev_row):
    return jnp.concatenate([prev_row[:, None, :], p[:, :-1]], axis=1)


def _rotary(x, pos):
    half = x.shape[-1] // 2
    inv = ROPE_BASE ** (-jnp.arange(half, dtype=F32) / half)
    ang = pos.astype(F32)[:, None] * inv[None, :]
    cos = jnp.cos(ang)[None, :, None, :]
    sin = jnp.sin(ang)[None, :, None, :]
    x1, x2 = x[..., :half], x[..., half:]
    return jnp.concatenate([x1 * cos - x2 * sin, x1 * sin + x2 * cos], -1)


def _small_matmul(x, w):
    lead = x.shape[:-1]
    kdim, n = w.shape
    x2 = x.reshape(-1, kdim)
    m = x2.shape[0]
    kp = -(-kdim // LANES) * LANES
    npad = -(-n // LANES) * LANES
    x2 = jnp.pad(x2.astype(BF16), ((0, 0), (0, kp - kdim)))
    w2 = jnp.pad(w.astype(BF16), ((0, kp - kdim), (0, npad - n)))
    tm = 1024 if m % 1024 == 0 else m
    out = _matmul(x2, w2, tm, npad)
    return out[:, :n].reshape(lead + (n,))


def _mix_prompt(p, pos, v_first, lp):
    o_rw, v_first, s_new = _rwkv_prompt(p, lp, v_first, (1, 1, 1, 1))
    o_ret, r_new = _ret_prompt(p, pos)
    o_ml, c_new, n_new, m_new = _ml_prompt(p, lp)
    return (o_rw, o_ret, o_ml), v_first, (s_new, r_new, c_new, n_new, m_new)


def _mix_sample(p, pos, v_first, st, lp, prev_row):
    bsz, t_len, _ = p.shape

    p_rw = p[..., :RW_P]
    mixed = p_rw + (_shift_prev(p_rw, prev_row[:, :RW_P]) - p_rw) * lp['rw_mu']
    sizes = np.cumsum([RW_W, RW_W, RW_W, RW_LORA_W, RW_LORA_A, RW_LORA_G])[:-1]
    r, k, v, xw, xa, xg = jnp.split(mixed, [int(s) for s in sizes], axis=-1)
    w = -jax.nn.softplus(-(lp['rw_w0'] + _small_matmul(jnp.tanh(xw), lp['rw_w2']))) - 0.5
    a = jax.nn.sigmoid(lp['rw_a0'] + _small_matmul(xa, lp['rw_a2']))
    g = _small_matmul(jax.nn.sigmoid(xg), lp['rw_g2'])
    if v_first is None:
        v_first = v
    else:
        pv = p[..., P_V1:P_V1 + RW_LORA_V]
        xv = pv + (_shift_prev(pv, prev_row[:, P_V1:P_V1 + RW_LORA_V]) - pv) * lp['rw_vmu']
        v = v + (v_first - v) * jax.nn.sigmoid(lp['rw_v0'] + _small_matmul(xv, lp['rw_v2']))
    kk = _heads(k * lp['rw_kk'], RW_HEADS)
    kk = kk * lax.rsqrt(jnp.maximum(jnp.sum(jnp.square(kk), -1, keepdims=True), 1e-24))
    kk = kk.reshape(bsz, t_len, RW_W)
    k = k * (1.0 + (a - 1.0) * lp['rw_ka'])
    lw = -jnp.exp(w)
    y, s_new = _rwkv_step(jnp.exp(lw)[:, 0], (kk * a)[:, 0], k[:, 0], kk[:, 0], r[:, 0], v[:, 0],
                          st['rw_wkv_t'], st['layer'])
    y = _heads(y[:, None, :], RW_HEADS)
    y_mu = jnp.mean(y, -1, keepdims=True)
    y_var = jnp.mean(jnp.square(y - y_mu), -1, keepdims=True)
    y = ((y - y_mu) * lax.rsqrt(y_var + RW_GN_EPS)).reshape(bsz, t_len, RW_W)
    y = y * lp['rw_lnx_g'] + lp['rw_lnx_b']
    rh, kh, vh = (_heads(u, RW_HEADS) for u in (r, k, v))
    bonus = jnp.sum(rh * kh * _heads(lp['rw_rk'], RW_HEADS), -1, keepdims=True) * vh
    o_rw = ((y + bonus.reshape(bsz, t_len, RW_W)) * g).astype(BF16)

    nqk = RET_HEADS * RET_QK
    p_ret = p[..., P_RET:P_RET + P_MAIN]
    qr, kr, vr, gr = (p_ret[..., :nqk], p_ret[..., nqk:2 * nqk],
                      p_ret[..., 2 * nqk:2 * nqk + RET_W], p_ret[..., 2 * nqk + RET_W:])
    qh = _rotary(_heads(qr, RET_HEADS), pos).reshape(bsz, t_len, nqk)
    khr = (_rotary(_heads(kr, RET_HEADS), pos) * (RET_QK ** -0.5)).reshape(bsz, t_len, nqk)
    yr, r_new = _ret_step(qh[:, 0], khr[:, 0], vr[:, 0], st['ret_all'], st['layer'])
    yr = _heads(yr[:, None, :], RET_HEADS)
    yr = yr * lax.rsqrt(jnp.mean(jnp.square(yr), -1, keepdims=True) + NORM_EPS)
    o_ret = (jax.nn.silu(gr) * yr.reshape(bsz, t_len, RET_W)).astype(BF16)

    nqk = ML_HEADS * ML_QK
    p_ml = p[..., P_ML:P_ML + P_MAIN]
    qm, km, vm, om = (p_ml[..., :nqk], p_ml[..., nqk:2 * nqk],
                      p_ml[..., 2 * nqk:2 * nqk + ML_W], p_ml[..., 2 * nqk + ML_W:])
    im = p[..., P_GATE:P_GATE + ML_HEADS]
    fm = p[..., P_GATE + ML_HEADS:P_GATE + 2 * ML_HEADS]
    ig = ML_GATE_CAP * jnp.tanh((im + lp['ml_ib']) / ML_GATE_CAP)
    lf = jax.nn.log_sigmoid(ML_GATE_CAP * jnp.tanh((fm + lp['ml_fb']) / ML_GATE_CAP))
    km = km * (ML_QK ** -0.5)
    hm, c_new, n_new, m_new = _ml_step(qm[:, 0], km[:, 0], vm[:, 0], ig[:, 0], lf[:, 0],
                                       st['ml_c_all'], st['layer'], st['ml_n'], st['ml_m'])
    hm = _heads(hm[:, None, :], ML_HEADS)
    hm = hm * lax.rsqrt(jnp.mean(jnp.square(hm), -1, keepdims=True) + NORM_EPS)
    o_ml = (jax.nn.sigmoid(om) * (hm.reshape(bsz, t_len, ML_W) * lp['ml_norm'])).astype(BF16)

    return (o_rw, o_ret, o_ml), v_first, (s_new, r_new, c_new, n_new, m_new)


_T_V1 = P_V1 // LANES
_T_GATE = P_GATE // LANES
_T_RET = P_RET // LANES
_T_SHIFT = (P_RET - RW_P) // LANES
_T_SRC_GATE = (RW_P + RET_P + 2 * ML_HEADS * ML_QK + 2 * ML_W) // LANES


def _pack_body(w_ref, v1_ref, *o_refs):
    j = pl.program_id(0)
    row = lax.broadcasted_iota(jnp.int32, (LANES, 1), 0)
    spare = jnp.logical_and(j > _T_GATE, j < _T_RET)
    for l, o_ref in enumerate(o_refs):
        w = w_ref[:, l, :]
        gates = jnp.where(row < 2 * ML_HEADS, w, 0.0)
        out = jnp.where(j == _T_V1, v1_ref[l], jnp.where(j == _T_GATE, gates, jnp.where(spare, 0.0, w)))
        o_ref[...] = out.astype(BF16)


def _pack_w_in(w_in, rw_v1):
    depth, d, _ = w_in.shape
    w_t = w_in.transpose(2, 0, 1)
    v1_t = jnp.pad(rw_v1.transpose(0, 2, 1), ((1, 0), (0, LANES - RW_LORA_V), (0, 0)))

    def src_tile(j):
        return jnp.where(j < _T_V1, j, jnp.where(j == _T_GATE, _T_SRC_GATE, j - _T_SHIFT))

    return pl.pallas_call(
        _pack_body,
        grid=(P_PAD // LANES,),
        in_specs=[pl.BlockSpec((LANES, depth, d), lambda j: (src_tile(j), 0, 0)),
                  pl.BlockSpec((depth, LANES, d), lambda j: (0, 0, 0))],
        out_specs=[pl.BlockSpec((LANES, d), lambda j: (j, 0))] * depth,
        out_shape=[jax.ShapeDtypeStruct((P_PAD, d), BF16)] * depth,
        compiler_params=_cparams(("parallel",)),
        name="pack_w_in",
    )(w_t, v1_t)


def _token_tiles(m):
    if m % 2048 == 0:
        return 2048, 512, 512
    return m, m, m


def kernel(x_prompt, x_sample, state_rw_shift, state_rw_wkv, state_ret, state_ml_c, state_ml_n, state_ml_m,
           ln0_g, ln0_b, w_in, rw_mu, rw_w0, rw_w2, rw_a0, rw_a2, rw_g2, rw_kk, rw_ka, rw_rk,
           rw_lnx_g, rw_lnx_b, rw_v0, rw_v1, rw_vmu, rw_v2, ml_ib, ml_fb, ml_norm, w_out,
           ln1_g, ln1_b, w_gate, w_up, w_down, ln2_g, ln2_b):
    bp, tp, d = x_prompt.shape
    bs, ts, _ = x_sample.shape
    mp, ms = bp * tp, bs * ts
    pos_p = jnp.arange(tp)
    pos_s = PAST_LEN + jnp.arange(ts)
    tm_big, tm_out, tm_down = _token_tiles(mp)
    rw_wkv_t = state_rw_wkv.transpose(0, 2, 3, 4, 1)
    w_in_packed = _pack_w_in(w_in, rw_v1)
    xf_p, xb_p = _layernorm(x_prompt.reshape(mp, d), ln0_g, ln0_b, tm_out)
    xf_s, xb_s = _layernorm(x_sample.reshape(ms, d), ln0_g, ln0_b, ms)
    vf_p = vf_s = None
    outs_p, outs_s = [], []

    for l in range(DEPTH):
        lp = {
            'rw_mu': rw_mu[l], 'rw_w0': rw_w0[l], 'rw_w2': rw_w2[l], 'rw_a0': rw_a0[l], 'rw_a2': rw_a2[l],
            'rw_g2': rw_g2[l], 'rw_kk': rw_kk[l], 'rw_ka': rw_ka[l], 'rw_rk': rw_rk[l],
            'rw_lnx_g': rw_lnx_g[l], 'rw_lnx_b': rw_lnx_b[l], 'ml_ib': ml_ib[l], 'ml_fb': ml_fb[l],
            'ml_norm': ml_norm[l],
        }
        if l > 0:
            lp.update(rw_v0=rw_v0[l - 1], rw_vmu=rw_vmu[l - 1], rw_v2=rw_v2[l - 1])
        w_out_b = w_out[l].astype(BF16)
        w_down_b = w_down[l].astype(BF16)
        x_side = jnp.concatenate([xb_s, state_rw_shift[l].astype(BF16)], axis=0)
        p_p, p_side = _in_proj(xb_p, x_side, w_in_packed[l], tm_big, P_TN)
        o_p, vf_p, st_p = _mix_prompt(p_p.reshape(bp, tp, P_PAD), pos_p, vf_p, lp)
        st = {'rw_wkv_t': rw_wkv_t, 'layer': l, 'ret_all': state_ret, 'ml_c_all': state_ml_c,
              'ml_n': state_ml_n[l], 'ml_m': state_ml_m[l]}
        o_s, vf_s, st_s = _mix_sample(p_side[:ms].reshape(bs, ts, P_PAD), pos_s, vf_s, st, lp, p_side[ms:])
        outs_p.append((xf_p.reshape(bp, tp, d)[:, -1],) + st_p)
        outs_s.append((xf_s.reshape(bs, ts, d)[:, -1],) + st_s)
        flat = lambda o, m: tuple(u.reshape(m, u.shape[-1]) for u in o)
        x1f_p, x1b_p, x1f_s, x1b_s = _out_proj_ln(flat(o_p, mp), xf_p, flat(o_s, ms), xf_s, w_out_b,
                                                  ln1_g[l], ln1_b[l], tm_out)
        hdn_p, hdn_s = _matmul_swiglu(x1b_p, x1b_s, w_gate, w_up, l, tm_big, 512)
        xf_p, xb_p, xf_s, xb_s = _matmul_res_ln(hdn_p, hdn_s, w_down_b, x1f_p, x1f_s, ln2_g[l], ln2_b[l],
                                                tm_down, DOWN_TK)

    y_p = xf_p.reshape(bp, tp, d)
    y_s = xf_s.reshape(bs, ts, d)
    sp = [jnp.stack([o[i] for o in outs_p]) for i in range(6)]
    ss = [jnp.stack([o[i] for o in outs_s]) for i in range(6)]
    ss[1] = ss[1].transpose(0, 4, 1, 2, 3)
    return (y_p, y_s, sp[0], sp[1], sp[2], sp[3], sp[4], sp[5], ss[0], ss[1], ss[2], ss[3], ss[4], ss[5])
```

```python
import functools
import math

import numpy as np
import jax
import jax.numpy as jnp
from jax import lax
from jax.experimental import pallas as pl
from jax.experimental.pallas import tpu as pltpu

F32 = jnp.float32
BF16 = jnp.bfloat16

D_MODEL = 2048
DEPTH = 2
PAST_LEN = 16384
RW_HD = 64
RW_W = D_MODEL // 4
RW_HEADS = RW_W // RW_HD
RW_LORA_W = 64
RW_LORA_A = 64
RW_LORA_V = 32
RW_LORA_G = 128
RW_P = 3 * RW_W + RW_LORA_W + RW_LORA_A + RW_LORA_G
RW_GN_EPS = 64e-5
RET_V = 128
RET_QK = 64
RET_W = 3 * D_MODEL // 8
RET_HEADS = RET_W // RET_V
RET_P = 2 * RET_HEADS * RET_QK + 2 * RET_W
ML_V = 128
ML_QK = 64
ML_W = D_MODEL - RW_W - RET_W
ML_HEADS = ML_W // ML_V
ML_P = 2 * ML_HEADS * ML_QK + 2 * ML_W + 2 * ML_HEADS
ML_GATE_CAP = 15.0
P_TOTAL = RW_P + RET_P + ML_P
D_FF = ((8 * D_MODEL + 3 * 256 - 1) // (3 * 256)) * 256
CHUNK = 128
ROPE_BASE = 10000.0
LN_EPS = 1e-5
NORM_EPS = 1e-6
ALPHA = (2 * DEPTH) ** 0.25

LANES = 128
P_V1 = RW_P
P_GATE = RW_P + LANES
P_RET = 2304
P_ML = 2 * P_RET
P_MAIN = 2304
P_PAD = 3 * P_RET
P_TN = 768
RW_CHUNK = 64
DEC_TB = 8
VMEM_LIMIT = 56 * 1024 * 1024

HIGHEST = lax.Precision.HIGHEST


def _cparams(sem):
    return pltpu.CompilerParams(dimension_semantics=sem, vmem_limit_bytes=VMEM_LIMIT)


def _dot(a, b, precision=None):
    return lax.dot_general(a, b, (((1,), (0,)), ((), ())), precision=precision, preferred_element_type=F32)


def _dot_nt(a, b, precision=None):
    return lax.dot_general(a, b, (((1,), (1,)), ((), ())), precision=precision, preferred_element_type=F32)


def _dot_tn(a, b, precision=None):
    return lax.dot_general(a, b, (((0,), (0,)), ((), ())), precision=precision, preferred_element_type=F32)


def _ln_rows(x, g, b):
    mu = jnp.mean(x, -1, keepdims=True)
    xc = x - mu
    var = jnp.mean(xc * xc, -1, keepdims=True)
    return xc * lax.rsqrt(var + LN_EPS) * g + b


def _ln_body(x_ref, g_ref, b_ref, of_ref, ob_ref):
    y = _ln_rows(x_ref[...], g_ref[...], b_ref[...])
    of_ref[...] = y
    ob_ref[...] = y.astype(BF16)


def _layernorm(x, g, b, tm):
    m, d = x.shape
    return pl.pallas_call(
        _ln_body,
        grid=(m // tm,),
        in_specs=[pl.BlockSpec((tm, d), lambda i: (i, 0)),
                  pl.BlockSpec((1, d), lambda i: (0, 0)),
                  pl.BlockSpec((1, d), lambda i: (0, 0))],
        out_specs=[pl.BlockSpec((tm, d), lambda i: (i, 0)),
                   pl.BlockSpec((tm, d), lambda i: (i, 0))],
        out_shape=[jax.ShapeDtypeStruct((m, d), F32), jax.ShapeDtypeStruct((m, d), BF16)],
        compiler_params=_cparams(("parallel",)),
        name="layernorm",
    )(x, g.reshape(1, d), b.reshape(1, d))


def _mm_body(x_ref, w_ref, o_ref, *, w_transposed):
    dot = _dot_nt if w_transposed else _dot
    o_ref[...] = dot(x_ref[...], w_ref[...]).astype(o_ref.dtype)


def _matmul(x, w, tm, tn, out_dtype=F32, w_transposed=False):
    m, k = x.shape
    n = w.shape[0] if w_transposed else w.shape[1]
    w_spec = (pl.BlockSpec((tn, k), lambda i, j: (j, 0)) if w_transposed
              else pl.BlockSpec((k, tn), lambda i, j: (0, j)))
    return pl.pallas_call(
        functools.partial(_mm_body, w_transposed=w_transposed),
        grid=(m // tm, n // tn),
        in_specs=[pl.BlockSpec((tm, k), lambda i, j: (i, 0)), w_spec],
        out_specs=pl.BlockSpec((tm, tn), lambda i, j: (i, j)),
        out_shape=jax.ShapeDtypeStruct((m, n), out_dtype),
        compiler_params=_cparams(("parallel", "parallel")),
        name="matmul",
    )(x, w)


def _in_proj_body(x_ref, xs_ref, w_ref, o_ref, os_ref):
    @pl.when(pl.program_id(1) == 0)
    def _():
        os_ref[...] = _dot_nt(xs_ref[...], w_ref[...])

    o_ref[...] = _dot_nt(x_ref[...], w_ref[...])


def _in_proj(x, x_side, w_t, tm, tn):
    m, k = x.shape
    ms = x_side.shape[0]
    n = w_t.shape[0]
    return pl.pallas_call(
        _in_proj_body,
        grid=(n // tn, m // tm),
        in_specs=[pl.BlockSpec((tm, k), lambda j, i: (i, 0)), pl.BlockSpec((ms, k), lambda j, i: (0, 0)),
                  pl.BlockSpec((tn, k), lambda j, i: (j, 0))],
        out_specs=[pl.BlockSpec((tm, tn), lambda j, i: (i, j)), pl.BlockSpec((ms, tn), lambda j, i: (0, j))],
        out_shape=[jax.ShapeDtypeStruct((m, n), F32), jax.ShapeDtypeStruct((ms, n), F32)],
        compiler_params=_cparams(("parallel", "arbitrary")),
        name="in_proj",
    )(x, x_side, w_t)


def _swiglu_body(x_ref, xs_ref, wg_ref, wu_ref, o_ref, os_ref, wg_scr, wu_scr):
    def act(x):
        g = _dot(x, wg_scr[...])
        return (g * jax.nn.sigmoid(g) * _dot(x, wu_scr[...])).astype(BF16)

    @pl.when(pl.program_id(1) == 0)
    def _():
        wg_scr[...] = wg_ref[...].astype(BF16)
        wu_scr[...] = wu_ref[...].astype(BF16)
        os_ref[...] = act(xs_ref[...])

    o_ref[...] = act(x_ref[...])


def _matmul_swiglu(x, x_side, wg, wu, layer, tm, tn):
    m, k = x.shape
    ms = x_side.shape[0]
    n = wg.shape[2]
    w_spec = pl.BlockSpec((None, k, tn), lambda j, i: (layer, 0, j))
    return pl.pallas_call(
        _swiglu_body,
        grid=(n // tn, m // tm),
        in_specs=[pl.BlockSpec((tm, k), lambda j, i: (i, 0)), pl.BlockSpec((ms, k), lambda j, i: (0, 0)),
                  w_spec, w_spec],
        out_specs=[pl.BlockSpec((tm, tn), lambda j, i: (i, j)), pl.BlockSpec((ms, tn), lambda j, i: (0, j))],
        out_shape=[jax.ShapeDtypeStruct((m, n), BF16), jax.ShapeDtypeStruct((ms, n), BF16)],
        scratch_shapes=[pltpu.VMEM((k, tn), BF16), pltpu.VMEM((k, tn), BF16)],
        compiler_params=_cparams(("parallel", "arbitrary")),
        name="matmul_swiglu",
    )(x, x_side, wg, wu)


DOWN_TK = D_FF // 4
LN_ROWS = 256
OUT_ROWS = 256


def _res_ln_store(acc_ref, res_ref, g_ref, b_ref, of_ref, ob_ref, n_rows):
    step = min(LN_ROWS, n_rows)
    for r in range(0, n_rows, step):
        rows = pl.ds(r, step)
        y = _ln_rows(ALPHA * res_ref[rows, :] + acc_ref[rows, :], g_ref[...], b_ref[...])
        of_ref[rows, :] = y
        ob_ref[rows, :] = y.astype(BF16)


def _mm_res_ln_body(x_ref, xs_ref, w_ref, res_ref, ress_ref, g_ref, b_ref, of_ref, ob_ref, ofs_ref, obs_ref, *, nk, tm, ms):
    i = pl.program_id(0)
    kk = pl.program_id(1)

    @pl.when(kk == 0)
    def _():
        of_ref[...] = jnp.zeros_like(of_ref)

    of_ref[...] += _dot(x_ref[...], w_ref[...])

    @pl.when(kk == nk - 1)
    def _():
        _res_ln_store(of_ref, res_ref, g_ref, b_ref, of_ref, ob_ref, tm)

    @pl.when(i == 0)
    def _():
        @pl.when(kk == 0)
        def _():
            ofs_ref[...] = jnp.zeros_like(ofs_ref)

        ofs_ref[...] += _dot(xs_ref[...], w_ref[...])

        @pl.when(kk == nk - 1)
        def _():
            _res_ln_store(ofs_ref, ress_ref, g_ref, b_ref, ofs_ref, obs_ref, ms)


def _matmul_res_ln(x, x_side, w, res, res_side, g, b, tm, tk):
    m, k = x.shape
    ms = x_side.shape[0]
    n = w.shape[1]
    nk = k // tk
    const = lambda r: pl.BlockSpec((r, n), lambda i, j: (0, 0))
    main = pl.BlockSpec((tm, n), lambda i, j: (i, 0))
    return pl.pallas_call(
        functools.partial(_mm_res_ln_body, nk=nk, tm=tm, ms=ms),
        grid=(m // tm, nk),
        in_specs=[pl.BlockSpec((tm, tk), lambda i, j: (i, j)), pl.BlockSpec((ms, tk), lambda i, j: (0, j)),
                  pl.BlockSpec((tk, n), lambda i, j: (j, 0)), main, const(ms), const(1), const(1)],
        out_specs=[main, main, const(ms), const(ms)],
        out_shape=[jax.ShapeDtypeStruct((m, n), F32), jax.ShapeDtypeStruct((m, n), BF16),
                   jax.ShapeDtypeStruct((ms, n), F32), jax.ShapeDtypeStruct((ms, n), BF16)],
        compiler_params=_cparams(("arbitrary", "arbitrary")),
        name="matmul_res_ln",
    )(x, x_side, w, res, res_side, g.reshape(1, n), b.reshape(1, n))


def _out_proj_ln_body(o_rw_ref, o_ret_ref, o_ml_ref, res_ref, s_rw_ref, s_ret_ref, s_ml_ref, ress_ref,
                      w_ref, g_ref, b_ref, of_ref, ob_ref, ofs_ref, obs_ref, *, tm, ms):
    def project(rw_ref, ret_ref, ml_ref, r_ref, f_ref, h_ref, n_rows):
        step = min(OUT_ROWS, n_rows)
        for r in range(0, n_rows, step):
            rows = pl.ds(r, step)
            mix = (_dot(rw_ref[rows, :], w_ref[0:RW_W, :])
                   + _dot(ret_ref[rows, :], w_ref[RW_W:RW_W + RET_W, :])
                   + _dot(ml_ref[rows, :], w_ref[RW_W + RET_W:, :]))
            y = _ln_rows(ALPHA * r_ref[rows, :] + mix, g_ref[...], b_ref[...])
            f_ref[rows, :] = y
            h_ref[rows, :] = y.astype(BF16)

    @pl.when(pl.program_id(0) == 0)
    def _():
        project(s_rw_ref, s_ret_ref, s_ml_ref, ress_ref, ofs_ref, obs_ref, ms)

    project(o_rw_ref, o_ret_ref, o_ml_ref, res_ref, of_ref, ob_ref, tm)


def _out_proj_ln(o, res, o_side, res_side, w, g, b, tm):
    m = res.shape[0]
    ms = res_side.shape[0]
    n = w.shape[1]
    rows = lambda width: pl.BlockSpec((tm, width), lambda i: (i, 0))
    const = lambda r, c: pl.BlockSpec((r, c), lambda i: (0, 0))
    widths = (RW_W, RET_W, ML_W)
    return pl.pallas_call(
        functools.partial(_out_proj_ln_body, tm=tm, ms=ms),
        grid=(m // tm,),
        in_specs=([rows(c) for c in widths] + [rows(n)] + [const(ms, c) for c in widths] + [const(ms, n)]
                  + [const(D_MODEL, n), const(1, n), const(1, n)]),
        out_specs=[rows(n), rows(n), const(ms, n), const(ms, n)],
        out_shape=[jax.ShapeDtypeStruct((m, n), F32), jax.ShapeDtypeStruct((m, n), BF16),
                   jax.ShapeDtypeStruct((ms, n), F32), jax.ShapeDtypeStruct((ms, n), BF16)],
        compiler_params=_cparams(("arbitrary",)),
        name="out_proj_ln",
    )(*o, res, *o_side, res_side, w, g.reshape(1, n), b.reshape(1, n))


RW_TB = 256
RW_GH = 4
RW_GW = RW_GH * RW_HD
RW_VEC_ROWS = 8


def _split3(x):
    hi = x.astype(BF16)
    r1 = x - hi.astype(F32)
    mid = r1.astype(BF16)
    lo = (r1 - mid.astype(F32)).astype(BF16)
    return hi, mid, lo


def _mm(a, b, dims, passes):
    dg = lambda x, y: lax.dot_general(x, y, (dims, ((), ())), preferred_element_type=F32)
    if passes == 6:
        return lax.dot_general(a, b, (dims, ((), ())), precision=HIGHEST, preferred_element_type=F32)
    ah = a.astype(BF16)
    bh = b.astype(BF16)
    if passes == 1:
        return dg(ah, bh)
    al = (a - ah.astype(F32)).astype(BF16)
    bl = (b - bh.astype(F32)).astype(BF16)
    return dg(ah, bh) + (dg(ah, bl) + dg(al, bh))


_NN = ((1,), (0,))
_NT = ((1,), (1,))
_TN = ((0,), (0,))


def _exact_lhs_dot(a_bf16, b):
    hi, mid, lo = _split3(b)
    dg = lambda y: lax.dot_general(a_bf16, y, (_NN, ((), ())), preferred_element_type=F32)
    return dg(hi) + (dg(mid) + dg(lo))


def _exact_rhs_dot(a, b_bf16):
    hi, mid, lo = _split3(a)
    dg = lambda x: lax.dot_general(x, b_bf16, (_NN, ((), ())), preferred_element_type=F32)
    return dg(hi) + (dg(mid) + dg(lo))


def _seg_sum(a, seg_bf16):
    hi = a.astype(BF16)
    lo = (a - hi.astype(F32)).astype(BF16)
    dg = lambda x: lax.dot_general(x, seg_bf16, (_NN, ((), ())), preferred_element_type=F32)
    return dg(hi) + dg(lo)


def _rw_scan(r, lw, k, v, kk, a, st_scr, y_scr, passes):
    L, TB, G = RW_CHUNK, RW_TB, RW_GW
    p_sc, p_inv, p_app, p_st = passes
    row = lax.broadcasted_iota(jnp.int32, (L, G), 0)
    col = lax.broadcasted_iota(jnp.int32, (L, G), 1) & (L - 1)
    strict, lower, eye = row > col, row >= col, (row == col).astype(F32)
    rg = lax.broadcasted_iota(jnp.int32, (G, G), 0) // RW_HD
    cg = lax.broadcasted_iota(jnp.int32, (G, G), 1) // RW_HD
    mask_bd = rg == cg
    rt = lax.broadcasted_iota(jnp.int32, (TB, TB), 0)
    ct = lax.broadcasted_iota(jnp.int32, (TB, TB), 1)
    tri = jnp.logical_and(rt >= ct, rt // L == ct // L).astype(BF16)
    bd = lambda x: jnp.where(mask_bd, jnp.concatenate([x] * RW_GH, axis=0), 0.0)
    blocks = [(s, g) for s in range(TB // L) for g in range(RW_HEADS // RW_GH)]
    cut = lambda x, b: x[b[0] * L:(b[0] + 1) * L, b[1] * G:(b[1] + 1) * G]

    cum = _exact_lhs_dot(tri, lw)
    e_neg = jnp.exp(-cum)
    ap = kk * a
    ap_h = ap * e_neg
    k_h = k * e_neg
    kk_t = kk * jnp.exp(cum - lw)
    r_t = r * jnp.exp(cum)

    lhs, n_m, m_a, m_kr = {}, {}, {}, {}
    for b in blocks:
        lhs[b] = jnp.concatenate([cut(kk_t, b), cut(r_t, b)], axis=0)
        sc_a = _mm(lhs[b], bd(cut(ap_h, b)), _NT, p_sc)
        sc_k = _mm(lhs[b], bd(cut(k_h, b)), _NT, p_sc)
        n_m[b] = jnp.where(strict, sc_a[:L], 0.0)
        m_a[b] = jnp.where(lower, sc_a[L:], 0.0)
        m_kr[b] = jnp.concatenate([jnp.where(strict, sc_k[:L], 0.0), jnp.where(lower, sc_k[L:], 0.0)], axis=0)
    inv = {b: eye - n_m[b] for b in blocks}
    pw = {b: _mm(n_m[b], bd(n_m[b]), _NN, p_inv) for b in blocks}
    n_iter = int(math.log2(L)) - 1
    for j in range(n_iter):
        last = j == n_iter - 1
        for b in blocks:
            lhs_j = inv[b] if last else jnp.concatenate([inv[b], pw[b]], axis=0)
            prod = _mm(lhs_j, bd(pw[b]), _NN, p_inv)
            inv[b] = inv[b] + prod[:L]
            if not last:
                pw[b] = prod[L:]
    mv, vk, a_end, decay = {}, {}, {}, {}
    for b in blocks:
        s, g = b
        tot = cum[(s + 1) * L - 1:(s + 1) * L, g * G:(g + 1) * G]
        e_end = jnp.exp(tot - cut(cum, b))
        mv[b] = _mm(m_kr[b], bd(cut(v, b)), _NN, p_app)
        vk[b] = _mm(cut(v, b), cut(k, b) * e_end, _TN, p_st)
        a_end[b] = cut(ap, b) * e_end
        decay[b] = jnp.exp(tot)

    for b in blocks:
        s, g = b
        st = st_scr[g]
        s_terms = _mm(lhs[b], st, _NT, p_app)
        u = _mm(inv[b], bd(s_terms[:L] + mv[b][:L]), _NN, p_app)
        y_scr[s * L:(s + 1) * L, g * G:(g + 1) * G] = s_terms[L:] + mv[b][L:] - _mm(m_a[b], bd(u), _NN, p_app)
        st_scr[g] = jnp.where(mask_bd, st * decay[b] + vk[b] - _mm(u, a_end[b], _TN, p_st), 0.0)


def _softplus(z):
    return jnp.maximum(z, 0.0) + jnp.log(1.0 + jnp.exp(-jnp.abs(z)))


def _rwkv_fused_body(*refs, nc, has_vres, passes):
    if has_vres:
        (p_ref, pv_ref, vf_ref, mu_ref, vec_ref, wa_ref, g2_ref, seg_ref, vmu_ref, v2_ref,
         o_ref, sf_ref, st_scr, prev_scr, y_scr, prevv_scr) = refs
    else:
        (p_ref, mu_ref, vec_ref, wa_ref, g2_ref, seg_ref,
         o_ref, vfo_ref, sf_ref, st_scr, prev_scr, y_scr) = refs
    TB, W, L = RW_TB, RW_W, RW_CHUNK
    c = pl.program_id(1)

    @pl.when(c == 0)
    def _():
        st_scr[...] = jnp.zeros_like(st_scr)
        prev_scr[...] = jnp.zeros_like(prev_scr)
        if has_vres:
            prevv_scr[...] = jnp.zeros_like(prevv_scr)

    first_row = lax.broadcasted_iota(jnp.int32, (TB, 1), 0) == 0

    def shift_mix(x, carry_ref, mu):
        prev = jnp.where(first_row, carry_ref[...], pltpu.roll(x, 1, 0))
        carry_ref[...] = x[TB - 1:TB, :]
        return x + (prev - x) * mu

    mixed = shift_mix(p_ref[0], prev_scr, mu_ref[...])
    r = mixed[:, 0:W]
    k = mixed[:, W:2 * W]
    v = mixed[:, 2 * W:3 * W]
    xwa = mixed[:, 3 * W:3 * W + LANES]
    xg = mixed[:, 3 * W + LANES:3 * W + 2 * LANES]
    vec = vec_ref[...]
    w0, a0, kk_s, ka, rk, lnx_g, lnx_b, v0 = (vec[i:i + 1, :] for i in range(RW_VEC_ROWS))
    seg = seg_ref[...]
    wa = wa_ref[...]
    w_lora = _dot(jnp.tanh(xwa).astype(BF16), wa[:, 0:W])
    a_lora = _dot(xwa.astype(BF16), wa[:, W:2 * W])
    lw = -jnp.exp(-_softplus(-(w0 + w_lora)) - 0.5)
    a = jax.nn.sigmoid(a0 + a_lora)
    g = _dot(jax.nn.sigmoid(xg).astype(BF16), g2_ref[...])
    if has_vres:
        xv = shift_mix(pv_ref[0], prevv_scr, vmu_ref[...])
        v = v + (vf_ref[0] - v) * jax.nn.sigmoid(v0 + _dot(xv.astype(BF16), v2_ref[...]))
    else:
        vfo_ref[0] = v
    kk = k * kk_s
    kk = kk * lax.rsqrt(jnp.maximum(_seg_sum(kk * kk, seg), 1e-24))
    k = k * (1.0 + (a - 1.0) * ka)

    _rw_scan(r, lw, k, v, kk, a, st_scr, y_scr, passes)

    y = y_scr[...]
    inv_n = 1.0 / RW_HD
    y_mu = _seg_sum(y, seg) * inv_n
    yc = y - y_mu
    y_var = _seg_sum(yc * yc, seg) * inv_n
    y = yc * lax.rsqrt(y_var + RW_GN_EPS) * lnx_g + lnx_b
    bonus = _seg_sum(r * k * rk, seg) * v
    o_ref[0] = ((y + bonus) * g).astype(BF16)

    @pl.when(c == nc - 1)
    def _():
        sf_ref[0] = st_scr[...]


def _rwkv_prompt(p3, lp, v_first, passes=(3, 3, 3, 3)):
    b, t, _ = p3.shape
    nc = t // RW_TB
    has_vres = v_first is not None
    ng = RW_HEADS // RW_GH
    zpad = jnp.zeros((RW_LORA_W, RW_W), F32)
    wa = jnp.concatenate([jnp.concatenate([lp['rw_w2'], zpad], 0), jnp.concatenate([zpad, lp['rw_a2']], 0)], 1)
    vec = jnp.stack([lp['rw_w0'], lp['rw_a0'], lp['rw_kk'], lp['rw_ka'], lp['rw_rk'], lp['rw_lnx_g'], lp['rw_lnx_b'],
                     lp['rw_v0'] if has_vres else jnp.zeros((RW_W,), F32)])
    hid = jnp.arange(RW_W) // RW_HD
    seg = (hid[:, None] == hid[None, :]).astype(BF16)
    full = lambda shape: pl.BlockSpec(shape, lambda i, j: (0,) * len(shape))
    seq = lambda w, blk: pl.BlockSpec((1, RW_TB, w), lambda i, j: (i, j, blk))
    in_specs = [seq(RW_P, 0)]
    args = [p3]
    if has_vres:
        in_specs += [seq(LANES, P_V1 // LANES), seq(RW_W, 0)]
        args += [p3, v_first]
    in_specs += [full((1, RW_P)), full((RW_VEC_ROWS, RW_W)), full((LANES, 2 * RW_W)), full((RW_LORA_G, RW_W)),
                 full((RW_W, RW_W))]
    args += [lp['rw_mu'].reshape(1, RW_P), vec, wa.astype(BF16), lp['rw_g2'].astype(BF16), seg]
    if has_vres:
        in_specs += [full((1, LANES)), full((LANES, RW_W))]
        args += [jnp.pad(lp['rw_vmu'], (0, LANES - RW_LORA_V)).reshape(1, LANES),
                 jnp.pad(lp['rw_v2'], ((0, LANES - RW_LORA_V), (0, 0))).astype(BF16)]
    out_specs = [seq(RW_W, 0)]
    out_shape = [jax.ShapeDtypeStruct((b, t, RW_W), BF16)]
    if not has_vres:
        out_specs.append(seq(RW_W, 0))
        out_shape.append(jax.ShapeDtypeStruct((b, t, RW_W), F32))
    out_specs.append(pl.BlockSpec((1, ng, RW_GW, RW_GW), lambda i, j: (i, 0, 0, 0)))
    out_shape.append(jax.ShapeDtypeStruct((b, ng, RW_GW, RW_GW), F32))
    scratch = [pltpu.VMEM((ng, RW_GW, RW_GW), F32), pltpu.VMEM((1, RW_P), F32), pltpu.VMEM((RW_TB, RW_W), F32)]
    if has_vres:
        scratch.append(pltpu.VMEM((1, LANES), F32))
    outs = pl.pallas_call(
        functools.partial(_rwkv_fused_body, nc=nc, has_vres=has_vres, passes=passes),
        grid=(b, nc),
        in_specs=in_specs,
        out_specs=out_specs,
        out_shape=out_shape,
        scratch_shapes=scratch,
        compiler_params=_cparams(("parallel", "arbitrary")),
        name="rwkv_fused",
    )(*args)
    if has_vres:
        o, st_bd = outs
    else:
        o, v_first, st_bd = outs
    st5 = st_bd.reshape(b, ng, RW_GH, RW_HD, RW_GH, RW_HD)
    s_fin = jnp.stack([st5[:, :, h, :, h, :] for h in range(RW_GH)], axis=2)
    s_fin = s_fin.reshape(b, RW_HEADS, RW_HD, RW_HD).transpose(0, 1, 3, 2)
    return o, v_first, s_fin


def _ret_log_gamma(h):
    return math.log1p(-(2.0 ** (-5.0 - h)))


def _rotary_tables(pos, heads, dk):
    half = dk // 2
    inv = ROPE_BASE ** (-jnp.arange(half, dtype=F32) / half)
    ang = pos.astype(F32)[:, None] * inv[None, :]
    cos = jnp.tile(jnp.concatenate([jnp.cos(ang), jnp.cos(ang)], -1), (1, heads))
    sin = jnp.tile(jnp.concatenate([-jnp.sin(ang), jnp.sin(ang)], -1), (1, heads))
    lane = jnp.arange(heads * dk)
    perm = (lane[:, None] == (lane[None, :] ^ half)).astype(BF16)
    return cos, sin, perm


def _ret_fused_body(p_ref, cos_ref, sin_ref, perm_ref, o_ref, sf_ref, s_scr, *, nc):
    L, DK, DV, H = CHUNK, RET_QK, RET_V, RET_HEADS
    nq = H * DK
    c = pl.program_id(1)

    @pl.when(c == 0)
    def _():
        s_scr[...] = jnp.zeros_like(s_scr)

    cos = cos_ref[...]
    sin = sin_ref[...]
    perm = perm_ref[...]
    rot = lambda x: x * cos + _exact_rhs_dot(x, perm) * sin
    q_all = rot(p_ref[0, :, 0:nq])
    k_all = rot(p_ref[0, :, nq:2 * nq]) * (DK ** -0.5)
    row = lax.broadcasted_iota(jnp.int32, (L, L), 0)
    col = lax.broadcasted_iota(jnp.int32, (L, L), 1)
    rel = (row - col).astype(F32)
    idx = lax.broadcasted_iota(jnp.int32, (L, 1), 0).astype(F32)
    vs = lambda h: p_ref[0, :, 2 * nq + h * DV:2 * nq + (h + 1) * DV].astype(BF16)
    mean_w = jnp.full((DV, DV), 1.0 / DV, BF16)
    hd = [dict() for _ in range(H)]
    for h, t in enumerate(hd):
        lg = _ret_log_gamma(h)
        q = q_all[:, h * DK:(h + 1) * DK]
        k = k_all[:, h * DK:(h + 1) * DK]
        t['s_prev'] = s_scr[h]
        t['dmask'] = jnp.where(rel >= 0, jnp.exp(jnp.maximum(rel, 0.0) * lg), 0.0)
        t['qk'] = _dot_nt(q.astype(BF16), k.astype(BF16))
        q_dec = q * jnp.exp((idx + 1.0) * lg)
        t['qs'] = _dot(q_dec.astype(BF16), t['s_prev'].astype(BF16))
        k_end = k * jnp.exp((L - 1.0 - idx) * lg)
        t['kv'] = _dot_tn(k_end.astype(BF16), vs(h))
    for h, t in enumerate(hd):
        y = _dot((t['qk'] * t['dmask']).astype(BF16), vs(h)) + t['qs']
        y = y * lax.rsqrt(_seg_sum(y * y, mean_w) + NORM_EPS)
        gate = p_ref[0, :, 2 * nq + RET_W + h * DV:2 * nq + RET_W + (h + 1) * DV]
        o_ref[0, :, h * DV:(h + 1) * DV] = (gate * jax.nn.sigmoid(gate) * y).astype(BF16)
    for h, t in enumerate(hd):
        s_scr[h] = math.exp(L * _ret_log_gamma(h)) * t['s_prev'] + t['kv']

    @pl.when(c == nc - 1)
    def _():
        sf_ref[0] = s_scr[...]


def _ret_prompt(p3, pos):
    b, t, _ = p3.shape
    L = CHUNK
    nc = t // L
    nq = RET_HEADS * RET_QK
    cos, sin, perm = _rotary_tables(pos, RET_HEADS, RET_QK)
    tab = pl.BlockSpec((L, nq), lambda i, j: (j, 0))
    st = pl.BlockSpec((1, RET_HEADS, RET_QK, RET_V), lambda i, j: (i, 0, 0, 0))
    return pl.pallas_call(
        functools.partial(_ret_fused_body, nc=nc),
        grid=(b, nc),
        in_specs=[pl.BlockSpec((1, L, P_MAIN), lambda i, j: (i, j, P_RET // P_MAIN)), tab, tab,
                  pl.BlockSpec((nq, nq), lambda i, j: (0, 0))],
        out_specs=[pl.BlockSpec((1, L, RET_W), lambda i, j: (i, j, 0)), st],
        out_shape=[jax.ShapeDtypeStruct((b, t, RET_W), BF16),
                   jax.ShapeDtypeStruct((b, RET_HEADS, RET_QK, RET_V), F32)],
        scratch_shapes=[pltpu.VMEM((RET_HEADS, RET_QK, RET_V), F32)],
        compiler_params=_cparams(("parallel", "arbitrary")),
        name="ret_fused",
    )(p3, cos, sin, perm)


ML_HPAD = 8


def _ml_fused_body(p_ref, gate_ref, bias_ref, norm_ref, sel_ref, o_ref, cf_ref, nf_ref, mf_ref,
                   c_scr, n_scr, m_scr, *, nc):
    L, DK, DV, H = CHUNK, ML_QK, ML_V, ML_HEADS
    nq = H * DK
    ci = pl.program_id(1)

    @pl.when(ci == 0)
    def _():
        c_scr[...] = jnp.zeros_like(c_scr)
        n_scr[...] = jnp.zeros_like(n_scr)
        m_scr[...] = jnp.zeros_like(m_scr)

    row = lax.broadcasted_iota(jnp.int32, (L, L), 0)
    col = lax.broadcasted_iota(jnp.int32, (L, L), 1)
    causal = row >= col
    tri = causal.astype(BF16)
    capped = ML_GATE_CAP * jnp.tanh((gate_ref[0] + bias_ref[...]) * (1.0 / ML_GATE_CAP))
    lane = lax.broadcasted_iota(jnp.int32, (L, LANES), 1)
    g = jnp.where(lane < H, capped, jnp.where(lane < 2 * H, -_softplus(-capped), 0.0))
    g_rep = _exact_rhs_dot(g, sel_ref[...])
    b_rep_all = _exact_lhs_dot(tri, g_rep[:, H * LANES:])
    g_t = g.T
    cum_t = _exact_lhs_dot(tri, g).T
    cm_all = g_rep[:, :H * LANES] - b_rep_all
    row_id = lax.broadcasted_iota(jnp.int32, (L, 1), 0)
    shift = 1
    while shift < L:
        cm_all = jnp.maximum(cm_all, jnp.where(row_id >= shift, pltpu.roll(cm_all, shift, 0), -jnp.inf))
        shift *= 2
    ones = jnp.ones((L, LANES), BF16)
    mean_w = jnp.full((DV, LANES), 1.0 / DV, BF16)
    m_all = m_scr[...]
    hd = [dict() for _ in range(H)]
    for h, t in enumerate(hd):
        hs = slice(h * LANES, (h + 1) * LANES)
        q = p_ref[0, :, h * DK:(h + 1) * DK].astype(BF16)
        k = p_ref[0, :, nq + h * DK:nq + (h + 1) * DK] * (DK ** -0.5)
        t['v1'] = jnp.concatenate([p_ref[0, :, 2 * nq + h * DV:2 * nq + (h + 1) * DV].astype(BF16), ones], axis=1)
        ig_rep = g_rep[:, hs]
        b_rep = b_rep_all[:, hs]
        ig_row = g_t[h:h + 1, :]
        b_row = cum_t[H + h:H + h + 1, :]
        b_tot = b_rep[L - 1:L, :]
        m_prev = m_all[h:h + 1, :]
        t['c_prev'] = c_scr[h]
        t['n_prev'] = n_scr[h]
        t['m_new'] = jnp.maximum(b_tot + m_prev, jnp.max(b_tot - b_rep + ig_rep, axis=0, keepdims=True))
        t['dec'] = jnp.exp(b_tot + m_prev - t['m_new'])
        kw = k * jnp.exp((b_tot - b_rep + ig_rep - t['m_new'])[:, :DK])
        t['kvn'] = _dot_tn(kw.astype(BF16), t['v1'])
        inter = b_rep + m_prev
        t['m_i'] = b_rep + jnp.maximum(cm_all[:, hs], m_prev)
        t['e'] = jnp.exp(jnp.where(causal, (b_rep - t['m_i']) - b_row + ig_row, -jnp.inf))
        t['sc'] = jnp.exp(inter - t['m_i'])
        t['qk'] = _dot_nt(q, k.astype(BF16))
        cn = jnp.concatenate([t['c_prev'], t['n_prev']], axis=1).astype(BF16)
        t['qcn'] = _dot(q, cn)
    for h, t in enumerate(hd):
        nd = _dot((t['qk'] * t['e']).astype(BF16), t['v1'])
        num = nd[:, :DV] + t['sc'] * t['qcn'][:, :DV]
        den = nd[:, DV:] + t['sc'] * t['qcn'][:, DV:]
        hid = num / jnp.maximum(jnp.abs(den), jnp.exp(-t['m_i']))
        hid = hid * lax.rsqrt(_seg_sum(hid * hid, mean_w) + NORM_EPS)
        og = p_ref[0, :, 2 * nq + ML_W + h * DV:2 * nq + ML_W + (h + 1) * DV]
        o_ref[0, :, h * DV:(h + 1) * DV] = (jax.nn.sigmoid(og) * (hid * norm_ref[:, h * DV:(h + 1) * DV])).astype(BF16)
    for h, t in enumerate(hd):
        c_scr[h] = t['dec'] * t['c_prev'] + t['kvn'][:, :DV]
        n_scr[h] = t['dec'] * t['n_prev'] + t['kvn'][:, DV:]
        m_scr[h:h + 1, :] = t['m_new']

    @pl.when(ci == nc - 1)
    def _():
        cf_ref[0] = c_scr[...]
        nf_ref[0] = n_scr[...]
        mf_ref[0] = m_scr[...]


def _ml_prompt(p3, lp):
    b, t, _ = p3.shape
    L = CHUNK
    nc = t // L
    bias = jnp.pad(jnp.concatenate([lp['ml_ib'], lp['ml_fb']]), (0, LANES - 2 * ML_HEADS)).reshape(1, LANES)
    n_rep = 2 * ML_HEADS * LANES
    sel = (jnp.arange(LANES)[:, None] == jnp.arange(n_rep)[None, :] // LANES).astype(BF16)
    vs = pl.BlockSpec((1, L, ML_W), lambda i, j: (i, j, 0))
    cs = pl.BlockSpec((1, ML_HEADS, ML_QK, ML_V), lambda i, j: (i, 0, 0, 0))
    ns = pl.BlockSpec((1, ML_HEADS, ML_QK, LANES), lambda i, j: (i, 0, 0, 0))
    ms = pl.BlockSpec((1, ML_HPAD, LANES), lambda i, j: (i, 0, 0))
    o, c_f, n_f, m_f = pl.pallas_call(
        functools.partial(_ml_fused_body, nc=nc),
        grid=(b, nc),
        in_specs=[pl.BlockSpec((1, L, P_MAIN), lambda i, j: (i, j, P_ML // P_MAIN)),
                  pl.BlockSpec((1, L, LANES), lambda i, j: (i, j, P_GATE // LANES)),
                  pl.BlockSpec((1, LANES), lambda i, j: (0, 0)),
                  pl.BlockSpec((1, ML_W), lambda i, j: (0, 0)),
                  pl.BlockSpec((LANES, n_rep), lambda i, j: (0, 0))],
        out_specs=[vs, cs, ns, ms],
        out_shape=[jax.ShapeDtypeStruct((b, t, ML_W), BF16),
                   jax.ShapeDtypeStruct((b, ML_HEADS, ML_QK, ML_V), F32),
                   jax.ShapeDtypeStruct((b, ML_HEADS, ML_QK, LANES), F32),
                   jax.ShapeDtypeStruct((b, ML_HPAD, LANES), F32)],
        scratch_shapes=[pltpu.VMEM((ML_HEADS, ML_QK, ML_V), F32),
                        pltpu.VMEM((ML_HEADS, ML_QK, LANES), F32),
                        pltpu.VMEM((ML_HPAD, LANES), F32)],
        compiler_params=_cparams(("parallel", "arbitrary")),
        name="ml_fused",
    )(p3, p3, bias, lp['ml_norm'].reshape(1, ML_W), sel)
    return o, c_f, n_f[..., 0], m_f[:, :ML_HEADS, 0]


def _to_cols(x):
    b, c = x.shape
    return x.reshape(b // DEC_TB, DEC_TB, c).transpose(0, 2, 1)


def _rwkv_step_body(w_ref, ap_ref, k_ref, kk_ref, r_ref, v_ref, s_ref, y_ref, so_ref):
    N = RW_HD
    v = v_ref[0]

    def sa_step(i, acc):
        return acc + kk_ref[0, pl.ds(i, 1), :] * s_ref[i]

    sa = lax.fori_loop(0, N, sa_step, jnp.zeros_like(v), unroll=8)

    def upd_step(i, y):
        row = lambda ref: ref[0, pl.ds(i, 1), :]
        s_new = row(w_ref) * s_ref[i] - row(ap_ref) * sa + row(k_ref) * v
        so_ref[i] = s_new
        return y + row(r_ref) * s_new

    y_ref[0] = lax.fori_loop(0, N, upd_step, jnp.zeros_like(v), unroll=8)


def _rwkv_step(wdec, ap, k, kk, r, v, s_all, layer):
    b, w = v.shape
    heads = lambda x: x.reshape(b, RW_HEADS, RW_HD).transpose(1, 2, 0)
    vec = pl.BlockSpec((1, RW_HD, b), lambda h: (h, 0, 0))
    y, s_new = pl.pallas_call(
        _rwkv_step_body,
        grid=(RW_HEADS,),
        in_specs=[vec] * 6 + [pl.BlockSpec((None, None, RW_HD, RW_HD, b), lambda h: (layer, h, 0, 0, 0))],
        out_specs=[vec, pl.BlockSpec((None, RW_HD, RW_HD, b), lambda h: (h, 0, 0, 0))],
        out_shape=[jax.ShapeDtypeStruct((RW_HEADS, RW_HD, b), F32),
                   jax.ShapeDtypeStruct((RW_HEADS, RW_HD, RW_HD, b), F32)],
        compiler_params=_cparams(("parallel",)),
        name="rwkv_step",
    )(heads(wdec), heads(ap), heads(k), heads(kk), heads(r), heads(v), s_all)
    return y.transpose(2, 0, 1).reshape(b, w), s_new


def _ret_step_body(qc_ref, kc_ref, v_ref, s_ref, y_ref, so_ref):
    DK, DV = RET_QK, RET_V
    for j in range(DEC_TB):
        for h in range(RET_HEADS):
            gamma = math.exp(_ret_log_gamma(h))
            q = qc_ref[0, h * DK:(h + 1) * DK, j:j + 1]
            k = kc_ref[0, h * DK:(h + 1) * DK, j:j + 1]
            v = v_ref[j:j + 1, h * DV:(h + 1) * DV]
            s = s_ref[j, h]
            qk = jnp.sum(q * k, axis=0, keepdims=True)
            y_ref[j:j + 1, h * DV:(h + 1) * DV] = qk * v + gamma * jnp.sum(q * s, axis=0, keepdims=True)
            so_ref[j, h] = gamma * s + k * v


def _ret_step(q, k, v, s_all, layer):
    b = q.shape[0]
    cols = pl.BlockSpec((1, RET_HEADS * RET_QK, DEC_TB), lambda i: (i, 0, 0))
    rows = pl.BlockSpec((DEC_TB, RET_W), lambda i: (i, 0))
    st = pl.BlockSpec((DEC_TB, RET_HEADS, RET_QK, RET_V), lambda i: (i, 0, 0, 0))
    st_in = pl.BlockSpec((None, DEC_TB, RET_HEADS, RET_QK, RET_V), lambda i: (layer, i, 0, 0, 0))
    return pl.pallas_call(
        _ret_step_body,
        grid=(b // DEC_TB,),
        in_specs=[cols, cols, rows, st_in],
        out_specs=[rows, st],
        out_shape=[jax.ShapeDtypeStruct((b, RET_W), F32), jax.ShapeDtypeStruct(s_all.shape[1:], F32)],
        compiler_params=_cparams(("parallel",)),
        name="ret_step",
    )(_to_cols(q), _to_cols(k), v, s_all)


def _ml_step_body(qc_ref, kc_ref, q_ref, k_ref, v_ref, ig_ref, lf_ref, c_ref, n_ref, m_ref,
                  h_ref, co_ref, no_ref, mo_ref):
    DK, DV = ML_QK, ML_V
    for j in range(DEC_TB):
        for h in range(ML_HEADS):
            ks = slice(h * DK, (h + 1) * DK)
            vs = slice(h * DV, (h + 1) * DV)
            q_col = qc_ref[0, ks, j:j + 1]
            k_col = kc_ref[0, ks, j:j + 1]
            q_row = q_ref[j:j + 1, ks]
            k_row = k_ref[j:j + 1, ks]
            v = v_ref[j:j + 1, vs]
            ig = ig_ref[j:j + 1, h:h + 1]
            lf = lf_ref[j:j + 1, h:h + 1]
            m_prev = m_ref[j:j + 1, h:h + 1]
            c_prev = c_ref[j, h]
            n_prev = n_ref[j, h:h + 1, :]
            m_new = jnp.maximum(lf + m_prev, ig)
            dec = jnp.exp(lf + m_prev - m_new)
            wgt = jnp.exp(ig - m_new)
            co_ref[j, h] = dec * c_prev + (k_col * wgt) * v
            no_ref[j, h:h + 1, :] = dec * n_prev + k_row * wgt
            mo_ref[j:j + 1, h:h + 1] = m_new
            s = jnp.sum(q_row * k_row, axis=1, keepdims=True) * wgt
            num = s * v + dec * jnp.sum(q_col * c_prev, axis=0, keepdims=True)
            den = s + dec * jnp.sum(q_row * n_prev, axis=1, keepdims=True)
            h_ref[j:j + 1, vs] = num / jnp.maximum(jnp.abs(den), jnp.exp(-m_new))


def _ml_step(q, k, v, ig, lf, c_all, layer, n0, m0):
    b = q.shape[0]
    cols = pl.BlockSpec((1, ML_HEADS * ML_QK, DEC_TB), lambda i: (i, 0, 0))
    qk_rows = pl.BlockSpec((DEC_TB, ML_HEADS * ML_QK), lambda i: (i, 0))
    rows = pl.BlockSpec((DEC_TB, ML_W), lambda i: (i, 0))
    sc = pl.BlockSpec((DEC_TB, ML_HEADS), lambda i: (i, 0))
    cs = pl.BlockSpec((DEC_TB, ML_HEADS, ML_QK, ML_V), lambda i: (i, 0, 0, 0))
    ns = pl.BlockSpec((DEC_TB, ML_HEADS, ML_QK), lambda i: (i, 0, 0))
    cs_in = pl.BlockSpec((None, DEC_TB, ML_HEADS, ML_QK, ML_V), lambda i: (layer, i, 0, 0, 0))
    return pl.pallas_call(
        _ml_step_body,
        grid=(b // DEC_TB,),
        in_specs=[cols, cols, qk_rows, qk_rows, rows, sc, sc, cs_in, ns, sc],
        out_specs=[rows, cs, ns, sc],
        out_shape=[jax.ShapeDtypeStruct((b, ML_W), F32), jax.ShapeDtypeStruct(c_all.shape[1:], F32),
                   jax.ShapeDtypeStruct(n0.shape, F32), jax.ShapeDtypeStruct(m0.shape, F32)],
        compiler_params=_cparams(("parallel",)),
        name="ml_step",
    )(_to_cols(q), _to_cols(k), q, k, v, ig, lf, c_all, n0, m0)


def _heads(a, h):
    return a.reshape(a.shape[:-1] + (h, a.shape[-1] // h))


def _shift_prev(p, prev_row):
    return jnp.concatenate([prev_row[:, None, :], p[:, :-1]], axis=1)


def _rotary(x, pos):
    half = x.shape[-1] // 2
    inv = ROPE_BASE ** (-jnp.arange(half, dtype=F32) / half)
    ang = pos.astype(F32)[:, None] * inv[None, :]
    cos = jnp.cos(ang)[None, :, None, :]
    sin = jnp.sin(ang)[None, :, None, :]
    x1, x2 = x[..., :half], x[..., half:]
    return jnp.concatenate([x1 * cos - x2 * sin, x1 * sin + x2 * cos], -1)


def _small_matmul(x, w):
    lead = x.shape[:-1]
    kdim, n = w.shape
    x2 = x.reshape(-1, kdim)
    m = x2.shape[0]
    kp = -(-kdim // LANES) * LANES
    npad = -(-n // LANES) * LANES
    x2 = jnp.pad(x2.astype(BF16), ((0, 0), (0, kp - kdim)))
    w2 = jnp.pad(w.astype(BF16), ((0, kp - kdim), (0, npad - n)))
    tm = 1024 if m % 1024 == 0 else m
    out = _matmul(x2, w2, tm, npad)
    return out[:, :n].reshape(lead + (n,))


def _mix_prompt(p, pos, v_first, lp):
    o_rw, v_first, s_new = _rwkv_prompt(p, lp, v_first, (1, 1, 1, 1))
    o_ret, r_new = _ret_prompt(p, pos)
    o_ml, c_new, n_new, m_new = _ml_prompt(p, lp)
    return (o_rw, o_ret, o_ml), v_first, (s_new, r_new, c_new, n_new, m_new)


def _mix_sample(p, pos, v_first, st, lp, prev_row):
    bsz, t_len, _ = p.shape

    p_rw = p[..., :RW_P]
    mixed = p_rw + (_shift_prev(p_rw, prev_row[:, :RW_P]) - p_rw) * lp['rw_mu']
    sizes = np.cumsum([RW_W, RW_W, RW_W, RW_LORA_W, RW_LORA_A, RW_LORA_G])[:-1]
    r, k, v, xw, xa, xg = jnp.split(mixed, [int(s) for s in sizes], axis=-1)
    w = -jax.nn.softplus(-(lp['rw_w0'] + _small_matmul(jnp.tanh(xw), lp['rw_w2']))) - 0.5
    a = jax.nn.sigmoid(lp['rw_a0'] + _small_matmul(xa, lp['rw_a2']))
    g = _small_matmul(jax.nn.sigmoid(xg), lp['rw_g2'])
    if v_first is None:
        v_first = v
    else:
        pv = p[..., P_V1:P_V1 + RW_LORA_V]
        xv = pv + (_shift_prev(pv, prev_row[:, P_V1:P_V1 + RW_LORA_V]) - pv) * lp['rw_vmu']
        v = v + (v_first - v) * jax.nn.sigmoid(lp['rw_v0'] + _small_matmul(xv, lp['rw_v2']))
    kk = _heads(k * lp['rw_kk'], RW_HEADS)
    kk = kk * lax.rsqrt(jnp.maximum(jnp.sum(jnp.square(kk), -1, keepdims=True), 1e-24))
    kk = kk.reshape(bsz, t_len, RW_W)
    k = k * (1.0 + (a - 1.0) * lp['rw_ka'])
    lw = -jnp.exp(w)
    y, s_new = _rwkv_step(jnp.exp(lw)[:, 0], (kk * a)[:, 0], k[:, 0], kk[:, 0], r[:, 0], v[:, 0],
                          st['rw_wkv_t'], st['layer'])
    y = _heads(y[:, None, :], RW_HEADS)
    y_mu = jnp.mean(y, -1, keepdims=True)
    y_var = jnp.mean(jnp.square(y - y_mu), -1, keepdims=True)
    y = ((y - y_mu) * lax.rsqrt(y_var + RW_GN_EPS)).reshape(bsz, t_len, RW_W)
    y = y * lp['rw_lnx_g'] + lp['rw_lnx_b']
    rh, kh, vh = (_heads(u, RW_HEADS) for u in (r, k, v))
    bonus = jnp.sum(rh * kh * _heads(lp['rw_rk'], RW_HEADS), -1, keepdims=True) * vh
    o_rw = ((y + bonus.reshape(bsz, t_len, RW_W)) * g).astype(BF16)

    nqk = RET_HEADS * RET_QK
    p_ret = p[..., P_RET:P_RET + P_MAIN]
    qr, kr, vr, gr = (p_ret[..., :nqk], p_ret[..., nqk:2 * nqk],
                      p_ret[..., 2 * nqk:2 * nqk + RET_W], p_ret[..., 2 * nqk + RET_W:])
    qh = _rotary(_heads(qr, RET_HEADS), pos).reshape(bsz, t_len, nqk)
    khr = (_rotary(_heads(kr, RET_HEADS), pos) * (RET_QK ** -0.5)).reshape(bsz, t_len, nqk)
    yr, r_new = _ret_step(qh[:, 0], khr[:, 0], vr[:, 0], st['ret_all'], st['layer'])
    yr = _heads(yr[:, None, :], RET_HEADS)
    yr = yr * lax.rsqrt(jnp.mean(jnp.square(yr), -1, keepdims=True) + NORM_EPS)
    o_ret = (jax.nn.silu(gr) * yr.reshape(bsz, t_len, RET_W)).astype(BF16)

    nqk = ML_HEADS * ML_QK
    p_ml = p[..., P_ML:P_ML + P_MAIN]
    qm, km, vm, om = (p_ml[..., :nqk], p_ml[..., nqk:2 * nqk],
                      p_ml[..., 2 * nqk:2 * nqk + ML_W], p_ml[..., 2 * nqk + ML_W:])
    im = p[..., P_GATE:P_GATE + ML_HEADS]
    fm = p[..., P_GATE + ML_HEADS:P_GATE + 2 * ML_HEADS]
    ig = ML_GATE_CAP * jnp.tanh((im + lp['ml_ib']) / ML_GATE_CAP)
    lf = jax.nn.log_sigmoid(ML_GATE_CAP * jnp.tanh((fm + lp['ml_fb']) / ML_GATE_CAP))
    km = km * (ML_QK ** -0.5)
    hm, c_new, n_new, m_new = _ml_step(qm[:, 0], km[:, 0], vm[:, 0], ig[:, 0], lf[:, 0],
                                       st['ml_c_all'], st['layer'], st['ml_n'], st['ml_m'])
    hm = _heads(hm[:, None, :], ML_HEADS)
    hm = hm * lax.rsqrt(jnp.mean(jnp.square(hm), -1, keepdims=True) + NORM_EPS)
    o_ml = (jax.nn.sigmoid(om) * (hm.reshape(bsz, t_len, ML_W) * lp['ml_norm'])).astype(BF16)

    return (o_rw, o_ret, o_ml), v_first, (s_new, r_new, c_new, n_new, m_new)


_T_V1 = P_V1 // LANES
_T_GATE = P_GATE // LANES
_T_RET = P_RET // LANES
_T_SHIFT = (P_RET - RW_P) // LANES
_T_SRC_GATE = (RW_P + RET_P + 2 * ML_HEADS * ML_QK + 2 * ML_W) // LANES


def _pack_body(w_ref, v1_ref, *o_refs):
    j = pl.program_id(0)
    row = lax.broadcasted_iota(jnp.int32, (LANES, 1), 0)
    spare = jnp.logical_and(j > _T_GATE, j < _T_RET)
    for l, o_ref in enumerate(o_refs):
        w = w_ref[:, l, :]
        gates = jnp.where(row < 2 * ML_HEADS, w, 0.0)
        out = jnp.where(j == _T_V1, v1_ref[l], jnp.where(j == _T_GATE, gates, jnp.where(spare, 0.0, w)))
        o_ref[...] = out.astype(BF16)


def _pack_w_in(w_in, rw_v1):
    depth, d, _ = w_in.shape
    w_t = w_in.transpose(2, 0, 1)
    v1_t = jnp.pad(rw_v1.transpose(0, 2, 1), ((1, 0), (0, LANES - RW_LORA_V), (0, 0)))

    def src_tile(j):
        return jnp.where(j < _T_V1, j, jnp.where(j == _T_GATE, _T_SRC_GATE, j - _T_SHIFT))

    return pl.pallas_call(
        _pack_body,
        grid=(P_PAD // LANES,),
        in_specs=[pl.BlockSpec((LANES, depth, d), lambda j: (src_tile(j), 0, 0)),
                  pl.BlockSpec((depth, LANES, d), lambda j: (0, 0, 0))],
        out_specs=[pl.BlockSpec((LANES, d), lambda j: (j, 0))] * depth,
        out_shape=[jax.ShapeDtypeStruct((P_PAD, d), BF16)] * depth,
        compiler_params=_cparams(("parallel",)),
        name="pack_w_in",
    )(w_t, v1_t)


def _token_tiles(m):
    if m % 2048 == 0:
        return 2048, 512, 512
    return m, m, m


def kernel(x_prompt, x_sample, state_rw_shift, state_rw_wkv, state_ret, state_ml_c, state_ml_n, state_ml_m,
           ln0_g, ln0_b, w_in, rw_mu, rw_w0, rw_w2, rw_a0, rw_a2, rw_g2, rw_kk, rw_ka, rw_rk,
           rw_lnx_g, rw_lnx_b, rw_v0, rw_v1, rw_vmu, rw_v2, ml_ib, ml_fb, ml_norm, w_out,
           ln1_g, ln1_b, w_gate, w_up, w_down, ln2_g, ln2_b):
    bp, tp, d = x_prompt.shape
    bs, ts, _ = x_sample.shape
    mp, ms = bp * tp, bs * ts
    pos_p = jnp.arange(tp)
    pos_s = PAST_LEN + jnp.arange(ts)
    tm_big, tm_out, tm_down = _token_tiles(mp)
    rw_wkv_t = state_rw_wkv.transpose(0, 2, 3, 4, 1)
    w_in_packed = _pack_w_in(w_in, rw_v1)
    xf_p, xb_p = _layernorm(x_prompt.reshape(mp, d), ln0_g, ln0_b, tm_out)
    xf_s, xb_s = _layernorm(x_sample.reshape(ms, d), ln0_g, ln0_b, ms)
    vf_p = vf_s = None
    outs_p, outs_s = [], []

    for l in range(DEPTH):
        lp = {
            'rw_mu': rw_mu[l], 'rw_w0': rw_w0[l], 'rw_w2': rw_w2[l], 'rw_a0': rw_a0[l], 'rw_a2': rw_a2[l],
            'rw_g2': rw_g2[l], 'rw_kk': rw_kk[l], 'rw_ka': rw_ka[l], 'rw_rk': rw_rk[l],
            'rw_lnx_g': rw_lnx_g[l], 'rw_lnx_b': rw_lnx_b[l], 'ml_ib': ml_ib[l], 'ml_fb': ml_fb[l],
            'ml_norm': ml_norm[l],
        }
        if l > 0:
            lp.update(rw_v0=rw_v0[l - 1], rw_vmu=rw_vmu[l - 1], rw_v2=rw_v2[l - 1])
        w_out_b = w_out[l].astype(BF16)
        w_down_b = w_down[l].astype(BF16)
        x_side = jnp.concatenate([xb_s, state_rw_shift[l].astype(BF16)], axis=0)
        p_p, p_side = _in_proj(xb_p, x_side, w_in_packed[l], tm_big, P_TN)
        o_p, vf_p, st_p = _mix_prompt(p_p.reshape(bp, tp, P_PAD), pos_p, vf_p, lp)
        st = {'rw_wkv_t': rw_wkv_t, 'layer': l, 'ret_all': state_ret, 'ml_c_all': state_ml_c,
              'ml_n': state_ml_n[l], 'ml_m': state_ml_m[l]}
        o_s, vf_s, st_s = _mix_sample(p_side[:ms].reshape(bs, ts, P_PAD), pos_s, vf_s, st, lp, p_side[ms:])
        outs_p.append((xf_p.reshape(bp, tp, d)[:, -1],) + st_p)
        outs_s.append((xf_s.reshape(bs, ts, d)[:, -1],) + st_s)
        flat = lambda o, m: tuple(u.reshape(m, u.shape[-1]) for u in o)
        x1f_p, x1b_p, x1f_s, x1b_s = _out_proj_ln(flat(o_p, mp), xf_p, flat(o_s, ms), xf_s, w_out_b,
                                                  ln1_g[l], ln1_b[l], tm_out)
        hdn_p, hdn_s = _matmul_swiglu(x1b_p, x1b_s, w_gate, w_up, l, tm_big, 512)
        xf_p, xb_p, xf_s, xb_s = _matmul_res_ln(hdn_p, hdn_s, w_down_b, x1f_p, x1f_s, ln2_g[l], ln2_b[l],
                                                tm_down, DOWN_TK)

    y_p = xf_p.reshape(bp, tp, d)
    y_s = xf_s.reshape(bs, ts, d)
    sp = [jnp.stack([o[i] for o in outs_p]) for i in range(6)]
    ss = [jnp.stack([o[i] for o in outs_s]) for i in range(6)]
    ss[1] = ss[1].transpose(0, 4, 1, 2, 3)
    return (y_p, y_s, sp[0], sp[1], sp[2], sp[3], sp[4], sp[5], ss[0], ss[1], ss[2], ss[3], ss[4], ss[5])
```

```python
import functools
import math

import numpy as np
import jax
import jax.numpy as jnp
from jax import lax
from jax.experimental import pallas as pl
from jax.experimental.pallas import tpu as pltpu

F32 = jnp.float32
BF16 = jnp.bfloat16

D_MODEL = 2048
DEPTH = 2
PAST_LEN = 16384
RW_HD = 64
RW_W = D_MODEL // 4
RW_HEADS = RW_W // RW_HD
RW_LORA_W = 64
RW_LORA_A = 64
RW_LORA_V = 32
RW_LORA_G = 128
RW_P = 3 * RW_W + RW_LORA_W + RW_LORA_A + RW_LORA_G
RW_GN_EPS = 64e-5
RET_V = 128
RET_QK = 64
RET_W = 3 * D_MODEL // 8
RET_HEADS = RET_W // RET_V
RET_P = 2 * RET_HEADS * RET_QK + 2 * RET_W
ML_V = 128
ML_QK = 64
ML_W = D_MODEL - RW_W - RET_W
ML_HEADS = ML_W // ML_V
ML_P = 2 * ML_HEADS * ML_QK + 2 * ML_W + 2 * ML_HEADS
ML_GATE_CAP = 15.0
P_TOTAL = RW_P + RET_P + ML_P
D_FF = ((8 * D_MODEL + 3 * 256 - 1) // (3 * 256)) * 256
CHUNK = 128
ROPE_BASE = 10000.0
LN_EPS = 1e-5
NORM_EPS = 1e-6
ALPHA = (2 * DEPTH) ** 0.25

LANES = 128
P_V1 = RW_P
P_GATE = RW_P + LANES
P_RET = 2304
P_ML = 2 * P_RET
P_MAIN = 2304
P_PAD = 3 * P_RET
P_TN = 768
RW_CHUNK = 64
DEC_TB = 8
VMEM_LIMIT = 56 * 1024 * 1024

HIGHEST = lax.Precision.HIGHEST


def _cparams(sem):
    return pltpu.CompilerParams(dimension_semantics=sem, vmem_limit_bytes=VMEM_LIMIT)


def _dot(a, b, precision=None):
    return lax.dot_general(a, b, (((1,), (0,)), ((), ())), precision=precision, preferred_element_type=F32)


def _dot_nt(a, b, precision=None):
    return lax.dot_general(a, b, (((1,), (1,)), ((), ())), precision=precision, preferred_element_type=F32)


def _dot_tn(a, b, precision=None):
    return lax.dot_general(a, b, (((0,), (0,)), ((), ())), precision=precision, preferred_element_type=F32)


def _ln_rows(x, g, b):
    mu = jnp.mean(x, -1, keepdims=True)
    xc = x - mu
    var = jnp.mean(xc * xc, -1, keepdims=True)
    return xc * lax.rsqrt(var + LN_EPS) * g + b


def _ln_body(x_ref, g_ref, b_ref, of_ref, ob_ref):
    y = _ln_rows(x_ref[...], g_ref[...], b_ref[...])
    of_ref[...] = y
    ob_ref[...] = y.astype(BF16)


def _layernorm(x, g, b, tm):
    m, d = x.shape
    return pl.pallas_call(
        _ln_body,
        grid=(m // tm,),
        in_specs=[pl.BlockSpec((tm, d), lambda i: (i, 0)),
                  pl.BlockSpec((1, d), lambda i: (0, 0)),
                  pl.BlockSpec((1, d), lambda i: (0, 0))],
        out_specs=[pl.BlockSpec((tm, d), lambda i: (i, 0)),
                   pl.BlockSpec((tm, d), lambda i: (i, 0))],
        out_shape=[jax.ShapeDtypeStruct((m, d), F32), jax.ShapeDtypeStruct((m, d), BF16)],
        compiler_params=_cparams(("parallel",)),
        name="layernorm",
    )(x, g.reshape(1, d), b.reshape(1, d))


def _mm_body(x_ref, w_ref, o_ref, *, w_transposed):
    dot = _dot_nt if w_transposed else _dot
    o_ref[...] = dot(x_ref[...], w_ref[...]).astype(o_ref.dtype)


def _matmul(x, w, tm, tn, out_dtype=F32, w_transposed=False):
    m, k = x.shape
    n = w.shape[0] if w_transposed else w.shape[1]
    w_spec = (pl.BlockSpec((tn, k), lambda i, j: (j, 0)) if w_transposed
              else pl.BlockSpec((k, tn), lambda i, j: (0, j)))
    return pl.pallas_call(
        functools.partial(_mm_body, w_transposed=w_transposed),
        grid=(m // tm, n // tn),
        in_specs=[pl.BlockSpec((tm, k), lambda i, j: (i, 0)), w_spec],
        out_specs=pl.BlockSpec((tm, tn), lambda i, j: (i, j)),
        out_shape=jax.ShapeDtypeStruct((m, n), out_dtype),
        compiler_params=_cparams(("parallel", "parallel")),
        name="matmul",
    )(x, w)


def _in_proj_body(x_ref, xs_ref, w_ref, o_ref, os_ref):
    @pl.when(pl.program_id(1) == 0)
    def _():
        os_ref[...] = _dot_nt(xs_ref[...], w_ref[...])

    o_ref[...] = _dot_nt(x_ref[...], w_ref[...])


def _in_proj(x, x_side, w_t, tm, tn):
    m, k = x.shape
    ms = x_side.shape[0]
    n = w_t.shape[0]
    return pl.pallas_call(
        _in_proj_body,
        grid=(n // tn, m // tm),
        in_specs=[pl.BlockSpec((tm, k), lambda j, i: (i, 0)), pl.BlockSpec((ms, k), lambda j, i: (0, 0)),
                  pl.BlockSpec((tn, k), lambda j, i: (j, 0))],
        out_specs=[pl.BlockSpec((tm, tn), lambda j, i: (i, j)), pl.BlockSpec((ms, tn), lambda j, i: (0, j))],
        out_shape=[jax.ShapeDtypeStruct((m, n), F32), jax.ShapeDtypeStruct((ms, n), F32)],
        compiler_params=_cparams(("parallel", "arbitrary")),
        name="in_proj",
    )(x, x_side, w_t)


def _swiglu_body(x_ref, xs_ref, wg_ref, wu_ref, o_ref, os_ref, wg_scr, wu_scr):
    def act(x):
        g = _dot(x, wg_scr[...])
        return (g * jax.nn.sigmoid(g) * _dot(x, wu_scr[...])).astype(BF16)

    @pl.when(pl.program_id(1) == 0)
    def _():
        wg_scr[...] = wg_ref[...].astype(BF16)
        wu_scr[...] = wu_ref[...].astype(BF16)
        os_ref[...] = act(xs_ref[...])

    o_ref[...] = act(x_ref[...])


def _matmul_swiglu(x, x_side, wg, wu, layer, tm, tn):
    m, k = x.shape
    ms = x_side.shape[0]
    n = wg.shape[2]
    w_spec = pl.BlockSpec((None, k, tn), lambda j, i: (layer, 0, j))
    return pl.pallas_call(
        _swiglu_body,
        grid=(n // tn, m // tm),
        in_specs=[pl.BlockSpec((tm, k), lambda j, i: (i, 0)), pl.BlockSpec((ms, k), lambda j, i: (0, 0)),
                  w_spec, w_spec],
        out_specs=[pl.BlockSpec((tm, tn), lambda j, i: (i, j)), pl.BlockSpec((ms, tn), lambda j, i: (0, j))],
        out_shape=[jax.ShapeDtypeStruct((m, n), BF16), jax.ShapeDtypeStruct((ms, n), BF16)],
        scratch_shapes=[pltpu.VMEM((k, tn), BF16), pltpu.VMEM((k, tn), BF16)],
        compiler_params=_cparams(("parallel", "arbitrary")),
        name="matmul_swiglu",
    )(x, x_side, wg, wu)


DOWN_TK = D_FF // 4
LN_ROWS = 256
OUT_ROWS = 256


def _res_ln_store(acc_ref, res_ref, g_ref, b_ref, of_ref, ob_ref, n_rows):
    step = min(LN_ROWS, n_rows)
    for r in range(0, n_rows, step):
        rows = pl.ds(r, step)
        y = _ln_rows(ALPHA * res_ref[rows, :] + acc_ref[rows, :], g_ref[...], b_ref[...])
        of_ref[rows, :] = y
        ob_ref[rows, :] = y.astype(BF16)


def _mm_res_ln_body(x_ref, xs_ref, w_ref, res_ref, ress_ref, g_ref, b_ref, of_ref, ob_ref, ofs_ref, obs_ref, *, nk, tm, ms):
    i = pl.program_id(0)
    kk = pl.program_id(1)

    @pl.when(kk == 0)
    def _():
        of_ref[...] = jnp.zeros_like(of_ref)

    of_ref[...] += _dot(x_ref[...], w_ref[...])

    @pl.when(kk == nk - 1)
    def _():
        _res_ln_store(of_ref, res_ref, g_ref, b_ref, of_ref, ob_ref, tm)

    @pl.when(i == 0)
    def _():
        @pl.when(kk == 0)
        def _():
            ofs_ref[...] = jnp.zeros_like(ofs_ref)

        ofs_ref[...] += _dot(xs_ref[...], w_ref[...])

        @pl.when(kk == nk - 1)
        def _():
            _res_ln_store(ofs_ref, ress_ref, g_ref, b_ref, ofs_ref, obs_ref, ms)


def _matmul_res_ln(x, x_side, w, res, res_side, g, b, tm, tk):
    m, k = x.shape
    ms = x_side.shape[0]
    n = w.shape[1]
    nk = k // tk
    const = lambda r: pl.BlockSpec((r, n), lambda i, j: (0, 0))
    main = pl.BlockSpec((tm, n), lambda i, j: (i, 0))
    return pl.pallas_call(
        functools.partial(_mm_res_ln_body, nk=nk, tm=tm, ms=ms),
        grid=(m // tm, nk),
        in_specs=[pl.BlockSpec((tm, tk), lambda i, j: (i, j)), pl.BlockSpec((ms, tk), lambda i, j: (0, j)),
                  pl.BlockSpec((tk, n), lambda i, j: (j, 0)), main, const(ms), const(1), const(1)],
        out_specs=[main, main, const(ms), const(ms)],
        out_shape=[jax.ShapeDtypeStruct((m, n), F32), jax.ShapeDtypeStruct((m, n), BF16),
                   jax.ShapeDtypeStruct((ms, n), F32), jax.ShapeDtypeStruct((ms, n), BF16)],
        compiler_params=_cparams(("arbitrary", "arbitrary")),
        name="matmul_res_ln",
    )(x, x_side, w, res, res_side, g.reshape(1, n), b.reshape(1, n))


def _out_proj_ln_body(o_rw_ref, o_ret_ref, o_ml_ref, res_ref, s_rw_ref, s_ret_ref, s_ml_ref, ress_ref,
                      w_ref, g_ref, b_ref, of_ref, ob_ref, ofs_ref, obs_ref, *, tm, ms):
    def project(rw_ref, ret_ref, ml_ref, r_ref, f_ref, h_ref, n_rows):
        step = min(OUT_ROWS, n_rows)
        for r in range(0, n_rows, step):
            rows = pl.ds(r, step)
            mix = (_dot(rw_ref[rows, :], w_ref[0:RW_W, :])
                   + _dot(ret_ref[rows, :], w_ref[RW_W:RW_W + RET_W, :])
                   + _dot(ml_ref[rows, :], w_ref[RW_W + RET_W:, :]))
            y = _ln_rows(ALPHA * r_ref[rows, :] + mix, g_ref[...], b_ref[...])
            f_ref[rows, :] = y
            h_ref[rows, :] = y.astype(BF16)

    @pl.when(pl.program_id(0) == 0)
    def _():
        project(s_rw_ref, s_ret_ref, s_ml_ref, ress_ref, ofs_ref, obs_ref, ms)

    project(o_rw_ref, o_ret_ref, o_ml_ref, res_ref, of_ref, ob_ref, tm)


def _out_proj_ln(o, res, o_side, res_side, w, g, b, tm):
    m = res.shape[0]
    ms = res_side.shape[0]
    n = w.shape[1]
    rows = lambda width: pl.BlockSpec((tm, width), lambda i: (i, 0))
    const = lambda r, c: pl.BlockSpec((r, c), lambda i: (0, 0))
    widths = (RW_W, RET_W, ML_W)
    return pl.pallas_call(
        functools.partial(_out_proj_ln_body, tm=tm, ms=ms),
        grid=(m // tm,),
        in_specs=([rows(c) for c in widths] + [rows(n)] + [const(ms, c) for c in widths] + [const(ms, n)]
                  + [const(D_MODEL, n), const(1, n), const(1, n)]),
        out_specs=[rows(n), rows(n), const(ms, n), const(ms, n)],
        out_shape=[jax.ShapeDtypeStruct((m, n), F32), jax.ShapeDtypeStruct((m, n), BF16),
                   jax.ShapeDtypeStruct((ms, n), F32), jax.ShapeDtypeStruct((ms, n), BF16)],
        compiler_params=_cparams(("arbitrary",)),
        name="out_proj_ln",
    )(*o, res, *o_side, res_side, w, g.reshape(1, n), b.reshape(1, n))


RW_TB = 256
RW_GH = 4
RW_GW = RW_GH * RW_HD
RW_VEC_ROWS = 8


def _split3(x):
    hi = x.astype(BF16)
    r1 = x - hi.astype(F32)
    mid = r1.astype(BF16)
    lo = (r1 - mid.astype(F32)).astype(BF16)
    return hi, mid, lo


def _mm(a, b, dims, passes):
    dg = lambda x, y: lax.dot_general(x, y, (dims, ((), ())), preferred_element_type=F32)
    if passes == 6:
        return lax.dot_general(a, b, (dims, ((), ())), precision=HIGHEST, preferred_element_type=F32)
    ah = a.astype(BF16)
    bh = b.astype(BF16)
    if passes == 1:
        return dg(ah, bh)
    al = (a - ah.astype(F32)).astype(BF16)
    bl = (b - bh.astype(F32)).astype(BF16)
    return dg(ah, bh) + (dg(ah, bl) + dg(al, bh))


_NN = ((1,), (0,))
_NT = ((1,), (1,))
_TN = ((0,), (0,))


def _exact_lhs_dot(a_bf16, b):
    hi, mid, lo = _split3(b)
    dg = lambda y: lax.dot_general(a_bf16, y, (_NN, ((), ())), preferred_element_type=F32)
    return dg(hi) + (dg(mid) + dg(lo))


def _exact_rhs_dot(a, b_bf16):
    hi, mid, lo = _split3(a)
    dg = lambda x: lax.dot_general(x, b_bf16, (_NN, ((), ())), preferred_element_type=F32)
    return dg(hi) + (dg(mid) + dg(lo))


def _seg_sum(a, seg_bf16):
    hi = a.astype(BF16)
    lo = (a - hi.astype(F32)).astype(BF16)
    dg = lambda x: lax.dot_general(x, seg_bf16, (_NN, ((), ())), preferred_element_type=F32)
    return dg(hi) + dg(lo)


def _rw_scan(r, lw, k, v, kk, a, st_scr, y_scr, passes):
    L, TB, G = RW_CHUNK, RW_TB, RW_GW
    p_sc, p_inv, p_app, p_st = passes
    row = lax.broadcasted_iota(jnp.int32, (L, G), 0)
    col = lax.broadcasted_iota(jnp.int32, (L, G), 1) & (L - 1)
    strict, lower, eye = row > col, row >= col, (row == col).astype(F32)
    rg = lax.broadcasted_iota(jnp.int32, (G, G), 0) // RW_HD
    cg = lax.broadcasted_iota(jnp.int32, (G, G), 1) // RW_HD
    mask_bd = rg == cg
    rt = lax.broadcasted_iota(jnp.int32, (TB, TB), 0)
    ct = lax.broadcasted_iota(jnp.int32, (TB, TB), 1)
    tri = jnp.logical_and(rt >= ct, rt // L == ct // L).astype(BF16)
    bd = lambda x: jnp.where(mask_bd, jnp.concatenate([x] * RW_GH, axis=0), 0.0)
    blocks = [(s, g) for s in range(TB // L) for g in range(RW_HEADS // RW_GH)]
    cut = lambda x, b: x[b[0] * L:(b[0] + 1) * L, b[1] * G:(b[1] + 1) * G]

    cum = _exact_lhs_dot(tri, lw)
    e_neg = jnp.exp(-cum)
    ap = kk * a
    ap_h = ap * e_neg
    k_h = k * e_neg
    kk_t = kk * jnp.exp(cum - lw)
    r_t = r * jnp.exp(cum)

    lhs, n_m, m_a, m_kr = {}, {}, {}, {}
    for b in blocks:
        lhs[b] = jnp.concatenate([cut(kk_t, b), cut(r_t, b)], axis=0)
        sc_a = _mm(lhs[b], bd(cut(ap_h, b)), _NT, p_sc)
        sc_k = _mm(lhs[b], bd(cut(k_h, b)), _NT, p_sc)
        n_m[b] = jnp.where(strict, sc_a[:L], 0.0)
        m_a[b] = jnp.where(lower, sc_a[L:], 0.0)
        m_kr[b] = jnp.concatenate([jnp.where(strict, sc_k[:L], 0.0), jnp.where(lower, sc_k[L:], 0.0)], axis=0)
    inv = {b: eye - n_m[b] for b in blocks}
    pw = {b: _mm(n_m[b], bd(n_m[b]), _NN, p_inv) for b in blocks}
    n_iter = int(math.log2(L)) - 1
    for j in range(n_iter):
        last = j == n_iter - 1
        for b in blocks:
            lhs_j = inv[b] if last else jnp.concatenate([inv[b], pw[b]], axis=0)
            prod = _mm(lhs_j, bd(pw[b]), _NN, p_inv)
            inv[b] = inv[b] + prod[:L]
            if not last:
                pw[b] = prod[L:]
    mv, vk, a_end, decay = {}, {}, {}, {}
    for b in blocks:
        s, g = b
        tot = cum[(s + 1) * L - 1:(s + 1) * L, g * G:(g + 1) * G]
        e_end = jnp.exp(tot - cut(cum, b))
        mv[b] = _mm(m_kr[b], bd(cut(v, b)), _NN, p_app)
        vk[b] = _mm(cut(v, b), cut(k, b) * e_end, _TN, p_st)
        a_end[b] = cut(ap, b) * e_end
        decay[b] = jnp.exp(tot)

    for b in blocks:
        s, g = b
        st = st_scr[g]
        s_terms = _mm(lhs[b], st, _NT, p_app)
        u = _mm(inv[b], bd(s_terms[:L] + mv[b][:L]), _NN, p_app)
        y_scr[s * L:(s + 1) * L, g * G:(g + 1) * G] = s_terms[L:] + mv[b][L:] - _mm(m_a[b], bd(u), _NN, p_app)
        st_scr[g] = jnp.where(mask_bd, st * decay[b] + vk[b] - _mm(u, a_end[b], _TN, p_st), 0.0)


def _softplus(z):
    return jnp.maximum(z, 0.0) + jnp.log(1.0 + jnp.exp(-jnp.abs(z)))


def _rwkv_fused_body(*refs, nc, has_vres, passes):
    if has_vres:
        (p_ref, pv_ref, vf_ref, mu_ref, vec_ref, wa_ref, g2_ref, seg_ref, vmu_ref, v2_ref,
         o_ref, sf_ref, st_scr, prev_scr, y_scr, prevv_scr) = refs
    else:
        (p_ref, mu_ref, vec_ref, wa_ref, g2_ref, seg_ref,
         o_ref, vfo_ref, sf_ref, st_scr, prev_scr, y_scr) = refs
    TB, W, L = RW_TB, RW_W, RW_CHUNK
    c = pl.program_id(1)

    @pl.when(c == 0)
    def _():
        st_scr[...] = jnp.zeros_like(st_scr)
        prev_scr[...] = jnp.zeros_like(prev_scr)
        if has_vres:
            prevv_scr[...] = jnp.zeros_like(prevv_scr)

    first_row = lax.broadcasted_iota(jnp.int32, (TB, 1), 0) == 0

    def shift_mix(x, carry_ref, mu):
        prev = jnp.where(first_row, carry_ref[...], pltpu.roll(x, 1, 0))
        carry_ref[...] = x[TB - 1:TB, :]
        return x + (prev - x) * mu

    mixed = shift_mix(p_ref[0], prev_scr, mu_ref[...])
    r = mixed[:, 0:W]
    k = mixed[:, W:2 * W]
    v = mixed[:, 2 * W:3 * W]
    xwa = mixed[:, 3 * W:3 * W + LANES]
    xg = mixed[:, 3 * W + LANES:3 * W + 2 * LANES]
    vec = vec_ref[...]
    w0, a0, kk_s, ka, rk, lnx_g, lnx_b, v0 = (vec[i:i + 1, :] for i in range(RW_VEC_ROWS))
    seg = seg_ref[...]
    wa = wa_ref[...]
    w_lora = _dot(jnp.tanh(xwa).astype(BF16), wa[:, 0:W])
    a_lora = _dot(xwa.astype(BF16), wa[:, W:2 * W])
    lw = -jnp.exp(-_softplus(-(w0 + w_lora)) - 0.5)
    a = jax.nn.sigmoid(a0 + a_lora)
    g = _dot(jax.nn.sigmoid(xg).astype(BF16), g2_ref[...])
    if has_vres:
        xv = shift_mix(pv_ref[0], prevv_scr, vmu_ref[...])
        v = v + (vf_ref[0] - v) * jax.nn.sigmoid(v0 + _dot(xv.astype(BF16), v2_ref[...]))
    else:
        vfo_ref[0] = v
    kk = k * kk_s
    kk = kk * lax.rsqrt(jnp.maximum(_seg_sum(kk * kk, seg), 1e-24))
    k = k * (1.0 + (a - 1.0) * ka)

    _rw_scan(r, lw, k, v, kk, a, st_scr, y_scr, passes)

    y = y_scr[...]
    inv_n = 1.0 / RW_HD
    y_mu = _seg_sum(y, seg) * inv_n
    yc = y - y_mu
    y_var = _seg_sum(yc * yc, seg) * inv_n
    y = yc * lax.rsqrt(y_var + RW_GN_EPS) * lnx_g + lnx_b
    bonus = _seg_sum(r * k * rk, seg) * v
    o_ref[0] = ((y + bonus) * g).astype(BF16)

    @pl.when(c == nc - 1)
    def _():
        sf_ref[0] = st_scr[...]


def _rwkv_prompt(p3, lp, v_first, passes=(3, 3, 3, 3)):
    b, t, _ = p3.shape
    nc = t // RW_TB
    has_vres = v_first is not None
    ng = RW_HEADS // RW_GH
    zpad = jnp.zeros((RW_LORA_W, RW_W), F32)
    wa = jnp.concatenate([jnp.concatenate([lp['rw_w2'], zpad], 0), jnp.concatenate([zpad, lp['rw_a2']], 0)], 1)
    vec = jnp.stack([lp['rw_w0'], lp['rw_a0'], lp['rw_kk'], lp['rw_ka'], lp['rw_rk'], lp['rw_lnx_g'], lp['rw_lnx_b'],
                     lp['rw_v0'] if has_vres else jnp.zeros((RW_W,), F32)])
    hid = jnp.arange(RW_W) // RW_HD
    seg = (hid[:, None] == hid[None, :]).astype(BF16)
    full = lambda shape: pl.BlockSpec(shape, lambda i, j: (0,) * len(shape))
    seq = lambda w, blk: pl.BlockSpec((1, RW_TB, w), lambda i, j: (i, j, blk))
    in_specs = [seq(RW_P, 0)]
    args = [p3]
    if has_vres:
        in_specs += [seq(LANES, P_V1 // LANES), seq(RW_W, 0)]
        args += [p3, v_first]
    in_specs += [full((1, RW_P)), full((RW_VEC_ROWS, RW_W)), full((LANES, 2 * RW_W)), full((RW_LORA_G, RW_W)),
                 full((RW_W, RW_W))]
    args += [lp['rw_mu'].reshape(1, RW_P), vec, wa.astype(BF16), lp['rw_g2'].astype(BF16), seg]
    if has_vres:
        in_specs += [full((1, LANES)), full((LANES, RW_W))]
        args += [jnp.pad(lp['rw_vmu'], (0, LANES - RW_LORA_V)).reshape(1, LANES),
                 jnp.pad(lp['rw_v2'], ((0, LANES - RW_LORA_V), (0, 0))).astype(BF16)]
    out_specs = [seq(RW_W, 0)]
    out_shape = [jax.ShapeDtypeStruct((b, t, RW_W), BF16)]
    if not has_vres:
        out_specs.append(seq(RW_W, 0))
        out_shape.append(jax.ShapeDtypeStruct((b, t, RW_W), F32))
    out_specs.append(pl.BlockSpec((1, ng, RW_GW, RW_GW), lambda i, j: (i, 0, 0, 0)))
    out_shape.append(jax.ShapeDtypeStruct((b, ng, RW_GW, RW_GW), F32))
    scratch = [pltpu.VMEM((ng, RW_GW, RW_GW), F32), pltpu.VMEM((1, RW_P), F32), pltpu.VMEM((RW_TB, RW_W), F32)]
    if has_vres:
        scratch.append(pltpu.VMEM((1, LANES), F32))
    outs = pl.pallas_call(
        functools.partial(_rwkv_fused_body, nc=nc, has_vres=has_vres, passes=passes),
        grid=(b, nc),
        in_specs=in_specs,
        out_specs=out_specs,
        out_shape=out_shape,
        scratch_shapes=scratch,
        compiler_params=_cparams(("parallel", "arbitrary")),
        name="rwkv_fused",
    )(*args)
    if has_vres:
        o, st_bd = outs
    else:
        o, v_first, st_bd = outs
    st5 = st_bd.reshape(b, ng, RW_GH, RW_HD, RW_GH, RW_HD)
    s_fin = jnp.stack([st5[:, :, h, :, h, :] for h in range(RW_GH)], axis=2)
    s_fin = s_fin.reshape(b, RW_HEADS, RW_HD, RW_HD).transpose(0, 1, 3, 2)
    return o, v_first, s_fin


def _ret_log_gamma(h):
    return math.log1p(-(2.0 ** (-5.0 - h)))


def _rotary_tables(pos, heads, dk):
    half = dk // 2
    inv = ROPE_BASE ** (-jnp.arange(half, dtype=F32) / half)
    ang = pos.astype(F32)[:, None] * inv[None, :]
    cos = jnp.tile(jnp.concatenate([jnp.cos(ang), jnp.cos(ang)], -1), (1, heads))
    sin = jnp.tile(jnp.concatenate([-jnp.sin(ang), jnp.sin(ang)], -1), (1, heads))
    lane = jnp.arange(heads * dk)
    perm = (lane[:, None] == (lane[None, :] ^ half)).astype(BF16)
    return cos, sin, perm


def _ret_fused_body(p_ref, cos_ref, sin_ref, perm_ref, o_ref, sf_ref, s_scr, *, nc):
    L, DK, DV, H = CHUNK, RET_QK, RET_V, RET_HEADS
    nq = H * DK
    c = pl.program_id(1)

    @pl.when(c == 0)
    def _():
        s_scr[...] = jnp.zeros_like(s_scr)

    cos = cos_ref[...]
    sin = sin_ref[...]
    perm = perm_ref[...]
    rot = lambda x: x * cos + _exact_rhs_dot(x, perm) * sin
    q_all = rot(p_ref[0, :, 0:nq])
    k_all = rot(p_ref[0, :, nq:2 * nq]) * (DK ** -0.5)
    row = lax.broadcasted_iota(jnp.int32, (L, L), 0)
    col = lax.broadcasted_iota(jnp.int32, (L, L), 1)
    rel = (row - col).astype(F32)
    idx = lax.broadcasted_iota(jnp.int32, (L, 1), 0).astype(F32)
    vs = lambda h: p_ref[0, :, 2 * nq + h * DV:2 * nq + (h + 1) * DV].astype(BF16)
    hd = [dict() for _ in range(H)]
    for h, t in enumerate(hd):
        lg = _ret_log_gamma(h)
        q = q_all[:, h * DK:(h + 1) * DK]
        k = k_all[:, h * DK:(h + 1) * DK]
        t['s_prev'] = s_scr[h]
        t['dmask'] = jnp.where(rel >= 0, jnp.exp(jnp.maximum(rel, 0.0) * lg), 0.0)
        t['qk'] = _dot_nt(q.astype(BF16), k.astype(BF16))
        q_dec = q * jnp.exp((idx + 1.0) * lg)
        t['qs'] = _dot(q_dec.astype(BF16), t['s_prev'].astype(BF16))
        k_end = k * jnp.exp((L - 1.0 - idx) * lg)
        t['kv'] = _dot_tn(k_end.astype(BF16), vs(h))
    for h, t in enumerate(hd):
        y = _dot((t['qk'] * t['dmask']).astype(BF16), vs(h)) + t['qs']
        y = y * lax.rsqrt(jnp.mean(y * y, -1, keepdims=True) + NORM_EPS)
        gate = p_ref[0, :, 2 * nq + RET_W + h * DV:2 * nq + RET_W + (h + 1) * DV]
        o_ref[0, :, h * DV:(h + 1) * DV] = (gate * jax.nn.sigmoid(gate) * y).astype(BF16)
    for h, t in enumerate(hd):
        s_scr[h] = math.exp(L * _ret_log_gamma(h)) * t['s_prev'] + t['kv']

    @pl.when(c == nc - 1)
    def _():
        sf_ref[0] = s_scr[...]


def _ret_prompt(p3, pos):
    b, t, _ = p3.shape
    L = CHUNK
    nc = t // L
    nq = RET_HEADS * RET_QK
    cos, sin, perm = _rotary_tables(pos, RET_HEADS, RET_QK)
    tab = pl.BlockSpec((L, nq), lambda i, j: (j, 0))
    st = pl.BlockSpec((1, RET_HEADS, RET_QK, RET_V), lambda i, j: (i, 0, 0, 0))
    return pl.pallas_call(
        functools.partial(_ret_fused_body, nc=nc),
        grid=(b, nc),
        in_specs=[pl.BlockSpec((1, L, P_MAIN), lambda i, j: (i, j, P_RET // P_MAIN)), tab, tab,
                  pl.BlockSpec((nq, nq), lambda i, j: (0, 0))],
        out_specs=[pl.BlockSpec((1, L, RET_W), lambda i, j: (i, j, 0)), st],
        out_shape=[jax.ShapeDtypeStruct((b, t, RET_W), BF16),
                   jax.ShapeDtypeStruct((b, RET_HEADS, RET_QK, RET_V), F32)],
        scratch_shapes=[pltpu.VMEM((RET_HEADS, RET_QK, RET_V), F32)],
        compiler_params=_cparams(("parallel", "arbitrary")),
        name="ret_fused",
    )(p3, cos, sin, perm)


ML_HPAD = 8


def _ml_fused_body(p_ref, gate_ref, bias_ref, norm_ref, sel_ref, o_ref, cf_ref, nf_ref, mf_ref,
                   c_scr, n_scr, m_scr, *, nc):
    L, DK, DV, H = CHUNK, ML_QK, ML_V, ML_HEADS
    nq = H * DK
    ci = pl.program_id(1)

    @pl.when(ci == 0)
    def _():
        c_scr[...] = jnp.zeros_like(c_scr)
        n_scr[...] = jnp.zeros_like(n_scr)
        m_scr[...] = jnp.zeros_like(m_scr)

    row = lax.broadcasted_iota(jnp.int32, (L, L), 0)
    col = lax.broadcasted_iota(jnp.int32, (L, L), 1)
    causal = row >= col
    tri = causal.astype(BF16)
    capped = ML_GATE_CAP * jnp.tanh((gate_ref[0] + bias_ref[...]) * (1.0 / ML_GATE_CAP))
    lane = lax.broadcasted_iota(jnp.int32, (L, LANES), 1)
    g = jnp.where(lane < H, capped, jnp.where(lane < 2 * H, -_softplus(-capped), 0.0))
    g_rep = _exact_rhs_dot(g, sel_ref[...])
    b_rep_all = _exact_lhs_dot(tri, g_rep[:, H * LANES:])
    g_t = g.T
    cum_t = _exact_lhs_dot(tri, g).T
    cm_all = g_rep[:, :H * LANES] - b_rep_all
    row_id = lax.broadcasted_iota(jnp.int32, (L, 1), 0)
    shift = 1
    while shift < L:
        cm_all = jnp.maximum(cm_all, jnp.where(row_id >= shift, pltpu.roll(cm_all, shift, 0), -jnp.inf))
        shift *= 2
    ones = jnp.ones((L, LANES), BF16)
    mean_w = jnp.full((DV, LANES), 1.0 / DV, BF16)
    m_all = m_scr[...]
    hd = [dict() for _ in range(H)]
    for h, t in enumerate(hd):
        hs = slice(h * LANES, (h + 1) * LANES)
        q = p_ref[0, :, h * DK:(h + 1) * DK].astype(BF16)
        k = p_ref[0, :, nq + h * DK:nq + (h + 1) * DK] * (DK ** -0.5)
        t['v1'] = jnp.concatenate([p_ref[0, :, 2 * nq + h * DV:2 * nq + (h + 1) * DV].astype(BF16), ones], axis=1)
        ig_rep = g_rep[:, hs]
        b_rep = b_rep_all[:, hs]
        ig_row = g_t[h:h + 1, :]
        b_row = cum_t[H + h:H + h + 1, :]
        b_tot = b_rep[L - 1:L, :]
        m_prev = m_all[h:h + 1, :]
        t['c_prev'] = c_scr[h]
        t['n_prev'] = n_scr[h]
        t['m_new'] = jnp.maximum(b_tot + m_prev, jnp.max(b_tot - b_rep + ig_rep, axis=0, keepdims=True))
        t['dec'] = jnp.exp(b_tot + m_prev - t['m_new'])
        kw = k * jnp.exp((b_tot - b_rep + ig_rep - t['m_new'])[:, :DK])
        t['kvn'] = _dot_tn(kw.astype(BF16), t['v1'])
        inter = b_rep + m_prev
        t['m_i'] = b_rep + jnp.maximum(cm_all[:, hs], m_prev)
        t['e'] = jnp.exp(jnp.where(causal, (b_rep - t['m_i']) - b_row + ig_row, -jnp.inf))
        t['sc'] = jnp.exp(inter - t['m_i'])
        t['qk'] = _dot_nt(q, k.astype(BF16))
        cn = jnp.concatenate([t['c_prev'], t['n_prev']], axis=1).astype(BF16)
        t['qcn'] = _dot(q, cn)
    for h, t in enumerate(hd):
        nd = _dot((t['qk'] * t['e']).astype(BF16), t['v1'])
        num = nd[:, :DV] + t['sc'] * t['qcn'][:, :DV]
        den = nd[:, DV:] + t['sc'] * t['qcn'][:, DV:]
        hid = num / jnp.maximum(jnp.abs(den), jnp.exp(-t['m_i']))
        hid = hid * lax.rsqrt(_seg_sum(hid * hid, mean_w) + NORM_EPS)
        og = p_ref[0, :, 2 * nq + ML_W + h * DV:2 * nq + ML_W + (h + 1) * DV]
        o_ref[0, :, h * DV:(h + 1) * DV] = (jax.nn.sigmoid(og) * (hid * norm_ref[:, h * DV:(h + 1) * DV])).astype(BF16)
    for h, t in enumerate(hd):
        c_scr[h] = t['dec'] * t['c_prev'] + t['kvn'][:, :DV]
        n_scr[h] = t['dec'] * t['n_prev'] + t['kvn'][:, DV:]
        m_scr[h:h + 1, :] = t['m_new']

    @pl.when(ci == nc - 1)
    def _():
        cf_ref[0] = c_scr[...]
        nf_ref[0] = n_scr[...]
        mf_ref[0] = m_scr[...]


def _ml_prompt(p3, lp):
    b, t, _ = p3.shape
    L = CHUNK
    nc = t // L
    bias = jnp.pad(jnp.concatenate([lp['ml_ib'], lp['ml_fb']]), (0, LANES - 2 * ML_HEADS)).reshape(1, LANES)
    n_rep = 2 * ML_HEADS * LANES
    sel = (jnp.arange(LANES)[:, None] == jnp.arange(n_rep)[None, :] // LANES).astype(BF16)
    vs = pl.BlockSpec((1, L, ML_W), lambda i, j: (i, j, 0))
    cs = pl.BlockSpec((1, ML_HEADS, ML_QK, ML_V), lambda i, j: (i, 0, 0, 0))
    ns = pl.BlockSpec((1, ML_HEADS, ML_QK, LANES), lambda i, j: (i, 0, 0, 0))
    ms = pl.BlockSpec((1, ML_HPAD, LANES), lambda i, j: (i, 0, 0))
    o, c_f, n_f, m_f = pl.pallas_call(
        functools.partial(_ml_fused_body, nc=nc),
        grid=(b, nc),
        in_specs=[pl.BlockSpec((1, L, P_MAIN), lambda i, j: (i, j, P_ML // P_MAIN)),
                  pl.BlockSpec((1, L, LANES), lambda i, j: (i, j, P_GATE // LANES)),
                  pl.BlockSpec((1, LANES), lambda i, j: (0, 0)),
                  pl.BlockSpec((1, ML_W), lambda i, j: (0, 0)),
                  pl.BlockSpec((LANES, n_rep), lambda i, j: (0, 0))],
        out_specs=[vs, cs, ns, ms],
        out_shape=[jax.ShapeDtypeStruct((b, t, ML_W), BF16),
                   jax.ShapeDtypeStruct((b, ML_HEADS, ML_QK, ML_V), F32),
                   jax.ShapeDtypeStruct((b, ML_HEADS, ML_QK, LANES), F32),
                   jax.ShapeDtypeStruct((b, ML_HPAD, LANES), F32)],
        scratch_shapes=[pltpu.VMEM((ML_HEADS, ML_QK, ML_V), F32),
                        pltpu.VMEM((ML_HEADS, ML_QK, LANES), F32),
                        pltpu.VMEM((ML_HPAD, LANES), F32)],
        compiler_params=_cparams(("parallel", "arbitrary")),
        name="ml_fused",
    )(p3, p3, bias, lp['ml_norm'].reshape(1, ML_W), sel)
    return o, c_f, n_f[..., 0], m_f[:, :ML_HEADS, 0]


def _to_cols(x):
    b, c = x.shape
    cols = x.reshape(b // DEC_TB, DEC_TB, c).transpose(0, 2, 1)
    return jnp.pad(cols, ((0, 0), (0, 0), (0, DEC_TB)))


def _col_selector():
    j = jnp.arange(2 * DEC_TB)[:, None]
    return (j == jnp.arange(DEC_TB * LANES)[None, :] // LANES).astype(BF16)


def _rwkv_step_body(w_ref, ap_ref, k_ref, kk_ref, r_ref, v_ref, s_ref, y_ref, so_ref):
    N = RW_HD
    v = v_ref[0]

    def sa_step(i, acc):
        return acc + kk_ref[0, pl.ds(i, 1), :] * s_ref[i]

    sa = lax.fori_loop(0, N, sa_step, jnp.zeros_like(v), unroll=8)

    def upd_step(i, y):
        row = lambda ref: ref[0, pl.ds(i, 1), :]
        s_new = row(w_ref) * s_ref[i] - row(ap_ref) * sa + row(k_ref) * v
        so_ref[i] = s_new
        return y + row(r_ref) * s_new

    y_ref[0] = lax.fori_loop(0, N, upd_step, jnp.zeros_like(v), unroll=8)


def _rwkv_step(wdec, ap, k, kk, r, v, s_all, layer):
    b, w = v.shape
    heads = lambda x: x.reshape(b, RW_HEADS, RW_HD).transpose(1, 2, 0)
    vec = pl.BlockSpec((1, RW_HD, b), lambda h: (h, 0, 0))
    y, s_new = pl.pallas_call(
        _rwkv_step_body,
        grid=(RW_HEADS,),
        in_specs=[vec] * 6 + [pl.BlockSpec((None, None, RW_HD, RW_HD, b), lambda h: (layer, h, 0, 0, 0))],
        out_specs=[vec, pl.BlockSpec((None, RW_HD, RW_HD, b), lambda h: (h, 0, 0, 0))],
        out_shape=[jax.ShapeDtypeStruct((RW_HEADS, RW_HD, b), F32),
                   jax.ShapeDtypeStruct((RW_HEADS, RW_HD, RW_HD, b), F32)],
        compiler_params=_cparams(("parallel",)),
        name="rwkv_step",
    )(heads(wdec), heads(ap), heads(k), heads(kk), heads(r), heads(v), s_all)
    return y.transpose(2, 0, 1).reshape(b, w), s_new


def _ret_step_body(kc_ref, q_ref, k_ref, v_ref, esel_ref, seg_ref, s_ref, y_ref, so_ref):
    DK, DV = RET_QK, RET_V
    q = q_ref[...]
    qk_v = _exact_rhs_dot(q * k_ref[...], seg_ref[...])
    row_id = lax.broadcasted_iota(jnp.int32, (DEC_TB, 1), 0)
    for h in range(RET_HEADS):
        gamma = math.exp(_ret_log_gamma(h))
        ks = slice(h * DK, (h + 1) * DK)
        vs = slice(h * DV, (h + 1) * DV)
        k_rep = _exact_rhs_dot(kc_ref[0, ks, :], esel_ref[...])
        q_h = q[:, ks].astype(BF16)
        v_h = v_ref[:, vs]
        qs = jnp.zeros((DEC_TB, DV), F32)
        for j in range(DEC_TB):
            s = s_ref[j, h]
            so_ref[j, h] = gamma * s + k_rep[:, j * LANES:(j + 1) * LANES] * v_h[j:j + 1, :]
            qs = jnp.where(row_id == j, _dot(q_h, s.astype(BF16)), qs)
        y_ref[:, vs] = qk_v[:, vs] * v_h + gamma * qs


def _ret_step(q, k, v, s_all, layer):
    b = q.shape[0]
    nqk = RET_HEADS * RET_QK
    cols = pl.BlockSpec((1, nqk, 2 * DEC_TB), lambda i: (i, 0, 0))
    qk_rows = pl.BlockSpec((DEC_TB, nqk), lambda i: (i, 0))
    rows = pl.BlockSpec((DEC_TB, RET_W), lambda i: (i, 0))
    esel = pl.BlockSpec((2 * DEC_TB, DEC_TB * LANES), lambda i: (0, 0))
    seg = (jnp.arange(nqk)[:, None] // RET_QK == jnp.arange(RET_W)[None, :] // RET_V).astype(BF16)
    st = pl.BlockSpec((DEC_TB, RET_HEADS, RET_QK, RET_V), lambda i: (i, 0, 0, 0))
    st_in = pl.BlockSpec((None, DEC_TB, RET_HEADS, RET_QK, RET_V), lambda i: (layer, i, 0, 0, 0))
    return pl.pallas_call(
        _ret_step_body,
        grid=(b // DEC_TB,),
        in_specs=[cols, qk_rows, qk_rows, rows, esel, pl.BlockSpec((nqk, RET_W), lambda i: (0, 0)), st_in],
        out_specs=[rows, st],
        out_shape=[jax.ShapeDtypeStruct((b, RET_W), F32), jax.ShapeDtypeStruct(s_all.shape[1:], F32)],
        compiler_params=_cparams(("parallel",)),
        name="ret_step",
    )(_to_cols(k), q, k, v, _col_selector(), seg, s_all)


def _ml_step_body(kc_ref, q_ref, k_ref, v_ref, ig_ref, lf_ref, esel_ref, hsel_ref, hsel_k_ref, seg_ref,
                  c_ref, n_ref, m_ref, h_ref, co_ref, no_ref, mo_ref):
    DK, DV, H = ML_QK, ML_V, ML_HEADS
    ig, lf, m_prev = ig_ref[...], lf_ref[...], m_ref[...]
    m_new = jnp.maximum(lf + m_prev, ig)
    dec = jnp.exp(lf + m_prev - m_new)
    wgt = jnp.exp(ig - m_new)
    mo_ref[...] = m_new
    hsel = hsel_ref[...]
    dec_v = _exact_rhs_dot(dec, hsel)
    wgt_v = _exact_rhs_dot(wgt, hsel)
    floor_v = _exact_rhs_dot(jnp.exp(-m_new), hsel)
    q, k, n_prev = q_ref[...], k_ref[...], n_ref[...]
    no_ref[...] = _exact_rhs_dot(dec, hsel_k_ref[...]) * n_prev + k * _exact_rhs_dot(wgt, hsel_k_ref[...])
    s_v = _exact_rhs_dot(q * k, seg_ref[...]) * wgt_v
    den_v = s_v + dec_v * _exact_rhs_dot(q * n_prev, seg_ref[...])
    row_id = lax.broadcasted_iota(jnp.int32, (DEC_TB, 1), 0)
    for h in range(H):
        ks = slice(h * DK, (h + 1) * DK)
        vs = slice(h * DV, (h + 1) * DV)
        k_rep = _exact_rhs_dot(kc_ref[0, ks, :], esel_ref[...])
        q_h = q[:, ks].astype(BF16)
        v_h = v_ref[:, vs]
        kv_scale = wgt_v[:, vs] * v_h
        qc = jnp.zeros((DEC_TB, DV), F32)
        for j in range(DEC_TB):
            c_prev = c_ref[j, h]
            co_ref[j, h] = dec_v[j:j + 1, vs] * c_prev + k_rep[:, j * LANES:(j + 1) * LANES] * kv_scale[j:j + 1, :]
            qc = jnp.where(row_id == j, _dot(q_h, c_prev.astype(BF16)), qc)
        num = s_v[:, vs] * v_h + dec_v[:, vs] * qc
        h_ref[:, vs] = num / jnp.maximum(jnp.abs(den_v[:, vs]), floor_v[:, vs])


def _ml_step(q, k, v, ig, lf, c_all, layer, n0, m0):
    b = q.shape[0]
    nqk = ML_HEADS * ML_QK
    pad_h = lambda x: jnp.pad(x, ((0, 0), (0, LANES - ML_HEADS)))
    head = jnp.arange(LANES)[:, None]
    hsel = (head == jnp.arange(ML_W)[None, :] // ML_V).astype(BF16)
    hsel_k = (head == jnp.arange(nqk)[None, :] // ML_QK).astype(BF16)
    seg = (jnp.arange(nqk)[:, None] // ML_QK == jnp.arange(ML_W)[None, :] // ML_V).astype(BF16)
    const = lambda r, c: pl.BlockSpec((r, c), lambda i: (0, 0))
    cols = pl.BlockSpec((1, nqk, 2 * DEC_TB), lambda i: (i, 0, 0))
    qk_rows = pl.BlockSpec((DEC_TB, nqk), lambda i: (i, 0))
    rows = pl.BlockSpec((DEC_TB, ML_W), lambda i: (i, 0))
    sc = pl.BlockSpec((DEC_TB, LANES), lambda i: (i, 0))
    cs = pl.BlockSpec((DEC_TB, ML_HEADS, ML_QK, ML_V), lambda i: (i, 0, 0, 0))
    cs_in = pl.BlockSpec((None, DEC_TB, ML_HEADS, ML_QK, ML_V), lambda i: (layer, i, 0, 0, 0))
    hm, c_new, n_new, m_new = pl.pallas_call(
        _ml_step_body,
        grid=(b // DEC_TB,),
        in_specs=[cols, qk_rows, qk_rows, rows, sc, sc, const(2 * DEC_TB, DEC_TB * LANES), const(LANES, ML_W),
                  const(LANES, nqk), const(nqk, ML_W), cs_in, qk_rows, sc],
        out_specs=[rows, cs, qk_rows, sc],
        out_shape=[jax.ShapeDtypeStruct((b, ML_W), F32), jax.ShapeDtypeStruct(c_all.shape[1:], F32),
                   jax.ShapeDtypeStruct((b, nqk), F32), jax.ShapeDtypeStruct((b, LANES), F32)],
        compiler_params=_cparams(("parallel",)),
        name="ml_step",
    )(_to_cols(k), q, k, v, pad_h(ig), pad_h(lf), _col_selector(), hsel, hsel_k, seg,
      c_all, n0.reshape(b, nqk), pad_h(m0))
    return hm, c_new, n_new.reshape(n0.shape), m_new[:, :ML_HEADS]


def _heads(a, h):
    return a.reshape(a.shape[:-1] + (h, a.shape[-1] // h))


def _shift_prev(p, prev_row):
    return jnp.concatenate([prev_row[:, None, :], p[:, :-1]], axis=1)


def _rotary(x, pos):
    half = x.shape[-1] // 2
    inv = ROPE_BASE ** (-jnp.arange(half, dtype=F32) / half)
    ang = pos.astype(F32)[:, None] * inv[None, :]
    cos = jnp.cos(ang)[None, :, None, :]
    sin = jnp.sin(ang)[None, :, None, :]
    x1, x2 = x[..., :half], x[..., half:]
    return jnp.concatenate([x1 * cos - x2 * sin, x1 * sin + x2 * cos], -1)


def _small_matmul(x, w):
    lead = x.shape[:-1]
    kdim, n = w.shape
    x2 = x.reshape(-1, kdim)
    m = x2.shape[0]
    kp = -(-kdim // LANES) * LANES
    npad = -(-n // LANES) * LANES
    x2 = jnp.pad(x2.astype(BF16), ((0, 0), (0, kp - kdim)))
    w2 = jnp.pad(w.astype(BF16), ((0, kp - kdim), (0, npad - n)))
    tm = 1024 if m % 1024 == 0 else m
    out = _matmul(x2, w2, tm, npad)
    return out[:, :n].reshape(lead + (n,))


def _mix_prompt(p, pos, v_first, lp):
    o_rw, v_first, s_new = _rwkv_prompt(p, lp, v_first, (1, 1, 1, 1))
    o_ret, r_new = _ret_prompt(p, pos)
    o_ml, c_new, n_new, m_new = _ml_prompt(p, lp)
    return (o_rw, o_ret, o_ml), v_first, (s_new, r_new, c_new, n_new, m_new)


def _mix_sample(p, pos, v_first, st, lp, prev_row):
    bsz, t_len, _ = p.shape

    p_rw = p[..., :RW_P]
    mixed = p_rw + (_shift_prev(p_rw, prev_row[:, :RW_P]) - p_rw) * lp['rw_mu']
    sizes = np.cumsum([RW_W, RW_W, RW_W, RW_LORA_W, RW_LORA_A, RW_LORA_G])[:-1]
    r, k, v, xw, xa, xg = jnp.split(mixed, [int(s) for s in sizes], axis=-1)
    w = -jax.nn.softplus(-(lp['rw_w0'] + _small_matmul(jnp.tanh(xw), lp['rw_w2']))) - 0.5
    a = jax.nn.sigmoid(lp['rw_a0'] + _small_matmul(xa, lp['rw_a2']))
    g = _small_matmul(jax.nn.sigmoid(xg), lp['rw_g2'])
    if v_first is None:
        v_first = v
    else:
        pv = p[..., P_V1:P_V1 + RW_LORA_V]
        xv = pv + (_shift_prev(pv, prev_row[:, P_V1:P_V1 + RW_LORA_V]) - pv) * lp['rw_vmu']
        v = v + (v_first - v) * jax.nn.sigmoid(lp['rw_v0'] + _small_matmul(xv, lp['rw_v2']))
    kk = _heads(k * lp['rw_kk'], RW_HEADS)
    kk = kk * lax.rsqrt(jnp.maximum(jnp.sum(jnp.square(kk), -1, keepdims=True), 1e-24))
    kk = kk.reshape(bsz, t_len, RW_W)
    k = k * (1.0 + (a - 1.0) * lp['rw_ka'])
    lw = -jnp.exp(w)
    y, s_new = _rwkv_step(jnp.exp(lw)[:, 0], (kk * a)[:, 0], k[:, 0], kk[:, 0], r[:, 0], v[:, 0],
                          st['rw_wkv_t'], st['layer'])
    y = _heads(y[:, None, :], RW_HEADS)
    y_mu = jnp.mean(y, -1, keepdims=True)
    y_var = jnp.mean(jnp.square(y - y_mu), -1, keepdims=True)
    y = ((y - y_mu) * lax.rsqrt(y_var + RW_GN_EPS)).reshape(bsz, t_len, RW_W)
    y = y * lp['rw_lnx_g'] + lp['rw_lnx_b']
    rh, kh, vh = (_heads(u, RW_HEADS) for u in (r, k, v))
    bonus = jnp.sum(rh * kh * _heads(lp['rw_rk'], RW_HEADS), -1, keepdims=True) * vh
    o_rw = ((y + bonus.reshape(bsz, t_len, RW_W)) * g).astype(BF16)

    nqk = RET_HEADS * RET_QK
    p_ret = p[..., P_RET:P_RET + P_MAIN]
    qr, kr, vr, gr = (p_ret[..., :nqk], p_ret[..., nqk:2 * nqk],
                      p_ret[..., 2 * nqk:2 * nqk + RET_W], p_ret[..., 2 * nqk + RET_W:])
    qh = _rotary(_heads(qr, RET_HEADS), pos).reshape(bsz, t_len, nqk)
    khr = (_rotary(_heads(kr, RET_HEADS), pos) * (RET_QK ** -0.5)).reshape(bsz, t_len, nqk)
    yr, r_new = _ret_step(qh[:, 0], khr[:, 0], vr[:, 0], st['ret_all'], st['layer'])
    yr = _heads(yr[:, None, :], RET_HEADS)
    yr = yr * lax.rsqrt(jnp.mean(jnp.square(yr), -1, keepdims=True) + NORM_EPS)
    o_ret = (jax.nn.silu(gr) * yr.reshape(bsz, t_len, RET_W)).astype(BF16)

    nqk = ML_HEADS * ML_QK
    p_ml = p[..., P_ML:P_ML + P_MAIN]
    qm, km, vm, om = (p_ml[..., :nqk], p_ml[..., nqk:2 * nqk],
                      p_ml[..., 2 * nqk:2 * nqk + ML_W], p_ml[..., 2 * nqk + ML_W:])
    im = p[..., P_GATE:P_GATE + ML_HEADS]
    fm = p[..., P_GATE + ML_HEADS:P_GATE + 2 * ML_HEADS]
    ig = ML_GATE_CAP * jnp.tanh((im + lp['ml_ib']) / ML_GATE_CAP)
    lf = jax.nn.log_sigmoid(ML_GATE_CAP * jnp.tanh((fm + lp['ml_fb']) / ML_GATE_CAP))
    km = km * (ML_QK ** -0.5)
    hm, c_new, n_new, m_new = _ml_step(qm[:, 0], km[:, 0], vm[:, 0], ig[:, 0], lf[:, 0],
                                       st['ml_c_all'], st['layer'], st['ml_n'], st['ml_m'])
    hm = _heads(hm[:, None, :], ML_HEADS)
    hm = hm * lax.rsqrt(jnp.mean(jnp.square(hm), -1, keepdims=True) + NORM_EPS)
    o_ml = (jax.nn.sigmoid(om) * (hm.reshape(bsz, t_len, ML_W) * lp['ml_norm'])).astype(BF16)

    return (o_rw, o_ret, o_ml), v_first, (s_new, r_new, c_new, n_new, m_new)


_T_V1 = P_V1 // LANES
_T_GATE = P_GATE // LANES
_T_RET = P_RET // LANES
_T_SHIFT = (P_RET - RW_P) // LANES
_T_SRC_GATE = (RW_P + RET_P + 2 * ML_HEADS * ML_QK + 2 * ML_W) // LANES


def _pack_body(w_ref, v1_ref, *o_refs):
    j = pl.program_id(0)
    row = lax.broadcasted_iota(jnp.int32, (LANES, 1), 0)
    spare = jnp.logical_and(j > _T_GATE, j < _T_RET)
    for l, o_ref in enumerate(o_refs):
        w = w_ref[:, l, :]
        gates = jnp.where(row < 2 * ML_HEADS, w, 0.0)
        out = jnp.where(j == _T_V1, v1_ref[l], jnp.where(j == _T_GATE, gates, jnp.where(spare, 0.0, w)))
        o_ref[...] = out.astype(BF16)


def _pack_w_in(w_in, rw_v1):
    depth, d, _ = w_in.shape
    w_t = w_in.transpose(2, 0, 1)
    v1_t = jnp.pad(rw_v1.transpose(0, 2, 1), ((1, 0), (0, LANES - RW_LORA_V), (0, 0)))

    def src_tile(j):
        return jnp.where(j < _T_V1, j, jnp.where(j == _T_GATE, _T_SRC_GATE, j - _T_SHIFT))

    return pl.pallas_call(
        _pack_body,
        grid=(P_PAD // LANES,),
        in_specs=[pl.BlockSpec((LANES, depth, d), lambda j: (src_tile(j), 0, 0)),
                  pl.BlockSpec((depth, LANES, d), lambda j: (0, 0, 0))],
        out_specs=[pl.BlockSpec((LANES, d), lambda j: (j, 0))] * depth,
        out_shape=[jax.ShapeDtypeStruct((P_PAD, d), BF16)] * depth,
        compiler_params=_cparams(("parallel",)),
        name="pack_w_in",
    )(w_t, v1_t)


def _token_tiles(m):
    if m % 2048 == 0:
        return 2048, 512, 512
    return m, m, m


def kernel(x_prompt, x_sample, state_rw_shift, state_rw_wkv, state_ret, state_ml_c, state_ml_n, state_ml_m,
           ln0_g, ln0_b, w_in, rw_mu, rw_w0, rw_w2, rw_a0, rw_a2, rw_g2, rw_kk, rw_ka, rw_rk,
           rw_lnx_g, rw_lnx_b, rw_v0, rw_v1, rw_vmu, rw_v2, ml_ib, ml_fb, ml_norm, w_out,
           ln1_g, ln1_b, w_gate, w_up, w_down, ln2_g, ln2_b):
    bp, tp, d = x_prompt.shape
    bs, ts, _ = x_sample.shape
    mp, ms = bp * tp, bs * ts
    pos_p = jnp.arange(tp)
    pos_s = PAST_LEN + jnp.arange(ts)
    tm_big, tm_out, tm_down = _token_tiles(mp)
    rw_wkv_t = state_rw_wkv.transpose(0, 2, 3, 4, 1)
    w_in_packed = _pack_w_in(w_in, rw_v1)
    xf_p, xb_p = _layernorm(x_prompt.reshape(mp, d), ln0_g, ln0_b, tm_out)
    xf_s, xb_s = _layernorm(x_sample.reshape(ms, d), ln0_g, ln0_b, ms)
    vf_p = vf_s = None
    outs_p, outs_s = [], []

    for l in range(DEPTH):
        lp = {
            'rw_mu': rw_mu[l], 'rw_w0': rw_w0[l], 'rw_w2': rw_w2[l], 'rw_a0': rw_a0[l], 'rw_a2': rw_a2[l],
            'rw_g2': rw_g2[l], 'rw_kk': rw_kk[l], 'rw_ka': rw_ka[l], 'rw_rk': rw_rk[l],
            'rw_lnx_g': rw_lnx_g[l], 'rw_lnx_b': rw_lnx_b[l], 'ml_ib': ml_ib[l], 'ml_fb': ml_fb[l],
            'ml_norm': ml_norm[l],
        }
        if l > 0:
            lp.update(rw_v0=rw_v0[l - 1], rw_vmu=rw_vmu[l - 1], rw_v2=rw_v2[l - 1])
        w_out_b = w_out[l].astype(BF16)
        w_down_b = w_down[l].astype(BF16)
        x_side = jnp.concatenate([xb_s, state_rw_shift[l].astype(BF16)], axis=0)
        p_p, p_side = _in_proj(xb_p, x_side, w_in_packed[l], tm_big, P_TN)
        o_p, vf_p, st_p = _mix_prompt(p_p.reshape(bp, tp, P_PAD), pos_p, vf_p, lp)
        st = {'rw_wkv_t': rw_wkv_t, 'layer': l, 'ret_all': state_ret, 'ml_c_all': state_ml_c,
              'ml_n': state_ml_n[l], 'ml_m': state_ml_m[l]}
        o_s, vf_s, st_s = _mix_sample(p_side[:ms].reshape(bs, ts, P_PAD), pos_s, vf_s, st, lp, p_side[ms:])
        outs_p.append((xf_p.reshape(bp, tp, d)[:, -1],) + st_p)
        outs_s.append((xf_s.reshape(bs, ts, d)[:, -1],) + st_s)
        flat = lambda o, m: tuple(u.reshape(m, u.shape[-1]) for u in o)
        x1f_p, x1b_p, x1f_s, x1b_s = _out_proj_ln(flat(o_p, mp), xf_p, flat(o_s, ms), xf_s, w_out_b,
                                                  ln1_g[l], ln1_b[l], tm_out)
        hdn_p, hdn_s = _matmul_swiglu(x1b_p, x1b_s, w_gate, w_up, l, tm_big, 512)
        xf_p, xb_p, xf_s, xb_s = _matmul_res_ln(hdn_p, hdn_s, w_down_b, x1f_p, x1f_s, ln2_g[l], ln2_b[l],
                                                tm_down, DOWN_TK)

    y_p = xf_p.reshape(bp, tp, d)
    y_s = xf_s.reshape(bs, ts, d)
    sp = [jnp.stack([o[i] for o in outs_p]) for i in range(6)]
    ss = [jnp.stack([o[i] for o in outs_s]) for i in range(6)]
    ss[1] = ss[1].transpose(0, 4, 1, 2, 3)
    return (y_p, y_s, sp[0], sp[1], sp[2], sp[3], sp[4], sp[5], ss[0], ss[1], ss[2], ss[3], ss[4], ss[5])
```

```python
import functools
import math

import numpy as np
import jax
import jax.numpy as jnp
from jax import lax
from jax.experimental import pallas as pl
from jax.experimental.pallas import tpu as pltpu

F32 = jnp.float32
BF16 = jnp.bfloat16

D_MODEL = 2048
DEPTH = 2
PAST_LEN = 16384
RW_HD = 64
RW_W = D_MODEL // 4
RW_HEADS = RW_W // RW_HD
RW_LORA_W = 64
RW_LORA_A = 64
RW_LORA_V = 32
RW_LORA_G = 128
RW_P = 3 * RW_W + RW_LORA_W + RW_LORA_A + RW_LORA_G
RW_GN_EPS = 64e-5
RET_V = 128
RET_QK = 64
RET_W = 3 * D_MODEL // 8
RET_HEADS = RET_W // RET_V
RET_P = 2 * RET_HEADS * RET_QK + 2 * RET_W
ML_V = 128
ML_QK = 64
ML_W = D_MODEL - RW_W - RET_W
ML_HEADS = ML_W // ML_V
ML_P = 2 * ML_HEADS * ML_QK + 2 * ML_W + 2 * ML_HEADS
ML_GATE_CAP = 15.0
P_TOTAL = RW_P + RET_P + ML_P
D_FF = ((8 * D_MODEL + 3 * 256 - 1) // (3 * 256)) * 256
CHUNK = 128
ROPE_BASE = 10000.0
LN_EPS = 1e-5
NORM_EPS = 1e-6
ALPHA = (2 * DEPTH) ** 0.25

LANES = 128
P_V1 = RW_P
P_GATE = RW_P + LANES
P_RET = 2304
P_ML = 2 * P_RET
P_MAIN = 2304
P_PAD = 3 * P_RET
P_TN = 768
RW_CHUNK = 64
DEC_TB = 8
VMEM_LIMIT = 56 * 1024 * 1024

def _cparams(sem):
    return pltpu.CompilerParams(dimension_semantics=sem, vmem_limit_bytes=VMEM_LIMIT)


def _dot(a, b):
    return lax.dot_general(a, b, (((1,), (0,)), ((), ())), preferred_element_type=F32)


def _dot_nt(a, b):
    return lax.dot_general(a, b, (((1,), (1,)), ((), ())), preferred_element_type=F32)


def _dot_tn(a, b):
    return lax.dot_general(a, b, (((0,), (0,)), ((), ())), preferred_element_type=F32)


def _ln_rows(x, g, b):
    mu = jnp.mean(x, -1, keepdims=True)
    xc = x - mu
    var = jnp.mean(xc * xc, -1, keepdims=True)
    return xc * lax.rsqrt(var + LN_EPS) * g + b


def _ln_body(x_ref, g_ref, b_ref, of_ref, ob_ref):
    y = _ln_rows(x_ref[...], g_ref[...], b_ref[...])
    of_ref[...] = y
    ob_ref[...] = y.astype(BF16)


def _layernorm(x, g, b, tm):
    m, d = x.shape
    return pl.pallas_call(
        _ln_body,
        grid=(m // tm,),
        in_specs=[pl.BlockSpec((tm, d), lambda i: (i, 0)),
                  pl.BlockSpec((1, d), lambda i: (0, 0)),
                  pl.BlockSpec((1, d), lambda i: (0, 0))],
        out_specs=[pl.BlockSpec((tm, d), lambda i: (i, 0)),
                   pl.BlockSpec((tm, d), lambda i: (i, 0))],
        out_shape=[jax.ShapeDtypeStruct((m, d), F32), jax.ShapeDtypeStruct((m, d), BF16)],
        compiler_params=_cparams(("parallel",)),
        name="layernorm",
    )(x, g.reshape(1, d), b.reshape(1, d))


def _mm_body(x_ref, w_ref, o_ref, *, w_transposed):
    dot = _dot_nt if w_transposed else _dot
    o_ref[...] = dot(x_ref[...], w_ref[...]).astype(o_ref.dtype)


def _matmul(x, w, tm, tn, out_dtype=F32, w_transposed=False):
    m, k = x.shape
    n = w.shape[0] if w_transposed else w.shape[1]
    w_spec = (pl.BlockSpec((tn, k), lambda i, j: (j, 0)) if w_transposed
              else pl.BlockSpec((k, tn), lambda i, j: (0, j)))
    return pl.pallas_call(
        functools.partial(_mm_body, w_transposed=w_transposed),
        grid=(m // tm, n // tn),
        in_specs=[pl.BlockSpec((tm, k), lambda i, j: (i, 0)), w_spec],
        out_specs=pl.BlockSpec((tm, tn), lambda i, j: (i, j)),
        out_shape=jax.ShapeDtypeStruct((m, n), out_dtype),
        compiler_params=_cparams(("parallel", "parallel")),
        name="matmul",
    )(x, w)


def _in_proj_body(x_ref, xs_ref, w_ref, o_ref, os_ref):
    @pl.when(pl.program_id(1) == 0)
    def _():
        os_ref[...] = _dot_nt(xs_ref[...], w_ref[...])

    o_ref[...] = _dot_nt(x_ref[...], w_ref[...])


def _in_proj(x, x_side, w_t, tm, tn):
    m, k = x.shape
    ms = x_side.shape[0]
    n = w_t.shape[0]
    return pl.pallas_call(
        _in_proj_body,
        grid=(n // tn, m // tm),
        in_specs=[pl.BlockSpec((tm, k), lambda j, i: (i, 0)), pl.BlockSpec((ms, k), lambda j, i: (0, 0)),
                  pl.BlockSpec((tn, k), lambda j, i: (j, 0))],
        out_specs=[pl.BlockSpec((tm, tn), lambda j, i: (i, j)), pl.BlockSpec((ms, tn), lambda j, i: (0, j))],
        out_shape=[jax.ShapeDtypeStruct((m, n), F32), jax.ShapeDtypeStruct((ms, n), F32)],
        compiler_params=_cparams(("parallel", "arbitrary")),
        name="in_proj",
    )(x, x_side, w_t)


def _swiglu_body(x_ref, xs_ref, wg_ref, wu_ref, o_ref, os_ref, wg_scr, wu_scr):
    def act(x):
        g = _dot(x, wg_scr[...])
        return (g * jax.nn.sigmoid(g) * _dot(x, wu_scr[...])).astype(BF16)

    @pl.when(pl.program_id(1) == 0)
    def _():
        wg_scr[...] = wg_ref[...].astype(BF16)
        wu_scr[...] = wu_ref[...].astype(BF16)
        os_ref[...] = act(xs_ref[...])

    half = x_ref.shape[0] // 2
    for r in range(2):
        rows = pl.ds(r * half, half)
        o_ref[rows, :] = act(x_ref[rows, :])


def _matmul_swiglu(x, x_side, wg, wu, layer, tm, tn):
    m, k = x.shape
    ms = x_side.shape[0]
    n = wg.shape[2]
    w_spec = pl.BlockSpec((None, k, tn), lambda j, i: (layer, 0, j))
    return pl.pallas_call(
        _swiglu_body,
        grid=(n // tn, m // tm),
        in_specs=[pl.BlockSpec((tm, k), lambda j, i: (i, 0)), pl.BlockSpec((ms, k), lambda j, i: (0, 0)),
                  w_spec, w_spec],
        out_specs=[pl.BlockSpec((tm, tn), lambda j, i: (i, j)), pl.BlockSpec((ms, tn), lambda j, i: (0, j))],
        out_shape=[jax.ShapeDtypeStruct((m, n), BF16), jax.ShapeDtypeStruct((ms, n), BF16)],
        scratch_shapes=[pltpu.VMEM((k, tn), BF16), pltpu.VMEM((k, tn), BF16)],
        compiler_params=_cparams(("parallel", "arbitrary")),
        name="matmul_swiglu",
    )(x, x_side, wg, wu)


DOWN_TK = D_FF // 4
LN_ROWS = 256
OUT_ROWS = 256


def _res_ln_store(acc_ref, res_ref, g_ref, b_ref, of_ref, ob_ref, n_rows):
    step = min(LN_ROWS, n_rows)
    for r in range(0, n_rows, step):
        rows = pl.ds(r, step)
        y = _ln_rows(ALPHA * res_ref[rows, :] + acc_ref[rows, :], g_ref[...], b_ref[...])
        of_ref[rows, :] = y
        ob_ref[rows, :] = y.astype(BF16)


def _mm_res_ln_body(x_ref, xs_ref, w_ref, res_ref, ress_ref, g_ref, b_ref, of_ref, ob_ref, ofs_ref, obs_ref, *, nk, tm, ms):
    i = pl.program_id(0)
    kk = pl.program_id(1)

    @pl.when(kk == 0)
    def _():
        of_ref[...] = jnp.zeros_like(of_ref)

    of_ref[...] += _dot(x_ref[...], w_ref[...])

    @pl.when(kk == nk - 1)
    def _():
        _res_ln_store(of_ref, res_ref, g_ref, b_ref, of_ref, ob_ref, tm)

    @pl.when(i == 0)
    def _():
        @pl.when(kk == 0)
        def _():
            ofs_ref[...] = jnp.zeros_like(ofs_ref)

        ofs_ref[...] += _dot(xs_ref[...], w_ref[...])

        @pl.when(kk == nk - 1)
        def _():
            _res_ln_store(ofs_ref, ress_ref, g_ref, b_ref, ofs_ref, obs_ref, ms)


def _matmul_res_ln(x, x_side, w, res, res_side, g, b, tm, tk):
    m, k = x.shape
    ms = x_side.shape[0]
    n = w.shape[1]
    nk = k // tk
    const = lambda r: pl.BlockSpec((r, n), lambda i, j: (0, 0))
    main = pl.BlockSpec((tm, n), lambda i, j: (i, 0))
    return pl.pallas_call(
        functools.partial(_mm_res_ln_body, nk=nk, tm=tm, ms=ms),
        grid=(m // tm, nk),
        in_specs=[pl.BlockSpec((tm, tk), lambda i, j: (i, j)), pl.BlockSpec((ms, tk), lambda i, j: (0, j)),
                  pl.BlockSpec((tk, n), lambda i, j: (j, 0)), main, const(ms), const(1), const(1)],
        out_specs=[main, main, const(ms), const(ms)],
        out_shape=[jax.ShapeDtypeStruct((m, n), F32), jax.ShapeDtypeStruct((m, n), BF16),
                   jax.ShapeDtypeStruct((ms, n), F32), jax.ShapeDtypeStruct((ms, n), BF16)],
        compiler_params=_cparams(("arbitrary", "arbitrary")),
        name="matmul_res_ln",
    )(x, x_side, w, res, res_side, g.reshape(1, n), b.reshape(1, n))


def _out_proj_ln_body(o_rw_ref, o_ret_ref, o_ml_ref, res_ref, s_rw_ref, s_ret_ref, s_ml_ref, ress_ref,
                      w_ref, g_ref, b_ref, of_ref, ob_ref, ofs_ref, obs_ref, *, tm, ms):
    def project(rw_ref, ret_ref, ml_ref, r_ref, f_ref, h_ref, n_rows):
        step = min(OUT_ROWS, n_rows)
        for r in range(0, n_rows, step):
            rows = pl.ds(r, step)
            mix = (_dot(rw_ref[rows, :], w_ref[0:RW_W, :])
                   + _dot(ret_ref[rows, :], w_ref[RW_W:RW_W + RET_W, :])
                   + _dot(ml_ref[rows, :], w_ref[RW_W + RET_W:, :]))
            y = _ln_rows(ALPHA * r_ref[rows, :] + mix, g_ref[...], b_ref[...])
            f_ref[rows, :] = y
            h_ref[rows, :] = y.astype(BF16)

    @pl.when(pl.program_id(0) == 0)
    def _():
        project(s_rw_ref, s_ret_ref, s_ml_ref, ress_ref, ofs_ref, obs_ref, ms)

    project(o_rw_ref, o_ret_ref, o_ml_ref, res_ref, of_ref, ob_ref, tm)


def _out_proj_ln(o, res, o_side, res_side, w, g, b, tm):
    m = res.shape[0]
    ms = res_side.shape[0]
    n = w.shape[1]
    rows = lambda width: pl.BlockSpec((tm, width), lambda i: (i, 0))
    const = lambda r, c: pl.BlockSpec((r, c), lambda i: (0, 0))
    widths = (RW_W, RET_W, ML_W)
    return pl.pallas_call(
        functools.partial(_out_proj_ln_body, tm=tm, ms=ms),
        grid=(m // tm,),
        in_specs=([rows(c) for c in widths] + [rows(n)] + [const(ms, c) for c in widths] + [const(ms, n)]
                  + [const(D_MODEL, n), const(1, n), const(1, n)]),
        out_specs=[rows(n), rows(n), const(ms, n), const(ms, n)],
        out_shape=[jax.ShapeDtypeStruct((m, n), F32), jax.ShapeDtypeStruct((m, n), BF16),
                   jax.ShapeDtypeStruct((ms, n), F32), jax.ShapeDtypeStruct((ms, n), BF16)],
        compiler_params=_cparams(("arbitrary",)),
        name="out_proj_ln",
    )(*o, res, *o_side, res_side, w, g.reshape(1, n), b.reshape(1, n))


RW_TB = 512
RW_GH = 4
RW_GW = RW_GH * RW_HD
RW_AHEAD = 8
RW_VEC_ROWS = 8


def _split3(x):
    hi = x.astype(BF16)
    r1 = x - hi.astype(F32)
    mid = r1.astype(BF16)
    lo = (r1 - mid.astype(F32)).astype(BF16)
    return hi, mid, lo


def _mm(a, b, dims):
    return lax.dot_general(a.astype(BF16), b.astype(BF16), (dims, ((), ())), preferred_element_type=F32)


_NN = ((1,), (0,))
_NT = ((1,), (1,))
_TN = ((0,), (0,))


def _exact_lhs_dot(a_bf16, b):
    hi, mid, lo = _split3(b)
    dg = lambda y: lax.dot_general(a_bf16, y, (_NN, ((), ())), preferred_element_type=F32)
    return dg(hi) + (dg(mid) + dg(lo))


def _exact_rhs_dot(a, b_bf16):
    hi, mid, lo = _split3(a)
    dg = lambda x: lax.dot_general(x, b_bf16, (_NN, ((), ())), preferred_element_type=F32)
    return dg(hi) + (dg(mid) + dg(lo))


def _seg_sum(a, seg_bf16):
    hi = a.astype(BF16)
    lo = (a - hi.astype(F32)).astype(BF16)
    dg = lambda x: lax.dot_general(x, seg_bf16, (_NN, ((), ())), preferred_element_type=F32)
    return dg(hi) + dg(lo)


def _rw_scan(r, lw, k, v, kk, a, st_scr, y_scr):
    L, TB, G = RW_CHUNK, RW_TB, RW_GW
    row = lax.broadcasted_iota(jnp.int32, (L, G), 0)
    col = lax.broadcasted_iota(jnp.int32, (L, G), 1) & (L - 1)
    strict, lower, eye = row > col, row >= col, (row == col).astype(F32)
    rg = lax.broadcasted_iota(jnp.int32, (G, G), 0) // RW_HD
    cg = lax.broadcasted_iota(jnp.int32, (G, G), 1) // RW_HD
    mask_bd = rg == cg
    rt = lax.broadcasted_iota(jnp.int32, (TB, TB), 0)
    ct = lax.broadcasted_iota(jnp.int32, (TB, TB), 1)
    tri = jnp.logical_and(rt >= ct, rt // L == ct // L).astype(BF16)
    bd = lambda x: jnp.where(mask_bd, jnp.concatenate([x.astype(BF16)] * RW_GH, axis=0), 0.0)
    cut = lambda x, b: x[b[0] * L:(b[0] + 1) * L, b[1] * G:(b[1] + 1) * G]

    cum = _exact_lhs_dot(tri, lw)
    e_neg = jnp.exp(-cum)
    ap = kk * a
    ap_h = ap * e_neg
    k_h = k * e_neg
    kk_t = kk * jnp.exp(cum - lw)
    r_t = r * jnp.exp(cum)

    lhs, n_m, m_a, m_kr, inv, mv, vk, a_end, decay = ({} for _ in range(9))
    n_groups = RW_HEADS // RW_GH

    def prep(chunks):
        blk = [(s, g) for s in chunks for g in range(n_groups)]
        for b in blk:
            lhs[b] = jnp.concatenate([cut(kk_t, b), cut(r_t, b)], axis=0)
            sc_a = _mm(lhs[b], bd(cut(ap_h, b)), _NT)
            sc_k = _mm(lhs[b], bd(cut(k_h, b)), _NT)
            n_m[b] = jnp.where(strict, sc_a[:L], 0.0)
            m_a[b] = jnp.where(lower, sc_a[L:], 0.0)
            m_kr[b] = jnp.concatenate([jnp.where(strict, sc_k[:L], 0.0), jnp.where(lower, sc_k[L:], 0.0)], axis=0)
        pw = {}
        for b in blk:
            inv[b] = eye - n_m[b]
            pw[b] = _mm(n_m[b], bd(n_m[b]), _NN)
        n_iter = int(math.log2(L)) - 1
        for j in range(n_iter):
            last = j == n_iter - 1
            for b in blk:
                lhs_j = inv[b] if last else jnp.concatenate([inv[b], pw[b]], axis=0)
                prod = _mm(lhs_j, bd(pw[b]), _NN)
                inv[b] = inv[b] + prod[:L]
                if not last:
                    pw[b] = prod[L:]
        for b in blk:
            s, g = b
            tot = cum[(s + 1) * L - 1:(s + 1) * L, g * G:(g + 1) * G]
            e_end = jnp.exp(tot - cut(cum, b))
            mv[b] = _mm(m_kr[b], bd(cut(v, b)), _NN)
            vk[b] = _mm(cut(v, b), cut(k, b) * e_end, _TN)
            a_end[b] = cut(ap, b) * e_end
            decay[b] = jnp.exp(tot)

    def apply(s):
        for g in range(n_groups):
            b = (s, g)
            st = st_scr[g]
            s_terms = _mm(lhs[b], st, _NT)
            u = _mm(inv[b], bd(s_terms[:L] + mv[b][:L]), _NN)
            y_scr[s * L:(s + 1) * L, g * G:(g + 1) * G] = s_terms[L:] + mv[b][L:] - _mm(m_a[b], bd(u), _NN)
            st_scr[g] = jnp.where(mask_bd, st * decay[b] + vk[b] - _mm(u, a_end[b], _TN), 0.0)

    n_chunks = TB // L
    prep(range(min(RW_AHEAD, n_chunks)))
    for s in range(n_chunks):
        apply(s)
        if s + RW_AHEAD < n_chunks:
            prep([s + RW_AHEAD])


def _softplus(z):
    return jnp.maximum(z, 0.0) + jnp.log(1.0 + jnp.exp(-jnp.abs(z)))


def _rwkv_fused_body(*refs, nc, has_vres):
    if has_vres:
        (p_ref, pv_ref, vf_ref, mu_ref, vec_ref, wa_ref, g2_ref, seg_ref, vmu_ref, v2_ref,
         o_ref, sf_ref, st_scr, prev_scr, y_scr, prevv_scr) = refs
    else:
        (p_ref, mu_ref, vec_ref, wa_ref, g2_ref, seg_ref,
         o_ref, vfo_ref, sf_ref, st_scr, prev_scr, y_scr) = refs
    TB, W, L = RW_TB, RW_W, RW_CHUNK
    c = pl.program_id(1)

    @pl.when(c == 0)
    def _():
        st_scr[...] = jnp.zeros_like(st_scr)
        prev_scr[...] = jnp.zeros_like(prev_scr)
        if has_vres:
            prevv_scr[...] = jnp.zeros_like(prevv_scr)

    first_row = lax.broadcasted_iota(jnp.int32, (TB, 1), 0) == 0

    def shift_mix(x, carry_ref, mu):
        prev = jnp.where(first_row, carry_ref[...], pltpu.roll(x, 1, 0))
        carry_ref[...] = x[TB - 1:TB, :]
        return x + (prev - x) * mu

    mixed = shift_mix(p_ref[0], prev_scr, mu_ref[...])
    r = mixed[:, 0:W]
    k = mixed[:, W:2 * W]
    v = mixed[:, 2 * W:3 * W]
    xwa = mixed[:, 3 * W:3 * W + LANES]
    xg = mixed[:, 3 * W + LANES:3 * W + 2 * LANES]
    vec = vec_ref[...]
    w0, a0, kk_s, ka, rk, lnx_g, lnx_b, v0 = (vec[i:i + 1, :] for i in range(RW_VEC_ROWS))
    seg = seg_ref[...]
    wa = wa_ref[...]
    w_lora = _dot(jnp.tanh(xwa).astype(BF16), wa[:, 0:W])
    a_lora = _dot(xwa.astype(BF16), wa[:, W:2 * W])
    lw = -jnp.exp(-_softplus(-(w0 + w_lora)) - 0.5)
    a = jax.nn.sigmoid(a0 + a_lora)
    g = _dot(jax.nn.sigmoid(xg).astype(BF16), g2_ref[...])
    if has_vres:
        xv = shift_mix(pv_ref[0], prevv_scr, vmu_ref[...])
        v = v + (vf_ref[0] - v) * jax.nn.sigmoid(v0 + _dot(xv.astype(BF16), v2_ref[...]))
    else:
        vfo_ref[0] = v
    kk = k * kk_s
    kk = kk * lax.rsqrt(jnp.maximum(_seg_sum(kk * kk, seg), 1e-24))
    k = k * (1.0 + (a - 1.0) * ka)

    _rw_scan(r, lw, k, v, kk, a, st_scr, y_scr)

    y = y_scr[...]
    inv_n = 1.0 / RW_HD
    y_mu = _seg_sum(y, seg) * inv_n
    yc = y - y_mu
    y_var = _seg_sum(yc * yc, seg) * inv_n
    y = yc * lax.rsqrt(y_var + RW_GN_EPS) * lnx_g + lnx_b
    bonus = _seg_sum(r * k * rk, seg) * v
    o_ref[0] = ((y + bonus) * g).astype(BF16)

    @pl.when(c == nc - 1)
    def _():
        sf_ref[0] = st_scr[...]


def _rwkv_prompt(p3, lp, v_first):
    b, t, _ = p3.shape
    nc = t // RW_TB
    has_vres = v_first is not None
    ng = RW_HEADS // RW_GH
    zpad = jnp.zeros((RW_LORA_W, RW_W), F32)
    wa = jnp.concatenate([jnp.concatenate([lp['rw_w2'], zpad], 0), jnp.concatenate([zpad, lp['rw_a2']], 0)], 1)
    vec = jnp.stack([lp['rw_w0'], lp['rw_a0'], lp['rw_kk'], lp['rw_ka'], lp['rw_rk'], lp['rw_lnx_g'], lp['rw_lnx_b'],
                     lp['rw_v0'] if has_vres else jnp.zeros((RW_W,), F32)])
    hid = jnp.arange(RW_W) // RW_HD
    seg = (hid[:, None] == hid[None, :]).astype(BF16)
    full = lambda shape: pl.BlockSpec(shape, lambda i, j: (0,) * len(shape))
    seq = lambda w, blk: pl.BlockSpec((1, RW_TB, w), lambda i, j: (i, j, blk))
    in_specs = [seq(RW_P, 0)]
    args = [p3]
    if has_vres:
        in_specs += [seq(LANES, P_V1 // LANES), seq(RW_W, 0)]
        args += [p3, v_first]
    in_specs += [full((1, RW_P)), full((RW_VEC_ROWS, RW_W)), full((LANES, 2 * RW_W)), full((RW_LORA_G, RW_W)),
                 full((RW_W, RW_W))]
    args += [lp['rw_mu'].reshape(1, RW_P), vec, wa.astype(BF16), lp['rw_g2'].astype(BF16), seg]
    if has_vres:
        in_specs += [full((1, LANES)), full((LANES, RW_W))]
        args += [jnp.pad(lp['rw_vmu'], (0, LANES - RW_LORA_V)).reshape(1, LANES),
                 jnp.pad(lp['rw_v2'], ((0, LANES - RW_LORA_V), (0, 0))).astype(BF16)]
    out_specs = [seq(RW_W, 0)]
    out_shape = [jax.ShapeDtypeStruct((b, t, RW_W), BF16)]
    if not has_vres:
        out_specs.append(seq(RW_W, 0))
        out_shape.append(jax.ShapeDtypeStruct((b, t, RW_W), F32))
    out_specs.append(pl.BlockSpec((1, ng, RW_GW, RW_GW), lambda i, j: (i, 0, 0, 0)))
    out_shape.append(jax.ShapeDtypeStruct((b, ng, RW_GW, RW_GW), F32))
    scratch = [pltpu.VMEM((ng, RW_GW, RW_GW), F32), pltpu.VMEM((1, RW_P), F32), pltpu.VMEM((RW_TB, RW_W), F32)]
    if has_vres:
        scratch.append(pltpu.VMEM((1, LANES), F32))
    outs = pl.pallas_call(
        functools.partial(_rwkv_fused_body, nc=nc, has_vres=has_vres),
        grid=(b, nc),
        in_specs=in_specs,
        out_specs=out_specs,
        out_shape=out_shape,
        scratch_shapes=scratch,
        compiler_params=_cparams(("parallel", "arbitrary")),
        name="rwkv_fused",
    )(*args)
    if has_vres:
        o, st_bd = outs
    else:
        o, v_first, st_bd = outs
    st5 = st_bd.reshape(b, ng, RW_GH, RW_HD, RW_GH, RW_HD)
    s_fin = jnp.stack([st5[:, :, h, :, h, :] for h in range(RW_GH)], axis=2)
    s_fin = s_fin.reshape(b, RW_HEADS, RW_HD, RW_HD).transpose(0, 1, 3, 2)
    return o, v_first, s_fin


def _ret_log_gamma(h):
    return math.log1p(-(2.0 ** (-5.0 - h)))


def _rotary_tables(pos, heads, dk):
    half = dk // 2
    inv = ROPE_BASE ** (-jnp.arange(half, dtype=F32) / half)
    ang = pos.astype(F32)[:, None] * inv[None, :]
    cos = jnp.tile(jnp.concatenate([jnp.cos(ang), jnp.cos(ang)], -1), (1, heads))
    sin = jnp.tile(jnp.concatenate([-jnp.sin(ang), jnp.sin(ang)], -1), (1, heads))
    lane = jnp.arange(heads * dk)
    perm = (lane[:, None] == (lane[None, :] ^ half)).astype(BF16)
    return cos, sin, perm


def _ret_fused_body(p_ref, cos_ref, sin_ref, perm_ref, o_ref, sf_ref, s_scr, *, nc):
    L, DK, DV, H = CHUNK, RET_QK, RET_V, RET_HEADS
    nq = H * DK
    c = pl.program_id(1)

    @pl.when(c == 0)
    def _():
        s_scr[...] = jnp.zeros_like(s_scr)

    cos = cos_ref[...]
    sin = sin_ref[...]
    perm = perm_ref[...]
    rot = lambda x: x * cos + _exact_rhs_dot(x, perm) * sin
    q_all = rot(p_ref[0, :, 0:nq])
    k_all = rot(p_ref[0, :, nq:2 * nq]) * (DK ** -0.5)
    row = lax.broadcasted_iota(jnp.int32, (L, L), 0)
    col = lax.broadcasted_iota(jnp.int32, (L, L), 1)
    rel = (row - col).astype(F32)
    idx = lax.broadcasted_iota(jnp.int32, (L, 1), 0).astype(F32)
    vs = lambda h: p_ref[0, :, 2 * nq + h * DV:2 * nq + (h + 1) * DV].astype(BF16)
    hd = [dict() for _ in range(H)]
    for h, t in enumerate(hd):
        lg = _ret_log_gamma(h)
        q = q_all[:, h * DK:(h + 1) * DK]
        k = k_all[:, h * DK:(h + 1) * DK]
        t['s_prev'] = s_scr[h]
        t['dmask'] = jnp.where(rel >= 0, jnp.exp(jnp.maximum(rel, 0.0) * lg), 0.0)
        t['qk'] = _dot_nt(q.astype(BF16), k.astype(BF16))
        q_dec = q * jnp.exp((idx + 1.0) * lg)
        t['qs'] = _dot(q_dec.astype(BF16), t['s_prev'].astype(BF16))
        k_end = k * jnp.exp((L - 1.0 - idx) * lg)
        t['kv'] = _dot_tn(k_end.astype(BF16), vs(h))
    for h, t in enumerate(hd):
        y = _dot((t['qk'] * t['dmask']).astype(BF16), vs(h)) + t['qs']
        y = y * lax.rsqrt(jnp.mean(y * y, -1, keepdims=True) + NORM_EPS)
        gate = p_ref[0, :, 2 * nq + RET_W + h * DV:2 * nq + RET_W + (h + 1) * DV]
        o_ref[0, :, h * DV:(h + 1) * DV] = (gate * jax.nn.sigmoid(gate) * y).astype(BF16)
    for h, t in enumerate(hd):
        s_scr[h] = math.exp(L * _ret_log_gamma(h)) * t['s_prev'] + t['kv']

    @pl.when(c == nc - 1)
    def _():
        sf_ref[0] = s_scr[...]


def _ret_prompt(p3, pos):
    b, t, _ = p3.shape
    L = CHUNK
    nc = t // L
    nq = RET_HEADS * RET_QK
    cos, sin, perm = _rotary_tables(pos, RET_HEADS, RET_QK)
    tab = pl.BlockSpec((L, nq), lambda i, j: (j, 0))
    st = pl.BlockSpec((1, RET_HEADS, RET_QK, RET_V), lambda i, j: (i, 0, 0, 0))
    return pl.pallas_call(
        functools.partial(_ret_fused_body, nc=nc),
        grid=(b, nc),
        in_specs=[pl.BlockSpec((1, L, P_MAIN), lambda i, j: (i, j, P_RET // P_MAIN)), tab, tab,
                  pl.BlockSpec((nq, nq), lambda i, j: (0, 0))],
        out_specs=[pl.BlockSpec((1, L, RET_W), lambda i, j: (i, j, 0)), st],
        out_shape=[jax.ShapeDtypeStruct((b, t, RET_W), BF16),
                   jax.ShapeDtypeStruct((b, RET_HEADS, RET_QK, RET_V), F32)],
        scratch_shapes=[pltpu.VMEM((RET_HEADS, RET_QK, RET_V), F32)],
        compiler_params=_cparams(("parallel", "arbitrary")),
        name="ret_fused",
    )(p3, cos, sin, perm)


ML_HPAD = 8


def _ml_fused_body(p_ref, gate_ref, bias_ref, norm_ref, sel_ref, o_ref, cf_ref, nf_ref, mf_ref,
                   c_scr, n_scr, m_scr, *, nc):
    L, DK, DV, H = CHUNK, ML_QK, ML_V, ML_HEADS
    nq = H * DK
    ci = pl.program_id(1)

    @pl.when(ci == 0)
    def _():
        c_scr[...] = jnp.zeros_like(c_scr)
        n_scr[...] = jnp.zeros_like(n_scr)
        m_scr[...] = jnp.zeros_like(m_scr)

    row = lax.broadcasted_iota(jnp.int32, (L, L), 0)
    col = lax.broadcasted_iota(jnp.int32, (L, L), 1)
    causal = row >= col
    tri = causal.astype(BF16)
    capped = ML_GATE_CAP * jnp.tanh((gate_ref[0] + bias_ref[...]) * (1.0 / ML_GATE_CAP))
    lane = lax.broadcasted_iota(jnp.int32, (L, LANES), 1)
    g = jnp.where(lane < H, capped, jnp.where(lane < 2 * H, -_softplus(-capped), 0.0))
    g_rep = _exact_rhs_dot(g, sel_ref[...])
    b_rep_all = _exact_lhs_dot(tri, g_rep[:, H * LANES:])
    g_t = g.T
    cum_t = _exact_lhs_dot(tri, g).T
    cm_all = g_rep[:, :H * LANES] - b_rep_all
    row_id = lax.broadcasted_iota(jnp.int32, (L, 1), 0)
    shift = 1
    while shift < L:
        cm_all = jnp.maximum(cm_all, jnp.where(row_id >= shift, pltpu.roll(cm_all, shift, 0), -jnp.inf))
        shift *= 2
    ones = jnp.ones((L, LANES), BF16)
    mean_w = jnp.full((DV, LANES), 1.0 / DV, BF16)
    m_all = m_scr[...]
    hd = [dict() for _ in range(H)]
    for h, t in enumerate(hd):
        hs = slice(h * LANES, (h + 1) * LANES)
        q = p_ref[0, :, h * DK:(h + 1) * DK].astype(BF16)
        k = p_ref[0, :, nq + h * DK:nq + (h + 1) * DK] * (DK ** -0.5)
        t['v1'] = jnp.concatenate([p_ref[0, :, 2 * nq + h * DV:2 * nq + (h + 1) * DV].astype(BF16), ones], axis=1)
        ig_rep = g_rep[:, hs]
        b_rep = b_rep_all[:, hs]
        ig_row = g_t[h:h + 1, :]
        b_row = cum_t[H + h:H + h + 1, :]
        b_tot = b_rep[L - 1:L, :]
        m_prev = m_all[h:h + 1, :]
        t['c_prev'] = c_scr[h]
        t['n_prev'] = n_scr[h]
        t['m_new'] = jnp.maximum(b_tot + m_prev, jnp.max(b_tot - b_rep + ig_rep, axis=0, keepdims=True))
        t['dec'] = jnp.exp(b_tot + m_prev - t['m_new'])
        kw = k * jnp.exp((b_tot - b_rep + ig_rep - t['m_new'])[:, :DK])
        t['kvn'] = _dot_tn(kw.astype(BF16), t['v1'])
        inter = b_rep + m_prev
        t['m_i'] = b_rep + jnp.maximum(cm_all[:, hs], m_prev)
        t['e'] = jnp.exp(jnp.where(causal, (b_rep - t['m_i']) - b_row + ig_row, -jnp.inf))
        t['sc'] = jnp.exp(inter - t['m_i'])
        t['qk'] = _dot_nt(q, k.astype(BF16))
        cn = jnp.concatenate([t['c_prev'], t['n_prev']], axis=1).astype(BF16)
        t['qcn'] = _dot(q, cn)
    for h, t in enumerate(hd):
        nd = _dot((t['qk'] * t['e']).astype(BF16), t['v1'])
        num = nd[:, :DV] + t['sc'] * t['qcn'][:, :DV]
        den = nd[:, DV:] + t['sc'] * t['qcn'][:, DV:]
        hid = num / jnp.maximum(jnp.abs(den), jnp.exp(-t['m_i']))
        hid = hid * lax.rsqrt(_seg_sum(hid * hid, mean_w) + NORM_EPS)
        og = p_ref[0, :, 2 * nq + ML_W + h * DV:2 * nq + ML_W + (h + 1) * DV]
        o_ref[0, :, h * DV:(h + 1) * DV] = (jax.nn.sigmoid(og) * (hid * norm_ref[:, h * DV:(h + 1) * DV])).astype(BF16)
    for h, t in enumerate(hd):
        c_scr[h] = t['dec'] * t['c_prev'] + t['kvn'][:, :DV]
        n_scr[h] = t['dec'] * t['n_prev'] + t['kvn'][:, DV:]
        m_scr[h:h + 1, :] = t['m_new']

    @pl.when(ci == nc - 1)
    def _():
        cf_ref[0] = c_scr[...]
        nf_ref[0] = n_scr[...]
        mf_ref[0] = m_scr[...]


def _ml_prompt(p3, lp):
    b, t, _ = p3.shape
    L = CHUNK
    nc = t // L
    bias = jnp.pad(jnp.concatenate([lp['ml_ib'], lp['ml_fb']]), (0, LANES - 2 * ML_HEADS)).reshape(1, LANES)
    n_rep = 2 * ML_HEADS * LANES
    sel = (jnp.arange(LANES)[:, None] == jnp.arange(n_rep)[None, :] // LANES).astype(BF16)
    vs = pl.BlockSpec((1, L, ML_W), lambda i, j: (i, j, 0))
    cs = pl.BlockSpec((1, ML_HEADS, ML_QK, ML_V), lambda i, j: (i, 0, 0, 0))
    ns = pl.BlockSpec((1, ML_HEADS, ML_QK, LANES), lambda i, j: (i, 0, 0, 0))
    ms = pl.BlockSpec((1, ML_HPAD, LANES), lambda i, j: (i, 0, 0))
    o, c_f, n_f, m_f = pl.pallas_call(
        functools.partial(_ml_fused_body, nc=nc),
        grid=(b, nc),
        in_specs=[pl.BlockSpec((1, L, P_MAIN), lambda i, j: (i, j, P_ML // P_MAIN)),
                  pl.BlockSpec((1, L, LANES), lambda i, j: (i, j, P_GATE // LANES)),
                  pl.BlockSpec((1, LANES), lambda i, j: (0, 0)),
                  pl.BlockSpec((1, ML_W), lambda i, j: (0, 0)),
                  pl.BlockSpec((LANES, n_rep), lambda i, j: (0, 0))],
        out_specs=[vs, cs, ns, ms],
        out_shape=[jax.ShapeDtypeStruct((b, t, ML_W), BF16),
                   jax.ShapeDtypeStruct((b, ML_HEADS, ML_QK, ML_V), F32),
                   jax.ShapeDtypeStruct((b, ML_HEADS, ML_QK, LANES), F32),
                   jax.ShapeDtypeStruct((b, ML_HPAD, LANES), F32)],
        scratch_shapes=[pltpu.VMEM((ML_HEADS, ML_QK, ML_V), F32),
                        pltpu.VMEM((ML_HEADS, ML_QK, LANES), F32),
                        pltpu.VMEM((ML_HPAD, LANES), F32)],
        compiler_params=_cparams(("parallel", "arbitrary")),
        name="ml_fused",
    )(p3, p3, bias, lp['ml_norm'].reshape(1, ML_W), sel)
    return o, c_f, n_f[..., 0], m_f[:, :ML_HEADS, 0]


def _to_cols(x):
    b, c = x.shape
    cols = x.reshape(b // DEC_TB, DEC_TB, c).transpose(0, 2, 1)
    return jnp.pad(cols, ((0, 0), (0, 0), (0, DEC_TB)))


def _col_selector():
    j = jnp.arange(2 * DEC_TB)[:, None]
    return (j == jnp.arange(DEC_TB * LANES)[None, :] // LANES).astype(BF16)


def _rwkv_step_body(w_ref, ap_ref, k_ref, kk_ref, r_ref, v_ref, s_ref, y_ref, so_ref):
    N = RW_HD
    v = v_ref[0]

    def sa_step(i, acc):
        return acc + kk_ref[0, pl.ds(i, 1), :] * s_ref[i]

    sa = lax.fori_loop(0, N, sa_step, jnp.zeros_like(v), unroll=8)

    def upd_step(i, y):
        row = lambda ref: ref[0, pl.ds(i, 1), :]
        s_new = row(w_ref) * s_ref[i] - row(ap_ref) * sa + row(k_ref) * v
        so_ref[i] = s_new
        return y + row(r_ref) * s_new

    y_ref[0] = lax.fori_loop(0, N, upd_step, jnp.zeros_like(v), unroll=8)


def _rwkv_step(wdec, ap, k, kk, r, v, s_all, layer):
    b, w = v.shape
    heads = lambda x: x.reshape(b, RW_HEADS, RW_HD).transpose(1, 2, 0)
    vec = pl.BlockSpec((1, RW_HD, b), lambda h: (h, 0, 0))
    y, s_new = pl.pallas_call(
        _rwkv_step_body,
        grid=(RW_HEADS,),
        in_specs=[vec] * 6 + [pl.BlockSpec((None, None, RW_HD, RW_HD, b), lambda h: (layer, h, 0, 0, 0))],
        out_specs=[vec, pl.BlockSpec((None, RW_HD, RW_HD, b), lambda h: (h, 0, 0, 0))],
        out_shape=[jax.ShapeDtypeStruct((RW_HEADS, RW_HD, b), F32),
                   jax.ShapeDtypeStruct((RW_HEADS, RW_HD, RW_HD, b), F32)],
        compiler_params=_cparams(("parallel",)),
        name="rwkv_step",
    )(heads(wdec), heads(ap), heads(k), heads(kk), heads(r), heads(v), s_all)
    return y.transpose(2, 0, 1).reshape(b, w), s_new


def _ret_step_body(kc_ref, q_ref, k_ref, v_ref, esel_ref, seg_ref, s_ref, y_ref, so_ref):
    DK, DV = RET_QK, RET_V
    q = q_ref[...]
    qk_v = _exact_rhs_dot(q * k_ref[...], seg_ref[...])
    row_id = lax.broadcasted_iota(jnp.int32, (DEC_TB, 1), 0)
    for h in range(RET_HEADS):
        gamma = math.exp(_ret_log_gamma(h))
        ks = slice(h * DK, (h + 1) * DK)
        vs = slice(h * DV, (h + 1) * DV)
        k_rep = _exact_rhs_dot(kc_ref[0, ks, :], esel_ref[...])
        q_h = q[:, ks].astype(BF16)
        v_h = v_ref[:, vs]
        qs = jnp.zeros((DEC_TB, DV), F32)
        for j in range(DEC_TB):
            s = s_ref[j, h]
            so_ref[j, h] = gamma * s + k_rep[:, j * LANES:(j + 1) * LANES] * v_h[j:j + 1, :]
            qs = jnp.where(row_id == j, _dot(q_h, s.astype(BF16)), qs)
        y_ref[:, vs] = qk_v[:, vs] * v_h + gamma * qs


def _ret_step(q, k, v, s_all, layer):
    b = q.shape[0]
    nqk = RET_HEADS * RET_QK
    cols = pl.BlockSpec((1, nqk, 2 * DEC_TB), lambda i: (i, 0, 0))
    qk_rows = pl.BlockSpec((DEC_TB, nqk), lambda i: (i, 0))
    rows = pl.BlockSpec((DEC_TB, RET_W), lambda i: (i, 0))
    esel = pl.BlockSpec((2 * DEC_TB, DEC_TB * LANES), lambda i: (0, 0))
    seg = (jnp.arange(nqk)[:, None] // RET_QK == jnp.arange(RET_W)[None, :] // RET_V).astype(BF16)
    st = pl.BlockSpec((DEC_TB, RET_HEADS, RET_QK, RET_V), lambda i: (i, 0, 0, 0))
    st_in = pl.BlockSpec((None, DEC_TB, RET_HEADS, RET_QK, RET_V), lambda i: (layer, i, 0, 0, 0))
    return pl.pallas_call(
        _ret_step_body,
        grid=(b // DEC_TB,),
        in_specs=[cols, qk_rows, qk_rows, rows, esel, pl.BlockSpec((nqk, RET_W), lambda i: (0, 0)), st_in],
        out_specs=[rows, st],
        out_shape=[jax.ShapeDtypeStruct((b, RET_W), F32), jax.ShapeDtypeStruct(s_all.shape[1:], F32)],
        compiler_params=_cparams(("parallel",)),
        name="ret_step",
    )(_to_cols(k), q, k, v, _col_selector(), seg, s_all)


def _ml_step_body(kc_ref, q_ref, k_ref, v_ref, ig_ref, lf_ref, esel_ref, hsel_ref, hsel_k_ref, seg_ref,
                  c_ref, n_ref, m_ref, h_ref, co_ref, no_ref, mo_ref):
    DK, DV, H = ML_QK, ML_V, ML_HEADS
    ig, lf, m_prev = ig_ref[...], lf_ref[...], m_ref[...]
    m_new = jnp.maximum(lf + m_prev, ig)
    dec = jnp.exp(lf + m_prev - m_new)
    wgt = jnp.exp(ig - m_new)
    mo_ref[...] = m_new
    hsel = hsel_ref[...]
    dec_v = _exact_rhs_dot(dec, hsel)
    wgt_v = _exact_rhs_dot(wgt, hsel)
    floor_v = _exact_rhs_dot(jnp.exp(-m_new), hsel)
    q, k, n_prev = q_ref[...], k_ref[...], n_ref[...]
    no_ref[...] = _exact_rhs_dot(dec, hsel_k_ref[...]) * n_prev + k * _exact_rhs_dot(wgt, hsel_k_ref[...])
    s_v = _exact_rhs_dot(q * k, seg_ref[...]) * wgt_v
    den_v = s_v + dec_v * _exact_rhs_dot(q * n_prev, seg_ref[...])
    row_id = lax.broadcasted_iota(jnp.int32, (DEC_TB, 1), 0)
    for h in range(H):
        ks = slice(h * DK, (h + 1) * DK)
        vs = slice(h * DV, (h + 1) * DV)
        k_rep = _exact_rhs_dot(kc_ref[0, ks, :], esel_ref[...])
        q_h = q[:, ks].astype(BF16)
        v_h = v_ref[:, vs]
        kv_scale = wgt_v[:, vs] * v_h
        qc = jnp.zeros((DEC_TB, DV), F32)
        for j in range(DEC_TB):
            c_prev = c_ref[j, h]
            co_ref[j, h] = dec_v[j:j + 1, vs] * c_prev + k_rep[:, j * LANES:(j + 1) * LANES] * kv_scale[j:j + 1, :]
            qc = jnp.where(row_id == j, _dot(q_h, c_prev.astype(BF16)), qc)
        num = s_v[:, vs] * v_h + dec_v[:, vs] * qc
        h_ref[:, vs] = num / jnp.maximum(jnp.abs(den_v[:, vs]), floor_v[:, vs])


def _ml_step(q, k, v, ig, lf, c_all, layer, n0, m0):
    b = q.shape[0]
    nqk = ML_HEADS * ML_QK
    pad_h = lambda x: jnp.pad(x, ((0, 0), (0, LANES - ML_HEADS)))
    head = jnp.arange(LANES)[:, None]
    hsel = (head == jnp.arange(ML_W)[None, :] // ML_V).astype(BF16)
    hsel_k = (head == jnp.arange(nqk)[None, :] // ML_QK).astype(BF16)
    seg = (jnp.arange(nqk)[:, None] // ML_QK == jnp.arange(ML_W)[None, :] // ML_V).astype(BF16)
    const = lambda r, c: pl.BlockSpec((r, c), lambda i: (0, 0))
    cols = pl.BlockSpec((1, nqk, 2 * DEC_TB), lambda i: (i, 0, 0))
    qk_rows = pl.BlockSpec((DEC_TB, nqk), lambda i: (i, 0))
    rows = pl.BlockSpec((DEC_TB, ML_W), lambda i: (i, 0))
    sc = pl.BlockSpec((DEC_TB, LANES), lambda i: (i, 0))
    cs = pl.BlockSpec((DEC_TB, ML_HEADS, ML_QK, ML_V), lambda i: (i, 0, 0, 0))
    cs_in = pl.BlockSpec((None, DEC_TB, ML_HEADS, ML_QK, ML_V), lambda i: (layer, i, 0, 0, 0))
    hm, c_new, n_new, m_new = pl.pallas_call(
        _ml_step_body,
        grid=(b // DEC_TB,),
        in_specs=[cols, qk_rows, qk_rows, rows, sc, sc, const(2 * DEC_TB, DEC_TB * LANES), const(LANES, ML_W),
                  const(LANES, nqk), const(nqk, ML_W), cs_in, qk_rows, sc],
        out_specs=[rows, cs, qk_rows, sc],
        out_shape=[jax.ShapeDtypeStruct((b, ML_W), F32), jax.ShapeDtypeStruct(c_all.shape[1:], F32),
                   jax.ShapeDtypeStruct((b, nqk), F32), jax.ShapeDtypeStruct((b, LANES), F32)],
        compiler_params=_cparams(("parallel",)),
        name="ml_step",
    )(_to_cols(k), q, k, v, pad_h(ig), pad_h(lf), _col_selector(), hsel, hsel_k, seg,
      c_all, n0.reshape(b, nqk), pad_h(m0))
    return hm, c_new, n_new.reshape(n0.shape), m_new[:, :ML_HEADS]


def _heads(a, h):
    return a.reshape(a.shape[:-1] + (h, a.shape[-1] // h))


def _shift_prev(p, prev_row):
    return jnp.concatenate([prev_row[:, None, :], p[:, :-1]], axis=1)


def _rotary(x, pos):
    half = x.shape[-1] // 2
    inv = ROPE_BASE ** (-jnp.arange(half, dtype=F32) / half)
    ang = pos.astype(F32)[:, None] * inv[None, :]
    cos = jnp.cos(ang)[None, :, None, :]
    sin = jnp.sin(ang)[None, :, None, :]
    x1, x2 = x[..., :half], x[..., half:]
    return jnp.concatenate([x1 * cos - x2 * sin, x1 * sin + x2 * cos], -1)


def _small_matmul(x, w):
    lead = x.shape[:-1]
    kdim, n = w.shape
    x2 = x.reshape(-1, kdim)
    m = x2.shape[0]
    kp = -(-kdim // LANES) * LANES
    npad = -(-n // LANES) * LANES
    x2 = jnp.pad(x2.astype(BF16), ((0, 0), (0, kp - kdim)))
    w2 = jnp.pad(w.astype(BF16), ((0, kp - kdim), (0, npad - n)))
    tm = 1024 if m % 1024 == 0 else m
    out = _matmul(x2, w2, tm, npad)
    return out[:, :n].reshape(lead + (n,))


def _mix_prompt(p, pos, v_first, lp):
    o_rw, v_first, s_new = _rwkv_prompt(p, lp, v_first)
    o_ret, r_new = _ret_prompt(p, pos)
    o_ml, c_new, n_new, m_new = _ml_prompt(p, lp)
    return (o_rw, o_ret, o_ml), v_first, (s_new, r_new, c_new, n_new, m_new)


def _mix_sample(p, pos, v_first, st, lp, prev_row):
    bsz, t_len, _ = p.shape

    p_rw = p[..., :RW_P]
    mixed = p_rw + (_shift_prev(p_rw, prev_row[:, :RW_P]) - p_rw) * lp['rw_mu']
    sizes = np.cumsum([RW_W, RW_W, RW_W, RW_LORA_W, RW_LORA_A, RW_LORA_G])[:-1]
    r, k, v, xw, xa, xg = jnp.split(mixed, [int(s) for s in sizes], axis=-1)
    w = -jax.nn.softplus(-(lp['rw_w0'] + _small_matmul(jnp.tanh(xw), lp['rw_w2']))) - 0.5
    a = jax.nn.sigmoid(lp['rw_a0'] + _small_matmul(xa, lp['rw_a2']))
    g = _small_matmul(jax.nn.sigmoid(xg), lp['rw_g2'])
    if v_first is None:
        v_first = v
    else:
        pv = p[..., P_V1:P_V1 + RW_LORA_V]
        xv = pv + (_shift_prev(pv, prev_row[:, P_V1:P_V1 + RW_LORA_V]) - pv) * lp['rw_vmu']
        v = v + (v_first - v) * jax.nn.sigmoid(lp['rw_v0'] + _small_matmul(xv, lp['rw_v2']))
    kk = _heads(k * lp['rw_kk'], RW_HEADS)
    kk = kk * lax.rsqrt(jnp.maximum(jnp.sum(jnp.square(kk), -1, keepdims=True), 1e-24))
    kk = kk.reshape(bsz, t_len, RW_W)
    k = k * (1.0 + (a - 1.0) * lp['rw_ka'])
    lw = -jnp.exp(w)
    y, s_new = _rwkv_step(jnp.exp(lw)[:, 0], (kk * a)[:, 0], k[:, 0], kk[:, 0], r[:, 0], v[:, 0],
                          st['rw_wkv_t'], st['layer'])
    y = _heads(y[:, None, :], RW_HEADS)
    y_mu = jnp.mean(y, -1, keepdims=True)
    y_var = jnp.mean(jnp.square(y - y_mu), -1, keepdims=True)
    y = ((y - y_mu) * lax.rsqrt(y_var + RW_GN_EPS)).reshape(bsz, t_len, RW_W)
    y = y * lp['rw_lnx_g'] + lp['rw_lnx_b']
    rh, kh, vh = (_heads(u, RW_HEADS) for u in (r, k, v))
    bonus = jnp.sum(rh * kh * _heads(lp['rw_rk'], RW_HEADS), -1, keepdims=True) * vh
    o_rw = ((y + bonus.reshape(bsz, t_len, RW_W)) * g).astype(BF16)

    nqk = RET_HEADS * RET_QK
    p_ret = p[..., P_RET:P_RET + P_MAIN]
    qr, kr, vr, gr = (p_ret[..., :nqk], p_ret[..., nqk:2 * nqk],
                      p_ret[..., 2 * nqk:2 * nqk + RET_W], p_ret[..., 2 * nqk + RET_W:])
    qh = _rotary(_heads(qr, RET_HEADS), pos).reshape(bsz, t_len, nqk)
    khr = (_rotary(_heads(kr, RET_HEADS), pos) * (RET_QK ** -0.5)).reshape(bsz, t_len, nqk)
    yr, r_new = _ret_step(qh[:, 0], khr[:, 0], vr[:, 0], st['ret_all'], st['layer'])
    yr = _heads(yr[:, None, :], RET_HEADS)
    yr = yr * lax.rsqrt(jnp.mean(jnp.square(yr), -1, keepdims=True) + NORM_EPS)
    o_ret = (jax.nn.silu(gr) * yr.reshape(bsz, t_len, RET_W)).astype(BF16)

    nqk = ML_HEADS * ML_QK
    p_ml = p[..., P_ML:P_ML + P_MAIN]
    qm, km, vm, om = (p_ml[..., :nqk], p_ml[..., nqk:2 * nqk],
                      p_ml[..., 2 * nqk:2 * nqk + ML_W], p_ml[..., 2 * nqk + ML_W:])
    im = p[..., P_GATE:P_GATE + ML_HEADS]
    fm = p[..., P_GATE + ML_HEADS:P_GATE + 2 * ML_HEADS]
    ig = ML_GATE_CAP * jnp.tanh((im + lp['ml_ib']) / ML_GATE_CAP)
    lf = jax.nn.log_sigmoid(ML_GATE_CAP * jnp.tanh((fm + lp['ml_fb']) / ML_GATE_CAP))
    km = km * (ML_QK ** -0.5)
    hm, c_new, n_new, m_new = _ml_step(qm[:, 0], km[:, 0], vm[:, 0], ig[:, 0], lf[:, 0],
                                       st['ml_c_all'], st['layer'], st['ml_n'], st['ml_m'])
    hm = _heads(hm[:, None, :], ML_HEADS)
    hm = hm * lax.rsqrt(jnp.mean(jnp.square(hm), -1, keepdims=True) + NORM_EPS)
    o_ml = (jax.nn.sigmoid(om) * (hm.reshape(bsz, t_len, ML_W) * lp['ml_norm'])).astype(BF16)

    return (o_rw, o_ret, o_ml), v_first, (s_new, r_new, c_new, n_new, m_new)


_T_V1 = P_V1 // LANES
_T_GATE = P_GATE // LANES
_T_RET = P_RET // LANES
_T_SHIFT = (P_RET - RW_P) // LANES
_T_SRC_GATE = (RW_P + RET_P + 2 * ML_HEADS * ML_QK + 2 * ML_W) // LANES


def _pack_body(w_ref, v1_ref, *o_refs):
    j = pl.program_id(0)
    row = lax.broadcasted_iota(jnp.int32, (LANES, 1), 0)
    spare = jnp.logical_and(j > _T_GATE, j < _T_RET)
    for l, o_ref in enumerate(o_refs):
        w = w_ref[:, l, :]
        gates = jnp.where(row < 2 * ML_HEADS, w, 0.0)
        out = jnp.where(j == _T_V1, v1_ref[l], jnp.where(j == _T_GATE, gates, jnp.where(spare, 0.0, w)))
        o_ref[...] = out.astype(BF16)


def _pack_w_in(w_in, rw_v1):
    depth, d, _ = w_in.shape
    w_t = w_in.transpose(2, 0, 1)
    v1_t = jnp.pad(rw_v1.transpose(0, 2, 1), ((1, 0), (0, LANES - RW_LORA_V), (0, 0)))

    def src_tile(j):
        return jnp.where(j < _T_V1, j, jnp.where(j == _T_GATE, _T_SRC_GATE, j - _T_SHIFT))

    return pl.pallas_call(
        _pack_body,
        grid=(P_PAD // LANES,),
        in_specs=[pl.BlockSpec((LANES, depth, d), lambda j: (src_tile(j), 0, 0)),
                  pl.BlockSpec((depth, LANES, d), lambda j: (0, 0, 0))],
        out_specs=[pl.BlockSpec((LANES, d), lambda j: (j, 0))] * depth,
        out_shape=[jax.ShapeDtypeStruct((P_PAD, d), BF16)] * depth,
        compiler_params=_cparams(("parallel",)),
        name="pack_w_in",
    )(w_t, v1_t)


def _token_tiles(m):
    if m % 2048 == 0:
        return 2048, 512, 512
    return m, m, m


def kernel(x_prompt, x_sample, state_rw_shift, state_rw_wkv, state_ret, state_ml_c, state_ml_n, state_ml_m,
           ln0_g, ln0_b, w_in, rw_mu, rw_w0, rw_w2, rw_a0, rw_a2, rw_g2, rw_kk, rw_ka, rw_rk,
           rw_lnx_g, rw_lnx_b, rw_v0, rw_v1, rw_vmu, rw_v2, ml_ib, ml_fb, ml_norm, w_out,
           ln1_g, ln1_b, w_gate, w_up, w_down, ln2_g, ln2_b):
    bp, tp, d = x_prompt.shape
    bs, ts, _ = x_sample.shape
    mp, ms = bp * tp, bs * ts
    pos_p = jnp.arange(tp)
    pos_s = PAST_LEN + jnp.arange(ts)
    tm_big, tm_out, tm_down = _token_tiles(mp)
    rw_wkv_t = state_rw_wkv.transpose(0, 2, 3, 4, 1)
    w_in_packed = _pack_w_in(w_in, rw_v1)
    xf_p, xb_p = _layernorm(x_prompt.reshape(mp, d), ln0_g, ln0_b, tm_out)
    xf_s, xb_s = _layernorm(x_sample.reshape(ms, d), ln0_g, ln0_b, ms)
    vf_p = vf_s = None
    outs_p, outs_s = [], []

    for l in range(DEPTH):
        lp = {
            'rw_mu': rw_mu[l], 'rw_w0': rw_w0[l], 'rw_w2': rw_w2[l], 'rw_a0': rw_a0[l], 'rw_a2': rw_a2[l],
            'rw_g2': rw_g2[l], 'rw_kk': rw_kk[l], 'rw_ka': rw_ka[l], 'rw_rk': rw_rk[l],
            'rw_lnx_g': rw_lnx_g[l], 'rw_lnx_b': rw_lnx_b[l], 'ml_ib': ml_ib[l], 'ml_fb': ml_fb[l],
            'ml_norm': ml_norm[l],
        }
        if l > 0:
            lp.update(rw_v0=rw_v0[l - 1], rw_vmu=rw_vmu[l - 1], rw_v2=rw_v2[l - 1])
        w_out_b = w_out[l].astype(BF16)
        w_down_b = w_down[l].astype(BF16)
        x_side = jnp.concatenate([xb_s, state_rw_shift[l].astype(BF16)], axis=0)
        p_p, p_side = _in_proj(xb_p, x_side, w_in_packed[l], tm_big, P_TN)
        o_p, vf_p, st_p = _mix_prompt(p_p.reshape(bp, tp, P_PAD), pos_p, vf_p, lp)
        st = {'rw_wkv_t': rw_wkv_t, 'layer': l, 'ret_all': state_ret, 'ml_c_all': state_ml_c,
              'ml_n': state_ml_n[l], 'ml_m': state_ml_m[l]}
        o_s, vf_s, st_s = _mix_sample(p_side[:ms].reshape(bs, ts, P_PAD), pos_s, vf_s, st, lp, p_side[ms:])
        outs_p.append((xf_p.reshape(bp, tp, d)[:, -1],) + st_p)
        outs_s.append((xf_s.reshape(bs, ts, d)[:, -1],) + st_s)
        flat = lambda o, m: tuple(u.reshape(m, u.shape[-1]) for u in o)
        x1f_p, x1b_p, x1f_s, x1b_s = _out_proj_ln(flat(o_p, mp), xf_p, flat(o_s, ms), xf_s, w_out_b,
                                                  ln1_g[l], ln1_b[l], tm_out)
        hdn_p, hdn_s = _matmul_swiglu(x1b_p, x1b_s, w_gate, w_up, l, tm_big, 512)
        xf_p, xb_p, xf_s, xb_s = _matmul_res_ln(hdn_p, hdn_s, w_down_b, x1f_p, x1f_s, ln2_g[l], ln2_b[l],
                                                tm_down, DOWN_TK)

    y_p = xf_p.reshape(bp, tp, d)
    y_s = xf_s.reshape(bs, ts, d)
    sp = [jnp.stack([o[i] for o in outs_p]) for i in range(6)]
    ss = [jnp.stack([o[i] for o in outs_s]) for i in range(6)]
    ss[1] = ss[1].transpose(0, 4, 1, 2, 3)
    return (y_p, y_s, sp[0], sp[1], sp[2], sp[3], sp[4], sp[5], ss[0], ss[1], ss[2], ss[3], ss[4], ss[5])
```

```python
import functools
import math

import numpy as np
import jax
import jax.numpy as jnp
from jax import lax
from jax.experimental import pallas as pl
from jax.experimental.pallas import tpu as pltpu

F32 = jnp.float32
BF16 = jnp.bfloat16

D_MODEL = 2048
DEPTH = 2
PAST_LEN = 16384
RW_HD = 64
RW_W = D_MODEL // 4
RW_HEADS = RW_W // RW_HD
RW_LORA_W = 64
RW_LORA_A = 64
RW_LORA_V = 32
RW_LORA_G = 128
RW_P = 3 * RW_W + RW_LORA_W + RW_LORA_A + RW_LORA_G
RW_GN_EPS = 64e-5
RET_V = 128
RET_QK = 64
RET_W = 3 * D_MODEL // 8
RET_HEADS = RET_W // RET_V
RET_P = 2 * RET_HEADS * RET_QK + 2 * RET_W
ML_V = 128
ML_QK = 64
ML_W = D_MODEL - RW_W - RET_W
ML_HEADS = ML_W // ML_V
ML_P = 2 * ML_HEADS * ML_QK + 2 * ML_W + 2 * ML_HEADS
ML_GATE_CAP = 15.0
P_TOTAL = RW_P + RET_P + ML_P
D_FF = ((8 * D_MODEL + 3 * 256 - 1) // (3 * 256)) * 256
CHUNK = 128
ROPE_BASE = 10000.0
LN_EPS = 1e-5
NORM_EPS = 1e-6
ALPHA = (2 * DEPTH) ** 0.25

LANES = 128
P_V1 = RW_P
P_GATE = RW_P + LANES
P_RET = 2304
P_ML = 2 * P_RET
P_MAIN = 2304
P_PAD = 3 * P_RET
P_TN = 768
RW_CHUNK = 64
DEC_TB = 8
VMEM_LIMIT = 56 * 1024 * 1024

def _cparams(sem):
    return pltpu.CompilerParams(dimension_semantics=sem, vmem_limit_bytes=VMEM_LIMIT)


def _dot(a, b):
    return lax.dot_general(a, b, (((1,), (0,)), ((), ())), preferred_element_type=F32)


def _dot_nt(a, b):
    return lax.dot_general(a, b, (((1,), (1,)), ((), ())), preferred_element_type=F32)


def _dot_tn(a, b):
    return lax.dot_general(a, b, (((0,), (0,)), ((), ())), preferred_element_type=F32)


def _ln_rows(x, g, b):
    mu = jnp.mean(x, -1, keepdims=True)
    xc = x - mu
    var = jnp.mean(xc * xc, -1, keepdims=True)
    return xc * lax.rsqrt(var + LN_EPS) * g + b


def _ln_body(x_ref, g_ref, b_ref, of_ref, ob_ref):
    y = _ln_rows(x_ref[...], g_ref[...], b_ref[...])
    of_ref[...] = y
    ob_ref[...] = y.astype(BF16)


def _layernorm(x, g, b, tm):
    m, d = x.shape
    return pl.pallas_call(
        _ln_body,
        grid=(m // tm,),
        in_specs=[pl.BlockSpec((tm, d), lambda i: (i, 0)),
                  pl.BlockSpec((1, d), lambda i: (0, 0)),
                  pl.BlockSpec((1, d), lambda i: (0, 0))],
        out_specs=[pl.BlockSpec((tm, d), lambda i: (i, 0)),
                   pl.BlockSpec((tm, d), lambda i: (i, 0))],
        out_shape=[jax.ShapeDtypeStruct((m, d), F32), jax.ShapeDtypeStruct((m, d), BF16)],
        compiler_params=_cparams(("parallel",)),
        name="layernorm",
    )(x, g.reshape(1, d), b.reshape(1, d))


def _mm_body(x_ref, w_ref, o_ref, *, w_transposed):
    dot = _dot_nt if w_transposed else _dot
    o_ref[...] = dot(x_ref[...], w_ref[...]).astype(o_ref.dtype)


def _matmul(x, w, tm, tn, out_dtype=F32, w_transposed=False):
    m, k = x.shape
    n = w.shape[0] if w_transposed else w.shape[1]
    w_spec = (pl.BlockSpec((tn, k), lambda i, j: (j, 0)) if w_transposed
              else pl.BlockSpec((k, tn), lambda i, j: (0, j)))
    return pl.pallas_call(
        functools.partial(_mm_body, w_transposed=w_transposed),
        grid=(m // tm, n // tn),
        in_specs=[pl.BlockSpec((tm, k), lambda i, j: (i, 0)), w_spec],
        out_specs=pl.BlockSpec((tm, tn), lambda i, j: (i, j)),
        out_shape=jax.ShapeDtypeStruct((m, n), out_dtype),
        compiler_params=_cparams(("parallel", "parallel")),
        name="matmul",
    )(x, w)


def _in_proj_body(x_ref, xs_ref, w_ref, o_ref, os_ref):
    @pl.when(pl.program_id(1) == 0)
    def _():
        os_ref[...] = _dot_nt(xs_ref[...], w_ref[...])

    o_ref[...] = _dot_nt(x_ref[...], w_ref[...])


def _in_proj(x, x_side, w_t, tm, tn):
    m, k = x.shape
    ms = x_side.shape[0]
    n = w_t.shape[0]
    return pl.pallas_call(
        _in_proj_body,
        grid=(n // tn, m // tm),
        in_specs=[pl.BlockSpec((tm, k), lambda j, i: (i, 0)), pl.BlockSpec((ms, k), lambda j, i: (0, 0)),
                  pl.BlockSpec((tn, k), lambda j, i: (j, 0))],
        out_specs=[pl.BlockSpec((tm, tn), lambda j, i: (i, j)), pl.BlockSpec((ms, tn), lambda j, i: (0, j))],
        out_shape=[jax.ShapeDtypeStruct((m, n), F32), jax.ShapeDtypeStruct((ms, n), F32)],
        compiler_params=_cparams(("parallel", "arbitrary")),
        name="in_proj",
    )(x, x_side, w_t)


def _swiglu_body(x_ref, xs_ref, wg_ref, wu_ref, o_ref, os_ref, wg_scr, wu_scr):
    def act(x):
        g = _dot(x, wg_scr[...])
        return (g * jax.nn.sigmoid(g) * _dot(x, wu_scr[...])).astype(BF16)

    @pl.when(pl.program_id(1) == 0)
    def _():
        wg_scr[...] = wg_ref[...].astype(BF16)
        wu_scr[...] = wu_ref[...].astype(BF16)
        os_ref[...] = act(xs_ref[...])

    step = x_ref.shape[0] // SWIGLU_CHUNKS
    for r in range(SWIGLU_CHUNKS):
        rows = pl.ds(r * step, step)
        o_ref[rows, :] = act(x_ref[rows, :])


def _matmul_swiglu(x, x_side, wg, wu, layer, tm, tn):
    m, k = x.shape
    ms = x_side.shape[0]
    n = wg.shape[2]
    w_spec = pl.BlockSpec((None, k, tn), lambda j, i: (layer, 0, j))
    return pl.pallas_call(
        _swiglu_body,
        grid=(n // tn, m // tm),
        in_specs=[pl.BlockSpec((tm, k), lambda j, i: (i, 0)), pl.BlockSpec((ms, k), lambda j, i: (0, 0)),
                  w_spec, w_spec],
        out_specs=[pl.BlockSpec((tm, tn), lambda j, i: (i, j)), pl.BlockSpec((ms, tn), lambda j, i: (0, j))],
        out_shape=[jax.ShapeDtypeStruct((m, n), BF16), jax.ShapeDtypeStruct((ms, n), BF16)],
        scratch_shapes=[pltpu.VMEM((k, tn), BF16), pltpu.VMEM((k, tn), BF16)],
        compiler_params=_cparams(("parallel", "arbitrary")),
        name="matmul_swiglu",
    )(x, x_side, wg, wu)


SWIGLU_CHUNKS = 2
DOWN_TK = D_FF // 4
LN_ROWS = 256
OUT_ROWS = 256


def _res_ln_store(acc_ref, res_ref, g_ref, b_ref, of_ref, ob_ref, n_rows):
    step = min(LN_ROWS, n_rows)
    for r in range(0, n_rows, step):
        rows = pl.ds(r, step)
        y = _ln_rows(ALPHA * res_ref[rows, :] + acc_ref[rows, :], g_ref[...], b_ref[...])
        of_ref[rows, :] = y
        ob_ref[rows, :] = y.astype(BF16)


def _mm_res_ln_body(x_ref, xs_ref, w_ref, res_ref, ress_ref, g_ref, b_ref, of_ref, ob_ref, ofs_ref, obs_ref, *, nk, tm, ms):
    i = pl.program_id(0)
    kk = pl.program_id(1)

    @pl.when(kk == 0)
    def _():
        of_ref[...] = jnp.zeros_like(of_ref)

    of_ref[...] += _dot(x_ref[...], w_ref[...])

    @pl.when(kk == nk - 1)
    def _():
        _res_ln_store(of_ref, res_ref, g_ref, b_ref, of_ref, ob_ref, tm)

    @pl.when(i == 0)
    def _():
        @pl.when(kk == 0)
        def _():
            ofs_ref[...] = jnp.zeros_like(ofs_ref)

        ofs_ref[...] += _dot(xs_ref[...], w_ref[...])

        @pl.when(kk == nk - 1)
        def _():
            _res_ln_store(ofs_ref, ress_ref, g_ref, b_ref, ofs_ref, obs_ref, ms)


def _matmul_res_ln(x, x_side, w, res, res_side, g, b, tm, tk):
    m, k = x.shape
    ms = x_side.shape[0]
    n = w.shape[1]
    nk = k // tk
    const = lambda r: pl.BlockSpec((r, n), lambda i, j: (0, 0))
    main = pl.BlockSpec((tm, n), lambda i, j: (i, 0))
    return pl.pallas_call(
        functools.partial(_mm_res_ln_body, nk=nk, tm=tm, ms=ms),
        grid=(m // tm, nk),
        in_specs=[pl.BlockSpec((tm, tk), lambda i, j: (i, j)), pl.BlockSpec((ms, tk), lambda i, j: (0, j)),
                  pl.BlockSpec((tk, n), lambda i, j: (j, 0)), main, const(ms), const(1), const(1)],
        out_specs=[main, main, const(ms), const(ms)],
        out_shape=[jax.ShapeDtypeStruct((m, n), F32), jax.ShapeDtypeStruct((m, n), BF16),
                   jax.ShapeDtypeStruct((ms, n), F32), jax.ShapeDtypeStruct((ms, n), BF16)],
        compiler_params=_cparams(("arbitrary", "arbitrary")),
        name="matmul_res_ln",
    )(x, x_side, w, res, res_side, g.reshape(1, n), b.reshape(1, n))


def _out_proj_ln_body(o_rw_ref, o_ret_ref, o_ml_ref, res_ref, s_rw_ref, s_ret_ref, s_ml_ref, ress_ref,
                      w_ref, g_ref, b_ref, of_ref, ob_ref, ofs_ref, obs_ref, *, tm, ms):
    def project(rw_ref, ret_ref, ml_ref, r_ref, f_ref, h_ref, n_rows):
        step = min(OUT_ROWS, n_rows)
        for r in range(0, n_rows, step):
            rows = pl.ds(r, step)
            mix = (_dot(rw_ref[rows, :], w_ref[0:RW_W, :])
                   + _dot(ret_ref[rows, :], w_ref[RW_W:RW_W + RET_W, :])
                   + _dot(ml_ref[rows, :], w_ref[RW_W + RET_W:, :]))
            y = _ln_rows(ALPHA * r_ref[rows, :] + mix, g_ref[...], b_ref[...])
            f_ref[rows, :] = y
            h_ref[rows, :] = y.astype(BF16)

    @pl.when(pl.program_id(0) == 0)
    def _():
        project(s_rw_ref, s_ret_ref, s_ml_ref, ress_ref, ofs_ref, obs_ref, ms)

    project(o_rw_ref, o_ret_ref, o_ml_ref, res_ref, of_ref, ob_ref, tm)


def _out_proj_ln(o, res, o_side, res_side, w, g, b, tm):
    m = res.shape[0]
    ms = res_side.shape[0]
    n = w.shape[1]
    rows = lambda width: pl.BlockSpec((tm, width), lambda i: (i, 0))
    const = lambda r, c: pl.BlockSpec((r, c), lambda i: (0, 0))
    widths = (RW_W, RET_W, ML_W)
    return pl.pallas_call(
        functools.partial(_out_proj_ln_body, tm=tm, ms=ms),
        grid=(m // tm,),
        in_specs=([rows(c) for c in widths] + [rows(n)] + [const(ms, c) for c in widths] + [const(ms, n)]
                  + [const(D_MODEL, n), const(1, n), const(1, n)]),
        out_specs=[rows(n), rows(n), const(ms, n), const(ms, n)],
        out_shape=[jax.ShapeDtypeStruct((m, n), F32), jax.ShapeDtypeStruct((m, n), BF16),
                   jax.ShapeDtypeStruct((ms, n), F32), jax.ShapeDtypeStruct((ms, n), BF16)],
        compiler_params=_cparams(("arbitrary",)),
        name="out_proj_ln",
    )(*o, res, *o_side, res_side, w, g.reshape(1, n), b.reshape(1, n))


RW_TB = 512
RW_GH = 4
RW_GW = RW_GH * RW_HD
RW_AHEAD = 8
RW_VEC_ROWS = 8


def _split3(x):
    hi = x.astype(BF16)
    r1 = x - hi.astype(F32)
    mid = r1.astype(BF16)
    lo = (r1 - mid.astype(F32)).astype(BF16)
    return hi, mid, lo


def _mm(a, b, dims):
    return lax.dot_general(a.astype(BF16), b.astype(BF16), (dims, ((), ())), preferred_element_type=F32)


_NN = ((1,), (0,))
_NT = ((1,), (1,))
_TN = ((0,), (0,))


def _exact_lhs_dot(a_bf16, b):
    hi, mid, lo = _split3(b)
    dg = lambda y: lax.dot_general(a_bf16, y, (_NN, ((), ())), preferred_element_type=F32)
    return dg(hi) + (dg(mid) + dg(lo))


def _exact_rhs_dot(a, b_bf16):
    hi, mid, lo = _split3(a)
    dg = lambda x: lax.dot_general(x, b_bf16, (_NN, ((), ())), preferred_element_type=F32)
    return dg(hi) + (dg(mid) + dg(lo))


def _seg_sum(a, seg_bf16):
    hi = a.astype(BF16)
    lo = (a - hi.astype(F32)).astype(BF16)
    dg = lambda x: lax.dot_general(x, seg_bf16, (_NN, ((), ())), preferred_element_type=F32)
    return dg(hi) + dg(lo)


def _rw_scan(r, lw, k, v, kk, a, st_scr, y_scr):
    L, TB, G = RW_CHUNK, RW_TB, RW_GW
    row = lax.broadcasted_iota(jnp.int32, (L, G), 0)
    col = lax.broadcasted_iota(jnp.int32, (L, G), 1) & (L - 1)
    strict, lower, eye = row > col, row >= col, (row == col).astype(F32)
    rg = lax.broadcasted_iota(jnp.int32, (G, G), 0) // RW_HD
    cg = lax.broadcasted_iota(jnp.int32, (G, G), 1) // RW_HD
    mask_bd = rg == cg
    rt = lax.broadcasted_iota(jnp.int32, (2 * L, 2 * L), 0)
    ct = lax.broadcasted_iota(jnp.int32, (2 * L, 2 * L), 1)
    tri = jnp.logical_and(rt >= ct, rt // L == ct // L).astype(BF16)
    bd = lambda x: jnp.where(mask_bd, jnp.concatenate([x.astype(BF16)] * RW_GH, axis=0), 0.0)
    cut = lambda x, b: x[b[0] * L:(b[0] + 1) * L, b[1] * G:(b[1] + 1) * G]

    cum = jnp.concatenate([_exact_lhs_dot(tri, lw[i:i + 2 * L, :]) for i in range(0, TB, 2 * L)], axis=0)
    e_neg = jnp.exp(-cum)
    ap = kk * a
    ap_h = ap * e_neg
    k_h = k * e_neg
    kk_t = kk * jnp.exp(cum - lw)
    r_t = r * jnp.exp(cum)

    lhs, n_m, m_a, m_kr, inv, mv, vk, a_end, decay = ({} for _ in range(9))
    n_groups = RW_HEADS // RW_GH

    def prep(chunks):
        blk = [(s, g) for s in chunks for g in range(n_groups)]
        for b in blk:
            lhs[b] = jnp.concatenate([cut(kk_t, b), cut(r_t, b)], axis=0)
            sc_a = _mm(lhs[b], bd(cut(ap_h, b)), _NT)
            sc_k = _mm(lhs[b], bd(cut(k_h, b)), _NT)
            n_m[b] = jnp.where(strict, sc_a[:L], 0.0)
            m_a[b] = jnp.where(lower, sc_a[L:], 0.0)
            m_kr[b] = jnp.concatenate([jnp.where(strict, sc_k[:L], 0.0), jnp.where(lower, sc_k[L:], 0.0)], axis=0)
        pw = {}
        for b in blk:
            inv[b] = eye - n_m[b]
            pw[b] = _mm(n_m[b], bd(n_m[b]), _NN)
        n_iter = int(math.log2(L)) - 1
        for j in range(n_iter):
            last = j == n_iter - 1
            for b in blk:
                lhs_j = inv[b] if last else jnp.concatenate([inv[b], pw[b]], axis=0)
                prod = _mm(lhs_j, bd(pw[b]), _NN)
                inv[b] = inv[b] + prod[:L]
                if not last:
                    pw[b] = prod[L:]
        for b in blk:
            s, g = b
            tot = cum[(s + 1) * L - 1:(s + 1) * L, g * G:(g + 1) * G]
            e_end = jnp.exp(tot - cut(cum, b))
            mv[b] = _mm(m_kr[b], bd(cut(v, b)), _NN)
            vk[b] = _mm(cut(v, b), cut(k, b) * e_end, _TN)
            a_end[b] = cut(ap, b) * e_end
            decay[b] = jnp.exp(tot)

    def apply(s):
        for g in range(n_groups):
            b = (s, g)
            st = st_scr[g]
            s_terms = _mm(lhs[b], st, _NT)
            u = _mm(inv[b], bd(s_terms[:L] + mv[b][:L]), _NN)
            y_scr[s * L:(s + 1) * L, g * G:(g + 1) * G] = s_terms[L:] + mv[b][L:] - _mm(m_a[b], bd(u), _NN)
            st_scr[g] = jnp.where(mask_bd, st * decay[b] + vk[b] - _mm(u, a_end[b], _TN), 0.0)

    n_chunks = TB // L
    prep(range(min(RW_AHEAD, n_chunks)))
    for s in range(n_chunks):
        apply(s)
        if s + RW_AHEAD < n_chunks:
            prep([s + RW_AHEAD])


def _softplus(z):
    return jnp.maximum(z, 0.0) + jnp.log(1.0 + jnp.exp(-jnp.abs(z)))


def _rwkv_fused_body(*refs, nc, has_vres):
    if has_vres:
        (p_ref, pv_ref, vf_ref, mu_ref, vec_ref, wa_ref, g2_ref, seg_ref, vmu_ref, v2_ref,
         o_ref, sf_ref, st_scr, prev_scr, y_scr, prevv_scr) = refs
    else:
        (p_ref, mu_ref, vec_ref, wa_ref, g2_ref, seg_ref,
         o_ref, vfo_ref, sf_ref, st_scr, prev_scr, y_scr) = refs
    TB, W, L = RW_TB, RW_W, RW_CHUNK
    c = pl.program_id(1)

    @pl.when(c == 0)
    def _():
        st_scr[...] = jnp.zeros_like(st_scr)
        prev_scr[...] = jnp.zeros_like(prev_scr)
        if has_vres:
            prevv_scr[...] = jnp.zeros_like(prevv_scr)

    first_row = lax.broadcasted_iota(jnp.int32, (TB, 1), 0) == 0

    def shift_mix(x, carry_ref, mu):
        prev = jnp.where(first_row, carry_ref[...], pltpu.roll(x, 1, 0))
        carry_ref[...] = x[TB - 1:TB, :]
        return x + (prev - x) * mu

    mixed = shift_mix(p_ref[0], prev_scr, mu_ref[...])
    r = mixed[:, 0:W]
    k = mixed[:, W:2 * W]
    v = mixed[:, 2 * W:3 * W]
    xwa = mixed[:, 3 * W:3 * W + LANES]
    xg = mixed[:, 3 * W + LANES:3 * W + 2 * LANES]
    vec = vec_ref[...]
    w0, a0, kk_s, ka, rk, lnx_g, lnx_b, v0 = (vec[i:i + 1, :] for i in range(RW_VEC_ROWS))
    seg = seg_ref[...]
    wa = wa_ref[...]
    w_lora = _dot(jnp.tanh(xwa).astype(BF16), wa[:, 0:W])
    a_lora = _dot(xwa.astype(BF16), wa[:, W:2 * W])
    lw = -jnp.exp(-_softplus(-(w0 + w_lora)) - 0.5)
    a = jax.nn.sigmoid(a0 + a_lora)
    g = _dot(jax.nn.sigmoid(xg).astype(BF16), g2_ref[...])
    if has_vres:
        xv = shift_mix(pv_ref[0], prevv_scr, vmu_ref[...])
        v = v + (vf_ref[0] - v) * jax.nn.sigmoid(v0 + _dot(xv.astype(BF16), v2_ref[...]))
    else:
        vfo_ref[0] = v
    kk = k * kk_s
    kk = kk * lax.rsqrt(jnp.maximum(_seg_sum(kk * kk, seg), 1e-24))
    k = k * (1.0 + (a - 1.0) * ka)

    _rw_scan(r, lw, k, v, kk, a, st_scr, y_scr)

    y = y_scr[...]
    inv_n = 1.0 / RW_HD
    y_mu = _seg_sum(y, seg) * inv_n
    yc = y - y_mu
    y_var = _seg_sum(yc * yc, seg) * inv_n
    y = yc * lax.rsqrt(y_var + RW_GN_EPS) * lnx_g + lnx_b
    bonus = _seg_sum(r * k * rk, seg) * v
    o_ref[0] = ((y + bonus) * g).astype(BF16)

    @pl.when(c == nc - 1)
    def _():
        sf_ref[0] = st_scr[...]


def _rwkv_prompt(p3, lp, v_first):
    b, t, _ = p3.shape
    nc = t // RW_TB
    has_vres = v_first is not None
    ng = RW_HEADS // RW_GH
    zpad = jnp.zeros((RW_LORA_W, RW_W), F32)
    wa = jnp.concatenate([jnp.concatenate([lp['rw_w2'], zpad], 0), jnp.concatenate([zpad, lp['rw_a2']], 0)], 1)
    vec = jnp.stack([lp['rw_w0'], lp['rw_a0'], lp['rw_kk'], lp['rw_ka'], lp['rw_rk'], lp['rw_lnx_g'], lp['rw_lnx_b'],
                     lp['rw_v0'] if has_vres else jnp.zeros((RW_W,), F32)])
    hid = jnp.arange(RW_W) // RW_HD
    seg = (hid[:, None] == hid[None, :]).astype(BF16)
    full = lambda shape: pl.BlockSpec(shape, lambda i, j: (0,) * len(shape))
    seq = lambda w, blk: pl.BlockSpec((1, RW_TB, w), lambda i, j: (i, j, blk))
    in_specs = [seq(RW_P, 0)]
    args = [p3]
    if has_vres:
        in_specs += [seq(LANES, P_V1 // LANES), seq(RW_W, 0)]
        args += [p3, v_first]
    in_specs += [full((1, RW_P)), full((RW_VEC_ROWS, RW_W)), full((LANES, 2 * RW_W)), full((RW_LORA_G, RW_W)),
                 full((RW_W, RW_W))]
    args += [lp['rw_mu'].reshape(1, RW_P), vec, wa.astype(BF16), lp['rw_g2'].astype(BF16), seg]
    if has_vres:
        in_specs += [full((1, LANES)), full((LANES, RW_W))]
        args += [jnp.pad(lp['rw_vmu'], (0, LANES - RW_LORA_V)).reshape(1, LANES),
                 jnp.pad(lp['rw_v2'], ((0, LANES - RW_LORA_V), (0, 0))).astype(BF16)]
    out_specs = [seq(RW_W, 0)]
    out_shape = [jax.ShapeDtypeStruct((b, t, RW_W), BF16)]
    if not has_vres:
        out_specs.append(seq(RW_W, 0))
        out_shape.append(jax.ShapeDtypeStruct((b, t, RW_W), F32))
    out_specs.append(pl.BlockSpec((1, ng, RW_GW, RW_GW), lambda i, j: (i, 0, 0, 0)))
    out_shape.append(jax.ShapeDtypeStruct((b, ng, RW_GW, RW_GW), F32))
    scratch = [pltpu.VMEM((ng, RW_GW, RW_GW), F32), pltpu.VMEM((1, RW_P), F32), pltpu.VMEM((RW_TB, RW_W), F32)]
    if has_vres:
        scratch.append(pltpu.VMEM((1, LANES), F32))
    outs = pl.pallas_call(
        functools.partial(_rwkv_fused_body, nc=nc, has_vres=has_vres),
        grid=(b, nc),
        in_specs=in_specs,
        out_specs=out_specs,
        out_shape=out_shape,
        scratch_shapes=scratch,
        compiler_params=_cparams(("parallel", "arbitrary")),
        name="rwkv_fused",
    )(*args)
    if has_vres:
        o, st_bd = outs
    else:
        o, v_first, st_bd = outs
    st5 = st_bd.reshape(b, ng, RW_GH, RW_HD, RW_GH, RW_HD)
    s_fin = jnp.stack([st5[:, :, h, :, h, :] for h in range(RW_GH)], axis=2)
    s_fin = s_fin.reshape(b, RW_HEADS, RW_HD, RW_HD).transpose(0, 1, 3, 2)
    return o, v_first, s_fin


def _ret_log_gamma(h):
    return math.log1p(-(2.0 ** (-5.0 - h)))


def _rotary_tables(pos, heads, dk):
    half = dk // 2
    inv = ROPE_BASE ** (-jnp.arange(half, dtype=F32) / half)
    ang = pos.astype(F32)[:, None] * inv[None, :]
    cos = jnp.tile(jnp.concatenate([jnp.cos(ang), jnp.cos(ang)], -1), (1, heads))
    sin = jnp.tile(jnp.concatenate([-jnp.sin(ang), jnp.sin(ang)], -1), (1, heads))
    lane = jnp.arange(heads * dk)
    perm = (lane[:, None] == (lane[None, :] ^ half)).astype(BF16)
    return cos, sin, perm


def _ret_fused_body(p_ref, cos_ref, sin_ref, perm_ref, o_ref, sf_ref, s_scr, *, nc):
    L, DK, DV, H = CHUNK, RET_QK, RET_V, RET_HEADS
    nq = H * DK
    c = pl.program_id(1)

    @pl.when(c == 0)
    def _():
        s_scr[...] = jnp.zeros_like(s_scr)

    cos = cos_ref[...]
    sin = sin_ref[...]
    perm = perm_ref[...]
    rot = lambda x: x * cos + _exact_rhs_dot(x, perm) * sin
    q_all = rot(p_ref[0, :, 0:nq])
    k_all = rot(p_ref[0, :, nq:2 * nq]) * (DK ** -0.5)
    row = lax.broadcasted_iota(jnp.int32, (L, L), 0)
    col = lax.broadcasted_iota(jnp.int32, (L, L), 1)
    rel = (row - col).astype(F32)
    idx = lax.broadcasted_iota(jnp.int32, (L, 1), 0).astype(F32)
    vs = lambda h: p_ref[0, :, 2 * nq + h * DV:2 * nq + (h + 1) * DV].astype(BF16)
    hd = [dict() for _ in range(H)]
    for h, t in enumerate(hd):
        lg = _ret_log_gamma(h)
        q = q_all[:, h * DK:(h + 1) * DK]
        k = k_all[:, h * DK:(h + 1) * DK]
        t['s_prev'] = s_scr[h]
        t['dmask'] = jnp.where(rel >= 0, jnp.exp(jnp.maximum(rel, 0.0) * lg), 0.0)
        t['qk'] = _dot_nt(q.astype(BF16), k.astype(BF16))
        q_dec = q * jnp.exp((idx + 1.0) * lg)
        t['qs'] = _dot(q_dec.astype(BF16), t['s_prev'].astype(BF16))
        k_end = k * jnp.exp((L - 1.0 - idx) * lg)
        t['kv'] = _dot_tn(k_end.astype(BF16), vs(h))
    for h, t in enumerate(hd):
        y = _dot((t['qk'] * t['dmask']).astype(BF16), vs(h)) + t['qs']
        y = y * lax.rsqrt(jnp.mean(y * y, -1, keepdims=True) + NORM_EPS)
        gate = p_ref[0, :, 2 * nq + RET_W + h * DV:2 * nq + RET_W + (h + 1) * DV]
        o_ref[0, :, h * DV:(h + 1) * DV] = (gate * jax.nn.sigmoid(gate) * y).astype(BF16)
    for h, t in enumerate(hd):
        s_scr[h] = math.exp(L * _ret_log_gamma(h)) * t['s_prev'] + t['kv']

    @pl.when(c == nc - 1)
    def _():
        sf_ref[0] = s_scr[...]


def _ret_prompt(p3, pos):
    b, t, _ = p3.shape
    L = CHUNK
    nc = t // L
    nq = RET_HEADS * RET_QK
    cos, sin, perm = _rotary_tables(pos, RET_HEADS, RET_QK)
    tab = pl.BlockSpec((L, nq), lambda i, j: (j, 0))
    st = pl.BlockSpec((1, RET_HEADS, RET_QK, RET_V), lambda i, j: (i, 0, 0, 0))
    return pl.pallas_call(
        functools.partial(_ret_fused_body, nc=nc),
        grid=(b, nc),
        in_specs=[pl.BlockSpec((1, L, P_MAIN), lambda i, j: (i, j, P_RET // P_MAIN)), tab, tab,
                  pl.BlockSpec((nq, nq), lambda i, j: (0, 0))],
        out_specs=[pl.BlockSpec((1, L, RET_W), lambda i, j: (i, j, 0)), st],
        out_shape=[jax.ShapeDtypeStruct((b, t, RET_W), BF16),
                   jax.ShapeDtypeStruct((b, RET_HEADS, RET_QK, RET_V), F32)],
        scratch_shapes=[pltpu.VMEM((RET_HEADS, RET_QK, RET_V), F32)],
        compiler_params=_cparams(("parallel", "arbitrary")),
        name="ret_fused",
    )(p3, cos, sin, perm)


ML_HPAD = 8


def _ml_fused_body(p_ref, gate_ref, bias_ref, norm_ref, sel_ref, o_ref, cf_ref, nf_ref, mf_ref,
                   c_scr, n_scr, m_scr, *, nc):
    L, DK, DV, H = CHUNK, ML_QK, ML_V, ML_HEADS
    nq = H * DK
    ci = pl.program_id(1)

    @pl.when(ci == 0)
    def _():
        c_scr[...] = jnp.zeros_like(c_scr)
        n_scr[...] = jnp.zeros_like(n_scr)
        m_scr[...] = jnp.zeros_like(m_scr)

    row = lax.broadcasted_iota(jnp.int32, (L, L), 0)
    col = lax.broadcasted_iota(jnp.int32, (L, L), 1)
    causal = row >= col
    tri = causal.astype(BF16)
    capped = ML_GATE_CAP * jnp.tanh((gate_ref[0] + bias_ref[...]) * (1.0 / ML_GATE_CAP))
    lane = lax.broadcasted_iota(jnp.int32, (L, LANES), 1)
    g = jnp.where(lane < H, capped, jnp.where(lane < 2 * H, -_softplus(-capped), 0.0))
    g_rep = _exact_rhs_dot(g, sel_ref[...])
    b_rep_all = _exact_lhs_dot(tri, g_rep[:, H * LANES:])
    g_t = g.T
    cum_t = _exact_lhs_dot(tri, g).T
    cm_all = g_rep[:, :H * LANES] - b_rep_all
    row_id = lax.broadcasted_iota(jnp.int32, (L, 1), 0)
    shift = 1
    while shift < L:
        cm_all = jnp.maximum(cm_all, jnp.where(row_id >= shift, pltpu.roll(cm_all, shift, 0), -jnp.inf))
        shift *= 2
    ones = jnp.ones((L, LANES), BF16)
    mean_w = jnp.full((DV, LANES), 1.0 / DV, BF16)
    m_all = m_scr[...]
    hd = [dict() for _ in range(H)]
    for h, t in enumerate(hd):
        hs = slice(h * LANES, (h + 1) * LANES)
        q = p_ref[0, :, h * DK:(h + 1) * DK].astype(BF16)
        k = p_ref[0, :, nq + h * DK:nq + (h + 1) * DK] * (DK ** -0.5)
        t['v1'] = jnp.concatenate([p_ref[0, :, 2 * nq + h * DV:2 * nq + (h + 1) * DV].astype(BF16), ones], axis=1)
        ig_rep = g_rep[:, hs]
        b_rep = b_rep_all[:, hs]
        ig_row = g_t[h:h + 1, :]
        b_row = cum_t[H + h:H + h + 1, :]
        b_tot = b_rep[L - 1:L, :]
        m_prev = m_all[h:h + 1, :]
        t['c_prev'] = c_scr[h]
        t['n_prev'] = n_scr[h]
        t['m_new'] = jnp.maximum(b_tot + m_prev, jnp.max(b_tot - b_rep + ig_rep, axis=0, keepdims=True))
        t['dec'] = jnp.exp(b_tot + m_prev - t['m_new'])
        kw = k * jnp.exp((b_tot - b_rep + ig_rep - t['m_new'])[:, :DK])
        t['kvn'] = _dot_tn(kw.astype(BF16), t['v1'])
        inter = b_rep + m_prev
        t['m_i'] = b_rep + jnp.maximum(cm_all[:, hs], m_prev)
        t['e'] = jnp.exp(jnp.where(causal, (b_rep - t['m_i']) - b_row + ig_row, -jnp.inf))
        t['sc'] = jnp.exp(inter - t['m_i'])
        t['qk'] = _dot_nt(q, k.astype(BF16))
        cn = jnp.concatenate([t['c_prev'], t['n_prev']], axis=1).astype(BF16)
        t['qcn'] = _dot(q, cn)
    for h, t in enumerate(hd):
        nd = _dot((t['qk'] * t['e']).astype(BF16), t['v1'])
        num = nd[:, :DV] + t['sc'] * t['qcn'][:, :DV]
        den = nd[:, DV:] + t['sc'] * t['qcn'][:, DV:]
        hid = num / jnp.maximum(jnp.abs(den), jnp.exp(-t['m_i']))
        hid = hid * lax.rsqrt(_seg_sum(hid * hid, mean_w) + NORM_EPS)
        og = p_ref[0, :, 2 * nq + ML_W + h * DV:2 * nq + ML_W + (h + 1) * DV]
        o_ref[0, :, h * DV:(h + 1) * DV] = (jax.nn.sigmoid(og) * (hid * norm_ref[:, h * DV:(h + 1) * DV])).astype(BF16)
    for h, t in enumerate(hd):
        c_scr[h] = t['dec'] * t['c_prev'] + t['kvn'][:, :DV]
        n_scr[h] = t['dec'] * t['n_prev'] + t['kvn'][:, DV:]
        m_scr[h:h + 1, :] = t['m_new']

    @pl.when(ci == nc - 1)
    def _():
        cf_ref[0] = c_scr[...]
        nf_ref[0] = n_scr[...]
        mf_ref[0] = m_scr[...]


def _ml_prompt(p3, lp):
    b, t, _ = p3.shape
    L = CHUNK
    nc = t // L
    bias = jnp.pad(jnp.concatenate([lp['ml_ib'], lp['ml_fb']]), (0, LANES - 2 * ML_HEADS)).reshape(1, LANES)
    n_rep = 2 * ML_HEADS * LANES
    sel = (jnp.arange(LANES)[:, None] == jnp.arange(n_rep)[None, :] // LANES).astype(BF16)
    vs = pl.BlockSpec((1, L, ML_W), lambda i, j: (i, j, 0))
    cs = pl.BlockSpec((1, ML_HEADS, ML_QK, ML_V), lambda i, j: (i, 0, 0, 0))
    ns = pl.BlockSpec((1, ML_HEADS, ML_QK, LANES), lambda i, j: (i, 0, 0, 0))
    ms = pl.BlockSpec((1, ML_HPAD, LANES), lambda i, j: (i, 0, 0))
    o, c_f, n_f, m_f = pl.pallas_call(
        functools.partial(_ml_fused_body, nc=nc),
        grid=(b, nc),
        in_specs=[pl.BlockSpec((1, L, P_MAIN), lambda i, j: (i, j, P_ML // P_MAIN)),
                  pl.BlockSpec((1, L, LANES), lambda i, j: (i, j, P_GATE // LANES)),
                  pl.BlockSpec((1, LANES), lambda i, j: (0, 0)),
                  pl.BlockSpec((1, ML_W), lambda i, j: (0, 0)),
                  pl.BlockSpec((LANES, n_rep), lambda i, j: (0, 0))],
        out_specs=[vs, cs, ns, ms],
        out_shape=[jax.ShapeDtypeStruct((b, t, ML_W), BF16),
                   jax.ShapeDtypeStruct((b, ML_HEADS, ML_QK, ML_V), F32),
                   jax.ShapeDtypeStruct((b, ML_HEADS, ML_QK, LANES), F32),
                   jax.ShapeDtypeStruct((b, ML_HPAD, LANES), F32)],
        scratch_shapes=[pltpu.VMEM((ML_HEADS, ML_QK, ML_V), F32),
                        pltpu.VMEM((ML_HEADS, ML_QK, LANES), F32),
                        pltpu.VMEM((ML_HPAD, LANES), F32)],
        compiler_params=_cparams(("parallel", "arbitrary")),
        name="ml_fused",
    )(p3, p3, bias, lp['ml_norm'].reshape(1, ML_W), sel)
    return o, c_f, n_f[..., 0], m_f[:, :ML_HEADS, 0]


def _to_cols(x):
    b, c = x.shape
    cols = x.reshape(b // DEC_TB, DEC_TB, c).transpose(0, 2, 1)
    return jnp.pad(cols, ((0, 0), (0, 0), (0, DEC_TB)))


def _col_selector():
    j = jnp.arange(2 * DEC_TB)[:, None]
    return (j == jnp.arange(DEC_TB * LANES)[None, :] // LANES).astype(BF16)


def _rwkv_step_body(w_ref, ap_ref, k_ref, kk_ref, r_ref, v_ref, s_ref, y_ref, so_ref):
    N = RW_HD
    v = v_ref[0]

    def sa_step(i, acc):
        return acc + kk_ref[0, pl.ds(i, 1), :] * s_ref[i]

    sa = lax.fori_loop(0, N, sa_step, jnp.zeros_like(v), unroll=8)

    def upd_step(i, y):
        row = lambda ref: ref[0, pl.ds(i, 1), :]
        s_new = row(w_ref) * s_ref[i] - row(ap_ref) * sa + row(k_ref) * v
        so_ref[i] = s_new
        return y + row(r_ref) * s_new

    y_ref[0] = lax.fori_loop(0, N, upd_step, jnp.zeros_like(v), unroll=8)


def _rwkv_step(wdec, ap, k, kk, r, v, s_all, layer):
    b, w = v.shape
    heads = lambda x: x.reshape(b, RW_HEADS, RW_HD).transpose(1, 2, 0)
    vec = pl.BlockSpec((1, RW_HD, b), lambda h: (h, 0, 0))
    y, s_new = pl.pallas_call(
        _rwkv_step_body,
        grid=(RW_HEADS,),
        in_specs=[vec] * 6 + [pl.BlockSpec((None, None, RW_HD, RW_HD, b), lambda h: (layer, h, 0, 0, 0))],
        out_specs=[vec, pl.BlockSpec((None, RW_HD, RW_HD, b), lambda h: (h, 0, 0, 0))],
        out_shape=[jax.ShapeDtypeStruct((RW_HEADS, RW_HD, b), F32),
                   jax.ShapeDtypeStruct((RW_HEADS, RW_HD, RW_HD, b), F32)],
        compiler_params=_cparams(("parallel",)),
        name="rwkv_step",
    )(heads(wdec), heads(ap), heads(k), heads(kk), heads(r), heads(v), s_all)
    return y.transpose(2, 0, 1).reshape(b, w), s_new


def _ret_step_body(kc_ref, q_ref, k_ref, v_ref, esel_ref, seg_ref, s_ref, y_ref, so_ref):
    DK, DV = RET_QK, RET_V
    q = q_ref[...]
    qk_v = _exact_rhs_dot(q * k_ref[...], seg_ref[...])
    row_id = lax.broadcasted_iota(jnp.int32, (DEC_TB, 1), 0)
    for h in range(RET_HEADS):
        gamma = math.exp(_ret_log_gamma(h))
        ks = slice(h * DK, (h + 1) * DK)
        vs = slice(h * DV, (h + 1) * DV)
        k_rep = _exact_rhs_dot(kc_ref[0, ks, :], esel_ref[...])
        q_h = q[:, ks].astype(BF16)
        v_h = v_ref[:, vs]
        qs = jnp.zeros((DEC_TB, DV), F32)
        for j in range(DEC_TB):
            s = s_ref[j, h]
            so_ref[j, h] = gamma * s + k_rep[:, j * LANES:(j + 1) * LANES] * v_h[j:j + 1, :]
            qs = jnp.where(row_id == j, _dot(q_h, s.astype(BF16)), qs)
        y_ref[:, vs] = qk_v[:, vs] * v_h + gamma * qs


def _ret_step(q, k, v, s_all, layer):
    b = q.shape[0]
    nqk = RET_HEADS * RET_QK
    cols = pl.BlockSpec((1, nqk, 2 * DEC_TB), lambda i: (i, 0, 0))
    qk_rows = pl.BlockSpec((DEC_TB, nqk), lambda i: (i, 0))
    rows = pl.BlockSpec((DEC_TB, RET_W), lambda i: (i, 0))
    esel = pl.BlockSpec((2 * DEC_TB, DEC_TB * LANES), lambda i: (0, 0))
    seg = (jnp.arange(nqk)[:, None] // RET_QK == jnp.arange(RET_W)[None, :] // RET_V).astype(BF16)
    st = pl.BlockSpec((DEC_TB, RET_HEADS, RET_QK, RET_V), lambda i: (i, 0, 0, 0))
    st_in = pl.BlockSpec((None, DEC_TB, RET_HEADS, RET_QK, RET_V), lambda i: (layer, i, 0, 0, 0))
    return pl.pallas_call(
        _ret_step_body,
        grid=(b // DEC_TB,),
        in_specs=[cols, qk_rows, qk_rows, rows, esel, pl.BlockSpec((nqk, RET_W), lambda i: (0, 0)), st_in],
        out_specs=[rows, st],
        out_shape=[jax.ShapeDtypeStruct((b, RET_W), F32), jax.ShapeDtypeStruct(s_all.shape[1:], F32)],
        compiler_params=_cparams(("parallel",)),
        name="ret_step",
    )(_to_cols(k), q, k, v, _col_selector(), seg, s_all)


def _ml_step_body(kc_ref, q_ref, k_ref, v_ref, ig_ref, lf_ref, esel_ref, hsel_ref, hsel_k_ref, seg_ref,
                  c_ref, n_ref, m_ref, h_ref, co_ref, no_ref, mo_ref):
    DK, DV, H = ML_QK, ML_V, ML_HEADS
    ig, lf, m_prev = ig_ref[...], lf_ref[...], m_ref[...]
    m_new = jnp.maximum(lf + m_prev, ig)
    dec = jnp.exp(lf + m_prev - m_new)
    wgt = jnp.exp(ig - m_new)
    mo_ref[...] = m_new
    hsel = hsel_ref[...]
    dec_v = _exact_rhs_dot(dec, hsel)
    wgt_v = _exact_rhs_dot(wgt, hsel)
    floor_v = _exact_rhs_dot(jnp.exp(-m_new), hsel)
    q, k, n_prev = q_ref[...], k_ref[...], n_ref[...]
    no_ref[...] = _exact_rhs_dot(dec, hsel_k_ref[...]) * n_prev + k * _exact_rhs_dot(wgt, hsel_k_ref[...])
    s_v = _exact_rhs_dot(q * k, seg_ref[...]) * wgt_v
    den_v = s_v + dec_v * _exact_rhs_dot(q * n_prev, seg_ref[...])
    row_id = lax.broadcasted_iota(jnp.int32, (DEC_TB, 1), 0)
    for h in range(H):
        ks = slice(h * DK, (h + 1) * DK)
        vs = slice(h * DV, (h + 1) * DV)
        k_rep = _exact_rhs_dot(kc_ref[0, ks, :], esel_ref[...])
        q_h = q[:, ks].astype(BF16)
        v_h = v_ref[:, vs]
        kv_scale = wgt_v[:, vs] * v_h
        qc = jnp.zeros((DEC_TB, DV), F32)
        for j in range(DEC_TB):
            c_prev = c_ref[j, h]
            co_ref[j, h] = dec_v[j:j + 1, vs] * c_prev + k_rep[:, j * LANES:(j + 1) * LANES] * kv_scale[j:j + 1, :]
            qc = jnp.where(row_id == j, _dot(q_h, c_prev.astype(BF16)), qc)
        num = s_v[:, vs] * v_h + dec_v[:, vs] * qc
        h_ref[:, vs] = num / jnp.maximum(jnp.abs(den_v[:, vs]), floor_v[:, vs])


def _ml_step(q, k, v, ig, lf, c_all, layer, n0, m0):
    b = q.shape[0]
    nqk = ML_HEADS * ML_QK
    pad_h = lambda x: jnp.pad(x, ((0, 0), (0, LANES - ML_HEADS)))
    head = jnp.arange(LANES)[:, None]
    hsel = (head == jnp.arange(ML_W)[None, :] // ML_V).astype(BF16)
    hsel_k = (head == jnp.arange(nqk)[None, :] // ML_QK).astype(BF16)
    seg = (jnp.arange(nqk)[:, None] // ML_QK == jnp.arange(ML_W)[None, :] // ML_V).astype(BF16)
    const = lambda r, c: pl.BlockSpec((r, c), lambda i: (0, 0))
    cols = pl.BlockSpec((1, nqk, 2 * DEC_TB), lambda i: (i, 0, 0))
    qk_rows = pl.BlockSpec((DEC_TB, nqk), lambda i: (i, 0))
    rows = pl.BlockSpec((DEC_TB, ML_W), lambda i: (i, 0))
    sc = pl.BlockSpec((DEC_TB, LANES), lambda i: (i, 0))
    cs = pl.BlockSpec((DEC_TB, ML_HEADS, ML_QK, ML_V), lambda i: (i, 0, 0, 0))
    cs_in = pl.BlockSpec((None, DEC_TB, ML_HEADS, ML_QK, ML_V), lambda i: (layer, i, 0, 0, 0))
    hm, c_new, n_new, m_new = pl.pallas_call(
        _ml_step_body,
        grid=(b // DEC_TB,),
        in_specs=[cols, qk_rows, qk_rows, rows, sc, sc, const(2 * DEC_TB, DEC_TB * LANES), const(LANES, ML_W),
                  const(LANES, nqk), const(nqk, ML_W), cs_in, qk_rows, sc],
        out_specs=[rows, cs, qk_rows, sc],
        out_shape=[jax.ShapeDtypeStruct((b, ML_W), F32), jax.ShapeDtypeStruct(c_all.shape[1:], F32),
                   jax.ShapeDtypeStruct((b, nqk), F32), jax.ShapeDtypeStruct((b, LANES), F32)],
        compiler_params=_cparams(("parallel",)),
        name="ml_step",
    )(_to_cols(k), q, k, v, pad_h(ig), pad_h(lf), _col_selector(), hsel, hsel_k, seg,
      c_all, n0.reshape(b, nqk), pad_h(m0))
    return hm, c_new, n_new.reshape(n0.shape), m_new[:, :ML_HEADS]


def _heads(a, h):
    return a.reshape(a.shape[:-1] + (h, a.shape[-1] // h))


def _shift_prev(p, prev_row):
    return jnp.concatenate([prev_row[:, None, :], p[:, :-1]], axis=1)


def _rotary(x, pos):
    half = x.shape[-1] // 2
    inv = ROPE_BASE ** (-jnp.arange(half, dtype=F32) / half)
    ang = pos.astype(F32)[:, None] * inv[None, :]
    cos = jnp.cos(ang)[None, :, None, :]
    sin = jnp.sin(ang)[None, :, None, :]
    x1, x2 = x[..., :half], x[..., half:]
    return jnp.concatenate([x1 * cos - x2 * sin, x1 * sin + x2 * cos], -1)


def _small_matmul(x, w):
    lead = x.shape[:-1]
    kdim, n = w.shape
    x2 = x.reshape(-1, kdim)
    m = x2.shape[0]
    kp = -(-kdim // LANES) * LANES
    npad = -(-n // LANES) * LANES
    x2 = jnp.pad(x2.astype(BF16), ((0, 0), (0, kp - kdim)))
    w2 = jnp.pad(w.astype(BF16), ((0, kp - kdim), (0, npad - n)))
    tm = 1024 if m % 1024 == 0 else m
    out = _matmul(x2, w2, tm, npad)
    return out[:, :n].reshape(lead + (n,))


def _mix_prompt(p, pos, v_first, lp):
    o_rw, v_first, s_new = _rwkv_prompt(p, lp, v_first)
    o_ret, r_new = _ret_prompt(p, pos)
    o_ml, c_new, n_new, m_new = _ml_prompt(p, lp)
    return (o_rw, o_ret, o_ml), v_first, (s_new, r_new, c_new, n_new, m_new)


def _mix_sample(p, pos, v_first, st, lp, prev_row):
    bsz, t_len, _ = p.shape

    p_rw = p[..., :RW_P]
    mixed = p_rw + (_shift_prev(p_rw, prev_row[:, :RW_P]) - p_rw) * lp['rw_mu']
    sizes = np.cumsum([RW_W, RW_W, RW_W, RW_LORA_W, RW_LORA_A, RW_LORA_G])[:-1]
    r, k, v, xw, xa, xg = jnp.split(mixed, [int(s) for s in sizes], axis=-1)
    lora_in = [jnp.tanh(xw), xa, jax.nn.sigmoid(xg)]
    lora_w = [lp['rw_w2'], lp['rw_a2'], lp['rw_g2']]
    if v_first is not None:
        pv = p[..., P_V1:P_V1 + RW_LORA_V]
        lora_in.append(pv + (_shift_prev(pv, prev_row[:, P_V1:P_V1 + RW_LORA_V]) - pv) * lp['rw_vmu'])
        lora_w.append(lp['rw_v2'])
    lora = _small_matmul(jnp.concatenate(lora_in, -1), jax.scipy.linalg.block_diag(*lora_w))
    w = -jax.nn.softplus(-(lp['rw_w0'] + lora[..., :RW_W])) - 0.5
    a = jax.nn.sigmoid(lp['rw_a0'] + lora[..., RW_W:2 * RW_W])
    g = lora[..., 2 * RW_W:3 * RW_W]
    if v_first is None:
        v_first = v
    else:
        v = v + (v_first - v) * jax.nn.sigmoid(lp['rw_v0'] + lora[..., 3 * RW_W:])
    kk = _heads(k * lp['rw_kk'], RW_HEADS)
    kk = kk * lax.rsqrt(jnp.maximum(jnp.sum(jnp.square(kk), -1, keepdims=True), 1e-24))
    kk = kk.reshape(bsz, t_len, RW_W)
    k = k * (1.0 + (a - 1.0) * lp['rw_ka'])
    lw = -jnp.exp(w)
    y, s_new = _rwkv_step(jnp.exp(lw)[:, 0], (kk * a)[:, 0], k[:, 0], kk[:, 0], r[:, 0], v[:, 0],
                          st['rw_wkv_t'], st['layer'])
    y = _heads(y[:, None, :], RW_HEADS)
    y_mu = jnp.mean(y, -1, keepdims=True)
    y_var = jnp.mean(jnp.square(y - y_mu), -1, keepdims=True)
    y = ((y - y_mu) * lax.rsqrt(y_var + RW_GN_EPS)).reshape(bsz, t_len, RW_W)
    y = y * lp['rw_lnx_g'] + lp['rw_lnx_b']
    rh, kh, vh = (_heads(u, RW_HEADS) for u in (r, k, v))
    bonus = jnp.sum(rh * kh * _heads(lp['rw_rk'], RW_HEADS), -1, keepdims=True) * vh
    o_rw = ((y + bonus.reshape(bsz, t_len, RW_W)) * g).astype(BF16)

    nqk = RET_HEADS * RET_QK
    p_ret = p[..., P_RET:P_RET + P_MAIN]
    qr, kr, vr, gr = (p_ret[..., :nqk], p_ret[..., nqk:2 * nqk],
                      p_ret[..., 2 * nqk:2 * nqk + RET_W], p_ret[..., 2 * nqk + RET_W:])
    qh = _rotary(_heads(qr, RET_HEADS), pos).reshape(bsz, t_len, nqk)
    khr = (_rotary(_heads(kr, RET_HEADS), pos) * (RET_QK ** -0.5)).reshape(bsz, t_len, nqk)
    yr, r_new = _ret_step(qh[:, 0], khr[:, 0], vr[:, 0], st['ret_all'], st['layer'])
    yr = _heads(yr[:, None, :], RET_HEADS)
    yr = yr * lax.rsqrt(jnp.mean(jnp.square(yr), -1, keepdims=True) + NORM_EPS)
    o_ret = (jax.nn.silu(gr) * yr.reshape(bsz, t_len, RET_W)).astype(BF16)

    nqk = ML_HEADS * ML_QK
    p_ml = p[..., P_ML:P_ML + P_MAIN]
    qm, km, vm, om = (p_ml[..., :nqk], p_ml[..., nqk:2 * nqk],
                      p_ml[..., 2 * nqk:2 * nqk + ML_W], p_ml[..., 2 * nqk + ML_W:])
    im = p[..., P_GATE:P_GATE + ML_HEADS]
    fm = p[..., P_GATE + ML_HEADS:P_GATE + 2 * ML_HEADS]
    ig = ML_GATE_CAP * jnp.tanh((im + lp['ml_ib']) / ML_GATE_CAP)
    lf = jax.nn.log_sigmoid(ML_GATE_CAP * jnp.tanh((fm + lp['ml_fb']) / ML_GATE_CAP))
    km = km * (ML_QK ** -0.5)
    hm, c_new, n_new, m_new = _ml_step(qm[:, 0], km[:, 0], vm[:, 0], ig[:, 0], lf[:, 0],
                                       st['ml_c_all'], st['layer'], st['ml_n'], st['ml_m'])
    hm = _heads(hm[:, None, :], ML_HEADS)
    hm = hm * lax.rsqrt(jnp.mean(jnp.square(hm), -1, keepdims=True) + NORM_EPS)
    o_ml = (jax.nn.sigmoid(om) * (hm.reshape(bsz, t_len, ML_W) * lp['ml_norm'])).astype(BF16)

    return (o_rw, o_ret, o_ml), v_first, (s_new, r_new, c_new, n_new, m_new)


_T_V1 = P_V1 // LANES
_T_GATE = P_GATE // LANES
_T_RET = P_RET // LANES
_T_SHIFT = (P_RET - RW_P) // LANES
_T_SRC_GATE = (RW_P + RET_P + 2 * ML_HEADS * ML_QK + 2 * ML_W) // LANES


def _pack_body(w_ref, v1_ref, *o_refs):
    j = pl.program_id(0)
    row = lax.broadcasted_iota(jnp.int32, (LANES, 1), 0)
    spare = jnp.logical_and(j > _T_GATE, j < _T_RET)
    for l, o_ref in enumerate(o_refs):
        w = w_ref[:, l, :]
        gates = jnp.where(row < 2 * ML_HEADS, w, 0.0)
        out = jnp.where(j == _T_V1, v1_ref[l], jnp.where(j == _T_GATE, gates, jnp.where(spare, 0.0, w)))
        o_ref[...] = out.astype(BF16)


def _pack_w_in(w_in, rw_v1):
    depth, d, _ = w_in.shape
    w_t = w_in.transpose(2, 0, 1)
    v1_t = jnp.pad(rw_v1.transpose(0, 2, 1), ((1, 0), (0, LANES - RW_LORA_V), (0, 0)))

    def src_tile(j):
        return jnp.where(j < _T_V1, j, jnp.where(j == _T_GATE, _T_SRC_GATE, j - _T_SHIFT))

    return pl.pallas_call(
        _pack_body,
        grid=(P_PAD // LANES,),
        in_specs=[pl.BlockSpec((LANES, depth, d), lambda j: (src_tile(j), 0, 0)),
                  pl.BlockSpec((depth, LANES, d), lambda j: (0, 0, 0))],
        out_specs=[pl.BlockSpec((LANES, d), lambda j: (j, 0))] * depth,
        out_shape=[jax.ShapeDtypeStruct((P_PAD, d), BF16)] * depth,
        compiler_params=_cparams(("parallel",)),
        name="pack_w_in",
    )(w_t, v1_t)


def _token_tiles(m):
    if m % 2048 == 0:
        return 2048, 512, 512
    return m, m, m


def kernel(x_prompt, x_sample, state_rw_shift, state_rw_wkv, state_ret, state_ml_c, state_ml_n, state_ml_m,
           ln0_g, ln0_b, w_in, rw_mu, rw_w0, rw_w2, rw_a0, rw_a2, rw_g2, rw_kk, rw_ka, rw_rk,
           rw_lnx_g, rw_lnx_b, rw_v0, rw_v1, rw_vmu, rw_v2, ml_ib, ml_fb, ml_norm, w_out,
           ln1_g, ln1_b, w_gate, w_up, w_down, ln2_g, ln2_b):
    bp, tp, d = x_prompt.shape
    bs, ts, _ = x_sample.shape
    mp, ms = bp * tp, bs * ts
    pos_p = jnp.arange(tp)
    pos_s = PAST_LEN + jnp.arange(ts)
    tm_big, tm_out, tm_down = _token_tiles(mp)
    rw_wkv_t = state_rw_wkv.transpose(0, 2, 3, 4, 1)
    w_in_packed = _pack_w_in(w_in, rw_v1)
    xf_p, xb_p = _layernorm(x_prompt.reshape(mp, d), ln0_g, ln0_b, tm_out)
    xf_s, xb_s = _layernorm(x_sample.reshape(ms, d), ln0_g, ln0_b, ms)
    vf_p = vf_s = None
    outs_p, outs_s = [], []

    for l in range(DEPTH):
        lp = {
            'rw_mu': rw_mu[l], 'rw_w0': rw_w0[l], 'rw_w2': rw_w2[l], 'rw_a0': rw_a0[l], 'rw_a2': rw_a2[l],
            'rw_g2': rw_g2[l], 'rw_kk': rw_kk[l], 'rw_ka': rw_ka[l], 'rw_rk': rw_rk[l],
            'rw_lnx_g': rw_lnx_g[l], 'rw_lnx_b': rw_lnx_b[l], 'ml_ib': ml_ib[l], 'ml_fb': ml_fb[l],
            'ml_norm': ml_norm[l],
        }
        if l > 0:
            lp.update(rw_v0=rw_v0[l - 1], rw_vmu=rw_vmu[l - 1], rw_v2=rw_v2[l - 1])
        w_out_b = w_out[l].astype(BF16)
        w_down_b = w_down[l].astype(BF16)
        x_side = jnp.concatenate([xb_s, state_rw_shift[l].astype(BF16)], axis=0)
        p_p, p_side = _in_proj(xb_p, x_side, w_in_packed[l], tm_big, P_TN)
        o_p, vf_p, st_p = _mix_prompt(p_p.reshape(bp, tp, P_PAD), pos_p, vf_p, lp)
        st = {'rw_wkv_t': rw_wkv_t, 'layer': l, 'ret_all': state_ret, 'ml_c_all': state_ml_c,
              'ml_n': state_ml_n[l], 'ml_m': state_ml_m[l]}
        o_s, vf_s, st_s = _mix_sample(p_side[:ms].reshape(bs, ts, P_PAD), pos_s, vf_s, st, lp, p_side[ms:])
        outs_p.append((xf_p.reshape(bp, tp, d)[:, -1],) + st_p)
        outs_s.append((xf_s.reshape(bs, ts, d)[:, -1],) + st_s)
        flat = lambda o, m: tuple(u.reshape(m, u.shape[-1]) for u in o)
        x1f_p, x1b_p, x1f_s, x1b_s = _out_proj_ln(flat(o_p, mp), xf_p, flat(o_s, ms), xf_s, w_out_b,
                                                  ln1_g[l], ln1_b[l], tm_out)
        hdn_p, hdn_s = _matmul_swiglu(x1b_p, x1b_s, w_gate, w_up, l, tm_big, 512)
        xf_p, xb_p, xf_s, xb_s = _matmul_res_ln(hdn_p, hdn_s, w_down_b, x1f_p, x1f_s, ln2_g[l], ln2_b[l],
                                                tm_down, DOWN_TK)

    y_p = xf_p.reshape(bp, tp, d)
    y_s = xf_s.reshape(bs, ts, d)
    sp = [jnp.stack([o[i] for o in outs_p]) for i in range(6)]
    ss = [jnp.stack([o[i] for o in outs_s]) for i in range(6)]
    ss[1] = ss[1].transpose(0, 4, 1, 2, 3)
    return (y_p, y_s, sp[0], sp[1], sp[2], sp[3], sp[4], sp[5], ss[0], ss[1], ss[2], ss[3], ss[4], ss[5])
```

```python
import functools
import math

import numpy as np
import jax
import jax.numpy as jnp
from jax import lax
from jax.experimental import pallas as pl
from jax.experimental.pallas import tpu as pltpu

F32 = jnp.float32
BF16 = jnp.bfloat16

D_MODEL = 2048
DEPTH = 2
PAST_LEN = 16384
RW_HD = 64
RW_W = D_MODEL // 4
RW_HEADS = RW_W // RW_HD
RW_LORA_W = 64
RW_LORA_A = 64
RW_LORA_V = 32
RW_LORA_G = 128
RW_P = 3 * RW_W + RW_LORA_W + RW_LORA_A + RW_LORA_G
RW_GN_EPS = 64e-5
RET_V = 128
RET_QK = 64
RET_W = 3 * D_MODEL // 8
RET_HEADS = RET_W // RET_V
RET_P = 2 * RET_HEADS * RET_QK + 2 * RET_W
ML_V = 128
ML_QK = 64
ML_W = D_MODEL - RW_W - RET_W
ML_HEADS = ML_W // ML_V
ML_P = 2 * ML_HEADS * ML_QK + 2 * ML_W + 2 * ML_HEADS
ML_GATE_CAP = 15.0
P_TOTAL = RW_P + RET_P + ML_P
D_FF = ((8 * D_MODEL + 3 * 256 - 1) // (3 * 256)) * 256
CHUNK = 128
ROPE_BASE = 10000.0
LN_EPS = 1e-5
NORM_EPS = 1e-6
ALPHA = (2 * DEPTH) ** 0.25

LANES = 128
P_V1 = RW_P
P_GATE = RW_P + LANES
P_RET = 2304
P_ML = 2 * P_RET
P_MAIN = 2304
P_PAD = 3 * P_RET
P_TN = 768
RW_CHUNK = 64
DEC_TB = 8
VMEM_LIMIT = 56 * 1024 * 1024

def _cparams(sem):
    return pltpu.CompilerParams(dimension_semantics=sem, vmem_limit_bytes=VMEM_LIMIT)


def _dot(a, b):
    return lax.dot_general(a, b, (((1,), (0,)), ((), ())), preferred_element_type=F32)


def _dot_nt(a, b):
    return lax.dot_general(a, b, (((1,), (1,)), ((), ())), preferred_element_type=F32)


def _dot_tn(a, b):
    return lax.dot_general(a, b, (((0,), (0,)), ((), ())), preferred_element_type=F32)


def _ln_rows(x, g, b):
    mu = jnp.mean(x, -1, keepdims=True)
    xc = x - mu
    var = jnp.mean(xc * xc, -1, keepdims=True)
    return xc * lax.rsqrt(var + LN_EPS) * g + b


def _ln_body(x_ref, g_ref, b_ref, of_ref, ob_ref):
    y = _ln_rows(x_ref[...], g_ref[...], b_ref[...])
    of_ref[...] = y
    ob_ref[...] = y.astype(BF16)


def _layernorm(x, g, b, tm):
    m, d = x.shape
    return pl.pallas_call(
        _ln_body,
        grid=(m // tm,),
        in_specs=[pl.BlockSpec((tm, d), lambda i: (i, 0)),
                  pl.BlockSpec((1, d), lambda i: (0, 0)),
                  pl.BlockSpec((1, d), lambda i: (0, 0))],
        out_specs=[pl.BlockSpec((tm, d), lambda i: (i, 0)),
                   pl.BlockSpec((tm, d), lambda i: (i, 0))],
        out_shape=[jax.ShapeDtypeStruct((m, d), F32), jax.ShapeDtypeStruct((m, d), BF16)],
        compiler_params=_cparams(("parallel",)),
        name="layernorm",
    )(x, g.reshape(1, d), b.reshape(1, d))


def _mm_body(x_ref, w_ref, o_ref, *, w_transposed):
    dot = _dot_nt if w_transposed else _dot
    o_ref[...] = dot(x_ref[...], w_ref[...]).astype(o_ref.dtype)


def _matmul(x, w, tm, tn, out_dtype=F32, w_transposed=False):
    m, k = x.shape
    n = w.shape[0] if w_transposed else w.shape[1]
    w_spec = (pl.BlockSpec((tn, k), lambda i, j: (j, 0)) if w_transposed
              else pl.BlockSpec((k, tn), lambda i, j: (0, j)))
    return pl.pallas_call(
        functools.partial(_mm_body, w_transposed=w_transposed),
        grid=(m // tm, n // tn),
        in_specs=[pl.BlockSpec((tm, k), lambda i, j: (i, 0)), w_spec],
        out_specs=pl.BlockSpec((tm, tn), lambda i, j: (i, j)),
        out_shape=jax.ShapeDtypeStruct((m, n), out_dtype),
        compiler_params=_cparams(("parallel", "parallel")),
        name="matmul",
    )(x, w)


def _in_proj_body(x_ref, xs_ref, w_ref, o_ref, os_ref):
    @pl.when(pl.program_id(1) == 0)
    def _():
        os_ref[...] = _dot_nt(xs_ref[...], w_ref[...])

    o_ref[...] = _dot_nt(x_ref[...], w_ref[...])


def _in_proj(x, x_side, w_t, tm, tn):
    m, k = x.shape
    ms = x_side.shape[0]
    n = w_t.shape[0]
    return pl.pallas_call(
        _in_proj_body,
        grid=(n // tn, m // tm),
        in_specs=[pl.BlockSpec((tm, k), lambda j, i: (i, 0)), pl.BlockSpec((ms, k), lambda j, i: (0, 0)),
                  pl.BlockSpec((tn, k), lambda j, i: (j, 0))],
        out_specs=[pl.BlockSpec((tm, tn), lambda j, i: (i, j)), pl.BlockSpec((ms, tn), lambda j, i: (0, j))],
        out_shape=[jax.ShapeDtypeStruct((m, n), F32), jax.ShapeDtypeStruct((ms, n), F32)],
        compiler_params=_cparams(("parallel", "arbitrary")),
        name="in_proj",
    )(x, x_side, w_t)


def _swiglu_body(x_ref, xs_ref, wg_ref, wu_ref, o_ref, os_ref, wg_scr, wu_scr):
    def act(x):
        g = _dot(x, wg_scr[...])
        return (g * jax.nn.sigmoid(g) * _dot(x, wu_scr[...])).astype(BF16)

    @pl.when(pl.program_id(1) == 0)
    def _():
        wg_scr[...] = wg_ref[...].astype(BF16)
        wu_scr[...] = wu_ref[...].astype(BF16)
        os_ref[...] = act(xs_ref[...])

    step = x_ref.shape[0] // SWIGLU_CHUNKS
    for r in range(SWIGLU_CHUNKS):
        rows = pl.ds(r * step, step)
        o_ref[rows, :] = act(x_ref[rows, :])


def _matmul_swiglu(x, x_side, wg, wu, layer, tm, tn):
    m, k = x.shape
    ms = x_side.shape[0]
    n = wg.shape[2]
    w_spec = pl.BlockSpec((None, k, tn), lambda j, i: (layer, 0, j))
    return pl.pallas_call(
        _swiglu_body,
        grid=(n // tn, m // tm),
        in_specs=[pl.BlockSpec((tm, k), lambda j, i: (i, 0)), pl.BlockSpec((ms, k), lambda j, i: (0, 0)),
                  w_spec, w_spec],
        out_specs=[pl.BlockSpec((tm, tn), lambda j, i: (i, j)), pl.BlockSpec((ms, tn), lambda j, i: (0, j))],
        out_shape=[jax.ShapeDtypeStruct((m, n), BF16), jax.ShapeDtypeStruct((ms, n), BF16)],
        scratch_shapes=[pltpu.VMEM((k, tn), BF16), pltpu.VMEM((k, tn), BF16)],
        compiler_params=_cparams(("parallel", "arbitrary")),
        name="matmul_swiglu",
    )(x, x_side, wg, wu)


SWIGLU_CHUNKS = 2
DOWN_TK = D_FF // 4
LN_ROWS = 256
OUT_ROWS = 256


def _res_ln_store(acc_ref, res_ref, g_ref, b_ref, of_ref, ob_ref, n_rows):
    step = min(LN_ROWS, n_rows)
    for r in range(0, n_rows, step):
        rows = pl.ds(r, step)
        y = _ln_rows(ALPHA * res_ref[rows, :] + acc_ref[rows, :], g_ref[...], b_ref[...])
        of_ref[rows, :] = y
        ob_ref[rows, :] = y.astype(BF16)


def _mm_res_ln_body(x_ref, xs_ref, w_ref, res_ref, ress_ref, g_ref, b_ref, of_ref, ob_ref, ofs_ref, obs_ref, *, nk, tm, ms):
    i = pl.program_id(0)
    kk = pl.program_id(1)

    @pl.when(kk == 0)
    def _():
        of_ref[...] = jnp.zeros_like(of_ref)

    of_ref[...] += _dot(x_ref[...], w_ref[...])

    @pl.when(kk == nk - 1)
    def _():
        _res_ln_store(of_ref, res_ref, g_ref, b_ref, of_ref, ob_ref, tm)

    @pl.when(i == 0)
    def _():
        @pl.when(kk == 0)
        def _():
            ofs_ref[...] = jnp.zeros_like(ofs_ref)

        ofs_ref[...] += _dot(xs_ref[...], w_ref[...])

        @pl.when(kk == nk - 1)
        def _():
            _res_ln_store(ofs_ref, ress_ref, g_ref, b_ref, ofs_ref, obs_ref, ms)


def _matmul_res_ln(x, x_side, w, res, res_side, g, b, tm, tk):
    m, k = x.shape
    ms = x_side.shape[0]
    n = w.shape[1]
    nk = k // tk
    const = lambda r: pl.BlockSpec((r, n), lambda i, j: (0, 0))
    main = pl.BlockSpec((tm, n), lambda i, j: (i, 0))
    return pl.pallas_call(
        functools.partial(_mm_res_ln_body, nk=nk, tm=tm, ms=ms),
        grid=(m // tm, nk),
        in_specs=[pl.BlockSpec((tm, tk), lambda i, j: (i, j)), pl.BlockSpec((ms, tk), lambda i, j: (0, j)),
                  pl.BlockSpec((tk, n), lambda i, j: (j, 0)), main, const(ms), const(1), const(1)],
        out_specs=[main, main, const(ms), const(ms)],
        out_shape=[jax.ShapeDtypeStruct((m, n), F32), jax.ShapeDtypeStruct((m, n), BF16),
                   jax.ShapeDtypeStruct((ms, n), F32), jax.ShapeDtypeStruct((ms, n), BF16)],
        compiler_params=_cparams(("arbitrary", "arbitrary")),
        name="matmul_res_ln",
    )(x, x_side, w, res, res_side, g.reshape(1, n), b.reshape(1, n))


def _out_proj_ln_body(o_rw_ref, o_ret_ref, o_ml_ref, res_ref, s_rw_ref, s_ret_ref, s_ml_ref, ress_ref,
                      w_ref, g_ref, b_ref, of_ref, ob_ref, ofs_ref, obs_ref, *, tm, ms):
    def project(rw_ref, ret_ref, ml_ref, r_ref, f_ref, h_ref, n_rows):
        step = min(OUT_ROWS, n_rows)
        for r in range(0, n_rows, step):
            rows = pl.ds(r, step)
            mix = (_dot(rw_ref[rows, :], w_ref[0:RW_W, :])
                   + _dot(ret_ref[rows, :], w_ref[RW_W:RW_W + RET_W, :])
                   + _dot(ml_ref[rows, :], w_ref[RW_W + RET_W:, :]))
            y = _ln_rows(ALPHA * r_ref[rows, :] + mix, g_ref[...], b_ref[...])
            f_ref[rows, :] = y
            h_ref[rows, :] = y.astype(BF16)

    @pl.when(pl.program_id(0) == 0)
    def _():
        project(s_rw_ref, s_ret_ref, s_ml_ref, ress_ref, ofs_ref, obs_ref, ms)

    project(o_rw_ref, o_ret_ref, o_ml_ref, res_ref, of_ref, ob_ref, tm)


def _out_proj_ln(o, res, o_side, res_side, w, g, b, tm):
    m = res.shape[0]
    ms = res_side.shape[0]
    n = w.shape[1]
    rows = lambda width: pl.BlockSpec((tm, width), lambda i: (i, 0))
    const = lambda r, c: pl.BlockSpec((r, c), lambda i: (0, 0))
    widths = (RW_W, RET_W, ML_W)
    return pl.pallas_call(
        functools.partial(_out_proj_ln_body, tm=tm, ms=ms),
        grid=(m // tm,),
        in_specs=([rows(c) for c in widths] + [rows(n)] + [const(ms, c) for c in widths] + [const(ms, n)]
                  + [const(D_MODEL, n), const(1, n), const(1, n)]),
        out_specs=[rows(n), rows(n), const(ms, n), const(ms, n)],
        out_shape=[jax.ShapeDtypeStruct((m, n), F32), jax.ShapeDtypeStruct((m, n), BF16),
                   jax.ShapeDtypeStruct((ms, n), F32), jax.ShapeDtypeStruct((ms, n), BF16)],
        compiler_params=_cparams(("arbitrary",)),
        name="out_proj_ln",
    )(*o, res, *o_side, res_side, w, g.reshape(1, n), b.reshape(1, n))


RW_TB = 512
RW_GH = 4
RW_GW = RW_GH * RW_HD
RW_AHEAD = 8
RW_VEC_ROWS = 8


def _split3(x):
    hi = x.astype(BF16)
    r1 = x - hi.astype(F32)
    mid = r1.astype(BF16)
    lo = (r1 - mid.astype(F32)).astype(BF16)
    return hi, mid, lo


def _mm(a, b, dims):
    return lax.dot_general(a.astype(BF16), b.astype(BF16), (dims, ((), ())), preferred_element_type=F32)


_NN = ((1,), (0,))
_NT = ((1,), (1,))
_TN = ((0,), (0,))


def _exact_lhs_dot(a_bf16, b):
    hi, mid, lo = _split3(b)
    dg = lambda y: lax.dot_general(a_bf16, y, (_NN, ((), ())), preferred_element_type=F32)
    return dg(hi) + (dg(mid) + dg(lo))


def _exact_rhs_dot(a, b_bf16):
    hi, mid, lo = _split3(a)
    dg = lambda x: lax.dot_general(x, b_bf16, (_NN, ((), ())), preferred_element_type=F32)
    return dg(hi) + (dg(mid) + dg(lo))


def _seg_sum(a, seg_bf16):
    hi = a.astype(BF16)
    lo = (a - hi.astype(F32)).astype(BF16)
    gw = seg_bf16.shape[0]
    dg = lambda x: lax.dot_general(x, seg_bf16, (_NN, ((), ())), preferred_element_type=F32)
    groups = [dg(hi[:, c:c + gw]) + dg(lo[:, c:c + gw]) for c in range(0, a.shape[1], gw)]
    return groups[0] if len(groups) == 1 else jnp.concatenate(groups, axis=1)


def _rw_scan(r, lw, k, v, kk, a, st_scr, y_scr):
    L, TB, G = RW_CHUNK, RW_TB, RW_GW
    row = lax.broadcasted_iota(jnp.int32, (L, G), 0)
    col = lax.broadcasted_iota(jnp.int32, (L, G), 1) & (L - 1)
    strict, lower, eye = row > col, row >= col, (row == col).astype(F32)
    rg = lax.broadcasted_iota(jnp.int32, (G, G), 0) // RW_HD
    cg = lax.broadcasted_iota(jnp.int32, (G, G), 1) // RW_HD
    mask_bd = rg == cg
    rt = lax.broadcasted_iota(jnp.int32, (2 * L, 2 * L), 0)
    ct = lax.broadcasted_iota(jnp.int32, (2 * L, 2 * L), 1)
    tri = jnp.logical_and(rt >= ct, rt // L == ct // L).astype(BF16)
    bd = lambda x: jnp.where(mask_bd, jnp.concatenate([x.astype(BF16)] * RW_GH, axis=0), 0.0)
    cut = lambda x, b: x[b[0] * L:(b[0] + 1) * L, b[1] * G:(b[1] + 1) * G]

    cum = jnp.concatenate([_exact_lhs_dot(tri, lw[i:i + 2 * L, :]) for i in range(0, TB, 2 * L)], axis=0)
    e_neg = jnp.exp(-cum)
    ap = kk * a
    ap_h = ap * e_neg
    k_h = k * e_neg
    kk_t = kk * jnp.exp(cum - lw)
    r_t = r * jnp.exp(cum)

    lhs, n_m, m_a, m_kr, inv, mv, vk, a_end, decay = ({} for _ in range(9))
    n_groups = RW_HEADS // RW_GH

    def prep(chunks):
        blk = [(s, g) for s in chunks for g in range(n_groups)]
        for b in blk:
            lhs[b] = jnp.concatenate([cut(kk_t, b), cut(r_t, b)], axis=0)
            sc_a = _mm(lhs[b], bd(cut(ap_h, b)), _NT)
            sc_k = _mm(lhs[b], bd(cut(k_h, b)), _NT)
            n_m[b] = jnp.where(strict, sc_a[:L], 0.0)
            m_a[b] = jnp.where(lower, sc_a[L:], 0.0)
            m_kr[b] = jnp.concatenate([jnp.where(strict, sc_k[:L], 0.0), jnp.where(lower, sc_k[L:], 0.0)], axis=0)
        pw = {}
        for b in blk:
            inv[b] = eye - n_m[b]
            pw[b] = _mm(n_m[b], bd(n_m[b]), _NN)
        n_iter = int(math.log2(L)) - 1
        for j in range(n_iter):
            last = j == n_iter - 1
            for b in blk:
                lhs_j = inv[b] if last else jnp.concatenate([inv[b], pw[b]], axis=0)
                prod = _mm(lhs_j, bd(pw[b]), _NN)
                inv[b] = inv[b] + prod[:L]
                if not last:
                    pw[b] = prod[L:]
        for b in blk:
            s, g = b
            tot = cum[(s + 1) * L - 1:(s + 1) * L, g * G:(g + 1) * G]
            e_end = jnp.exp(tot - cut(cum, b))
            mv[b] = _mm(m_kr[b], bd(cut(v, b)), _NN)
            vk[b] = _mm(cut(v, b), cut(k, b) * e_end, _TN)
            a_end[b] = cut(ap, b) * e_end
            decay[b] = jnp.exp(tot)

    def apply(s):
        for g in range(n_groups):
            b = (s, g)
            st = st_scr[g]
            s_terms = _mm(lhs[b], st, _NT)
            u = _mm(inv[b], bd(s_terms[:L] + mv[b][:L]), _NN)
            y_scr[s * L:(s + 1) * L, g * G:(g + 1) * G] = s_terms[L:] + mv[b][L:] - _mm(m_a[b], bd(u), _NN)
            st_scr[g] = jnp.where(mask_bd, st * decay[b] + vk[b] - _mm(u, a_end[b], _TN), 0.0)

    n_chunks = TB // L
    prep(range(min(RW_AHEAD, n_chunks)))
    for s in range(n_chunks):
        apply(s)
        if s + RW_AHEAD < n_chunks:
            prep([s + RW_AHEAD])


def _softplus(z):
    return jnp.maximum(z, 0.0) + jnp.log(1.0 + jnp.exp(-jnp.abs(z)))


def _rwkv_fused_body(*refs, nc, has_vres):
    if has_vres:
        (p_ref, pv_ref, vf_ref, mu_ref, vec_ref, wa_ref, g2_ref, seg_ref, vmu_ref, v2_ref,
         o_ref, sf_ref, st_scr, prev_scr, y_scr, prevv_scr) = refs
    else:
        (p_ref, mu_ref, vec_ref, wa_ref, g2_ref, seg_ref,
         o_ref, vfo_ref, sf_ref, st_scr, prev_scr, y_scr) = refs
    TB, W, L = RW_TB, RW_W, RW_CHUNK
    c = pl.program_id(1)

    @pl.when(c == 0)
    def _():
        st_scr[...] = jnp.zeros_like(st_scr)
        prev_scr[...] = jnp.zeros_like(prev_scr)
        if has_vres:
            prevv_scr[...] = jnp.zeros_like(prevv_scr)

    first_row = lax.broadcasted_iota(jnp.int32, (TB, 1), 0) == 0

    def shift_mix(x, carry_ref, mu):
        prev = jnp.where(first_row, carry_ref[...], pltpu.roll(x, 1, 0))
        carry_ref[...] = x[TB - 1:TB, :]
        return x + (prev - x) * mu

    mixed = shift_mix(p_ref[0], prev_scr, mu_ref[...])
    r = mixed[:, 0:W]
    k = mixed[:, W:2 * W]
    v = mixed[:, 2 * W:3 * W]
    xwa = mixed[:, 3 * W:3 * W + LANES]
    xg = mixed[:, 3 * W + LANES:3 * W + 2 * LANES]
    vec = vec_ref[...]
    w0, a0, kk_s, ka, rk, lnx_g, lnx_b, v0 = (vec[i:i + 1, :] for i in range(RW_VEC_ROWS))
    seg = seg_ref[...]
    wa = wa_ref[...]
    w_lora = _dot(jnp.tanh(xwa).astype(BF16), wa[:, 0:W])
    a_lora = _dot(xwa.astype(BF16), wa[:, W:2 * W])
    lw = -jnp.exp(-_softplus(-(w0 + w_lora)) - 0.5)
    a = jax.nn.sigmoid(a0 + a_lora)
    g = _dot(jax.nn.sigmoid(xg).astype(BF16), g2_ref[...])
    if has_vres:
        xv = shift_mix(pv_ref[0], prevv_scr, vmu_ref[...])
        v = v + (vf_ref[0] - v) * jax.nn.sigmoid(v0 + _dot(xv.astype(BF16), v2_ref[...]))
    else:
        vfo_ref[0] = v
    kk = k * kk_s
    kk = kk * lax.rsqrt(jnp.maximum(_seg_sum(kk * kk, seg), 1e-24))
    k = k * (1.0 + (a - 1.0) * ka)

    _rw_scan(r, lw, k, v, kk, a, st_scr, y_scr)

    y = y_scr[...]
    inv_n = 1.0 / RW_HD
    y_mu = _seg_sum(y, seg) * inv_n
    yc = y - y_mu
    y_var = _seg_sum(yc * yc, seg) * inv_n
    y = yc * lax.rsqrt(y_var + RW_GN_EPS) * lnx_g + lnx_b
    bonus = _seg_sum(r * k * rk, seg) * v
    o_ref[0] = ((y + bonus) * g).astype(BF16)

    @pl.when(c == nc - 1)
    def _():
        sf_ref[0] = st_scr[...]


def _rwkv_prompt(p3, lp, v_first):
    b, t, _ = p3.shape
    nc = t // RW_TB
    has_vres = v_first is not None
    ng = RW_HEADS // RW_GH
    zpad = jnp.zeros((RW_LORA_W, RW_W), F32)
    wa = jnp.concatenate([jnp.concatenate([lp['rw_w2'], zpad], 0), jnp.concatenate([zpad, lp['rw_a2']], 0)], 1)
    vec = jnp.stack([lp['rw_w0'], lp['rw_a0'], lp['rw_kk'], lp['rw_ka'], lp['rw_rk'], lp['rw_lnx_g'], lp['rw_lnx_b'],
                     lp['rw_v0'] if has_vres else jnp.zeros((RW_W,), F32)])
    hid = jnp.arange(RW_GW) // RW_HD
    seg = (hid[:, None] == hid[None, :]).astype(BF16)
    full = lambda shape: pl.BlockSpec(shape, lambda i, j: (0,) * len(shape))
    seq = lambda w, blk: pl.BlockSpec((1, RW_TB, w), lambda i, j: (i, j, blk))
    in_specs = [seq(RW_P, 0)]
    args = [p3]
    if has_vres:
        in_specs += [seq(LANES, P_V1 // LANES), seq(RW_W, 0)]
        args += [p3, v_first]
    in_specs += [full((1, RW_P)), full((RW_VEC_ROWS, RW_W)), full((LANES, 2 * RW_W)), full((RW_LORA_G, RW_W)),
                 full((RW_GW, RW_GW))]
    args += [lp['rw_mu'].reshape(1, RW_P), vec, wa.astype(BF16), lp['rw_g2'].astype(BF16), seg]
    if has_vres:
        in_specs += [full((1, LANES)), full((LANES, RW_W))]
        args += [jnp.pad(lp['rw_vmu'], (0, LANES - RW_LORA_V)).reshape(1, LANES),
                 jnp.pad(lp['rw_v2'], ((0, LANES - RW_LORA_V), (0, 0))).astype(BF16)]
    out_specs = [seq(RW_W, 0)]
    out_shape = [jax.ShapeDtypeStruct((b, t, RW_W), BF16)]
    if not has_vres:
        out_specs.append(seq(RW_W, 0))
        out_shape.append(jax.ShapeDtypeStruct((b, t, RW_W), F32))
    out_specs.append(pl.BlockSpec((1, ng, RW_GW, RW_GW), lambda i, j: (i, 0, 0, 0)))
    out_shape.append(jax.ShapeDtypeStruct((b, ng, RW_GW, RW_GW), F32))
    scratch = [pltpu.VMEM((ng, RW_GW, RW_GW), F32), pltpu.VMEM((1, RW_P), F32), pltpu.VMEM((RW_TB, RW_W), F32)]
    if has_vres:
        scratch.append(pltpu.VMEM((1, LANES), F32))
    outs = pl.pallas_call(
        functools.partial(_rwkv_fused_body, nc=nc, has_vres=has_vres),
        grid=(b, nc),
        in_specs=in_specs,
        out_specs=out_specs,
        out_shape=out_shape,
        scratch_shapes=scratch,
        compiler_params=_cparams(("parallel", "arbitrary")),
        name="rwkv_fused",
    )(*args)
    if has_vres:
        o, st_bd = outs
    else:
        o, v_first, st_bd = outs
    st5 = st_bd.reshape(b, ng, RW_GH, RW_HD, RW_GH, RW_HD)
    s_fin = jnp.stack([st5[:, :, h, :, h, :] for h in range(RW_GH)], axis=2)
    s_fin = s_fin.reshape(b, RW_HEADS, RW_HD, RW_HD).transpose(0, 1, 3, 2)
    return o, v_first, s_fin


def _ret_log_gamma(h):
    return math.log1p(-(2.0 ** (-5.0 - h)))


def _rotary_tables(pos, heads, dk):
    half = dk // 2
    inv = ROPE_BASE ** (-jnp.arange(half, dtype=F32) / half)
    ang = pos.astype(F32)[:, None] * inv[None, :]
    cos = jnp.tile(jnp.concatenate([jnp.cos(ang), jnp.cos(ang)], -1), (1, heads))
    sin = jnp.tile(jnp.concatenate([-jnp.sin(ang), jnp.sin(ang)], -1), (1, heads))
    lane = jnp.arange(LANES)
    perm = (lane[:, None] == (lane[None, :] ^ half)).astype(BF16)
    return cos, sin, perm


def _ret_fused_body(p_ref, cos_ref, sin_ref, perm_ref, o_ref, sf_ref, s_scr, *, nc):
    L, DK, DV, H = CHUNK, RET_QK, RET_V, RET_HEADS
    nq = H * DK
    c = pl.program_id(1)

    @pl.when(c == 0)
    def _():
        s_scr[...] = jnp.zeros_like(s_scr)

    cos = cos_ref[...]
    sin = sin_ref[...]
    perm = perm_ref[...]
    swap = lambda x: jnp.concatenate(
        [_exact_rhs_dot(x[:, c:c + LANES], perm) for c in range(0, nq, LANES)], axis=1)
    rot = lambda x: x * cos + swap(x) * sin
    q_all = rot(p_ref[0, :, 0:nq])
    k_all = rot(p_ref[0, :, nq:2 * nq]) * (DK ** -0.5)
    row = lax.broadcasted_iota(jnp.int32, (L, L), 0)
    col = lax.broadcasted_iota(jnp.int32, (L, L), 1)
    rel = (row - col).astype(F32)
    idx = lax.broadcasted_iota(jnp.int32, (L, 1), 0).astype(F32)
    vs = lambda h: p_ref[0, :, 2 * nq + h * DV:2 * nq + (h + 1) * DV].astype(BF16)
    hd = [dict() for _ in range(H)]
    for h, t in enumerate(hd):
        lg = _ret_log_gamma(h)
        q = q_all[:, h * DK:(h + 1) * DK]
        k = k_all[:, h * DK:(h + 1) * DK]
        t['s_prev'] = s_scr[h]
        t['dmask'] = jnp.where(rel >= 0, jnp.exp(jnp.maximum(rel, 0.0) * lg), 0.0)
        t['qk'] = _dot_nt(q.astype(BF16), k.astype(BF16))
        q_dec = q * jnp.exp((idx + 1.0) * lg)
        t['qs'] = _dot(q_dec.astype(BF16), t['s_prev'].astype(BF16))
        k_end = k * jnp.exp((L - 1.0 - idx) * lg)
        t['kv'] = _dot_tn(k_end.astype(BF16), vs(h))
    for h, t in enumerate(hd):
        y = _dot((t['qk'] * t['dmask']).astype(BF16), vs(h)) + t['qs']
        y = y * lax.rsqrt(jnp.mean(y * y, -1, keepdims=True) + NORM_EPS)
        gate = p_ref[0, :, 2 * nq + RET_W + h * DV:2 * nq + RET_W + (h + 1) * DV]
        o_ref[0, :, h * DV:(h + 1) * DV] = (gate * jax.nn.sigmoid(gate) * y).astype(BF16)
    for h, t in enumerate(hd):
        s_scr[h] = math.exp(L * _ret_log_gamma(h)) * t['s_prev'] + t['kv']

    @pl.when(c == nc - 1)
    def _():
        sf_ref[0] = s_scr[...]


def _ret_prompt(p3, pos):
    b, t, _ = p3.shape
    L = CHUNK
    nc = t // L
    nq = RET_HEADS * RET_QK
    cos, sin, perm = _rotary_tables(pos, RET_HEADS, RET_QK)
    tab = pl.BlockSpec((L, nq), lambda i, j: (j, 0))
    st = pl.BlockSpec((1, RET_HEADS, RET_QK, RET_V), lambda i, j: (i, 0, 0, 0))
    return pl.pallas_call(
        functools.partial(_ret_fused_body, nc=nc),
        grid=(b, nc),
        in_specs=[pl.BlockSpec((1, L, P_MAIN), lambda i, j: (i, j, P_RET // P_MAIN)), tab, tab,
                  pl.BlockSpec((LANES, LANES), lambda i, j: (0, 0))],
        out_specs=[pl.BlockSpec((1, L, RET_W), lambda i, j: (i, j, 0)), st],
        out_shape=[jax.ShapeDtypeStruct((b, t, RET_W), BF16),
                   jax.ShapeDtypeStruct((b, RET_HEADS, RET_QK, RET_V), F32)],
        scratch_shapes=[pltpu.VMEM((RET_HEADS, RET_QK, RET_V), F32)],
        compiler_params=_cparams(("parallel", "arbitrary")),
        name="ret_fused",
    )(p3, cos, sin, perm)


ML_HPAD = 8


def _ml_fused_body(p_ref, gate_ref, bias_ref, norm_ref, sel_ref, o_ref, cf_ref, nf_ref, mf_ref,
                   c_scr, n_scr, m_scr, *, nc):
    L, DK, DV, H = CHUNK, ML_QK, ML_V, ML_HEADS
    nq = H * DK
    ci = pl.program_id(1)

    @pl.when(ci == 0)
    def _():
        c_scr[...] = jnp.zeros_like(c_scr)
        n_scr[...] = jnp.zeros_like(n_scr)
        m_scr[...] = jnp.zeros_like(m_scr)

    row = lax.broadcasted_iota(jnp.int32, (L, L), 0)
    col = lax.broadcasted_iota(jnp.int32, (L, L), 1)
    causal = row >= col
    tri = causal.astype(BF16)
    capped = ML_GATE_CAP * jnp.tanh((gate_ref[0] + bias_ref[...]) * (1.0 / ML_GATE_CAP))
    lane = lax.broadcasted_iota(jnp.int32, (L, LANES), 1)
    g = jnp.where(lane < H, capped, jnp.where(lane < 2 * H, -_softplus(-capped), 0.0))
    g_rep = _exact_rhs_dot(g, sel_ref[...])
    b_rep_all = _exact_lhs_dot(tri, g_rep[:, H * LANES:])
    g_t = g.T
    cum_t = _exact_lhs_dot(tri, g).T
    cm_all = g_rep[:, :H * LANES] - b_rep_all
    row_id = lax.broadcasted_iota(jnp.int32, (L, 1), 0)
    shift = 1
    while shift < L:
        cm_all = jnp.maximum(cm_all, jnp.where(row_id >= shift, pltpu.roll(cm_all, shift, 0), -jnp.inf))
        shift *= 2
    ones = jnp.ones((L, LANES), BF16)
    mean_w = jnp.full((DV, LANES), 1.0 / DV, BF16)
    m_all = m_scr[...]
    hd = [dict() for _ in range(H)]
    for h, t in enumerate(hd):
        hs = slice(h * LANES, (h + 1) * LANES)
        q = p_ref[0, :, h * DK:(h + 1) * DK].astype(BF16)
        k = p_ref[0, :, nq + h * DK:nq + (h + 1) * DK] * (DK ** -0.5)
        t['v1'] = jnp.concatenate([p_ref[0, :, 2 * nq + h * DV:2 * nq + (h + 1) * DV].astype(BF16), ones], axis=1)
        ig_rep = g_rep[:, hs]
        b_rep = b_rep_all[:, hs]
        ig_row = g_t[h:h + 1, :]
        b_row = cum_t[H + h:H + h + 1, :]
        b_tot = b_rep[L - 1:L, :]
        m_prev = m_all[h:h + 1, :]
        t['c_prev'] = c_scr[h]
        t['n_prev'] = n_scr[h]
        t['m_new'] = jnp.maximum(b_tot + m_prev, jnp.max(b_tot - b_rep + ig_rep, axis=0, keepdims=True))
        t['dec'] = jnp.exp(b_tot + m_prev - t['m_new'])
        kw = k * jnp.exp((b_tot - b_rep + ig_rep - t['m_new'])[:, :DK])
        t['kvn'] = _dot_tn(kw.astype(BF16), t['v1'])
        inter = b_rep + m_prev
        t['m_i'] = b_rep + jnp.maximum(cm_all[:, hs], m_prev)
        t['e'] = jnp.exp(jnp.where(causal, (b_rep - t['m_i']) - b_row + ig_row, -jnp.inf))
        t['sc'] = jnp.exp(inter - t['m_i'])
        t['qk'] = _dot_nt(q, k.astype(BF16))
        cn = jnp.concatenate([t['c_prev'], t['n_prev']], axis=1).astype(BF16)
        t['qcn'] = _dot(q, cn)
    for h, t in enumerate(hd):
        nd = _dot((t['qk'] * t['e']).astype(BF16), t['v1'])
        num = nd[:, :DV] + t['sc'] * t['qcn'][:, :DV]
        den = nd[:, DV:] + t['sc'] * t['qcn'][:, DV:]
        hid = num / jnp.maximum(jnp.abs(den), jnp.exp(-t['m_i']))
        hid = hid * lax.rsqrt(_seg_sum(hid * hid, mean_w) + NORM_EPS)
        og = p_ref[0, :, 2 * nq + ML_W + h * DV:2 * nq + ML_W + (h + 1) * DV]
        o_ref[0, :, h * DV:(h + 1) * DV] = (jax.nn.sigmoid(og) * (hid * norm_ref[:, h * DV:(h + 1) * DV])).astype(BF16)
    for h, t in enumerate(hd):
        c_scr[h] = t['dec'] * t['c_prev'] + t['kvn'][:, :DV]
        n_scr[h] = t['dec'] * t['n_prev'] + t['kvn'][:, DV:]
        m_scr[h:h + 1, :] = t['m_new']

    @pl.when(ci == nc - 1)
    def _():
        cf_ref[0] = c_scr[...]
        nf_ref[0] = n_scr[...]
        mf_ref[0] = m_scr[...]


def _ml_prompt(p3, lp):
    b, t, _ = p3.shape
    L = CHUNK
    nc = t // L
    bias = jnp.pad(jnp.concatenate([lp['ml_ib'], lp['ml_fb']]), (0, LANES - 2 * ML_HEADS)).reshape(1, LANES)
    n_rep = 2 * ML_HEADS * LANES
    sel = (jnp.arange(LANES)[:, None] == jnp.arange(n_rep)[None, :] // LANES).astype(BF16)
    vs = pl.BlockSpec((1, L, ML_W), lambda i, j: (i, j, 0))
    cs = pl.BlockSpec((1, ML_HEADS, ML_QK, ML_V), lambda i, j: (i, 0, 0, 0))
    ns = pl.BlockSpec((1, ML_HEADS, ML_QK, LANES), lambda i, j: (i, 0, 0, 0))
    ms = pl.BlockSpec((1, ML_HPAD, LANES), lambda i, j: (i, 0, 0))
    o, c_f, n_f, m_f = pl.pallas_call(
        functools.partial(_ml_fused_body, nc=nc),
        grid=(b, nc),
        in_specs=[pl.BlockSpec((1, L, P_MAIN), lambda i, j: (i, j, P_ML // P_MAIN)),
                  pl.BlockSpec((1, L, LANES), lambda i, j: (i, j, P_GATE // LANES)),
                  pl.BlockSpec((1, LANES), lambda i, j: (0, 0)),
                  pl.BlockSpec((1, ML_W), lambda i, j: (0, 0)),
                  pl.BlockSpec((LANES, n_rep), lambda i, j: (0, 0))],
        out_specs=[vs, cs, ns, ms],
        out_shape=[jax.ShapeDtypeStruct((b, t, ML_W), BF16),
                   jax.ShapeDtypeStruct((b, ML_HEADS, ML_QK, ML_V), F32),
                   jax.ShapeDtypeStruct((b, ML_HEADS, ML_QK, LANES), F32),
                   jax.ShapeDtypeStruct((b, ML_HPAD, LANES), F32)],
        scratch_shapes=[pltpu.VMEM((ML_HEADS, ML_QK, ML_V), F32),
                        pltpu.VMEM((ML_HEADS, ML_QK, LANES), F32),
                        pltpu.VMEM((ML_HPAD, LANES), F32)],
        compiler_params=_cparams(("parallel", "arbitrary")),
        name="ml_fused",
    )(p3, p3, bias, lp['ml_norm'].reshape(1, ML_W), sel)
    return o, c_f, n_f[..., 0], m_f[:, :ML_HEADS, 0]


def _to_cols(x):
    b, c = x.shape
    cols = x.reshape(b // DEC_TB, DEC_TB, c).transpose(0, 2, 1)
    return jnp.pad(cols, ((0, 0), (0, 0), (0, DEC_TB)))


def _col_selector():
    j = jnp.arange(2 * DEC_TB)[:, None]
    return (j == jnp.arange(DEC_TB * LANES)[None, :] // LANES).astype(BF16)


def _rwkv_step_body(w_ref, ap_ref, k_ref, kk_ref, r_ref, v_ref, s_ref, y_ref, so_ref):
    N = RW_HD
    v = v_ref[0]

    def sa_step(i, acc):
        return acc + kk_ref[0, pl.ds(i, 1), :] * s_ref[i]

    sa = lax.fori_loop(0, N, sa_step, jnp.zeros_like(v), unroll=8)

    def upd_step(i, y):
        row = lambda ref: ref[0, pl.ds(i, 1), :]
        s_new = row(w_ref) * s_ref[i] - row(ap_ref) * sa + row(k_ref) * v
        so_ref[i] = s_new
        return y + row(r_ref) * s_new

    y_ref[0] = lax.fori_loop(0, N, upd_step, jnp.zeros_like(v), unroll=8)


def _rwkv_step(wdec, ap, k, kk, r, v, s_all, layer):
    b, w = v.shape
    heads = lambda x: x.reshape(b, RW_HEADS, RW_HD).transpose(1, 2, 0)
    vec = pl.BlockSpec((1, RW_HD, b), lambda h: (h, 0, 0))
    y, s_new = pl.pallas_call(
        _rwkv_step_body,
        grid=(RW_HEADS,),
        in_specs=[vec] * 6 + [pl.BlockSpec((None, None, RW_HD, RW_HD, b), lambda h: (layer, h, 0, 0, 0))],
        out_specs=[vec, pl.BlockSpec((None, RW_HD, RW_HD, b), lambda h: (h, 0, 0, 0))],
        out_shape=[jax.ShapeDtypeStruct((RW_HEADS, RW_HD, b), F32),
                   jax.ShapeDtypeStruct((RW_HEADS, RW_HD, RW_HD, b), F32)],
        compiler_params=_cparams(("parallel",)),
        name="rwkv_step",
    )(heads(wdec), heads(ap), heads(k), heads(kk), heads(r), heads(v), s_all)
    return y.transpose(2, 0, 1).reshape(b, w), s_new


def _ret_step_body(kc_ref, q_ref, k_ref, v_ref, esel_ref, seg_ref, s_ref, y_ref, so_ref):
    DK, DV = RET_QK, RET_V
    q = q_ref[...]
    qk_v = _exact_rhs_dot(q * k_ref[...], seg_ref[...])
    row_id = lax.broadcasted_iota(jnp.int32, (DEC_TB, 1), 0)
    for h in range(RET_HEADS):
        gamma = math.exp(_ret_log_gamma(h))
        ks = slice(h * DK, (h + 1) * DK)
        vs = slice(h * DV, (h + 1) * DV)
        k_rep = _exact_rhs_dot(kc_ref[0, ks, :], esel_ref[...])
        q_h = q[:, ks].astype(BF16)
        v_h = v_ref[:, vs]
        qs = jnp.zeros((DEC_TB, DV), F32)
        for j in range(DEC_TB):
            s = s_ref[j, h]
            so_ref[j, h] = gamma * s + k_rep[:, j * LANES:(j + 1) * LANES] * v_h[j:j + 1, :]
            qs = jnp.where(row_id == j, _dot(q_h, s.astype(BF16)), qs)
        y_ref[:, vs] = qk_v[:, vs] * v_h + gamma * qs


def _ret_step(q, k, v, s_all, layer):
    b = q.shape[0]
    nqk = RET_HEADS * RET_QK
    cols = pl.BlockSpec((1, nqk, 2 * DEC_TB), lambda i: (i, 0, 0))
    qk_rows = pl.BlockSpec((DEC_TB, nqk), lambda i: (i, 0))
    rows = pl.BlockSpec((DEC_TB, RET_W), lambda i: (i, 0))
    esel = pl.BlockSpec((2 * DEC_TB, DEC_TB * LANES), lambda i: (0, 0))
    seg = (jnp.arange(nqk)[:, None] // RET_QK == jnp.arange(RET_W)[None, :] // RET_V).astype(BF16)
    st = pl.BlockSpec((DEC_TB, RET_HEADS, RET_QK, RET_V), lambda i: (i, 0, 0, 0))
    st_in = pl.BlockSpec((None, DEC_TB, RET_HEADS, RET_QK, RET_V), lambda i: (layer, i, 0, 0, 0))
    return pl.pallas_call(
        _ret_step_body,
        grid=(b // DEC_TB,),
        in_specs=[cols, qk_rows, qk_rows, rows, esel, pl.BlockSpec((nqk, RET_W), lambda i: (0, 0)), st_in],
        out_specs=[rows, st],
        out_shape=[jax.ShapeDtypeStruct((b, RET_W), F32), jax.ShapeDtypeStruct(s_all.shape[1:], F32)],
        compiler_params=_cparams(("parallel",)),
        name="ret_step",
    )(_to_cols(k), q, k, v, _col_selector(), seg, s_all)


def _ml_step_body(kc_ref, q_ref, k_ref, v_ref, ig_ref, lf_ref, esel_ref, hsel_ref, hsel_k_ref, seg_ref,
                  c_ref, n_ref, m_ref, h_ref, co_ref, no_ref, mo_ref):
    DK, DV, H = ML_QK, ML_V, ML_HEADS
    ig, lf, m_prev = ig_ref[...], lf_ref[...], m_ref[...]
    m_new = jnp.maximum(lf + m_prev, ig)
    dec = jnp.exp(lf + m_prev - m_new)
    wgt = jnp.exp(ig - m_new)
    mo_ref[...] = m_new
    hsel = hsel_ref[...]
    dec_v = _exact_rhs_dot(dec, hsel)
    wgt_v = _exact_rhs_dot(wgt, hsel)
    floor_v = _exact_rhs_dot(jnp.exp(-m_new), hsel)
    q, k, n_prev = q_ref[...], k_ref[...], n_ref[...]
    no_ref[...] = _exact_rhs_dot(dec, hsel_k_ref[...]) * n_prev + k * _exact_rhs_dot(wgt, hsel_k_ref[...])
    s_v = _exact_rhs_dot(q * k, seg_ref[...]) * wgt_v
    den_v = s_v + dec_v * _exact_rhs_dot(q * n_prev, seg_ref[...])
    row_id = lax.broadcasted_iota(jnp.int32, (DEC_TB, 1), 0)
    for h in range(H):
        ks = slice(h * DK, (h + 1) * DK)
        vs = slice(h * DV, (h + 1) * DV)
        k_rep = _exact_rhs_dot(kc_ref[0, ks, :], esel_ref[...])
        q_h = q[:, ks].astype(BF16)
        v_h = v_ref[:, vs]
        kv_scale = wgt_v[:, vs] * v_h
        qc = jnp.zeros((DEC_TB, DV), F32)
        for j in range(DEC_TB):
            c_prev = c_ref[j, h]
            co_ref[j, h] = dec_v[j:j + 1, vs] * c_prev + k_rep[:, j * LANES:(j + 1) * LANES] * kv_scale[j:j + 1, :]
            qc = jnp.where(row_id == j, _dot(q_h, c_prev.astype(BF16)), qc)
        num = s_v[:, vs] * v_h + dec_v[:, vs] * qc
        h_ref[:, vs] = num / jnp.maximum(jnp.abs(den_v[:, vs]), floor_v[:, vs])


def _ml_step(q, k, v, ig, lf, c_all, layer, n0, m0):
    b = q.shape[0]
    nqk = ML_HEADS * ML_QK
    pad_h = lambda x: jnp.pad(x, ((0, 0), (0, LANES - ML_HEADS)))
    head = jnp.arange(LANES)[:, None]
    hsel = (head == jnp.arange(ML_W)[None, :] // ML_V).astype(BF16)
    hsel_k = (head == jnp.arange(nqk)[None, :] // ML_QK).astype(BF16)
    seg = (jnp.arange(nqk)[:, None] // ML_QK == jnp.arange(ML_W)[None, :] // ML_V).astype(BF16)
    const = lambda r, c: pl.BlockSpec((r, c), lambda i: (0, 0))
    cols = pl.BlockSpec((1, nqk, 2 * DEC_TB), lambda i: (i, 0, 0))
    qk_rows = pl.BlockSpec((DEC_TB, nqk), lambda i: (i, 0))
    rows = pl.BlockSpec((DEC_TB, ML_W), lambda i: (i, 0))
    sc = pl.BlockSpec((DEC_TB, LANES), lambda i: (i, 0))
    cs = pl.BlockSpec((DEC_TB, ML_HEADS, ML_QK, ML_V), lambda i: (i, 0, 0, 0))
    cs_in = pl.BlockSpec((None, DEC_TB, ML_HEADS, ML_QK, ML_V), lambda i: (layer, i, 0, 0, 0))
    hm, c_new, n_new, m_new = pl.pallas_call(
        _ml_step_body,
        grid=(b // DEC_TB,),
        in_specs=[cols, qk_rows, qk_rows, rows, sc, sc, const(2 * DEC_TB, DEC_TB * LANES), const(LANES, ML_W),
                  const(LANES, nqk), const(nqk, ML_W), cs_in, qk_rows, sc],
        out_specs=[rows, cs, qk_rows, sc],
        out_shape=[jax.ShapeDtypeStruct((b, ML_W), F32), jax.ShapeDtypeStruct(c_all.shape[1:], F32),
                   jax.ShapeDtypeStruct((b, nqk), F32), jax.ShapeDtypeStruct((b, LANES), F32)],
        compiler_params=_cparams(("parallel",)),
        name="ml_step",
    )(_to_cols(k), q, k, v, pad_h(ig), pad_h(lf), _col_selector(), hsel, hsel_k, seg,
      c_all, n0.reshape(b, nqk), pad_h(m0))
    return hm, c_new, n_new.reshape(n0.shape), m_new[:, :ML_HEADS]


def _heads(a, h):
    return a.reshape(a.shape[:-1] + (h, a.shape[-1] // h))


def _shift_prev(p, prev_row):
    return jnp.concatenate([prev_row[:, None, :], p[:, :-1]], axis=1)


def _rotary(x, pos):
    half = x.shape[-1] // 2
    inv = ROPE_BASE ** (-jnp.arange(half, dtype=F32) / half)
    ang = pos.astype(F32)[:, None] * inv[None, :]
    cos = jnp.cos(ang)[None, :, None, :]
    sin = jnp.sin(ang)[None, :, None, :]
    x1, x2 = x[..., :half], x[..., half:]
    return jnp.concatenate([x1 * cos - x2 * sin, x1 * sin + x2 * cos], -1)


def _small_matmul(x, w):
    lead = x.shape[:-1]
    kdim, n = w.shape
    x2 = x.reshape(-1, kdim)
    m = x2.shape[0]
    kp = -(-kdim // LANES) * LANES
    npad = -(-n // LANES) * LANES
    x2 = jnp.pad(x2.astype(BF16), ((0, 0), (0, kp - kdim)))
    w2 = jnp.pad(w.astype(BF16), ((0, kp - kdim), (0, npad - n)))
    tm = 1024 if m % 1024 == 0 else m
    out = _matmul(x2, w2, tm, npad)
    return out[:, :n].reshape(lead + (n,))


def _mix_prompt(p, pos, v_first, lp):
    o_rw, v_first, s_new = _rwkv_prompt(p, lp, v_first)
    o_ret, r_new = _ret_prompt(p, pos)
    o_ml, c_new, n_new, m_new = _ml_prompt(p, lp)
    return (o_rw, o_ret, o_ml), v_first, (s_new, r_new, c_new, n_new, m_new)


def _mix_sample(p, pos, v_first, st, lp, prev_row):
    bsz, t_len, _ = p.shape

    p_rw = p[..., :RW_P]
    mixed = p_rw + (_shift_prev(p_rw, prev_row[:, :RW_P]) - p_rw) * lp['rw_mu']
    sizes = np.cumsum([RW_W, RW_W, RW_W, RW_LORA_W, RW_LORA_A, RW_LORA_G])[:-1]
    r, k, v, xw, xa, xg = jnp.split(mixed, [int(s) for s in sizes], axis=-1)
    lora_in = [jnp.tanh(xw), xa, jax.nn.sigmoid(xg)]
    lora_w = [lp['rw_w2'], lp['rw_a2'], lp['rw_g2']]
    if v_first is not None:
        pv = p[..., P_V1:P_V1 + RW_LORA_V]
        lora_in.append(pv + (_shift_prev(pv, prev_row[:, P_V1:P_V1 + RW_LORA_V]) - pv) * lp['rw_vmu'])
        lora_w.append(lp['rw_v2'])
    lora = _small_matmul(jnp.concatenate(lora_in, -1), jax.scipy.linalg.block_diag(*lora_w))
    w = -jax.nn.softplus(-(lp['rw_w0'] + lora[..., :RW_W])) - 0.5
    a = jax.nn.sigmoid(lp['rw_a0'] + lora[..., RW_W:2 * RW_W])
    g = lora[..., 2 * RW_W:3 * RW_W]
    if v_first is None:
        v_first = v
    else:
        v = v + (v_first - v) * jax.nn.sigmoid(lp['rw_v0'] + lora[..., 3 * RW_W:])
    kk = _heads(k * lp['rw_kk'], RW_HEADS)
    kk = kk * lax.rsqrt(jnp.maximum(jnp.sum(jnp.square(kk), -1, keepdims=True), 1e-24))
    kk = kk.reshape(bsz, t_len, RW_W)
    k = k * (1.0 + (a - 1.0) * lp['rw_ka'])
    lw = -jnp.exp(w)
    y, s_new = _rwkv_step(jnp.exp(lw)[:, 0], (kk * a)[:, 0], k[:, 0], kk[:, 0], r[:, 0], v[:, 0],
                          st['rw_wkv_t'], st['layer'])
    y = _heads(y[:, None, :], RW_HEADS)
    y_mu = jnp.mean(y, -1, keepdims=True)
    y_var = jnp.mean(jnp.square(y - y_mu), -1, keepdims=True)
    y = ((y - y_mu) * lax.rsqrt(y_var + RW_GN_EPS)).reshape(bsz, t_len, RW_W)
    y = y * lp['rw_lnx_g'] + lp['rw_lnx_b']
    rh, kh, vh = (_heads(u, RW_HEADS) for u in (r, k, v))
    bonus = jnp.sum(rh * kh * _heads(lp['rw_rk'], RW_HEADS), -1, keepdims=True) * vh
    o_rw = ((y + bonus.reshape(bsz, t_len, RW_W)) * g).astype(BF16)

    nqk = RET_HEADS * RET_QK
    p_ret = p[..., P_RET:P_RET + P_MAIN]
    qr, kr, vr, gr = (p_ret[..., :nqk], p_ret[..., nqk:2 * nqk],
                      p_ret[..., 2 * nqk:2 * nqk + RET_W], p_ret[..., 2 * nqk + RET_W:])
    qh = _rotary(_heads(qr, RET_HEADS), pos).reshape(bsz, t_len, nqk)
    khr = (_rotary(_heads(kr, RET_HEADS), pos) * (RET_QK ** -0.5)).reshape(bsz, t_len, nqk)
    yr, r_new = _ret_step(qh[:, 0], khr[:, 0], vr[:, 0], st['ret_all'], st['layer'])
    yr = _heads(yr[:, None, :], RET_HEADS)
    yr = yr * lax.rsqrt(jnp.mean(jnp.square(yr), -1, keepdims=True) + NORM_EPS)
    o_ret = (jax.nn.silu(gr) * yr.reshape(bsz, t_len, RET_W)).astype(BF16)

    nqk = ML_HEADS * ML_QK
    p_ml = p[..., P_ML:P_ML + P_MAIN]
    qm, km, vm, om = (p_ml[..., :nqk], p_ml[..., nqk:2 * nqk],
                      p_ml[..., 2 * nqk:2 * nqk + ML_W], p_ml[..., 2 * nqk + ML_W:])
    im = p[..., P_GATE:P_GATE + ML_HEADS]
    fm = p[..., P_GATE + ML_HEADS:P_GATE + 2 * ML_HEADS]
    ig = ML_GATE_CAP * jnp.tanh((im + lp['ml_ib']) / ML_GATE_CAP)
    lf = jax.nn.log_sigmoid(ML_GATE_CAP * jnp.tanh((fm + lp['ml_fb']) / ML_GATE_CAP))
    km = km * (ML_QK ** -0.5)
    hm, c_new, n_new, m_new = _ml_step(qm[:, 0], km[:, 0], vm[:, 0], ig[:, 0], lf[:, 0],
                                       st['ml_c_all'], st['layer'], st['ml_n'], st['ml_m'])
    hm = _heads(hm[:, None, :], ML_HEADS)
    hm = hm * lax.rsqrt(jnp.mean(jnp.square(hm), -1, keepdims=True) + NORM_EPS)
    o_ml = (jax.nn.sigmoid(om) * (hm.reshape(bsz, t_len, ML_W) * lp['ml_norm'])).astype(BF16)

    return (o_rw, o_ret, o_ml), v_first, (s_new, r_new, c_new, n_new, m_new)


_T_V1 = P_V1 // LANES
_T_GATE = P_GATE // LANES
_T_RET = P_RET // LANES
_T_SHIFT = (P_RET - RW_P) // LANES
_T_SRC_GATE = (RW_P + RET_P + 2 * ML_HEADS * ML_QK + 2 * ML_W) // LANES


def _pack_body(w_ref, v1_ref, *o_refs):
    j = pl.program_id(0)
    row = lax.broadcasted_iota(jnp.int32, (LANES, 1), 0)
    spare = jnp.logical_and(j > _T_GATE, j < _T_RET)
    for l, o_ref in enumerate(o_refs):
        w = w_ref[:, l, :]
        gates = jnp.where(row < 2 * ML_HEADS, w, 0.0)
        out = jnp.where(j == _T_V1, v1_ref[l], jnp.where(j == _T_GATE, gates, jnp.where(spare, 0.0, w)))
        o_ref[...] = out.astype(BF16)


def _pack_w_in(w_in, rw_v1):
    depth, d, _ = w_in.shape
    w_t = w_in.transpose(2, 0, 1)
    v1_t = jnp.pad(rw_v1.transpose(0, 2, 1), ((1, 0), (0, LANES - RW_LORA_V), (0, 0)))

    def src_tile(j):
        return jnp.where(j < _T_V1, j, jnp.where(j == _T_GATE, _T_SRC_GATE, j - _T_SHIFT))

    return pl.pallas_call(
        _pack_body,
        grid=(P_PAD // LANES,),
        in_specs=[pl.BlockSpec((LANES, depth, d), lambda j: (src_tile(j), 0, 0)),
                  pl.BlockSpec((depth, LANES, d), lambda j: (0, 0, 0))],
        out_specs=[pl.BlockSpec((LANES, d), lambda j: (j, 0))] * depth,
        out_shape=[jax.ShapeDtypeStruct((P_PAD, d), BF16)] * depth,
        compiler_params=_cparams(("parallel",)),
        name="pack_w_in",
    )(w_t, v1_t)


def _token_tiles(m):
    if m % 2048 == 0:
        return 2048, 512, 512
    return m, m, m


def kernel(x_prompt, x_sample, state_rw_shift, state_rw_wkv, state_ret, state_ml_c, state_ml_n, state_ml_m,
           ln0_g, ln0_b, w_in, rw_mu, rw_w0, rw_w2, rw_a0, rw_a2, rw_g2, rw_kk, rw_ka, rw_rk,
           rw_lnx_g, rw_lnx_b, rw_v0, rw_v1, rw_vmu, rw_v2, ml_ib, ml_fb, ml_norm, w_out,
           ln1_g, ln1_b, w_gate, w_up, w_down, ln2_g, ln2_b):
    bp, tp, d = x_prompt.shape
    bs, ts, _ = x_sample.shape
    mp, ms = bp * tp, bs * ts
    pos_p = jnp.arange(tp)
    pos_s = PAST_LEN + jnp.arange(ts)
    tm_big, tm_out, tm_down = _token_tiles(mp)
    rw_wkv_t = state_rw_wkv.transpose(0, 2, 3, 4, 1)
    w_in_packed = _pack_w_in(w_in, rw_v1)
    xf_p, xb_p = _layernorm(x_prompt.reshape(mp, d), ln0_g, ln0_b, tm_out)
    xf_s, xb_s = _layernorm(x_sample.reshape(ms, d), ln0_g, ln0_b, ms)
    vf_p = vf_s = None
    outs_p, outs_s = [], []

    for l in range(DEPTH):
        lp = {
            'rw_mu': rw_mu[l], 'rw_w0': rw_w0[l], 'rw_w2': rw_w2[l], 'rw_a0': rw_a0[l], 'rw_a2': rw_a2[l],
            'rw_g2': rw_g2[l], 'rw_kk': rw_kk[l], 'rw_ka': rw_ka[l], 'rw_rk': rw_rk[l],
            'rw_lnx_g': rw_lnx_g[l], 'rw_lnx_b': rw_lnx_b[l], 'ml_ib': ml_ib[l], 'ml_fb': ml_fb[l],
            'ml_norm': ml_norm[l],
        }
        if l > 0:
            lp.update(rw_v0=rw_v0[l - 1], rw_vmu=rw_vmu[l - 1], rw_v2=rw_v2[l - 1])
        w_out_b = w_out[l].astype(BF16)
        w_down_b = w_down[l].astype(BF16)
        x_side = jnp.concatenate([xb_s, state_rw_shift[l].astype(BF16)], axis=0)
        p_p, p_side = _in_proj(xb_p, x_side, w_in_packed[l], tm_big, P_TN)
        o_p, vf_p, st_p = _mix_prompt(p_p.reshape(bp, tp, P_PAD), pos_p, vf_p, lp)
        st = {'rw_wkv_t': rw_wkv_t, 'layer': l, 'ret_all': state_ret, 'ml_c_all': state_ml_c,
              'ml_n': state_ml_n[l], 'ml_m': state_ml_m[l]}
        o_s, vf_s, st_s = _mix_sample(p_side[:ms].reshape(bs, ts, P_PAD), pos_s, vf_s, st, lp, p_side[ms:])
        outs_p.append((xf_p.reshape(bp, tp, d)[:, -1],) + st_p)
        outs_s.append((xf_s.reshape(bs, ts, d)[:, -1],) + st_s)
        flat = lambda o, m: tuple(u.reshape(m, u.shape[-1]) for u in o)
        x1f_p, x1b_p, x1f_s, x1b_s = _out_proj_ln(flat(o_p, mp), xf_p, flat(o_s, ms), xf_s, w_out_b,
                                                  ln1_g[l], ln1_b[l], tm_out)
        hdn_p, hdn_s = _matmul_swiglu(x1b_p, x1b_s, w_gate, w_up, l, tm_big, 512)
        xf_p, xb_p, xf_s, xb_s = _matmul_res_ln(hdn_p, hdn_s, w_down_b, x1f_p, x1f_s, ln2_g[l], ln2_b[l],
                                                tm_down, DOWN_TK)

    y_p = xf_p.reshape(bp, tp, d)
    y_s = xf_s.reshape(bs, ts, d)
    sp = [jnp.stack([o[i] for o in outs_p]) for i in range(6)]
    ss = [jnp.stack([o[i] for o in outs_s]) for i in range(6)]
    ss[1] = ss[1].transpose(0, 4, 1, 2, 3)
    return (y_p, y_s, sp[0], sp[1], sp[2], sp[3], sp[4], sp[5], ss[0], ss[1], ss[2], ss[3], ss[4], ss[5])
```

```python
import functools
import math

import numpy as np
import jax
import jax.numpy as jnp
from jax import lax
from jax.experimental import pallas as pl
from jax.experimental.pallas import tpu as pltpu

F32 = jnp.float32
BF16 = jnp.bfloat16

D_MODEL = 2048
DEPTH = 2
PAST_LEN = 16384
RW_HD = 64
RW_W = D_MODEL // 4
RW_HEADS = RW_W // RW_HD
RW_LORA_W = 64
RW_LORA_A = 64
RW_LORA_V = 32
RW_LORA_G = 128
RW_P = 3 * RW_W + RW_LORA_W + RW_LORA_A + RW_LORA_G
RW_GN_EPS = 64e-5
RET_V = 128
RET_QK = 64
RET_W = 3 * D_MODEL // 8
RET_HEADS = RET_W // RET_V
RET_P = 2 * RET_HEADS * RET_QK + 2 * RET_W
ML_V = 128
ML_QK = 64
ML_W = D_MODEL - RW_W - RET_W
ML_HEADS = ML_W // ML_V
ML_P = 2 * ML_HEADS * ML_QK + 2 * ML_W + 2 * ML_HEADS
ML_GATE_CAP = 15.0
P_TOTAL = RW_P + RET_P + ML_P
D_FF = ((8 * D_MODEL + 3 * 256 - 1) // (3 * 256)) * 256
CHUNK = 128
ROPE_BASE = 10000.0
LN_EPS = 1e-5
NORM_EPS = 1e-6
ALPHA = (2 * DEPTH) ** 0.25

LANES = 128
P_V1 = RW_P
P_GATE = RW_P + LANES
P_RET = 2304
P_ML = 2 * P_RET
P_MAIN = 2304
P_PAD = 3 * P_RET
P_TN = 768
RW_CHUNK = 64
DEC_TB = 8
VMEM_LIMIT = 56 * 1024 * 1024

def _cparams(sem):
    return pltpu.CompilerParams(dimension_semantics=sem, vmem_limit_bytes=VMEM_LIMIT)


def _dot(a, b):
    return lax.dot_general(a, b, (((1,), (0,)), ((), ())), preferred_element_type=F32)


def _dot_nt(a, b):
    return lax.dot_general(a, b, (((1,), (1,)), ((), ())), preferred_element_type=F32)


def _dot_tn(a, b):
    return lax.dot_general(a, b, (((0,), (0,)), ((), ())), preferred_element_type=F32)


def _ln_rows(x, g, b):
    mu = jnp.mean(x, -1, keepdims=True)
    xc = x - mu
    var = jnp.mean(xc * xc, -1, keepdims=True)
    return xc * lax.rsqrt(var + LN_EPS) * g + b


def _ln_body(x_ref, g_ref, b_ref, of_ref, ob_ref):
    y = _ln_rows(x_ref[...], g_ref[...], b_ref[...])
    of_ref[...] = y
    ob_ref[...] = y.astype(BF16)


def _layernorm(x, g, b, tm):
    m, d = x.shape
    return pl.pallas_call(
        _ln_body,
        grid=(m // tm,),
        in_specs=[pl.BlockSpec((tm, d), lambda i: (i, 0)),
                  pl.BlockSpec((1, d), lambda i: (0, 0)),
                  pl.BlockSpec((1, d), lambda i: (0, 0))],
        out_specs=[pl.BlockSpec((tm, d), lambda i: (i, 0)),
                   pl.BlockSpec((tm, d), lambda i: (i, 0))],
        out_shape=[jax.ShapeDtypeStruct((m, d), F32), jax.ShapeDtypeStruct((m, d), BF16)],
        compiler_params=_cparams(("parallel",)),
        name="layernorm",
    )(x, g.reshape(1, d), b.reshape(1, d))


def _mm_body(x_ref, w_ref, o_ref, *, w_transposed):
    dot = _dot_nt if w_transposed else _dot
    o_ref[...] = dot(x_ref[...], w_ref[...]).astype(o_ref.dtype)


def _matmul(x, w, tm, tn, out_dtype=F32, w_transposed=False):
    m, k = x.shape
    n = w.shape[0] if w_transposed else w.shape[1]
    w_spec = (pl.BlockSpec((tn, k), lambda i, j: (j, 0)) if w_transposed
              else pl.BlockSpec((k, tn), lambda i, j: (0, j)))
    return pl.pallas_call(
        functools.partial(_mm_body, w_transposed=w_transposed),
        grid=(m // tm, n // tn),
        in_specs=[pl.BlockSpec((tm, k), lambda i, j: (i, 0)), w_spec],
        out_specs=pl.BlockSpec((tm, tn), lambda i, j: (i, j)),
        out_shape=jax.ShapeDtypeStruct((m, n), out_dtype),
        compiler_params=_cparams(("parallel", "parallel")),
        name="matmul",
    )(x, w)


def _in_proj_body(x_ref, xs_ref, w_ref, o_ref, os_ref):
    @pl.when(pl.program_id(1) == 0)
    def _():
        os_ref[...] = _dot_nt(xs_ref[...], w_ref[...])

    o_ref[...] = _dot_nt(x_ref[...], w_ref[...])


def _in_proj(x, x_side, w_t, tm, tn):
    m, k = x.shape
    ms = x_side.shape[0]
    n = w_t.shape[0]
    return pl.pallas_call(
        _in_proj_body,
        grid=(n // tn, m // tm),
        in_specs=[pl.BlockSpec((tm, k), lambda j, i: (i, 0)), pl.BlockSpec((ms, k), lambda j, i: (0, 0)),
                  pl.BlockSpec((tn, k), lambda j, i: (j, 0))],
        out_specs=[pl.BlockSpec((tm, tn), lambda j, i: (i, j)), pl.BlockSpec((ms, tn), lambda j, i: (0, j))],
        out_shape=[jax.ShapeDtypeStruct((m, n), F32), jax.ShapeDtypeStruct((ms, n), F32)],
        compiler_params=_cparams(("parallel", "arbitrary")),
        name="in_proj",
    )(x, x_side, w_t)


def _swiglu_body(x_ref, xs_ref, wg_ref, wu_ref, wd_ref, o_ref, os_ref, wdb_ref, wg_scr, wu_scr):
    def act(x):
        g = _dot(x, wg_scr[...])
        return (g * jax.nn.sigmoid(g) * _dot(x, wu_scr[...])).astype(BF16)

    wdb_ref[...] = wd_ref[...].astype(BF16)

    @pl.when(pl.program_id(1) == 0)
    def _():
        wg_scr[...] = wg_ref[...].astype(BF16)
        wu_scr[...] = wu_ref[...].astype(BF16)
        os_ref[...] = act(xs_ref[...])

    step = x_ref.shape[0] // SWIGLU_CHUNKS
    for r in range(SWIGLU_CHUNKS):
        rows = pl.ds(r * step, step)
        o_ref[rows, :] = act(x_ref[rows, :])


def _matmul_swiglu(x, x_side, wg, wu, wd, layer, tm, tn):
    m, k = x.shape
    ms = x_side.shape[0]
    n = wg.shape[2]
    n_i = m // tm
    slab = n // ((n // tn) * n_i)
    assert slab * (n // tn) * n_i == n and slab % 16 == 0, (n, tn, n_i)
    w_spec = pl.BlockSpec((None, k, tn), lambda j, i: (layer, 0, j))
    return pl.pallas_call(
        _swiglu_body,
        grid=(n // tn, n_i),
        in_specs=[pl.BlockSpec((tm, k), lambda j, i: (i, 0)), pl.BlockSpec((ms, k), lambda j, i: (0, 0)),
                  w_spec, w_spec, pl.BlockSpec((None, slab, k), lambda j, i: (layer, j * n_i + i, 0))],
        out_specs=[pl.BlockSpec((tm, tn), lambda j, i: (i, j)), pl.BlockSpec((ms, tn), lambda j, i: (0, j)),
                   pl.BlockSpec((slab, k), lambda j, i: (j * n_i + i, 0))],
        out_shape=[jax.ShapeDtypeStruct((m, n), BF16), jax.ShapeDtypeStruct((ms, n), BF16),
                   jax.ShapeDtypeStruct((n, k), BF16)],
        scratch_shapes=[pltpu.VMEM((k, tn), BF16), pltpu.VMEM((k, tn), BF16)],
        compiler_params=_cparams(("parallel", "arbitrary")),
        name="matmul_swiglu",
    )(x, x_side, wg, wu, wd)


SWIGLU_CHUNKS = 2
DOWN_TK = D_FF // 2
LN_ROWS = 256
OUT_ROWS = 256


def _res_ln_store(acc_ref, res_ref, g_ref, b_ref, of_ref, ob_ref, n_rows):
    step = min(LN_ROWS, n_rows)
    for r in range(0, n_rows, step):
        rows = pl.ds(r, step)
        y = _ln_rows(ALPHA * res_ref[rows, :] + acc_ref[rows, :], g_ref[...], b_ref[...])
        of_ref[rows, :] = y
        ob_ref[rows, :] = y.astype(BF16)


def _mm_res_ln_body(x_ref, xs_ref, w_ref, res_ref, ress_ref, g_ref, b_ref, of_ref, ob_ref, ofs_ref, obs_ref, *, nk, tm, ms):
    i = pl.program_id(0)
    kk = pl.program_id(1)

    @pl.when(kk == 0)
    def _():
        of_ref[...] = jnp.zeros_like(of_ref)

    of_ref[...] += _dot(x_ref[...], w_ref[...])

    @pl.when(kk == nk - 1)
    def _():
        _res_ln_store(of_ref, res_ref, g_ref, b_ref, of_ref, ob_ref, tm)

    @pl.when(i == 0)
    def _():
        @pl.when(kk == 0)
        def _():
            ofs_ref[...] = jnp.zeros_like(ofs_ref)

        ofs_ref[...] += _dot(xs_ref[...], w_ref[...])

        @pl.when(kk == nk - 1)
        def _():
            _res_ln_store(ofs_ref, ress_ref, g_ref, b_ref, ofs_ref, obs_ref, ms)


def _matmul_res_ln(x, x_side, w, res, res_side, g, b, tm, tk):
    m, k = x.shape
    ms = x_side.shape[0]
    n = w.shape[1]
    nk = k // tk
    const = lambda r: pl.BlockSpec((r, n), lambda i, j: (0, 0))
    main = pl.BlockSpec((tm, n), lambda i, j: (i, 0))
    return pl.pallas_call(
        functools.partial(_mm_res_ln_body, nk=nk, tm=tm, ms=ms),
        grid=(m // tm, nk),
        in_specs=[pl.BlockSpec((tm, tk), lambda i, j: (i, j)), pl.BlockSpec((ms, tk), lambda i, j: (0, j)),
                  pl.BlockSpec((tk, n), lambda i, j: (j, 0)), main, const(ms), const(1), const(1)],
        out_specs=[main, main, const(ms), const(ms)],
        out_shape=[jax.ShapeDtypeStruct((m, n), F32), jax.ShapeDtypeStruct((m, n), BF16),
                   jax.ShapeDtypeStruct((ms, n), F32), jax.ShapeDtypeStruct((ms, n), BF16)],
        compiler_params=_cparams(("arbitrary", "arbitrary")),
        name="matmul_res_ln",
    )(x, x_side, w, res, res_side, g.reshape(1, n), b.reshape(1, n))


def _out_proj_ln_body(o_rw_ref, o_ret_ref, o_ml_ref, res_ref, s_rw_ref, s_ret_ref, s_ml_ref, ress_ref,
                      w_ref, g_ref, b_ref, of_ref, ob_ref, ofs_ref, obs_ref, *, tm, ms):
    def project(rw_ref, ret_ref, ml_ref, r_ref, f_ref, h_ref, n_rows):
        step = min(OUT_ROWS, n_rows)
        for r in range(0, n_rows, step):
            rows = pl.ds(r, step)
            mix = (_dot(rw_ref[rows, :], w_ref[0:RW_W, :])
                   + _dot(ret_ref[rows, :], w_ref[RW_W:RW_W + RET_W, :])
                   + _dot(ml_ref[rows, :], w_ref[RW_W + RET_W:, :]))
            y = _ln_rows(ALPHA * r_ref[rows, :] + mix, g_ref[...], b_ref[...])
            f_ref[rows, :] = y
            h_ref[rows, :] = y.astype(BF16)

    @pl.when(pl.program_id(0) == 0)
    def _():
        project(s_rw_ref, s_ret_ref, s_ml_ref, ress_ref, ofs_ref, obs_ref, ms)

    project(o_rw_ref, o_ret_ref, o_ml_ref, res_ref, of_ref, ob_ref, tm)


def _out_proj_ln(o, res, o_side, res_side, w, g, b, tm):
    m = res.shape[0]
    ms = res_side.shape[0]
    n = w.shape[1]
    rows = lambda width: pl.BlockSpec((tm, width), lambda i: (i, 0))
    const = lambda r, c: pl.BlockSpec((r, c), lambda i: (0, 0))
    widths = (RW_W, RET_W, ML_W)
    return pl.pallas_call(
        functools.partial(_out_proj_ln_body, tm=tm, ms=ms),
        grid=(m // tm,),
        in_specs=([rows(c) for c in widths] + [rows(n)] + [const(ms, c) for c in widths] + [const(ms, n)]
                  + [const(D_MODEL, n), const(1, n), const(1, n)]),
        out_specs=[rows(n), rows(n), const(ms, n), const(ms, n)],
        out_shape=[jax.ShapeDtypeStruct((m, n), F32), jax.ShapeDtypeStruct((m, n), BF16),
                   jax.ShapeDtypeStruct((ms, n), F32), jax.ShapeDtypeStruct((ms, n), BF16)],
        compiler_params=_cparams(("arbitrary",)),
        name="out_proj_ln",
    )(*o, res, *o_side, res_side, w, g.reshape(1, n), b.reshape(1, n))


RW_TB = 512
RW_GH = 4
RW_GW = RW_GH * RW_HD
RW_AHEAD = 8
RW_VEC_ROWS = 8


def _split3(x):
    hi = x.astype(BF16)
    r1 = x - hi.astype(F32)
    mid = r1.astype(BF16)
    lo = (r1 - mid.astype(F32)).astype(BF16)
    return hi, mid, lo


def _mm(a, b, dims):
    return lax.dot_general(a.astype(BF16), b.astype(BF16), (dims, ((), ())), preferred_element_type=F32)


_NN = ((1,), (0,))
_NT = ((1,), (1,))
_TN = ((0,), (0,))


def _exact_lhs_dot(a_bf16, b):
    hi, mid, lo = _split3(b)
    dg = lambda y: lax.dot_general(a_bf16, y, (_NN, ((), ())), preferred_element_type=F32)
    return dg(hi) + (dg(mid) + dg(lo))


def _exact_rhs_dot(a, b_bf16):
    hi, mid, lo = _split3(a)
    dg = lambda x: lax.dot_general(x, b_bf16, (_NN, ((), ())), preferred_element_type=F32)
    return dg(hi) + (dg(mid) + dg(lo))


def _seg_sum(a, seg_bf16):
    hi = a.astype(BF16)
    lo = (a - hi.astype(F32)).astype(BF16)
    gw = seg_bf16.shape[0]
    dg = lambda x: lax.dot_general(x, seg_bf16, (_NN, ((), ())), preferred_element_type=F32)
    groups = [dg(hi[:, c:c + gw]) + dg(lo[:, c:c + gw]) for c in range(0, a.shape[1], gw)]
    return groups[0] if len(groups) == 1 else jnp.concatenate(groups, axis=1)


def _rw_scan(r, lw, k, v, kk, a, st_scr, y_scr):
    L, TB, G = RW_CHUNK, RW_TB, RW_GW
    row = lax.broadcasted_iota(jnp.int32, (L, G), 0)
    col = lax.broadcasted_iota(jnp.int32, (L, G), 1) & (L - 1)
    strict, lower, eye = row > col, row >= col, (row == col).astype(F32)
    rg = lax.broadcasted_iota(jnp.int32, (G, G), 0) // RW_HD
    cg = lax.broadcasted_iota(jnp.int32, (G, G), 1) // RW_HD
    mask_bd = rg == cg
    rt = lax.broadcasted_iota(jnp.int32, (2 * L, 2 * L), 0)
    ct = lax.broadcasted_iota(jnp.int32, (2 * L, 2 * L), 1)
    tri = jnp.logical_and(rt >= ct, rt // L == ct // L).astype(BF16)
    bd = lambda x: jnp.where(mask_bd, jnp.concatenate([x.astype(BF16)] * RW_GH, axis=0), 0.0)
    cut = lambda x, b: x[b[0] * L:(b[0] + 1) * L, b[1] * G:(b[1] + 1) * G]

    cum = jnp.concatenate([_exact_lhs_dot(tri, lw[i:i + 2 * L, :]) for i in range(0, TB, 2 * L)], axis=0)
    e_neg = jnp.exp(-cum)
    ap = kk * a
    ap_h = ap * e_neg
    k_h = k * e_neg
    kk_t = kk * jnp.exp(cum - lw)
    r_t = r * jnp.exp(cum)

    lhs, n_m, m_a, m_kr, inv, mv, vk, a_end, decay = ({} for _ in range(9))
    n_groups = RW_HEADS // RW_GH

    def prep(chunks):
        blk = [(s, g) for s in chunks for g in range(n_groups)]
        for b in blk:
            lhs[b] = jnp.concatenate([cut(kk_t, b), cut(r_t, b)], axis=0)
            sc_a = _mm(lhs[b], bd(cut(ap_h, b)), _NT)
            sc_k = _mm(lhs[b], bd(cut(k_h, b)), _NT)
            n_m[b] = jnp.where(strict, sc_a[:L], 0.0)
            m_a[b] = jnp.where(lower, sc_a[L:], 0.0)
            m_kr[b] = jnp.concatenate([jnp.where(strict, sc_k[:L], 0.0), jnp.where(lower, sc_k[L:], 0.0)], axis=0)
        pw = {}
        for b in blk:
            inv[b] = eye - n_m[b]
            pw[b] = _mm(n_m[b], bd(n_m[b]), _NN)
        n_iter = int(math.log2(L)) - 1
        for j in range(n_iter):
            last = j == n_iter - 1
            for b in blk:
                lhs_j = inv[b] if last else jnp.concatenate([inv[b], pw[b]], axis=0)
                prod = _mm(lhs_j, bd(pw[b]), _NN)
                inv[b] = inv[b] + prod[:L]
                if not last:
                    pw[b] = prod[L:]
        for b in blk:
            s, g = b
            tot = cum[(s + 1) * L - 1:(s + 1) * L, g * G:(g + 1) * G]
            e_end = jnp.exp(tot - cut(cum, b))
            mv[b] = _mm(m_kr[b], bd(cut(v, b)), _NN)
            vk[b] = _mm(cut(v, b), cut(k, b) * e_end, _TN)
            a_end[b] = cut(ap, b) * e_end
            decay[b] = jnp.exp(tot)

    def apply(s):
        for g in range(n_groups):
            b = (s, g)
            st = st_scr[g]
            s_terms = _mm(lhs[b], st, _NT)
            u = _mm(inv[b], bd(s_terms[:L] + mv[b][:L]), _NN)
            y_scr[s * L:(s + 1) * L, g * G:(g + 1) * G] = s_terms[L:] + mv[b][L:] - _mm(m_a[b], bd(u), _NN)
            st_scr[g] = jnp.where(mask_bd, st * decay[b] + vk[b] - _mm(u, a_end[b], _TN), 0.0)

    n_chunks = TB // L
    prep(range(min(RW_AHEAD, n_chunks)))
    for s in range(n_chunks):
        apply(s)
        if s + RW_AHEAD < n_chunks:
            prep([s + RW_AHEAD])


def _softplus(z):
    return jnp.maximum(z, 0.0) + jnp.log(1.0 + jnp.exp(-jnp.abs(z)))


def _rwkv_fused_body(*refs, nc, has_vres):
    if has_vres:
        (p_ref, pv_ref, vf_ref, mu_ref, vec_ref, wa_ref, g2_ref, seg_ref, vmu_ref, v2_ref,
         o_ref, sf_ref, st_scr, prev_scr, y_scr, prevv_scr) = refs
    else:
        (p_ref, mu_ref, vec_ref, wa_ref, g2_ref, seg_ref,
         o_ref, vfo_ref, sf_ref, st_scr, prev_scr, y_scr) = refs
    TB, W, L = RW_TB, RW_W, RW_CHUNK
    c = pl.program_id(1)

    @pl.when(c == 0)
    def _():
        st_scr[...] = jnp.zeros_like(st_scr)
        prev_scr[...] = jnp.zeros_like(prev_scr)
        if has_vres:
            prevv_scr[...] = jnp.zeros_like(prevv_scr)

    first_row = lax.broadcasted_iota(jnp.int32, (TB, 1), 0) == 0

    def shift_mix(x, carry_ref, mu):
        prev = jnp.where(first_row, carry_ref[...], pltpu.roll(x, 1, 0))
        carry_ref[...] = x[TB - 1:TB, :]
        return x + (prev - x) * mu

    mixed = shift_mix(p_ref[0], prev_scr, mu_ref[...])
    r = mixed[:, 0:W]
    k = mixed[:, W:2 * W]
    v = mixed[:, 2 * W:3 * W]
    xwa = mixed[:, 3 * W:3 * W + LANES]
    xg = mixed[:, 3 * W + LANES:3 * W + 2 * LANES]
    vec = vec_ref[...]
    w0, a0, kk_s, ka, rk, lnx_g, lnx_b, v0 = (vec[i:i + 1, :] for i in range(RW_VEC_ROWS))
    seg = seg_ref[...]
    wa = wa_ref[...]
    w_lora = _dot(jnp.tanh(xwa).astype(BF16), wa[:, 0:W])
    a_lora = _dot(xwa.astype(BF16), wa[:, W:2 * W])
    lw = -jnp.exp(-_softplus(-(w0 + w_lora)) - 0.5)
    a = jax.nn.sigmoid(a0 + a_lora)
    g = _dot(jax.nn.sigmoid(xg).astype(BF16), g2_ref[...])
    if has_vres:
        xv = shift_mix(pv_ref[0], prevv_scr, vmu_ref[...])
        v = v + (vf_ref[0] - v) * jax.nn.sigmoid(v0 + _dot(xv.astype(BF16), v2_ref[...]))
    else:
        vfo_ref[0] = v
    kk = k * kk_s
    kk = kk * lax.rsqrt(jnp.maximum(_seg_sum(kk * kk, seg), 1e-24))
    k = k * (1.0 + (a - 1.0) * ka)

    _rw_scan(r, lw, k, v, kk, a, st_scr, y_scr)

    y = y_scr[...]
    inv_n = 1.0 / RW_HD
    y_mu = _seg_sum(y, seg) * inv_n
    yc = y - y_mu
    y_var = _seg_sum(yc * yc, seg) * inv_n
    y = yc * lax.rsqrt(y_var + RW_GN_EPS) * lnx_g + lnx_b
    bonus = _seg_sum(r * k * rk, seg) * v
    o_ref[0] = ((y + bonus) * g).astype(BF16)

    @pl.when(c == nc - 1)
    def _():
        sf_ref[0] = st_scr[...]


def _rwkv_prompt(p3, lp, v_first):
    b, t, _ = p3.shape
    nc = t // RW_TB
    has_vres = v_first is not None
    ng = RW_HEADS // RW_GH
    zpad = jnp.zeros((RW_LORA_W, RW_W), F32)
    wa = jnp.concatenate([jnp.concatenate([lp['rw_w2'], zpad], 0), jnp.concatenate([zpad, lp['rw_a2']], 0)], 1)
    vec = jnp.stack([lp['rw_w0'], lp['rw_a0'], lp['rw_kk'], lp['rw_ka'], lp['rw_rk'], lp['rw_lnx_g'], lp['rw_lnx_b'],
                     lp['rw_v0'] if has_vres else jnp.zeros((RW_W,), F32)])
    hid = jnp.arange(RW_GW) // RW_HD
    seg = (hid[:, None] == hid[None, :]).astype(BF16)
    full = lambda shape: pl.BlockSpec(shape, lambda i, j: (0,) * len(shape))
    seq = lambda w, blk: pl.BlockSpec((1, RW_TB, w), lambda i, j: (i, j, blk))
    in_specs = [seq(RW_P, 0)]
    args = [p3]
    if has_vres:
        in_specs += [seq(LANES, P_V1 // LANES), seq(RW_W, 0)]
        args += [p3, v_first]
    in_specs += [full((1, RW_P)), full((RW_VEC_ROWS, RW_W)), full((LANES, 2 * RW_W)), full((RW_LORA_G, RW_W)),
                 full((RW_GW, RW_GW))]
    args += [lp['rw_mu'].reshape(1, RW_P), vec, wa.astype(BF16), lp['rw_g2'].astype(BF16), seg]
    if has_vres:
        in_specs += [full((1, LANES)), full((LANES, RW_W))]
        args += [jnp.pad(lp['rw_vmu'], (0, LANES - RW_LORA_V)).reshape(1, LANES),
                 jnp.pad(lp['rw_v2'], ((0, LANES - RW_LORA_V), (0, 0))).astype(BF16)]
    out_specs = [seq(RW_W, 0)]
    out_shape = [jax.ShapeDtypeStruct((b, t, RW_W), BF16)]
    if not has_vres:
        out_specs.append(seq(RW_W, 0))
        out_shape.append(jax.ShapeDtypeStruct((b, t, RW_W), F32))
    out_specs.append(pl.BlockSpec((1, ng, RW_GW, RW_GW), lambda i, j: (i, 0, 0, 0)))
    out_shape.append(jax.ShapeDtypeStruct((b, ng, RW_GW, RW_GW), F32))
    scratch = [pltpu.VMEM((ng, RW_GW, RW_GW), F32), pltpu.VMEM((1, RW_P), F32), pltpu.VMEM((RW_TB, RW_W), F32)]
    if has_vres:
        scratch.append(pltpu.VMEM((1, LANES), F32))
    outs = pl.pallas_call(
        functools.partial(_rwkv_fused_body, nc=nc, has_vres=has_vres),
        grid=(b, nc),
        in_specs=in_specs,
        out_specs=out_specs,
        out_shape=out_shape,
        scratch_shapes=scratch,
        compiler_params=_cparams(("parallel", "arbitrary")),
        name="rwkv_fused",
    )(*args)
    if has_vres:
        o, st_bd = outs
    else:
        o, v_first, st_bd = outs
    st5 = st_bd.reshape(b, ng, RW_GH, RW_HD, RW_GH, RW_HD)
    s_fin = jnp.stack([st5[:, :, h, :, h, :] for h in range(RW_GH)], axis=2)
    s_fin = s_fin.reshape(b, RW_HEADS, RW_HD, RW_HD).transpose(0, 1, 3, 2)
    return o, v_first, s_fin


def _ret_log_gamma(h):
    return math.log1p(-(2.0 ** (-5.0 - h)))


def _rotary_tables(pos, heads, dk):
    half = dk // 2
    inv = ROPE_BASE ** (-jnp.arange(half, dtype=F32) / half)
    ang = pos.astype(F32)[:, None] * inv[None, :]
    cos = jnp.tile(jnp.concatenate([jnp.cos(ang), jnp.cos(ang)], -1), (1, heads))
    sin = jnp.tile(jnp.concatenate([-jnp.sin(ang), jnp.sin(ang)], -1), (1, heads))
    lane = jnp.arange(LANES)
    perm = (lane[:, None] == (lane[None, :] ^ half)).astype(BF16)
    return cos, sin, perm


def _ret_fused_body(p_ref, cos_ref, sin_ref, perm_ref, o_ref, sf_ref, s_scr, *, nc):
    L, DK, DV, H = CHUNK, RET_QK, RET_V, RET_HEADS
    nq = H * DK
    c = pl.program_id(1)

    @pl.when(c == 0)
    def _():
        s_scr[...] = jnp.zeros_like(s_scr)

    cos = cos_ref[...]
    sin = sin_ref[...]
    perm = perm_ref[...]
    swap = lambda x: jnp.concatenate(
        [_exact_rhs_dot(x[:, c:c + LANES], perm) for c in range(0, nq, LANES)], axis=1)
    rot = lambda x: x * cos + swap(x) * sin
    q_all = rot(p_ref[0, :, 0:nq])
    k_all = rot(p_ref[0, :, nq:2 * nq]) * (DK ** -0.5)
    row = lax.broadcasted_iota(jnp.int32, (L, L), 0)
    col = lax.broadcasted_iota(jnp.int32, (L, L), 1)
    rel = (row - col).astype(F32)
    idx = lax.broadcasted_iota(jnp.int32, (L, 1), 0).astype(F32)
    vs = lambda h: p_ref[0, :, 2 * nq + h * DV:2 * nq + (h + 1) * DV].astype(BF16)
    hd = [dict() for _ in range(H)]
    for h, t in enumerate(hd):
        lg = _ret_log_gamma(h)
        q = q_all[:, h * DK:(h + 1) * DK]
        k = k_all[:, h * DK:(h + 1) * DK]
        t['s_prev'] = s_scr[h]
        t['dmask'] = jnp.where(rel >= 0, jnp.exp(jnp.maximum(rel, 0.0) * lg), 0.0)
        t['qk'] = _dot_nt(q.astype(BF16), k.astype(BF16))
        q_dec = q * jnp.exp((idx + 1.0) * lg)
        t['qs'] = _dot(q_dec.astype(BF16), t['s_prev'].astype(BF16))
        k_end = k * jnp.exp((L - 1.0 - idx) * lg)
        t['kv'] = _dot_tn(k_end.astype(BF16), vs(h))
    for h, t in enumerate(hd):
        y = _dot((t['qk'] * t['dmask']).astype(BF16), vs(h)) + t['qs']
        y = y * lax.rsqrt(jnp.mean(y * y, -1, keepdims=True) + NORM_EPS)
        gate = p_ref[0, :, 2 * nq + RET_W + h * DV:2 * nq + RET_W + (h + 1) * DV]
        o_ref[0, :, h * DV:(h + 1) * DV] = (gate * jax.nn.sigmoid(gate) * y).astype(BF16)
    for h, t in enumerate(hd):
        s_scr[h] = math.exp(L * _ret_log_gamma(h)) * t['s_prev'] + t['kv']

    @pl.when(c == nc - 1)
    def _():
        sf_ref[0] = s_scr[...]


def _ret_prompt(p3, pos):
    b, t, _ = p3.shape
    L = CHUNK
    nc = t // L
    nq = RET_HEADS * RET_QK
    cos, sin, perm = _rotary_tables(pos, RET_HEADS, RET_QK)
    tab = pl.BlockSpec((L, nq), lambda i, j: (j, 0))
    st = pl.BlockSpec((1, RET_HEADS, RET_QK, RET_V), lambda i, j: (i, 0, 0, 0))
    return pl.pallas_call(
        functools.partial(_ret_fused_body, nc=nc),
        grid=(b, nc),
        in_specs=[pl.BlockSpec((1, L, P_MAIN), lambda i, j: (i, j, P_RET // P_MAIN)), tab, tab,
                  pl.BlockSpec((LANES, LANES), lambda i, j: (0, 0))],
        out_specs=[pl.BlockSpec((1, L, RET_W), lambda i, j: (i, j, 0)), st],
        out_shape=[jax.ShapeDtypeStruct((b, t, RET_W), BF16),
                   jax.ShapeDtypeStruct((b, RET_HEADS, RET_QK, RET_V), F32)],
        scratch_shapes=[pltpu.VMEM((RET_HEADS, RET_QK, RET_V), F32)],
        compiler_params=_cparams(("parallel", "arbitrary")),
        name="ret_fused",
    )(p3, cos, sin, perm)


ML_HPAD = 8
ML_SEL_ROWS = 16


def _ml_fused_body(p_ref, gate_ref, bias_ref, norm_ref, sel_ref, o_ref, cf_ref, nf_ref, mf_ref,
                   c_scr, n_scr, m_scr, *, nc):
    L, DK, DV, H = CHUNK, ML_QK, ML_V, ML_HEADS
    nq = H * DK
    ci = pl.program_id(1)

    @pl.when(ci == 0)
    def _():
        c_scr[...] = jnp.zeros_like(c_scr)
        n_scr[...] = jnp.zeros_like(n_scr)
        m_scr[...] = jnp.zeros_like(m_scr)

    row = lax.broadcasted_iota(jnp.int32, (L, L), 0)
    col = lax.broadcasted_iota(jnp.int32, (L, L), 1)
    causal = row >= col
    tri = causal.astype(BF16)
    capped = ML_GATE_CAP * jnp.tanh((gate_ref[0] + bias_ref[...]) * (1.0 / ML_GATE_CAP))
    lane = lax.broadcasted_iota(jnp.int32, (L, LANES), 1)
    g = jnp.where(lane < H, capped, jnp.where(lane < 2 * H, -_softplus(-capped), 0.0))
    g_rep = _exact_rhs_dot(g[:, :ML_SEL_ROWS], sel_ref[...])
    b_rep_all = _exact_lhs_dot(tri, g_rep[:, H * LANES:])
    g_t = g.T
    cum_t = _exact_lhs_dot(tri, g).T
    cm_all = g_rep[:, :H * LANES] - b_rep_all
    row_id = lax.broadcasted_iota(jnp.int32, (L, 1), 0)
    shift = 1
    while shift < L:
        cm_all = jnp.maximum(cm_all, jnp.where(row_id >= shift, pltpu.roll(cm_all, shift, 0), -jnp.inf))
        shift *= 2
    ones = jnp.ones((L, LANES), BF16)
    mean_w = jnp.full((DV, LANES), 1.0 / DV, BF16)
    m_all = m_scr[...]
    hd = [dict() for _ in range(H)]
    for h, t in enumerate(hd):
        hs = slice(h * LANES, (h + 1) * LANES)
        q = p_ref[0, :, h * DK:(h + 1) * DK].astype(BF16)
        k = p_ref[0, :, nq + h * DK:nq + (h + 1) * DK] * (DK ** -0.5)
        t['v1'] = jnp.concatenate([p_ref[0, :, 2 * nq + h * DV:2 * nq + (h + 1) * DV].astype(BF16), ones], axis=1)
        ig_rep = g_rep[:, hs]
        b_rep = b_rep_all[:, hs]
        ig_row = g_t[h:h + 1, :]
        b_row = cum_t[H + h:H + h + 1, :]
        b_tot = b_rep[L - 1:L, :]
        m_prev = m_all[h:h + 1, :]
        t['c_prev'] = c_scr[h]
        t['n_prev'] = n_scr[h]
        t['m_new'] = jnp.maximum(b_tot + m_prev, jnp.max(b_tot - b_rep + ig_rep, axis=0, keepdims=True))
        t['dec'] = jnp.exp(b_tot + m_prev - t['m_new'])
        kw = k * jnp.exp((b_tot - b_rep + ig_rep - t['m_new'])[:, :DK])
        t['kvn'] = _dot_tn(kw.astype(BF16), t['v1'])
        inter = b_rep + m_prev
        t['m_i'] = b_rep + jnp.maximum(cm_all[:, hs], m_prev)
        t['e'] = jnp.exp(jnp.where(causal, (b_rep - t['m_i']) - b_row + ig_row, -jnp.inf))
        t['sc'] = jnp.exp(inter - t['m_i'])
        t['qk'] = _dot_nt(q, k.astype(BF16))
        cn = jnp.concatenate([t['c_prev'], t['n_prev']], axis=1).astype(BF16)
        t['qcn'] = _dot(q, cn)
    for h, t in enumerate(hd):
        nd = _dot((t['qk'] * t['e']).astype(BF16), t['v1'])
        num = nd[:, :DV] + t['sc'] * t['qcn'][:, :DV]
        den = nd[:, DV:] + t['sc'] * t['qcn'][:, DV:]
        hid = num / jnp.maximum(jnp.abs(den), jnp.exp(-t['m_i']))
        hid = hid * lax.rsqrt(_seg_sum(hid * hid, mean_w) + NORM_EPS)
        og = p_ref[0, :, 2 * nq + ML_W + h * DV:2 * nq + ML_W + (h + 1) * DV]
        o_ref[0, :, h * DV:(h + 1) * DV] = (jax.nn.sigmoid(og) * (hid * norm_ref[:, h * DV:(h + 1) * DV])).astype(BF16)
    for h, t in enumerate(hd):
        c_scr[h] = t['dec'] * t['c_prev'] + t['kvn'][:, :DV]
        n_scr[h] = t['dec'] * t['n_prev'] + t['kvn'][:, DV:]
        m_scr[h:h + 1, :] = t['m_new']

    @pl.when(ci == nc - 1)
    def _():
        cf_ref[0] = c_scr[...]
        nf_ref[0] = n_scr[...]
        mf_ref[0] = m_scr[...]


def _ml_prompt(p3, lp):
    b, t, _ = p3.shape
    L = CHUNK
    nc = t // L
    bias = jnp.pad(jnp.concatenate([lp['ml_ib'], lp['ml_fb']]), (0, LANES - 2 * ML_HEADS)).reshape(1, LANES)
    n_rep = 2 * ML_HEADS * LANES
    sel = (jnp.arange(ML_SEL_ROWS)[:, None] == jnp.arange(n_rep)[None, :] // LANES).astype(BF16)
    vs = pl.BlockSpec((1, L, ML_W), lambda i, j: (i, j, 0))
    cs = pl.BlockSpec((1, ML_HEADS, ML_QK, ML_V), lambda i, j: (i, 0, 0, 0))
    ns = pl.BlockSpec((1, ML_HEADS, ML_QK, LANES), lambda i, j: (i, 0, 0, 0))
    ms = pl.BlockSpec((1, ML_HPAD, LANES), lambda i, j: (i, 0, 0))
    o, c_f, n_f, m_f = pl.pallas_call(
        functools.partial(_ml_fused_body, nc=nc),
        grid=(b, nc),
        in_specs=[pl.BlockSpec((1, L, P_MAIN), lambda i, j: (i, j, P_ML // P_MAIN)),
                  pl.BlockSpec((1, L, LANES), lambda i, j: (i, j, P_GATE // LANES)),
                  pl.BlockSpec((1, LANES), lambda i, j: (0, 0)),
                  pl.BlockSpec((1, ML_W), lambda i, j: (0, 0)),
                  pl.BlockSpec((ML_SEL_ROWS, n_rep), lambda i, j: (0, 0))],
        out_specs=[vs, cs, ns, ms],
        out_shape=[jax.ShapeDtypeStruct((b, t, ML_W), BF16),
                   jax.ShapeDtypeStruct((b, ML_HEADS, ML_QK, ML_V), F32),
                   jax.ShapeDtypeStruct((b, ML_HEADS, ML_QK, LANES), F32),
                   jax.ShapeDtypeStruct((b, ML_HPAD, LANES), F32)],
        scratch_shapes=[pltpu.VMEM((ML_HEADS, ML_QK, ML_V), F32),
                        pltpu.VMEM((ML_HEADS, ML_QK, LANES), F32),
                        pltpu.VMEM((ML_HPAD, LANES), F32)],
        compiler_params=_cparams(("parallel", "arbitrary")),
        name="ml_fused",
    )(p3, p3, bias, lp['ml_norm'].reshape(1, ML_W), sel)
    return o, c_f, n_f[..., 0], m_f[:, :ML_HEADS, 0]


def _to_cols(x):
    b, c = x.shape
    cols = x.reshape(b // DEC_TB, DEC_TB, c).transpose(0, 2, 1)
    return jnp.pad(cols, ((0, 0), (0, 0), (0, DEC_TB)))


def _col_selector():
    j = jnp.arange(2 * DEC_TB)[:, None]
    return (j == jnp.arange(DEC_TB * LANES)[None, :] // LANES).astype(BF16)


def _rwkv_step_body(w_ref, ap_ref, k_ref, kk_ref, r_ref, v_ref, s_ref, y_ref, so_ref):
    N = RW_HD
    v = v_ref[0]

    def sa_step(i, acc):
        return acc + kk_ref[0, pl.ds(i, 1), :] * s_ref[i]

    sa = lax.fori_loop(0, N, sa_step, jnp.zeros_like(v), unroll=8)

    def upd_step(i, y):
        row = lambda ref: ref[0, pl.ds(i, 1), :]
        s_new = row(w_ref) * s_ref[i] - row(ap_ref) * sa + row(k_ref) * v
        so_ref[i] = s_new
        return y + row(r_ref) * s_new

    y_ref[0] = lax.fori_loop(0, N, upd_step, jnp.zeros_like(v), unroll=8)


def _rwkv_step(wdec, ap, k, kk, r, v, s_all, layer):
    b, w = v.shape
    heads = lambda x: x.reshape(b, RW_HEADS, RW_HD).transpose(1, 2, 0)
    vec = pl.BlockSpec((1, RW_HD, b), lambda h: (h, 0, 0))
    y, s_new = pl.pallas_call(
        _rwkv_step_body,
        grid=(RW_HEADS,),
        in_specs=[vec] * 6 + [pl.BlockSpec((None, None, RW_HD, RW_HD, b), lambda h: (layer, h, 0, 0, 0))],
        out_specs=[vec, pl.BlockSpec((None, RW_HD, RW_HD, b), lambda h: (h, 0, 0, 0))],
        out_shape=[jax.ShapeDtypeStruct((RW_HEADS, RW_HD, b), F32),
                   jax.ShapeDtypeStruct((RW_HEADS, RW_HD, RW_HD, b), F32)],
        compiler_params=_cparams(("parallel",)),
        name="rwkv_step",
    )(heads(wdec), heads(ap), heads(k), heads(kk), heads(r), heads(v), s_all)
    return y.transpose(2, 0, 1).reshape(b, w), s_new


def _ret_step_body(kc_ref, q_ref, k_ref, v_ref, esel_ref, seg_ref, s_ref, y_ref, so_ref):
    DK, DV = RET_QK, RET_V
    q = q_ref[...]
    qk_v = _exact_rhs_dot(q * k_ref[...], seg_ref[...])
    row_id = lax.broadcasted_iota(jnp.int32, (DEC_TB, 1), 0)
    for h in range(RET_HEADS):
        gamma = math.exp(_ret_log_gamma(h))
        ks = slice(h * DK, (h + 1) * DK)
        vs = slice(h * DV, (h + 1) * DV)
        k_rep = _exact_rhs_dot(kc_ref[0, ks, :], esel_ref[...])
        q_h = q[:, ks].astype(BF16)
        v_h = v_ref[:, vs]
        qs = jnp.zeros((DEC_TB, DV), F32)
        for j in range(DEC_TB):
            s = s_ref[j, h]
            so_ref[j, h] = gamma * s + k_rep[:, j * LANES:(j + 1) * LANES] * v_h[j:j + 1, :]
            qs = jnp.where(row_id == j, _dot(q_h, s.astype(BF16)), qs)
        y_ref[:, vs] = qk_v[:, vs] * v_h + gamma * qs


def _ret_step(q, k, v, s_all, layer):
    b = q.shape[0]
    nqk = RET_HEADS * RET_QK
    cols = pl.BlockSpec((1, nqk, 2 * DEC_TB), lambda i: (i, 0, 0))
    qk_rows = pl.BlockSpec((DEC_TB, nqk), lambda i: (i, 0))
    rows = pl.BlockSpec((DEC_TB, RET_W), lambda i: (i, 0))
    esel = pl.BlockSpec((2 * DEC_TB, DEC_TB * LANES), lambda i: (0, 0))
    seg = (jnp.arange(nqk)[:, None] // RET_QK == jnp.arange(RET_W)[None, :] // RET_V).astype(BF16)
    st = pl.BlockSpec((DEC_TB, RET_HEADS, RET_QK, RET_V), lambda i: (i, 0, 0, 0))
    st_in = pl.BlockSpec((None, DEC_TB, RET_HEADS, RET_QK, RET_V), lambda i: (layer, i, 0, 0, 0))
    return pl.pallas_call(
        _ret_step_body,
        grid=(b // DEC_TB,),
        in_specs=[cols, qk_rows, qk_rows, rows, esel, pl.BlockSpec((nqk, RET_W), lambda i: (0, 0)), st_in],
        out_specs=[rows, st],
        out_shape=[jax.ShapeDtypeStruct((b, RET_W), F32), jax.ShapeDtypeStruct(s_all.shape[1:], F32)],
        compiler_params=_cparams(("parallel",)),
        name="ret_step",
    )(_to_cols(k), q, k, v, _col_selector(), seg, s_all)


def _ml_step_body(kc_ref, q_ref, k_ref, v_ref, ig_ref, lf_ref, esel_ref, hsel_ref, hsel_k_ref, seg_ref,
                  c_ref, n_ref, m_ref, h_ref, co_ref, no_ref, mo_ref):
    DK, DV, H = ML_QK, ML_V, ML_HEADS
    ig, lf, m_prev = ig_ref[...], lf_ref[...], m_ref[...]
    m_new = jnp.maximum(lf + m_prev, ig)
    dec = jnp.exp(lf + m_prev - m_new)
    wgt = jnp.exp(ig - m_new)
    mo_ref[...] = m_new
    hsel = hsel_ref[...]
    dec_v = _exact_rhs_dot(dec, hsel)
    wgt_v = _exact_rhs_dot(wgt, hsel)
    floor_v = _exact_rhs_dot(jnp.exp(-m_new), hsel)
    q, k, n_prev = q_ref[...], k_ref[...], n_ref[...]
    no_ref[...] = _exact_rhs_dot(dec, hsel_k_ref[...]) * n_prev + k * _exact_rhs_dot(wgt, hsel_k_ref[...])
    s_v = _exact_rhs_dot(q * k, seg_ref[...]) * wgt_v
    den_v = s_v + dec_v * _exact_rhs_dot(q * n_prev, seg_ref[...])
    row_id = lax.broadcasted_iota(jnp.int32, (DEC_TB, 1), 0)
    for h in range(H):
        ks = slice(h * DK, (h + 1) * DK)
        vs = slice(h * DV, (h + 1) * DV)
        k_rep = _exact_rhs_dot(kc_ref[0, ks, :], esel_ref[...])
        q_h = q[:, ks].astype(BF16)
        v_h = v_ref[:, vs]
        kv_scale = wgt_v[:, vs] * v_h
        qc = jnp.zeros((DEC_TB, DV), F32)
        for j in range(DEC_TB):
            c_prev = c_ref[j, h]
            co_ref[j, h] = dec_v[j:j + 1, vs] * c_prev + k_rep[:, j * LANES:(j + 1) * LANES] * kv_scale[j:j + 1, :]
            qc = jnp.where(row_id == j, _dot(q_h, c_prev.astype(BF16)), qc)
        num = s_v[:, vs] * v_h + dec_v[:, vs] * qc
        h_ref[:, vs] = num / jnp.maximum(jnp.abs(den_v[:, vs]), floor_v[:, vs])


def _ml_step(q, k, v, ig, lf, c_all, layer, n0, m0):
    b = q.shape[0]
    nqk = ML_HEADS * ML_QK
    pad_h = lambda x: jnp.pad(x, ((0, 0), (0, LANES - ML_HEADS)))
    head = jnp.arange(LANES)[:, None]
    hsel = (head == jnp.arange(ML_W)[None, :] // ML_V).astype(BF16)
    hsel_k = (head == jnp.arange(nqk)[None, :] // ML_QK).astype(BF16)
    seg = (jnp.arange(nqk)[:, None] // ML_QK == jnp.arange(ML_W)[None, :] // ML_V).astype(BF16)
    const = lambda r, c: pl.BlockSpec((r, c), lambda i: (0, 0))
    cols = pl.BlockSpec((1, nqk, 2 * DEC_TB), lambda i: (i, 0, 0))
    qk_rows = pl.BlockSpec((DEC_TB, nqk), lambda i: (i, 0))
    rows = pl.BlockSpec((DEC_TB, ML_W), lambda i: (i, 0))
    sc = pl.BlockSpec((DEC_TB, LANES), lambda i: (i, 0))
    cs = pl.BlockSpec((DEC_TB, ML_HEADS, ML_QK, ML_V), lambda i: (i, 0, 0, 0))
    cs_in = pl.BlockSpec((None, DEC_TB, ML_HEADS, ML_QK, ML_V), lambda i: (layer, i, 0, 0, 0))
    hm, c_new, n_new, m_new = pl.pallas_call(
        _ml_step_body,
        grid=(b // DEC_TB,),
        in_specs=[cols, qk_rows, qk_rows, rows, sc, sc, const(2 * DEC_TB, DEC_TB * LANES), const(LANES, ML_W),
                  const(LANES, nqk), const(nqk, ML_W), cs_in, qk_rows, sc],
        out_specs=[rows, cs, qk_rows, sc],
        out_shape=[jax.ShapeDtypeStruct((b, ML_W), F32), jax.ShapeDtypeStruct(c_all.shape[1:], F32),
                   jax.ShapeDtypeStruct((b, nqk), F32), jax.ShapeDtypeStruct((b, LANES), F32)],
        compiler_params=_cparams(("parallel",)),
        name="ml_step",
    )(_to_cols(k), q, k, v, pad_h(ig), pad_h(lf), _col_selector(), hsel, hsel_k, seg,
      c_all, n0.reshape(b, nqk), pad_h(m0))
    return hm, c_new, n_new.reshape(n0.shape), m_new[:, :ML_HEADS]


def _heads(a, h):
    return a.reshape(a.shape[:-1] + (h, a.shape[-1] // h))


def _shift_prev(p, prev_row):
    return jnp.concatenate([prev_row[:, None, :], p[:, :-1]], axis=1)


def _rotary(x, pos):
    half = x.shape[-1] // 2
    inv = ROPE_BASE ** (-jnp.arange(half, dtype=F32) / half)
    ang = pos.astype(F32)[:, None] * inv[None, :]
    cos = jnp.cos(ang)[None, :, None, :]
    sin = jnp.sin(ang)[None, :, None, :]
    x1, x2 = x[..., :half], x[..., half:]
    return jnp.concatenate([x1 * cos - x2 * sin, x1 * sin + x2 * cos], -1)


def _small_matmul(x, w):
    lead = x.shape[:-1]
    kdim, n = w.shape
    x2 = x.reshape(-1, kdim)
    m = x2.shape[0]
    kp = -(-kdim // LANES) * LANES
    npad = -(-n // LANES) * LANES
    x2 = jnp.pad(x2.astype(BF16), ((0, 0), (0, kp - kdim)))
    w2 = jnp.pad(w.astype(BF16), ((0, kp - kdim), (0, npad - n)))
    tm = 1024 if m % 1024 == 0 else m
    out = _matmul(x2, w2, tm, npad)
    return out[:, :n].reshape(lead + (n,))


def _mix_prompt(p, pos, v_first, lp):
    o_rw, v_first, s_new = _rwkv_prompt(p, lp, v_first)
    o_ret, r_new = _ret_prompt(p, pos)
    o_ml, c_new, n_new, m_new = _ml_prompt(p, lp)
    return (o_rw, o_ret, o_ml), v_first, (s_new, r_new, c_new, n_new, m_new)


def _mix_sample(p, pos, v_first, st, lp, prev_row):
    bsz, t_len, _ = p.shape

    p_rw = p[..., :RW_P]
    mixed = p_rw + (_shift_prev(p_rw, prev_row[:, :RW_P]) - p_rw) * lp['rw_mu']
    sizes = np.cumsum([RW_W, RW_W, RW_W, RW_LORA_W, RW_LORA_A, RW_LORA_G])[:-1]
    r, k, v, xw, xa, xg = jnp.split(mixed, [int(s) for s in sizes], axis=-1)
    lora_in = [jnp.tanh(xw), xa, jax.nn.sigmoid(xg)]
    lora_w = [lp['rw_w2'], lp['rw_a2'], lp['rw_g2']]
    if v_first is not None:
        pv = p[..., P_V1:P_V1 + RW_LORA_V]
        lora_in.append(pv + (_shift_prev(pv, prev_row[:, P_V1:P_V1 + RW_LORA_V]) - pv) * lp['rw_vmu'])
        lora_w.append(lp['rw_v2'])
    lora = _small_matmul(jnp.concatenate(lora_in, -1), jax.scipy.linalg.block_diag(*lora_w))
    w = -jax.nn.softplus(-(lp['rw_w0'] + lora[..., :RW_W])) - 0.5
    a = jax.nn.sigmoid(lp['rw_a0'] + lora[..., RW_W:2 * RW_W])
    g = lora[..., 2 * RW_W:3 * RW_W]
    if v_first is None:
        v_first = v
    else:
        v = v + (v_first - v) * jax.nn.sigmoid(lp['rw_v0'] + lora[..., 3 * RW_W:])
    kk = _heads(k * lp['rw_kk'], RW_HEADS)
    kk = kk * lax.rsqrt(jnp.maximum(jnp.sum(jnp.square(kk), -1, keepdims=True), 1e-24))
    kk = kk.reshape(bsz, t_len, RW_W)
    k = k * (1.0 + (a - 1.0) * lp['rw_ka'])
    lw = -jnp.exp(w)
    y, s_new = _rwkv_step(jnp.exp(lw)[:, 0], (kk * a)[:, 0], k[:, 0], kk[:, 0], r[:, 0], v[:, 0],
                          st['rw_wkv_t'], st['layer'])
    y = _heads(y[:, None, :], RW_HEADS)
    y_mu = jnp.mean(y, -1, keepdims=True)
    y_var = jnp.mean(jnp.square(y - y_mu), -1, keepdims=True)
    y = ((y - y_mu) * lax.rsqrt(y_var + RW_GN_EPS)).reshape(bsz, t_len, RW_W)
    y = y * lp['rw_lnx_g'] + lp['rw_lnx_b']
    rh, kh, vh = (_heads(u, RW_HEADS) for u in (r, k, v))
    bonus = jnp.sum(rh * kh * _heads(lp['rw_rk'], RW_HEADS), -1, keepdims=True) * vh
    o_rw = ((y + bonus.reshape(bsz, t_len, RW_W)) * g).astype(BF16)

    nqk = RET_HEADS * RET_QK
    p_ret = p[..., P_RET:P_RET + P_MAIN]
    qr, kr, vr, gr = (p_ret[..., :nqk], p_ret[..., nqk:2 * nqk],
                      p_ret[..., 2 * nqk:2 * nqk + RET_W], p_ret[..., 2 * nqk + RET_W:])
    qh = _rotary(_heads(qr, RET_HEADS), pos).reshape(bsz, t_len, nqk)
    khr = (_rotary(_heads(kr, RET_HEADS), pos) * (RET_QK ** -0.5)).reshape(bsz, t_len, nqk)
    yr, r_new = _ret_step(qh[:, 0], khr[:, 0], vr[:, 0], st['ret_all'], st['layer'])
    yr = _heads(yr[:, None, :], RET_HEADS)
    yr = yr * lax.rsqrt(jnp.mean(jnp.square(yr), -1, keepdims=True) + NORM_EPS)
    o_ret = (jax.nn.silu(gr) * yr.reshape(bsz, t_len, RET_W)).astype(BF16)

    nqk = ML_HEADS * ML_QK
    p_ml = p[..., P_ML:P_ML + P_MAIN]
    qm, km, vm, om = (p_ml[..., :nqk], p_ml[..., nqk:2 * nqk],
                      p_ml[..., 2 * nqk:2 * nqk + ML_W], p_ml[..., 2 * nqk + ML_W:])
    im = p[..., P_GATE:P_GATE + ML_HEADS]
    fm = p[..., P_GATE + ML_HEADS:P_GATE + 2 * ML_HEADS]
    ig = ML_GATE_CAP * jnp.tanh((im + lp['ml_ib']) / ML_GATE_CAP)
    lf = jax.nn.log_sigmoid(ML_GATE_CAP * jnp.tanh((fm + lp['ml_fb']) / ML_GATE_CAP))
    km = km * (ML_QK ** -0.5)
    hm, c_new, n_new, m_new = _ml_step(qm[:, 0], km[:, 0], vm[:, 0], ig[:, 0], lf[:, 0],
                                       st['ml_c_all'], st['layer'], st['ml_n'], st['ml_m'])
    hm = _heads(hm[:, None, :], ML_HEADS)
    hm = hm * lax.rsqrt(jnp.mean(jnp.square(hm), -1, keepdims=True) + NORM_EPS)
    o_ml = (jax.nn.sigmoid(om) * (hm.reshape(bsz, t_len, ML_W) * lp['ml_norm'])).astype(BF16)

    return (o_rw, o_ret, o_ml), v_first, (s_new, r_new, c_new, n_new, m_new)


_T_V1 = P_V1 // LANES
_T_GATE = P_GATE // LANES
_T_RET = P_RET // LANES
_T_SHIFT = (P_RET - RW_P) // LANES
_T_SRC_GATE = (RW_P + RET_P + 2 * ML_HEADS * ML_QK + 2 * ML_W) // LANES


def _pack_body(w_ref, v1_ref, *o_refs):
    j = pl.program_id(0)
    row = lax.broadcasted_iota(jnp.int32, (LANES, 1), 0)
    spare = jnp.logical_and(j > _T_GATE, j < _T_RET)
    for l, o_ref in enumerate(o_refs):
        w = w_ref[:, l, :]
        gates = jnp.where(row < 2 * ML_HEADS, w, 0.0)
        out = jnp.where(j == _T_V1, v1_ref[l], jnp.where(j == _T_GATE, gates, jnp.where(spare, 0.0, w)))
        o_ref[...] = out.astype(BF16)


def _pack_w_in(w_in, rw_v1):
    depth, d, _ = w_in.shape
    w_t = w_in.transpose(2, 0, 1)
    v1_t = jnp.pad(rw_v1.transpose(0, 2, 1), ((1, 0), (0, LANES - RW_LORA_V), (0, 0)))

    def src_tile(j):
        return jnp.where(j < _T_V1, j, jnp.where(j == _T_GATE, _T_SRC_GATE, j - _T_SHIFT))

    return pl.pallas_call(
        _pack_body,
        grid=(P_PAD // LANES,),
        in_specs=[pl.BlockSpec((LANES, depth, d), lambda j: (src_tile(j), 0, 0)),
                  pl.BlockSpec((depth, LANES, d), lambda j: (0, 0, 0))],
        out_specs=[pl.BlockSpec((LANES, d), lambda j: (j, 0))] * depth,
        out_shape=[jax.ShapeDtypeStruct((P_PAD, d), BF16)] * depth,
        compiler_params=_cparams(("parallel",)),
        name="pack_w_in",
    )(w_t, v1_t)


def _token_tiles(m):
    if m % 2048 == 0:
        return 2048, 512, 512
    return m, m, m


def kernel(x_prompt, x_sample, state_rw_shift, state_rw_wkv, state_ret, state_ml_c, state_ml_n, state_ml_m,
           ln0_g, ln0_b, w_in, rw_mu, rw_w0, rw_w2, rw_a0, rw_a2, rw_g2, rw_kk, rw_ka, rw_rk,
           rw_lnx_g, rw_lnx_b, rw_v0, rw_v1, rw_vmu, rw_v2, ml_ib, ml_fb, ml_norm, w_out,
           ln1_g, ln1_b, w_gate, w_up, w_down, ln2_g, ln2_b):
    bp, tp, d = x_prompt.shape
    bs, ts, _ = x_sample.shape
    mp, ms = bp * tp, bs * ts
    pos_p = jnp.arange(tp)
    pos_s = PAST_LEN + jnp.arange(ts)
    tm_big, tm_out, tm_down = _token_tiles(mp)
    rw_wkv_t = state_rw_wkv.transpose(0, 2, 3, 4, 1)
    w_in_packed = _pack_w_in(w_in, rw_v1)
    xf_p, xb_p = _layernorm(x_prompt.reshape(mp, d), ln0_g, ln0_b, tm_out)
    xf_s, xb_s = _layernorm(x_sample.reshape(ms, d), ln0_g, ln0_b, ms)
    vf_p = vf_s = None
    outs_p, outs_s = [], []

    for l in range(DEPTH):
        lp = {
            'rw_mu': rw_mu[l], 'rw_w0': rw_w0[l], 'rw_w2': rw_w2[l], 'rw_a0': rw_a0[l], 'rw_a2': rw_a2[l],
            'rw_g2': rw_g2[l], 'rw_kk': rw_kk[l], 'rw_ka': rw_ka[l], 'rw_rk': rw_rk[l],
            'rw_lnx_g': rw_lnx_g[l], 'rw_lnx_b': rw_lnx_b[l], 'ml_ib': ml_ib[l], 'ml_fb': ml_fb[l],
            'ml_norm': ml_norm[l],
        }
        if l > 0:
            lp.update(rw_v0=rw_v0[l - 1], rw_vmu=rw_vmu[l - 1], rw_v2=rw_v2[l - 1])
        w_out_b = w_out[l].astype(BF16)
        x_side = jnp.concatenate([xb_s, state_rw_shift[l].astype(BF16)], axis=0)
        p_p, p_side = _in_proj(xb_p, x_side, w_in_packed[l], tm_big, P_TN)
        o_p, vf_p, st_p = _mix_prompt(p_p.reshape(bp, tp, P_PAD), pos_p, vf_p, lp)
        st = {'rw_wkv_t': rw_wkv_t, 'layer': l, 'ret_all': state_ret, 'ml_c_all': state_ml_c,
              'ml_n': state_ml_n[l], 'ml_m': state_ml_m[l]}
        o_s, vf_s, st_s = _mix_sample(p_side[:ms].reshape(bs, ts, P_PAD), pos_s, vf_s, st, lp, p_side[ms:])
        outs_p.append((xf_p.reshape(bp, tp, d)[:, -1],) + st_p)
        outs_s.append((xf_s.reshape(bs, ts, d)[:, -1],) + st_s)
        flat = lambda o, m: tuple(u.reshape(m, u.shape[-1]) for u in o)
        x1f_p, x1b_p, x1f_s, x1b_s = _out_proj_ln(flat(o_p, mp), xf_p, flat(o_s, ms), xf_s, w_out_b,
                                                  ln1_g[l], ln1_b[l], tm_out)
        hdn_p, hdn_s, w_down_b = _matmul_swiglu(x1b_p, x1b_s, w_gate, w_up, w_down, l, tm_big, 512)
        xf_p, xb_p, xf_s, xb_s = _matmul_res_ln(hdn_p, hdn_s, w_down_b, x1f_p, x1f_s, ln2_g[l], ln2_b[l],
                                                tm_down, DOWN_TK)

    y_p = xf_p.reshape(bp, tp, d)
    y_s = xf_s.reshape(bs, ts, d)
    sp = [jnp.stack([o[i] for o in outs_p]) for i in range(6)]
    ss = [jnp.stack([o[i] for o in outs_s]) for i in range(6)]
    ss[1] = ss[1].transpose(0, 4, 1, 2, 3)
    return (y_p, y_s, sp[0], sp[1], sp[2], sp[3], sp[4], sp[5], ss[0], ss[1], ss[2], ss[3], ss[4], ss[5])
```

```python
import functools
import math

import numpy as np
import jax
import jax.numpy as jnp
from jax import lax
from jax.experimental import pallas as pl
from jax.experimental.pallas import tpu as pltpu

F32 = jnp.float32
BF16 = jnp.bfloat16

D_MODEL = 2048
DEPTH = 2
PAST_LEN = 16384
RW_HD = 64
RW_W = D_MODEL // 4
RW_HEADS = RW_W // RW_HD
RW_LORA_W = 64
RW_LORA_A = 64
RW_LORA_V = 32
RW_LORA_G = 128
RW_P = 3 * RW_W + RW_LORA_W + RW_LORA_A + RW_LORA_G
RW_GN_EPS = 64e-5
RET_V = 128
RET_QK = 64
RET_W = 3 * D_MODEL // 8
RET_HEADS = RET_W // RET_V
RET_P = 2 * RET_HEADS * RET_QK + 2 * RET_W
ML_V = 128
ML_QK = 64
ML_W = D_MODEL - RW_W - RET_W
ML_HEADS = ML_W // ML_V
ML_P = 2 * ML_HEADS * ML_QK + 2 * ML_W + 2 * ML_HEADS
ML_GATE_CAP = 15.0
P_TOTAL = RW_P + RET_P + ML_P
D_FF = ((8 * D_MODEL + 3 * 256 - 1) // (3 * 256)) * 256
CHUNK = 128
ROPE_BASE = 10000.0
LN_EPS = 1e-5
NORM_EPS = 1e-6
ALPHA = (2 * DEPTH) ** 0.25

LANES = 128
P_V1 = RW_P
P_GATE = RW_P + LANES
P_RET = 2304
P_ML = 2 * P_RET
P_MAIN = 2304
P_PAD = 3 * P_RET
P_TN = 768
RW_CHUNK = 64
DEC_TB = 8
VMEM_LIMIT = 56 * 1024 * 1024

def _cparams(sem):
    return pltpu.CompilerParams(dimension_semantics=sem, vmem_limit_bytes=VMEM_LIMIT)


def _dot(a, b):
    return lax.dot_general(a, b, (((1,), (0,)), ((), ())), preferred_element_type=F32)


def _dot_nt(a, b):
    return lax.dot_general(a, b, (((1,), (1,)), ((), ())), preferred_element_type=F32)


def _dot_tn(a, b):
    return lax.dot_general(a, b, (((0,), (0,)), ((), ())), preferred_element_type=F32)


def _ln_rows(x, g, b):
    mu = jnp.mean(x, -1, keepdims=True)
    xc = x - mu
    var = jnp.mean(xc * xc, -1, keepdims=True)
    return xc * lax.rsqrt(var + LN_EPS) * g + b


def _ln_body(x_ref, g_ref, b_ref, of_ref, ob_ref):
    y = _ln_rows(x_ref[...], g_ref[...], b_ref[...])
    of_ref[...] = y
    ob_ref[...] = y.astype(BF16)


def _layernorm(x, g, b, tm):
    m, d = x.shape
    return pl.pallas_call(
        _ln_body,
        grid=(m // tm,),
        in_specs=[pl.BlockSpec((tm, d), lambda i: (i, 0)),
                  pl.BlockSpec((1, d), lambda i: (0, 0)),
                  pl.BlockSpec((1, d), lambda i: (0, 0))],
        out_specs=[pl.BlockSpec((tm, d), lambda i: (i, 0)),
                   pl.BlockSpec((tm, d), lambda i: (i, 0))],
        out_shape=[jax.ShapeDtypeStruct((m, d), F32), jax.ShapeDtypeStruct((m, d), BF16)],
        compiler_params=_cparams(("parallel",)),
        name="layernorm",
    )(x, g.reshape(1, d), b.reshape(1, d))


def _mm_body(x_ref, w_ref, o_ref, *, w_transposed):
    dot = _dot_nt if w_transposed else _dot
    o_ref[...] = dot(x_ref[...], w_ref[...]).astype(o_ref.dtype)


def _matmul(x, w, tm, tn, out_dtype=F32, w_transposed=False):
    m, k = x.shape
    n = w.shape[0] if w_transposed else w.shape[1]
    w_spec = (pl.BlockSpec((tn, k), lambda i, j: (j, 0)) if w_transposed
              else pl.BlockSpec((k, tn), lambda i, j: (0, j)))
    return pl.pallas_call(
        functools.partial(_mm_body, w_transposed=w_transposed),
        grid=(m // tm, n // tn),
        in_specs=[pl.BlockSpec((tm, k), lambda i, j: (i, 0)), w_spec],
        out_specs=pl.BlockSpec((tm, tn), lambda i, j: (i, j)),
        out_shape=jax.ShapeDtypeStruct((m, n), out_dtype),
        compiler_params=_cparams(("parallel", "parallel")),
        name="matmul",
    )(x, w)


def _in_proj_body(x_ref, xs_ref, w_ref, o_ref, os_ref):
    @pl.when(pl.program_id(1) == 0)
    def _():
        os_ref[...] = _dot_nt(xs_ref[...], w_ref[...])

    o_ref[...] = _dot_nt(x_ref[...], w_ref[...])


def _in_proj(x, x_side, w_t, tm, tn):
    m, k = x.shape
    ms = x_side.shape[0]
    n = w_t.shape[0]
    return pl.pallas_call(
        _in_proj_body,
        grid=(n // tn, m // tm),
        in_specs=[pl.BlockSpec((tm, k), lambda j, i: (i, 0)), pl.BlockSpec((ms, k), lambda j, i: (0, 0)),
                  pl.BlockSpec((tn, k), lambda j, i: (j, 0))],
        out_specs=[pl.BlockSpec((tm, tn), lambda j, i: (i, j)), pl.BlockSpec((ms, tn), lambda j, i: (0, j))],
        out_shape=[jax.ShapeDtypeStruct((m, n), F32), jax.ShapeDtypeStruct((ms, n), F32)],
        compiler_params=_cparams(("parallel", "arbitrary")),
        name="in_proj",
    )(x, x_side, w_t)


def _swiglu_body(x_ref, xs_ref, wg_ref, wu_ref, wd_ref, o_ref, os_ref, wdb_ref, wg_scr, wu_scr):
    def act(x):
        g = _dot(x, wg_scr[...])
        return (g * jax.nn.sigmoid(g) * _dot(x, wu_scr[...])).astype(BF16)

    wdb_ref[...] = wd_ref[...].astype(BF16)

    @pl.when(pl.program_id(1) == 0)
    def _():
        wg_scr[...] = wg_ref[...].astype(BF16)
        wu_scr[...] = wu_ref[...].astype(BF16)
        os_ref[...] = act(xs_ref[...])

    step = x_ref.shape[0] // SWIGLU_CHUNKS
    for r in range(SWIGLU_CHUNKS):
        rows = pl.ds(r * step, step)
        o_ref[rows, :] = act(x_ref[rows, :])


def _matmul_swiglu(x, x_side, wg, wu, wd, layer, tm, tn):
    m, k = x.shape
    ms = x_side.shape[0]
    n = wg.shape[2]
    n_i = m // tm
    slab = n // ((n // tn) * n_i)
    assert slab * (n // tn) * n_i == n and slab % 16 == 0, (n, tn, n_i)
    w_spec = pl.BlockSpec((None, k, tn), lambda j, i: (layer, 0, j))
    return pl.pallas_call(
        _swiglu_body,
        grid=(n // tn, n_i),
        in_specs=[pl.BlockSpec((tm, k), lambda j, i: (i, 0)), pl.BlockSpec((ms, k), lambda j, i: (0, 0)),
                  w_spec, w_spec, pl.BlockSpec((None, slab, k), lambda j, i: (layer, j * n_i + i, 0))],
        out_specs=[pl.BlockSpec((tm, tn), lambda j, i: (i, j)), pl.BlockSpec((ms, tn), lambda j, i: (0, j)),
                   pl.BlockSpec((slab, k), lambda j, i: (j * n_i + i, 0))],
        out_shape=[jax.ShapeDtypeStruct((m, n), BF16), jax.ShapeDtypeStruct((ms, n), BF16),
                   jax.ShapeDtypeStruct((n, k), BF16)],
        scratch_shapes=[pltpu.VMEM((k, tn), BF16), pltpu.VMEM((k, tn), BF16)],
        compiler_params=_cparams(("parallel", "arbitrary")),
        name="matmul_swiglu",
    )(x, x_side, wg, wu, wd)


SWIGLU_CHUNKS = 2
DOWN_VMEM_LIMIT = 60 * 1024 * 1024
DOWN_TK = D_FF
LN_ROWS = 256
OUT_ROWS = 256


def _res_ln_store(acc_ref, res_ref, g_ref, b_ref, of_ref, ob_ref, n_rows):
    step = min(LN_ROWS, n_rows)
    for r in range(0, n_rows, step):
        rows = pl.ds(r, step)
        y = _ln_rows(ALPHA * res_ref[rows, :] + acc_ref[rows, :], g_ref[...], b_ref[...])
        of_ref[rows, :] = y
        ob_ref[rows, :] = y.astype(BF16)


def _mm_res_ln_body(x_ref, xs_ref, w_ref, res_ref, ress_ref, g_ref, b_ref, of_ref, ob_ref, ofs_ref, obs_ref, *, nk, tm, ms):
    i = pl.program_id(0)
    kk = pl.program_id(1)

    def accumulate(a_ref, acc_ref, r_ref, h_ref, n_rows):
        if nk == 1:
            acc_ref[...] = _dot(a_ref[...], w_ref[...])
            _res_ln_store(acc_ref, r_ref, g_ref, b_ref, acc_ref, h_ref, n_rows)
            return

        @pl.when(kk == 0)
        def _():
            acc_ref[...] = jnp.zeros_like(acc_ref)

        acc_ref[...] += _dot(a_ref[...], w_ref[...])

        @pl.when(kk == nk - 1)
        def _():
            _res_ln_store(acc_ref, r_ref, g_ref, b_ref, acc_ref, h_ref, n_rows)

    accumulate(x_ref, of_ref, res_ref, ob_ref, tm)

    @pl.when(i == 0)
    def _():
        accumulate(xs_ref, ofs_ref, ress_ref, obs_ref, ms)


def _matmul_res_ln(x, x_side, w, res, res_side, g, b, tm, tk):
    m, k = x.shape
    ms = x_side.shape[0]
    n = w.shape[1]
    nk = k // tk
    const = lambda r: pl.BlockSpec((r, n), lambda i, j: (0, 0))
    main = pl.BlockSpec((tm, n), lambda i, j: (i, 0))
    single = dict(pipeline_mode=pl.Buffered(1)) if nk == 1 else {}
    return pl.pallas_call(
        functools.partial(_mm_res_ln_body, nk=nk, tm=tm, ms=ms),
        grid=(m // tm, nk),
        in_specs=[pl.BlockSpec((tm, tk), lambda i, j: (i, j)), pl.BlockSpec((ms, tk), lambda i, j: (0, j)),
                  pl.BlockSpec((tk, n), lambda i, j: (j, 0), **single),
                  pl.BlockSpec((tm, n), lambda i, j: (i, 0), **single), const(ms), const(1), const(1)],
        out_specs=[main, main, const(ms), const(ms)],
        out_shape=[jax.ShapeDtypeStruct((m, n), F32), jax.ShapeDtypeStruct((m, n), BF16),
                   jax.ShapeDtypeStruct((ms, n), F32), jax.ShapeDtypeStruct((ms, n), BF16)],
        compiler_params=pltpu.CompilerParams(dimension_semantics=("arbitrary", "arbitrary"),
                                             vmem_limit_bytes=DOWN_VMEM_LIMIT),
        name="matmul_res_ln",
    )(x, x_side, w, res, res_side, g.reshape(1, n), b.reshape(1, n))


def _out_proj_ln_body(o_rw_ref, o_ret_ref, o_ml_ref, res_ref, s_rw_ref, s_ret_ref, s_ml_ref, ress_ref,
                      w_ref, g_ref, b_ref, of_ref, ob_ref, ofs_ref, obs_ref, *, tm, ms):
    def project(rw_ref, ret_ref, ml_ref, r_ref, f_ref, h_ref, n_rows):
        step = min(OUT_ROWS, n_rows)
        for r in range(0, n_rows, step):
            rows = pl.ds(r, step)
            mix = (_dot(rw_ref[rows, :], w_ref[0:RW_W, :])
                   + _dot(ret_ref[rows, :], w_ref[RW_W:RW_W + RET_W, :])
                   + _dot(ml_ref[rows, :], w_ref[RW_W + RET_W:, :]))
            y = _ln_rows(ALPHA * r_ref[rows, :] + mix, g_ref[...], b_ref[...])
            f_ref[rows, :] = y
            h_ref[rows, :] = y.astype(BF16)

    @pl.when(pl.program_id(0) == 0)
    def _():
        project(s_rw_ref, s_ret_ref, s_ml_ref, ress_ref, ofs_ref, obs_ref, ms)

    project(o_rw_ref, o_ret_ref, o_ml_ref, res_ref, of_ref, ob_ref, tm)


def _out_proj_ln(o, res, o_side, res_side, w, g, b, tm):
    m = res.shape[0]
    ms = res_side.shape[0]
    n = w.shape[1]
    rows = lambda width: pl.BlockSpec((tm, width), lambda i: (i, 0))
    const = lambda r, c: pl.BlockSpec((r, c), lambda i: (0, 0))
    widths = (RW_W, RET_W, ML_W)
    return pl.pallas_call(
        functools.partial(_out_proj_ln_body, tm=tm, ms=ms),
        grid=(m // tm,),
        in_specs=([rows(c) for c in widths] + [rows(n)] + [const(ms, c) for c in widths] + [const(ms, n)]
                  + [const(D_MODEL, n), const(1, n), const(1, n)]),
        out_specs=[rows(n), rows(n), const(ms, n), const(ms, n)],
        out_shape=[jax.ShapeDtypeStruct((m, n), F32), jax.ShapeDtypeStruct((m, n), BF16),
                   jax.ShapeDtypeStruct((ms, n), F32), jax.ShapeDtypeStruct((ms, n), BF16)],
        compiler_params=_cparams(("arbitrary",)),
        name="out_proj_ln",
    )(*o, res, *o_side, res_side, w, g.reshape(1, n), b.reshape(1, n))


RW_TB = 512
RW_GH = 4
RW_GW = RW_GH * RW_HD
RW_AHEAD = 8
RW_VEC_ROWS = 8


def _split3(x):
    hi = x.astype(BF16)
    r1 = x - hi.astype(F32)
    mid = r1.astype(BF16)
    lo = (r1 - mid.astype(F32)).astype(BF16)
    return hi, mid, lo


def _mm(a, b, dims):
    return lax.dot_general(a.astype(BF16), b.astype(BF16), (dims, ((), ())), preferred_element_type=F32)


_NN = ((1,), (0,))
_NT = ((1,), (1,))
_TN = ((0,), (0,))


def _exact_lhs_dot(a_bf16, b):
    hi, mid, lo = _split3(b)
    dg = lambda y: lax.dot_general(a_bf16, y, (_NN, ((), ())), preferred_element_type=F32)
    return dg(hi) + (dg(mid) + dg(lo))


def _exact_rhs_dot(a, b_bf16):
    hi, mid, lo = _split3(a)
    dg = lambda x: lax.dot_general(x, b_bf16, (_NN, ((), ())), preferred_element_type=F32)
    return dg(hi) + (dg(mid) + dg(lo))


def _seg_sum(a, seg_bf16):
    hi = a.astype(BF16)
    lo = (a - hi.astype(F32)).astype(BF16)
    gw = seg_bf16.shape[0]
    dg = lambda x: lax.dot_general(x, seg_bf16, (_NN, ((), ())), preferred_element_type=F32)
    groups = [dg(hi[:, c:c + gw]) + dg(lo[:, c:c + gw]) for c in range(0, a.shape[1], gw)]
    return groups[0] if len(groups) == 1 else jnp.concatenate(groups, axis=1)


def _rw_scan(r, lw, k, v, kk, a, st_scr, y_scr):
    L, TB, G = RW_CHUNK, RW_TB, RW_GW
    row = lax.broadcasted_iota(jnp.int32, (L, G), 0)
    col = lax.broadcasted_iota(jnp.int32, (L, G), 1) & (L - 1)
    strict, lower, eye = row > col, row >= col, (row == col).astype(F32)
    rg = lax.broadcasted_iota(jnp.int32, (G, G), 0) // RW_HD
    cg = lax.broadcasted_iota(jnp.int32, (G, G), 1) // RW_HD
    mask_bd = rg == cg
    rt = lax.broadcasted_iota(jnp.int32, (2 * L, 2 * L), 0)
    ct = lax.broadcasted_iota(jnp.int32, (2 * L, 2 * L), 1)
    tri = jnp.logical_and(rt >= ct, rt // L == ct // L).astype(BF16)
    bd = lambda x: jnp.where(mask_bd, jnp.concatenate([x.astype(BF16)] * RW_GH, axis=0), 0.0)
    cut = lambda x, b: x[b[0] * L:(b[0] + 1) * L, b[1] * G:(b[1] + 1) * G]

    cum = jnp.concatenate([_exact_lhs_dot(tri, lw[i:i + 2 * L, :]) for i in range(0, TB, 2 * L)], axis=0)
    e_neg = jnp.exp(-cum)
    ap = kk * a
    ap_h = ap * e_neg
    k_h = k * e_neg
    kk_t = kk * jnp.exp(cum - lw)
    r_t = r * jnp.exp(cum)

    lhs, n_m, m_a, m_kr, inv, mv, vk, a_end, decay = ({} for _ in range(9))
    n_groups = RW_HEADS // RW_GH

    def prep(chunks):
        blk = [(s, g) for s in chunks for g in range(n_groups)]
        for b in blk:
            lhs[b] = jnp.concatenate([cut(kk_t, b), cut(r_t, b)], axis=0)
            sc_a = _mm(lhs[b], bd(cut(ap_h, b)), _NT)
            sc_k = _mm(lhs[b], bd(cut(k_h, b)), _NT)
            n_m[b] = jnp.where(strict, sc_a[:L], 0.0)
            m_a[b] = jnp.where(lower, sc_a[L:], 0.0)
            m_kr[b] = jnp.concatenate([jnp.where(strict, sc_k[:L], 0.0), jnp.where(lower, sc_k[L:], 0.0)], axis=0)
        pw = {}
        for b in blk:
            inv[b] = eye - n_m[b]
            pw[b] = _mm(n_m[b], bd(n_m[b]), _NN)
        n_iter = int(math.log2(L)) - 1
        for j in range(n_iter):
            last = j == n_iter - 1
            for b in blk:
                lhs_j = inv[b] if last else jnp.concatenate([inv[b], pw[b]], axis=0)
                prod = _mm(lhs_j, bd(pw[b]), _NN)
                inv[b] = inv[b] + prod[:L]
                if not last:
                    pw[b] = prod[L:]
        for b in blk:
            s, g = b
            tot = cum[(s + 1) * L - 1:(s + 1) * L, g * G:(g + 1) * G]
            e_end = jnp.exp(tot - cut(cum, b))
            mv[b] = _mm(m_kr[b], bd(cut(v, b)), _NN)
            vk[b] = _mm(cut(v, b), cut(k, b) * e_end, _TN)
            a_end[b] = cut(ap, b) * e_end
            decay[b] = jnp.exp(tot)

    def apply(s):
        for g in range(n_groups):
            b = (s, g)
            st = st_scr[g]
            s_terms = _mm(lhs[b], st, _NT)
            u = _mm(inv[b], bd(s_terms[:L] + mv[b][:L]), _NN)
            y_scr[s * L:(s + 1) * L, g * G:(g + 1) * G] = s_terms[L:] + mv[b][L:] - _mm(m_a[b], bd(u), _NN)
            st_scr[g] = jnp.where(mask_bd, st * decay[b] + vk[b] - _mm(u, a_end[b], _TN), 0.0)

    n_chunks = TB // L
    prep(range(min(RW_AHEAD, n_chunks)))
    for s in range(n_chunks):
        apply(s)
        if s + RW_AHEAD < n_chunks:
            prep([s + RW_AHEAD])


def _softplus(z):
    return jnp.maximum(z, 0.0) + jnp.log(1.0 + jnp.exp(-jnp.abs(z)))


def _rwkv_fused_body(*refs, nc, has_vres):
    if has_vres:
        (p_ref, pv_ref, vf_ref, mu_ref, vec_ref, wa_ref, g2_ref, seg_ref, vmu_ref, v2_ref,
         o_ref, sf_ref, st_scr, prev_scr, y_scr, prevv_scr) = refs
    else:
        (p_ref, mu_ref, vec_ref, wa_ref, g2_ref, seg_ref,
         o_ref, vfo_ref, sf_ref, st_scr, prev_scr, y_scr) = refs
    TB, W, L = RW_TB, RW_W, RW_CHUNK
    c = pl.program_id(1)

    @pl.when(c == 0)
    def _():
        st_scr[...] = jnp.zeros_like(st_scr)
        prev_scr[...] = jnp.zeros_like(prev_scr)
        if has_vres:
            prevv_scr[...] = jnp.zeros_like(prevv_scr)

    first_row = lax.broadcasted_iota(jnp.int32, (TB, 1), 0) == 0

    def shift_mix(x, carry_ref, mu):
        prev = jnp.where(first_row, carry_ref[...], pltpu.roll(x, 1, 0))
        carry_ref[...] = x[TB - 1:TB, :]
        return x + (prev - x) * mu

    mixed = shift_mix(p_ref[0], prev_scr, mu_ref[...])
    r = mixed[:, 0:W]
    k = mixed[:, W:2 * W]
    v = mixed[:, 2 * W:3 * W]
    xwa = mixed[:, 3 * W:3 * W + LANES]
    xg = mixed[:, 3 * W + LANES:3 * W + 2 * LANES]
    vec = vec_ref[...]
    w0, a0, kk_s, ka, rk, lnx_g, lnx_b, v0 = (vec[i:i + 1, :] for i in range(RW_VEC_ROWS))
    seg = seg_ref[...]
    wa = wa_ref[...]
    w_lora = _dot(jnp.tanh(xwa).astype(BF16), wa[:, 0:W])
    a_lora = _dot(xwa.astype(BF16), wa[:, W:2 * W])
    lw = -jnp.exp(-_softplus(-(w0 + w_lora)) - 0.5)
    a = jax.nn.sigmoid(a0 + a_lora)
    g = _dot(jax.nn.sigmoid(xg).astype(BF16), g2_ref[...])
    if has_vres:
        xv = shift_mix(pv_ref[0], prevv_scr, vmu_ref[...])
        v = v + (vf_ref[0] - v) * jax.nn.sigmoid(v0 + _dot(xv.astype(BF16), v2_ref[...]))
    else:
        vfo_ref[0] = v
    kk = k * kk_s
    kk = kk * lax.rsqrt(jnp.maximum(_seg_sum(kk * kk, seg), 1e-24))
    k = k * (1.0 + (a - 1.0) * ka)

    _rw_scan(r, lw, k, v, kk, a, st_scr, y_scr)

    y = y_scr[...]
    inv_n = 1.0 / RW_HD
    y_mu = _seg_sum(y, seg) * inv_n
    yc = y - y_mu
    y_var = _seg_sum(yc * yc, seg) * inv_n
    y = yc * lax.rsqrt(y_var + RW_GN_EPS) * lnx_g + lnx_b
    bonus = _seg_sum(r * k * rk, seg) * v
    o_ref[0] = ((y + bonus) * g).astype(BF16)

    @pl.when(c == nc - 1)
    def _():
        sf_ref[0] = st_scr[...]


def _rwkv_prompt(p3, lp, v_first):
    b, t, _ = p3.shape
    nc = t // RW_TB
    has_vres = v_first is not None
    ng = RW_HEADS // RW_GH
    zpad = jnp.zeros((RW_LORA_W, RW_W), F32)
    wa = jnp.concatenate([jnp.concatenate([lp['rw_w2'], zpad], 0), jnp.concatenate([zpad, lp['rw_a2']], 0)], 1)
    vec = jnp.stack([lp['rw_w0'], lp['rw_a0'], lp['rw_kk'], lp['rw_ka'], lp['rw_rk'], lp['rw_lnx_g'], lp['rw_lnx_b'],
                     lp['rw_v0'] if has_vres else jnp.zeros((RW_W,), F32)])
    hid = jnp.arange(RW_GW) // RW_HD
    seg = (hid[:, None] == hid[None, :]).astype(BF16)
    full = lambda shape: pl.BlockSpec(shape, lambda i, j: (0,) * len(shape))
    seq = lambda w, blk: pl.BlockSpec((1, RW_TB, w), lambda i, j: (i, j, blk))
    in_specs = [seq(RW_P, 0)]
    args = [p3]
    if has_vres:
        in_specs += [seq(LANES, P_V1 // LANES), seq(RW_W, 0)]
        args += [p3, v_first]
    in_specs += [full((1, RW_P)), full((RW_VEC_ROWS, RW_W)), full((LANES, 2 * RW_W)), full((RW_LORA_G, RW_W)),
                 full((RW_GW, RW_GW))]
    args += [lp['rw_mu'].reshape(1, RW_P), vec, wa.astype(BF16), lp['rw_g2'].astype(BF16), seg]
    if has_vres:
        in_specs += [full((1, LANES)), full((LANES, RW_W))]
        args += [jnp.pad(lp['rw_vmu'], (0, LANES - RW_LORA_V)).reshape(1, LANES),
                 jnp.pad(lp['rw_v2'], ((0, LANES - RW_LORA_V), (0, 0))).astype(BF16)]
    out_specs = [seq(RW_W, 0)]
    out_shape = [jax.ShapeDtypeStruct((b, t, RW_W), BF16)]
    if not has_vres:
        out_specs.append(seq(RW_W, 0))
        out_shape.append(jax.ShapeDtypeStruct((b, t, RW_W), F32))
    out_specs.append(pl.BlockSpec((1, ng, RW_GW, RW_GW), lambda i, j: (i, 0, 0, 0)))
    out_shape.append(jax.ShapeDtypeStruct((b, ng, RW_GW, RW_GW), F32))
    scratch = [pltpu.VMEM((ng, RW_GW, RW_GW), F32), pltpu.VMEM((1, RW_P), F32), pltpu.VMEM((RW_TB, RW_W), F32)]
    if has_vres:
        scratch.append(pltpu.VMEM((1, LANES), F32))
    outs = pl.pallas_call(
        functools.partial(_rwkv_fused_body, nc=nc, has_vres=has_vres),
        grid=(b, nc),
        in_specs=in_specs,
        out_specs=out_specs,
        out_shape=out_shape,
        scratch_shapes=scratch,
        compiler_params=_cparams(("parallel", "arbitrary")),
        name="rwkv_fused",
    )(*args)
    if has_vres:
        o, st_bd = outs
    else:
        o, v_first, st_bd = outs
    st5 = st_bd.reshape(b, ng, RW_GH, RW_HD, RW_GH, RW_HD)
    s_fin = jnp.stack([st5[:, :, h, :, h, :] for h in range(RW_GH)], axis=2)
    s_fin = s_fin.reshape(b, RW_HEADS, RW_HD, RW_HD).transpose(0, 1, 3, 2)
    return o, v_first, s_fin


def _ret_log_gamma(h):
    return math.log1p(-(2.0 ** (-5.0 - h)))


def _rotary_tables(pos, heads, dk):
    half = dk // 2
    inv = ROPE_BASE ** (-jnp.arange(half, dtype=F32) / half)
    ang = pos.astype(F32)[:, None] * inv[None, :]
    cos = jnp.tile(jnp.concatenate([jnp.cos(ang), jnp.cos(ang)], -1), (1, heads))
    sin = jnp.tile(jnp.concatenate([-jnp.sin(ang), jnp.sin(ang)], -1), (1, heads))
    lane = jnp.arange(LANES)
    perm = (lane[:, None] == (lane[None, :] ^ half)).astype(BF16)
    return cos, sin, perm


def _ret_fused_body(p_ref, cos_ref, sin_ref, perm_ref, o_ref, sf_ref, s_scr, *, nc):
    L, DK, DV, H = CHUNK, RET_QK, RET_V, RET_HEADS
    nq = H * DK
    c = pl.program_id(1)

    @pl.when(c == 0)
    def _():
        s_scr[...] = jnp.zeros_like(s_scr)

    cos = cos_ref[...]
    sin = sin_ref[...]
    perm = perm_ref[...]
    swap = lambda x: jnp.concatenate(
        [_exact_rhs_dot(x[:, c:c + LANES], perm) for c in range(0, nq, LANES)], axis=1)
    rot = lambda x: x * cos + swap(x) * sin
    q_all = rot(p_ref[0, :, 0:nq])
    k_all = rot(p_ref[0, :, nq:2 * nq]) * (DK ** -0.5)
    row = lax.broadcasted_iota(jnp.int32, (L, L), 0)
    col = lax.broadcasted_iota(jnp.int32, (L, L), 1)
    rel = (row - col).astype(F32)
    idx = lax.broadcasted_iota(jnp.int32, (L, 1), 0).astype(F32)
    vs = lambda h: p_ref[0, :, 2 * nq + h * DV:2 * nq + (h + 1) * DV].astype(BF16)
    hd = [dict() for _ in range(H)]
    for h, t in enumerate(hd):
        lg = _ret_log_gamma(h)
        q = q_all[:, h * DK:(h + 1) * DK]
        k = k_all[:, h * DK:(h + 1) * DK]
        t['s_prev'] = s_scr[h]
        t['dmask'] = jnp.where(rel >= 0, jnp.exp(jnp.maximum(rel, 0.0) * lg), 0.0)
        t['qk'] = _dot_nt(q.astype(BF16), k.astype(BF16))
        q_dec = q * jnp.exp((idx + 1.0) * lg)
        t['qs'] = _dot(q_dec.astype(BF16), t['s_prev'].astype(BF16))
        k_end = k * jnp.exp((L - 1.0 - idx) * lg)
        t['kv'] = _dot_tn(k_end.astype(BF16), vs(h))
    for h, t in enumerate(hd):
        y = _dot((t['qk'] * t['dmask']).astype(BF16), vs(h)) + t['qs']
        y = y * lax.rsqrt(jnp.mean(y * y, -1, keepdims=True) + NORM_EPS)
        gate = p_ref[0, :, 2 * nq + RET_W + h * DV:2 * nq + RET_W + (h + 1) * DV]
        o_ref[0, :, h * DV:(h + 1) * DV] = (gate * jax.nn.sigmoid(gate) * y).astype(BF16)
    for h, t in enumerate(hd):
        s_scr[h] = math.exp(L * _ret_log_gamma(h)) * t['s_prev'] + t['kv']

    @pl.when(c == nc - 1)
    def _():
        sf_ref[0] = s_scr[...]


def _ret_prompt(p3, pos):
    b, t, _ = p3.shape
    L = CHUNK
    nc = t // L
    nq = RET_HEADS * RET_QK
    cos, sin, perm = _rotary_tables(pos, RET_HEADS, RET_QK)
    tab = pl.BlockSpec((L, nq), lambda i, j: (j, 0))
    st = pl.BlockSpec((1, RET_HEADS, RET_QK, RET_V), lambda i, j: (i, 0, 0, 0))
    return pl.pallas_call(
        functools.partial(_ret_fused_body, nc=nc),
        grid=(b, nc),
        in_specs=[pl.BlockSpec((1, L, P_MAIN), lambda i, j: (i, j, P_RET // P_MAIN)), tab, tab,
                  pl.BlockSpec((LANES, LANES), lambda i, j: (0, 0))],
        out_specs=[pl.BlockSpec((1, L, RET_W), lambda i, j: (i, j, 0)), st],
        out_shape=[jax.ShapeDtypeStruct((b, t, RET_W), BF16),
                   jax.ShapeDtypeStruct((b, RET_HEADS, RET_QK, RET_V), F32)],
        scratch_shapes=[pltpu.VMEM((RET_HEADS, RET_QK, RET_V), F32)],
        compiler_params=_cparams(("parallel", "arbitrary")),
        name="ret_fused",
    )(p3, cos, sin, perm)


ML_HPAD = 8
ML_SEL_ROWS = 16


def _ml_fused_body(p_ref, gate_ref, bias_ref, norm_ref, sel_ref, o_ref, cf_ref, nf_ref, mf_ref,
                   c_scr, n_scr, m_scr, *, nc):
    L, DK, DV, H = CHUNK, ML_QK, ML_V, ML_HEADS
    nq = H * DK
    ci = pl.program_id(1)

    @pl.when(ci == 0)
    def _():
        c_scr[...] = jnp.zeros_like(c_scr)
        n_scr[...] = jnp.zeros_like(n_scr)
        m_scr[...] = jnp.zeros_like(m_scr)

    row = lax.broadcasted_iota(jnp.int32, (L, L), 0)
    col = lax.broadcasted_iota(jnp.int32, (L, L), 1)
    causal = row >= col
    tri = causal.astype(BF16)
    capped = ML_GATE_CAP * jnp.tanh((gate_ref[0] + bias_ref[...]) * (1.0 / ML_GATE_CAP))
    lane = lax.broadcasted_iota(jnp.int32, (L, LANES), 1)
    g = jnp.where(lane < H, capped, jnp.where(lane < 2 * H, -_softplus(-capped), 0.0))
    g_rep = _exact_rhs_dot(g[:, :ML_SEL_ROWS], sel_ref[...])
    b_rep_all = _exact_lhs_dot(tri, g_rep[:, H * LANES:])
    g_t = g.T
    cum_t = _exact_lhs_dot(tri, g).T
    cm_all = g_rep[:, :H * LANES] - b_rep_all
    row_id = lax.broadcasted_iota(jnp.int32, (L, 1), 0)
    shift = 1
    while shift < L:
        cm_all = jnp.maximum(cm_all, jnp.where(row_id >= shift, pltpu.roll(cm_all, shift, 0), -jnp.inf))
        shift *= 2
    ones = jnp.ones((L, LANES), BF16)
    mean_w = jnp.full((DV, LANES), 1.0 / DV, BF16)
    m_all = m_scr[...]
    hd = [dict() for _ in range(H)]
    for h, t in enumerate(hd):
        hs = slice(h * LANES, (h + 1) * LANES)
        q = p_ref[0, :, h * DK:(h + 1) * DK].astype(BF16)
        k = p_ref[0, :, nq + h * DK:nq + (h + 1) * DK] * (DK ** -0.5)
        t['v1'] = jnp.concatenate([p_ref[0, :, 2 * nq + h * DV:2 * nq + (h + 1) * DV].astype(BF16), ones], axis=1)
        ig_rep = g_rep[:, hs]
        b_rep = b_rep_all[:, hs]
        ig_row = g_t[h:h + 1, :]
        b_row = cum_t[H + h:H + h + 1, :]
        b_tot = b_rep[L - 1:L, :]
        m_prev = m_all[h:h + 1, :]
        t['c_prev'] = c_scr[h]
        t['n_prev'] = n_scr[h]
        t['m_new'] = jnp.maximum(b_tot + m_prev, jnp.max(b_tot - b_rep + ig_rep, axis=0, keepdims=True))
        t['dec'] = jnp.exp(b_tot + m_prev - t['m_new'])
        kw = k * jnp.exp((b_tot - b_rep + ig_rep - t['m_new'])[:, :DK])
        t['kvn'] = _dot_tn(kw.astype(BF16), t['v1'])
        inter = b_rep + m_prev
        t['m_i'] = b_rep + jnp.maximum(cm_all[:, hs], m_prev)
        t['e'] = jnp.exp(jnp.where(causal, (b_rep - t['m_i']) - b_row + ig_row, -jnp.inf))
        t['sc'] = jnp.exp(inter - t['m_i'])
        t['qk'] = _dot_nt(q, k.astype(BF16))
        cn = jnp.concatenate([t['c_prev'], t['n_prev']], axis=1).astype(BF16)
        t['qcn'] = _dot(q, cn)
    for h, t in enumerate(hd):
        nd = _dot((t['qk'] * t['e']).astype(BF16), t['v1'])
        num = nd[:, :DV] + t['sc'] * t['qcn'][:, :DV]
        den = nd[:, DV:] + t['sc'] * t['qcn'][:, DV:]
        hid = num / jnp.maximum(jnp.abs(den), jnp.exp(-t['m_i']))
        hid = hid * lax.rsqrt(_seg_sum(hid * hid, mean_w) + NORM_EPS)
        og = p_ref[0, :, 2 * nq + ML_W + h * DV:2 * nq + ML_W + (h + 1) * DV]
        o_ref[0, :, h * DV:(h + 1) * DV] = (jax.nn.sigmoid(og) * (hid * norm_ref[:, h * DV:(h + 1) * DV])).astype(BF16)
    for h, t in enumerate(hd):
        c_scr[h] = t['dec'] * t['c_prev'] + t['kvn'][:, :DV]
        n_scr[h] = t['dec'] * t['n_prev'] + t['kvn'][:, DV:]
        m_scr[h:h + 1, :] = t['m_new']

    @pl.when(ci == nc - 1)
    def _():
        cf_ref[0] = c_scr[...]
        nf_ref[0] = n_scr[...]
        mf_ref[0] = m_scr[...]


def _ml_prompt(p3, lp):
    b, t, _ = p3.shape
    L = CHUNK
    nc = t // L
    bias = jnp.pad(jnp.concatenate([lp['ml_ib'], lp['ml_fb']]), (0, LANES - 2 * ML_HEADS)).reshape(1, LANES)
    n_rep = 2 * ML_HEADS * LANES
    sel = (jnp.arange(ML_SEL_ROWS)[:, None] == jnp.arange(n_rep)[None, :] // LANES).astype(BF16)
    vs = pl.BlockSpec((1, L, ML_W), lambda i, j: (i, j, 0))
    cs = pl.BlockSpec((1, ML_HEADS, ML_QK, ML_V), lambda i, j: (i, 0, 0, 0))
    ns = pl.BlockSpec((1, ML_HEADS, ML_QK, LANES), lambda i, j: (i, 0, 0, 0))
    ms = pl.BlockSpec((1, ML_HPAD, LANES), lambda i, j: (i, 0, 0))
    o, c_f, n_f, m_f = pl.pallas_call(
        functools.partial(_ml_fused_body, nc=nc),
        grid=(b, nc),
        in_specs=[pl.BlockSpec((1, L, P_MAIN), lambda i, j: (i, j, P_ML // P_MAIN)),
                  pl.BlockSpec((1, L, LANES), lambda i, j: (i, j, P_GATE // LANES)),
                  pl.BlockSpec((1, LANES), lambda i, j: (0, 0)),
                  pl.BlockSpec((1, ML_W), lambda i, j: (0, 0)),
                  pl.BlockSpec((ML_SEL_ROWS, n_rep), lambda i, j: (0, 0))],
        out_specs=[vs, cs, ns, ms],
        out_shape=[jax.ShapeDtypeStruct((b, t, ML_W), BF16),
                   jax.ShapeDtypeStruct((b, ML_HEADS, ML_QK, ML_V), F32),
                   jax.ShapeDtypeStruct((b, ML_HEADS, ML_QK, LANES), F32),
                   jax.ShapeDtypeStruct((b, ML_HPAD, LANES), F32)],
        scratch_shapes=[pltpu.VMEM((ML_HEADS, ML_QK, ML_V), F32),
                        pltpu.VMEM((ML_HEADS, ML_QK, LANES), F32),
                        pltpu.VMEM((ML_HPAD, LANES), F32)],
        compiler_params=_cparams(("parallel", "arbitrary")),
        name="ml_fused",
    )(p3, p3, bias, lp['ml_norm'].reshape(1, ML_W), sel)
    return o, c_f, n_f[..., 0], m_f[:, :ML_HEADS, 0]


def _to_cols(x):
    b, c = x.shape
    cols = x.reshape(b // DEC_TB, DEC_TB, c).transpose(0, 2, 1)
    return jnp.pad(cols, ((0, 0), (0, 0), (0, DEC_TB)))


def _col_selector():
    j = jnp.arange(2 * DEC_TB)[:, None]
    return (j == jnp.arange(DEC_TB * LANES)[None, :] // LANES).astype(BF16)


def _rwkv_step_body(w_ref, ap_ref, k_ref, kk_ref, r_ref, v_ref, s_ref, y_ref, so_ref):
    N = RW_HD
    v = v_ref[0]

    def sa_step(i, acc):
        return acc + kk_ref[0, pl.ds(i, 1), :] * s_ref[i]

    sa = lax.fori_loop(0, N, sa_step, jnp.zeros_like(v), unroll=8)

    def upd_step(i, y):
        row = lambda ref: ref[0, pl.ds(i, 1), :]
        s_new = row(w_ref) * s_ref[i] - row(ap_ref) * sa + row(k_ref) * v
        so_ref[i] = s_new
        return y + row(r_ref) * s_new

    y_ref[0] = lax.fori_loop(0, N, upd_step, jnp.zeros_like(v), unroll=8)


def _rwkv_step(wdec, ap, k, kk, r, v, s_all, layer):
    b, w = v.shape
    heads = lambda x: x.reshape(b, RW_HEADS, RW_HD).transpose(1, 2, 0)
    vec = pl.BlockSpec((1, RW_HD, b), lambda h: (h, 0, 0))
    y, s_new = pl.pallas_call(
        _rwkv_step_body,
        grid=(RW_HEADS,),
        in_specs=[vec] * 6 + [pl.BlockSpec((None, None, RW_HD, RW_HD, b), lambda h: (layer, h, 0, 0, 0))],
        out_specs=[vec, pl.BlockSpec((None, RW_HD, RW_HD, b), lambda h: (h, 0, 0, 0))],
        out_shape=[jax.ShapeDtypeStruct((RW_HEADS, RW_HD, b), F32),
                   jax.ShapeDtypeStruct((RW_HEADS, RW_HD, RW_HD, b), F32)],
        compiler_params=_cparams(("parallel",)),
        name="rwkv_step",
    )(heads(wdec), heads(ap), heads(k), heads(kk), heads(r), heads(v), s_all)
    return y.transpose(2, 0, 1).reshape(b, w), s_new


def _ret_step_body(kc_ref, q_ref, k_ref, v_ref, esel_ref, seg_ref, s_ref, y_ref, so_ref):
    DK, DV = RET_QK, RET_V
    q = q_ref[...]
    qk_v = _exact_rhs_dot(q * k_ref[...], seg_ref[...])
    row_id = lax.broadcasted_iota(jnp.int32, (DEC_TB, 1), 0)
    for h in range(RET_HEADS):
        gamma = math.exp(_ret_log_gamma(h))
        ks = slice(h * DK, (h + 1) * DK)
        vs = slice(h * DV, (h + 1) * DV)
        k_rep = _exact_rhs_dot(kc_ref[0, ks, :], esel_ref[...])
        q_h = q[:, ks].astype(BF16)
        v_h = v_ref[:, vs]
        qs = jnp.zeros((DEC_TB, DV), F32)
        for j in range(DEC_TB):
            s = s_ref[j, h]
            so_ref[j, h] = gamma * s + k_rep[:, j * LANES:(j + 1) * LANES] * v_h[j:j + 1, :]
            qs = jnp.where(row_id == j, _dot(q_h, s.astype(BF16)), qs)
        y_ref[:, vs] = qk_v[:, vs] * v_h + gamma * qs


def _ret_step(q, k, v, s_all, layer):
    b = q.shape[0]
    nqk = RET_HEADS * RET_QK
    cols = pl.BlockSpec((1, nqk, 2 * DEC_TB), lambda i: (i, 0, 0))
    qk_rows = pl.BlockSpec((DEC_TB, nqk), lambda i: (i, 0))
    rows = pl.BlockSpec((DEC_TB, RET_W), lambda i: (i, 0))
    esel = pl.BlockSpec((2 * DEC_TB, DEC_TB * LANES), lambda i: (0, 0))
    seg = (jnp.arange(nqk)[:, None] // RET_QK == jnp.arange(RET_W)[None, :] // RET_V).astype(BF16)
    st = pl.BlockSpec((DEC_TB, RET_HEADS, RET_QK, RET_V), lambda i: (i, 0, 0, 0))
    st_in = pl.BlockSpec((None, DEC_TB, RET_HEADS, RET_QK, RET_V), lambda i: (layer, i, 0, 0, 0))
    return pl.pallas_call(
        _ret_step_body,
        grid=(b // DEC_TB,),
        in_specs=[cols, qk_rows, qk_rows, rows, esel, pl.BlockSpec((nqk, RET_W), lambda i: (0, 0)), st_in],
        out_specs=[rows, st],
        out_shape=[jax.ShapeDtypeStruct((b, RET_W), F32), jax.ShapeDtypeStruct(s_all.shape[1:], F32)],
        compiler_params=_cparams(("parallel",)),
        name="ret_step",
    )(_to_cols(k), q, k, v, _col_selector(), seg, s_all)


def _ml_step_body(kc_ref, q_ref, k_ref, v_ref, ig_ref, lf_ref, esel_ref, hsel_ref, hsel_k_ref, seg_ref,
                  c_ref, n_ref, m_ref, h_ref, co_ref, no_ref, mo_ref):
    DK, DV, H = ML_QK, ML_V, ML_HEADS
    ig, lf, m_prev = ig_ref[...], lf_ref[...], m_ref[...]
    m_new = jnp.maximum(lf + m_prev, ig)
    dec = jnp.exp(lf + m_prev - m_new)
    wgt = jnp.exp(ig - m_new)
    mo_ref[...] = m_new
    hsel = hsel_ref[...]
    dec_v = _exact_rhs_dot(dec, hsel)
    wgt_v = _exact_rhs_dot(wgt, hsel)
    floor_v = _exact_rhs_dot(jnp.exp(-m_new), hsel)
    q, k, n_prev = q_ref[...], k_ref[...], n_ref[...]
    no_ref[...] = _exact_rhs_dot(dec, hsel_k_ref[...]) * n_prev + k * _exact_rhs_dot(wgt, hsel_k_ref[...])
    s_v = _exact_rhs_dot(q * k, seg_ref[...]) * wgt_v
    den_v = s_v + dec_v * _exact_rhs_dot(q * n_prev, seg_ref[...])
    row_id = lax.broadcasted_iota(jnp.int32, (DEC_TB, 1), 0)
    for h in range(H):
        ks = slice(h * DK, (h + 1) * DK)
        vs = slice(h * DV, (h + 1) * DV)
        k_rep = _exact_rhs_dot(kc_ref[0, ks, :], esel_ref[...])
        q_h = q[:, ks].astype(BF16)
        v_h = v_ref[:, vs]
        kv_scale = wgt_v[:, vs] * v_h
        qc = jnp.zeros((DEC_TB, DV), F32)
        for j in range(DEC_TB):
            c_prev = c_ref[j, h]
            co_ref[j, h] = dec_v[j:j + 1, vs] * c_prev + k_rep[:, j * LANES:(j + 1) * LANES] * kv_scale[j:j + 1, :]
            qc = jnp.where(row_id == j, _dot(q_h, c_prev.astype(BF16)), qc)
        num = s_v[:, vs] * v_h + dec_v[:, vs] * qc
        h_ref[:, vs] = num / jnp.maximum(jnp.abs(den_v[:, vs]), floor_v[:, vs])


def _ml_step(q, k, v, ig, lf, c_all, layer, n0, m0):
    b = q.shape[0]
    nqk = ML_HEADS * ML_QK
    pad_h = lambda x: jnp.pad(x, ((0, 0), (0, LANES - ML_HEADS)))
    head = jnp.arange(LANES)[:, None]
    hsel = (head == jnp.arange(ML_W)[None, :] // ML_V).astype(BF16)
    hsel_k = (head == jnp.arange(nqk)[None, :] // ML_QK).astype(BF16)
    seg = (jnp.arange(nqk)[:, None] // ML_QK == jnp.arange(ML_W)[None, :] // ML_V).astype(BF16)
    const = lambda r, c: pl.BlockSpec((r, c), lambda i: (0, 0))
    cols = pl.BlockSpec((1, nqk, 2 * DEC_TB), lambda i: (i, 0, 0))
    qk_rows = pl.BlockSpec((DEC_TB, nqk), lambda i: (i, 0))
    rows = pl.BlockSpec((DEC_TB, ML_W), lambda i: (i, 0))
    sc = pl.BlockSpec((DEC_TB, LANES), lambda i: (i, 0))
    cs = pl.BlockSpec((DEC_TB, ML_HEADS, ML_QK, ML_V), lambda i: (i, 0, 0, 0))
    cs_in = pl.BlockSpec((None, DEC_TB, ML_HEADS, ML_QK, ML_V), lambda i: (layer, i, 0, 0, 0))
    hm, c_new, n_new, m_new = pl.pallas_call(
        _ml_step_body,
        grid=(b // DEC_TB,),
        in_specs=[cols, qk_rows, qk_rows, rows, sc, sc, const(2 * DEC_TB, DEC_TB * LANES), const(LANES, ML_W),
                  const(LANES, nqk), const(nqk, ML_W), cs_in, qk_rows, sc],
        out_specs=[rows, cs, qk_rows, sc],
        out_shape=[jax.ShapeDtypeStruct((b, ML_W), F32), jax.ShapeDtypeStruct(c_all.shape[1:], F32),
                   jax.ShapeDtypeStruct((b, nqk), F32), jax.ShapeDtypeStruct((b, LANES), F32)],
        compiler_params=_cparams(("parallel",)),
        name="ml_step",
    )(_to_cols(k), q, k, v, pad_h(ig), pad_h(lf), _col_selector(), hsel, hsel_k, seg,
      c_all, n0.reshape(b, nqk), pad_h(m0))
    return hm, c_new, n_new.reshape(n0.shape), m_new[:, :ML_HEADS]


def _heads(a, h):
    return a.reshape(a.shape[:-1] + (h, a.shape[-1] // h))


def _shift_prev(p, prev_row):
    return jnp.concatenate([prev_row[:, None, :], p[:, :-1]], axis=1)


def _rotary(x, pos):
    half = x.shape[-1] // 2
    inv = ROPE_BASE ** (-jnp.arange(half, dtype=F32) / half)
    ang = pos.astype(F32)[:, None] * inv[None, :]
    cos = jnp.cos(ang)[None, :, None, :]
    sin = jnp.sin(ang)[None, :, None, :]
    x1, x2 = x[..., :half], x[..., half:]
    return jnp.concatenate([x1 * cos - x2 * sin, x1 * sin + x2 * cos], -1)


def _small_matmul(x, w):
    lead = x.shape[:-1]
    kdim, n = w.shape
    x2 = x.reshape(-1, kdim)
    m = x2.shape[0]
    kp = -(-kdim // LANES) * LANES
    npad = -(-n // LANES) * LANES
    x2 = jnp.pad(x2.astype(BF16), ((0, 0), (0, kp - kdim)))
    w2 = jnp.pad(w.astype(BF16), ((0, kp - kdim), (0, npad - n)))
    tm = 1024 if m % 1024 == 0 else m
    out = _matmul(x2, w2, tm, npad)
    return out[:, :n].reshape(lead + (n,))


def _mix_prompt(p, pos, v_first, lp):
    o_rw, v_first, s_new = _rwkv_prompt(p, lp, v_first)
    o_ret, r_new = _ret_prompt(p, pos)
    o_ml, c_new, n_new, m_new = _ml_prompt(p, lp)
    return (o_rw, o_ret, o_ml), v_first, (s_new, r_new, c_new, n_new, m_new)


def _mix_sample(p, pos, v_first, st, lp, prev_row):
    bsz, t_len, _ = p.shape

    p_rw = p[..., :RW_P]
    mixed = p_rw + (_shift_prev(p_rw, prev_row[:, :RW_P]) - p_rw) * lp['rw_mu']
    sizes = np.cumsum([RW_W, RW_W, RW_W, RW_LORA_W, RW_LORA_A, RW_LORA_G])[:-1]
    r, k, v, xw, xa, xg = jnp.split(mixed, [int(s) for s in sizes], axis=-1)
    lora_in = [jnp.tanh(xw), xa, jax.nn.sigmoid(xg)]
    lora_w = [lp['rw_w2'], lp['rw_a2'], lp['rw_g2']]
    if v_first is not None:
        pv = p[..., P_V1:P_V1 + RW_LORA_V]
        lora_in.append(pv + (_shift_prev(pv, prev_row[:, P_V1:P_V1 + RW_LORA_V]) - pv) * lp['rw_vmu'])
        lora_w.append(lp['rw_v2'])
    lora = _small_matmul(jnp.concatenate(lora_in, -1), jax.scipy.linalg.block_diag(*lora_w))
    w = -jax.nn.softplus(-(lp['rw_w0'] + lora[..., :RW_W])) - 0.5
    a = jax.nn.sigmoid(lp['rw_a0'] + lora[..., RW_W:2 * RW_W])
    g = lora[..., 2 * RW_W:3 * RW_W]
    if v_first is None:
        v_first = v
    else:
        v = v + (v_first - v) * jax.nn.sigmoid(lp['rw_v0'] + lora[..., 3 * RW_W:])
    kk = _heads(k * lp['rw_kk'], RW_HEADS)
    kk = kk * lax.rsqrt(jnp.maximum(jnp.sum(jnp.square(kk), -1, keepdims=True), 1e-24))
    kk = kk.reshape(bsz, t_len, RW_W)
    k = k * (1.0 + (a - 1.0) * lp['rw_ka'])
    lw = -jnp.exp(w)
    y, s_new = _rwkv_step(jnp.exp(lw)[:, 0], (kk * a)[:, 0], k[:, 0], kk[:, 0], r[:, 0], v[:, 0],
                          st['rw_wkv_t'], st['layer'])
    y = _heads(y[:, None, :], RW_HEADS)
    y_mu = jnp.mean(y, -1, keepdims=True)
    y_var = jnp.mean(jnp.square(y - y_mu), -1, keepdims=True)
    y = ((y - y_mu) * lax.rsqrt(y_var + RW_GN_EPS)).reshape(bsz, t_len, RW_W)
    y = y * lp['rw_lnx_g'] + lp['rw_lnx_b']
    rh, kh, vh = (_heads(u, RW_HEADS) for u in (r, k, v))
    bonus = jnp.sum(rh * kh * _heads(lp['rw_rk'], RW_HEADS), -1, keepdims=True) * vh
    o_rw = ((y + bonus.reshape(bsz, t_len, RW_W)) * g).astype(BF16)

    nqk = RET_HEADS * RET_QK
    p_ret = p[..., P_RET:P_RET + P_MAIN]
    qr, kr, vr, gr = (p_ret[..., :nqk], p_ret[..., nqk:2 * nqk],
                      p_ret[..., 2 * nqk:2 * nqk + RET_W], p_ret[..., 2 * nqk + RET_W:])
    qh = _rotary(_heads(qr, RET_HEADS), pos).reshape(bsz, t_len, nqk)
    khr = (_rotary(_heads(kr, RET_HEADS), pos) * (RET_QK ** -0.5)).reshape(bsz, t_len, nqk)
    yr, r_new = _ret_step(qh[:, 0], khr[:, 0], vr[:, 0], st['ret_all'], st['layer'])
    yr = _heads(yr[:, None, :], RET_HEADS)
    yr = yr * lax.rsqrt(jnp.mean(jnp.square(yr), -1, keepdims=True) + NORM_EPS)
    o_ret = (jax.nn.silu(gr) * yr.reshape(bsz, t_len, RET_W)).astype(BF16)

    nqk = ML_HEADS * ML_QK
    p_ml = p[..., P_ML:P_ML + P_MAIN]
    qm, km, vm, om = (p_ml[..., :nqk], p_ml[..., nqk:2 * nqk],
                      p_ml[..., 2 * nqk:2 * nqk + ML_W], p_ml[..., 2 * nqk + ML_W:])
    im = p[..., P_GATE:P_GATE + ML_HEADS]
    fm = p[..., P_GATE + ML_HEADS:P_GATE + 2 * ML_HEADS]
    ig = ML_GATE_CAP * jnp.tanh((im + lp['ml_ib']) / ML_GATE_CAP)
    lf = jax.nn.log_sigmoid(ML_GATE_CAP * jnp.tanh((fm + lp['ml_fb']) / ML_GATE_CAP))
    km = km * (ML_QK ** -0.5)
    hm, c_new, n_new, m_new = _ml_step(qm[:, 0], km[:, 0], vm[:, 0], ig[:, 0], lf[:, 0],
                                       st['ml_c_all'], st['layer'], st['ml_n'], st['ml_m'])
    hm = _heads(hm[:, None, :], ML_HEADS)
    hm = hm * lax.rsqrt(jnp.mean(jnp.square(hm), -1, keepdims=True) + NORM_EPS)
    o_ml = (jax.nn.sigmoid(om) * (hm.reshape(bsz, t_len, ML_W) * lp['ml_norm'])).astype(BF16)

    return (o_rw, o_ret, o_ml), v_first, (s_new, r_new, c_new, n_new, m_new)


_T_V1 = P_V1 // LANES
_T_GATE = P_GATE // LANES
_T_RET = P_RET // LANES
_T_SHIFT = (P_RET - RW_P) // LANES
_T_SRC_GATE = (RW_P + RET_P + 2 * ML_HEADS * ML_QK + 2 * ML_W) // LANES


def _pack_body(w_ref, v1_ref, *o_refs):
    j = pl.program_id(0)
    row = lax.broadcasted_iota(jnp.int32, (LANES, 1), 0)
    spare = jnp.logical_and(j > _T_GATE, j < _T_RET)
    for l, o_ref in enumerate(o_refs):
        w = w_ref[:, l, :]
        gates = jnp.where(row < 2 * ML_HEADS, w, 0.0)
        out = jnp.where(j == _T_V1, v1_ref[l], jnp.where(j == _T_GATE, gates, jnp.where(spare, 0.0, w)))
        o_ref[...] = out.astype(BF16)


def _pack_w_in(w_in, rw_v1):
    depth, d, _ = w_in.shape
    w_t = w_in.transpose(2, 0, 1)
    v1_t = jnp.pad(rw_v1.transpose(0, 2, 1), ((1, 0), (0, LANES - RW_LORA_V), (0, 0)))

    def src_tile(j):
        return jnp.where(j < _T_V1, j, jnp.where(j == _T_GATE, _T_SRC_GATE, j - _T_SHIFT))

    return pl.pallas_call(
        _pack_body,
        grid=(P_PAD // LANES,),
        in_specs=[pl.BlockSpec((LANES, depth, d), lambda j: (src_tile(j), 0, 0)),
                  pl.BlockSpec((depth, LANES, d), lambda j: (0, 0, 0))],
        out_specs=[pl.BlockSpec((LANES, d), lambda j: (j, 0))] * depth,
        out_shape=[jax.ShapeDtypeStruct((P_PAD, d), BF16)] * depth,
        compiler_params=_cparams(("parallel",)),
        name="pack_w_in",
    )(w_t, v1_t)


def _token_tiles(m):
    if m % 2048 == 0:
        return 2048, 512, 512
    return m, m, m


def kernel(x_prompt, x_sample, state_rw_shift, state_rw_wkv, state_ret, state_ml_c, state_ml_n, state_ml_m,
           ln0_g, ln0_b, w_in, rw_mu, rw_w0, rw_w2, rw_a0, rw_a2, rw_g2, rw_kk, rw_ka, rw_rk,
           rw_lnx_g, rw_lnx_b, rw_v0, rw_v1, rw_vmu, rw_v2, ml_ib, ml_fb, ml_norm, w_out,
           ln1_g, ln1_b, w_gate, w_up, w_down, ln2_g, ln2_b):
    bp, tp, d = x_prompt.shape
    bs, ts, _ = x_sample.shape
    mp, ms = bp * tp, bs * ts
    pos_p = jnp.arange(tp)
    pos_s = PAST_LEN + jnp.arange(ts)
    tm_big, tm_out, tm_down = _token_tiles(mp)
    rw_wkv_t = state_rw_wkv.transpose(0, 2, 3, 4, 1)
    w_in_packed = _pack_w_in(w_in, rw_v1)
    xf_p, xb_p = _layernorm(x_prompt.reshape(mp, d), ln0_g, ln0_b, tm_out)
    xf_s, xb_s = _layernorm(x_sample.reshape(ms, d), ln0_g, ln0_b, ms)
    vf_p = vf_s = None
    outs_p, outs_s = [], []

    for l in range(DEPTH):
        lp = {
            'rw_mu': rw_mu[l], 'rw_w0': rw_w0[l], 'rw_w2': rw_w2[l], 'rw_a0': rw_a0[l], 'rw_a2': rw_a2[l],
            'rw_g2': rw_g2[l], 'rw_kk': rw_kk[l], 'rw_ka': rw_ka[l], 'rw_rk': rw_rk[l],
            'rw_lnx_g': rw_lnx_g[l], 'rw_lnx_b': rw_lnx_b[l], 'ml_ib': ml_ib[l], 'ml_fb': ml_fb[l],
            'ml_norm': ml_norm[l],
        }
        if l > 0:
            lp.update(rw_v0=rw_v0[l - 1], rw_vmu=rw_vmu[l - 1], rw_v2=rw_v2[l - 1])
        w_out_b = w_out[l].astype(BF16)
        x_side = jnp.concatenate([xb_s, state_rw_shift[l].astype(BF16)], axis=0)
        p_p, p_side = _in_proj(xb_p, x_side, w_in_packed[l], tm_big, P_TN)
        o_p, vf_p, st_p = _mix_prompt(p_p.reshape(bp, tp, P_PAD), pos_p, vf_p, lp)
        st = {'rw_wkv_t': rw_wkv_t, 'layer': l, 'ret_all': state_ret, 'ml_c_all': state_ml_c,
              'ml_n': state_ml_n[l], 'ml_m': state_ml_m[l]}
        o_s, vf_s, st_s = _mix_sample(p_side[:ms].reshape(bs, ts, P_PAD), pos_s, vf_s, st, lp, p_side[ms:])
        outs_p.append((xf_p.reshape(bp, tp, d)[:, -1],) + st_p)
        outs_s.append((xf_s.reshape(bs, ts, d)[:, -1],) + st_s)
        flat = lambda o, m: tuple(u.reshape(m, u.shape[-1]) for u in o)
        x1f_p, x1b_p, x1f_s, x1b_s = _out_proj_ln(flat(o_p, mp), xf_p, flat(o_s, ms), xf_s, w_out_b,
                                                  ln1_g[l], ln1_b[l], tm_out)
        hdn_p, hdn_s, w_down_b = _matmul_swiglu(x1b_p, x1b_s, w_gate, w_up, w_down, l, tm_big, 512)
        xf_p, xb_p, xf_s, xb_s = _matmul_res_ln(hdn_p, hdn_s, w_down_b, x1f_p, x1f_s, ln2_g[l], ln2_b[l],
                                                tm_down, DOWN_TK)

    y_p = xf_p.reshape(bp, tp, d)
    y_s = xf_s.reshape(bs, ts, d)
    sp = [jnp.stack([o[i] for o in outs_p]) for i in range(6)]
    ss = [jnp.stack([o[i] for o in outs_s]) for i in range(6)]
    ss[1] = ss[1].transpose(0, 4, 1, 2, 3)
    return (y_p, y_s, sp[0], sp[1], sp[2], sp[3], sp[4], sp[5], ss[0], ss[1], ss[2], ss[3], ss[4], ss[5])
```

```python
import functools
import math

import numpy as np
import jax
import jax.numpy as jnp
from jax import lax
from jax.experimental import pallas as pl
from jax.experimental.pallas import tpu as pltpu

F32 = jnp.float32
BF16 = jnp.bfloat16

D_MODEL = 2048
DEPTH = 2
PAST_LEN = 16384
RW_HD = 64
RW_W = D_MODEL // 4
RW_HEADS = RW_W // RW_HD
RW_LORA_W = 64
RW_LORA_A = 64
RW_LORA_V = 32
RW_LORA_G = 128
RW_P = 3 * RW_W + RW_LORA_W + RW_LORA_A + RW_LORA_G
RW_GN_EPS = 64e-5
RET_V = 128
RET_QK = 64
RET_W = 3 * D_MODEL // 8
RET_HEADS = RET_W // RET_V
RET_P = 2 * RET_HEADS * RET_QK + 2 * RET_W
ML_V = 128
ML_QK = 64
ML_W = D_MODEL - RW_W - RET_W
ML_HEADS = ML_W // ML_V
ML_P = 2 * ML_HEADS * ML_QK + 2 * ML_W + 2 * ML_HEADS
ML_GATE_CAP = 15.0
P_TOTAL = RW_P + RET_P + ML_P
D_FF = ((8 * D_MODEL + 3 * 256 - 1) // (3 * 256)) * 256
CHUNK = 128
ROPE_BASE = 10000.0
LN_EPS = 1e-5
NORM_EPS = 1e-6
ALPHA = (2 * DEPTH) ** 0.25

LANES = 128
P_V1 = RW_P
P_GATE = RW_P + LANES
P_RET = 2304
P_ML = 2 * P_RET
P_MAIN = 2304
P_PAD = 3 * P_RET
P_TN = 768
RW_CHUNK = 64
DEC_TB = 8
VMEM_LIMIT = 56 * 1024 * 1024

def _cparams(sem):
    return pltpu.CompilerParams(dimension_semantics=sem, vmem_limit_bytes=VMEM_LIMIT)


def _dot(a, b):
    return lax.dot_general(a, b, (((1,), (0,)), ((), ())), preferred_element_type=F32)


def _dot_nt(a, b):
    return lax.dot_general(a, b, (((1,), (1,)), ((), ())), preferred_element_type=F32)


def _dot_tn(a, b):
    return lax.dot_general(a, b, (((0,), (0,)), ((), ())), preferred_element_type=F32)


def _ln_rows(x, g, b):
    mu = jnp.mean(x, -1, keepdims=True)
    xc = x - mu
    var = jnp.mean(xc * xc, -1, keepdims=True)
    return xc * lax.rsqrt(var + LN_EPS) * g + b


def _ln_body(x_ref, g_ref, b_ref, of_ref, ob_ref):
    y = _ln_rows(x_ref[...], g_ref[...], b_ref[...])
    of_ref[...] = y
    ob_ref[...] = y.astype(BF16)


def _layernorm(x, g, b, tm):
    m, d = x.shape
    return pl.pallas_call(
        _ln_body,
        grid=(m // tm,),
        in_specs=[pl.BlockSpec((tm, d), lambda i: (i, 0)),
                  pl.BlockSpec((1, d), lambda i: (0, 0)),
                  pl.BlockSpec((1, d), lambda i: (0, 0))],
        out_specs=[pl.BlockSpec((tm, d), lambda i: (i, 0)),
                   pl.BlockSpec((tm, d), lambda i: (i, 0))],
        out_shape=[jax.ShapeDtypeStruct((m, d), F32), jax.ShapeDtypeStruct((m, d), BF16)],
        compiler_params=_cparams(("parallel",)),
        name="layernorm",
    )(x, g.reshape(1, d), b.reshape(1, d))


def _mm_body(x_ref, w_ref, o_ref, *, w_transposed):
    dot = _dot_nt if w_transposed else _dot
    o_ref[...] = dot(x_ref[...], w_ref[...]).astype(o_ref.dtype)


def _matmul(x, w, tm, tn, out_dtype=F32, w_transposed=False):
    m, k = x.shape
    n = w.shape[0] if w_transposed else w.shape[1]
    w_spec = (pl.BlockSpec((tn, k), lambda i, j: (j, 0)) if w_transposed
              else pl.BlockSpec((k, tn), lambda i, j: (0, j)))
    return pl.pallas_call(
        functools.partial(_mm_body, w_transposed=w_transposed),
        grid=(m // tm, n // tn),
        in_specs=[pl.BlockSpec((tm, k), lambda i, j: (i, 0)), w_spec],
        out_specs=pl.BlockSpec((tm, tn), lambda i, j: (i, j)),
        out_shape=jax.ShapeDtypeStruct((m, n), out_dtype),
        compiler_params=_cparams(("parallel", "parallel")),
        name="matmul",
    )(x, w)


def _in_proj_body(x_ref, xs_ref, w_ref, o_ref, os_ref):
    @pl.when(pl.program_id(1) == 0)
    def _():
        os_ref[...] = _dot_nt(xs_ref[...], w_ref[...])

    o_ref[...] = _dot_nt(x_ref[...], w_ref[...])


def _in_proj(x, x_side, w_t, tm, tn):
    m, k = x.shape
    ms = x_side.shape[0]
    n = w_t.shape[0]
    return pl.pallas_call(
        _in_proj_body,
        grid=(n // tn, m // tm),
        in_specs=[pl.BlockSpec((tm, k), lambda j, i: (i, 0)), pl.BlockSpec((ms, k), lambda j, i: (0, 0)),
                  pl.BlockSpec((tn, k), lambda j, i: (j, 0))],
        out_specs=[pl.BlockSpec((tm, tn), lambda j, i: (i, j)), pl.BlockSpec((ms, tn), lambda j, i: (0, j))],
        out_shape=[jax.ShapeDtypeStruct((m, n), F32), jax.ShapeDtypeStruct((ms, n), F32)],
        compiler_params=_cparams(("parallel", "arbitrary")),
        name="in_proj",
    )(x, x_side, w_t)


def _swiglu_body(x_ref, xs_ref, wg_ref, wu_ref, wd_ref, o_ref, os_ref, wdb_ref, wg_scr, wu_scr):
    def act(x):
        g = _dot(x, wg_scr[...])
        return (g * jax.nn.sigmoid(g) * _dot(x, wu_scr[...])).astype(BF16)

    wdb_ref[...] = wd_ref[...].astype(BF16)

    @pl.when(pl.program_id(1) == 0)
    def _():
        wg_scr[...] = wg_ref[...].astype(BF16)
        wu_scr[...] = wu_ref[...].astype(BF16)
        os_ref[...] = act(xs_ref[...])

    step = x_ref.shape[0] // SWIGLU_CHUNKS
    for r in range(SWIGLU_CHUNKS):
        rows = pl.ds(r * step, step)
        o_ref[rows, :] = act(x_ref[rows, :])


def _matmul_swiglu(x, x_side, wg, wu, wd, layer, tm, tn):
    m, k = x.shape
    ms = x_side.shape[0]
    n = wg.shape[2]
    n_i = m // tm
    slab = n // ((n // tn) * n_i)
    assert slab * (n // tn) * n_i == n and slab % 16 == 0, (n, tn, n_i)
    w_spec = pl.BlockSpec((None, k, tn), lambda j, i: (layer, 0, j))
    return pl.pallas_call(
        _swiglu_body,
        grid=(n // tn, n_i),
        in_specs=[pl.BlockSpec((tm, k), lambda j, i: (i, 0)), pl.BlockSpec((ms, k), lambda j, i: (0, 0)),
                  w_spec, w_spec, pl.BlockSpec((None, slab, k), lambda j, i: (layer, j * n_i + i, 0))],
        out_specs=[pl.BlockSpec((tm, tn), lambda j, i: (i, j)), pl.BlockSpec((ms, tn), lambda j, i: (0, j)),
                   pl.BlockSpec((slab, k), lambda j, i: (j * n_i + i, 0))],
        out_shape=[jax.ShapeDtypeStruct((m, n), BF16), jax.ShapeDtypeStruct((ms, n), BF16),
                   jax.ShapeDtypeStruct((n, k), BF16)],
        scratch_shapes=[pltpu.VMEM((k, tn), BF16), pltpu.VMEM((k, tn), BF16)],
        compiler_params=_cparams(("parallel", "arbitrary")),
        name="matmul_swiglu",
    )(x, x_side, wg, wu, wd)


SWIGLU_CHUNKS = 2
DOWN_TK = D_FF // 2
LN_ROWS = 256
OUT_ROWS = 256


def _res_ln_store(acc_ref, res_ref, g_ref, b_ref, of_ref, ob_ref, n_rows):
    step = min(LN_ROWS, n_rows)
    for r in range(0, n_rows, step):
        rows = pl.ds(r, step)
        y = _ln_rows(ALPHA * res_ref[rows, :] + acc_ref[rows, :], g_ref[...], b_ref[...])
        of_ref[rows, :] = y
        ob_ref[rows, :] = y.astype(BF16)


def _mm_res_ln_body(x_ref, xs_ref, w_ref, res_ref, ress_ref, g_ref, b_ref, of_ref, ob_ref, ofs_ref, obs_ref, *, nk, tm, ms):
    i = pl.program_id(0)
    kk = pl.program_id(1)

    @pl.when(kk == 0)
    def _():
        of_ref[...] = jnp.zeros_like(of_ref)

    of_ref[...] += _dot(x_ref[...], w_ref[...])

    @pl.when(kk == nk - 1)
    def _():
        _res_ln_store(of_ref, res_ref, g_ref, b_ref, of_ref, ob_ref, tm)

    @pl.when(i == 0)
    def _():
        @pl.when(kk == 0)
        def _():
            ofs_ref[...] = jnp.zeros_like(ofs_ref)

        ofs_ref[...] += _dot(xs_ref[...], w_ref[...])

        @pl.when(kk == nk - 1)
        def _():
            _res_ln_store(ofs_ref, ress_ref, g_ref, b_ref, ofs_ref, obs_ref, ms)


def _matmul_res_ln(x, x_side, w, res, res_side, g, b, tm, tk):
    m, k = x.shape
    ms = x_side.shape[0]
    n = w.shape[1]
    nk = k // tk
    const = lambda r: pl.BlockSpec((r, n), lambda i, j: (0, 0))
    main = pl.BlockSpec((tm, n), lambda i, j: (i, 0))
    return pl.pallas_call(
        functools.partial(_mm_res_ln_body, nk=nk, tm=tm, ms=ms),
        grid=(m // tm, nk),
        in_specs=[pl.BlockSpec((tm, tk), lambda i, j: (i, j)), pl.BlockSpec((ms, tk), lambda i, j: (0, j)),
                  pl.BlockSpec((tk, n), lambda i, j: (j, 0)), main, const(ms), const(1), const(1)],
        out_specs=[main, main, const(ms), const(ms)],
        out_shape=[jax.ShapeDtypeStruct((m, n), F32), jax.ShapeDtypeStruct((m, n), BF16),
                   jax.ShapeDtypeStruct((ms, n), F32), jax.ShapeDtypeStruct((ms, n), BF16)],
        compiler_params=_cparams(("arbitrary", "arbitrary")),
        name="matmul_res_ln",
    )(x, x_side, w, res, res_side, g.reshape(1, n), b.reshape(1, n))


def _out_proj_ln_body(o_rw_ref, o_ret_ref, o_ml_ref, res_ref, s_rw_ref, s_ret_ref, s_ml_ref, ress_ref,
                      w_ref, g_ref, b_ref, of_ref, ob_ref, ofs_ref, obs_ref, *, tm, ms):
    def project(rw_ref, ret_ref, ml_ref, r_ref, f_ref, h_ref, n_rows):
        step = min(OUT_ROWS, n_rows)
        for r in range(0, n_rows, step):
            rows = pl.ds(r, step)
            mix = (_dot(rw_ref[rows, :], w_ref[0:RW_W, :])
                   + _dot(ret_ref[rows, :], w_ref[RW_W:RW_W + RET_W, :])
                   + _dot(ml_ref[rows, :], w_ref[RW_W + RET_W:, :]))
            y = _ln_rows(ALPHA * r_ref[rows, :] + mix, g_ref[...], b_ref[...])
            f_ref[rows, :] = y
            h_ref[rows, :] = y.astype(BF16)

    @pl.when(pl.program_id(0) == 0)
    def _():
        project(s_rw_ref, s_ret_ref, s_ml_ref, ress_ref, ofs_ref, obs_ref, ms)

    project(o_rw_ref, o_ret_ref, o_ml_ref, res_ref, of_ref, ob_ref, tm)


def _out_proj_ln(o, res, o_side, res_side, w, g, b, tm):
    m = res.shape[0]
    ms = res_side.shape[0]
    n = w.shape[1]
    rows = lambda width: pl.BlockSpec((tm, width), lambda i: (i, 0))
    const = lambda r, c: pl.BlockSpec((r, c), lambda i: (0, 0))
    widths = (RW_W, RET_W, ML_W)
    return pl.pallas_call(
        functools.partial(_out_proj_ln_body, tm=tm, ms=ms),
        grid=(m // tm,),
        in_specs=([rows(c) for c in widths] + [rows(n)] + [const(ms, c) for c in widths] + [const(ms, n)]
                  + [const(D_MODEL, n), const(1, n), const(1, n)]),
        out_specs=[rows(n), rows(n), const(ms, n), const(ms, n)],
        out_shape=[jax.ShapeDtypeStruct((m, n), F32), jax.ShapeDtypeStruct((m, n), BF16),
                   jax.ShapeDtypeStruct((ms, n), F32), jax.ShapeDtypeStruct((ms, n), BF16)],
        compiler_params=_cparams(("arbitrary",)),
        name="out_proj_ln",
    )(*o, res, *o_side, res_side, w, g.reshape(1, n), b.reshape(1, n))


RW_TB = 512
RW_GH = 4
RW_GW = RW_GH * RW_HD
RW_AHEAD = 8
RW_VEC_ROWS = 8


def _split3(x):
    hi = x.astype(BF16)
    r1 = x - hi.astype(F32)
    mid = r1.astype(BF16)
    lo = (r1 - mid.astype(F32)).astype(BF16)
    return hi, mid, lo


def _mm(a, b, dims):
    return lax.dot_general(a.astype(BF16), b.astype(BF16), (dims, ((), ())), preferred_element_type=F32)


_NN = ((1,), (0,))
_NT = ((1,), (1,))
_TN = ((0,), (0,))


def _exact_lhs_dot(a_bf16, b):
    hi, mid, lo = _split3(b)
    dg = lambda y: lax.dot_general(a_bf16, y, (_NN, ((), ())), preferred_element_type=F32)
    return dg(hi) + (dg(mid) + dg(lo))


def _exact_rhs_dot(a, b_bf16):
    hi, mid, lo = _split3(a)
    dg = lambda x: lax.dot_general(x, b_bf16, (_NN, ((), ())), preferred_element_type=F32)
    return dg(hi) + (dg(mid) + dg(lo))


def _seg_sum(a, seg_bf16):
    hi = a.astype(BF16)
    lo = (a - hi.astype(F32)).astype(BF16)
    gw = seg_bf16.shape[0]
    dg = lambda x: lax.dot_general(x, seg_bf16, (_NN, ((), ())), preferred_element_type=F32)
    groups = [dg(hi[:, c:c + gw]) + dg(lo[:, c:c + gw]) for c in range(0, a.shape[1], gw)]
    return groups[0] if len(groups) == 1 else jnp.concatenate(groups, axis=1)


def _rw_scan(r, lw, k, v, kk, a, st_scr, y_scr):
    L, TB, G = RW_CHUNK, RW_TB, RW_GW
    row = lax.broadcasted_iota(jnp.int32, (L, G), 0)
    col = lax.broadcasted_iota(jnp.int32, (L, G), 1) & (L - 1)
    strict, lower, eye = row > col, row >= col, (row == col).astype(F32)
    rg = lax.broadcasted_iota(jnp.int32, (G, G), 0) // RW_HD
    cg = lax.broadcasted_iota(jnp.int32, (G, G), 1) // RW_HD
    mask_bd = rg == cg
    rt = lax.broadcasted_iota(jnp.int32, (2 * L, 2 * L), 0)
    ct = lax.broadcasted_iota(jnp.int32, (2 * L, 2 * L), 1)
    tri = jnp.logical_and(rt >= ct, rt // L == ct // L).astype(BF16)
    bd = lambda x: jnp.where(mask_bd, jnp.concatenate([x.astype(BF16)] * RW_GH, axis=0), 0.0)
    cut = lambda x, b: x[b[0] * L:(b[0] + 1) * L, b[1] * G:(b[1] + 1) * G]

    cum = jnp.concatenate([_exact_lhs_dot(tri, lw[i:i + 2 * L, :]) for i in range(0, TB, 2 * L)], axis=0)
    e_neg = jnp.exp(-cum)
    ap = kk * a
    ap_h = ap * e_neg
    k_h = k * e_neg
    kk_t = kk * jnp.exp(cum - lw)
    r_t = r * jnp.exp(cum)

    lhs, n_m, m_a, m_kr, inv, mv, vk, a_end, decay = ({} for _ in range(9))
    n_groups = RW_HEADS // RW_GH

    def prep(chunks):
        blk = [(s, g) for s in chunks for g in range(n_groups)]
        for b in blk:
            lhs[b] = jnp.concatenate([cut(kk_t, b), cut(r_t, b)], axis=0)
            sc_a = _mm(lhs[b], bd(cut(ap_h, b)), _NT)
            sc_k = _mm(lhs[b], bd(cut(k_h, b)), _NT)
            n_m[b] = jnp.where(strict, sc_a[:L], 0.0)
            m_a[b] = jnp.where(lower, sc_a[L:], 0.0)
            m_kr[b] = jnp.concatenate([jnp.where(strict, sc_k[:L], 0.0), jnp.where(lower, sc_k[L:], 0.0)], axis=0)
        pw = {}
        for b in blk:
            inv[b] = eye - n_m[b]
            pw[b] = _mm(n_m[b], bd(n_m[b]), _NN)
        n_iter = int(math.log2(L)) - 1
        for j in range(n_iter):
            last = j == n_iter - 1
            for b in blk:
                lhs_j = inv[b] if last else jnp.concatenate([inv[b], pw[b]], axis=0)
                prod = _mm(lhs_j, bd(pw[b]), _NN)
                inv[b] = inv[b] + prod[:L]
                if not last:
                    pw[b] = prod[L:]
        for b in blk:
            s, g = b
            tot = cum[(s + 1) * L - 1:(s + 1) * L, g * G:(g + 1) * G]
            e_end = jnp.exp(tot - cut(cum, b))
            mv[b] = _mm(m_kr[b], bd(cut(v, b)), _NN)
            vk[b] = _mm(cut(v, b), cut(k, b) * e_end, _TN)
            a_end[b] = cut(ap, b) * e_end
            decay[b] = jnp.exp(tot)

    def apply(s):
        for g in range(n_groups):
            b = (s, g)
            st = st_scr[g]
            s_terms = _mm(lhs[b], st, _NT)
            u = _mm(inv[b], bd(s_terms[:L] + mv[b][:L]), _NN)
            y_scr[s * L:(s + 1) * L, g * G:(g + 1) * G] = s_terms[L:] + mv[b][L:] - _mm(m_a[b], bd(u), _NN)
            st_scr[g] = jnp.where(mask_bd, st * decay[b] + vk[b] - _mm(u, a_end[b], _TN), 0.0)

    n_chunks = TB // L
    prep(range(min(RW_AHEAD, n_chunks)))
    for s in range(n_chunks):
        apply(s)
        if s + RW_AHEAD < n_chunks:
            prep([s + RW_AHEAD])


def _softplus(z):
    return jnp.maximum(z, 0.0) + jnp.log(1.0 + jnp.exp(-jnp.abs(z)))


def _rwkv_fused_body(*refs, nc, has_vres):
    if has_vres:
        (p_ref, pv_ref, vf_ref, mu_ref, vec_ref, wa_ref, g2_ref, seg_ref, vmu_ref, v2_ref,
         o_ref, sf_ref, st_scr, prev_scr, y_scr, prevv_scr) = refs
    else:
        (p_ref, mu_ref, vec_ref, wa_ref, g2_ref, seg_ref,
         o_ref, vfo_ref, sf_ref, st_scr, prev_scr, y_scr) = refs
    TB, W, L = RW_TB, RW_W, RW_CHUNK
    c = pl.program_id(1)

    @pl.when(c == 0)
    def _():
        st_scr[...] = jnp.zeros_like(st_scr)
        prev_scr[...] = jnp.zeros_like(prev_scr)
        if has_vres:
            prevv_scr[...] = jnp.zeros_like(prevv_scr)

    first_row = lax.broadcasted_iota(jnp.int32, (TB, 1), 0) == 0

    def shift_mix(x, carry_ref, mu):
        prev = jnp.where(first_row, carry_ref[...], pltpu.roll(x, 1, 0))
        carry_ref[...] = x[TB - 1:TB, :]
        return x + (prev - x) * mu

    mixed = shift_mix(p_ref[0], prev_scr, mu_ref[...])
    r = mixed[:, 0:W]
    k = mixed[:, W:2 * W]
    v = mixed[:, 2 * W:3 * W]
    xwa = mixed[:, 3 * W:3 * W + LANES]
    xg = mixed[:, 3 * W + LANES:3 * W + 2 * LANES]
    vec = vec_ref[...]
    w0, a0, kk_s, ka, rk, lnx_g, lnx_b, v0 = (vec[i:i + 1, :] for i in range(RW_VEC_ROWS))
    seg = seg_ref[...]
    wa = wa_ref[...]
    w_lora = _dot(jnp.tanh(xwa).astype(BF16), wa[:, 0:W])
    a_lora = _dot(xwa.astype(BF16), wa[:, W:2 * W])
    lw = -jnp.exp(-_softplus(-(w0 + w_lora)) - 0.5)
    a = jax.nn.sigmoid(a0 + a_lora)
    g = _dot(jax.nn.sigmoid(xg).astype(BF16), g2_ref[...])
    if has_vres:
        xv = shift_mix(pv_ref[0], prevv_scr, vmu_ref[...])
        v = v + (vf_ref[0] - v) * jax.nn.sigmoid(v0 + _dot(xv.astype(BF16), v2_ref[...]))
    else:
        vfo_ref[0] = v
    kk = k * kk_s
    kk = kk * lax.rsqrt(jnp.maximum(_seg_sum(kk * kk, seg), 1e-24))
    k = k * (1.0 + (a - 1.0) * ka)

    _rw_scan(r, lw, k, v, kk, a, st_scr, y_scr)

    y = y_scr[...]
    inv_n = 1.0 / RW_HD
    y_mu = _seg_sum(y, seg) * inv_n
    yc = y - y_mu
    y_var = _seg_sum(yc * yc, seg) * inv_n
    y = yc * lax.rsqrt(y_var + RW_GN_EPS) * lnx_g + lnx_b
    bonus = _seg_sum(r * k * rk, seg) * v
    o_ref[0] = ((y + bonus) * g).astype(BF16)

    @pl.when(c == nc - 1)
    def _():
        sf_ref[0] = st_scr[...]


def _rwkv_prompt(p3, lp, v_first):
    b, t, _ = p3.shape
    nc = t // RW_TB
    has_vres = v_first is not None
    ng = RW_HEADS // RW_GH
    zpad = jnp.zeros((RW_LORA_W, RW_W), F32)
    wa = jnp.concatenate([jnp.concatenate([lp['rw_w2'], zpad], 0), jnp.concatenate([zpad, lp['rw_a2']], 0)], 1)
    vec = jnp.stack([lp['rw_w0'], lp['rw_a0'], lp['rw_kk'], lp['rw_ka'], lp['rw_rk'], lp['rw_lnx_g'], lp['rw_lnx_b'],
                     lp['rw_v0'] if has_vres else jnp.zeros((RW_W,), F32)])
    hid = jnp.arange(RW_GW) // RW_HD
    seg = (hid[:, None] == hid[None, :]).astype(BF16)
    full = lambda shape: pl.BlockSpec(shape, lambda i, j: (0,) * len(shape))
    seq = lambda w, blk: pl.BlockSpec((1, RW_TB, w), lambda i, j: (i, j, blk))
    in_specs = [seq(RW_P, 0)]
    args = [p3]
    if has_vres:
        in_specs += [seq(LANES, P_V1 // LANES), seq(RW_W, 0)]
        args += [p3, v_first]
    in_specs += [full((1, RW_P)), full((RW_VEC_ROWS, RW_W)), full((LANES, 2 * RW_W)), full((RW_LORA_G, RW_W)),
                 full((RW_GW, RW_GW))]
    args += [lp['rw_mu'].reshape(1, RW_P), vec, wa.astype(BF16), lp['rw_g2'].astype(BF16), seg]
    if has_vres:
        in_specs += [full((1, LANES)), full((LANES, RW_W))]
        args += [jnp.pad(lp['rw_vmu'], (0, LANES - RW_LORA_V)).reshape(1, LANES),
                 jnp.pad(lp['rw_v2'], ((0, LANES - RW_LORA_V), (0, 0))).astype(BF16)]
    out_specs = [seq(RW_W, 0)]
    out_shape = [jax.ShapeDtypeStruct((b, t, RW_W), BF16)]
    if not has_vres:
        out_specs.append(seq(RW_W, 0))
        out_shape.append(jax.ShapeDtypeStruct((b, t, RW_W), F32))
    out_specs.append(pl.BlockSpec((1, ng, RW_GW, RW_GW), lambda i, j: (i, 0, 0, 0)))
    out_shape.append(jax.ShapeDtypeStruct((b, ng, RW_GW, RW_GW), F32))
    scratch = [pltpu.VMEM((ng, RW_GW, RW_GW), F32), pltpu.VMEM((1, RW_P), F32), pltpu.VMEM((RW_TB, RW_W), F32)]
    if has_vres:
        scratch.append(pltpu.VMEM((1, LANES), F32))
    outs = pl.pallas_call(
        functools.partial(_rwkv_fused_body, nc=nc, has_vres=has_vres),
        grid=(b, nc),
        in_specs=in_specs,
        out_specs=out_specs,
        out_shape=out_shape,
        scratch_shapes=scratch,
        compiler_params=_cparams(("parallel", "arbitrary")),
        name="rwkv_fused",
    )(*args)
    if has_vres:
        o, st_bd = outs
    else:
        o, v_first, st_bd = outs
    st5 = st_bd.reshape(b, ng, RW_GH, RW_HD, RW_GH, RW_HD)
    s_fin = jnp.stack([st5[:, :, h, :, h, :] for h in range(RW_GH)], axis=2)
    s_fin = s_fin.reshape(b, RW_HEADS, RW_HD, RW_HD).transpose(0, 1, 3, 2)
    return o, v_first, s_fin


def _ret_log_gamma(h):
    return math.log1p(-(2.0 ** (-5.0 - h)))


def _rotary_tables(pos, heads, dk):
    half = dk // 2
    inv = ROPE_BASE ** (-jnp.arange(half, dtype=F32) / half)
    ang = pos.astype(F32)[:, None] * inv[None, :]
    cos = jnp.tile(jnp.concatenate([jnp.cos(ang), jnp.cos(ang)], -1), (1, heads))
    sin = jnp.tile(jnp.concatenate([-jnp.sin(ang), jnp.sin(ang)], -1), (1, heads))
    lane = jnp.arange(LANES)
    perm = (lane[:, None] == (lane[None, :] ^ half)).astype(BF16)
    return cos, sin, perm


MIX_RB = 2


def _per_sequence(body):
    def batched(*refs, batched_refs, **kw):
        for bb in range(MIX_RB):
            one = [r.at[pl.ds(bb, 1)] if mode == 'keep' else (r.at[bb] if mode == 'drop' else r)
                   for r, mode in zip(refs, batched_refs)]
            body(*one, **kw)
    return batched


def _ret_fused_body(p_ref, cos_ref, sin_ref, perm_ref, o_ref, sf_ref, s_scr, *, nc):
    L, DK, DV, H = CHUNK, RET_QK, RET_V, RET_HEADS
    nq = H * DK
    c = pl.program_id(1)

    @pl.when(c == 0)
    def _():
        s_scr[...] = jnp.zeros_like(s_scr)

    cos = cos_ref[...]
    sin = sin_ref[...]
    perm = perm_ref[...]
    swap = lambda x: jnp.concatenate(
        [_exact_rhs_dot(x[:, c:c + LANES], perm) for c in range(0, nq, LANES)], axis=1)
    rot = lambda x: x * cos + swap(x) * sin
    q_all = rot(p_ref[0, :, 0:nq])
    k_all = rot(p_ref[0, :, nq:2 * nq]) * (DK ** -0.5)
    row = lax.broadcasted_iota(jnp.int32, (L, L), 0)
    col = lax.broadcasted_iota(jnp.int32, (L, L), 1)
    rel = (row - col).astype(F32)
    idx = lax.broadcasted_iota(jnp.int32, (L, 1), 0).astype(F32)
    vs = lambda h: p_ref[0, :, 2 * nq + h * DV:2 * nq + (h + 1) * DV].astype(BF16)
    hd = [dict() for _ in range(H)]
    for h, t in enumerate(hd):
        lg = _ret_log_gamma(h)
        q = q_all[:, h * DK:(h + 1) * DK]
        k = k_all[:, h * DK:(h + 1) * DK]
        t['s_prev'] = s_scr[h]
        t['dmask'] = jnp.where(rel >= 0, jnp.exp(jnp.maximum(rel, 0.0) * lg), 0.0)
        t['qk'] = _dot_nt(q.astype(BF16), k.astype(BF16))
        q_dec = q * jnp.exp((idx + 1.0) * lg)
        t['qs'] = _dot(q_dec.astype(BF16), t['s_prev'].astype(BF16))
        k_end = k * jnp.exp((L - 1.0 - idx) * lg)
        t['kv'] = _dot_tn(k_end.astype(BF16), vs(h))
    for h, t in enumerate(hd):
        y = _dot((t['qk'] * t['dmask']).astype(BF16), vs(h)) + t['qs']
        y = y * lax.rsqrt(jnp.mean(y * y, -1, keepdims=True) + NORM_EPS)
        gate = p_ref[0, :, 2 * nq + RET_W + h * DV:2 * nq + RET_W + (h + 1) * DV]
        o_ref[0, :, h * DV:(h + 1) * DV] = (gate * jax.nn.sigmoid(gate) * y).astype(BF16)
    for h, t in enumerate(hd):
        s_scr[h] = math.exp(L * _ret_log_gamma(h)) * t['s_prev'] + t['kv']

    @pl.when(c == nc - 1)
    def _():
        sf_ref[0] = s_scr[...]


def _ret_prompt(p3, pos):
    b, t, _ = p3.shape
    L = CHUNK
    nc = t // L
    nq = RET_HEADS * RET_QK
    cos, sin, perm = _rotary_tables(pos, RET_HEADS, RET_QK)
    tab = pl.BlockSpec((L, nq), lambda i, j: (j, 0))
    st = pl.BlockSpec((MIX_RB, RET_HEADS, RET_QK, RET_V), lambda i, j: (i, 0, 0, 0))
    modes = ('keep', None, None, None, 'keep', 'keep', 'drop')
    return pl.pallas_call(
        functools.partial(_per_sequence(_ret_fused_body), batched_refs=modes, nc=nc),
        grid=(b // MIX_RB, nc),
        in_specs=[pl.BlockSpec((MIX_RB, L, P_MAIN), lambda i, j: (i, j, P_RET // P_MAIN)), tab, tab,
                  pl.BlockSpec((LANES, LANES), lambda i, j: (0, 0))],
        out_specs=[pl.BlockSpec((MIX_RB, L, RET_W), lambda i, j: (i, j, 0)), st],
        out_shape=[jax.ShapeDtypeStruct((b, t, RET_W), BF16),
                   jax.ShapeDtypeStruct((b, RET_HEADS, RET_QK, RET_V), F32)],
        scratch_shapes=[pltpu.VMEM((MIX_RB, RET_HEADS, RET_QK, RET_V), F32)],
        compiler_params=_cparams(("parallel", "arbitrary")),
        name="ret_fused",
    )(p3, cos, sin, perm)


ML_HPAD = 8
ML_SEL_ROWS = 16


def _ml_fused_body(p_ref, gate_ref, bias_ref, norm_ref, sel_ref, o_ref, cf_ref, nf_ref, mf_ref,
                   c_scr, n_scr, m_scr, *, nc):
    L, DK, DV, H = CHUNK, ML_QK, ML_V, ML_HEADS
    nq = H * DK
    ci = pl.program_id(1)

    @pl.when(ci == 0)
    def _():
        c_scr[...] = jnp.zeros_like(c_scr)
        n_scr[...] = jnp.zeros_like(n_scr)
        m_scr[...] = jnp.zeros_like(m_scr)

    row = lax.broadcasted_iota(jnp.int32, (L, L), 0)
    col = lax.broadcasted_iota(jnp.int32, (L, L), 1)
    causal = row >= col
    tri = causal.astype(BF16)
    capped = ML_GATE_CAP * jnp.tanh((gate_ref[0] + bias_ref[...]) * (1.0 / ML_GATE_CAP))
    lane = lax.broadcasted_iota(jnp.int32, (L, LANES), 1)
    g = jnp.where(lane < H, capped, jnp.where(lane < 2 * H, -_softplus(-capped), 0.0))
    g_rep = _exact_rhs_dot(g[:, :ML_SEL_ROWS], sel_ref[...])
    b_rep_all = _exact_lhs_dot(tri, g_rep[:, H * LANES:])
    g_t = g.T
    cum_t = _exact_lhs_dot(tri, g).T
    cm_all = g_rep[:, :H * LANES] - b_rep_all
    row_id = lax.broadcasted_iota(jnp.int32, (L, 1), 0)
    shift = 1
    while shift < L:
        cm_all = jnp.maximum(cm_all, jnp.where(row_id >= shift, pltpu.roll(cm_all, shift, 0), -jnp.inf))
        shift *= 2
    ones = jnp.ones((L, LANES), BF16)
    mean_w = jnp.full((DV, LANES), 1.0 / DV, BF16)
    m_all = m_scr[...]
    hd = [dict() for _ in range(H)]
    for h, t in enumerate(hd):
        hs = slice(h * LANES, (h + 1) * LANES)
        q = p_ref[0, :, h * DK:(h + 1) * DK].astype(BF16)
        k = p_ref[0, :, nq + h * DK:nq + (h + 1) * DK] * (DK ** -0.5)
        t['v1'] = jnp.concatenate([p_ref[0, :, 2 * nq + h * DV:2 * nq + (h + 1) * DV].astype(BF16), ones], axis=1)
        ig_rep = g_rep[:, hs]
        b_rep = b_rep_all[:, hs]
        ig_row = g_t[h:h + 1, :]
        b_row = cum_t[H + h:H + h + 1, :]
        b_tot = b_rep[L - 1:L, :]
        m_prev = m_all[h:h + 1, :]
        t['c_prev'] = c_scr[h]
        t['n_prev'] = n_scr[h]
        t['m_new'] = jnp.maximum(b_tot + m_prev, jnp.max(b_tot - b_rep + ig_rep, axis=0, keepdims=True))
        t['dec'] = jnp.exp(b_tot + m_prev - t['m_new'])
        kw = k * jnp.exp((b_tot - b_rep + ig_rep - t['m_new'])[:, :DK])
        t['kvn'] = _dot_tn(kw.astype(BF16), t['v1'])
        inter = b_rep + m_prev
        t['m_i'] = b_rep + jnp.maximum(cm_all[:, hs], m_prev)
        t['e'] = jnp.exp(jnp.where(causal, (b_rep - t['m_i']) - b_row + ig_row, -jnp.inf))
        t['sc'] = jnp.exp(inter - t['m_i'])
        t['qk'] = _dot_nt(q, k.astype(BF16))
        cn = jnp.concatenate([t['c_prev'], t['n_prev']], axis=1).astype(BF16)
        t['qcn'] = _dot(q, cn)
    for h, t in enumerate(hd):
        nd = _dot((t['qk'] * t['e']).astype(BF16), t['v1'])
        num = nd[:, :DV] + t['sc'] * t['qcn'][:, :DV]
        den = nd[:, DV:] + t['sc'] * t['qcn'][:, DV:]
        hid = num / jnp.maximum(jnp.abs(den), jnp.exp(-t['m_i']))
        hid = hid * lax.rsqrt(_seg_sum(hid * hid, mean_w) + NORM_EPS)
        og = p_ref[0, :, 2 * nq + ML_W + h * DV:2 * nq + ML_W + (h + 1) * DV]
        o_ref[0, :, h * DV:(h + 1) * DV] = (jax.nn.sigmoid(og) * (hid * norm_ref[:, h * DV:(h + 1) * DV])).astype(BF16)
    for h, t in enumerate(hd):
        c_scr[h] = t['dec'] * t['c_prev'] + t['kvn'][:, :DV]
        n_scr[h] = t['dec'] * t['n_prev'] + t['kvn'][:, DV:]
        m_scr[h:h + 1, :] = t['m_new']

    @pl.when(ci == nc - 1)
    def _():
        cf_ref[0] = c_scr[...]
        nf_ref[0] = n_scr[...]
        mf_ref[0] = m_scr[...]


def _ml_prompt(p3, lp):
    b, t, _ = p3.shape
    L = CHUNK
    nc = t // L
    bias = jnp.pad(jnp.concatenate([lp['ml_ib'], lp['ml_fb']]), (0, LANES - 2 * ML_HEADS)).reshape(1, LANES)
    n_rep = 2 * ML_HEADS * LANES
    sel = (jnp.arange(ML_SEL_ROWS)[:, None] == jnp.arange(n_rep)[None, :] // LANES).astype(BF16)
    vs = pl.BlockSpec((MIX_RB, L, ML_W), lambda i, j: (i, j, 0))
    cs = pl.BlockSpec((MIX_RB, ML_HEADS, ML_QK, ML_V), lambda i, j: (i, 0, 0, 0))
    ns = pl.BlockSpec((MIX_RB, ML_HEADS, ML_QK, LANES), lambda i, j: (i, 0, 0, 0))
    ms = pl.BlockSpec((MIX_RB, ML_HPAD, LANES), lambda i, j: (i, 0, 0))
    modes = ('keep', 'keep', None, None, None, 'keep', 'keep', 'keep', 'keep', 'drop', 'drop', 'drop')
    o, c_f, n_f, m_f = pl.pallas_call(
        functools.partial(_per_sequence(_ml_fused_body), batched_refs=modes, nc=nc),
        grid=(b // MIX_RB, nc),
        in_specs=[pl.BlockSpec((MIX_RB, L, P_MAIN), lambda i, j: (i, j, P_ML // P_MAIN)),
                  pl.BlockSpec((MIX_RB, L, LANES), lambda i, j: (i, j, P_GATE // LANES)),
                  pl.BlockSpec((1, LANES), lambda i, j: (0, 0)),
                  pl.BlockSpec((1, ML_W), lambda i, j: (0, 0)),
                  pl.BlockSpec((ML_SEL_ROWS, n_rep), lambda i, j: (0, 0))],
        out_specs=[vs, cs, ns, ms],
        out_shape=[jax.ShapeDtypeStruct((b, t, ML_W), BF16),
                   jax.ShapeDtypeStruct((b, ML_HEADS, ML_QK, ML_V), F32),
                   jax.ShapeDtypeStruct((b, ML_HEADS, ML_QK, LANES), F32),
                   jax.ShapeDtypeStruct((b, ML_HPAD, LANES), F32)],
        scratch_shapes=[pltpu.VMEM((MIX_RB, ML_HEADS, ML_QK, ML_V), F32),
                        pltpu.VMEM((MIX_RB, ML_HEADS, ML_QK, LANES), F32),
                        pltpu.VMEM((MIX_RB, ML_HPAD, LANES), F32)],
        compiler_params=_cparams(("parallel", "arbitrary")),
        name="ml_fused",
    )(p3, p3, bias, lp['ml_norm'].reshape(1, ML_W), sel)
    return o, c_f, n_f[..., 0], m_f[:, :ML_HEADS, 0]


def _to_cols(x):
    b, c = x.shape
    cols = x.reshape(b // DEC_TB, DEC_TB, c).transpose(0, 2, 1)
    return jnp.pad(cols, ((0, 0), (0, 0), (0, DEC_TB)))


def _col_selector():
    j = jnp.arange(2 * DEC_TB)[:, None]
    return (j == jnp.arange(DEC_TB * LANES)[None, :] // LANES).astype(BF16)


def _rwkv_step_body(w_ref, ap_ref, k_ref, kk_ref, r_ref, v_ref, s_ref, y_ref, so_ref):
    N = RW_HD
    v = v_ref[0]

    def sa_step(i, acc):
        return acc + kk_ref[0, pl.ds(i, 1), :] * s_ref[i]

    sa = lax.fori_loop(0, N, sa_step, jnp.zeros_like(v), unroll=8)

    def upd_step(i, y):
        row = lambda ref: ref[0, pl.ds(i, 1), :]
        s_new = row(w_ref) * s_ref[i] - row(ap_ref) * sa + row(k_ref) * v
        so_ref[i] = s_new
        return y + row(r_ref) * s_new

    y_ref[0] = lax.fori_loop(0, N, upd_step, jnp.zeros_like(v), unroll=8)


def _rwkv_step(wdec, ap, k, kk, r, v, s_all, layer):
    b, w = v.shape
    heads = lambda x: x.reshape(b, RW_HEADS, RW_HD).transpose(1, 2, 0)
    vec = pl.BlockSpec((1, RW_HD, b), lambda h: (h, 0, 0))
    y, s_new = pl.pallas_call(
        _rwkv_step_body,
        grid=(RW_HEADS,),
        in_specs=[vec] * 6 + [pl.BlockSpec((None, None, RW_HD, RW_HD, b), lambda h: (layer, h, 0, 0, 0))],
        out_specs=[vec, pl.BlockSpec((None, RW_HD, RW_HD, b), lambda h: (h, 0, 0, 0))],
        out_shape=[jax.ShapeDtypeStruct((RW_HEADS, RW_HD, b), F32),
                   jax.ShapeDtypeStruct((RW_HEADS, RW_HD, RW_HD, b), F32)],
        compiler_params=_cparams(("parallel",)),
        name="rwkv_step",
    )(heads(wdec), heads(ap), heads(k), heads(kk), heads(r), heads(v), s_all)
    return y.transpose(2, 0, 1).reshape(b, w), s_new


def _ret_step_body(kc_ref, q_ref, k_ref, v_ref, esel_ref, seg_ref, s_ref, y_ref, so_ref):
    DK, DV = RET_QK, RET_V
    q = q_ref[...]
    qk_v = _exact_rhs_dot(q * k_ref[...], seg_ref[...])
    row_id = lax.broadcasted_iota(jnp.int32, (DEC_TB, 1), 0)
    for h in range(RET_HEADS):
        gamma = math.exp(_ret_log_gamma(h))
        ks = slice(h * DK, (h + 1) * DK)
        vs = slice(h * DV, (h + 1) * DV)
        k_rep = _exact_rhs_dot(kc_ref[0, ks, :], esel_ref[...])
        q_h = q[:, ks].astype(BF16)
        v_h = v_ref[:, vs]
        qs = jnp.zeros((DEC_TB, DV), F32)
        for j in range(DEC_TB):
            s = s_ref[j, h]
            so_ref[j, h] = gamma * s + k_rep[:, j * LANES:(j + 1) * LANES] * v_h[j:j + 1, :]
            qs = jnp.where(row_id == j, _dot(q_h, s.astype(BF16)), qs)
        y_ref[:, vs] = qk_v[:, vs] * v_h + gamma * qs


def _ret_step(q, k, v, s_all, layer):
    b = q.shape[0]
    nqk = RET_HEADS * RET_QK
    cols = pl.BlockSpec((1, nqk, 2 * DEC_TB), lambda i: (i, 0, 0))
    qk_rows = pl.BlockSpec((DEC_TB, nqk), lambda i: (i, 0))
    rows = pl.BlockSpec((DEC_TB, RET_W), lambda i: (i, 0))
    esel = pl.BlockSpec((2 * DEC_TB, DEC_TB * LANES), lambda i: (0, 0))
    seg = (jnp.arange(nqk)[:, None] // RET_QK == jnp.arange(RET_W)[None, :] // RET_V).astype(BF16)
    st = pl.BlockSpec((DEC_TB, RET_HEADS, RET_QK, RET_V), lambda i: (i, 0, 0, 0))
    st_in = pl.BlockSpec((None, DEC_TB, RET_HEADS, RET_QK, RET_V), lambda i: (layer, i, 0, 0, 0))
    return pl.pallas_call(
        _ret_step_body,
        grid=(b // DEC_TB,),
        in_specs=[cols, qk_rows, qk_rows, rows, esel, pl.BlockSpec((nqk, RET_W), lambda i: (0, 0)), st_in],
        out_specs=[rows, st],
        out_shape=[jax.ShapeDtypeStruct((b, RET_W), F32), jax.ShapeDtypeStruct(s_all.shape[1:], F32)],
        compiler_params=_cparams(("parallel",)),
        name="ret_step",
    )(_to_cols(k), q, k, v, _col_selector(), seg, s_all)


def _ml_step_body(kc_ref, q_ref, k_ref, v_ref, ig_ref, lf_ref, esel_ref, hsel_ref, hsel_k_ref, seg_ref,
                  c_ref, n_ref, m_ref, h_ref, co_ref, no_ref, mo_ref):
    DK, DV, H = ML_QK, ML_V, ML_HEADS
    ig, lf, m_prev = ig_ref[...], lf_ref[...], m_ref[...]
    m_new = jnp.maximum(lf + m_prev, ig)
    dec = jnp.exp(lf + m_prev - m_new)
    wgt = jnp.exp(ig - m_new)
    mo_ref[...] = m_new
    hsel = hsel_ref[...]
    dec_v = _exact_rhs_dot(dec, hsel)
    wgt_v = _exact_rhs_dot(wgt, hsel)
    floor_v = _exact_rhs_dot(jnp.exp(-m_new), hsel)
    q, k, n_prev = q_ref[...], k_ref[...], n_ref[...]
    no_ref[...] = _exact_rhs_dot(dec, hsel_k_ref[...]) * n_prev + k * _exact_rhs_dot(wgt, hsel_k_ref[...])
    s_v = _exact_rhs_dot(q * k, seg_ref[...]) * wgt_v
    den_v = s_v + dec_v * _exact_rhs_dot(q * n_prev, seg_ref[...])
    row_id = lax.broadcasted_iota(jnp.int32, (DEC_TB, 1), 0)
    for h in range(H):
        ks = slice(h * DK, (h + 1) * DK)
        vs = slice(h * DV, (h + 1) * DV)
        k_rep = _exact_rhs_dot(kc_ref[0, ks, :], esel_ref[...])
        q_h = q[:, ks].astype(BF16)
        v_h = v_ref[:, vs]
        kv_scale = wgt_v[:, vs] * v_h
        qc = jnp.zeros((DEC_TB, DV), F32)
        for j in range(DEC_TB):
            c_prev = c_ref[j, h]
            co_ref[j, h] = dec_v[j:j + 1, vs] * c_prev + k_rep[:, j * LANES:(j + 1) * LANES] * kv_scale[j:j + 1, :]
            qc = jnp.where(row_id == j, _dot(q_h, c_prev.astype(BF16)), qc)
        num = s_v[:, vs] * v_h + dec_v[:, vs] * qc
        h_ref[:, vs] = num / jnp.maximum(jnp.abs(den_v[:, vs]), floor_v[:, vs])


def _ml_step(q, k, v, ig, lf, c_all, layer, n0, m0):
    b = q.shape[0]
    nqk = ML_HEADS * ML_QK
    pad_h = lambda x: jnp.pad(x, ((0, 0), (0, LANES - ML_HEADS)))
    head = jnp.arange(LANES)[:, None]
    hsel = (head == jnp.arange(ML_W)[None, :] // ML_V).astype(BF16)
    hsel_k = (head == jnp.arange(nqk)[None, :] // ML_QK).astype(BF16)
    seg = (jnp.arange(nqk)[:, None] // ML_QK == jnp.arange(ML_W)[None, :] // ML_V).astype(BF16)
    const = lambda r, c: pl.BlockSpec((r, c), lambda i: (0, 0))
    cols = pl.BlockSpec((1, nqk, 2 * DEC_TB), lambda i: (i, 0, 0))
    qk_rows = pl.BlockSpec((DEC_TB, nqk), lambda i: (i, 0))
    rows = pl.BlockSpec((DEC_TB, ML_W), lambda i: (i, 0))
    sc = pl.BlockSpec((DEC_TB, LANES), lambda i: (i, 0))
    cs = pl.BlockSpec((DEC_TB, ML_HEADS, ML_QK, ML_V), lambda i: (i, 0, 0, 0))
    cs_in = pl.BlockSpec((None, DEC_TB, ML_HEADS, ML_QK, ML_V), lambda i: (layer, i, 0, 0, 0))
    hm, c_new, n_new, m_new = pl.pallas_call(
        _ml_step_body,
        grid=(b // DEC_TB,),
        in_specs=[cols, qk_rows, qk_rows, rows, sc, sc, const(2 * DEC_TB, DEC_TB * LANES), const(LANES, ML_W),
                  const(LANES, nqk), const(nqk, ML_W), cs_in, qk_rows, sc],
        out_specs=[rows, cs, qk_rows, sc],
        out_shape=[jax.ShapeDtypeStruct((b, ML_W), F32), jax.ShapeDtypeStruct(c_all.shape[1:], F32),
                   jax.ShapeDtypeStruct((b, nqk), F32), jax.ShapeDtypeStruct((b, LANES), F32)],
        compiler_params=_cparams(("parallel",)),
        name="ml_step",
    )(_to_cols(k), q, k, v, pad_h(ig), pad_h(lf), _col_selector(), hsel, hsel_k, seg,
      c_all, n0.reshape(b, nqk), pad_h(m0))
    return hm, c_new, n_new.reshape(n0.shape), m_new[:, :ML_HEADS]


def _heads(a, h):
    return a.reshape(a.shape[:-1] + (h, a.shape[-1] // h))


def _shift_prev(p, prev_row):
    return jnp.concatenate([prev_row[:, None, :], p[:, :-1]], axis=1)


def _rotary(x, pos):
    half = x.shape[-1] // 2
    inv = ROPE_BASE ** (-jnp.arange(half, dtype=F32) / half)
    ang = pos.astype(F32)[:, None] * inv[None, :]
    cos = jnp.cos(ang)[None, :, None, :]
    sin = jnp.sin(ang)[None, :, None, :]
    x1, x2 = x[..., :half], x[..., half:]
    return jnp.concatenate([x1 * cos - x2 * sin, x1 * sin + x2 * cos], -1)


def _small_matmul(x, w):
    lead = x.shape[:-1]
    kdim, n = w.shape
    x2 = x.reshape(-1, kdim)
    m = x2.shape[0]
    kp = -(-kdim // LANES) * LANES
    npad = -(-n // LANES) * LANES
    x2 = jnp.pad(x2.astype(BF16), ((0, 0), (0, kp - kdim)))
    w2 = jnp.pad(w.astype(BF16), ((0, kp - kdim), (0, npad - n)))
    tm = 1024 if m % 1024 == 0 else m
    out = _matmul(x2, w2, tm, npad)
    return out[:, :n].reshape(lead + (n,))


def _mix_prompt(p, pos, v_first, lp):
    o_rw, v_first, s_new = _rwkv_prompt(p, lp, v_first)
    o_ret, r_new = _ret_prompt(p, pos)
    o_ml, c_new, n_new, m_new = _ml_prompt(p, lp)
    return (o_rw, o_ret, o_ml), v_first, (s_new, r_new, c_new, n_new, m_new)


def _mix_sample(p, pos, v_first, st, lp, prev_row):
    bsz, t_len, _ = p.shape

    p_rw = p[..., :RW_P]
    mixed = p_rw + (_shift_prev(p_rw, prev_row[:, :RW_P]) - p_rw) * lp['rw_mu']
    sizes = np.cumsum([RW_W, RW_W, RW_W, RW_LORA_W, RW_LORA_A, RW_LORA_G])[:-1]
    r, k, v, xw, xa, xg = jnp.split(mixed, [int(s) for s in sizes], axis=-1)
    lora_in = [jnp.tanh(xw), xa, jax.nn.sigmoid(xg)]
    lora_w = [lp['rw_w2'], lp['rw_a2'], lp['rw_g2']]
    if v_first is not None:
        pv = p[..., P_V1:P_V1 + RW_LORA_V]
        lora_in.append(pv + (_shift_prev(pv, prev_row[:, P_V1:P_V1 + RW_LORA_V]) - pv) * lp['rw_vmu'])
        lora_w.append(lp['rw_v2'])
    lora = _small_matmul(jnp.concatenate(lora_in, -1), jax.scipy.linalg.block_diag(*lora_w))
    w = -jax.nn.softplus(-(lp['rw_w0'] + lora[..., :RW_W])) - 0.5
    a = jax.nn.sigmoid(lp['rw_a0'] + lora[..., RW_W:2 * RW_W])
    g = lora[..., 2 * RW_W:3 * RW_W]
    if v_first is None:
        v_first = v
    else:
        v = v + (v_first - v) * jax.nn.sigmoid(lp['rw_v0'] + lora[..., 3 * RW_W:])
    kk = _heads(k * lp['rw_kk'], RW_HEADS)
    kk = kk * lax.rsqrt(jnp.maximum(jnp.sum(jnp.square(kk), -1, keepdims=True), 1e-24))
    kk = kk.reshape(bsz, t_len, RW_W)
    k = k * (1.0 + (a - 1.0) * lp['rw_ka'])
    lw = -jnp.exp(w)
    y, s_new = _rwkv_step(jnp.exp(lw)[:, 0], (kk * a)[:, 0], k[:, 0], kk[:, 0], r[:, 0], v[:, 0],
                          st['rw_wkv_t'], st['layer'])
    y = _heads(y[:, None, :], RW_HEADS)
    y_mu = jnp.mean(y, -1, keepdims=True)
    y_var = jnp.mean(jnp.square(y - y_mu), -1, keepdims=True)
    y = ((y - y_mu) * lax.rsqrt(y_var + RW_GN_EPS)).reshape(bsz, t_len, RW_W)
    y = y * lp['rw_lnx_g'] + lp['rw_lnx_b']
    rh, kh, vh = (_heads(u, RW_HEADS) for u in (r, k, v))
    bonus = jnp.sum(rh * kh * _heads(lp['rw_rk'], RW_HEADS), -1, keepdims=True) * vh
    o_rw = ((y + bonus.reshape(bsz, t_len, RW_W)) * g).astype(BF16)

    nqk = RET_HEADS * RET_QK
    p_ret = p[..., P_RET:P_RET + P_MAIN]
    qr, kr, vr, gr = (p_ret[..., :nqk], p_ret[..., nqk:2 * nqk],
                      p_ret[..., 2 * nqk:2 * nqk + RET_W], p_ret[..., 2 * nqk + RET_W:])
    qh = _rotary(_heads(qr, RET_HEADS), pos).reshape(bsz, t_len, nqk)
    khr = (_rotary(_heads(kr, RET_HEADS), pos) * (RET_QK ** -0.5)).reshape(bsz, t_len, nqk)
    yr, r_new = _ret_step(qh[:, 0], khr[:, 0], vr[:, 0], st['ret_all'], st['layer'])
    yr = _heads(yr[:, None, :], RET_HEADS)
    yr = yr * lax.rsqrt(jnp.mean(jnp.square(yr), -1, keepdims=True) + NORM_EPS)
    o_ret = (jax.nn.silu(gr) * yr.reshape(bsz, t_len, RET_W)).astype(BF16)

    nqk = ML_HEADS * ML_QK
    p_ml = p[..., P_ML:P_ML + P_MAIN]
    qm, km, vm, om = (p_ml[..., :nqk], p_ml[..., nqk:2 * nqk],
                      p_ml[..., 2 * nqk:2 * nqk + ML_W], p_ml[..., 2 * nqk + ML_W:])
    im = p[..., P_GATE:P_GATE + ML_HEADS]
    fm = p[..., P_GATE + ML_HEADS:P_GATE + 2 * ML_HEADS]
    ig = ML_GATE_CAP * jnp.tanh((im + lp['ml_ib']) / ML_GATE_CAP)
    lf = jax.nn.log_sigmoid(ML_GATE_CAP * jnp.tanh((fm + lp['ml_fb']) / ML_GATE_CAP))
    km = km * (ML_QK ** -0.5)
    hm, c_new, n_new, m_new = _ml_step(qm[:, 0], km[:, 0], vm[:, 0], ig[:, 0], lf[:, 0],
                                       st['ml_c_all'], st['layer'], st['ml_n'], st['ml_m'])
    hm = _heads(hm[:, None, :], ML_HEADS)
    hm = hm * lax.rsqrt(jnp.mean(jnp.square(hm), -1, keepdims=True) + NORM_EPS)
    o_ml = (jax.nn.sigmoid(om) * (hm.reshape(bsz, t_len, ML_W) * lp['ml_norm'])).astype(BF16)

    return (o_rw, o_ret, o_ml), v_first, (s_new, r_new, c_new, n_new, m_new)


_T_V1 = P_V1 // LANES
_T_GATE = P_GATE // LANES
_T_RET = P_RET // LANES
_T_SHIFT = (P_RET - RW_P) // LANES
_T_SRC_GATE = (RW_P + RET_P + 2 * ML_HEADS * ML_QK + 2 * ML_W) // LANES


def _pack_body(w_ref, v1_ref, *o_refs):
    j = pl.program_id(0)
    row = lax.broadcasted_iota(jnp.int32, (LANES, 1), 0)
    spare = jnp.logical_and(j > _T_GATE, j < _T_RET)
    for l, o_ref in enumerate(o_refs):
        w = w_ref[:, l, :]
        gates = jnp.where(row < 2 * ML_HEADS, w, 0.0)
        out = jnp.where(j == _T_V1, v1_ref[l], jnp.where(j == _T_GATE, gates, jnp.where(spare, 0.0, w)))
        o_ref[...] = out.astype(BF16)


def _pack_w_in(w_in, rw_v1):
    depth, d, _ = w_in.shape
    w_t = w_in.transpose(2, 0, 1)
    v1_t = jnp.pad(rw_v1.transpose(0, 2, 1), ((1, 0), (0, LANES - RW_LORA_V), (0, 0)))

    def src_tile(j):
        return jnp.where(j < _T_V1, j, jnp.where(j == _T_GATE, _T_SRC_GATE, j - _T_SHIFT))

    return pl.pallas_call(
        _pack_body,
        grid=(P_PAD // LANES,),
        in_specs=[pl.BlockSpec((LANES, depth, d), lambda j: (src_tile(j), 0, 0)),
                  pl.BlockSpec((depth, LANES, d), lambda j: (0, 0, 0))],
        out_specs=[pl.BlockSpec((LANES, d), lambda j: (j, 0))] * depth,
        out_shape=[jax.ShapeDtypeStruct((P_PAD, d), BF16)] * depth,
        compiler_params=_cparams(("parallel",)),
        name="pack_w_in",
    )(w_t, v1_t)


def _token_tiles(m):
    if m % 2048 == 0:
        return 2048, 512, 512
    return m, m, m


def kernel(x_prompt, x_sample, state_rw_shift, state_rw_wkv, state_ret, state_ml_c, state_ml_n, state_ml_m,
           ln0_g, ln0_b, w_in, rw_mu, rw_w0, rw_w2, rw_a0, rw_a2, rw_g2, rw_kk, rw_ka, rw_rk,
           rw_lnx_g, rw_lnx_b, rw_v0, rw_v1, rw_vmu, rw_v2, ml_ib, ml_fb, ml_norm, w_out,
           ln1_g, ln1_b, w_gate, w_up, w_down, ln2_g, ln2_b):
    bp, tp, d = x_prompt.shape
    bs, ts, _ = x_sample.shape
    mp, ms = bp * tp, bs * ts
    pos_p = jnp.arange(tp)
    pos_s = PAST_LEN + jnp.arange(ts)
    tm_big, tm_out, tm_down = _token_tiles(mp)
    rw_wkv_t = state_rw_wkv.transpose(0, 2, 3, 4, 1)
    w_in_packed = _pack_w_in(w_in, rw_v1)
    xf_p, xb_p = _layernorm(x_prompt.reshape(mp, d), ln0_g, ln0_b, tm_out)
    xf_s, xb_s = _layernorm(x_sample.reshape(ms, d), ln0_g, ln0_b, ms)
    vf_p = vf_s = None
    outs_p, outs_s = [], []

    for l in range(DEPTH):
        lp = {
            'rw_mu': rw_mu[l], 'rw_w0': rw_w0[l], 'rw_w2': rw_w2[l], 'rw_a0': rw_a0[l], 'rw_a2': rw_a2[l],
            'rw_g2': rw_g2[l], 'rw_kk': rw_kk[l], 'rw_ka': rw_ka[l], 'rw_rk': rw_rk[l],
            'rw_lnx_g': rw_lnx_g[l], 'rw_lnx_b': rw_lnx_b[l], 'ml_ib': ml_ib[l], 'ml_fb': ml_fb[l],
            'ml_norm': ml_norm[l],
        }
        if l > 0:
            lp.update(rw_v0=rw_v0[l - 1], rw_vmu=rw_vmu[l - 1], rw_v2=rw_v2[l - 1])
        w_out_b = w_out[l].astype(BF16)
        x_side = jnp.concatenate([xb_s, state_rw_shift[l].astype(BF16)], axis=0)
        p_p, p_side = _in_proj(xb_p, x_side, w_in_packed[l], tm_big, P_TN)
        o_p, vf_p, st_p = _mix_prompt(p_p.reshape(bp, tp, P_PAD), pos_p, vf_p, lp)
        st = {'rw_wkv_t': rw_wkv_t, 'layer': l, 'ret_all': state_ret, 'ml_c_all': state_ml_c,
              'ml_n': state_ml_n[l], 'ml_m': state_ml_m[l]}
        o_s, vf_s, st_s = _mix_sample(p_side[:ms].reshape(bs, ts, P_PAD), pos_s, vf_s, st, lp, p_side[ms:])
        outs_p.append((xf_p.reshape(bp, tp, d)[:, -1],) + st_p)
        outs_s.append((xf_s.reshape(bs, ts, d)[:, -1],) + st_s)
        flat = lambda o, m: tuple(u.reshape(m, u.shape[-1]) for u in o)
        x1f_p, x1b_p, x1f_s, x1b_s = _out_proj_ln(flat(o_p, mp), xf_p, flat(o_s, ms), xf_s, w_out_b,
                                                  ln1_g[l], ln1_b[l], tm_out)
        hdn_p, hdn_s, w_down_b = _matmul_swiglu(x1b_p, x1b_s, w_gate, w_up, w_down, l, tm_big, 512)
        xf_p, xb_p, xf_s, xb_s = _matmul_res_ln(hdn_p, hdn_s, w_down_b, x1f_p, x1f_s, ln2_g[l], ln2_b[l],
                                                tm_down, DOWN_TK)

    y_p = xf_p.reshape(bp, tp, d)
    y_s = xf_s.reshape(bs, ts, d)
    sp = [jnp.stack([o[i] for o in outs_p]) for i in range(6)]
    ss = [jnp.stack([o[i] for o in outs_s]) for i in range(6)]
    ss[1] = ss[1].transpose(0, 4, 1, 2, 3)
    return (y_p, y_s, sp[0], sp[1], sp[2], sp[3], sp[4], sp[5], ss[0], ss[1], ss[2], ss[3], ss[4], ss[5])
```

```python
import functools
import math

import numpy as np
import jax
import jax.numpy as jnp
from jax import lax
from jax.experimental import pallas as pl
from jax.experimental.pallas import tpu as pltpu

F32 = jnp.float32
BF16 = jnp.bfloat16

D_MODEL = 2048
DEPTH = 2
PAST_LEN = 16384
RW_HD = 64
RW_W = D_MODEL // 4
RW_HEADS = RW_W // RW_HD
RW_LORA_W = 64
RW_LORA_A = 64
RW_LORA_V = 32
RW_LORA_G = 128
RW_P = 3 * RW_W + RW_LORA_W + RW_LORA_A + RW_LORA_G
RW_GN_EPS = 64e-5
RET_V = 128
RET_QK = 64
RET_W = 3 * D_MODEL // 8
RET_HEADS = RET_W // RET_V
RET_P = 2 * RET_HEADS * RET_QK + 2 * RET_W
ML_V = 128
ML_QK = 64
ML_W = D_MODEL - RW_W - RET_W
ML_HEADS = ML_W // ML_V
ML_P = 2 * ML_HEADS * ML_QK + 2 * ML_W + 2 * ML_HEADS
ML_GATE_CAP = 15.0
P_TOTAL = RW_P + RET_P + ML_P
D_FF = ((8 * D_MODEL + 3 * 256 - 1) // (3 * 256)) * 256
CHUNK = 128
ROPE_BASE = 10000.0
LN_EPS = 1e-5
NORM_EPS = 1e-6
ALPHA = (2 * DEPTH) ** 0.25

LANES = 128
P_V1 = RW_P
P_GATE = RW_P + LANES
P_RET = 2304
P_ML = 2 * P_RET
P_MAIN = 2304
P_PAD = 3 * P_RET
P_TN = 768
RW_CHUNK = 64
DEC_TB = 8
VMEM_LIMIT = 56 * 1024 * 1024

def _cparams(sem):
    return pltpu.CompilerParams(dimension_semantics=sem, vmem_limit_bytes=VMEM_LIMIT)


def _dot(a, b):
    return lax.dot_general(a, b, (((1,), (0,)), ((), ())), preferred_element_type=F32)


def _dot_nt(a, b):
    return lax.dot_general(a, b, (((1,), (1,)), ((), ())), preferred_element_type=F32)


def _dot_tn(a, b):
    return lax.dot_general(a, b, (((0,), (0,)), ((), ())), preferred_element_type=F32)


def _ln_rows(x, g, b):
    mu = jnp.mean(x, -1, keepdims=True)
    xc = x - mu
    var = jnp.mean(xc * xc, -1, keepdims=True)
    return xc * lax.rsqrt(var + LN_EPS) * g + b


def _ln_body(x_ref, g_ref, b_ref, of_ref, ob_ref):
    y = _ln_rows(x_ref[...], g_ref[...], b_ref[...])
    of_ref[...] = y
    ob_ref[...] = y.astype(BF16)


def _layernorm(x, g, b, tm):
    m, d = x.shape
    return pl.pallas_call(
        _ln_body,
        grid=(m // tm,),
        in_specs=[pl.BlockSpec((tm, d), lambda i: (i, 0)),
                  pl.BlockSpec((1, d), lambda i: (0, 0)),
                  pl.BlockSpec((1, d), lambda i: (0, 0))],
        out_specs=[pl.BlockSpec((tm, d), lambda i: (i, 0)),
                   pl.BlockSpec((tm, d), lambda i: (i, 0))],
        out_shape=[jax.ShapeDtypeStruct((m, d), F32), jax.ShapeDtypeStruct((m, d), BF16)],
        compiler_params=_cparams(("parallel",)),
        name="layernorm",
    )(x, g.reshape(1, d), b.reshape(1, d))


def _mm_body(x_ref, w_ref, o_ref, *, w_transposed):
    dot = _dot_nt if w_transposed else _dot
    o_ref[...] = dot(x_ref[...], w_ref[...]).astype(o_ref.dtype)


def _matmul(x, w, tm, tn, out_dtype=F32, w_transposed=False):
    m, k = x.shape
    n = w.shape[0] if w_transposed else w.shape[1]
    w_spec = (pl.BlockSpec((tn, k), lambda i, j: (j, 0)) if w_transposed
              else pl.BlockSpec((k, tn), lambda i, j: (0, j)))
    return pl.pallas_call(
        functools.partial(_mm_body, w_transposed=w_transposed),
        grid=(m // tm, n // tn),
        in_specs=[pl.BlockSpec((tm, k), lambda i, j: (i, 0)), w_spec],
        out_specs=pl.BlockSpec((tm, tn), lambda i, j: (i, j)),
        out_shape=jax.ShapeDtypeStruct((m, n), out_dtype),
        compiler_params=_cparams(("parallel", "parallel")),
        name="matmul",
    )(x, w)


def _in_proj_body(x_ref, xs_ref, w_ref, o_ref, os_ref):
    @pl.when(pl.program_id(1) == 0)
    def _():
        os_ref[...] = _dot_nt(xs_ref[...], w_ref[...])

    o_ref[...] = _dot_nt(x_ref[...], w_ref[...])


def _in_proj(x, x_side, w_t, tm, tn):
    m, k = x.shape
    ms = x_side.shape[0]
    n = w_t.shape[0]
    return pl.pallas_call(
        _in_proj_body,
        grid=(n // tn, m // tm),
        in_specs=[pl.BlockSpec((tm, k), lambda j, i: (i, 0)), pl.BlockSpec((ms, k), lambda j, i: (0, 0)),
                  pl.BlockSpec((tn, k), lambda j, i: (j, 0))],
        out_specs=[pl.BlockSpec((tm, tn), lambda j, i: (i, j)), pl.BlockSpec((ms, tn), lambda j, i: (0, j))],
        out_shape=[jax.ShapeDtypeStruct((m, n), F32), jax.ShapeDtypeStruct((ms, n), F32)],
        compiler_params=_cparams(("parallel", "arbitrary")),
        name="in_proj",
    )(x, x_side, w_t)


def _swiglu_body(x_ref, xs_ref, wg_ref, wu_ref, wd_ref, o_ref, os_ref, wdb_ref, wg_scr, wu_scr):
    def act(x):
        g = _dot(x, wg_scr[...])
        return (g * jax.nn.sigmoid(g) * _dot(x, wu_scr[...])).astype(BF16)

    wdb_ref[...] = wd_ref[...].astype(BF16)

    @pl.when(pl.program_id(1) == 0)
    def _():
        wg_scr[...] = wg_ref[...].astype(BF16)
        wu_scr[...] = wu_ref[...].astype(BF16)
        os_ref[...] = act(xs_ref[...])

    step = x_ref.shape[0] // SWIGLU_CHUNKS
    for r in range(SWIGLU_CHUNKS):
        rows = pl.ds(r * step, step)
        o_ref[rows, :] = act(x_ref[rows, :])


def _matmul_swiglu(x, x_side, wg, wu, wd, layer, tm, tn):
    m, k = x.shape
    ms = x_side.shape[0]
    n = wg.shape[2]
    n_i = m // tm
    slab = n // ((n // tn) * n_i)
    assert slab * (n // tn) * n_i == n and slab % 16 == 0, (n, tn, n_i)
    w_spec = pl.BlockSpec((None, k, tn), lambda j, i: (layer, 0, j))
    return pl.pallas_call(
        _swiglu_body,
        grid=(n // tn, n_i),
        in_specs=[pl.BlockSpec((tm, k), lambda j, i: (i, 0)), pl.BlockSpec((ms, k), lambda j, i: (0, 0)),
                  w_spec, w_spec, pl.BlockSpec((None, slab, k), lambda j, i: (layer, j * n_i + i, 0))],
        out_specs=[pl.BlockSpec((tm, tn), lambda j, i: (i, j)), pl.BlockSpec((ms, tn), lambda j, i: (0, j)),
                   pl.BlockSpec((slab, k), lambda j, i: (j * n_i + i, 0))],
        out_shape=[jax.ShapeDtypeStruct((m, n), BF16), jax.ShapeDtypeStruct((ms, n), BF16),
                   jax.ShapeDtypeStruct((n, k), BF16)],
        scratch_shapes=[pltpu.VMEM((k, tn), BF16), pltpu.VMEM((k, tn), BF16)],
        compiler_params=_cparams(("parallel", "arbitrary")),
        name="matmul_swiglu",
    )(x, x_side, wg, wu, wd)


SWIGLU_CHUNKS = 2
DOWN_TK = D_FF // 2
LN_ROWS = 256
OUT_ROWS = 256


def _res_ln_store(acc_ref, res_ref, g_ref, b_ref, of_ref, ob_ref, n_rows):
    step = min(LN_ROWS, n_rows)
    for r in range(0, n_rows, step):
        rows = pl.ds(r, step)
        y = _ln_rows(ALPHA * res_ref[rows, :] + acc_ref[rows, :], g_ref[...], b_ref[...])
        of_ref[rows, :] = y
        ob_ref[rows, :] = y.astype(BF16)


def _mm_res_ln_body(x_ref, xs_ref, w_ref, res_ref, ress_ref, g_ref, b_ref, of_ref, ob_ref, ofs_ref, obs_ref, *, nk, tm, ms):
    i = pl.program_id(0)
    kk = pl.program_id(1)

    @pl.when(kk == 0)
    def _():
        of_ref[...] = jnp.zeros_like(of_ref)

    of_ref[...] += _dot(x_ref[...], w_ref[...])

    @pl.when(kk == nk - 1)
    def _():
        _res_ln_store(of_ref, res_ref, g_ref, b_ref, of_ref, ob_ref, tm)

    @pl.when(i == 0)
    def _():
        @pl.when(kk == 0)
        def _():
            ofs_ref[...] = jnp.zeros_like(ofs_ref)

        ofs_ref[...] += _dot(xs_ref[...], w_ref[...])

        @pl.when(kk == nk - 1)
        def _():
            _res_ln_store(ofs_ref, ress_ref, g_ref, b_ref, ofs_ref, obs_ref, ms)


def _matmul_res_ln(x, x_side, w, res, res_side, g, b, tm, tk):
    m, k = x.shape
    ms = x_side.shape[0]
    n = w.shape[1]
    nk = k // tk
    const = lambda r: pl.BlockSpec((r, n), lambda i, j: (0, 0))
    main = pl.BlockSpec((tm, n), lambda i, j: (i, 0))
    return pl.pallas_call(
        functools.partial(_mm_res_ln_body, nk=nk, tm=tm, ms=ms),
        grid=(m // tm, nk),
        in_specs=[pl.BlockSpec((tm, tk), lambda i, j: (i, j)), pl.BlockSpec((ms, tk), lambda i, j: (0, j)),
                  pl.BlockSpec((tk, n), lambda i, j: (j, 0)), main, const(ms), const(1), const(1)],
        out_specs=[main, main, const(ms), const(ms)],
        out_shape=[jax.ShapeDtypeStruct((m, n), F32), jax.ShapeDtypeStruct((m, n), BF16),
                   jax.ShapeDtypeStruct((ms, n), F32), jax.ShapeDtypeStruct((ms, n), BF16)],
        compiler_params=_cparams(("arbitrary", "arbitrary")),
        name="matmul_res_ln",
    )(x, x_side, w, res, res_side, g.reshape(1, n), b.reshape(1, n))


def _out_proj_ln_body(o_rw_ref, o_ret_ref, o_ml_ref, res_ref, s_rw_ref, s_ret_ref, s_ml_ref, ress_ref,
                      w_ref, g_ref, b_ref, of_ref, ob_ref, ofs_ref, obs_ref, *, tm, ms):
    def project(rw_ref, ret_ref, ml_ref, r_ref, f_ref, h_ref, n_rows):
        step = min(OUT_ROWS, n_rows)
        for r in range(0, n_rows, step):
            rows = pl.ds(r, step)
            mix = (_dot(rw_ref[rows, :], w_ref[0:RW_W, :])
                   + _dot(ret_ref[rows, :], w_ref[RW_W:RW_W + RET_W, :])
                   + _dot(ml_ref[rows, :], w_ref[RW_W + RET_W:, :]))
            y = _ln_rows(ALPHA * r_ref[rows, :] + mix, g_ref[...], b_ref[...])
            f_ref[rows, :] = y
            h_ref[rows, :] = y.astype(BF16)

    @pl.when(pl.program_id(0) == 0)
    def _():
        project(s_rw_ref, s_ret_ref, s_ml_ref, ress_ref, ofs_ref, obs_ref, ms)

    project(o_rw_ref, o_ret_ref, o_ml_ref, res_ref, of_ref, ob_ref, tm)


def _out_proj_ln(o, res, o_side, res_side, w, g, b, tm):
    m = res.shape[0]
    ms = res_side.shape[0]
    n = w.shape[1]
    rows = lambda width: pl.BlockSpec((tm, width), lambda i: (i, 0))
    const = lambda r, c: pl.BlockSpec((r, c), lambda i: (0, 0))
    widths = (RW_W, RET_W, ML_W)
    return pl.pallas_call(
        functools.partial(_out_proj_ln_body, tm=tm, ms=ms),
        grid=(m // tm,),
        in_specs=([rows(c) for c in widths] + [rows(n)] + [const(ms, c) for c in widths] + [const(ms, n)]
                  + [const(D_MODEL, n), const(1, n), const(1, n)]),
        out_specs=[rows(n), rows(n), const(ms, n), const(ms, n)],
        out_shape=[jax.ShapeDtypeStruct((m, n), F32), jax.ShapeDtypeStruct((m, n), BF16),
                   jax.ShapeDtypeStruct((ms, n), F32), jax.ShapeDtypeStruct((ms, n), BF16)],
        compiler_params=_cparams(("arbitrary",)),
        name="out_proj_ln",
    )(*o, res, *o_side, res_side, w, g.reshape(1, n), b.reshape(1, n))


RW_TB = 512
RW_GH = 4
RW_GW = RW_GH * RW_HD
RW_AHEAD = 8
RW_VEC_ROWS = 8


def _split3(x):
    hi = x.astype(BF16)
    r1 = x - hi.astype(F32)
    mid = r1.astype(BF16)
    lo = (r1 - mid.astype(F32)).astype(BF16)
    return hi, mid, lo


def _mm(a, b, dims):
    return lax.dot_general(a.astype(BF16), b.astype(BF16), (dims, ((), ())), preferred_element_type=F32)


_NN = ((1,), (0,))
_NT = ((1,), (1,))
_TN = ((0,), (0,))


def _exact_lhs_dot(a_bf16, b):
    hi, mid, lo = _split3(b)
    dg = lambda y: lax.dot_general(a_bf16, y, (_NN, ((), ())), preferred_element_type=F32)
    return dg(hi) + (dg(mid) + dg(lo))


def _exact_rhs_dot(a, b_bf16):
    hi, mid, lo = _split3(a)
    dg = lambda x: lax.dot_general(x, b_bf16, (_NN, ((), ())), preferred_element_type=F32)
    return dg(hi) + (dg(mid) + dg(lo))


def _seg_sum(a, seg_bf16):
    hi = a.astype(BF16)
    lo = (a - hi.astype(F32)).astype(BF16)
    gw = seg_bf16.shape[0]
    dg = lambda x: lax.dot_general(x, seg_bf16, (_NN, ((), ())), preferred_element_type=F32)
    groups = [dg(hi[:, c:c + gw]) + dg(lo[:, c:c + gw]) for c in range(0, a.shape[1], gw)]
    return groups[0] if len(groups) == 1 else jnp.concatenate(groups, axis=1)


def _rw_scan(r, lw, k, v, kk, a, st_scr, y_scr):
    L, TB, G = RW_CHUNK, RW_TB, RW_GW
    row = lax.broadcasted_iota(jnp.int32, (L, G), 0)
    col = lax.broadcasted_iota(jnp.int32, (L, G), 1) & (L - 1)
    strict, lower, eye = row > col, row >= col, (row == col).astype(F32)
    rg = lax.broadcasted_iota(jnp.int32, (G, G), 0) // RW_HD
    cg = lax.broadcasted_iota(jnp.int32, (G, G), 1) // RW_HD
    mask_bd = rg == cg
    rt = lax.broadcasted_iota(jnp.int32, (2 * L, 2 * L), 0)
    ct = lax.broadcasted_iota(jnp.int32, (2 * L, 2 * L), 1)
    tri = jnp.logical_and(rt >= ct, rt // L == ct // L).astype(BF16)
    bd = lambda x: jnp.where(mask_bd, jnp.concatenate([x.astype(BF16)] * RW_GH, axis=0), 0.0)
    cut = lambda x, b: x[b[0] * L:(b[0] + 1) * L, b[1] * G:(b[1] + 1) * G]

    cum = jnp.concatenate([_exact_lhs_dot(tri, lw[i:i + 2 * L, :]) for i in range(0, TB, 2 * L)], axis=0)
    e_neg = jnp.exp(-cum)
    ap = kk * a
    ap_h = ap * e_neg
    k_h = k * e_neg
    kk_t = kk * jnp.exp(cum - lw)
    r_t = r * jnp.exp(cum)

    lhs, n_m, m_a, m_kr, inv, mv, vk, a_end, decay = ({} for _ in range(9))
    n_groups = RW_HEADS // RW_GH

    def prep(chunks):
        blk = [(s, g) for s in chunks for g in range(n_groups)]
        for b in blk:
            lhs[b] = jnp.concatenate([cut(kk_t, b), cut(r_t, b)], axis=0)
            sc_a = _mm(lhs[b], bd(cut(ap_h, b)), _NT)
            sc_k = _mm(lhs[b], bd(cut(k_h, b)), _NT)
            n_m[b] = jnp.where(strict, sc_a[:L], 0.0)
            m_a[b] = jnp.where(lower, sc_a[L:], 0.0)
            m_kr[b] = jnp.concatenate([jnp.where(strict, sc_k[:L], 0.0), jnp.where(lower, sc_k[L:], 0.0)], axis=0)
        pw = {}
        for b in blk:
            inv[b] = eye - n_m[b]
            pw[b] = _mm(n_m[b], bd(n_m[b]), _NN)
        n_iter = int(math.log2(L)) - 1
        for j in range(n_iter):
            last = j == n_iter - 1
            for b in blk:
                lhs_j = inv[b] if last else jnp.concatenate([inv[b], pw[b]], axis=0)
                prod = _mm(lhs_j, bd(pw[b]), _NN)
                inv[b] = inv[b] + prod[:L]
                if not last:
                    pw[b] = prod[L:]
        for b in blk:
            s, g = b
            tot = cum[(s + 1) * L - 1:(s + 1) * L, g * G:(g + 1) * G]
            e_end = jnp.exp(tot - cut(cum, b))
            mv[b] = _mm(m_kr[b], bd(cut(v, b)), _NN)
            vk[b] = _mm(cut(v, b), cut(k, b) * e_end, _TN)
            a_end[b] = cut(ap, b) * e_end
            decay[b] = jnp.exp(tot)

    def apply(s):
        for g in range(n_groups):
            b = (s, g)
            st = st_scr[g]
            s_terms = _mm(lhs[b], st, _NT)
            u = _mm(inv[b], bd(s_terms[:L] + mv[b][:L]), _NN)
            y_scr[s * L:(s + 1) * L, g * G:(g + 1) * G] = s_terms[L:] + mv[b][L:] - _mm(m_a[b], bd(u), _NN)
            st_scr[g] = jnp.where(mask_bd, st * decay[b] + vk[b] - _mm(u, a_end[b], _TN), 0.0)

    n_chunks = TB // L
    prep(range(min(RW_AHEAD, n_chunks)))
    for s in range(n_chunks):
        apply(s)
        if s + RW_AHEAD < n_chunks:
            prep([s + RW_AHEAD])


def _softplus(z):
    return jnp.maximum(z, 0.0) + jnp.log(1.0 + jnp.exp(-jnp.abs(z)))


def _rwkv_fused_body(*refs, nc, has_vres):
    if has_vres:
        (p_ref, pv_ref, vf_ref, mu_ref, vec_ref, wa_ref, g2_ref, seg_ref, vmu_ref, v2_ref,
         o_ref, sf_ref, st_scr, prev_scr, y_scr, prevv_scr) = refs
    else:
        (p_ref, mu_ref, vec_ref, wa_ref, g2_ref, seg_ref,
         o_ref, vfo_ref, sf_ref, st_scr, prev_scr, y_scr) = refs
    TB, W, L = RW_TB, RW_W, RW_CHUNK
    c = pl.program_id(1)

    @pl.when(c == 0)
    def _():
        st_scr[...] = jnp.zeros_like(st_scr)
        prev_scr[...] = jnp.zeros_like(prev_scr)
        if has_vres:
            prevv_scr[...] = jnp.zeros_like(prevv_scr)

    first_row = lax.broadcasted_iota(jnp.int32, (TB, 1), 0) == 0

    def shift_mix(x, carry_ref, mu):
        prev = jnp.where(first_row, carry_ref[...], pltpu.roll(x, 1, 0))
        carry_ref[...] = x[TB - 1:TB, :]
        return x + (prev - x) * mu

    mixed = shift_mix(p_ref[0], prev_scr, mu_ref[...])
    r = mixed[:, 0:W]
    k = mixed[:, W:2 * W]
    v = mixed[:, 2 * W:3 * W]
    xwa = mixed[:, 3 * W:3 * W + LANES]
    xg = mixed[:, 3 * W + LANES:3 * W + 2 * LANES]
    vec = vec_ref[...]
    w0, a0, kk_s, ka, rk, lnx_g, lnx_b, v0 = (vec[i:i + 1, :] for i in range(RW_VEC_ROWS))
    seg = seg_ref[...]
    wa = wa_ref[...]
    w_lora = _dot(jnp.tanh(xwa).astype(BF16), wa[:, 0:W])
    a_lora = _dot(xwa.astype(BF16), wa[:, W:2 * W])
    lw = -jnp.exp(-_softplus(-(w0 + w_lora)) - 0.5)
    a = jax.nn.sigmoid(a0 + a_lora)
    g = _dot(jax.nn.sigmoid(xg).astype(BF16), g2_ref[...])
    if has_vres:
        xv = shift_mix(pv_ref[0], prevv_scr, vmu_ref[...])
        v = v + (vf_ref[0] - v) * jax.nn.sigmoid(v0 + _dot(xv.astype(BF16), v2_ref[...]))
    else:
        vfo_ref[0] = v
    kk = k * kk_s
    kk = kk * lax.rsqrt(jnp.maximum(_seg_sum(kk * kk, seg), 1e-24))
    k = k * (1.0 + (a - 1.0) * ka)

    _rw_scan(r, lw, k, v, kk, a, st_scr, y_scr)

    y = y_scr[...]
    inv_n = 1.0 / RW_HD
    y_mu = _seg_sum(y, seg) * inv_n
    yc = y - y_mu
    y_var = _seg_sum(yc * yc, seg) * inv_n
    y = yc * lax.rsqrt(y_var + RW_GN_EPS) * lnx_g + lnx_b
    bonus = _seg_sum(r * k * rk, seg) * v
    o_ref[0] = ((y + bonus) * g).astype(BF16)

    @pl.when(c == nc - 1)
    def _():
        sf_ref[0] = st_scr[...]


def _rwkv_prompt(p3, lp, v_first):
    b, t, _ = p3.shape
    nc = t // RW_TB
    has_vres = v_first is not None
    ng = RW_HEADS // RW_GH
    zpad = jnp.zeros((RW_LORA_W, RW_W), F32)
    wa = jnp.concatenate([jnp.concatenate([lp['rw_w2'], zpad], 0), jnp.concatenate([zpad, lp['rw_a2']], 0)], 1)
    vec = jnp.stack([lp['rw_w0'], lp['rw_a0'], lp['rw_kk'], lp['rw_ka'], lp['rw_rk'], lp['rw_lnx_g'], lp['rw_lnx_b'],
                     lp['rw_v0'] if has_vres else jnp.zeros((RW_W,), F32)])
    hid = jnp.arange(RW_GW) // RW_HD
    seg = (hid[:, None] == hid[None, :]).astype(BF16)
    full = lambda shape: pl.BlockSpec(shape, lambda i, j: (0,) * len(shape))
    seq = lambda w, blk: pl.BlockSpec((1, RW_TB, w), lambda i, j: (i, j, blk))
    in_specs = [seq(RW_P, 0)]
    args = [p3]
    if has_vres:
        in_specs += [seq(LANES, P_V1 // LANES), seq(RW_W, 0)]
        args += [p3, v_first]
    in_specs += [full((1, RW_P)), full((RW_VEC_ROWS, RW_W)), full((LANES, 2 * RW_W)), full((RW_LORA_G, RW_W)),
                 full((RW_GW, RW_GW))]
    args += [lp['rw_mu'].reshape(1, RW_P), vec, wa.astype(BF16), lp['rw_g2'].astype(BF16), seg]
    if has_vres:
        in_specs += [full((1, LANES)), full((LANES, RW_W))]
        args += [jnp.pad(lp['rw_vmu'], (0, LANES - RW_LORA_V)).reshape(1, LANES),
                 jnp.pad(lp['rw_v2'], ((0, LANES - RW_LORA_V), (0, 0))).astype(BF16)]
    out_specs = [seq(RW_W, 0)]
    out_shape = [jax.ShapeDtypeStruct((b, t, RW_W), BF16)]
    if not has_vres:
        out_specs.append(seq(RW_W, 0))
        out_shape.append(jax.ShapeDtypeStruct((b, t, RW_W), F32))
    out_specs.append(pl.BlockSpec((1, ng, RW_GW, RW_GW), lambda i, j: (i, 0, 0, 0)))
    out_shape.append(jax.ShapeDtypeStruct((b, ng, RW_GW, RW_GW), F32))
    scratch = [pltpu.VMEM((ng, RW_GW, RW_GW), F32), pltpu.VMEM((1, RW_P), F32), pltpu.VMEM((RW_TB, RW_W), F32)]
    if has_vres:
        scratch.append(pltpu.VMEM((1, LANES), F32))
    outs = pl.pallas_call(
        functools.partial(_rwkv_fused_body, nc=nc, has_vres=has_vres),
        grid=(b, nc),
        in_specs=in_specs,
        out_specs=out_specs,
        out_shape=out_shape,
        scratch_shapes=scratch,
        compiler_params=_cparams(("parallel", "arbitrary")),
        name="rwkv_fused",
    )(*args)
    if has_vres:
        o, st_bd = outs
    else:
        o, v_first, st_bd = outs
    st5 = st_bd.reshape(b, ng, RW_GH, RW_HD, RW_GH, RW_HD)
    s_fin = jnp.stack([st5[:, :, h, :, h, :] for h in range(RW_GH)], axis=2)
    s_fin = s_fin.reshape(b, RW_HEADS, RW_HD, RW_HD).transpose(0, 1, 3, 2)
    return o, v_first, s_fin


def _ret_log_gamma(h):
    return math.log1p(-(2.0 ** (-5.0 - h)))


def _rotary_tables(pos, heads, dk):
    half = dk // 2
    inv = ROPE_BASE ** (-jnp.arange(half, dtype=F32) / half)
    ang = pos.astype(F32)[:, None] * inv[None, :]
    cos = jnp.tile(jnp.concatenate([jnp.cos(ang), jnp.cos(ang)], -1), (1, heads))
    sin = jnp.tile(jnp.concatenate([-jnp.sin(ang), jnp.sin(ang)], -1), (1, heads))
    lane = jnp.arange(LANES)
    perm = (lane[:, None] == (lane[None, :] ^ half)).astype(BF16)
    return cos, sin, perm


MIX_RB = 4


def _per_sequence(body):
    def batched(*refs, batched_refs, **kw):
        for bb in range(MIX_RB):
            one = [r.at[pl.ds(bb, 1)] if mode == 'keep' else (r.at[bb] if mode == 'drop' else r)
                   for r, mode in zip(refs, batched_refs)]
            body(*one, **kw)
    return batched


def _ret_fused_body(p_ref, cos_ref, sin_ref, perm_ref, o_ref, sf_ref, s_scr, *, nc):
    L, DK, DV, H = CHUNK, RET_QK, RET_V, RET_HEADS
    nq = H * DK
    c = pl.program_id(1)

    @pl.when(c == 0)
    def _():
        s_scr[...] = jnp.zeros_like(s_scr)

    cos = cos_ref[...]
    sin = sin_ref[...]
    perm = perm_ref[...]
    swap = lambda x: jnp.concatenate(
        [_exact_rhs_dot(x[:, c:c + LANES], perm) for c in range(0, nq, LANES)], axis=1)
    rot = lambda x: x * cos + swap(x) * sin
    q_all = rot(p_ref[0, :, 0:nq])
    k_all = rot(p_ref[0, :, nq:2 * nq]) * (DK ** -0.5)
    row = lax.broadcasted_iota(jnp.int32, (L, L), 0)
    col = lax.broadcasted_iota(jnp.int32, (L, L), 1)
    rel = (row - col).astype(F32)
    idx = lax.broadcasted_iota(jnp.int32, (L, 1), 0).astype(F32)
    vs = lambda h: p_ref[0, :, 2 * nq + h * DV:2 * nq + (h + 1) * DV].astype(BF16)
    hd = [dict() for _ in range(H)]
    for h, t in enumerate(hd):
        lg = _ret_log_gamma(h)
        q = q_all[:, h * DK:(h + 1) * DK]
        k = k_all[:, h * DK:(h + 1) * DK]
        t['s_prev'] = s_scr[h]
        t['dmask'] = jnp.where(rel >= 0, jnp.exp(jnp.maximum(rel, 0.0) * lg), 0.0)
        t['qk'] = _dot_nt(q.astype(BF16), k.astype(BF16))
        q_dec = q * jnp.exp((idx + 1.0) * lg)
        t['qs'] = _dot(q_dec.astype(BF16), t['s_prev'].astype(BF16))
        k_end = k * jnp.exp((L - 1.0 - idx) * lg)
        t['kv'] = _dot_tn(k_end.astype(BF16), vs(h))
    for h, t in enumerate(hd):
        y = _dot((t['qk'] * t['dmask']).astype(BF16), vs(h)) + t['qs']
        y = y * lax.rsqrt(jnp.mean(y * y, -1, keepdims=True) + NORM_EPS)
        gate = p_ref[0, :, 2 * nq + RET_W + h * DV:2 * nq + RET_W + (h + 1) * DV]
        o_ref[0, :, h * DV:(h + 1) * DV] = (gate * jax.nn.sigmoid(gate) * y).astype(BF16)
    for h, t in enumerate(hd):
        s_scr[h] = math.exp(L * _ret_log_gamma(h)) * t['s_prev'] + t['kv']

    @pl.when(c == nc - 1)
    def _():
        sf_ref[0] = s_scr[...]


def _ret_prompt(p3, pos):
    b, t, _ = p3.shape
    L = CHUNK
    nc = t // L
    nq = RET_HEADS * RET_QK
    cos, sin, perm = _rotary_tables(pos, RET_HEADS, RET_QK)
    tab = pl.BlockSpec((L, nq), lambda i, j: (j, 0))
    st = pl.BlockSpec((MIX_RB, RET_HEADS, RET_QK, RET_V), lambda i, j: (i, 0, 0, 0))
    modes = ('keep', None, None, None, 'keep', 'keep', 'drop')
    return pl.pallas_call(
        functools.partial(_per_sequence(_ret_fused_body), batched_refs=modes, nc=nc),
        grid=(b // MIX_RB, nc),
        in_specs=[pl.BlockSpec((MIX_RB, L, P_MAIN), lambda i, j: (i, j, P_RET // P_MAIN)), tab, tab,
                  pl.BlockSpec((LANES, LANES), lambda i, j: (0, 0))],
        out_specs=[pl.BlockSpec((MIX_RB, L, RET_W), lambda i, j: (i, j, 0)), st],
        out_shape=[jax.ShapeDtypeStruct((b, t, RET_W), BF16),
                   jax.ShapeDtypeStruct((b, RET_HEADS, RET_QK, RET_V), F32)],
        scratch_shapes=[pltpu.VMEM((MIX_RB, RET_HEADS, RET_QK, RET_V), F32)],
        compiler_params=_cparams(("parallel", "arbitrary")),
        name="ret_fused",
    )(p3, cos, sin, perm)


ML_HPAD = 8
ML_SEL_ROWS = 16


def _ml_fused_body(p_ref, gate_ref, bias_ref, norm_ref, sel_ref, o_ref, cf_ref, nf_ref, mf_ref,
                   c_scr, n_scr, m_scr, *, nc):
    L, DK, DV, H = CHUNK, ML_QK, ML_V, ML_HEADS
    nq = H * DK
    ci = pl.program_id(1)

    @pl.when(ci == 0)
    def _():
        c_scr[...] = jnp.zeros_like(c_scr)
        n_scr[...] = jnp.zeros_like(n_scr)
        m_scr[...] = jnp.zeros_like(m_scr)

    row = lax.broadcasted_iota(jnp.int32, (L, L), 0)
    col = lax.broadcasted_iota(jnp.int32, (L, L), 1)
    causal = row >= col
    tri = causal.astype(BF16)
    capped = ML_GATE_CAP * jnp.tanh((gate_ref[0] + bias_ref[...]) * (1.0 / ML_GATE_CAP))
    lane = lax.broadcasted_iota(jnp.int32, (L, LANES), 1)
    g = jnp.where(lane < H, capped, jnp.where(lane < 2 * H, -_softplus(-capped), 0.0))
    g_rep = _exact_rhs_dot(g[:, :ML_SEL_ROWS], sel_ref[...])
    b_rep_all = _exact_lhs_dot(tri, g_rep[:, H * LANES:])
    g_t = g.T
    cum_t = _exact_lhs_dot(tri, g).T
    cm_all = g_rep[:, :H * LANES] - b_rep_all
    row_id = lax.broadcasted_iota(jnp.int32, (L, 1), 0)
    shift = 1
    while shift < L:
        cm_all = jnp.maximum(cm_all, jnp.where(row_id >= shift, pltpu.roll(cm_all, shift, 0), -jnp.inf))
        shift *= 2
    ones = jnp.ones((L, LANES), BF16)
    mean_w = jnp.full((DV, LANES), 1.0 / DV, BF16)
    m_all = m_scr[...]
    hd = [dict() for _ in range(H)]
    for h, t in enumerate(hd):
        hs = slice(h * LANES, (h + 1) * LANES)
        q = p_ref[0, :, h * DK:(h + 1) * DK].astype(BF16)
        k = p_ref[0, :, nq + h * DK:nq + (h + 1) * DK] * (DK ** -0.5)
        t['v1'] = jnp.concatenate([p_ref[0, :, 2 * nq + h * DV:2 * nq + (h + 1) * DV].astype(BF16), ones], axis=1)
        ig_rep = g_rep[:, hs]
        b_rep = b_rep_all[:, hs]
        ig_row = g_t[h:h + 1, :]
        b_row = cum_t[H + h:H + h + 1, :]
        b_tot = b_rep[L - 1:L, :]
        m_prev = m_all[h:h + 1, :]
        t['c_prev'] = c_scr[h]
        t['n_prev'] = n_scr[h]
        t['m_new'] = jnp.maximum(b_tot + m_prev, jnp.max(b_tot - b_rep + ig_rep, axis=0, keepdims=True))
        t['dec'] = jnp.exp(b_tot + m_prev - t['m_new'])
        kw = k * jnp.exp((b_tot - b_rep + ig_rep - t['m_new'])[:, :DK])
        t['kvn'] = _dot_tn(kw.astype(BF16), t['v1'])
        inter = b_rep + m_prev
        t['m_i'] = b_rep + jnp.maximum(cm_all[:, hs], m_prev)
        t['e'] = jnp.exp(jnp.where(causal, (b_rep - t['m_i']) - b_row + ig_row, -jnp.inf))
        t['sc'] = jnp.exp(inter - t['m_i'])
        t['qk'] = _dot_nt(q, k.astype(BF16))
        cn = jnp.concatenate([t['c_prev'], t['n_prev']], axis=1).astype(BF16)
        t['qcn'] = _dot(q, cn)
    for h, t in enumerate(hd):
        nd = _dot((t['qk'] * t['e']).astype(BF16), t['v1'])
        num = nd[:, :DV] + t['sc'] * t['qcn'][:, :DV]
        den = nd[:, DV:] + t['sc'] * t['qcn'][:, DV:]
        hid = num / jnp.maximum(jnp.abs(den), jnp.exp(-t['m_i']))
        hid = hid * lax.rsqrt(_seg_sum(hid * hid, mean_w) + NORM_EPS)
        og = p_ref[0, :, 2 * nq + ML_W + h * DV:2 * nq + ML_W + (h + 1) * DV]
        o_ref[0, :, h * DV:(h + 1) * DV] = (jax.nn.sigmoid(og) * (hid * norm_ref[:, h * DV:(h + 1) * DV])).astype(BF16)
    for h, t in enumerate(hd):
        c_scr[h] = t['dec'] * t['c_prev'] + t['kvn'][:, :DV]
        n_scr[h] = t['dec'] * t['n_prev'] + t['kvn'][:, DV:]
        m_scr[h:h + 1, :] = t['m_new']

    @pl.when(ci == nc - 1)
    def _():
        cf_ref[0] = c_scr[...]
        nf_ref[0] = n_scr[...]
        mf_ref[0] = m_scr[...]


def _ml_prompt(p3, lp):
    b, t, _ = p3.shape
    L = CHUNK
    nc = t // L
    bias = jnp.pad(jnp.concatenate([lp['ml_ib'], lp['ml_fb']]), (0, LANES - 2 * ML_HEADS)).reshape(1, LANES)
    n_rep = 2 * ML_HEADS * LANES
    sel = (jnp.arange(ML_SEL_ROWS)[:, None] == jnp.arange(n_rep)[None, :] // LANES).astype(BF16)
    vs = pl.BlockSpec((MIX_RB, L, ML_W), lambda i, j: (i, j, 0))
    cs = pl.BlockSpec((MIX_RB, ML_HEADS, ML_QK, ML_V), lambda i, j: (i, 0, 0, 0))
    ns = pl.BlockSpec((MIX_RB, ML_HEADS, ML_QK, LANES), lambda i, j: (i, 0, 0, 0))
    ms = pl.BlockSpec((MIX_RB, ML_HPAD, LANES), lambda i, j: (i, 0, 0))
    modes = ('keep', 'keep', None, None, None, 'keep', 'keep', 'keep', 'keep', 'drop', 'drop', 'drop')
    o, c_f, n_f, m_f = pl.pallas_call(
        functools.partial(_per_sequence(_ml_fused_body), batched_refs=modes, nc=nc),
        grid=(b // MIX_RB, nc),
        in_specs=[pl.BlockSpec((MIX_RB, L, P_MAIN), lambda i, j: (i, j, P_ML // P_MAIN)),
                  pl.BlockSpec((MIX_RB, L, LANES), lambda i, j: (i, j, P_GATE // LANES)),
                  pl.BlockSpec((1, LANES), lambda i, j: (0, 0)),
                  pl.BlockSpec((1, ML_W), lambda i, j: (0, 0)),
                  pl.BlockSpec((ML_SEL_ROWS, n_rep), lambda i, j: (0, 0))],
        out_specs=[vs, cs, ns, ms],
        out_shape=[jax.ShapeDtypeStruct((b, t, ML_W), BF16),
                   jax.ShapeDtypeStruct((b, ML_HEADS, ML_QK, ML_V), F32),
                   jax.ShapeDtypeStruct((b, ML_HEADS, ML_QK, LANES), F32),
                   jax.ShapeDtypeStruct((b, ML_HPAD, LANES), F32)],
        scratch_shapes=[pltpu.VMEM((MIX_RB, ML_HEADS, ML_QK, ML_V), F32),
                        pltpu.VMEM((MIX_RB, ML_HEADS, ML_QK, LANES), F32),
                        pltpu.VMEM((MIX_RB, ML_HPAD, LANES), F32)],
        compiler_params=_cparams(("parallel", "arbitrary")),
        name="ml_fused",
    )(p3, p3, bias, lp['ml_norm'].reshape(1, ML_W), sel)
    return o, c_f, n_f[..., 0], m_f[:, :ML_HEADS, 0]


def _to_cols(x):
    b, c = x.shape
    cols = x.reshape(b // DEC_TB, DEC_TB, c).transpose(0, 2, 1)
    return jnp.pad(cols, ((0, 0), (0, 0), (0, DEC_TB)))


def _col_selector():
    j = jnp.arange(2 * DEC_TB)[:, None]
    return (j == jnp.arange(DEC_TB * LANES)[None, :] // LANES).astype(BF16)


def _rwkv_step_body(w_ref, ap_ref, k_ref, kk_ref, r_ref, v_ref, s_ref, y_ref, so_ref):
    N = RW_HD
    v = v_ref[0]

    def sa_step(i, acc):
        return acc + kk_ref[0, pl.ds(i, 1), :] * s_ref[i]

    sa = lax.fori_loop(0, N, sa_step, jnp.zeros_like(v), unroll=8)

    def upd_step(i, y):
        row = lambda ref: ref[0, pl.ds(i, 1), :]
        s_new = row(w_ref) * s_ref[i] - row(ap_ref) * sa + row(k_ref) * v
        so_ref[i] = s_new
        return y + row(r_ref) * s_new

    y_ref[0] = lax.fori_loop(0, N, upd_step, jnp.zeros_like(v), unroll=8)


def _rwkv_step(wdec, ap, k, kk, r, v, s_all, layer):
    b, w = v.shape
    heads = lambda x: x.reshape(b, RW_HEADS, RW_HD).transpose(1, 2, 0)
    vec = pl.BlockSpec((1, RW_HD, b), lambda h: (h, 0, 0))
    y, s_new = pl.pallas_call(
        _rwkv_step_body,
        grid=(RW_HEADS,),
        in_specs=[vec] * 6 + [pl.BlockSpec((None, None, RW_HD, RW_HD, b), lambda h: (layer, h, 0, 0, 0))],
        out_specs=[vec, pl.BlockSpec((None, RW_HD, RW_HD, b), lambda h: (h, 0, 0, 0))],
        out_shape=[jax.ShapeDtypeStruct((RW_HEADS, RW_HD, b), F32),
                   jax.ShapeDtypeStruct((RW_HEADS, RW_HD, RW_HD, b), F32)],
        compiler_params=_cparams(("parallel",)),
        name="rwkv_step",
    )(heads(wdec), heads(ap), heads(k), heads(kk), heads(r), heads(v), s_all)
    return y.transpose(2, 0, 1).reshape(b, w), s_new


def _ret_step_body(kc_ref, q_ref, k_ref, v_ref, esel_ref, seg_ref, s_ref, y_ref, so_ref):
    DK, DV = RET_QK, RET_V
    q = q_ref[...]
    qk_v = _exact_rhs_dot(q * k_ref[...], seg_ref[...])
    row_id = lax.broadcasted_iota(jnp.int32, (DEC_TB, 1), 0)
    for h in range(RET_HEADS):
        gamma = math.exp(_ret_log_gamma(h))
        ks = slice(h * DK, (h + 1) * DK)
        vs = slice(h * DV, (h + 1) * DV)
        k_rep = _exact_rhs_dot(kc_ref[0, ks, :], esel_ref[...])
        q_h = q[:, ks].astype(BF16)
        v_h = v_ref[:, vs]
        qs = jnp.zeros((DEC_TB, DV), F32)
        for j in range(DEC_TB):
            s = s_ref[j, h]
            so_ref[j, h] = gamma * s + k_rep[:, j * LANES:(j + 1) * LANES] * v_h[j:j + 1, :]
            qs = jnp.where(row_id == j, _dot(q_h, s.astype(BF16)), qs)
        y_ref[:, vs] = qk_v[:, vs] * v_h + gamma * qs


def _ret_step(q, k, v, s_all, layer):
    b = q.shape[0]
    nqk = RET_HEADS * RET_QK
    cols = pl.BlockSpec((1, nqk, 2 * DEC_TB), lambda i: (i, 0, 0))
    qk_rows = pl.BlockSpec((DEC_TB, nqk), lambda i: (i, 0))
    rows = pl.BlockSpec((DEC_TB, RET_W), lambda i: (i, 0))
    esel = pl.BlockSpec((2 * DEC_TB, DEC_TB * LANES), lambda i: (0, 0))
    seg = (jnp.arange(nqk)[:, None] // RET_QK == jnp.arange(RET_W)[None, :] // RET_V).astype(BF16)
    st = pl.BlockSpec((DEC_TB, RET_HEADS, RET_QK, RET_V), lambda i: (i, 0, 0, 0))
    st_in = pl.BlockSpec((None, DEC_TB, RET_HEADS, RET_QK, RET_V), lambda i: (layer, i, 0, 0, 0))
    return pl.pallas_call(
        _ret_step_body,
        grid=(b // DEC_TB,),
        in_specs=[cols, qk_rows, qk_rows, rows, esel, pl.BlockSpec((nqk, RET_W), lambda i: (0, 0)), st_in],
        out_specs=[rows, st],
        out_shape=[jax.ShapeDtypeStruct((b, RET_W), F32), jax.ShapeDtypeStruct(s_all.shape[1:], F32)],
        compiler_params=_cparams(("parallel",)),
        name="ret_step",
    )(_to_cols(k), q, k, v, _col_selector(), seg, s_all)


def _ml_step_body(kc_ref, q_ref, k_ref, v_ref, ig_ref, lf_ref, esel_ref, hsel_ref, hsel_k_ref, seg_ref,
                  c_ref, n_ref, m_ref, h_ref, co_ref, no_ref, mo_ref):
    DK, DV, H = ML_QK, ML_V, ML_HEADS
    ig, lf, m_prev = ig_ref[...], lf_ref[...], m_ref[...]
    m_new = jnp.maximum(lf + m_prev, ig)
    dec = jnp.exp(lf + m_prev - m_new)
    wgt = jnp.exp(ig - m_new)
    mo_ref[...] = m_new
    hsel = hsel_ref[...]
    dec_v = _exact_rhs_dot(dec, hsel)
    wgt_v = _exact_rhs_dot(wgt, hsel)
    floor_v = _exact_rhs_dot(jnp.exp(-m_new), hsel)
    q, k, n_prev = q_ref[...], k_ref[...], n_ref[...]
    no_ref[...] = _exact_rhs_dot(dec, hsel_k_ref[...]) * n_prev + k * _exact_rhs_dot(wgt, hsel_k_ref[...])
    s_v = _exact_rhs_dot(q * k, seg_ref[...]) * wgt_v
    den_v = s_v + dec_v * _exact_rhs_dot(q * n_prev, seg_ref[...])
    row_id = lax.broadcasted_iota(jnp.int32, (DEC_TB, 1), 0)
    for h in range(H):
        ks = slice(h * DK, (h + 1) * DK)
        vs = slice(h * DV, (h + 1) * DV)
        k_rep = _exact_rhs_dot(kc_ref[0, ks, :], esel_ref[...])
        q_h = q[:, ks].astype(BF16)
        v_h = v_ref[:, vs]
        kv_scale = wgt_v[:, vs] * v_h
        qc = jnp.zeros((DEC_TB, DV), F32)
        for j in range(DEC_TB):
            c_prev = c_ref[j, h]
            co_ref[j, h] = dec_v[j:j + 1, vs] * c_prev + k_rep[:, j * LANES:(j + 1) * LANES] * kv_scale[j:j + 1, :]
            qc = jnp.where(row_id == j, _dot(q_h, c_prev.astype(BF16)), qc)
        num = s_v[:, vs] * v_h + dec_v[:, vs] * qc
        h_ref[:, vs] = num / jnp.maximum(jnp.abs(den_v[:, vs]), floor_v[:, vs])


def _ml_step(q, k, v, ig, lf, c_all, layer, n0, m0):
    b = q.shape[0]
    nqk = ML_HEADS * ML_QK
    pad_h = lambda x: jnp.pad(x, ((0, 0), (0, LANES - ML_HEADS)))
    head = jnp.arange(LANES)[:, None]
    hsel = (head == jnp.arange(ML_W)[None, :] // ML_V).astype(BF16)
    hsel_k = (head == jnp.arange(nqk)[None, :] // ML_QK).astype(BF16)
    seg = (jnp.arange(nqk)[:, None] // ML_QK == jnp.arange(ML_W)[None, :] // ML_V).astype(BF16)
    const = lambda r, c: pl.BlockSpec((r, c), lambda i: (0, 0))
    cols = pl.BlockSpec((1, nqk, 2 * DEC_TB), lambda i: (i, 0, 0))
    qk_rows = pl.BlockSpec((DEC_TB, nqk), lambda i: (i, 0))
    rows = pl.BlockSpec((DEC_TB, ML_W), lambda i: (i, 0))
    sc = pl.BlockSpec((DEC_TB, LANES), lambda i: (i, 0))
    cs = pl.BlockSpec((DEC_TB, ML_HEADS, ML_QK, ML_V), lambda i: (i, 0, 0, 0))
    cs_in = pl.BlockSpec((None, DEC_TB, ML_HEADS, ML_QK, ML_V), lambda i: (layer, i, 0, 0, 0))
    hm, c_new, n_new, m_new = pl.pallas_call(
        _ml_step_body,
        grid=(b // DEC_TB,),
        in_specs=[cols, qk_rows, qk_rows, rows, sc, sc, const(2 * DEC_TB, DEC_TB * LANES), const(LANES, ML_W),
                  const(LANES, nqk), const(nqk, ML_W), cs_in, qk_rows, sc],
        out_specs=[rows, cs, qk_rows, sc],
        out_shape=[jax.ShapeDtypeStruct((b, ML_W), F32), jax.ShapeDtypeStruct(c_all.shape[1:], F32),
                   jax.ShapeDtypeStruct((b, nqk), F32), jax.ShapeDtypeStruct((b, LANES), F32)],
        compiler_params=_cparams(("parallel",)),
        name="ml_step",
    )(_to_cols(k), q, k, v, pad_h(ig), pad_h(lf), _col_selector(), hsel, hsel_k, seg,
      c_all, n0.reshape(b, nqk), pad_h(m0))
    return hm, c_new, n_new.reshape(n0.shape), m_new[:, :ML_HEADS]


def _heads(a, h):
    return a.reshape(a.shape[:-1] + (h, a.shape[-1] // h))


def _shift_prev(p, prev_row):
    return jnp.concatenate([prev_row[:, None, :], p[:, :-1]], axis=1)


def _rotary(x, pos):
    half = x.shape[-1] // 2
    inv = ROPE_BASE ** (-jnp.arange(half, dtype=F32) / half)
    ang = pos.astype(F32)[:, None] * inv[None, :]
    cos = jnp.cos(ang)[None, :, None, :]
    sin = jnp.sin(ang)[None, :, None, :]
    x1, x2 = x[..., :half], x[..., half:]
    return jnp.concatenate([x1 * cos - x2 * sin, x1 * sin + x2 * cos], -1)


def _small_matmul(x, w):
    lead = x.shape[:-1]
    kdim, n = w.shape
    x2 = x.reshape(-1, kdim)
    m = x2.shape[0]
    kp = -(-kdim // LANES) * LANES
    npad = -(-n // LANES) * LANES
    x2 = jnp.pad(x2.astype(BF16), ((0, 0), (0, kp - kdim)))
    w2 = jnp.pad(w.astype(BF16), ((0, kp - kdim), (0, npad - n)))
    tm = 1024 if m % 1024 == 0 else m
    out = _matmul(x2, w2, tm, npad)
    return out[:, :n].reshape(lead + (n,))


def _mix_prompt(p, pos, v_first, lp):
    o_rw, v_first, s_new = _rwkv_prompt(p, lp, v_first)
    o_ret, r_new = _ret_prompt(p, pos)
    o_ml, c_new, n_new, m_new = _ml_prompt(p, lp)
    return (o_rw, o_ret, o_ml), v_first, (s_new, r_new, c_new, n_new, m_new)


def _mix_sample(p, pos, v_first, st, lp, prev_row):
    bsz, t_len, _ = p.shape

    p_rw = p[..., :RW_P]
    mixed = p_rw + (_shift_prev(p_rw, prev_row[:, :RW_P]) - p_rw) * lp['rw_mu']
    sizes = np.cumsum([RW_W, RW_W, RW_W, RW_LORA_W, RW_LORA_A, RW_LORA_G])[:-1]
    r, k, v, xw, xa, xg = jnp.split(mixed, [int(s) for s in sizes], axis=-1)
    lora_in = [jnp.tanh(xw), xa, jax.nn.sigmoid(xg)]
    lora_w = [lp['rw_w2'], lp['rw_a2'], lp['rw_g2']]
    if v_first is not None:
        pv = p[..., P_V1:P_V1 + RW_LORA_V]
        lora_in.append(pv + (_shift_prev(pv, prev_row[:, P_V1:P_V1 + RW_LORA_V]) - pv) * lp['rw_vmu'])
        lora_w.append(lp['rw_v2'])
    lora = _small_matmul(jnp.concatenate(lora_in, -1), jax.scipy.linalg.block_diag(*lora_w))
    w = -jax.nn.softplus(-(lp['rw_w0'] + lora[..., :RW_W])) - 0.5
    a = jax.nn.sigmoid(lp['rw_a0'] + lora[..., RW_W:2 * RW_W])
    g = lora[..., 2 * RW_W:3 * RW_W]
    if v_first is None:
        v_first = v
    else:
        v = v + (v_first - v) * jax.nn.sigmoid(lp['rw_v0'] + lora[..., 3 * RW_W:])
    kk = _heads(k * lp['rw_kk'], RW_HEADS)
    kk = kk * lax.rsqrt(jnp.maximum(jnp.sum(jnp.square(kk), -1, keepdims=True), 1e-24))
    kk = kk.reshape(bsz, t_len, RW_W)
    k = k * (1.0 + (a - 1.0) * lp['rw_ka'])
    lw = -jnp.exp(w)
    y, s_new = _rwkv_step(jnp.exp(lw)[:, 0], (kk * a)[:, 0], k[:, 0], kk[:, 0], r[:, 0], v[:, 0],
                          st['rw_wkv_t'], st['layer'])
    y = _heads(y[:, None, :], RW_HEADS)
    y_mu = jnp.mean(y, -1, keepdims=True)
    y_var = jnp.mean(jnp.square(y - y_mu), -1, keepdims=True)
    y = ((y - y_mu) * lax.rsqrt(y_var + RW_GN_EPS)).reshape(bsz, t_len, RW_W)
    y = y * lp['rw_lnx_g'] + lp['rw_lnx_b']
    rh, kh, vh = (_heads(u, RW_HEADS) for u in (r, k, v))
    bonus = jnp.sum(rh * kh * _heads(lp['rw_rk'], RW_HEADS), -1, keepdims=True) * vh
    o_rw = ((y + bonus.reshape(bsz, t_len, RW_W)) * g).astype(BF16)

    nqk = RET_HEADS * RET_QK
    p_ret = p[..., P_RET:P_RET + P_MAIN]
    qr, kr, vr, gr = (p_ret[..., :nqk], p_ret[..., nqk:2 * nqk],
                      p_ret[..., 2 * nqk:2 * nqk + RET_W], p_ret[..., 2 * nqk + RET_W:])
    qh = _rotary(_heads(qr, RET_HEADS), pos).reshape(bsz, t_len, nqk)
    khr = (_rotary(_heads(kr, RET_HEADS), pos) * (RET_QK ** -0.5)).reshape(bsz, t_len, nqk)
    yr, r_new = _ret_step(qh[:, 0], khr[:, 0], vr[:, 0], st['ret_all'], st['layer'])
    yr = _heads(yr[:, None, :], RET_HEADS)
    yr = yr * lax.rsqrt(jnp.mean(jnp.square(yr), -1, keepdims=True) + NORM_EPS)
    o_ret = (jax.nn.silu(gr) * yr.reshape(bsz, t_len, RET_W)).astype(BF16)

    nqk = ML_HEADS * ML_QK
    p_ml = p[..., P_ML:P_ML + P_MAIN]
    qm, km, vm, om = (p_ml[..., :nqk], p_ml[..., nqk:2 * nqk],
                      p_ml[..., 2 * nqk:2 * nqk + ML_W], p_ml[..., 2 * nqk + ML_W:])
    im = p[..., P_GATE:P_GATE + ML_HEADS]
    fm = p[..., P_GATE + ML_HEADS:P_GATE + 2 * ML_HEADS]
    ig = ML_GATE_CAP * jnp.tanh((im + lp['ml_ib']) / ML_GATE_CAP)
    lf = jax.nn.log_sigmoid(ML_GATE_CAP * jnp.tanh((fm + lp['ml_fb']) / ML_GATE_CAP))
    km = km * (ML_QK ** -0.5)
    hm, c_new, n_new, m_new = _ml_step(qm[:, 0], km[:, 0], vm[:, 0], ig[:, 0], lf[:, 0],
                                       st['ml_c_all'], st['layer'], st['ml_n'], st['ml_m'])
    hm = _heads(hm[:, None, :], ML_HEADS)
    hm = hm * lax.rsqrt(jnp.mean(jnp.square(hm), -1, keepdims=True) + NORM_EPS)
    o_ml = (jax.nn.sigmoid(om) * (hm.reshape(bsz, t_len, ML_W) * lp['ml_norm'])).astype(BF16)

    return (o_rw, o_ret, o_ml), v_first, (s_new, r_new, c_new, n_new, m_new)


_T_V1 = P_V1 // LANES
_T_GATE = P_GATE // LANES
_T_RET = P_RET // LANES
_T_SHIFT = (P_RET - RW_P) // LANES
_T_SRC_GATE = (RW_P + RET_P + 2 * ML_HEADS * ML_QK + 2 * ML_W) // LANES


def _pack_body(w_ref, v1_ref, *o_refs):
    j = pl.program_id(0)
    row = lax.broadcasted_iota(jnp.int32, (LANES, 1), 0)
    spare = jnp.logical_and(j > _T_GATE, j < _T_RET)
    for l, o_ref in enumerate(o_refs):
        w = w_ref[:, l, :]
        gates = jnp.where(row < 2 * ML_HEADS, w, 0.0)
        out = jnp.where(j == _T_V1, v1_ref[l], jnp.where(j == _T_GATE, gates, jnp.where(spare, 0.0, w)))
        o_ref[...] = out.astype(BF16)


def _pack_w_in(w_in, rw_v1):
    depth, d, _ = w_in.shape
    w_t = w_in.transpose(2, 0, 1)
    v1_t = jnp.pad(rw_v1.transpose(0, 2, 1), ((1, 0), (0, LANES - RW_LORA_V), (0, 0)))

    def src_tile(j):
        return jnp.where(j < _T_V1, j, jnp.where(j == _T_GATE, _T_SRC_GATE, j - _T_SHIFT))

    return pl.pallas_call(
        _pack_body,
        grid=(P_PAD // LANES,),
        in_specs=[pl.BlockSpec((LANES, depth, d), lambda j: (src_tile(j), 0, 0)),
                  pl.BlockSpec((depth, LANES, d), lambda j: (0, 0, 0))],
        out_specs=[pl.BlockSpec((LANES, d), lambda j: (j, 0))] * depth,
        out_shape=[jax.ShapeDtypeStruct((P_PAD, d), BF16)] * depth,
        compiler_params=_cparams(("parallel",)),
        name="pack_w_in",
    )(w_t, v1_t)


def _token_tiles(m):
    if m % 2048 == 0:
        return 2048, 512, 512
    return m, m, m


def kernel(x_prompt, x_sample, state_rw_shift, state_rw_wkv, state_ret, state_ml_c, state_ml_n, state_ml_m,
           ln0_g, ln0_b, w_in, rw_mu, rw_w0, rw_w2, rw_a0, rw_a2, rw_g2, rw_kk, rw_ka, rw_rk,
           rw_lnx_g, rw_lnx_b, rw_v0, rw_v1, rw_vmu, rw_v2, ml_ib, ml_fb, ml_norm, w_out,
           ln1_g, ln1_b, w_gate, w_up, w_down, ln2_g, ln2_b):
    bp, tp, d = x_prompt.shape
    bs, ts, _ = x_sample.shape
    mp, ms = bp * tp, bs * ts
    pos_p = jnp.arange(tp)
    pos_s = PAST_LEN + jnp.arange(ts)
    tm_big, tm_out, tm_down = _token_tiles(mp)
    rw_wkv_t = state_rw_wkv.transpose(0, 2, 3, 4, 1)
    w_in_packed = _pack_w_in(w_in, rw_v1)
    xf_p, xb_p = _layernorm(x_prompt.reshape(mp, d), ln0_g, ln0_b, tm_out)
    xf_s, xb_s = _layernorm(x_sample.reshape(ms, d), ln0_g, ln0_b, ms)
    vf_p = vf_s = None
    outs_p, outs_s = [], []

    for l in range(DEPTH):
        lp = {
            'rw_mu': rw_mu[l], 'rw_w0': rw_w0[l], 'rw_w2': rw_w2[l], 'rw_a0': rw_a0[l], 'rw_a2': rw_a2[l],
            'rw_g2': rw_g2[l], 'rw_kk': rw_kk[l], 'rw_ka': rw_ka[l], 'rw_rk': rw_rk[l],
            'rw_lnx_g': rw_lnx_g[l], 'rw_lnx_b': rw_lnx_b[l], 'ml_ib': ml_ib[l], 'ml_fb': ml_fb[l],
            'ml_norm': ml_norm[l],
        }
        if l > 0:
            lp.update(rw_v0=rw_v0[l - 1], rw_vmu=rw_vmu[l - 1], rw_v2=rw_v2[l - 1])
        w_out_b = w_out[l].astype(BF16)
        x_side = jnp.concatenate([xb_s, state_rw_shift[l].astype(BF16)], axis=0)
        p_p, p_side = _in_proj(xb_p, x_side, w_in_packed[l], tm_big, P_TN)
        o_p, vf_p, st_p = _mix_prompt(p_p.reshape(bp, tp, P_PAD), pos_p, vf_p, lp)
        st = {'rw_wkv_t': rw_wkv_t, 'layer': l, 'ret_all': state_ret, 'ml_c_all': state_ml_c,
              'ml_n': state_ml_n[l], 'ml_m': state_ml_m[l]}
        o_s, vf_s, st_s = _mix_sample(p_side[:ms].reshape(bs, ts, P_PAD), pos_s, vf_s, st, lp, p_side[ms:])
        outs_p.append((xf_p.reshape(bp, tp, d)[:, -1],) + st_p)
        outs_s.append((xf_s.reshape(bs, ts, d)[:, -1],) + st_s)
        flat = lambda o, m: tuple(u.reshape(m, u.shape[-1]) for u in o)
        x1f_p, x1b_p, x1f_s, x1b_s = _out_proj_ln(flat(o_p, mp), xf_p, flat(o_s, ms), xf_s, w_out_b,
                                                  ln1_g[l], ln1_b[l], tm_out)
        hdn_p, hdn_s, w_down_b = _matmul_swiglu(x1b_p, x1b_s, w_gate, w_up, w_down, l, tm_big, 512)
        xf_p, xb_p, xf_s, xb_s = _matmul_res_ln(hdn_p, hdn_s, w_down_b, x1f_p, x1f_s, ln2_g[l], ln2_b[l],
                                                tm_down, DOWN_TK)

    y_p = xf_p.reshape(bp, tp, d)
    y_s = xf_s.reshape(bs, ts, d)
    sp = [jnp.stack([o[i] for o in outs_p]) for i in range(6)]
    ss = [jnp.stack([o[i] for o in outs_s]) for i in range(6)]
    ss[1] = ss[1].transpose(0, 4, 1, 2, 3)
    return (y_p, y_s, sp[0], sp[1], sp[2], sp[3], sp[4], sp[5], ss[0], ss[1], ss[2], ss[3], ss[4], ss[5])
```

```python
import functools
import math

import numpy as np
import jax
import jax.numpy as jnp
from jax import lax
from jax.experimental import pallas as pl
from jax.experimental.pallas import tpu as pltpu

F32 = jnp.float32
BF16 = jnp.bfloat16

D_MODEL = 2048
DEPTH = 2
PAST_LEN = 16384
RW_HD = 64
RW_W = D_MODEL // 4
RW_HEADS = RW_W // RW_HD
RW_LORA_W = 64
RW_LORA_A = 64
RW_LORA_V = 32
RW_LORA_G = 128
RW_P = 3 * RW_W + RW_LORA_W + RW_LORA_A + RW_LORA_G
RW_GN_EPS = 64e-5
RET_V = 128
RET_QK = 64
RET_W = 3 * D_MODEL // 8
RET_HEADS = RET_W // RET_V
RET_P = 2 * RET_HEADS * RET_QK + 2 * RET_W
ML_V = 128
ML_QK = 64
ML_W = D_MODEL - RW_W - RET_W
ML_HEADS = ML_W // ML_V
ML_P = 2 * ML_HEADS * ML_QK + 2 * ML_W + 2 * ML_HEADS
ML_GATE_CAP = 15.0
P_TOTAL = RW_P + RET_P + ML_P
D_FF = ((8 * D_MODEL + 3 * 256 - 1) // (3 * 256)) * 256
CHUNK = 128
ROPE_BASE = 10000.0
LN_EPS = 1e-5
NORM_EPS = 1e-6
ALPHA = (2 * DEPTH) ** 0.25

LANES = 128
P_V1 = RW_P
P_GATE = RW_P + LANES
P_RET = 2304
P_ML = 2 * P_RET
P_MAIN = 2304
P_PAD = 3 * P_RET
P_TN = 768
RW_CHUNK = 64
DEC_TB = 8
VMEM_LIMIT = 56 * 1024 * 1024

def _cparams(sem):
    return pltpu.CompilerParams(dimension_semantics=sem, vmem_limit_bytes=VMEM_LIMIT)


def _dot(a, b):
    return lax.dot_general(a, b, (((1,), (0,)), ((), ())), preferred_element_type=F32)


def _dot_nt(a, b):
    return lax.dot_general(a, b, (((1,), (1,)), ((), ())), preferred_element_type=F32)


def _dot_tn(a, b):
    return lax.dot_general(a, b, (((0,), (0,)), ((), ())), preferred_element_type=F32)


def _ln_rows(x, g, b):
    mu = jnp.mean(x, -1, keepdims=True)
    xc = x - mu
    var = jnp.mean(xc * xc, -1, keepdims=True)
    return xc * lax.rsqrt(var + LN_EPS) * g + b


def _ln_body(x_ref, g_ref, b_ref, of_ref, ob_ref):
    y = _ln_rows(x_ref[...], g_ref[...], b_ref[...])
    of_ref[...] = y
    ob_ref[...] = y.astype(BF16)


def _layernorm(x, g, b, tm):
    m, d = x.shape
    return pl.pallas_call(
        _ln_body,
        grid=(m // tm,),
        in_specs=[pl.BlockSpec((tm, d), lambda i: (i, 0)),
                  pl.BlockSpec((1, d), lambda i: (0, 0)),
                  pl.BlockSpec((1, d), lambda i: (0, 0))],
        out_specs=[pl.BlockSpec((tm, d), lambda i: (i, 0)),
                   pl.BlockSpec((tm, d), lambda i: (i, 0))],
        out_shape=[jax.ShapeDtypeStruct((m, d), F32), jax.ShapeDtypeStruct((m, d), BF16)],
        compiler_params=_cparams(("parallel",)),
        name="layernorm",
    )(x, g.reshape(1, d), b.reshape(1, d))


def _mm_body(x_ref, w_ref, o_ref, *, w_transposed):
    dot = _dot_nt if w_transposed else _dot
    o_ref[...] = dot(x_ref[...], w_ref[...]).astype(o_ref.dtype)


def _matmul(x, w, tm, tn, out_dtype=F32, w_transposed=False):
    m, k = x.shape
    n = w.shape[0] if w_transposed else w.shape[1]
    w_spec = (pl.BlockSpec((tn, k), lambda i, j: (j, 0)) if w_transposed
              else pl.BlockSpec((k, tn), lambda i, j: (0, j)))
    return pl.pallas_call(
        functools.partial(_mm_body, w_transposed=w_transposed),
        grid=(m // tm, n // tn),
        in_specs=[pl.BlockSpec((tm, k), lambda i, j: (i, 0)), w_spec],
        out_specs=pl.BlockSpec((tm, tn), lambda i, j: (i, j)),
        out_shape=jax.ShapeDtypeStruct((m, n), out_dtype),
        compiler_params=_cparams(("parallel", "parallel")),
        name="matmul",
    )(x, w)


def _in_proj_body(x_ref, xs_ref, w_ref, o_ref, os_ref):
    @pl.when(pl.program_id(1) == 0)
    def _():
        os_ref[...] = _dot_nt(xs_ref[...], w_ref[...])

    o_ref[...] = _dot_nt(x_ref[...], w_ref[...])


def _in_proj(x, x_side, w_t, tm, tn):
    m, k = x.shape
    ms = x_side.shape[0]
    n = w_t.shape[0]
    return pl.pallas_call(
        _in_proj_body,
        grid=(n // tn, m // tm),
        in_specs=[pl.BlockSpec((tm, k), lambda j, i: (i, 0)), pl.BlockSpec((ms, k), lambda j, i: (0, 0)),
                  pl.BlockSpec((tn, k), lambda j, i: (j, 0))],
        out_specs=[pl.BlockSpec((tm, tn), lambda j, i: (i, j)), pl.BlockSpec((ms, tn), lambda j, i: (0, j))],
        out_shape=[jax.ShapeDtypeStruct((m, n), F32), jax.ShapeDtypeStruct((ms, n), F32)],
        compiler_params=_cparams(("parallel", "arbitrary")),
        name="in_proj",
    )(x, x_side, w_t)


def _swiglu_body(x_ref, xs_ref, wg_ref, wu_ref, wd_ref, o_ref, os_ref, wdb_ref, wg_scr, wu_scr):
    def act(x):
        g = _dot(x, wg_scr[...])
        return (g * jax.nn.sigmoid(g) * _dot(x, wu_scr[...])).astype(BF16)

    wdb_ref[...] = wd_ref[...].astype(BF16)

    @pl.when(pl.program_id(1) == 0)
    def _():
        wg_scr[...] = wg_ref[...].astype(BF16)
        wu_scr[...] = wu_ref[...].astype(BF16)
        os_ref[...] = act(xs_ref[...])

    step = x_ref.shape[0] // SWIGLU_CHUNKS
    for r in range(SWIGLU_CHUNKS):
        rows = pl.ds(r * step, step)
        o_ref[rows, :] = act(x_ref[rows, :])


def _matmul_swiglu(x, x_side, wg, wu, wd, layer, tm, tn):
    m, k = x.shape
    ms = x_side.shape[0]
    n = wg.shape[2]
    n_i = m // tm
    slab = n // ((n // tn) * n_i)
    assert slab * (n // tn) * n_i == n and slab % 16 == 0, (n, tn, n_i)
    w_spec = pl.BlockSpec((None, k, tn), lambda j, i: (layer, 0, j))
    return pl.pallas_call(
        _swiglu_body,
        grid=(n // tn, n_i),
        in_specs=[pl.BlockSpec((tm, k), lambda j, i: (i, 0)), pl.BlockSpec((ms, k), lambda j, i: (0, 0)),
                  w_spec, w_spec, pl.BlockSpec((None, slab, k), lambda j, i: (layer, j * n_i + i, 0))],
        out_specs=[pl.BlockSpec((tm, tn), lambda j, i: (i, j)), pl.BlockSpec((ms, tn), lambda j, i: (0, j)),
                   pl.BlockSpec((slab, k), lambda j, i: (j * n_i + i, 0))],
        out_shape=[jax.ShapeDtypeStruct((m, n), BF16), jax.ShapeDtypeStruct((ms, n), BF16),
                   jax.ShapeDtypeStruct((n, k), BF16)],
        scratch_shapes=[pltpu.VMEM((k, tn), BF16), pltpu.VMEM((k, tn), BF16)],
        compiler_params=_cparams(("parallel", "arbitrary")),
        name="matmul_swiglu",
    )(x, x_side, wg, wu, wd)


SWIGLU_CHUNKS = 2
DOWN_TK = D_FF // 2
LN_ROWS = 256
OUT_ROWS = 256


def _res_ln_store(acc_ref, res_ref, g_ref, b_ref, of_ref, ob_ref, n_rows):
    step = min(LN_ROWS, n_rows)
    for r in range(0, n_rows, step):
        rows = pl.ds(r, step)
        y = _ln_rows(ALPHA * res_ref[rows, :] + acc_ref[rows, :], g_ref[...], b_ref[...])
        of_ref[rows, :] = y
        ob_ref[rows, :] = y.astype(BF16)


def _mm_res_ln_body(x_ref, xs_ref, w_ref, res_ref, ress_ref, g_ref, b_ref, of_ref, ob_ref, ofs_ref, obs_ref, *, nk, tm, ms):
    i = pl.program_id(0)
    kk = pl.program_id(1)

    @pl.when(kk == 0)
    def _():
        of_ref[...] = jnp.zeros_like(of_ref)

    of_ref[...] += _dot(x_ref[...], w_ref[...])

    @pl.when(kk == nk - 1)
    def _():
        _res_ln_store(of_ref, res_ref, g_ref, b_ref, of_ref, ob_ref, tm)

    @pl.when(i == 0)
    def _():
        @pl.when(kk == 0)
        def _():
            ofs_ref[...] = jnp.zeros_like(ofs_ref)

        ofs_ref[...] += _dot(xs_ref[...], w_ref[...])

        @pl.when(kk == nk - 1)
        def _():
            _res_ln_store(ofs_ref, ress_ref, g_ref, b_ref, ofs_ref, obs_ref, ms)


def _matmul_res_ln(x, x_side, w, res, res_side, g, b, tm, tk):
    m, k = x.shape
    ms = x_side.shape[0]
    n = w.shape[1]
    nk = k // tk
    const = lambda r: pl.BlockSpec((r, n), lambda i, j: (0, 0))
    main = pl.BlockSpec((tm, n), lambda i, j: (i, 0))
    return pl.pallas_call(
        functools.partial(_mm_res_ln_body, nk=nk, tm=tm, ms=ms),
        grid=(m // tm, nk),
        in_specs=[pl.BlockSpec((tm, tk), lambda i, j: (i, j)), pl.BlockSpec((ms, tk), lambda i, j: (0, j)),
                  pl.BlockSpec((tk, n), lambda i, j: (j, 0)), main, const(ms), const(1), const(1)],
        out_specs=[main, main, const(ms), const(ms)],
        out_shape=[jax.ShapeDtypeStruct((m, n), F32), jax.ShapeDtypeStruct((m, n), BF16),
                   jax.ShapeDtypeStruct((ms, n), F32), jax.ShapeDtypeStruct((ms, n), BF16)],
        compiler_params=_cparams(("arbitrary", "arbitrary")),
        name="matmul_res_ln",
    )(x, x_side, w, res, res_side, g.reshape(1, n), b.reshape(1, n))


def _out_proj_ln_body(o_rw_ref, o_ret_ref, o_ml_ref, res_ref, s_rw_ref, s_ret_ref, s_ml_ref, ress_ref,
                      w_ref, g_ref, b_ref, of_ref, ob_ref, ofs_ref, obs_ref, *, tm, ms):
    def project(rw_ref, ret_ref, ml_ref, r_ref, f_ref, h_ref, n_rows):
        step = min(OUT_ROWS, n_rows)
        for r in range(0, n_rows, step):
            rows = pl.ds(r, step)
            mix = (_dot(rw_ref[rows, :], w_ref[0:RW_W, :])
                   + _dot(ret_ref[rows, :], w_ref[RW_W:RW_W + RET_W, :])
                   + _dot(ml_ref[rows, :], w_ref[RW_W + RET_W:, :]))
            y = _ln_rows(ALPHA * r_ref[rows, :] + mix, g_ref[...], b_ref[...])
            f_ref[rows, :] = y
            h_ref[rows, :] = y.astype(BF16)

    @pl.when(pl.program_id(0) == 0)
    def _():
        project(s_rw_ref, s_ret_ref, s_ml_ref, ress_ref, ofs_ref, obs_ref, ms)

    project(o_rw_ref, o_ret_ref, o_ml_ref, res_ref, of_ref, ob_ref, tm)


def _out_proj_ln(o, res, o_side, res_side, w, g, b, tm):
    m = res.shape[0]
    ms = res_side.shape[0]
    n = w.shape[1]
    rows = lambda width: pl.BlockSpec((tm, width), lambda i: (i, 0))
    const = lambda r, c: pl.BlockSpec((r, c), lambda i: (0, 0))
    widths = (RW_W, RET_W, ML_W)
    return pl.pallas_call(
        functools.partial(_out_proj_ln_body, tm=tm, ms=ms),
        grid=(m // tm,),
        in_specs=([rows(c) for c in widths] + [rows(n)] + [const(ms, c) for c in widths] + [const(ms, n)]
                  + [const(D_MODEL, n), const(1, n), const(1, n)]),
        out_specs=[rows(n), rows(n), const(ms, n), const(ms, n)],
        out_shape=[jax.ShapeDtypeStruct((m, n), F32), jax.ShapeDtypeStruct((m, n), BF16),
                   jax.ShapeDtypeStruct((ms, n), F32), jax.ShapeDtypeStruct((ms, n), BF16)],
        compiler_params=_cparams(("arbitrary",)),
        name="out_proj_ln",
    )(*o, res, *o_side, res_side, w, g.reshape(1, n), b.reshape(1, n))


RW_TB = 1024
RW_GH = 4
RW_GW = RW_GH * RW_HD
RW_AHEAD = 16
RW_VEC_ROWS = 8


def _split3(x):
    hi = x.astype(BF16)
    r1 = x - hi.astype(F32)
    mid = r1.astype(BF16)
    lo = (r1 - mid.astype(F32)).astype(BF16)
    return hi, mid, lo


def _mm(a, b, dims):
    return lax.dot_general(a.astype(BF16), b.astype(BF16), (dims, ((), ())), preferred_element_type=F32)


_NN = ((1,), (0,))
_NT = ((1,), (1,))
_TN = ((0,), (0,))


def _exact_lhs_dot(a_bf16, b):
    hi, mid, lo = _split3(b)
    dg = lambda y: lax.dot_general(a_bf16, y, (_NN, ((), ())), preferred_element_type=F32)
    return dg(hi) + (dg(mid) + dg(lo))


def _exact_rhs_dot(a, b_bf16):
    hi, mid, lo = _split3(a)
    dg = lambda x: lax.dot_general(x, b_bf16, (_NN, ((), ())), preferred_element_type=F32)
    return dg(hi) + (dg(mid) + dg(lo))


def _seg_sum(a, seg_bf16):
    hi = a.astype(BF16)
    lo = (a - hi.astype(F32)).astype(BF16)
    gw = seg_bf16.shape[0]
    dg = lambda x: lax.dot_general(x, seg_bf16, (_NN, ((), ())), preferred_element_type=F32)
    groups = [dg(hi[:, c:c + gw]) + dg(lo[:, c:c + gw]) for c in range(0, a.shape[1], gw)]
    return groups[0] if len(groups) == 1 else jnp.concatenate(groups, axis=1)


def _rw_scan(r, lw, k, v, kk, a, st_scr, y_scr):
    L, TB, G = RW_CHUNK, RW_TB, RW_GW
    row = lax.broadcasted_iota(jnp.int32, (L, G), 0)
    col = lax.broadcasted_iota(jnp.int32, (L, G), 1) & (L - 1)
    strict, lower, eye = row > col, row >= col, (row == col).astype(F32)
    rg = lax.broadcasted_iota(jnp.int32, (G, G), 0) // RW_HD
    cg = lax.broadcasted_iota(jnp.int32, (G, G), 1) // RW_HD
    mask_bd = rg == cg
    rt = lax.broadcasted_iota(jnp.int32, (2 * L, 2 * L), 0)
    ct = lax.broadcasted_iota(jnp.int32, (2 * L, 2 * L), 1)
    tri = jnp.logical_and(rt >= ct, rt // L == ct // L).astype(BF16)
    bd = lambda x: jnp.where(mask_bd, jnp.concatenate([x.astype(BF16)] * RW_GH, axis=0), 0.0)
    cut = lambda x, b: x[b[0] * L:(b[0] + 1) * L, b[1] * G:(b[1] + 1) * G]

    cum = jnp.concatenate([_exact_lhs_dot(tri, lw[i:i + 2 * L, :]) for i in range(0, TB, 2 * L)], axis=0)
    e_neg = jnp.exp(-cum)
    ap = kk * a
    ap_h = ap * e_neg
    k_h = k * e_neg
    kk_t = kk * jnp.exp(cum - lw)
    r_t = r * jnp.exp(cum)

    lhs, n_m, m_a, m_kr, inv, mv, vk, a_end, decay = ({} for _ in range(9))
    n_groups = RW_HEADS // RW_GH

    def prep(chunks):
        blk = [(s, g) for s in chunks for g in range(n_groups)]
        for b in blk:
            lhs[b] = jnp.concatenate([cut(kk_t, b), cut(r_t, b)], axis=0)
            sc_a = _mm(lhs[b], bd(cut(ap_h, b)), _NT)
            sc_k = _mm(lhs[b], bd(cut(k_h, b)), _NT)
            n_m[b] = jnp.where(strict, sc_a[:L], 0.0)
            m_a[b] = jnp.where(lower, sc_a[L:], 0.0)
            m_kr[b] = jnp.concatenate([jnp.where(strict, sc_k[:L], 0.0), jnp.where(lower, sc_k[L:], 0.0)], axis=0)
        pw = {}
        for b in blk:
            inv[b] = eye - n_m[b]
            pw[b] = _mm(n_m[b], bd(n_m[b]), _NN)
        n_iter = int(math.log2(L)) - 1
        for j in range(n_iter):
            last = j == n_iter - 1
            for b in blk:
                lhs_j = inv[b] if last else jnp.concatenate([inv[b], pw[b]], axis=0)
                prod = _mm(lhs_j, bd(pw[b]), _NN)
                inv[b] = inv[b] + prod[:L]
                if not last:
                    pw[b] = prod[L:]
        for b in blk:
            s, g = b
            tot = cum[(s + 1) * L - 1:(s + 1) * L, g * G:(g + 1) * G]
            e_end = jnp.exp(tot - cut(cum, b))
            mv[b] = _mm(m_kr[b], bd(cut(v, b)), _NN)
            vk[b] = _mm(cut(v, b), cut(k, b) * e_end, _TN)
            a_end[b] = cut(ap, b) * e_end
            decay[b] = jnp.exp(tot)

    def apply(s):
        for g in range(n_groups):
            b = (s, g)
            st = st_scr[g]
            s_terms = _mm(lhs[b], st, _NT)
            u = _mm(inv[b], bd(s_terms[:L] + mv[b][:L]), _NN)
            y_scr[s * L:(s + 1) * L, g * G:(g + 1) * G] = s_terms[L:] + mv[b][L:] - _mm(m_a[b], bd(u), _NN)
            st_scr[g] = jnp.where(mask_bd, st * decay[b] + vk[b] - _mm(u, a_end[b], _TN), 0.0)

    n_chunks = TB // L
    prep(range(min(RW_AHEAD, n_chunks)))
    for s in range(n_chunks):
        apply(s)
        if s + RW_AHEAD < n_chunks:
            prep([s + RW_AHEAD])


def _softplus(z):
    return jnp.maximum(z, 0.0) + jnp.log(1.0 + jnp.exp(-jnp.abs(z)))


def _rwkv_fused_body(*refs, nc, has_vres):
    if has_vres:
        (p_ref, pv_ref, vf_ref, mu_ref, vec_ref, wa_ref, g2_ref, seg_ref, vmu_ref, v2_ref,
         o_ref, sf_ref, st_scr, prev_scr, y_scr, prevv_scr) = refs
    else:
        (p_ref, mu_ref, vec_ref, wa_ref, g2_ref, seg_ref,
         o_ref, vfo_ref, sf_ref, st_scr, prev_scr, y_scr) = refs
    TB, W, L = RW_TB, RW_W, RW_CHUNK
    c = pl.program_id(1)

    @pl.when(c == 0)
    def _():
        st_scr[...] = jnp.zeros_like(st_scr)
        prev_scr[...] = jnp.zeros_like(prev_scr)
        if has_vres:
            prevv_scr[...] = jnp.zeros_like(prevv_scr)

    first_row = lax.broadcasted_iota(jnp.int32, (TB, 1), 0) == 0

    def shift_mix(x, carry_ref, mu):
        prev = jnp.where(first_row, carry_ref[...], pltpu.roll(x, 1, 0))
        carry_ref[...] = x[TB - 1:TB, :]
        return x + (prev - x) * mu

    mixed = shift_mix(p_ref[0], prev_scr, mu_ref[...])
    r = mixed[:, 0:W]
    k = mixed[:, W:2 * W]
    v = mixed[:, 2 * W:3 * W]
    xwa = mixed[:, 3 * W:3 * W + LANES]
    xg = mixed[:, 3 * W + LANES:3 * W + 2 * LANES]
    vec = vec_ref[...]
    w0, a0, kk_s, ka, rk, lnx_g, lnx_b, v0 = (vec[i:i + 1, :] for i in range(RW_VEC_ROWS))
    seg = seg_ref[...]
    wa = wa_ref[...]
    w_lora = _dot(jnp.tanh(xwa).astype(BF16), wa[:, 0:W])
    a_lora = _dot(xwa.astype(BF16), wa[:, W:2 * W])
    lw = -jnp.exp(-_softplus(-(w0 + w_lora)) - 0.5)
    a = jax.nn.sigmoid(a0 + a_lora)
    g = _dot(jax.nn.sigmoid(xg).astype(BF16), g2_ref[...])
    if has_vres:
        xv = shift_mix(pv_ref[0], prevv_scr, vmu_ref[...])
        v = v + (vf_ref[0] - v) * jax.nn.sigmoid(v0 + _dot(xv.astype(BF16), v2_ref[...]))
    else:
        vfo_ref[0] = v
    kk = k * kk_s
    kk = kk * lax.rsqrt(jnp.maximum(_seg_sum(kk * kk, seg), 1e-24))
    k = k * (1.0 + (a - 1.0) * ka)

    _rw_scan(r, lw, k, v, kk, a, st_scr, y_scr)

    y = y_scr[...]
    inv_n = 1.0 / RW_HD
    y_mu = _seg_sum(y, seg) * inv_n
    yc = y - y_mu
    y_var = _seg_sum(yc * yc, seg) * inv_n
    y = yc * lax.rsqrt(y_var + RW_GN_EPS) * lnx_g + lnx_b
    bonus = _seg_sum(r * k * rk, seg) * v
    o_ref[0] = ((y + bonus) * g).astype(BF16)

    @pl.when(c == nc - 1)
    def _():
        sf_ref[0] = st_scr[...]


def _rwkv_prompt(p3, lp, v_first):
    b, t, _ = p3.shape
    nc = t // RW_TB
    has_vres = v_first is not None
    ng = RW_HEADS // RW_GH
    zpad = jnp.zeros((RW_LORA_W, RW_W), F32)
    wa = jnp.concatenate([jnp.concatenate([lp['rw_w2'], zpad], 0), jnp.concatenate([zpad, lp['rw_a2']], 0)], 1)
    vec = jnp.stack([lp['rw_w0'], lp['rw_a0'], lp['rw_kk'], lp['rw_ka'], lp['rw_rk'], lp['rw_lnx_g'], lp['rw_lnx_b'],
                     lp['rw_v0'] if has_vres else jnp.zeros((RW_W,), F32)])
    hid = jnp.arange(RW_GW) // RW_HD
    seg = (hid[:, None] == hid[None, :]).astype(BF16)
    full = lambda shape: pl.BlockSpec(shape, lambda i, j: (0,) * len(shape))
    seq = lambda w, blk: pl.BlockSpec((1, RW_TB, w), lambda i, j: (i, j, blk))
    in_specs = [seq(RW_P, 0)]
    args = [p3]
    if has_vres:
        in_specs += [seq(LANES, P_V1 // LANES), seq(RW_W, 0)]
        args += [p3, v_first]
    in_specs += [full((1, RW_P)), full((RW_VEC_ROWS, RW_W)), full((LANES, 2 * RW_W)), full((RW_LORA_G, RW_W)),
                 full((RW_GW, RW_GW))]
    args += [lp['rw_mu'].reshape(1, RW_P), vec, wa.astype(BF16), lp['rw_g2'].astype(BF16), seg]
    if has_vres:
        in_specs += [full((1, LANES)), full((LANES, RW_W))]
        args += [jnp.pad(lp['rw_vmu'], (0, LANES - RW_LORA_V)).reshape(1, LANES),
                 jnp.pad(lp['rw_v2'], ((0, LANES - RW_LORA_V), (0, 0))).astype(BF16)]
    out_specs = [seq(RW_W, 0)]
    out_shape = [jax.ShapeDtypeStruct((b, t, RW_W), BF16)]
    if not has_vres:
        out_specs.append(seq(RW_W, 0))
        out_shape.append(jax.ShapeDtypeStruct((b, t, RW_W), F32))
    out_specs.append(pl.BlockSpec((1, ng, RW_GW, RW_GW), lambda i, j: (i, 0, 0, 0)))
    out_shape.append(jax.ShapeDtypeStruct((b, ng, RW_GW, RW_GW), F32))
    scratch = [pltpu.VMEM((ng, RW_GW, RW_GW), F32), pltpu.VMEM((1, RW_P), F32), pltpu.VMEM((RW_TB, RW_W), F32)]
    if has_vres:
        scratch.append(pltpu.VMEM((1, LANES), F32))
    outs = pl.pallas_call(
        functools.partial(_rwkv_fused_body, nc=nc, has_vres=has_vres),
        grid=(b, nc),
        in_specs=in_specs,
        out_specs=out_specs,
        out_shape=out_shape,
        scratch_shapes=scratch,
        compiler_params=_cparams(("parallel", "arbitrary")),
        name="rwkv_fused",
    )(*args)
    if has_vres:
        o, st_bd = outs
    else:
        o, v_first, st_bd = outs
    st5 = st_bd.reshape(b, ng, RW_GH, RW_HD, RW_GH, RW_HD)
    s_fin = jnp.stack([st5[:, :, h, :, h, :] for h in range(RW_GH)], axis=2)
    s_fin = s_fin.reshape(b, RW_HEADS, RW_HD, RW_HD).transpose(0, 1, 3, 2)
    return o, v_first, s_fin


def _ret_log_gamma(h):
    return math.log1p(-(2.0 ** (-5.0 - h)))


def _rotary_tables(pos, heads, dk):
    half = dk // 2
    inv = ROPE_BASE ** (-jnp.arange(half, dtype=F32) / half)
    ang = pos.astype(F32)[:, None] * inv[None, :]
    cos = jnp.tile(jnp.concatenate([jnp.cos(ang), jnp.cos(ang)], -1), (1, heads))
    sin = jnp.tile(jnp.concatenate([-jnp.sin(ang), jnp.sin(ang)], -1), (1, heads))
    lane = jnp.arange(LANES)
    perm = (lane[:, None] == (lane[None, :] ^ half)).astype(BF16)
    return cos, sin, perm


MIX_RB = 4


def _per_sequence(body):
    def batched(*refs, batched_refs, **kw):
        for bb in range(MIX_RB):
            one = [r.at[pl.ds(bb, 1)] if mode == 'keep' else (r.at[bb] if mode == 'drop' else r)
                   for r, mode in zip(refs, batched_refs)]
            body(*one, **kw)
    return batched


def _ret_fused_body(p_ref, cos_ref, sin_ref, perm_ref, o_ref, sf_ref, s_scr, *, nc):
    L, DK, DV, H = CHUNK, RET_QK, RET_V, RET_HEADS
    nq = H * DK
    c = pl.program_id(1)

    @pl.when(c == 0)
    def _():
        s_scr[...] = jnp.zeros_like(s_scr)

    cos = cos_ref[...]
    sin = sin_ref[...]
    perm = perm_ref[...]
    swap = lambda x: jnp.concatenate(
        [_exact_rhs_dot(x[:, c:c + LANES], perm) for c in range(0, nq, LANES)], axis=1)
    rot = lambda x: x * cos + swap(x) * sin
    q_all = rot(p_ref[0, :, 0:nq])
    k_all = rot(p_ref[0, :, nq:2 * nq]) * (DK ** -0.5)
    row = lax.broadcasted_iota(jnp.int32, (L, L), 0)
    col = lax.broadcasted_iota(jnp.int32, (L, L), 1)
    rel = (row - col).astype(F32)
    idx = lax.broadcasted_iota(jnp.int32, (L, 1), 0).astype(F32)
    vs = lambda h: p_ref[0, :, 2 * nq + h * DV:2 * nq + (h + 1) * DV].astype(BF16)
    hd = [dict() for _ in range(H)]
    for h, t in enumerate(hd):
        lg = _ret_log_gamma(h)
        q = q_all[:, h * DK:(h + 1) * DK]
        k = k_all[:, h * DK:(h + 1) * DK]
        t['s_prev'] = s_scr[h]
        t['dmask'] = jnp.where(rel >= 0, jnp.exp(jnp.maximum(rel, 0.0) * lg), 0.0)
        t['qk'] = _dot_nt(q.astype(BF16), k.astype(BF16))
        q_dec = q * jnp.exp((idx + 1.0) * lg)
        t['qs'] = _dot(q_dec.astype(BF16), t['s_prev'].astype(BF16))
        k_end = k * jnp.exp((L - 1.0 - idx) * lg)
        t['kv'] = _dot_tn(k_end.astype(BF16), vs(h))
    for h, t in enumerate(hd):
        y = _dot((t['qk'] * t['dmask']).astype(BF16), vs(h)) + t['qs']
        y = y * lax.rsqrt(jnp.mean(y * y, -1, keepdims=True) + NORM_EPS)
        gate = p_ref[0, :, 2 * nq + RET_W + h * DV:2 * nq + RET_W + (h + 1) * DV]
        o_ref[0, :, h * DV:(h + 1) * DV] = (gate * jax.nn.sigmoid(gate) * y).astype(BF16)
    for h, t in enumerate(hd):
        s_scr[h] = math.exp(L * _ret_log_gamma(h)) * t['s_prev'] + t['kv']

    @pl.when(c == nc - 1)
    def _():
        sf_ref[0] = s_scr[...]


def _ret_prompt(p3, pos):
    b, t, _ = p3.shape
    L = CHUNK
    nc = t // L
    nq = RET_HEADS * RET_QK
    cos, sin, perm = _rotary_tables(pos, RET_HEADS, RET_QK)
    tab = pl.BlockSpec((L, nq), lambda i, j: (j, 0))
    st = pl.BlockSpec((MIX_RB, RET_HEADS, RET_QK, RET_V), lambda i, j: (i, 0, 0, 0))
    modes = ('keep', None, None, None, 'keep', 'keep', 'drop')
    return pl.pallas_call(
        functools.partial(_per_sequence(_ret_fused_body), batched_refs=modes, nc=nc),
        grid=(b // MIX_RB, nc),
        in_specs=[pl.BlockSpec((MIX_RB, L, P_MAIN), lambda i, j: (i, j, P_RET // P_MAIN)), tab, tab,
                  pl.BlockSpec((LANES, LANES), lambda i, j: (0, 0))],
        out_specs=[pl.BlockSpec((MIX_RB, L, RET_W), lambda i, j: (i, j, 0)), st],
        out_shape=[jax.ShapeDtypeStruct((b, t, RET_W), BF16),
                   jax.ShapeDtypeStruct((b, RET_HEADS, RET_QK, RET_V), F32)],
        scratch_shapes=[pltpu.VMEM((MIX_RB, RET_HEADS, RET_QK, RET_V), F32)],
        compiler_params=_cparams(("parallel", "arbitrary")),
        name="ret_fused",
    )(p3, cos, sin, perm)


ML_HPAD = 8
ML_SEL_ROWS = 16


def _ml_fused_body(p_ref, gate_ref, bias_ref, norm_ref, sel_ref, o_ref, cf_ref, nf_ref, mf_ref,
                   c_scr, n_scr, m_scr, *, nc):
    L, DK, DV, H = CHUNK, ML_QK, ML_V, ML_HEADS
    nq = H * DK
    ci = pl.program_id(1)

    @pl.when(ci == 0)
    def _():
        c_scr[...] = jnp.zeros_like(c_scr)
        n_scr[...] = jnp.zeros_like(n_scr)
        m_scr[...] = jnp.zeros_like(m_scr)

    row = lax.broadcasted_iota(jnp.int32, (L, L), 0)
    col = lax.broadcasted_iota(jnp.int32, (L, L), 1)
    causal = row >= col
    tri = causal.astype(BF16)
    capped = ML_GATE_CAP * jnp.tanh((gate_ref[0] + bias_ref[...]) * (1.0 / ML_GATE_CAP))
    lane = lax.broadcasted_iota(jnp.int32, (L, LANES), 1)
    g = jnp.where(lane < H, capped, jnp.where(lane < 2 * H, -_softplus(-capped), 0.0))
    g_rep = _exact_rhs_dot(g[:, :ML_SEL_ROWS], sel_ref[...])
    b_rep_all = _exact_lhs_dot(tri, g_rep[:, H * LANES:])
    g_t = g.T
    cum_t = _exact_lhs_dot(tri, g).T
    cm_all = g_rep[:, :H * LANES] - b_rep_all
    row_id = lax.broadcasted_iota(jnp.int32, (L, 1), 0)
    shift = 1
    while shift < L:
        cm_all = jnp.maximum(cm_all, jnp.where(row_id >= shift, pltpu.roll(cm_all, shift, 0), -jnp.inf))
        shift *= 2
    ones = jnp.ones((L, LANES), BF16)
    mean_w = jnp.full((DV, LANES), 1.0 / DV, BF16)
    m_all = m_scr[...]
    hd = [dict() for _ in range(H)]
    for h, t in enumerate(hd):
        hs = slice(h * LANES, (h + 1) * LANES)
        q = p_ref[0, :, h * DK:(h + 1) * DK].astype(BF16)
        k = p_ref[0, :, nq + h * DK:nq + (h + 1) * DK] * (DK ** -0.5)
        t['v1'] = jnp.concatenate([p_ref[0, :, 2 * nq + h * DV:2 * nq + (h + 1) * DV].astype(BF16), ones], axis=1)
        ig_rep = g_rep[:, hs]
        b_rep = b_rep_all[:, hs]
        ig_row = g_t[h:h + 1, :]
        b_row = cum_t[H + h:H + h + 1, :]
        b_tot = b_rep[L - 1:L, :]
        m_prev = m_all[h:h + 1, :]
        t['c_prev'] = c_scr[h]
        t['n_prev'] = n_scr[h]
        t['m_new'] = jnp.maximum(b_tot + m_prev, jnp.max(b_tot - b_rep + ig_rep, axis=0, keepdims=True))
        t['dec'] = jnp.exp(b_tot + m_prev - t['m_new'])
        kw = k * jnp.exp((b_tot - b_rep + ig_rep - t['m_new'])[:, :DK])
        t['kvn'] = _dot_tn(kw.astype(BF16), t['v1'])
        inter = b_rep + m_prev
        t['m_i'] = b_rep + jnp.maximum(cm_all[:, hs], m_prev)
        t['e'] = jnp.exp(jnp.where(causal, (b_rep - t['m_i']) - b_row + ig_row, -jnp.inf))
        t['sc'] = jnp.exp(inter - t['m_i'])
        t['qk'] = _dot_nt(q, k.astype(BF16))
        cn = jnp.concatenate([t['c_prev'], t['n_prev']], axis=1).astype(BF16)
        t['qcn'] = _dot(q, cn)
    for h, t in enumerate(hd):
        nd = _dot((t['qk'] * t['e']).astype(BF16), t['v1'])
        num = nd[:, :DV] + t['sc'] * t['qcn'][:, :DV]
        den = nd[:, DV:] + t['sc'] * t['qcn'][:, DV:]
        hid = num / jnp.maximum(jnp.abs(den), jnp.exp(-t['m_i']))
        hid = hid * lax.rsqrt(_seg_sum(hid * hid, mean_w) + NORM_EPS)
        og = p_ref[0, :, 2 * nq + ML_W + h * DV:2 * nq + ML_W + (h + 1) * DV]
        o_ref[0, :, h * DV:(h + 1) * DV] = (jax.nn.sigmoid(og) * (hid * norm_ref[:, h * DV:(h + 1) * DV])).astype(BF16)
    for h, t in enumerate(hd):
        c_scr[h] = t['dec'] * t['c_prev'] + t['kvn'][:, :DV]
        n_scr[h] = t['dec'] * t['n_prev'] + t['kvn'][:, DV:]
        m_scr[h:h + 1, :] = t['m_new']

    @pl.when(ci == nc - 1)
    def _():
        cf_ref[0] = c_scr[...]
        nf_ref[0] = n_scr[...]
        mf_ref[0] = m_scr[...]


def _ml_prompt(p3, lp):
    b, t, _ = p3.shape
    L = CHUNK
    nc = t // L
    bias = jnp.pad(jnp.concatenate([lp['ml_ib'], lp['ml_fb']]), (0, LANES - 2 * ML_HEADS)).reshape(1, LANES)
    n_rep = 2 * ML_HEADS * LANES
    sel = (jnp.arange(ML_SEL_ROWS)[:, None] == jnp.arange(n_rep)[None, :] // LANES).astype(BF16)
    vs = pl.BlockSpec((MIX_RB, L, ML_W), lambda i, j: (i, j, 0))
    cs = pl.BlockSpec((MIX_RB, ML_HEADS, ML_QK, ML_V), lambda i, j: (i, 0, 0, 0))
    ns = pl.BlockSpec((MIX_RB, ML_HEADS, ML_QK, LANES), lambda i, j: (i, 0, 0, 0))
    ms = pl.BlockSpec((MIX_RB, ML_HPAD, LANES), lambda i, j: (i, 0, 0))
    modes = ('keep', 'keep', None, None, None, 'keep', 'keep', 'keep', 'keep', 'drop', 'drop', 'drop')
    o, c_f, n_f, m_f = pl.pallas_call(
        functools.partial(_per_sequence(_ml_fused_body), batched_refs=modes, nc=nc),
        grid=(b // MIX_RB, nc),
        in_specs=[pl.BlockSpec((MIX_RB, L, P_MAIN), lambda i, j: (i, j, P_ML // P_MAIN)),
                  pl.BlockSpec((MIX_RB, L, LANES), lambda i, j: (i, j, P_GATE // LANES)),
                  pl.BlockSpec((1, LANES), lambda i, j: (0, 0)),
                  pl.BlockSpec((1, ML_W), lambda i, j: (0, 0)),
                  pl.BlockSpec((ML_SEL_ROWS, n_rep), lambda i, j: (0, 0))],
        out_specs=[vs, cs, ns, ms],
        out_shape=[jax.ShapeDtypeStruct((b, t, ML_W), BF16),
                   jax.ShapeDtypeStruct((b, ML_HEADS, ML_QK, ML_V), F32),
                   jax.ShapeDtypeStruct((b, ML_HEADS, ML_QK, LANES), F32),
                   jax.ShapeDtypeStruct((b, ML_HPAD, LANES), F32)],
        scratch_shapes=[pltpu.VMEM((MIX_RB, ML_HEADS, ML_QK, ML_V), F32),
                        pltpu.VMEM((MIX_RB, ML_HEADS, ML_QK, LANES), F32),
                        pltpu.VMEM((MIX_RB, ML_HPAD, LANES), F32)],
        compiler_params=_cparams(("parallel", "arbitrary")),
        name="ml_fused",
    )(p3, p3, bias, lp['ml_norm'].reshape(1, ML_W), sel)
    return o, c_f, n_f[..., 0], m_f[:, :ML_HEADS, 0]


def _to_cols(x):
    b, c = x.shape
    cols = x.reshape(b // DEC_TB, DEC_TB, c).transpose(0, 2, 1)
    return jnp.pad(cols, ((0, 0), (0, 0), (0, DEC_TB)))


def _col_selector():
    j = jnp.arange(2 * DEC_TB)[:, None]
    return (j == jnp.arange(DEC_TB * LANES)[None, :] // LANES).astype(BF16)


def _rwkv_step_body(w_ref, ap_ref, k_ref, kk_ref, r_ref, v_ref, s_ref, y_ref, so_ref):
    N = RW_HD
    v = v_ref[0]

    def sa_step(i, acc):
        return acc + kk_ref[0, pl.ds(i, 1), :] * s_ref[i]

    sa = lax.fori_loop(0, N, sa_step, jnp.zeros_like(v), unroll=8)

    def upd_step(i, y):
        row = lambda ref: ref[0, pl.ds(i, 1), :]
        s_new = row(w_ref) * s_ref[i] - row(ap_ref) * sa + row(k_ref) * v
        so_ref[i] = s_new
        return y + row(r_ref) * s_new

    y_ref[0] = lax.fori_loop(0, N, upd_step, jnp.zeros_like(v), unroll=8)


def _rwkv_step(wdec, ap, k, kk, r, v, s_all, layer):
    b, w = v.shape
    heads = lambda x: x.reshape(b, RW_HEADS, RW_HD).transpose(1, 2, 0)
    vec = pl.BlockSpec((1, RW_HD, b), lambda h: (h, 0, 0))
    y, s_new = pl.pallas_call(
        _rwkv_step_body,
        grid=(RW_HEADS,),
        in_specs=[vec] * 6 + [pl.BlockSpec((None, None, RW_HD, RW_HD, b), lambda h: (layer, h, 0, 0, 0))],
        out_specs=[vec, pl.BlockSpec((None, RW_HD, RW_HD, b), lambda h: (h, 0, 0, 0))],
        out_shape=[jax.ShapeDtypeStruct((RW_HEADS, RW_HD, b), F32),
                   jax.ShapeDtypeStruct((RW_HEADS, RW_HD, RW_HD, b), F32)],
        compiler_params=_cparams(("parallel",)),
        name="rwkv_step",
    )(heads(wdec), heads(ap), heads(k), heads(kk), heads(r), heads(v), s_all)
    return y.transpose(2, 0, 1).reshape(b, w), s_new


def _ret_step_body(kc_ref, q_ref, k_ref, v_ref, esel_ref, seg_ref, s_ref, y_ref, so_ref):
    DK, DV = RET_QK, RET_V
    q = q_ref[...]
    qk_v = _exact_rhs_dot(q * k_ref[...], seg_ref[...])
    row_id = lax.broadcasted_iota(jnp.int32, (DEC_TB, 1), 0)
    for h in range(RET_HEADS):
        gamma = math.exp(_ret_log_gamma(h))
        ks = slice(h * DK, (h + 1) * DK)
        vs = slice(h * DV, (h + 1) * DV)
        k_rep = _exact_rhs_dot(kc_ref[0, ks, :], esel_ref[...])
        q_h = q[:, ks].astype(BF16)
        v_h = v_ref[:, vs]
        qs = jnp.zeros((DEC_TB, DV), F32)
        for j in range(DEC_TB):
            s = s_ref[j, h]
            so_ref[j, h] = gamma * s + k_rep[:, j * LANES:(j + 1) * LANES] * v_h[j:j + 1, :]
            qs = jnp.where(row_id == j, _dot(q_h, s.astype(BF16)), qs)
        y_ref[:, vs] = qk_v[:, vs] * v_h + gamma * qs


def _ret_step(q, k, v, s_all, layer):
    b = q.shape[0]
    nqk = RET_HEADS * RET_QK
    cols = pl.BlockSpec((1, nqk, 2 * DEC_TB), lambda i: (i, 0, 0))
    qk_rows = pl.BlockSpec((DEC_TB, nqk), lambda i: (i, 0))
    rows = pl.BlockSpec((DEC_TB, RET_W), lambda i: (i, 0))
    esel = pl.BlockSpec((2 * DEC_TB, DEC_TB * LANES), lambda i: (0, 0))
    seg = (jnp.arange(nqk)[:, None] // RET_QK == jnp.arange(RET_W)[None, :] // RET_V).astype(BF16)
    st = pl.BlockSpec((DEC_TB, RET_HEADS, RET_QK, RET_V), lambda i: (i, 0, 0, 0))
    st_in = pl.BlockSpec((None, DEC_TB, RET_HEADS, RET_QK, RET_V), lambda i: (layer, i, 0, 0, 0))
    return pl.pallas_call(
        _ret_step_body,
        grid=(b // DEC_TB,),
        in_specs=[cols, qk_rows, qk_rows, rows, esel, pl.BlockSpec((nqk, RET_W), lambda i: (0, 0)), st_in],
        out_specs=[rows, st],
        out_shape=[jax.ShapeDtypeStruct((b, RET_W), F32), jax.ShapeDtypeStruct(s_all.shape[1:], F32)],
        compiler_params=_cparams(("parallel",)),
        name="ret_step",
    )(_to_cols(k), q, k, v, _col_selector(), seg, s_all)


def _ml_step_body(kc_ref, q_ref, k_ref, v_ref, ig_ref, lf_ref, esel_ref, hsel_ref, hsel_k_ref, seg_ref,
                  c_ref, n_ref, m_ref, h_ref, co_ref, no_ref, mo_ref):
    DK, DV, H = ML_QK, ML_V, ML_HEADS
    ig, lf, m_prev = ig_ref[...], lf_ref[...], m_ref[...]
    m_new = jnp.maximum(lf + m_prev, ig)
    dec = jnp.exp(lf + m_prev - m_new)
    wgt = jnp.exp(ig - m_new)
    mo_ref[...] = m_new
    hsel = hsel_ref[...]
    dec_v = _exact_rhs_dot(dec, hsel)
    wgt_v = _exact_rhs_dot(wgt, hsel)
    floor_v = _exact_rhs_dot(jnp.exp(-m_new), hsel)
    q, k, n_prev = q_ref[...], k_ref[...], n_ref[...]
    no_ref[...] = _exact_rhs_dot(dec, hsel_k_ref[...]) * n_prev + k * _exact_rhs_dot(wgt, hsel_k_ref[...])
    s_v = _exact_rhs_dot(q * k, seg_ref[...]) * wgt_v
    den_v = s_v + dec_v * _exact_rhs_dot(q * n_prev, seg_ref[...])
    row_id = lax.broadcasted_iota(jnp.int32, (DEC_TB, 1), 0)
    for h in range(H):
        ks = slice(h * DK, (h + 1) * DK)
        vs = slice(h * DV, (h + 1) * DV)
        k_rep = _exact_rhs_dot(kc_ref[0, ks, :], esel_ref[...])
        q_h = q[:, ks].astype(BF16)
        v_h = v_ref[:, vs]
        kv_scale = wgt_v[:, vs] * v_h
        qc = jnp.zeros((DEC_TB, DV), F32)
        for j in range(DEC_TB):
            c_prev = c_ref[j, h]
            co_ref[j, h] = dec_v[j:j + 1, vs] * c_prev + k_rep[:, j * LANES:(j + 1) * LANES] * kv_scale[j:j + 1, :]
            qc = jnp.where(row_id == j, _dot(q_h, c_prev.astype(BF16)), qc)
        num = s_v[:, vs] * v_h + dec_v[:, vs] * qc
        h_ref[:, vs] = num / jnp.maximum(jnp.abs(den_v[:, vs]), floor_v[:, vs])


def _ml_step(q, k, v, ig, lf, c_all, layer, n0, m0):
    b = q.shape[0]
    nqk = ML_HEADS * ML_QK
    pad_h = lambda x: jnp.pad(x, ((0, 0), (0, LANES - ML_HEADS)))
    head = jnp.arange(LANES)[:, None]
    hsel = (head == jnp.arange(ML_W)[None, :] // ML_V).astype(BF16)
    hsel_k = (head == jnp.arange(nqk)[None, :] // ML_QK).astype(BF16)
    seg = (jnp.arange(nqk)[:, None] // ML_QK == jnp.arange(ML_W)[None, :] // ML_V).astype(BF16)
    const = lambda r, c: pl.BlockSpec((r, c), lambda i: (0, 0))
    cols = pl.BlockSpec((1, nqk, 2 * DEC_TB), lambda i: (i, 0, 0))
    qk_rows = pl.BlockSpec((DEC_TB, nqk), lambda i: (i, 0))
    rows = pl.BlockSpec((DEC_TB, ML_W), lambda i: (i, 0))
    sc = pl.BlockSpec((DEC_TB, LANES), lambda i: (i, 0))
    cs = pl.BlockSpec((DEC_TB, ML_HEADS, ML_QK, ML_V), lambda i: (i, 0, 0, 0))
    cs_in = pl.BlockSpec((None, DEC_TB, ML_HEADS, ML_QK, ML_V), lambda i: (layer, i, 0, 0, 0))
    hm, c_new, n_new, m_new = pl.pallas_call(
        _ml_step_body,
        grid=(b // DEC_TB,),
        in_specs=[cols, qk_rows, qk_rows, rows, sc, sc, const(2 * DEC_TB, DEC_TB * LANES), const(LANES, ML_W),
                  const(LANES, nqk), const(nqk, ML_W), cs_in, qk_rows, sc],
        out_specs=[rows, cs, qk_rows, sc],
        out_shape=[jax.ShapeDtypeStruct((b, ML_W), F32), jax.ShapeDtypeStruct(c_all.shape[1:], F32),
                   jax.ShapeDtypeStruct((b, nqk), F32), jax.ShapeDtypeStruct((b, LANES), F32)],
        compiler_params=_cparams(("parallel",)),
        name="ml_step",
    )(_to_cols(k), q, k, v, pad_h(ig), pad_h(lf), _col_selector(), hsel, hsel_k, seg,
      c_all, n0.reshape(b, nqk), pad_h(m0))
    return hm, c_new, n_new.reshape(n0.shape), m_new[:, :ML_HEADS]


def _heads(a, h):
    return a.reshape(a.shape[:-1] + (h, a.shape[-1] // h))


def _shift_prev(p, prev_row):
    return jnp.concatenate([prev_row[:, None, :], p[:, :-1]], axis=1)


def _rotary(x, pos):
    half = x.shape[-1] // 2
    inv = ROPE_BASE ** (-jnp.arange(half, dtype=F32) / half)
    ang = pos.astype(F32)[:, None] * inv[None, :]
    cos = jnp.cos(ang)[None, :, None, :]
    sin = jnp.sin(ang)[None, :, None, :]
    x1, x2 = x[..., :half], x[..., half:]
    return jnp.concatenate([x1 * cos - x2 * sin, x1 * sin + x2 * cos], -1)


def _small_matmul(x, w):
    lead = x.shape[:-1]
    kdim, n = w.shape
    x2 = x.reshape(-1, kdim)
    m = x2.shape[0]
    kp = -(-kdim // LANES) * LANES
    npad = -(-n // LANES) * LANES
    x2 = jnp.pad(x2.astype(BF16), ((0, 0), (0, kp - kdim)))
    w2 = jnp.pad(w.astype(BF16), ((0, kp - kdim), (0, npad - n)))
    tm = 1024 if m % 1024 == 0 else m
    out = _matmul(x2, w2, tm, npad)
    return out[:, :n].reshape(lead + (n,))


def _mix_prompt(p, pos, v_first, lp):
    o_rw, v_first, s_new = _rwkv_prompt(p, lp, v_first)
    o_ret, r_new = _ret_prompt(p, pos)
    o_ml, c_new, n_new, m_new = _ml_prompt(p, lp)
    return (o_rw, o_ret, o_ml), v_first, (s_new, r_new, c_new, n_new, m_new)


def _mix_sample(p, pos, v_first, st, lp, prev_row):
    bsz, t_len, _ = p.shape

    p_rw = p[..., :RW_P]
    mixed = p_rw + (_shift_prev(p_rw, prev_row[:, :RW_P]) - p_rw) * lp['rw_mu']
    sizes = np.cumsum([RW_W, RW_W, RW_W, RW_LORA_W, RW_LORA_A, RW_LORA_G])[:-1]
    r, k, v, xw, xa, xg = jnp.split(mixed, [int(s) for s in sizes], axis=-1)
    lora_in = [jnp.tanh(xw), xa, jax.nn.sigmoid(xg)]
    lora_w = [lp['rw_w2'], lp['rw_a2'], lp['rw_g2']]
    if v_first is not None:
        pv = p[..., P_V1:P_V1 + RW_LORA_V]
        lora_in.append(pv + (_shift_prev(pv, prev_row[:, P_V1:P_V1 + RW_LORA_V]) - pv) * lp['rw_vmu'])
        lora_w.append(lp['rw_v2'])
    lora = _small_matmul(jnp.concatenate(lora_in, -1), jax.scipy.linalg.block_diag(*lora_w))
    w = -jax.nn.softplus(-(lp['rw_w0'] + lora[..., :RW_W])) - 0.5
    a = jax.nn.sigmoid(lp['rw_a0'] + lora[..., RW_W:2 * RW_W])
    g = lora[..., 2 * RW_W:3 * RW_W]
    if v_first is None:
        v_first = v
    else:
        v = v + (v_first - v) * jax.nn.sigmoid(lp['rw_v0'] + lora[..., 3 * RW_W:])
    kk = _heads(k * lp['rw_kk'], RW_HEADS)
    kk = kk * lax.rsqrt(jnp.maximum(jnp.sum(jnp.square(kk), -1, keepdims=True), 1e-24))
    kk = kk.reshape(bsz, t_len, RW_W)
    k = k * (1.0 + (a - 1.0) * lp['rw_ka'])
    lw = -jnp.exp(w)
    y, s_new = _rwkv_step(jnp.exp(lw)[:, 0], (kk * a)[:, 0], k[:, 0], kk[:, 0], r[:, 0], v[:, 0],
                          st['rw_wkv_t'], st['layer'])
    y = _heads(y[:, None, :], RW_HEADS)
    y_mu = jnp.mean(y, -1, keepdims=True)
    y_var = jnp.mean(jnp.square(y - y_mu), -1, keepdims=True)
    y = ((y - y_mu) * lax.rsqrt(y_var + RW_GN_EPS)).reshape(bsz, t_len, RW_W)
    y = y * lp['rw_lnx_g'] + lp['rw_lnx_b']
    rh, kh, vh = (_heads(u, RW_HEADS) for u in (r, k, v))
    bonus = jnp.sum(rh * kh * _heads(lp['rw_rk'], RW_HEADS), -1, keepdims=True) * vh
    o_rw = ((y + bonus.reshape(bsz, t_len, RW_W)) * g).astype(BF16)

    nqk = RET_HEADS * RET_QK
    p_ret = p[..., P_RET:P_RET + P_MAIN]
    qr, kr, vr, gr = (p_ret[..., :nqk], p_ret[..., nqk:2 * nqk],
                      p_ret[..., 2 * nqk:2 * nqk + RET_W], p_ret[..., 2 * nqk + RET_W:])
    qh = _rotary(_heads(qr, RET_HEADS), pos).reshape(bsz, t_len, nqk)
    khr = (_rotary(_heads(kr, RET_HEADS), pos) * (RET_QK ** -0.5)).reshape(bsz, t_len, nqk)
    yr, r_new = _ret_step(qh[:, 0], khr[:, 0], vr[:, 0], st['ret_all'], st['layer'])
    yr = _heads(yr[:, None, :], RET_HEADS)
    yr = yr * lax.rsqrt(jnp.mean(jnp.square(yr), -1, keepdims=True) + NORM_EPS)
    o_ret = (jax.nn.silu(gr) * yr.reshape(bsz, t_len, RET_W)).astype(BF16)

    nqk = ML_HEADS * ML_QK
    p_ml = p[..., P_ML:P_ML + P_MAIN]
    qm, km, vm, om = (p_ml[..., :nqk], p_ml[..., nqk:2 * nqk],
                      p_ml[..., 2 * nqk:2 * nqk + ML_W], p_ml[..., 2 * nqk + ML_W:])
    im = p[..., P_GATE:P_GATE + ML_HEADS]
    fm = p[..., P_GATE + ML_HEADS:P_GATE + 2 * ML_HEADS]
    ig = ML_GATE_CAP * jnp.tanh((im + lp['ml_ib']) / ML_GATE_CAP)
    lf = jax.nn.log_sigmoid(ML_GATE_CAP * jnp.tanh((fm + lp['ml_fb']) / ML_GATE_CAP))
    km = km * (ML_QK ** -0.5)
    hm, c_new, n_new, m_new = _ml_step(qm[:, 0], km[:, 0], vm[:, 0], ig[:, 0], lf[:, 0],
                                       st['ml_c_all'], st['layer'], st['ml_n'], st['ml_m'])
    hm = _heads(hm[:, None, :], ML_HEADS)
    hm = hm * lax.rsqrt(jnp.mean(jnp.square(hm), -1, keepdims=True) + NORM_EPS)
    o_ml = (jax.nn.sigmoid(om) * (hm.reshape(bsz, t_len, ML_W) * lp['ml_norm'])).astype(BF16)

    return (o_rw, o_ret, o_ml), v_first, (s_new, r_new, c_new, n_new, m_new)


_T_V1 = P_V1 // LANES
_T_GATE = P_GATE // LANES
_T_RET = P_RET // LANES
_T_SHIFT = (P_RET - RW_P) // LANES
_T_SRC_GATE = (RW_P + RET_P + 2 * ML_HEADS * ML_QK + 2 * ML_W) // LANES


def _pack_body(w_ref, v1_ref, *o_refs):
    j = pl.program_id(0)
    row = lax.broadcasted_iota(jnp.int32, (LANES, 1), 0)
    spare = jnp.logical_and(j > _T_GATE, j < _T_RET)
    for l, o_ref in enumerate(o_refs):
        w = w_ref[:, l, :]
        gates = jnp.where(row < 2 * ML_HEADS, w, 0.0)
        out = jnp.where(j == _T_V1, v1_ref[l], jnp.where(j == _T_GATE, gates, jnp.where(spare, 0.0, w)))
        o_ref[...] = out.astype(BF16)


def _pack_w_in(w_in, rw_v1):
    depth, d, _ = w_in.shape
    w_t = w_in.transpose(2, 0, 1)
    v1_t = jnp.pad(rw_v1.transpose(0, 2, 1), ((1, 0), (0, LANES - RW_LORA_V), (0, 0)))

    def src_tile(j):
        return jnp.where(j < _T_V1, j, jnp.where(j == _T_GATE, _T_SRC_GATE, j - _T_SHIFT))

    return pl.pallas_call(
        _pack_body,
        grid=(P_PAD // LANES,),
        in_specs=[pl.BlockSpec((LANES, depth, d), lambda j: (src_tile(j), 0, 0)),
                  pl.BlockSpec((depth, LANES, d), lambda j: (0, 0, 0))],
        out_specs=[pl.BlockSpec((LANES, d), lambda j: (j, 0))] * depth,
        out_shape=[jax.ShapeDtypeStruct((P_PAD, d), BF16)] * depth,
        compiler_params=_cparams(("parallel",)),
        name="pack_w_in",
    )(w_t, v1_t)


def _token_tiles(m):
    if m % 2048 == 0:
        return 2048, 512, 512
    return m, m, m


def kernel(x_prompt, x_sample, state_rw_shift, state_rw_wkv, state_ret, state_ml_c, state_ml_n, state_ml_m,
           ln0_g, ln0_b, w_in, rw_mu, rw_w0, rw_w2, rw_a0, rw_a2, rw_g2, rw_kk, rw_ka, rw_rk,
           rw_lnx_g, rw_lnx_b, rw_v0, rw_v1, rw_vmu, rw_v2, ml_ib, ml_fb, ml_norm, w_out,
           ln1_g, ln1_b, w_gate, w_up, w_down, ln2_g, ln2_b):
    bp, tp, d = x_prompt.shape
    bs, ts, _ = x_sample.shape
    mp, ms = bp * tp, bs * ts
    pos_p = jnp.arange(tp)
    pos_s = PAST_LEN + jnp.arange(ts)
    tm_big, tm_out, tm_down = _token_tiles(mp)
    rw_wkv_t = state_rw_wkv.transpose(0, 2, 3, 4, 1)
    w_in_packed = _pack_w_in(w_in, rw_v1)
    xf_p, xb_p = _layernorm(x_prompt.reshape(mp, d), ln0_g, ln0_b, tm_out)
    xf_s, xb_s = _layernorm(x_sample.reshape(ms, d), ln0_g, ln0_b, ms)
    vf_p = vf_s = None
    outs_p, outs_s = [], []

    for l in range(DEPTH):
        lp = {
            'rw_mu': rw_mu[l], 'rw_w0': rw_w0[l], 'rw_w2': rw_w2[l], 'rw_a0': rw_a0[l], 'rw_a2': rw_a2[l],
            'rw_g2': rw_g2[l], 'rw_kk': rw_kk[l], 'rw_ka': rw_ka[l], 'rw_rk': rw_rk[l],
            'rw_lnx_g': rw_lnx_g[l], 'rw_lnx_b': rw_lnx_b[l], 'ml_ib': ml_ib[l], 'ml_fb': ml_fb[l],
            'ml_norm': ml_norm[l],
        }
        if l > 0:
            lp.update(rw_v0=rw_v0[l - 1], rw_vmu=rw_vmu[l - 1], rw_v2=rw_v2[l - 1])
        w_out_b = w_out[l].astype(BF16)
        x_side = jnp.concatenate([xb_s, state_rw_shift[l].astype(BF16)], axis=0)
        p_p, p_side = _in_proj(xb_p, x_side, w_in_packed[l], tm_big, P_TN)
        o_p, vf_p, st_p = _mix_prompt(p_p.reshape(bp, tp, P_PAD), pos_p, vf_p, lp)
        st = {'rw_wkv_t': rw_wkv_t, 'layer': l, 'ret_all': state_ret, 'ml_c_all': state_ml_c,
              'ml_n': state_ml_n[l], 'ml_m': state_ml_m[l]}
        o_s, vf_s, st_s = _mix_sample(p_side[:ms].reshape(bs, ts, P_PAD), pos_s, vf_s, st, lp, p_side[ms:])
        outs_p.append((xf_p.reshape(bp, tp, d)[:, -1],) + st_p)
        outs_s.append((xf_s.reshape(bs, ts, d)[:, -1],) + st_s)
        flat = lambda o, m: tuple(u.reshape(m, u.shape[-1]) for u in o)
        x1f_p, x1b_p, x1f_s, x1b_s = _out_proj_ln(flat(o_p, mp), xf_p, flat(o_s, ms), xf_s, w_out_b,
                                                  ln1_g[l], ln1_b[l], tm_out)
        hdn_p, hdn_s, w_down_b = _matmul_swiglu(x1b_p, x1b_s, w_gate, w_up, w_down, l, tm_big, 512)
        xf_p, xb_p, xf_s, xb_s = _matmul_res_ln(hdn_p, hdn_s, w_down_b, x1f_p, x1f_s, ln2_g[l], ln2_b[l],
                                                tm_down, DOWN_TK)

    y_p = xf_p.reshape(bp, tp, d)
    y_s = xf_s.reshape(bs, ts, d)
    sp = [jnp.stack([o[i] for o in outs_p]) for i in range(6)]
    ss = [jnp.stack([o[i] for o in outs_s]) for i in range(6)]
    ss[1] = ss[1].transpose(0, 4, 1, 2, 3)
    return (y_p, y_s, sp[0], sp[1], sp[2], sp[3], sp[4], sp[5], ss[0], ss[1], ss[2], ss[3], ss[4], ss[5])
```
